```python
import math
import jax, jax.numpy as jnp
from jax import lax
import numpy as np

D_MODEL = 1024
BATCH = 8
SEQ = 4096
DEPTH = 2

GRID_W = 64
CTX_LEN = 256

GDN_HEADS = 4
GDN_DK = 128
GDN_DV = 128
GDN_QK = GDN_HEADS * GDN_DK
GDN_V = GDN_HEADS * GDN_DV
GDN_CHUNK = 64
GDN_SHORT_CONV = 5
POOL_WINDOWS = (2, 4, 8, 16)
N_POOL = len(POOL_WINDOWS)
POOL_DIM = D_MODEL // 2
POOL_GROUP = POOL_DIM // N_POOL
E_K = GDN_QK
E_V = 2 * GDN_QK
E_GATE = E_V + GDN_V
E_POOL = E_GATE + GDN_V
E_SCAL = E_POOL + POOL_DIM
EVEN_IN = E_SCAL + 4 * GDN_HEADS
EVEN_MIX = GDN_V + POOL_DIM

SC_DIM = D_MODEL // 2
SC_WIDTH = 3
CF_DIM = D_MODEL // 2
CF_WIDTH = 31
ODD_IN = 3 * SC_DIM + 2 * CF_DIM
ODD_MIX = SC_DIM + CF_DIM

D_FF = ((8 * D_MODEL // 3 + 127) // 128) * 128
FFN_CONV = 3

ALPHA = (2 * DEPTH) ** 0.25
BETA = (8 * DEPTH) ** -0.25
LN_EPS = 1e-5
RMS_EPS = 1e-6

kernel_name = 'hybrid_gdn_pool_conv_diffusion_block'


def _layernorm(x, g, b):
    xf = x.astype(jnp.float32)
    mu = jnp.mean(xf, -1, keepdims=True)
    var = jnp.mean(jnp.square(xf - mu), -1, keepdims=True)
    return ((xf - mu) * lax.rsqrt(var + LN_EPS)).astype(x.dtype) * g + b


def _modulate(h, shift, scale):
    return h * (1 + scale) + shift


def _dwconv1d(x, w):
    width, ch = w.shape
    return lax.conv_general_dilated(x, w[:, None, :], window_strides=(1,), padding=[(width // 2, width // 2)], dimension_numbers=('NWC', 'WIO', 'NWC'), feature_group_count=ch)


def _dwconv2d_grid(x, w):
    b, t, ch = x.shape
    rows = t // GRID_W
    xg = x.reshape(b, rows, GRID_W, ch)
    y = lax.conv_general_dilated(xg, w[:, :, None, :], window_strides=(1, 1), padding=[(FFN_CONV // 2, FFN_CONV // 2)] * 2, dimension_numbers=('NHWC', 'HWIO', 'NHWC'), feature_group_count=ch)
    return y.reshape(b, t, ch)


def _heads(t):
    b, n, hd = t.shape
    return t.reshape(b, n, GDN_HEADS, hd // GDN_HEADS).transpose(0, 2, 1, 3).astype(jnp.float32)


def _l2norm(t):
    return t * lax.rsqrt(jnp.sum(t * t, -1, keepdims=True) + RMS_EPS)


def _gdn_gates(s, a_log, dt_bias):
    b, n, _ = s.shape
    s = s.astype(jnp.float32).reshape(b, n, 4, GDN_HEADS).transpose(2, 0, 3, 1)
    beta = jax.nn.sigmoid(s[:2])
    g = -jnp.exp(a_log.astype(jnp.float32))[:, None, :, None] * jax.nn.softplus(s[2:] + dt_bias.astype(jnp.float32)[:, None, :, None])
    return beta, g


def _chunk_masks():
    idx = jnp.arange(GDN_CHUNK)
    return idx[:, None] >= idx[None, :], idx[:, None] > idx[None, :]


def _gdn_chunk_terms(k, v, g, beta):
    b, h, n, _ = k.shape
    nc = n // GDN_CHUNK
    k = k.reshape(b, h, nc, GDN_CHUNK, GDN_DK)
    v = v.reshape(b, h, nc, GDN_CHUNK, GDN_DV)
    beta = beta.reshape(b, h, nc, GDN_CHUNK, 1)
    gc = jnp.cumsum(g.reshape(b, h, nc, GDN_CHUNK), axis=-1)
    lower, strict = _chunk_masks()
    decay = jnp.exp(jnp.where(lower, gc[..., :, None] - gc[..., None, :], -jnp.inf))
    kb = k * beta
    a_mat = jnp.where(strict, jnp.einsum('bhnck,bhnsk->bhncs', kb, k) * decay, 0.0)
    rhs = jnp.concatenate([v * beta, kb * jnp.exp(gc)[..., None]], axis=-1)
    sol = lax.linalg.triangular_solve(a_mat + jnp.eye(GDN_CHUNK, dtype=a_mat.dtype), rhs, left_side=True, lower=True, unit_diagonal=True)
    u, w = sol[..., :GDN_DV], sol[..., GDN_DV:]
    k_dec = k * jnp.exp(gc[..., -1:] - gc)[..., None]
    g_last = jnp.exp(gc[..., -1])
    return gc, decay, u, w, k_dec, g_last


def _state_step(s, u_i, w_i, kd_i, gl_i):
    v_new = u_i - jnp.einsum('bhck,bhkv->bhcv', w_i, s)
    s_next = s * gl_i[..., None, None] + jnp.einsum('bhck,bhcv->bhkv', kd_i, v_new)
    return s_next, v_new


def _chunks_first(*arrays):
    return tuple(jnp.moveaxis(a, 2, 0) for a in arrays)


def _gdn_final_state(k, v, g, beta, s0):
    _, _, u, w, k_dec, g_last = _gdn_chunk_terms(k, v, g, beta)

    def step(s, xs):
        return _state_step(s, *xs)[0], None

    s, _ = lax.scan(step, s0, _chunks_first(u, w, k_dec, g_last))
    return s


def _gdn_attend(q, k, v, g, beta, s0):
    b, h, n, _ = q.shape
    gc, decay, u, w, k_dec, g_last = _gdn_chunk_terms(k, v, g, beta)
    qc = q.reshape(b, h, n // GDN_CHUNK, GDN_CHUNK, GDN_DK)
    kc = k.reshape(b, h, n // GDN_CHUNK, GDN_CHUNK, GDN_DK)
    lower, _ = _chunk_masks()
    attn = jnp.where(lower, jnp.einsum('bhnck,bhnsk->bhncs', qc, kc) * decay, 0.0)
    q_dec = qc * jnp.exp(gc)[..., None]

    def step(s, xs):
        u_i, w_i, kd_i, gl_i, qd_i, at_i = xs
        s_next, v_new = _state_step(s, u_i, w_i, kd_i, gl_i)
        o_i = jnp.einsum('bhck,bhkv->bhcv', qd_i, s) + jnp.einsum('bhcs,bhsv->bhcv', at_i, v_new)
        return s_next, o_i

    _, o = lax.scan(step, s0, _chunks_first(u, w, k_dec, g_last, q_dec, attn))
    return jnp.moveaxis(o, 0, 2).reshape(b, h, n, GDN_DV)


def _flip(a):
    return jnp.flip(a, axis=2)


def _multiscale_pool(p, pool_w, pool_scale):
    b, t, _ = p.shape
    pg = p.astype(jnp.float32).reshape(b, t, N_POOL, POOL_GROUP)
    cs = jnp.concatenate([jnp.zeros((b, 1, N_POOL, POOL_GROUP), jnp.float32), jnp.cumsum(pg, axis=1)], axis=1)
    pos = jnp.arange(t)
    groups = []
    for gi, win in enumerate(POOL_WINDOWS):
        lo = jnp.clip(pos - win // 2, 0, t)
        hi = jnp.clip(pos - win // 2 + win, 0, t)
        csg = cs[:, :, gi]
        mean = (csg[:, hi] - csg[:, lo]) / (hi - lo).astype(jnp.float32)[None, :, None]
        groups.append(mean - pg[:, :, gi])
    pooled = jnp.stack(groups, axis=2).astype(p.dtype)
    y = jnp.einsum('btgc,gcd->btgd', pooled, pool_w)
    return y.reshape(b, t, POOL_DIM) * pool_scale


def _even_mixer(u, ctx_u, w_in, w_out, conv_w, a_log, dt_bias, norm_w, pool_w, pool_scale):
    b, n, _ = u.shape
    p = u @ w_in
    qkv = jax.nn.silu(_dwconv1d(p[..., :E_GATE], conv_w))
    q = _l2norm(_heads(qkv[..., :E_K])) * GDN_DK ** -0.5
    k = _l2norm(_heads(qkv[..., E_K:E_V]))
    v = _heads(qkv[..., E_V:])
    beta, g = _gdn_gates(p[..., E_SCAL:], a_log, dt_bias)
    pc = ctx_u @ jnp.concatenate([w_in[:, E_K:E_GATE], w_in[:, E_SCAL:]], axis=1)
    kv_c = jax.nn.silu(_dwconv1d(pc[..., :E_GATE - E_K], conv_w[:, E_K:E_GATE]))
    k_c = _l2norm(_heads(kv_c[..., :GDN_QK]))
    v_c = _heads(kv_c[..., GDN_QK:])
    beta_c, g_c = _gdn_gates(pc[..., E_GATE - E_K:], a_log, dt_bias)
    s0 = jnp.zeros((b, GDN_HEADS, GDN_DK, GDN_DV), jnp.float32)
    s_fwd = _gdn_final_state(k_c, v_c, g_c[0], beta_c[0], s0)
    s_bwd = _gdn_final_state(_flip(k_c), _flip(v_c), _flip(g_c[1]), _flip(beta_c[1]), s0)
    o = _gdn_attend(q, k, v, g[0], beta[0], s_fwd) + _flip(_gdn_attend(_flip(q), _flip(k), _flip(v), _flip(g[1]), _flip(beta[1]), s_bwd))
    o = o.transpose(0, 2, 1, 3)
    o = o * lax.rsqrt(jnp.mean(o * o, -1, keepdims=True) + RMS_EPS) * norm_w.astype(jnp.float32)
    o = o.reshape(b, n, GDN_V).astype(u.dtype) * jax.nn.silu(p[..., E_GATE:E_POOL])
    y_pool = _multiscale_pool(p[..., E_POOL:E_SCAL], pool_w, pool_scale)
    return jnp.concatenate([o, y_pool], axis=-1) @ w_out


def _odd_mixer(u, w_in, w_out, sconv_w, conf_conv_w, conf_ln_g, conf_ln_b):
    p = u @ w_in
    g_b, g_c, h, glu_a, glu_b = jnp.split(p, [SC_DIM, 2 * SC_DIM, 3 * SC_DIM, 3 * SC_DIM + CF_DIM], axis=-1)
    y_sc = g_b * _dwconv1d(g_c * h, sconv_w)
    z = _dwconv1d(glu_a * jax.nn.sigmoid(glu_b), conf_conv_w)
    z = jax.nn.silu(_layernorm(z, conf_ln_g, conf_ln_b))
    return jnp.concatenate([y_sc, z], axis=-1) @ w_out


def _conv_ffn(u, w_up, conv_w, w_down):
    a, gate = jnp.split(u @ w_up, 2, axis=-1)
    return (jax.nn.silu(_dwconv2d_grid(a, conv_w)) * gate) @ w_down


def _fwd_setup_inputs(seed: int = 0) -> dict:
    key = jax.random.key(seed)
    ks = jax.random.split(key, 26)
    D = D_MODEL

    def nrm(k, shape, scale=1.0):
        return jax.random.normal(k, shape, jnp.float32) * scale

    dt = jnp.exp(jax.random.uniform(ks[10], (2, GDN_HEADS), jnp.float32, math.log(1e-3), math.log(1e-1)))
    return {
        'x': nrm(ks[0], (BATCH, SEQ, D)),
        'c': nrm(ks[1], (BATCH, D)),
        'ctx': nrm(ks[2], (BATCH, CTX_LEN, D)),
        'c_ctx': nrm(ks[3], (D,)),
        'ada_w': nrm(ks[4], (DEPTH, D, 6 * D), 0.5 * D ** -0.5),
        'ada_b': nrm(ks[5], (DEPTH, 6 * D), 0.02),
        'ln_g': 1.0 + nrm(ks[6], (DEPTH, 2, D), 0.02),
        'ln_b': nrm(ks[7], (DEPTH, 2, D), 0.02),
        'even_w_in': nrm(ks[8], (D, EVEN_IN), D ** -0.5),
        'even_w_out': nrm(ks[9], (EVEN_MIX, D), BETA * EVEN_MIX ** -0.5),
        'gdn_conv_w': nrm(ks[11], (GDN_SHORT_CONV, E_GATE), GDN_SHORT_CONV ** -0.5),
        'gdn_a_log': jnp.log(jax.random.uniform(ks[12], (2, GDN_HEADS), jnp.float32, 1.0, 16.0)),
        'gdn_dt_bias': dt + jnp.log(-jnp.expm1(-dt)),
        'gdn_norm_w': 1.0 + nrm(ks[13], (GDN_DV,), 0.02),
        'pool_w': nrm(ks[14], (N_POOL, POOL_GROUP, POOL_GROUP), POOL_GROUP ** -0.5),
        'pool_scale': 1.0 + nrm(ks[15], (POOL_DIM,), 0.1),
        'odd_w_in': nrm(ks[16], (D, ODD_IN), D ** -0.5),
        'odd_w_out': nrm(ks[17], (ODD_MIX, D), BETA * ODD_MIX ** -0.5),
        'sconv_w': nrm(ks[18], (SC_WIDTH, SC_DIM), SC_WIDTH ** -0.5),
        'conf_conv_w': nrm(ks[19], (CF_WIDTH, CF_DIM), CF_WIDTH ** -0.5),
        'conf_ln_g': 1.0 + nrm(ks[20], (CF_DIM,), 0.02),
        'conf_ln_b': nrm(ks[21], (CF_DIM,), 0.02),
        'ffn_w_up': nrm(ks[22], (DEPTH, D, 2 * D_FF), D ** -0.5),
        'ffn_conv_w': nrm(ks[23], (DEPTH, FFN_CONV, FFN_CONV, D_FF), 1.0 / FFN_CONV),
        'ffn_w_down': nrm(ks[24], (DEPTH, D_FF, D), BETA * D_FF ** -0.5),
    }


def _fwd_reference(x, c, ctx, c_ctx, ada_w, ada_b, ln_g, ln_b, even_w_in, even_w_out, gdn_conv_w, gdn_a_log, gdn_dt_bias, gdn_norm_w, pool_w, pool_scale, odd_w_in, odd_w_out, sconv_w, conf_conv_w, conf_ln_g, conf_ln_b, ffn_w_up, ffn_conv_w, ffn_w_down):
    D = D_MODEL
    silu_c = jax.nn.silu(c)
    silu_cc = jax.nn.silu(c_ctx)
    for layer in range(DEPTH):
        mod = silu_c @ ada_w[layer] + ada_b[layer]
        sh_m, sc_m, gt_m, sh_f, sc_f, gt_f = [m[:, None, :] for m in jnp.split(mod, 6, axis=-1)]
        u = _modulate(x, sh_m, sc_m)
        if layer % 2 == 0:
            mod_c = silu_cc @ ada_w[layer][:, :2 * D] + ada_b[layer][:2 * D]
            ctx_u = _modulate(ctx, mod_c[:D], mod_c[D:])
            y = _even_mixer(u, ctx_u, even_w_in, even_w_out, gdn_conv_w, gdn_a_log, gdn_dt_bias, gdn_norm_w, pool_w, pool_scale)
        else:
            y = _odd_mixer(u, odd_w_in, odd_w_out, sconv_w, conf_conv_w, conf_ln_g, conf_ln_b)
        x = _layernorm(ALPHA * x + gt_m * y, ln_g[layer, 0], ln_b[layer, 0])
        u = _modulate(x, sh_f, sc_f)
        y = _conv_ffn(u, ffn_w_up[layer], ffn_conv_w[layer], ffn_w_down[layer])
        x = _layernorm(ALPHA * x + gt_f * y, ln_g[layer, 1], ln_b[layer, 1])
    return x


import jax as _jax
import jax.numpy as _jnp

TWIN_FORMAT = 'train_step'
FWD_PARAMS = ['x', 'c', 'ctx', 'c_ctx', 'ada_w', 'ada_b', 'ln_g', 'ln_b', 'even_w_in', 'even_w_out', 'gdn_conv_w', 'gdn_a_log', 'gdn_dt_bias', 'gdn_norm_w', 'pool_w', 'pool_scale', 'odd_w_in', 'odd_w_out', 'sconv_w', 'conf_conv_w', 'conf_ln_g', 'conf_ln_b', 'ffn_w_up', 'ffn_conv_w', 'ffn_w_down']
TWIN_WEIGHTS = ['c_ctx', 'ada_w', 'ada_b', 'ln_g', 'ln_b', 'even_w_in', 'even_w_out', 'gdn_conv_w', 'gdn_a_log', 'gdn_dt_bias', 'gdn_norm_w', 'pool_w', 'pool_scale', 'odd_w_in', 'odd_w_out', 'sconv_w', 'conf_conv_w', 'conf_ln_g', 'conf_ln_b', 'ffn_w_up', 'ffn_conv_w', 'ffn_w_down']
TWIN_DIFF_INPUT = 'x'
TWIN_INPUTS = ['x', 'c', 'ctx', 'c_ctx', 'ada_w', 'ada_b', 'ln_g', 'ln_b', 'even_w_in', 'even_w_out', 'gdn_conv_w', 'gdn_a_log', 'gdn_dt_bias', 'gdn_norm_w', 'pool_w', 'pool_scale', 'odd_w_in', 'odd_w_out', 'sconv_w', 'conf_conv_w', 'conf_ln_g', 'conf_ln_b', 'ffn_w_up', 'ffn_conv_w', 'ffn_w_down', 'loss_target', 'm_c_ctx', 'm_ada_w', 'm_ada_b', 'm_ln_g', 'm_ln_b', 'm_even_w_in', 'm_even_w_out', 'm_gdn_conv_w', 'm_gdn_a_log', 'm_gdn_dt_bias', 'm_gdn_norm_w', 'm_pool_w', 'm_pool_scale', 'm_odd_w_in', 'm_odd_w_out', 'm_sconv_w', 'm_conf_conv_w', 'm_conf_ln_g', 'm_conf_ln_b', 'm_ffn_w_up', 'm_ffn_conv_w', 'm_ffn_w_down', 'v_c_ctx', 'v_ada_w', 'v_ada_b', 'v_ln_g', 'v_ln_b', 'v_even_w_in', 'v_even_w_out', 'v_gdn_conv_w', 'v_gdn_a_log', 'v_gdn_dt_bias', 'v_gdn_norm_w', 'v_pool_w', 'v_pool_scale', 'v_odd_w_in', 'v_odd_w_out', 'v_sconv_w', 'v_conf_conv_w', 'v_conf_ln_g', 'v_conf_ln_b', 'v_ffn_w_up', 'v_ffn_conv_w', 'v_ffn_w_down']
TWIN_OUTPUTS = ['loss', 'grad_x', 'grad_c_ctx', 'grad_ada_w', 'grad_ada_b', 'grad_ln_g', 'grad_ln_b', 'grad_even_w_in', 'grad_even_w_out', 'grad_gdn_conv_w', 'grad_gdn_a_log', 'grad_gdn_dt_bias', 'grad_gdn_norm_w', 'grad_pool_w', 'grad_pool_scale', 'grad_odd_w_in', 'grad_odd_w_out', 'grad_sconv_w', 'grad_conf_conv_w', 'grad_conf_ln_g', 'grad_conf_ln_b', 'grad_ffn_w_up', 'grad_ffn_conv_w', 'grad_ffn_w_down', 'delta_c_ctx', 'delta_ada_w', 'delta_ada_b', 'delta_ln_g', 'delta_ln_b', 'delta_even_w_in', 'delta_even_w_out', 'delta_gdn_conv_w', 'delta_gdn_a_log', 'delta_gdn_dt_bias', 'delta_gdn_norm_w', 'delta_pool_w', 'delta_pool_scale', 'delta_odd_w_in', 'delta_odd_w_out', 'delta_sconv_w', 'delta_conf_conv_w', 'delta_conf_ln_g', 'delta_conf_ln_b', 'delta_ffn_w_up', 'delta_ffn_conv_w', 'delta_ffn_w_down', 'new_m_c_ctx', 'new_m_ada_w', 'new_m_ada_b', 'new_m_ln_g', 'new_m_ln_b', 'new_m_even_w_in', 'new_m_even_w_out', 'new_m_gdn_conv_w', 'new_m_gdn_a_log', 'new_m_gdn_dt_bias', 'new_m_gdn_norm_w', 'new_m_pool_w', 'new_m_pool_scale', 'new_m_odd_w_in', 'new_m_odd_w_out', 'new_m_sconv_w', 'new_m_conf_conv_w', 'new_m_conf_ln_g', 'new_m_conf_ln_b', 'new_m_ffn_w_up', 'new_m_ffn_conv_w', 'new_m_ffn_w_down', 'new_v_c_ctx', 'new_v_ada_w', 'new_v_ada_b', 'new_v_ln_g', 'new_v_ln_b', 'new_v_even_w_in', 'new_v_even_w_out', 'new_v_gdn_conv_w', 'new_v_gdn_a_log', 'new_v_gdn_dt_bias', 'new_v_gdn_norm_w', 'new_v_pool_w', 'new_v_pool_scale', 'new_v_odd_w_in', 'new_v_odd_w_out', 'new_v_sconv_w', 'new_v_conf_conv_w', 'new_v_conf_ln_g', 'new_v_conf_ln_b', 'new_v_ffn_w_up', 'new_v_ffn_conv_w', 'new_v_ffn_w_down']
TWIN_LEAF_KINDS = {'loss': 'loss', 'grad_x': 'grad_x', 'grad_c_ctx': 'grad_w', 'grad_ada_w': 'grad_w', 'grad_ada_b': 'grad_w', 'grad_ln_g': 'grad_w', 'grad_ln_b': 'grad_w', 'grad_even_w_in': 'grad_w', 'grad_even_w_out': 'grad_w', 'grad_gdn_conv_w': 'grad_w', 'grad_gdn_a_log': 'grad_w', 'grad_gdn_dt_bias': 'grad_w', 'grad_gdn_norm_w': 'grad_w', 'grad_pool_w': 'grad_w', 'grad_pool_scale': 'grad_w', 'grad_odd_w_in': 'grad_w', 'grad_odd_w_out': 'grad_w', 'grad_sconv_w': 'grad_w', 'grad_conf_conv_w': 'grad_w', 'grad_conf_ln_g': 'grad_w', 'grad_conf_ln_b': 'grad_w', 'grad_ffn_w_up': 'grad_w', 'grad_ffn_conv_w': 'grad_w', 'grad_ffn_w_down': 'grad_w', 'delta_c_ctx': 'delta_w', 'delta_ada_w': 'delta_w', 'delta_ada_b': 'delta_w', 'delta_ln_g': 'delta_w', 'delta_ln_b': 'delta_w', 'delta_even_w_in': 'delta_w', 'delta_even_w_out': 'delta_w', 'delta_gdn_conv_w': 'delta_w', 'delta_gdn_a_log': 'delta_w', 'delta_gdn_dt_bias': 'delta_w', 'delta_gdn_norm_w': 'delta_w', 'delta_pool_w': 'delta_w', 'delta_pool_scale': 'delta_w', 'delta_odd_w_in': 'delta_w', 'delta_odd_w_out': 'delta_w', 'delta_sconv_w': 'delta_w', 'delta_conf_conv_w': 'delta_w', 'delta_conf_ln_g': 'delta_w', 'delta_conf_ln_b': 'delta_w', 'delta_ffn_w_up': 'delta_w', 'delta_ffn_conv_w': 'delta_w', 'delta_ffn_w_down': 'delta_w', 'new_m_c_ctx': 'new_m', 'new_m_ada_w': 'new_m', 'new_m_ada_b': 'new_m', 'new_m_ln_g': 'new_m', 'new_m_ln_b': 'new_m', 'new_m_even_w_in': 'new_m', 'new_m_even_w_out': 'new_m', 'new_m_gdn_conv_w': 'new_m', 'new_m_gdn_a_log': 'new_m', 'new_m_gdn_dt_bias': 'new_m', 'new_m_gdn_norm_w': 'new_m', 'new_m_pool_w': 'new_m', 'new_m_pool_scale': 'new_m', 'new_m_odd_w_in': 'new_m', 'new_m_odd_w_out': 'new_m', 'new_m_sconv_w': 'new_m', 'new_m_conf_conv_w': 'new_m', 'new_m_conf_ln_g': 'new_m', 'new_m_conf_ln_b': 'new_m', 'new_m_ffn_w_up': 'new_m', 'new_m_ffn_conv_w': 'new_m', 'new_m_ffn_w_down': 'new_m', 'new_v_c_ctx': 'new_v', 'new_v_ada_w': 'new_v', 'new_v_ada_b': 'new_v', 'new_v_ln_g': 'new_v', 'new_v_ln_b': 'new_v', 'new_v_even_w_in': 'new_v', 'new_v_even_w_out': 'new_v', 'new_v_gdn_conv_w': 'new_v', 'new_v_gdn_a_log': 'new_v', 'new_v_gdn_dt_bias': 'new_v', 'new_v_gdn_norm_w': 'new_v', 'new_v_pool_w': 'new_v', 'new_v_pool_scale': 'new_v', 'new_v_odd_w_in': 'new_v', 'new_v_odd_w_out': 'new_v', 'new_v_sconv_w': 'new_v', 'new_v_conf_conv_w': 'new_v', 'new_v_conf_ln_g': 'new_v', 'new_v_conf_ln_b': 'new_v', 'new_v_ffn_w_up': 'new_v', 'new_v_ffn_conv_w': 'new_v', 'new_v_ffn_w_down': 'new_v'}


def _forward(args):
    return _fwd_reference(*[args[k] for k in FWD_PARAMS])


def _output_shape():
    out = _jax.eval_shape(lambda: _forward(_fwd_setup_inputs(0)))
    return out.shape, out.dtype

N_MICROBATCH = 1
ADAM_LR = 0.001
ADAM_B1 = 0.9
ADAM_B2 = 0.999
ADAM_EPS = 1e-08
ADAM_WD = 0.01
ADAM_STEP = 10
PER_EXAMPLE_BATCH_AXIS = {'x': 0, 'c': 0, 'ctx': 0, 'loss_target': 0}
SHARED_INPUTS = []
_WEIGHT_DTYPES = {'c_ctx': _jnp.float32, 'ada_w': _jnp.float32, 'ada_b': _jnp.float32, 'ln_g': _jnp.float32, 'ln_b': _jnp.float32, 'even_w_in': _jnp.float32, 'even_w_out': _jnp.float32, 'gdn_conv_w': _jnp.float32, 'gdn_a_log': _jnp.float32, 'gdn_dt_bias': _jnp.float32, 'gdn_norm_w': _jnp.float32, 'pool_w': _jnp.float32, 'pool_scale': _jnp.float32, 'odd_w_in': _jnp.float32, 'odd_w_out': _jnp.float32, 'sconv_w': _jnp.float32, 'conf_conv_w': _jnp.float32, 'conf_ln_g': _jnp.float32, 'conf_ln_b': _jnp.float32, 'ffn_w_up': _jnp.float32, 'ffn_conv_w': _jnp.float32, 'ffn_w_down': _jnp.float32}
MOMENT_SCALE = {'c_ctx': 7.430907e-04, 'ada_w': 2.271141e-02, 'ada_b': 3.842452e-02, 'ln_g': 1.603767e+01, 'ln_b': 6.454058e-01, 'even_w_in': 1.205015e-02, 'even_w_out': 3.130659e-02, 'gdn_conv_w': 9.532502e-03, 'gdn_a_log': 2.154214e-02, 'gdn_dt_bias': 2.132936e-02, 'gdn_norm_w': 3.412352e-02, 'pool_w': 1.818130e-02, 'pool_scale': 2.001304e-02, 'odd_w_in': 2.020217e-02, 'odd_w_out': 3.981045e-02, 'sconv_w': 2.481945e-02, 'conf_conv_w': 1.230817e-02, 'conf_ln_g': 1.452347e-02, 'conf_ln_b': 1.444242e-02, 'ffn_w_up': 9.652336e-03, 'ffn_conv_w': 9.616776e-03, 'ffn_w_down': 3.153783e-02}


def _to_microbatches(a, axis):
    t = _jnp.moveaxis(a, axis, 0)
    t = t.reshape((N_MICROBATCH, t.shape[0] // N_MICROBATCH) + t.shape[1:])
    return _jnp.moveaxis(t, 1, axis + 1)


def setup_inputs(seed: int = 0) -> dict:
    inp = _fwd_setup_inputs(seed)
    key = _jax.random.fold_in(_jax.random.key(seed), 7919)
    shape, _ = _output_shape()
    out = dict(inp)
    out["loss_target"] = _jax.random.normal(_jax.random.fold_in(key, 0), shape, _jnp.float32)
    for i, name in enumerate(TWIN_WEIGHTS):
        w = inp[name].astype(_jnp.float32)
        if MOMENT_SCALE is None:
            s = _jnp.sqrt(_jnp.mean(_jnp.square(w)) + 1e-30)
        else:
            s = MOMENT_SCALE[name]
        km, kv = _jax.random.split(_jax.random.fold_in(key, i + 1))
        out[name] = w
        out["m_" + name] = s * _jax.random.normal(km, w.shape, _jnp.float32)
        out["v_" + name] = (s * s) * _jax.random.uniform(kv, w.shape, _jnp.float32, 0.5, 1.5)
    if N_MICROBATCH > 1:
        for name, axis in PER_EXAMPLE_BATCH_AXIS.items():
            out[name] = _to_microbatches(out[name], axis)
    return {'x': out['x'], 'c': out['c'], 'ctx': out['ctx'], 'c_ctx': out['c_ctx'], 'ada_w': out['ada_w'], 'ada_b': out['ada_b'], 'ln_g': out['ln_g'], 'ln_b': out['ln_b'], 'even_w_in': out['even_w_in'], 'even_w_out': out['even_w_out'], 'gdn_conv_w': out['gdn_conv_w'], 'gdn_a_log': out['gdn_a_log'], 'gdn_dt_bias': out['gdn_dt_bias'], 'gdn_norm_w': out['gdn_norm_w'], 'pool_w': out['pool_w'], 'pool_scale': out['pool_scale'], 'odd_w_in': out['odd_w_in'], 'odd_w_out': out['odd_w_out'], 'sconv_w': out['sconv_w'], 'conf_conv_w': out['conf_conv_w'], 'conf_ln_g': out['conf_ln_g'], 'conf_ln_b': out['conf_ln_b'], 'ffn_w_up': out['ffn_w_up'], 'ffn_conv_w': out['ffn_conv_w'], 'ffn_w_down': out['ffn_w_down'], 'loss_target': out['loss_target'], 'm_c_ctx': out['m_c_ctx'], 'm_ada_w': out['m_ada_w'], 'm_ada_b': out['m_ada_b'], 'm_ln_g': out['m_ln_g'], 'm_ln_b': out['m_ln_b'], 'm_even_w_in': out['m_even_w_in'], 'm_even_w_out': out['m_even_w_out'], 'm_gdn_conv_w': out['m_gdn_conv_w'], 'm_gdn_a_log': out['m_gdn_a_log'], 'm_gdn_dt_bias': out['m_gdn_dt_bias'], 'm_gdn_norm_w': out['m_gdn_norm_w'], 'm_pool_w': out['m_pool_w'], 'm_pool_scale': out['m_pool_scale'], 'm_odd_w_in': out['m_odd_w_in'], 'm_odd_w_out': out['m_odd_w_out'], 'm_sconv_w': out['m_sconv_w'], 'm_conf_conv_w': out['m_conf_conv_w'], 'm_conf_ln_g': out['m_conf_ln_g'], 'm_conf_ln_b': out['m_conf_ln_b'], 'm_ffn_w_up': out['m_ffn_w_up'], 'm_ffn_conv_w': out['m_ffn_conv_w'], 'm_ffn_w_down': out['m_ffn_w_down'], 'v_c_ctx': out['v_c_ctx'], 'v_ada_w': out['v_ada_w'], 'v_ada_b': out['v_ada_b'], 'v_ln_g': out['v_ln_g'], 'v_ln_b': out['v_ln_b'], 'v_even_w_in': out['v_even_w_in'], 'v_even_w_out': out['v_even_w_out'], 'v_gdn_conv_w': out['v_gdn_conv_w'], 'v_gdn_a_log': out['v_gdn_a_log'], 'v_gdn_dt_bias': out['v_gdn_dt_bias'], 'v_gdn_norm_w': out['v_gdn_norm_w'], 'v_pool_w': out['v_pool_w'], 'v_pool_scale': out['v_pool_scale'], 'v_odd_w_in': out['v_odd_w_in'], 'v_odd_w_out': out['v_odd_w_out'], 'v_sconv_w': out['v_sconv_w'], 'v_conf_conv_w': out['v_conf_conv_w'], 'v_conf_ln_g': out['v_conf_ln_g'], 'v_conf_ln_b': out['v_conf_ln_b'], 'v_ffn_w_up': out['v_ffn_w_up'], 'v_ffn_conv_w': out['v_ffn_conv_w'], 'v_ffn_w_down': out['v_ffn_w_down']}


def _loss(weights, diff, rest, loss_target):
    with _jax.named_scope("forward"):
        args = {**rest, TWIN_DIFF_INPUT: diff, **{k: w.astype(_WEIGHT_DTYPES[k]) for k, w in weights.items()}}
        y = _forward(args)
    with _jax.named_scope("loss_head"):
        err = _jnp.square(y.astype(_jnp.float32) - loss_target)
        return 0.5 * _jnp.sum(_jnp.mean(err, axis=-1)) if err.ndim else 0.5 * err


def _adamw(w, g, m, v):
    m = ADAM_B1 * m + (1.0 - ADAM_B1) * g
    v = ADAM_B2 * v + (1.0 - ADAM_B2) * _jnp.square(g)
    m_hat = m / (1.0 - ADAM_B1 ** ADAM_STEP)
    v_hat = v / (1.0 - ADAM_B2 ** ADAM_STEP)
    delta = -ADAM_LR * (m_hat / (_jnp.sqrt(v_hat) + ADAM_EPS) + ADAM_WD * w)
    return delta, m, v


def reference(x, c, ctx, c_ctx, ada_w, ada_b, ln_g, ln_b, even_w_in, even_w_out, gdn_conv_w, gdn_a_log, gdn_dt_bias, gdn_norm_w, pool_w, pool_scale, odd_w_in, odd_w_out, sconv_w, conf_conv_w, conf_ln_g, conf_ln_b, ffn_w_up, ffn_conv_w, ffn_w_down, loss_target, m_c_ctx, m_ada_w, m_ada_b, m_ln_g, m_ln_b, m_even_w_in, m_even_w_out, m_gdn_conv_w, m_gdn_a_log, m_gdn_dt_bias, m_gdn_norm_w, m_pool_w, m_pool_scale, m_odd_w_in, m_odd_w_out, m_sconv_w, m_conf_conv_w, m_conf_ln_g, m_conf_ln_b, m_ffn_w_up, m_ffn_conv_w, m_ffn_w_down, v_c_ctx, v_ada_w, v_ada_b, v_ln_g, v_ln_b, v_even_w_in, v_even_w_out, v_gdn_conv_w, v_gdn_a_log, v_gdn_dt_bias, v_gdn_norm_w, v_pool_w, v_pool_scale, v_odd_w_in, v_odd_w_out, v_sconv_w, v_conf_conv_w, v_conf_ln_g, v_conf_ln_b, v_ffn_w_up, v_ffn_conv_w, v_ffn_w_down):
    given = dict(x=x, c=c, ctx=ctx, c_ctx=c_ctx, ada_w=ada_w, ada_b=ada_b, ln_g=ln_g, ln_b=ln_b, even_w_in=even_w_in, even_w_out=even_w_out, gdn_conv_w=gdn_conv_w, gdn_a_log=gdn_a_log, gdn_dt_bias=gdn_dt_bias, gdn_norm_w=gdn_norm_w, pool_w=pool_w, pool_scale=pool_scale, odd_w_in=odd_w_in, odd_w_out=odd_w_out, sconv_w=sconv_w, conf_conv_w=conf_conv_w, conf_ln_g=conf_ln_g, conf_ln_b=conf_ln_b, ffn_w_up=ffn_w_up, ffn_conv_w=ffn_conv_w, ffn_w_down=ffn_w_down, loss_target=loss_target, m_c_ctx=m_c_ctx, m_ada_w=m_ada_w, m_ada_b=m_ada_b, m_ln_g=m_ln_g, m_ln_b=m_ln_b, m_even_w_in=m_even_w_in, m_even_w_out=m_even_w_out, m_gdn_conv_w=m_gdn_conv_w, m_gdn_a_log=m_gdn_a_log, m_gdn_dt_bias=m_gdn_dt_bias, m_gdn_norm_w=m_gdn_norm_w, m_pool_w=m_pool_w, m_pool_scale=m_pool_scale, m_odd_w_in=m_odd_w_in, m_odd_w_out=m_odd_w_out, m_sconv_w=m_sconv_w, m_conf_conv_w=m_conf_conv_w, m_conf_ln_g=m_conf_ln_g, m_conf_ln_b=m_conf_ln_b, m_ffn_w_up=m_ffn_w_up, m_ffn_conv_w=m_ffn_conv_w, m_ffn_w_down=m_ffn_w_down, v_c_ctx=v_c_ctx, v_ada_w=v_ada_w, v_ada_b=v_ada_b, v_ln_g=v_ln_g, v_ln_b=v_ln_b, v_even_w_in=v_even_w_in, v_even_w_out=v_even_w_out, v_gdn_conv_w=v_gdn_conv_w, v_gdn_a_log=v_gdn_a_log, v_gdn_dt_bias=v_gdn_dt_bias, v_gdn_norm_w=v_gdn_norm_w, v_pool_w=v_pool_w, v_pool_scale=v_pool_scale, v_odd_w_in=v_odd_w_in, v_odd_w_out=v_odd_w_out, v_sconv_w=v_sconv_w, v_conf_conv_w=v_conf_conv_w, v_conf_ln_g=v_conf_ln_g, v_conf_ln_b=v_conf_ln_b, v_ffn_w_up=v_ffn_w_up, v_ffn_conv_w=v_ffn_conv_w, v_ffn_w_down=v_ffn_w_down)
    weights = {n: given[n] for n in TWIN_WEIGHTS}
    shared = {n: given[n] for n in SHARED_INPUTS}
    per_example = {n: given[n] for n in ['x', 'c', 'ctx']}
    grad_fn = _jax.value_and_grad(_loss, argnums=(0, 1))

    def one_microbatch(ex, loss_target):
        ex = dict(ex)
        diff = ex.pop(TWIN_DIFF_INPUT)
        return grad_fn(weights, diff, {**shared, **ex}, loss_target)

    if N_MICROBATCH == 1:
        loss, (grad_w, grad_x) = one_microbatch(per_example, given["loss_target"])
    else:
        def body(carry, xs):
            loss_sum, grad_sum = carry
            l_k, (gw_k, gx_k) = one_microbatch(xs[0], xs[1])
            with _jax.named_scope("update"):
                return (loss_sum + l_k, _jax.tree.map(_jnp.add, grad_sum, gw_k)), gx_k

        init = (_jnp.zeros((), _jnp.float32), _jax.tree.map(_jnp.zeros_like, weights))
        (loss, grad_w), grad_x = _jax.lax.scan(body, init, (per_example, given["loss_target"]))
    with _jax.named_scope("update"):
        delta_w, new_m, new_v = {}, {}, {}
        for n in TWIN_WEIGHTS:
            delta_w[n], new_m[n], new_v[n] = _adamw(weights[n], grad_w[n], given["m_" + n], given["v_" + n])
    return (loss, grad_x, *[grad_w[n] for n in TWIN_WEIGHTS], *[delta_w[n] for n in TWIN_WEIGHTS],
            *[new_m[n] for n in TWIN_WEIGHTS], *[new_v[n] for n in TWIN_WEIGHTS])
```

```python
import functools
import math

import jax
import jax.numpy as jnp
from jax import lax
from jax.experimental import pallas as pl
from jax.experimental.pallas import tpu as pltpu

f32 = jnp.float32
bf16 = jnp.bfloat16
SDS = jax.ShapeDtypeStruct

N_DEV = 8
D_MODEL = 1024
DEPTH = 2
GRID_W = 64
GDN_HEADS = 4
GDN_DK = 128
CHUNK = 64
POOL_WINDOWS = (2, 4, 8, 16)
D_FF = 2816
ALPHA = (2 * DEPTH) ** 0.25
LN_EPS = 1e-5
RMS_EPS = 1e-6
LANE = 128
PAD_ROWS = 72
CONV_ROWS = 256
VMEM_LIMIT = 56 * 2**20

ADAM_LR, ADAM_B1, ADAM_B2, ADAM_EPS, ADAM_WD, ADAM_STEP = 0.001, 0.9, 0.999, 1e-08, 0.01, 10

HI = lax.Precision.HIGHEST


def _cparams(sem=None):
    return pltpu.CompilerParams(dimension_semantics=sem, vmem_limit_bytes=VMEM_LIMIT)


def _silu(x):
    return x * jax.nn.sigmoid(x)


def _dsilu(x):
    s = jax.nn.sigmoid(x)
    return s * (1.0 + x * (1.0 - s))


def _dotb(a, b, dims=(((1,), (0,)), ((), ()))):
    return lax.dot_general(a.astype(bf16), b.astype(bf16), dims, preferred_element_type=f32)


def _dotb_nt(a, b):
    return _dotb(a, b, (((1,), (1,)), ((), ())))


def _dotb_tn(a, b):
    return _dotb(a, b, (((0,), (0,)), ((), ())))


def _dotf(a, b, dims=(((1,), (0,)), ((), ()))):
    return lax.dot_general(a, b, dims, preferred_element_type=f32, precision=HI)


def _pick(n, cands):
    for c in cands:
        if n % c == 0:
            return c
    return n


def matmul(a, b, mode, out_dtype, name):
    if mode == "nn":
        (M, K), N = a.shape, b.shape[1]
    elif mode == "nt":
        (M, K), N = a.shape, b.shape[0]
    else:
        (K, M), N = a.shape, b.shape[1]
    tm = _pick(M, (1024, 768, 512, 256, 128)) if mode != "tn" else _pick(M, (1024, 512, 256, 128))
    tn = _pick(N, (1024, 896, 768, 640, 512, 384, 256, 128))
    tk = _pick(K, (1024, 512, 256, 128)) if mode != "tn" else _pick(K, (1024, 512, 256))
    nk = K // tk
    dims = {"nn": (((1,), (0,)), ((), ())), "nt": (((1,), (1,)), ((), ())), "tn": (((0,), (0,)), ((), ()))}[mode]

    def body(a_ref, b_ref, o_ref, acc_ref):
        k = pl.program_id(2)
        part = lax.dot_general(a_ref[...].astype(bf16), b_ref[...].astype(bf16), dims, preferred_element_type=f32)

        @pl.when(k == 0)
        def _():
            acc_ref[...] = part

        @pl.when(k > 0)
        def _():
            acc_ref[...] += part

        @pl.when(k == nk - 1)
        def _():
            o_ref[...] = acc_ref[...].astype(out_dtype)

    a_spec = {"nn": pl.BlockSpec((tm, tk), lambda i, j, k: (i, k)),
              "nt": pl.BlockSpec((tm, tk), lambda i, j, k: (i, k)),
              "tn": pl.BlockSpec((tk, tm), lambda i, j, k: (k, i))}[mode]
    b_spec = {"nn": pl.BlockSpec((tk, tn), lambda i, j, k: (k, j)),
              "nt": pl.BlockSpec((tn, tk), lambda i, j, k: (j, k)),
              "tn": pl.BlockSpec((tk, tn), lambda i, j, k: (k, j))}[mode]
    return pl.pallas_call(
        body, out_shape=SDS((M, N), out_dtype), grid=(M // tm, N // tn, nk),
        in_specs=[a_spec, b_spec], out_specs=pl.BlockSpec((tm, tn), lambda i, j, k: (i, j)),
        scratch_shapes=[pltpu.VMEM((tm, tn), f32)], name=name,
        compiler_params=_cparams(("parallel", "parallel", "arbitrary")),
    )(a, b)


def _row_tile(t):
    return _pick(t, (512, 256, 128, 64, 32, 16, 8))


def _row_spec(tt, d):
    return pl.BlockSpec((tt, d), lambda i: (i, 0))


def _vec_spec(d):
    return pl.BlockSpec((1, d), lambda i: (0, 0))


def _acc_rows(ref, val):
    @pl.when(pl.program_id(0) == 0)
    def _():
        ref[...] = val

    @pl.when(pl.program_id(0) > 0)
    def _():
        ref[...] += val


def modulate(x, scale, shift, name):
    t, d = x.shape
    tt = _row_tile(t)

    def body(x_ref, sc_ref, sh_ref, o_ref):
        o_ref[...] = (x_ref[...] * (1.0 + sc_ref[...]) + sh_ref[...]).astype(bf16)

    return pl.pallas_call(
        body, out_shape=SDS((t, d), bf16), grid=(t // tt,),
        in_specs=[_row_spec(tt, d), _vec_spec(d), _vec_spec(d)], out_specs=_row_spec(tt, d),
        name=name, compiler_params=_cparams(("parallel",)),
    )(x, scale, shift)


def modulate_bwd(du, x, scale, dres, name, du_row0=0):
    t, d = x.shape
    tt = _row_tile(t)
    blk0 = du_row0 // tt

    def body(du_ref, x_ref, sc_ref, dres_ref, dx_ref, dsc_ref, dsh_ref):
        du_v = du_ref[...]
        dx_ref[...] = du_v * (1.0 + sc_ref[...]) + dres_ref[...]
        _acc_rows(dsc_ref, jnp.sum(du_v * x_ref[...], axis=0, keepdims=True))
        _acc_rows(dsh_ref, jnp.sum(du_v, axis=0, keepdims=True))

    return pl.pallas_call(
        body, out_shape=(SDS((t, d), f32), SDS((1, d), f32), SDS((1, d), f32)), grid=(t // tt,),
        in_specs=[pl.BlockSpec((tt, d), lambda i: (i + blk0, 0)), _row_spec(tt, d), _vec_spec(d), _row_spec(tt, d)],
        out_specs=(_row_spec(tt, d), _vec_spec(d), _vec_spec(d)),
        name=name, compiler_params=_cparams(("arbitrary",)),
    )(du, x, scale, dres)


def _ln_stats(z):
    mu = jnp.mean(z, axis=-1, keepdims=True)
    zc = z - mu
    var = jnp.mean(zc * zc, axis=-1, keepdims=True)
    rstd = lax.rsqrt(var + LN_EPS)
    return zc * rstd, rstd


def _ln_bwd(dxhat, xhat, rstd):
    m1 = jnp.mean(dxhat, axis=-1, keepdims=True)
    m2 = jnp.mean(dxhat * xhat, axis=-1, keepdims=True)
    return rstd * (dxhat - m1 - xhat * m2)


def res_layernorm(x, y, gate, g, b, name):
    t, d = x.shape
    tt = _row_tile(t)

    def body(x_ref, y_ref, gt_ref, g_ref, b_ref, o_ref):
        xhat, _ = _ln_stats(ALPHA * x_ref[...] + gt_ref[...] * y_ref[...])
        o_ref[...] = xhat * g_ref[...] + b_ref[...]

    return pl.pallas_call(
        body, out_shape=SDS((t, d), f32), grid=(t // tt,),
        in_specs=[_row_spec(tt, d), _row_spec(tt, d), _vec_spec(d), _vec_spec(d), _vec_spec(d)],
        out_specs=_row_spec(tt, d), name=name, compiler_params=_cparams(("parallel",)),
    )(x, y, gate, g, b)


def res_layernorm_bwd(dout, x, y, gate, g, name):
    t, d = x.shape
    tt = _row_tile(t)

    def body(do_ref, x_ref, y_ref, gt_ref, g_ref, dxr_ref, dy_ref, dgt_ref, dg_ref, db_ref):
        y_v = y_ref[...]
        do_v = do_ref[...]
        xhat, rstd = _ln_stats(ALPHA * x_ref[...] + gt_ref[...] * y_v)
        dz = _ln_bwd(do_v * g_ref[...], xhat, rstd)
        dxr_ref[...] = ALPHA * dz
        dy_ref[...] = (gt_ref[...] * dz).astype(bf16)
        _acc_rows(dgt_ref, jnp.sum(dz * y_v, axis=0, keepdims=True))
        _acc_rows(dg_ref, jnp.sum(do_v * xhat, axis=0, keepdims=True))
        _acc_rows(db_ref, jnp.sum(do_v, axis=0, keepdims=True))

    vec = SDS((1, d), f32)
    return pl.pallas_call(
        body, out_shape=(SDS((t, d), f32), SDS((t, d), bf16), vec, vec, vec), grid=(t // tt,),
        in_specs=[_row_spec(tt, d), _row_spec(tt, d), _row_spec(tt, d), _vec_spec(d), _vec_spec(d)],
        out_specs=(_row_spec(tt, d), _row_spec(tt, d), _vec_spec(d), _vec_spec(d), _vec_spec(d)),
        name=name, compiler_params=_cparams(("arbitrary",)),
    )(dout, x, y, gate, g)


def loss_head(y, target, name):
    t, d = y.shape
    tt = _row_tile(t)

    def body(y_ref, t_ref, l_ref, dy_ref):
        e = y_ref[...] - t_ref[...]
        dy_ref[...] = e * (1.0 / d)
        part = jnp.sum(jnp.sum(e * e, axis=1, keepdims=True), axis=0, keepdims=True) * (0.5 / d)
        _acc_rows(l_ref, jnp.broadcast_to(part, (1, LANE)))

    return pl.pallas_call(
        body, out_shape=(SDS((1, LANE), f32), SDS((t, d), f32)), grid=(t // tt,),
        in_specs=[_row_spec(tt, d), _row_spec(tt, d)],
        out_specs=(pl.BlockSpec((1, LANE), lambda i: (0, 0)), _row_spec(tt, d)),
        name=name, compiler_params=_cparams(("arbitrary",)),
    )(y, target)


def _fill_pad(pad_ref, val, t):
    zeros = jnp.zeros((PAD_ROWS, LANE), f32)
    pad_ref[0:PAD_ROWS, :] = zeros
    pad_ref[PAD_ROWS + t:2 * PAD_ROWS + t, :] = zeros
    pad_ref[PAD_ROWS:PAD_ROWS + t, :] = val


def _grid_mask(r0, rows, dc):
    col = (lax.broadcasted_iota(jnp.int32, (rows, 1), 0) + r0) % GRID_W
    return ((col + dc >= 0) & (col + dc < GRID_W)).astype(f32)


def _taps_apply(pad_ref, w_ref, taps, r0, rows):
    acc = jnp.zeros((rows, LANE), f32)
    for off, dc, wi in taps:
        xs = pad_ref[PAD_ROWS + r0 + off:PAD_ROWS + r0 + off + rows, :]
        if dc is not None and dc != 0:
            xs = xs * _grid_mask(r0, rows, dc)
        acc = acc + w_ref[wi:wi + 1, :] * xs
    return acc


def _taps_wgrad(pad_ref, dy, taps, r0, rows, nw):
    out = jnp.zeros((nw, LANE), f32)
    rid = lax.broadcasted_iota(jnp.int32, (nw, 1), 0)
    for off, dc, wi in taps:
        xs = pad_ref[PAD_ROWS + r0 + off:PAD_ROWS + r0 + off + rows, :]
        if dc is not None and dc != 0:
            xs = xs * _grid_mask(r0, rows, dc)
        s = jnp.sum(dy * xs, axis=0, keepdims=True)
        out = out + jnp.where(rid == wi, s, 0.0)
    return out


def _transpose_taps(taps):
    return [(-off, None if dc is None else -dc, wi) for off, dc, wi in taps]


def _taps_1d(width):
    return [(j - width // 2, None, j) for j in range(width)]


def _taps_grid3():
    return [(GRID_W * dr + dc, dc, 3 * (dr + 1) + (dc + 1)) for dr in (-1, 0, 1) for dc in (-1, 0, 1)]


def _row_chunks(t):
    r = min(CONV_ROWS, t)
    return [(i * r, r) for i in range(t // r)]


def _col_spec(t, off):
    return pl.BlockSpec((t, LANE), lambda c: (0, c + off))


def _w_spec(nw, off=0):
    return pl.BlockSpec((nw, LANE), lambda c: (0, c + off))


def gdn_conv(p, w, col0, nblk, norm_scale, name):
    t = p.shape[0]
    nw = w.shape[0]
    taps = _taps_1d(5)

    def body(p_ref, w_ref, o_ref, pad_ref):
        _fill_pad(pad_ref, p_ref[...], t)
        for r0, rows in _row_chunks(t):
            a = _silu(_taps_apply(pad_ref, w_ref, taps, r0, rows))
            if norm_scale is not None:
                a = a * (lax.rsqrt(jnp.sum(a * a, axis=-1, keepdims=True) + RMS_EPS) * norm_scale)
            o_ref[r0:r0 + rows, :] = a

    return pl.pallas_call(
        body, out_shape=SDS((t, nblk * LANE), f32), grid=(nblk,),
        in_specs=[_col_spec(t, col0), _w_spec(nw, col0)], out_specs=_col_spec(t, 0),
        scratch_shapes=[pltpu.VMEM((t + 2 * PAD_ROWS, LANE), f32)], name=name,
        compiler_params=_cparams(("parallel",)),
    )(p, w)


def gdn_conv_bwd(p, w, d_a, d_b, col0, nblk, norm_scale, name):
    t = p.shape[0]
    nw = w.shape[0]
    taps = _taps_1d(5)
    ttaps = _transpose_taps(taps)

    def body(p_ref, w_ref, da_ref, db_ref, dp_ref, dw_ref, pad_ref, gpad_ref):
        _fill_pad(pad_ref, p_ref[...], t)
        for r0, rows in _row_chunks(t):
            pre = _taps_apply(pad_ref, w_ref, taps, r0, rows)
            a = _silu(pre)
            dy = da_ref[r0:r0 + rows, :] + db_ref[r0:r0 + rows, :]
            if norm_scale is not None:
                r = lax.rsqrt(jnp.sum(a * a, axis=-1, keepdims=True) + RMS_EPS)
                da = norm_scale * (dy * r - a * (r * r * r) * jnp.sum(dy * a, axis=-1, keepdims=True))
            else:
                da = dy
            gpad_ref[PAD_ROWS + r0:PAD_ROWS + r0 + rows, :] = da * _dsilu(pre)
        zeros = jnp.zeros((PAD_ROWS, LANE), f32)
        gpad_ref[0:PAD_ROWS, :] = zeros
        gpad_ref[PAD_ROWS + t:2 * PAD_ROWS + t, :] = zeros
        dw = jnp.zeros((nw, LANE), f32)
        for r0, rows in _row_chunks(t):
            dp_ref[r0:r0 + rows, :] = _taps_apply(gpad_ref, w_ref, ttaps, r0, rows).astype(bf16)
            dw = dw + _taps_wgrad(pad_ref, gpad_ref[PAD_ROWS + r0:PAD_ROWS + r0 + rows, :], taps, r0, rows, nw)
        dw_ref[...] = dw

    return pl.pallas_call(
        body, out_shape=(SDS((t, nblk * LANE), bf16), SDS((nw, nblk * LANE), f32)), grid=(nblk,),
        in_specs=[_col_spec(t, col0), _w_spec(nw, col0), _col_spec(t, 0), _col_spec(t, 0)],
        out_specs=(_col_spec(t, 0), _w_spec(nw)),
        scratch_shapes=[pltpu.VMEM((t + 2 * PAD_ROWS, LANE), f32)] * 2, name=name,
        compiler_params=_cparams(("parallel",)),
    )(p, w, d_a, d_b)


def short_conv(p, w, name):
    t = p.shape[0]
    nw = w.shape[0]
    taps = _taps_1d(3)

    def body(gb_ref, gc_ref, h_ref, w_ref, o_ref, pad_ref):
        _fill_pad(pad_ref, gc_ref[...] * h_ref[...], t)
        for r0, rows in _row_chunks(t):
            o_ref[r0:r0 + rows, :] = (gb_ref[r0:r0 + rows, :] * _taps_apply(pad_ref, w_ref, taps, r0, rows)).astype(bf16)

    return pl.pallas_call(
        body, out_shape=SDS((t, 4 * LANE), bf16), grid=(4,),
        in_specs=[_col_spec(t, 0), _col_spec(t, 4), _col_spec(t, 8), _w_spec(nw)], out_specs=_col_spec(t, 0),
        scratch_shapes=[pltpu.VMEM((t + 2 * PAD_ROWS, LANE), f32)], name=name,
        compiler_params=_cparams(("parallel",)),
    )(p, p, p, w)


def short_conv_bwd(p, w, dy, name):
    t = p.shape[0]
    nw = w.shape[0]
    taps = _taps_1d(3)
    ttaps = _transpose_taps(taps)

    def body(gb_ref, gc_ref, h_ref, w_ref, dy_ref, dgb_ref, dgc_ref, dh_ref, dw_ref, pad_ref, gpad_ref):
        _fill_pad(pad_ref, gc_ref[...] * h_ref[...], t)
        _fill_pad(gpad_ref, dy_ref[...] * gb_ref[...], t)
        dw = jnp.zeros((nw, LANE), f32)
        for r0, rows in _row_chunks(t):
            sl = slice(r0, r0 + rows)
            dgb_ref[sl, :] = (dy_ref[sl, :] * _taps_apply(pad_ref, w_ref, taps, r0, rows)).astype(bf16)
            dm = _taps_apply(gpad_ref, w_ref, ttaps, r0, rows)
            dgc_ref[sl, :] = (dm * h_ref[sl, :]).astype(bf16)
            dh_ref[sl, :] = (dm * gc_ref[sl, :]).astype(bf16)
            dw = dw + _taps_wgrad(pad_ref, gpad_ref[PAD_ROWS + r0:PAD_ROWS + r0 + rows, :], taps, r0, rows, nw)
        dw_ref[...] = dw

    blk = SDS((t, 4 * LANE), bf16)
    return pl.pallas_call(
        body, out_shape=(blk, blk, blk, SDS((nw, 4 * LANE), f32)), grid=(4,),
        in_specs=[_col_spec(t, 0), _col_spec(t, 4), _col_spec(t, 8), _w_spec(nw), _col_spec(t, 0)],
        out_specs=(_col_spec(t, 0), _col_spec(t, 0), _col_spec(t, 0), _w_spec(nw)),
        scratch_shapes=[pltpu.VMEM((t + 2 * PAD_ROWS, LANE), f32)] * 2, name=name,
        compiler_params=_cparams(("parallel",)),
    )(p, p, p, w, dy)


def conf_conv(p, w, name):
    t = p.shape[0]
    nw = w.shape[0]
    taps = _taps_1d(31)

    def body(a_ref, b_ref, w_ref, o_ref, pad_ref):
        _fill_pad(pad_ref, a_ref[...] * jax.nn.sigmoid(b_ref[...]), t)
        for r0, rows in _row_chunks(t):
            o_ref[r0:r0 + rows, :] = _taps_apply(pad_ref, w_ref, taps, r0, rows)

    return pl.pallas_call(
        body, out_shape=SDS((t, 4 * LANE), f32), grid=(4,),
        in_specs=[_col_spec(t, 12), _col_spec(t, 16), _w_spec(nw)], out_specs=_col_spec(t, 0),
        scratch_shapes=[pltpu.VMEM((t + 2 * PAD_ROWS, LANE), f32)], name=name,
        compiler_params=_cparams(("parallel",)),
    )(p, p, w)


def conf_conv_bwd(p, w, dz, name):
    t = p.shape[0]
    nw = w.shape[0]
    taps = _taps_1d(31)
    ttaps = _transpose_taps(taps)

    def body(a_ref, b_ref, w_ref, dz_ref, da_ref, db_ref, dw_ref, pad_ref, gpad_ref):
        _fill_pad(pad_ref, a_ref[...] * jax.nn.sigmoid(b_ref[...]), t)
        _fill_pad(gpad_ref, dz_ref[...], t)
        dw = jnp.zeros((nw, LANE), f32)
        for r0, rows in _row_chunks(t):
            sl = slice(r0, r0 + rows)
            dm = _taps_apply(gpad_ref, w_ref, ttaps, r0, rows)
            sg = jax.nn.sigmoid(b_ref[sl, :])
            da_ref[sl, :] = (dm * sg).astype(bf16)
            db_ref[sl, :] = (dm * a_ref[sl, :] * sg * (1.0 - sg)).astype(bf16)
            dw = dw + _taps_wgrad(pad_ref, dz_ref[sl, :], taps, r0, rows, nw)
        dw_ref[...] = dw

    blk = SDS((t, 4 * LANE), bf16)
    return pl.pallas_call(
        body, out_shape=(blk, blk, SDS((nw, 4 * LANE), f32)), grid=(4,),
        in_specs=[_col_spec(t, 12), _col_spec(t, 16), _w_spec(nw), _col_spec(t, 0)],
        out_specs=(_col_spec(t, 0), _col_spec(t, 0), _w_spec(nw)),
        scratch_shapes=[pltpu.VMEM((t + 2 * PAD_ROWS, LANE), f32)] * 2, name=name,
        compiler_params=_cparams(("parallel",)),
    )(p, p, w, dz)


def ffn_conv(h, w, name):
    t = h.shape[0]
    nblk = D_FF // LANE
    nw = w.shape[0]
    taps = _taps_grid3()

    def body(a_ref, g_ref, w_ref, o_ref, pad_ref):
        _fill_pad(pad_ref, a_ref[...], t)
        for r0, rows in _row_chunks(t):
            o_ref[r0:r0 + rows, :] = (_silu(_taps_apply(pad_ref, w_ref, taps, r0, rows)) * g_ref[r0:r0 + rows, :]).astype(bf16)

    return pl.pallas_call(
        body, out_shape=SDS((t, D_FF), bf16), grid=(nblk,),
        in_specs=[_col_spec(t, 0), _col_spec(t, nblk), _w_spec(nw)], out_specs=_col_spec(t, 0),
        scratch_shapes=[pltpu.VMEM((t + 2 * PAD_ROWS, LANE), f32)], name=name,
        compiler_params=_cparams(("parallel",)),
    )(h, h, w)


def ffn_conv_bwd(h, w, df, name):
    t = h.shape[0]
    nblk = D_FF // LANE
    nw = w.shape[0]
    taps = _taps_grid3()
    ttaps = _transpose_taps(taps)

    def body(a_ref, g_ref, w_ref, df_ref, dh_ref, dw_ref, pad_ref, gpad_ref, pre_ref):
        half = pl.program_id(1)

        @pl.when(half == 0)
        def _():
            _fill_pad(pad_ref, a_ref[...], t)
            zeros = jnp.zeros((PAD_ROWS, LANE), f32)
            gpad_ref[0:PAD_ROWS, :] = zeros
            gpad_ref[PAD_ROWS + t:2 * PAD_ROWS + t, :] = zeros
            for r0, rows in _row_chunks(t):
                sl = slice(r0, r0 + rows)
                pre = _taps_apply(pad_ref, w_ref, taps, r0, rows)
                pre_ref[sl, :] = pre
                gpad_ref[PAD_ROWS + r0:PAD_ROWS + r0 + rows, :] = df_ref[sl, :] * g_ref[sl, :] * _dsilu(pre)
            dw = jnp.zeros((nw, LANE), f32)
            for r0, rows in _row_chunks(t):
                dh_ref[r0:r0 + rows, :] = _taps_apply(gpad_ref, w_ref, ttaps, r0, rows).astype(bf16)
                dw = dw + _taps_wgrad(pad_ref, gpad_ref[PAD_ROWS + r0:PAD_ROWS + r0 + rows, :], taps, r0, rows, nw)
            dw_ref[...] = dw

        @pl.when(half == 1)
        def _():
            for r0, rows in _row_chunks(t):
                sl = slice(r0, r0 + rows)
                dh_ref[sl, :] = (df_ref[sl, :] * _silu(pre_ref[sl, :])).astype(bf16)

    cspec = lambda off: pl.BlockSpec((t, LANE), lambda c, s: (0, c + off))
    return pl.pallas_call(
        body, out_shape=(SDS((t, 2 * D_FF), bf16), SDS((nw, D_FF), f32)), grid=(nblk, 2),
        in_specs=[cspec(0), cspec(nblk), pl.BlockSpec((nw, LANE), lambda c, s: (0, c)), cspec(0)],
        out_specs=(pl.BlockSpec((t, LANE), lambda c, s: (0, c + nblk * s)), pl.BlockSpec((nw, LANE), lambda c, s: (0, c))),
        scratch_shapes=[pltpu.VMEM((t + 2 * PAD_ROWS, LANE), f32)] * 2 + [pltpu.VMEM((t, LANE), f32)], name=name,
        compiler_params=_cparams(("parallel", "arbitrary")),
    )(h, h, w, df)


def _pool_count(r0, rows, win, t):
    pos = lax.broadcasted_iota(jnp.int32, (rows, 1), 0) + r0
    lo = jnp.clip(pos - win // 2, 0, t)
    hi = jnp.clip(pos - win // 2 + win, 0, t)
    return (hi - lo).astype(f32)


def _window_sum(pad_ref, r0, rows, lo, hi):
    acc = jnp.zeros((rows, LANE), f32)
    for off in range(lo, hi):
        acc = acc + pad_ref[PAD_ROWS + r0 + off:PAD_ROWS + r0 + off + rows, :]
    return acc


def pool_mix(p, pool_w, pool_scale, name):
    t = p.shape[0]

    def body(x_ref, w_ref, s_ref, o_ref, pad_ref):
        for gi, win in enumerate(POOL_WINDOWS):
            cs = slice(gi * LANE, (gi + 1) * LANE)
            _fill_pad(pad_ref, x_ref[:, cs], t)
            wg = w_ref[gi].astype(bf16)
            for r0, rows in _row_chunks(t):
                pooled = _window_sum(pad_ref, r0, rows, -(win // 2), win - win // 2) / _pool_count(r0, rows, win, t) - x_ref[r0:r0 + rows, cs]
                o_ref[r0:r0 + rows, cs] = (_dotb(pooled, wg) * s_ref[:, cs]).astype(bf16)

    return pl.pallas_call(
        body, out_shape=SDS((t, 512), bf16), grid=(1,),
        in_specs=[pl.BlockSpec((t, 512), lambda i: (0, 4)), pl.BlockSpec((4, LANE, LANE), lambda i: (0, 0, 0)),
                  pl.BlockSpec((1, 512), lambda i: (0, 0))],
        out_specs=pl.BlockSpec((t, 512), lambda i: (0, 0)),
        scratch_shapes=[pltpu.VMEM((t + 2 * PAD_ROWS, LANE), f32)], name=name,
        compiler_params=_cparams(("arbitrary",)),
    )(p, pool_w, pool_scale)


def pool_mix_bwd(p, pool_w, pool_scale, dmix, name):
    t = p.shape[0]

    def body(x_ref, w_ref, s_ref, dy_ref, dp_ref, dw_ref, ds_ref, pad_ref, gpad_ref, dpool_ref):
        for gi, win in enumerate(POOL_WINDOWS):
            cs = slice(gi * LANE, (gi + 1) * LANE)
            h = win // 2
            _fill_pad(pad_ref, x_ref[:, cs], t)
            wg = w_ref[gi].astype(bf16)
            dw = jnp.zeros((LANE, LANE), f32)
            ds = jnp.zeros((1, LANE), f32)
            zeros = jnp.zeros((PAD_ROWS, LANE), f32)
            gpad_ref[0:PAD_ROWS, :] = zeros
            gpad_ref[PAD_ROWS + t:2 * PAD_ROWS + t, :] = zeros
            for r0, rows in _row_chunks(t):
                cnt = _pool_count(r0, rows, win, t)
                pooled = _window_sum(pad_ref, r0, rows, -h, win - h) / cnt - x_ref[r0:r0 + rows, cs]
                dy = dy_ref[r0:r0 + rows, cs]
                ds = ds + jnp.sum(dy * _dotb(pooled, wg), axis=0, keepdims=True)
                dypre = dy * s_ref[:, cs]
                dw = dw + _dotb_tn(pooled, dypre)
                dpooled = _dotb_nt(dypre, wg)
                gpad_ref[PAD_ROWS + r0:PAD_ROWS + r0 + rows, :] = dpooled / cnt
                dpool_ref[r0:r0 + rows, :] = dpooled
            dw_ref[gi] = dw
            ds_ref[:, cs] = ds
            for r0, rows in _row_chunks(t):
                dx = _window_sum(gpad_ref, r0, rows, -h + 1, h + 1) - dpool_ref[r0:r0 + rows, :]
                dp_ref[r0:r0 + rows, cs] = dx.astype(bf16)

    return pl.pallas_call(
        body, out_shape=(SDS((t, 512), bf16), SDS((4, LANE, LANE), f32), SDS((1, 512), f32)), grid=(1,),
        in_specs=[pl.BlockSpec((t, 512), lambda i: (0, 4)), pl.BlockSpec((4, LANE, LANE), lambda i: (0, 0, 0)),
                  pl.BlockSpec((1, 512), lambda i: (0, 0)), pl.BlockSpec((t, 512), lambda i: (0, 1))],
        out_specs=(pl.BlockSpec((t, 512), lambda i: (0, 0)), pl.BlockSpec((4, LANE, LANE), lambda i: (0, 0, 0)),
                   pl.BlockSpec((1, 512), lambda i: (0, 0))),
        scratch_shapes=[pltpu.VMEM((t + 2 * PAD_ROWS, LANE), f32)] * 2 + [pltpu.VMEM((t, LANE), f32)], name=name,
        compiler_params=_cparams(("arbitrary",)),
    )(p, pool_w, pool_scale, dmix)


def gated_rmsnorm(o_a, o_b, p, norm_w, name):
    t = o_a.shape[0]
    tt = _row_tile(t)

    def body(oa_ref, ob_ref, g_ref, nw_ref, y_ref):
        for h in range(GDN_HEADS):
            cs = slice(h * LANE, (h + 1) * LANE)
            o = oa_ref[:, cs] + ob_ref[:, cs]
            r = lax.rsqrt(jnp.mean(o * o, axis=-1, keepdims=True) + RMS_EPS)
            y_ref[:, cs] = (o * r * nw_ref[...] * _silu(g_ref[:, cs])).astype(bf16)

    return pl.pallas_call(
        body, out_shape=SDS((t, 512), bf16), grid=(t // tt,),
        in_specs=[_row_spec(tt, 512), _row_spec(tt, 512), pl.BlockSpec((tt, 512), lambda i: (i, 3)), _vec_spec(LANE)],
        out_specs=_row_spec(tt, 512), name=name, compiler_params=_cparams(("parallel",)),
    )(o_a, o_b, p, norm_w)


def gated_rmsnorm_bwd(o_a, o_b, p, norm_w, dmix, name):
    t = o_a.shape[0]
    tt = _row_tile(t)

    def body(oa_ref, ob_ref, g_ref, nw_ref, dy_ref, do_ref, dg_ref, dnw_ref):
        dnw = jnp.zeros((1, LANE), f32)
        for h in range(GDN_HEADS):
            cs = slice(h * LANE, (h + 1) * LANE)
            o = oa_ref[:, cs] + ob_ref[:, cs]
            r = lax.rsqrt(jnp.mean(o * o, axis=-1, keepdims=True) + RMS_EPS)
            gate = g_ref[:, cs]
            dy = dy_ref[:, cs]
            dy1 = dy * _silu(gate)
            dg_ref[:, cs] = (dy * (o * r * nw_ref[...]) * _dsilu(gate)).astype(bf16)
            dnw = dnw + jnp.sum(dy1 * o * r, axis=0, keepdims=True)
            dn = dy1 * nw_ref[...]
            do_ref[:, cs] = r * dn - o * (r * r * r) * jnp.mean(dn * o, axis=-1, keepdims=True)
        _acc_rows(dnw_ref, dnw)

    return pl.pallas_call(
        body, out_shape=(SDS((t, 512), f32), SDS((t, 512), bf16), SDS((1, LANE), f32)), grid=(t // tt,),
        in_specs=[_row_spec(tt, 512), _row_spec(tt, 512), pl.BlockSpec((tt, 512), lambda i: (i, 3)), _vec_spec(LANE),
                  _row_spec(tt, 512)],
        out_specs=(_row_spec(tt, 512), _row_spec(tt, 512), _vec_spec(LANE)),
        name=name, compiler_params=_cparams(("arbitrary",)),
    )(o_a, o_b, p, norm_w, dmix)


def ln_silu(z, g, b, name):
    t, d = z.shape
    tt = _row_tile(t)

    def body(z_ref, g_ref, b_ref, o_ref):
        xhat, _ = _ln_stats(z_ref[...])
        o_ref[...] = _silu(xhat * g_ref[...] + b_ref[...]).astype(bf16)

    return pl.pallas_call(
        body, out_shape=SDS((t, d), bf16), grid=(t // tt,),
        in_specs=[_row_spec(tt, d), _vec_spec(d), _vec_spec(d)], out_specs=_row_spec(tt, d),
        name=name, compiler_params=_cparams(("parallel",)),
    )(z, g, b)


def ln_silu_bwd(z, g, b, dmix, name):
    t, d = z.shape
    tt = _row_tile(t)

    def body(z_ref, g_ref, b_ref, dy_ref, dz_ref, dg_ref, db_ref):
        xhat, rstd = _ln_stats(z_ref[...])
        dn = dy_ref[...] * _dsilu(xhat * g_ref[...] + b_ref[...])
        dz_ref[...] = _ln_bwd(dn * g_ref[...], xhat, rstd)
        _acc_rows(dg_ref, jnp.sum(dn * xhat, axis=0, keepdims=True))
        _acc_rows(db_ref, jnp.sum(dn, axis=0, keepdims=True))

    return pl.pallas_call(
        body, out_shape=(SDS((t, d), f32), SDS((1, d), f32), SDS((1, d), f32)), grid=(t // tt,),
        in_specs=[_row_spec(tt, d), _vec_spec(d), _vec_spec(d), pl.BlockSpec((tt, d), lambda i: (i, 1))],
        out_specs=(_row_spec(tt, d), _vec_spec(d), _vec_spec(d)),
        name=name, compiler_params=_cparams(("arbitrary",)),
    )(z, g, b, dmix)


def gdn_gates(p, neg_a, dt_bias, name):
    t = p.shape[0]
    tt = _row_tile(t)

    def body(s_ref, na_ref, dt_ref, o_ref):
        s = s_ref[...]
        col = lax.broadcasted_iota(jnp.int32, s.shape, 1)
        o_ref[...] = jnp.where(col < 8, jax.nn.sigmoid(s), na_ref[...] * jax.nn.softplus(s + dt_ref[...]))

    return pl.pallas_call(
        body, out_shape=SDS((t, LANE), f32), grid=(t // tt,),
        in_specs=[pl.BlockSpec((tt, LANE), lambda i: (i, 20)), _vec_spec(LANE), _vec_spec(LANE)],
        out_specs=_row_spec(tt, LANE), name=name, compiler_params=_cparams(("parallel",)),
    )(p, neg_a, dt_bias)


def gdn_gates_bwd(p, neg_a, dt_bias, dbg, name):
    t = p.shape[0]
    tt = _row_tile(t)

    def body(s_ref, na_ref, dt_ref, d_ref, ds_ref, da_ref, ddt_ref):
        s = s_ref[...]
        d = d_ref[...]
        col = lax.broadcasted_iota(jnp.int32, s.shape, 1)
        sg = jax.nn.sigmoid(s)
        z = s + dt_ref[...]
        dz = jnp.where((col >= 8) & (col < 16), d * na_ref[...] * jax.nn.sigmoid(z), 0.0)
        ds_ref[...] = jnp.where(col < 8, d * sg * (1.0 - sg), dz).astype(bf16)
        dalog = jnp.where((col >= 8) & (col < 16), d * na_ref[...] * jax.nn.softplus(z), 0.0)
        _acc_rows(da_ref, jnp.sum(dalog, axis=0, keepdims=True))
        _acc_rows(ddt_ref, jnp.sum(dz, axis=0, keepdims=True))

    return pl.pallas_call(
        body, out_shape=(SDS((t, LANE), bf16), SDS((1, LANE), f32), SDS((1, LANE), f32)), grid=(t // tt,),
        in_specs=[pl.BlockSpec((tt, LANE), lambda i: (i, 20)), _vec_spec(LANE), _vec_spec(LANE), _row_spec(tt, LANE)],
        out_specs=(_row_spec(tt, LANE), _vec_spec(LANE), _vec_spec(LANE)),
        name=name, compiler_params=_cparams(("arbitrary",)),
    )(p, neg_a, dt_bias, dbg)


def _tri(lower_incl):
    r = lax.broadcasted_iota(jnp.int32, (CHUNK, CHUNK), 0)
    c = lax.broadcasted_iota(jnp.int32, (CHUNK, CHUNK), 1)
    return r >= c if lower_incl else r > c


def _triu_f32():
    r = lax.broadcasted_iota(jnp.int32, (CHUNK, CHUNK), 0)
    c = lax.broadcasted_iota(jnp.int32, (CHUNK, CHUNK), 1)
    return (r <= c).astype(f32)


def _chunk_common(bg, bgt):
    lower = _tri(True)
    gcol = _dotf(lower.astype(f32), bg)
    grow = _dotf(bgt, _triu_f32())
    return gcol, grow


def _chunk_terms(k, v, beta, gc, gr):
    lower, strict = _tri(True), _tri(False)
    e = jnp.exp(gc)
    g_last = gr[:, CHUNK - 1:CHUNK]
    f = jnp.exp(g_last - gc)
    dm = jnp.exp(jnp.where(lower, gc - gr, -1e30))
    kb = k * beta
    kk = _dotb_nt(kb, k)
    a = jnp.where(strict, kk * dm, 0.0)
    eye = (lax.broadcasted_iota(jnp.int32, (CHUNK, CHUNK), 0) == lax.broadcasted_iota(jnp.int32, (CHUNK, CHUNK), 1)).astype(f32)
    pw = -a
    tinv = eye + pw
    for _ in range(5):
        pw = _dotf(pw, pw)
        tinv = tinv + _dotf(tinv, pw)
    u = _dotb(tinv, v * beta)
    w = _dotb(tinv, kb * e)
    return dict(e=e, f=f, gl=jnp.exp(g_last), dm=dm, kb=kb, kk=kk, tinv=tinv, u=u, w=w, kd=k * f)


def gdn_forward(q, k, v, bg, bgt, s0, with_out, name):
    _, t, _ = k.shape
    nc = t // CHUNK

    def body(q_ref, k_ref, v_ref, bg_ref, bgt_ref, s0_ref, o_ref, sall_ref, sfin_ref, s_ref):
        i = pl.program_id(1)

        @pl.when(i == 0)
        def _():
            s_ref[...] = s0_ref[0]

        bgv = bg_ref[0]
        gcol, grow = _chunk_common(bgv, bgt_ref[0, 0])
        lower = _tri(True)
        for h in range(GDN_HEADS):
            cs = slice(h * LANE, (h + 1) * LANE)
            kh, vh = k_ref[0, :, cs], v_ref[0, :, cs]
            gc, gr = gcol[:, 4 + h:5 + h], grow[4 + h:5 + h, :]
            c = _chunk_terms(kh, vh, bgv[:, h:h + 1], gc, gr)
            s = s_ref[h]
            sall_ref[0, 0, h] = s
            vn = c["u"] - _dotb(c["w"], s)
            if with_out:
                qh = q_ref[0, :, cs]
                pm = jnp.where(lower, _dotb_nt(qh, kh) * c["dm"], 0.0)
                o_ref[0, :, cs] = _dotb(qh * c["e"], s) + _dotb(pm, vn)
            else:
                o_ref[0, :, cs] = jnp.zeros((CHUNK, LANE), f32)
            s_ref[h] = c["gl"] * s + _dotb_tn(c["kd"], vn)

        @pl.when(i == nc - 1)
        def _():
            sfin_ref[0] = s_ref[...]

    seq = pl.BlockSpec((1, CHUNK, 512), lambda d, i: (d, i, 0))
    st = pl.BlockSpec((1, GDN_HEADS, LANE, LANE), lambda d, i: (d, 0, 0, 0))
    return pl.pallas_call(
        body,
        out_shape=(SDS((2, t, 512), f32), SDS((2, nc, GDN_HEADS, LANE, LANE), f32), SDS((2, GDN_HEADS, LANE, LANE), f32)),
        grid=(2, nc),
        in_specs=[seq, seq, seq, pl.BlockSpec((1, CHUNK, 8), lambda d, i: (d, i, 0)),
                  pl.BlockSpec((1, 1, 8, CHUNK), lambda d, i: (d, i, 0, 0)), st],
        out_specs=(seq, pl.BlockSpec((1, 1, GDN_HEADS, LANE, LANE), lambda d, i: (d, i, 0, 0, 0)), st),
        scratch_shapes=[pltpu.VMEM((GDN_HEADS, LANE, LANE), f32)], name=name,
        compiler_params=_cparams(("parallel", "arbitrary")),
    )(q, k, v, bg, bgt, s0)


def gdn_backward(q, k, v, bg, bgt, sall, d_o, ds_fin, with_out, name):
    _, t, _ = k.shape
    nc = t // CHUNK

    def body(q_ref, k_ref, v_ref, bg_ref, bgt_ref, sall_ref, do_ref, dsf_ref,
             dq_ref, dk_ref, dv_ref, dbg_ref, ds0_ref, ds_ref):
        i = pl.program_id(1)

        @pl.when(i == 0)
        def _():
            ds_ref[...] = dsf_ref[0]

        bgv = bg_ref[0]
        gcol, grow = _chunk_common(bgv, bgt_ref[0, 0])
        lower, strict = _tri(True), _tri(False)
        ones = jnp.ones((CHUNK, LANE), f32)
        lane8 = lax.broadcasted_iota(jnp.int32, (1, 8), 1)
        last = (lax.broadcasted_iota(jnp.int32, (CHUNK, 1), 0) == CHUNK - 1).astype(f32)
        dbeta_all = jnp.zeros((CHUNK, 8), f32)
        dgc_all = jnp.zeros((CHUNK, 8), f32)
        for h in range(GDN_HEADS):
            cs = slice(h * LANE, (h + 1) * LANE)
            kh, vh = k_ref[0, :, cs], v_ref[0, :, cs]
            beta = bgv[:, h:h + 1]
            gc, gr = gcol[:, 4 + h:5 + h], grow[4 + h:5 + h, :]
            c = _chunk_terms(kh, vh, beta, gc, gr)
            e, f, gl, dm, kb, kk, tinv, u, w, kd = (c[n] for n in ("e", "f", "gl", "dm", "kb", "kk", "tinv", "u", "w", "kd"))
            s = sall_ref[0, 0, h]
            dsn = ds_ref[h]
            vn = u - _dotb(w, s)
            ds = gl * dsn
            dgl = jnp.sum(jnp.sum(s * dsn, axis=1, keepdims=True), axis=0, keepdims=True)
            dkd = _dotb_nt(vn, dsn)
            dvn = _dotb(kd, dsn)
            dm_grad = jnp.zeros((CHUNK, CHUNK), f32)
            de = jnp.zeros((CHUNK, 1), f32)
            if with_out:
                qh = q_ref[0, :, cs]
                doh = do_ref[0, :, cs]
                qk = _dotb_nt(qh, kh)
                pm = jnp.where(lower, qk * dm, 0.0)
                qd = qh * e
                dqd = _dotb_nt(doh, s)
                ds = ds + _dotb_tn(qd, doh)
                dpm = jnp.where(lower, _dotb_nt(doh, vn), 0.0)
                dvn = dvn + _dotb_tn(pm, doh)
                dqk = dpm * dm
                dm_grad = dm_grad + dpm * qk
                dq = _dotb(dqk, kh) + dqd * e
                dk = _dotb_tn(dqk, qh)
                de = de + jnp.sum(dqd * qh, axis=1, keepdims=True)
                dq_ref[0, :, cs] = dq
            else:
                dk = jnp.zeros((CHUNK, LANE), f32)
                dq_ref[0, :, cs] = jnp.zeros((CHUNK, LANE), f32)
            dw = -_dotb_nt(dvn, s)
            ds = ds - _dotb_tn(w, dvn)
            drv = _dotb_tn(tinv, dvn)
            drk = _dotb_tn(tinv, dw)
            da = -jnp.where(strict, _dotb_nt(drv, u) + _dotb_nt(drk, w), 0.0)
            dbeta = jnp.sum(drv * vh, axis=1, keepdims=True)
            dv_ref[0, :, cs] = drv * beta
            dkb = drk * e
            de = de + jnp.sum(drk * kb, axis=1, keepdims=True)
            dkk = da * dm
            dm_grad = dm_grad + da * kk
            dkb = dkb + _dotb(dkk, kh)
            dk = dk + _dotb_tn(dkk, kb) + dkd * f
            df = jnp.sum(dkd * kh, axis=1, keepdims=True)
            dbeta = dbeta + jnp.sum(dkb * kh, axis=1, keepdims=True)
            dk_ref[0, :, cs] = dk + dkb * beta
            m = dm_grad * dm
            rsum = jnp.sum(m, axis=1, keepdims=True)
            csum = _dotf(m, ones, (((0,), (0,)), ((), ())))[:, 0:1]
            dgl_tot = jnp.sum(df * f, axis=0, keepdims=True) + dgl * gl
            dgc = de * e - df * f + rsum - csum + last * dgl_tot
            dbeta_all = dbeta_all + dbeta * (lane8 == h).astype(f32)
            dgc_all = dgc_all + dgc * (lane8 == 4 + h).astype(f32)
            ds_ref[h] = ds
        dbg_ref[0] = dbeta_all + _dotf(_triu_f32(), dgc_all)

        @pl.when(i == nc - 1)
        def _():
            ds0_ref[0] = ds_ref[...]

    seq = pl.BlockSpec((1, CHUNK, 512), lambda d, i: (d, nc - 1 - i, 0))
    st = pl.BlockSpec((1, GDN_HEADS, LANE, LANE), lambda d, i: (d, 0, 0, 0))
    gspec = pl.BlockSpec((1, CHUNK, 8), lambda d, i: (d, nc - 1 - i, 0))
    return pl.pallas_call(
        body,
        out_shape=(SDS((2, t, 512), f32), SDS((2, t, 512), f32), SDS((2, t, 512), f32), SDS((2, t, 8), f32),
                   SDS((2, GDN_HEADS, LANE, LANE), f32)),
        grid=(2, nc),
        in_specs=[seq, seq, seq, gspec, pl.BlockSpec((1, 1, 8, CHUNK), lambda d, i: (d, nc - 1 - i, 0, 0)),
                  pl.BlockSpec((1, 1, GDN_HEADS, LANE, LANE), lambda d, i: (d, nc - 1 - i, 0, 0, 0)), seq, st],
        out_specs=(seq, seq, seq, gspec, st),
        scratch_shapes=[pltpu.VMEM((GDN_HEADS, LANE, LANE), f32)], name=name,
        compiler_params=_cparams(("parallel", "arbitrary")),
    )(q, k, v, bg, bgt, sall, d_o, ds_fin)


def _my_position():
    x, y, c = lax.axis_index("x"), lax.axis_index("y"), lax.axis_index("c")
    return x, y, c, 4 * x + 2 * y + c


def exchange(arrays, scatter, name):
    n = len(arrays)
    shapes = [a.shape[1:] if scatter else a.shape for a in arrays]

    def body(*refs):
        ins, outs = refs[:n], refs[n:2 * n]
        send_sems, recv_sems, local_sems = refs[2 * n:]
        x, y, c, me = _my_position()
        started = []
        for a in range(n):
            mine = pltpu.make_async_copy(ins[a].at[me] if scatter else ins[a], outs[a].at[me], local_sems.at[a])
            mine.start()
            started.append(mine)
        waits = []
        for r in range(1, N_DEV):
            px = 1 - x if r & 4 else x
            py = 1 - y if r & 2 else y
            pc = 1 - c if r & 1 else c
            pid = 4 * px + 2 * py + pc
            for a in range(n):
                cp = pltpu.make_async_remote_copy(
                    src_ref=ins[a].at[pid] if scatter else ins[a], dst_ref=outs[a].at[me],
                    send_sem=send_sems.at[a, r - 1], recv_sem=recv_sems.at[a, r - 1],
                    device_id=(px, py, pc), device_id_type=pl.DeviceIdType.MESH)
                cp.start()
                arrive = pltpu.make_async_remote_copy(
                    src_ref=ins[a].at[pid] if scatter else ins[a], dst_ref=outs[a].at[pid],
                    send_sem=send_sems.at[a, r - 1], recv_sem=recv_sems.at[a, r - 1],
                    device_id=(px, py, pc), device_id_type=pl.DeviceIdType.MESH)
                waits.append((cp, arrive))
        for cp, arrive in waits:
            cp.wait_send()
            arrive.wait_recv()
        for mine in started:
            mine.wait()

    any_spec = pl.BlockSpec(memory_space=pl.ANY)
    return pl.pallas_call(
        body, out_shape=tuple(SDS((N_DEV,) + tuple(s), a.dtype) for s, a in zip(shapes, arrays)),
        in_specs=[any_spec] * n, out_specs=tuple([any_spec] * n),
        scratch_shapes=[pltpu.SemaphoreType.DMA((n, N_DEV - 1)), pltpu.SemaphoreType.DMA((n, N_DEV - 1)),
                        pltpu.SemaphoreType.DMA((n,))],
        name=name,
    )(*arrays)


def ada_forward(a_raw, ada_w, ada_b_loc, name):
    def body(a_ref, w_ref, b_ref, o_ref):
        a = _silu(a_ref[...])
        for l in range(DEPTH):
            o_ref[l] = _dotf(a, w_ref[l]) + b_ref[l]

    return pl.pallas_call(body, out_shape=SDS((DEPTH, 16, ada_w.shape[2]), f32), name=name,
                          compiler_params=_cparams())(a_raw, ada_w, ada_b_loc)


def ada_backward(a_raw, ada_w, dm, name):
    def body(a_ref, w_ref, dm_ref, gw_ref, dcc_ref):
        a = _silu(a_ref[...])
        for l in range(DEPTH):
            gw_ref[l] = _dotf(a, dm_ref[l], (((0,), (0,)), ((), ())))
        dcc_ref[...] = _dotf(dm_ref[0, 8:16, :], w_ref[0], (((1,), (1,)), ((), ())))

    return pl.pallas_call(body, out_shape=(SDS(ada_w.shape, f32), SDS((8, ada_w.shape[1]), f32)), name=name,
                          compiler_params=_cparams())(a_raw, ada_w, dm)


def sum_parts(parts, name):
    _, r, c = parts.shape

    def body(p_ref, o_ref):
        acc = p_ref[0]
        for i in range(1, N_DEV):
            acc = acc + p_ref[i]
        o_ref[...] = acc

    return pl.pallas_call(body, out_shape=SDS((r, c), f32), name=name, compiler_params=_cparams())(parts)


def cctx_grad(parts, c_ctx, name):
    def body(p_ref, c_ref, o_ref):
        acc = p_ref[0, 0:1, :]
        for i in range(1, N_DEV):
            acc = acc + p_ref[i, 0:1, :]
        o_ref[...] = acc * _dsilu(c_ref[...])

    return pl.pallas_call(body, out_shape=SDS((1, c_ctx.shape[1]), f32), name=name, compiler_params=_cparams())(parts, c_ctx)


def _adamw_math(g, w, m, v):
    m = ADAM_B1 * m + (1.0 - ADAM_B1) * g
    v = ADAM_B2 * v + (1.0 - ADAM_B2) * (g * g)
    m_hat = m / (1.0 - ADAM_B1 ** ADAM_STEP)
    v_hat = v / (1.0 - ADAM_B2 ** ADAM_STEP)
    delta = -ADAM_LR * (m_hat / (jnp.sqrt(v_hat) + ADAM_EPS) + ADAM_WD * w)
    return delta, m, v


def adamw(parts, w, m, v, name):
    n, r, c = parts.shape
    tr = _pick(r, (256, 128, 64, 32, 16, 8))

    def body(p_ref, w_ref, m_ref, v_ref, g_ref, d_ref, nm_ref, nv_ref):
        g = p_ref[0].astype(f32)
        for i in range(1, n):
            g = g + p_ref[i].astype(f32)
        g_ref[...] = g
        d_ref[...], nm_ref[...], nv_ref[...] = _adamw_math(g, w_ref[...], m_ref[...], v_ref[...])

    blk = pl.BlockSpec((tr, c), lambda i: (i, 0))
    out = SDS((r, c), f32)
    return pl.pallas_call(
        body, out_shape=(out, out, out, out), grid=(r // tr,),
        in_specs=[pl.BlockSpec((n, tr, c), lambda i: (0, i, 0)), blk, blk, blk], out_specs=(blk, blk, blk, blk),
        name=name, compiler_params=_cparams(("parallel",)),
    )(parts, w, m, v)


def adamw_small(items, name):
    n = len(items)

    def body(*refs):
        ins, outs = refs[:4 * n], refs[4 * n:]
        for i in range(n):
            g, w, m, v = (ins[4 * i + j][...] for j in range(4))
            outs[3 * i][...], outs[3 * i + 1][...], outs[3 * i + 2][...] = _adamw_math(g, w, m, v)

    flat = [a for it in items for a in it]
    out_shape = tuple(SDS(it[1].shape, f32) for it in items for _ in range(3))
    res = pl.pallas_call(body, out_shape=out_shape, name=name, compiler_params=_cparams())(*flat)
    return [tuple(res[3 * i:3 * i + 3]) for i in range(n)]


def _unshard(g, axis):
    loc = g.shape[1:]
    return jnp.moveaxis(g, 0, axis).reshape(loc[:axis] + (N_DEV * loc[axis],) + loc[axis + 1:])


def _shard_major(full, axis):
    s = full.shape
    return jnp.moveaxis(full.reshape(s[:axis] + (N_DEV, s[axis] // N_DEV) + s[axis + 1:]), axis, 0)


def _my_block(full, axis, me):
    n = full.shape[axis] // N_DEV
    return lax.dynamic_slice_in_dim(full, me * n, n, axis)


def _pack(arrays):
    flat = [a.reshape(-1) for a in arrays]
    sizes = [f.shape[0] for f in flat]
    total = sum(sizes)
    padded = -(-total // (8 * LANE)) * (8 * LANE)
    flat.append(jnp.zeros((padded - total,), f32))
    offs = [sum(sizes[:i]) for i in range(len(sizes))]
    return jnp.concatenate(flat).reshape(padded // LANE, LANE), offs


def _pad_rows(w, n):
    return jnp.concatenate([w, jnp.zeros((n - w.shape[0],) + w.shape[1:], w.dtype)], 0)


def _dir_stack(a):
    return jnp.stack([a, jnp.flip(a, 0)])


def _gate_pack(bg):
    t = bg.shape[0]
    fwd = jnp.concatenate([bg[:, 0:4], bg[:, 8:12]], 1)
    bwd = jnp.flip(jnp.concatenate([bg[:, 4:8], bg[:, 12:16]], 1), 0)
    bg2 = jnp.stack([fwd, bwd])
    return bg2, bg2.reshape(2, t // CHUNK, CHUNK, 8).transpose(0, 1, 3, 2)


def _gate_unpack(dbg2):
    t = dbg2.shape[1]
    fwd, bwd = dbg2[0], jnp.flip(dbg2[1], 0)
    return jnp.concatenate([fwd[:, :4], bwd[:, :4], fwd[:, 4:], bwd[:, 4:], jnp.zeros((t, LANE - 16), f32)], 1)


def _rows(vec, n):
    m = vec.reshape(n, 1, -1)
    return [m[i] for i in range(n)]


def kernel(x, c, ctx, c_ctx, ada_w, ada_b, ln_g, ln_b, even_w_in, even_w_out, gdn_conv_w, gdn_a_log, gdn_dt_bias, gdn_norm_w, pool_w, pool_scale, odd_w_in, odd_w_out, sconv_w, conf_conv_w, conf_ln_g, conf_ln_b, ffn_w_up, ffn_conv_w, ffn_w_down, loss_target, m_c_ctx, m_ada_w, m_ada_b, m_ln_g, m_ln_b, m_even_w_in, m_even_w_out, m_gdn_conv_w, m_gdn_a_log, m_gdn_dt_bias, m_gdn_norm_w, m_pool_w, m_pool_scale, m_odd_w_in, m_odd_w_out, m_sconv_w, m_conf_conv_w, m_conf_ln_g, m_conf_ln_b, m_ffn_w_up, m_ffn_conv_w, m_ffn_w_down, v_c_ctx, v_ada_w, v_ada_b, v_ln_g, v_ln_b, v_even_w_in, v_even_w_out, v_gdn_conv_w, v_gdn_a_log, v_gdn_dt_bias, v_gdn_norm_w, v_pool_w, v_pool_scale, v_odd_w_in, v_odd_w_out, v_sconv_w, v_conf_conv_w, v_conf_ln_g, v_conf_ln_b, v_ffn_w_up, v_ffn_conv_w, v_ffn_w_down):
    weights = dict(c_ctx=c_ctx, ada_w=ada_w, ada_b=ada_b, ln_g=ln_g, ln_b=ln_b, even_w_in=even_w_in, even_w_out=even_w_out, gdn_conv_w=gdn_conv_w, gdn_a_log=gdn_a_log, gdn_dt_bias=gdn_dt_bias, gdn_norm_w=gdn_norm_w, pool_w=pool_w, pool_scale=pool_scale, odd_w_in=odd_w_in, odd_w_out=odd_w_out, sconv_w=sconv_w, conf_conv_w=conf_conv_w, conf_ln_g=conf_ln_g, conf_ln_b=conf_ln_b, ffn_w_up=ffn_w_up, ffn_conv_w=ffn_conv_w, ffn_w_down=ffn_w_down)
    mom1 = dict(c_ctx=m_c_ctx, ada_w=m_ada_w, ada_b=m_ada_b, ln_g=m_ln_g, ln_b=m_ln_b, even_w_in=m_even_w_in, even_w_out=m_even_w_out, gdn_conv_w=m_gdn_conv_w, gdn_a_log=m_gdn_a_log, gdn_dt_bias=m_gdn_dt_bias, gdn_norm_w=m_gdn_norm_w, pool_w=m_pool_w, pool_scale=m_pool_scale, odd_w_in=m_odd_w_in, odd_w_out=m_odd_w_out, sconv_w=m_sconv_w, conf_conv_w=m_conf_conv_w, conf_ln_g=m_conf_ln_g, conf_ln_b=m_conf_ln_b, ffn_w_up=m_ffn_w_up, ffn_conv_w=m_ffn_conv_w, ffn_w_down=m_ffn_w_down)
    mom2 = dict(c_ctx=v_c_ctx, ada_w=v_ada_w, ada_b=v_ada_b, ln_g=v_ln_g, ln_b=v_ln_b, even_w_in=v_even_w_in, even_w_out=v_even_w_out, gdn_conv_w=v_gdn_conv_w, gdn_a_log=v_gdn_a_log, gdn_dt_bias=v_gdn_dt_bias, gdn_norm_w=v_gdn_norm_w, pool_w=v_pool_w, pool_scale=v_pool_scale, odd_w_in=v_odd_w_in, odd_w_out=v_odd_w_out, sconv_w=v_sconv_w, conf_conv_w=v_conf_conv_w, conf_ln_g=v_conf_ln_g, conf_ln_b=v_conf_ln_b, ffn_w_up=v_ffn_w_up, ffn_conv_w=v_ffn_conv_w, ffn_w_down=v_ffn_w_down)
    order = list(weights)
    me = 4 * lax.axis_index("x") + 2 * lax.axis_index("y") + lax.axis_index("c")
    x, ctx, target = x[0], ctx[0], loss_target[0]
    t, d = x.shape
    tc = ctx.shape[0]

    small_in = [ln_g, ln_b, gdn_conv_w, sconv_w, conf_conv_w, ffn_conv_w, c]
    small_axes = [2, 2, 1, 1, 1, 3, 0]
    small_pack, small_offs = _pack(small_in)
    wire = [w.astype(bf16) for w in (even_w_in, even_w_out, odd_w_in, odd_w_out, ffn_w_up, ffn_w_down)]
    gath = exchange(wire + [small_pack], False, "gather_weights")
    e_in = even_w_in.shape[1] * N_DEV
    e_pad = -(-e_in // LANE) * LANE
    win_e = jnp.pad(_unshard(gath[0], 1), ((0, 0), (0, e_pad - e_in)))
    wout_e = _unshard(gath[1], 0)
    win_o = _unshard(gath[2], 1)
    wout_o = _unshard(gath[3], 0)
    wup = _unshard(gath[4], 2)
    wdown = _unshard(gath[5], 1)
    sm = gath[6].reshape(N_DEV, -1)
    lng_f, lnb_f, gconv_f, sconv_f, cconv_f, fconv_f, c_all = [
        _unshard(sm[:, o:o + a.size].reshape((N_DEV,) + a.shape), ax) for a, o, ax in zip(small_in, small_offs, small_axes)]
    gw8 = _pad_rows(gconv_f, 8)
    sw8 = _pad_rows(sconv_f, 8)
    cw32 = _pad_rows(cconv_f, 32)
    fw16 = [_pad_rows(fconv_f[l].reshape(9, D_FF), 16) for l in range(DEPTH)]

    a_raw = jnp.concatenate([c_all, c_ctx[None], jnp.zeros((7, d), f32)], 0)
    ncol = ada_w.shape[2]
    ada_b_loc = lax.dynamic_slice_in_dim(ada_b, me * ncol, ncol, 1)[:, None, :]
    modpart = ada_forward(a_raw, ada_w, ada_b_loc, "ada_forward")
    mod_send = jnp.stack([jnp.transpose(modpart[:, :N_DEV], (1, 0, 2)),
                          jnp.broadcast_to(modpart[:, N_DEV][None], (N_DEV, DEPTH, ncol))], axis=2)
    mod_recv = exchange([mod_send], True, "scatter_mod")[0]
    mod = jnp.transpose(mod_recv[:, :, 0, :], (1, 0, 2)).reshape(DEPTH, 6 * d)
    modc = mod_recv[:, 0, 1, :].reshape(6 * d)
    sh_c, sc_c = modc[None, :d], modc[None, d:2 * d]
    mods = [_rows(mod[l], 6) for l in range(DEPTH)]
    lng = [[lng_f[l, j][None] for j in range(2)] for l in range(DEPTH)]
    lnb = [[lnb_f[l, j][None] for j in range(2)] for l in range(DEPTH)]

    neg_a = jnp.zeros((1, LANE), f32).at[0, 8:16].set(-jnp.exp(gdn_a_log).reshape(8))
    dt_row = jnp.zeros((1, LANE), f32).at[0, 8:16].set(gdn_dt_bias.reshape(8))
    nw_row, ps_row = gdn_norm_w[None], pool_scale[None]
    cg_row, cb_row = conf_ln_g[None], conf_ln_b[None]
    q_scale = GDN_DK ** -0.5

    sh_m, sc_m, gt_m, sh_f, sc_f, gt_f = mods[0]
    u0 = modulate(x, sc_m, sh_m, "mod_l0_mix")
    cu = modulate(ctx, sc_c, sh_c, "mod_ctx")
    p0 = matmul(u0, win_e, "nn", f32, "even_in")
    pc = matmul(cu, win_e, "nn", f32, "even_in_ctx")
    qn = gdn_conv(p0, gw8, 0, 4, q_scale, "gdn_conv_q")
    kn = gdn_conv(p0, gw8, 4, 4, 1.0, "gdn_conv_k")
    vv = gdn_conv(p0, gw8, 8, 4, None, "gdn_conv_v")
    kc = gdn_conv(pc, gw8, 4, 4, 1.0, "gdn_conv_k_ctx")
    vc = gdn_conv(pc, gw8, 8, 4, None, "gdn_conv_v_ctx")
    bg2, bgt = _gate_pack(gdn_gates(p0, neg_a, dt_row, "gdn_gates"))
    bgc2, bgtc = _gate_pack(gdn_gates(pc, neg_a, dt_row, "gdn_gates_ctx"))
    q2, k2, v2, kc2, vc2 = (_dir_stack(a) for a in (qn, kn, vv, kc, vc))
    zero_state = jnp.zeros((2, GDN_HEADS, LANE, LANE), f32)
    _, sall_c, sfin_c = gdn_forward(kc2, kc2, vc2, bgc2, bgtc, zero_state, False, "gdn_fwd_ctx")
    o2, sall, _ = gdn_forward(q2, k2, v2, bg2, bgt, sfin_c, True, "gdn_fwd")
    o_f, o_b = o2[0], jnp.flip(o2[1], 0)
    mix0 = jnp.concatenate([gated_rmsnorm(o_f, o_b, p0, nw_row, "gated_rmsnorm"),
                            pool_mix(p0, pool_w, ps_row, "pool_mix")], 1)
    y0 = matmul(mix0, wout_e, "nn", f32, "even_out")
    x1 = res_layernorm(x, y0, gt_m, lng[0][0], lnb[0][0], "resln_l0_mix")
    u1 = modulate(x1, sc_f, sh_f, "mod_l0_ffn")
    h0 = matmul(u1, wup[0], "nn", f32, "ffn_up_l0")
    f0 = ffn_conv(h0, fw16[0], "ffn_conv_l0")
    y0f = matmul(f0, wdown[0], "nn", f32, "ffn_down_l0")
    x2 = res_layernorm(x1, y0f, gt_f, lng[0][1], lnb[0][1], "resln_l0_ffn")

    sh_m1, sc_m1, gt_m1, sh_f1, sc_f1, gt_f1 = mods[1]
    u2 = modulate(x2, sc_m1, sh_m1, "mod_l1_mix")
    p1 = matmul(u2, win_o, "nn", f32, "odd_in")
    zc = conf_conv(p1, cw32, "conf_conv")
    mix1 = jnp.concatenate([short_conv(p1, sw8, "short_conv"), ln_silu(zc, cg_row, cb_row, "conf_ln_silu")], 1)
    y1 = matmul(mix1, wout_o, "nn", f32, "odd_out")
    x3 = res_layernorm(x2, y1, gt_m1, lng[1][0], lnb[1][0], "resln_l1_mix")
    u3 = modulate(x3, sc_f1, sh_f1, "mod_l1_ffn")
    h1 = matmul(u3, wup[1], "nn", f32, "ffn_up_l1")
    f1 = ffn_conv(h1, fw16[1], "ffn_conv_l1")
    y1f = matmul(f1, wdown[1], "nn", f32, "ffn_down_l1")
    x4 = res_layernorm(x3, y1f, gt_f1, lng[1][1], lnb[1][1], "resln_l1_ffn")

    loss_row, dx4 = loss_head(x4, target, "loss_head")
    loss = lax.psum(loss_row[0, 0], ("x", "y", "c"))

    def ffn_backward(dout, x_in, y, gate, g_row, scale, u, h, f, l):
        dxr, dy, dgt, dlg, dlb = res_layernorm_bwd(dout, x_in, y, gate, g_row, f"resln_bwd_l{l}_ffn")
        df = matmul(dy, wdown[l], "nt", f32, f"ffn_down_dgrad_l{l}")
        g_down = matmul(f, dy, "tn", bf16, f"ffn_down_wgrad_l{l}")
        dh, dcw = ffn_conv_bwd(h, fw16[l], df, f"ffn_conv_bwd_l{l}")
        du = matmul(dh, wup[l], "nt", f32, f"ffn_up_dgrad_l{l}")
        g_up = matmul(u, dh, "tn", bf16, f"ffn_up_wgrad_l{l}")
        dx_in, dsc, dsh = modulate_bwd(du, x_in, scale, dxr, f"mod_bwd_l{l}_ffn")
        return dx_in, (dsh, dsc, dgt), (dlg, dlb), dcw, g_up, g_down

    dx3, dmod_f1, dln_f1, dfcw1, g_up1, g_down1 = ffn_backward(dx4, x3, y1f, gt_f1, lng[1][1], sc_f1, u3, h1, f1, 1)

    dxr, dy, dgt, dlg, dlb = res_layernorm_bwd(dx3, x2, y1, gt_m1, lng[1][0], "resln_bwd_l1_mix")
    dln_m1 = (dlg, dlb)
    dmix = matmul(dy, wout_o, "nt", f32, "odd_out_dgrad")
    g_wout_o = matmul(mix1, dy, "tn", bf16, "odd_out_wgrad")
    dgb, dgc, dhh, d_sconv = short_conv_bwd(p1, sw8, dmix, "short_conv_bwd")
    dzc, d_cg, d_cb = ln_silu_bwd(zc, cg_row, cb_row, dmix, "conf_ln_silu_bwd")
    dga, dgbb, d_cconv = conf_conv_bwd(p1, cw32, dzc, "conf_conv_bwd")
    dp1 = jnp.concatenate([dgb, dgc, dhh, dga, dgbb], 1)
    du = matmul(dp1, win_o, "nt", f32, "odd_in_dgrad")
    g_win_o = matmul(u2, dp1, "tn", bf16, "odd_in_wgrad")
    dx2, dsc, dsh = modulate_bwd(du, x2, sc_m1, dxr, "mod_bwd_l1_mix")
    dmod_m1 = (dsh, dsc, dgt)

    dx1, dmod_f0, dln_f0, dfcw0, g_up0, g_down0 = ffn_backward(dx2, x1, y0f, gt_f, lng[0][1], sc_f, u1, h0, f0, 0)

    dxr, dy, dgt, dlg, dlb = res_layernorm_bwd(dx1, x, y0, gt_m, lng[0][0], "resln_bwd_l0_mix")
    dln_m0 = (dlg, dlb)
    dmix = matmul(dy, wout_e, "nt", f32, "even_out_dgrad")
    g_wout_e = matmul(mix0, dy, "tn", bf16, "even_out_wgrad")
    d_o, dgate, d_nw = gated_rmsnorm_bwd(o_f, o_b, p0, nw_row, dmix, "gated_rmsnorm_bwd")
    dpool, d_pw, d_ps = pool_mix_bwd(p0, pool_w, ps_row, dmix, "pool_mix_bwd")
    dq2, dk2, dv2, dbg2, ds0 = gdn_backward(q2, k2, v2, bg2, bgt, sall, _dir_stack(d_o), zero_state, True, "gdn_bwd")
    _, dkc2, dvc2, dbgc2, _ = gdn_backward(kc2, kc2, vc2, bgc2, bgtc, sall_c, jnp.zeros((2, tc, 512), f32), ds0, False, "gdn_bwd_ctx")
    dqp, dwq = gdn_conv_bwd(p0, gw8, dq2[0], jnp.flip(dq2[1], 0), 0, 4, q_scale, "gdn_conv_q_bwd")
    dkp, dwk = gdn_conv_bwd(p0, gw8, dk2[0], jnp.flip(dk2[1], 0), 4, 4, 1.0, "gdn_conv_k_bwd")
    dvp, dwv = gdn_conv_bwd(p0, gw8, dv2[0], jnp.flip(dv2[1], 0), 8, 4, None, "gdn_conv_v_bwd")
    dkcp, dwkc = gdn_conv_bwd(pc, gw8, dkc2[0], jnp.flip(dkc2[1], 0), 4, 4, 1.0, "gdn_conv_k_ctx_bwd")
    dvcp, dwvc = gdn_conv_bwd(pc, gw8, dvc2[0], jnp.flip(dvc2[1], 0), 8, 4, None, "gdn_conv_v_ctx_bwd")
    ds_l, da_l, ddt_l = gdn_gates_bwd(p0, neg_a, dt_row, _gate_unpack(dbg2), "gdn_gates_bwd")
    ds_c, da_c, ddt_c = gdn_gates_bwd(pc, neg_a, dt_row, _gate_unpack(dbgc2), "gdn_gates_ctx_bwd")
    zc512 = jnp.zeros((tc, 512), bf16)
    dp_all = jnp.concatenate([
        jnp.concatenate([dqp, dkp, dvp, dgate, dpool, ds_l], 1),
        jnp.concatenate([zc512, dkcp, dvcp, zc512, zc512, ds_c], 1)], 0)
    u_all = jnp.concatenate([u0, cu], 0)
    du_all = matmul(dp_all, win_e, "nt", f32, "even_in_dgrad")
    g_win_e = matmul(u_all, dp_all, "tn", bf16, "even_in_wgrad")[:, :e_in]
    grad_x, dsc, dsh = modulate_bwd(du_all, x, sc_m, dxr, "mod_bwd_l0_mix")
    dmod_m0 = (dsh, dsc, dgt)
    _, dsc_c, dsh_c = modulate_bwd(du_all, ctx, sc_c, jnp.zeros((tc, d), f32), "mod_bwd_ctx", du_row0=t)

    dmod0 = jnp.concatenate(dmod_m0 + dmod_f0, 1)
    dmod1 = jnp.concatenate(dmod_m1 + dmod_f1, 1)
    dmodc = jnp.concatenate([dsh_c, dsc_c], 1)
    d_gconv = jnp.concatenate([dwq, dwk + dwkc, dwv + dwvc], 1)[:5]
    small_g = [dmod0, dmod1, dmodc,
               jnp.concatenate([dln_m0[0], dln_f0[0], dln_m1[0], dln_f1[0]], 0),
               jnp.concatenate([dln_m0[1], dln_f0[1], dln_m1[1], dln_f1[1]], 0),
               d_gconv, (da_l + da_c)[0, 8:16], (ddt_l + ddt_c)[0, 8:16], d_nw, d_pw, d_ps,
               d_sconv[:3], d_cconv[:31], d_cg, d_cb, jnp.stack([dfcw0[:9], dfcw1[:9]])]
    gpack, goffs = _pack(small_g)
    gparts = exchange([gpack], False, "gather_small_grads")[0]
    gsum = sum_parts(gparts, "sum_small_grads").reshape(-1)
    gs = [gsum[o:o + a.size].reshape(a.shape) for a, o in zip(small_g, goffs)]
    gflat = gparts.reshape(N_DEV, -1)
    dmodc_cols = _my_block(jnp.pad(gs[2], ((0, 0), (0, 4 * d))), 1, me)
    dm = jnp.stack([
        jnp.concatenate([_my_block(gflat[:, goffs[0]:goffs[0] + 6 * d], 1, me), dmodc_cols, jnp.zeros((7, ncol), f32)], 0),
        jnp.concatenate([_my_block(gflat[:, goffs[1]:goffs[1] + 6 * d], 1, me), jnp.zeros((8, ncol), f32)], 0)])
    g_ada_w, dcc = ada_backward(a_raw, ada_w, dm, "ada_backward")
    g_cctx = cctx_grad(exchange([dcc], False, "gather_cctx")[0], c_ctx[None], "cctx_grad")

    grads = {}
    grads["c_ctx"] = g_cctx.reshape(c_ctx.shape)
    grads["ada_b"] = jnp.concatenate([gs[0] + jnp.pad(gs[2], ((0, 0), (0, 4 * d))), gs[1]], 0)
    grads["ln_g"] = _my_block(gs[3].reshape(DEPTH, 2, d), 2, me)
    grads["ln_b"] = _my_block(gs[4].reshape(DEPTH, 2, d), 2, me)
    grads["gdn_conv_w"] = _my_block(gs[5], 1, me)
    grads["gdn_a_log"] = gs[6].reshape(2, GDN_HEADS)
    grads["gdn_dt_bias"] = gs[7].reshape(2, GDN_HEADS)
    grads["gdn_norm_w"] = gs[8].reshape(LANE)
    grads["pool_w"] = gs[9]
    grads["pool_scale"] = gs[10].reshape(-1)
    grads["sconv_w"] = _my_block(gs[11], 1, me)
    grads["conf_conv_w"] = _my_block(gs[12], 1, me)
    grads["conf_ln_g"] = gs[13].reshape(-1)
    grads["conf_ln_b"] = gs[14].reshape(-1)
    grads["ffn_conv_w"] = _my_block(gs[15].reshape(DEPTH, 3, 3, D_FF), 3, me)

    def as2d(a):
        return a.reshape(-1, a.shape[-1]) if a.ndim > 1 else a.reshape(1, -1)

    small_names = [n for n in order if n in grads]
    res = adamw_small([(as2d(grads[n]), as2d(weights[n]), as2d(mom1[n]), as2d(mom2[n])) for n in small_names], "adamw_small")
    delta, new_m, new_v = {}, {}, {}
    for n, (dl, nm, nv) in zip(small_names, res):
        delta[n], new_m[n], new_v[n] = (a.reshape(weights[n].shape) for a in (dl, nm, nv))

    big = [("even_w_in", g_win_e, 1), ("even_w_out", g_wout_e, 0), ("odd_w_in", g_win_o, 1), ("odd_w_out", g_wout_o, 0),
           ("ffn_w_up", jnp.stack([g_up0, g_up1]), 2), ("ffn_w_down", jnp.stack([g_down0, g_down1]), 1)]
    recv = exchange([_shard_major(g, ax) for _, g, ax in big], True, "scatter_grads")
    big_parts = {n: r for (n, _, _), r in zip(big, recv)}
    big_parts["ada_w"] = g_ada_w[None]
    for n, parts in big_parts.items():
        w = weights[n]
        cols = w.shape[-1]
        out = adamw(parts.reshape(parts.shape[0], -1, cols), w.reshape(-1, cols), mom1[n].reshape(-1, cols),
                    mom2[n].reshape(-1, cols), f"adamw_{n}")
        grads[n], delta[n], new_m[n], new_v[n] = (a.reshape(w.shape) for a in out)

    return (loss, grad_x[None], *[grads[n] for n in order], *[delta[n] for n in order],
            *[new_m[n] for n in order], *[new_v[n] for n in order])
```

```python
import functools
import math

import jax
import jax.numpy as jnp
from jax import lax
from jax.experimental import pallas as pl
from jax.experimental.pallas import tpu as pltpu

f32 = jnp.float32
bf16 = jnp.bfloat16
SDS = jax.ShapeDtypeStruct

N_DEV = 8
D_MODEL = 1024
DEPTH = 2
GRID_W = 64
GDN_HEADS = 4
GDN_DK = 128
CHUNK = 64
POOL_WINDOWS = (2, 4, 8, 16)
D_FF = 2816
ALPHA = (2 * DEPTH) ** 0.25
LN_EPS = 1e-5
RMS_EPS = 1e-6
LANE = 128
PAD_ROWS = 72
CONV_ROWS = 256
VMEM_LIMIT = 56 * 2**20

ADAM_LR, ADAM_B1, ADAM_B2, ADAM_EPS, ADAM_WD, ADAM_STEP = 0.001, 0.9, 0.999, 1e-08, 0.01, 10

HI = lax.Precision.HIGHEST


def _cparams(sem=None):
    return pltpu.CompilerParams(dimension_semantics=sem, vmem_limit_bytes=VMEM_LIMIT)


def _silu(x):
    return x * jax.nn.sigmoid(x)


def _dsilu(x):
    s = jax.nn.sigmoid(x)
    return s * (1.0 + x * (1.0 - s))


def _dotb(a, b, dims=(((1,), (0,)), ((), ()))):
    return lax.dot_general(a.astype(bf16), b.astype(bf16), dims, preferred_element_type=f32)


def _dotb_nt(a, b):
    return _dotb(a, b, (((1,), (1,)), ((), ())))


def _dotb_tn(a, b):
    return _dotb(a, b, (((0,), (0,)), ((), ())))


def _dotf(a, b, dims=(((1,), (0,)), ((), ()))):
    return lax.dot_general(a, b, dims, preferred_element_type=f32, precision=HI)


def _pick(n, cands):
    for c in cands:
        if n % c == 0:
            return c
    return n


def matmul(a, b, mode, out_dtype, name):
    if mode == "nn":
        (M, K), N = a.shape, b.shape[1]
    elif mode == "nt":
        (M, K), N = a.shape, b.shape[0]
    else:
        (K, M), N = a.shape, b.shape[1]
    tm = _pick(M, (1024, 768, 512, 256, 128)) if mode != "tn" else _pick(M, (1024, 512, 256, 128))
    tn = _pick(N, (1024, 896, 768, 640, 512, 384, 256, 128))
    tk = _pick(K, (1024, 896, 768, 640, 512, 384, 256, 128)) if mode != "tn" else _pick(K, (1024, 512, 256))
    nk = K // tk
    dims = {"nn": (((1,), (0,)), ((), ())), "nt": (((1,), (1,)), ((), ())), "tn": (((0,), (0,)), ((), ()))}[mode]

    def body(a_ref, b_ref, o_ref, acc_ref):
        k = pl.program_id(2)
        part = lax.dot_general(a_ref[...].astype(bf16), b_ref[...].astype(bf16), dims, preferred_element_type=f32)

        @pl.when(k == 0)
        def _():
            acc_ref[...] = part

        @pl.when(k > 0)
        def _():
            acc_ref[...] += part

        @pl.when(k == nk - 1)
        def _():
            o_ref[...] = acc_ref[...].astype(out_dtype)

    a_spec = {"nn": pl.BlockSpec((tm, tk), lambda i, j, k: (i, k)),
              "nt": pl.BlockSpec((tm, tk), lambda i, j, k: (i, k)),
              "tn": pl.BlockSpec((tk, tm), lambda i, j, k: (k, i))}[mode]
    b_spec = {"nn": pl.BlockSpec((tk, tn), lambda i, j, k: (k, j)),
              "nt": pl.BlockSpec((tn, tk), lambda i, j, k: (j, k)),
              "tn": pl.BlockSpec((tk, tn), lambda i, j, k: (k, j))}[mode]
    return pl.pallas_call(
        body, out_shape=SDS((M, N), out_dtype), grid=(M // tm, N // tn, nk),
        in_specs=[a_spec, b_spec], out_specs=pl.BlockSpec((tm, tn), lambda i, j, k: (i, j)),
        scratch_shapes=[pltpu.VMEM((tm, tn), f32)], name=name,
        compiler_params=_cparams(("parallel", "parallel", "arbitrary")),
    )(a, b)


def _row_tile(t):
    return _pick(t, (512, 256, 128, 64, 32, 16, 8))


def _row_spec(tt, d):
    return pl.BlockSpec((tt, d), lambda i: (i, 0))


def _vec_spec(d):
    return pl.BlockSpec((1, d), lambda i: (0, 0))


def _acc_rows(ref, val):
    @pl.when(pl.program_id(0) == 0)
    def _():
        ref[...] = val

    @pl.when(pl.program_id(0) > 0)
    def _():
        ref[...] += val


def modulate(x, scale, shift, name):
    t, d = x.shape
    tt = _row_tile(t)

    def body(x_ref, sc_ref, sh_ref, o_ref):
        o_ref[...] = (x_ref[...] * (1.0 + sc_ref[...]) + sh_ref[...]).astype(bf16)

    return pl.pallas_call(
        body, out_shape=SDS((t, d), bf16), grid=(t // tt,),
        in_specs=[_row_spec(tt, d), _vec_spec(d), _vec_spec(d)], out_specs=_row_spec(tt, d),
        name=name, compiler_params=_cparams(("parallel",)),
    )(x, scale, shift)


def modulate_bwd(du, x, scale, dres, name, du_row0=0):
    t, d = x.shape
    tt = _row_tile(t)
    blk0 = du_row0 // tt

    def body(du_ref, x_ref, sc_ref, dres_ref, dx_ref, dsc_ref, dsh_ref):
        du_v = du_ref[...]
        dx_ref[...] = du_v * (1.0 + sc_ref[...]) + dres_ref[...]
        _acc_rows(dsc_ref, jnp.sum(du_v * x_ref[...], axis=0, keepdims=True))
        _acc_rows(dsh_ref, jnp.sum(du_v, axis=0, keepdims=True))

    return pl.pallas_call(
        body, out_shape=(SDS((t, d), f32), SDS((1, d), f32), SDS((1, d), f32)), grid=(t // tt,),
        in_specs=[pl.BlockSpec((tt, d), lambda i: (i + blk0, 0)), _row_spec(tt, d), _vec_spec(d), _row_spec(tt, d)],
        out_specs=(_row_spec(tt, d), _vec_spec(d), _vec_spec(d)),
        name=name, compiler_params=_cparams(("arbitrary",)),
    )(du, x, scale, dres)


def _ln_stats(z):
    mu = jnp.mean(z, axis=-1, keepdims=True)
    zc = z - mu
    var = jnp.mean(zc * zc, axis=-1, keepdims=True)
    rstd = lax.rsqrt(var + LN_EPS)
    return zc * rstd, rstd


def _ln_bwd(dxhat, xhat, rstd):
    m1 = jnp.mean(dxhat, axis=-1, keepdims=True)
    m2 = jnp.mean(dxhat * xhat, axis=-1, keepdims=True)
    return rstd * (dxhat - m1 - xhat * m2)


def res_layernorm(x, y, gate, g, b, name):
    t, d = x.shape
    tt = _row_tile(t)

    def body(x_ref, y_ref, gt_ref, g_ref, b_ref, o_ref):
        xhat, _ = _ln_stats(ALPHA * x_ref[...] + gt_ref[...] * y_ref[...])
        o_ref[...] = xhat * g_ref[...] + b_ref[...]

    return pl.pallas_call(
        body, out_shape=SDS((t, d), f32), grid=(t // tt,),
        in_specs=[_row_spec(tt, d), _row_spec(tt, d), _vec_spec(d), _vec_spec(d), _vec_spec(d)],
        out_specs=_row_spec(tt, d), name=name, compiler_params=_cparams(("parallel",)),
    )(x, y, gate, g, b)


def res_layernorm_bwd(dout, x, y, gate, g, name):
    t, d = x.shape
    tt = _row_tile(t)

    def body(do_ref, x_ref, y_ref, gt_ref, g_ref, dxr_ref, dy_ref, dgt_ref, dg_ref, db_ref):
        y_v = y_ref[...]
        do_v = do_ref[...]
        xhat, rstd = _ln_stats(ALPHA * x_ref[...] + gt_ref[...] * y_v)
        dz = _ln_bwd(do_v * g_ref[...], xhat, rstd)
        dxr_ref[...] = ALPHA * dz
        dy_ref[...] = (gt_ref[...] * dz).astype(bf16)
        _acc_rows(dgt_ref, jnp.sum(dz * y_v, axis=0, keepdims=True))
        _acc_rows(dg_ref, jnp.sum(do_v * xhat, axis=0, keepdims=True))
        _acc_rows(db_ref, jnp.sum(do_v, axis=0, keepdims=True))

    vec = SDS((1, d), f32)
    return pl.pallas_call(
        body, out_shape=(SDS((t, d), f32), SDS((t, d), bf16), vec, vec, vec), grid=(t // tt,),
        in_specs=[_row_spec(tt, d), _row_spec(tt, d), _row_spec(tt, d), _vec_spec(d), _vec_spec(d)],
        out_specs=(_row_spec(tt, d), _row_spec(tt, d), _vec_spec(d), _vec_spec(d), _vec_spec(d)),
        name=name, compiler_params=_cparams(("arbitrary",)),
    )(dout, x, y, gate, g)


def loss_head(y, target, name):
    t, d = y.shape
    tt = _row_tile(t)

    def body(y_ref, t_ref, l_ref, dy_ref):
        e = y_ref[...] - t_ref[...]
        dy_ref[...] = e * (1.0 / d)
        part = jnp.sum(jnp.sum(e * e, axis=1, keepdims=True), axis=0, keepdims=True) * (0.5 / d)
        _acc_rows(l_ref, jnp.broadcast_to(part, (1, LANE)))

    return pl.pallas_call(
        body, out_shape=(SDS((1, LANE), f32), SDS((t, d), f32)), grid=(t // tt,),
        in_specs=[_row_spec(tt, d), _row_spec(tt, d)],
        out_specs=(pl.BlockSpec((1, LANE), lambda i: (0, 0)), _row_spec(tt, d)),
        name=name, compiler_params=_cparams(("arbitrary",)),
    )(y, target)


def _fill_pad(pad_ref, val, t):
    zeros = jnp.zeros((PAD_ROWS, LANE), f32)
    pad_ref[0:PAD_ROWS, :] = zeros
    pad_ref[PAD_ROWS + t:2 * PAD_ROWS + t, :] = zeros
    pad_ref[PAD_ROWS:PAD_ROWS + t, :] = val


def _grid_mask(r0, rows, dc):
    col = (lax.broadcasted_iota(jnp.int32, (rows, 1), 0) + r0) % GRID_W
    return ((col + dc >= 0) & (col + dc < GRID_W)).astype(f32)


def _taps_apply(pad_ref, w_ref, taps, r0, rows):
    acc = jnp.zeros((rows, LANE), f32)
    for off, dc, wi in taps:
        xs = pad_ref[PAD_ROWS + r0 + off:PAD_ROWS + r0 + off + rows, :]
        if dc is not None and dc != 0:
            xs = xs * _grid_mask(r0, rows, dc)
        acc = acc + w_ref[wi:wi + 1, :] * xs
    return acc


def _taps_wgrad(pad_ref, dy, taps, r0, rows, nw):
    out = jnp.zeros((nw, LANE), f32)
    rid = lax.broadcasted_iota(jnp.int32, (nw, 1), 0)
    for off, dc, wi in taps:
        xs = pad_ref[PAD_ROWS + r0 + off:PAD_ROWS + r0 + off + rows, :]
        if dc is not None and dc != 0:
            xs = xs * _grid_mask(r0, rows, dc)
        s = jnp.sum(dy * xs, axis=0, keepdims=True)
        out = out + jnp.where(rid == wi, s, 0.0)
    return out


def _transpose_taps(taps):
    return [(-off, None if dc is None else -dc, wi) for off, dc, wi in taps]


def _taps_1d(width):
    return [(j - width // 2, None, j) for j in range(width)]


def _taps_grid3():
    return [(GRID_W * dr + dc, dc, 3 * (dr + 1) + (dc + 1)) for dr in (-1, 0, 1) for dc in (-1, 0, 1)]


def _row_chunks(t):
    r = min(CONV_ROWS, t)
    return [(i * r, r) for i in range(t // r)]


def _col_spec(t, off):
    return pl.BlockSpec((t, LANE), lambda c: (0, c + off))


def _w_spec(nw, off=0):
    return pl.BlockSpec((nw, LANE), lambda c: (0, c + off))


def gdn_conv(p, w, col0, nblk, norm_scale, name):
    t = p.shape[0]
    nw = w.shape[0]
    taps = _taps_1d(5)

    def body(p_ref, w_ref, o_ref, pad_ref):
        _fill_pad(pad_ref, p_ref[...], t)
        for r0, rows in _row_chunks(t):
            a = _silu(_taps_apply(pad_ref, w_ref, taps, r0, rows))
            if norm_scale is not None:
                a = a * (lax.rsqrt(jnp.sum(a * a, axis=-1, keepdims=True) + RMS_EPS) * norm_scale)
            o_ref[r0:r0 + rows, :] = a

    return pl.pallas_call(
        body, out_shape=SDS((t, nblk * LANE), f32), grid=(nblk,),
        in_specs=[_col_spec(t, col0), _w_spec(nw, col0)], out_specs=_col_spec(t, 0),
        scratch_shapes=[pltpu.VMEM((t + 2 * PAD_ROWS, LANE), f32)], name=name,
        compiler_params=_cparams(("parallel",)),
    )(p, w)


def gdn_conv_bwd(p, w, d_a, d_b, col0, nblk, norm_scale, name):
    t = p.shape[0]
    nw = w.shape[0]
    taps = _taps_1d(5)
    ttaps = _transpose_taps(taps)

    def body(p_ref, w_ref, da_ref, db_ref, dp_ref, dw_ref, pad_ref, gpad_ref):
        _fill_pad(pad_ref, p_ref[...], t)
        for r0, rows in _row_chunks(t):
            pre = _taps_apply(pad_ref, w_ref, taps, r0, rows)
            a = _silu(pre)
            dy = da_ref[r0:r0 + rows, :] + db_ref[r0:r0 + rows, :]
            if norm_scale is not None:
                r = lax.rsqrt(jnp.sum(a * a, axis=-1, keepdims=True) + RMS_EPS)
                da = norm_scale * (dy * r - a * (r * r * r) * jnp.sum(dy * a, axis=-1, keepdims=True))
            else:
                da = dy
            gpad_ref[PAD_ROWS + r0:PAD_ROWS + r0 + rows, :] = da * _dsilu(pre)
        zeros = jnp.zeros((PAD_ROWS, LANE), f32)
        gpad_ref[0:PAD_ROWS, :] = zeros
        gpad_ref[PAD_ROWS + t:2 * PAD_ROWS + t, :] = zeros
        dw = jnp.zeros((nw, LANE), f32)
        for r0, rows in _row_chunks(t):
            dp_ref[r0:r0 + rows, :] = _taps_apply(gpad_ref, w_ref, ttaps, r0, rows).astype(bf16)
            dw = dw + _taps_wgrad(pad_ref, gpad_ref[PAD_ROWS + r0:PAD_ROWS + r0 + rows, :], taps, r0, rows, nw)
        dw_ref[...] = dw

    return pl.pallas_call(
        body, out_shape=(SDS((t, nblk * LANE), bf16), SDS((nw, nblk * LANE), f32)), grid=(nblk,),
        in_specs=[_col_spec(t, col0), _w_spec(nw, col0), _col_spec(t, 0), _col_spec(t, 0)],
        out_specs=(_col_spec(t, 0), _w_spec(nw)),
        scratch_shapes=[pltpu.VMEM((t + 2 * PAD_ROWS, LANE), f32)] * 2, name=name,
        compiler_params=_cparams(("parallel",)),
    )(p, w, d_a, d_b)


def short_conv(p, w, name):
    t = p.shape[0]
    nw = w.shape[0]
    taps = _taps_1d(3)

    def body(gb_ref, gc_ref, h_ref, w_ref, o_ref, pad_ref):
        _fill_pad(pad_ref, gc_ref[...] * h_ref[...], t)
        for r0, rows in _row_chunks(t):
            o_ref[r0:r0 + rows, :] = (gb_ref[r0:r0 + rows, :] * _taps_apply(pad_ref, w_ref, taps, r0, rows)).astype(bf16)

    return pl.pallas_call(
        body, out_shape=SDS((t, 4 * LANE), bf16), grid=(4,),
        in_specs=[_col_spec(t, 0), _col_spec(t, 4), _col_spec(t, 8), _w_spec(nw)], out_specs=_col_spec(t, 0),
        scratch_shapes=[pltpu.VMEM((t + 2 * PAD_ROWS, LANE), f32)], name=name,
        compiler_params=_cparams(("parallel",)),
    )(p, p, p, w)


def short_conv_bwd(p, w, dy, name):
    t = p.shape[0]
    nw = w.shape[0]
    taps = _taps_1d(3)
    ttaps = _transpose_taps(taps)

    def body(gb_ref, gc_ref, h_ref, w_ref, dy_ref, dgb_ref, dgc_ref, dh_ref, dw_ref, pad_ref, gpad_ref):
        _fill_pad(pad_ref, gc_ref[...] * h_ref[...], t)
        _fill_pad(gpad_ref, dy_ref[...] * gb_ref[...], t)
        dw = jnp.zeros((nw, LANE), f32)
        for r0, rows in _row_chunks(t):
            sl = slice(r0, r0 + rows)
            dgb_ref[sl, :] = (dy_ref[sl, :] * _taps_apply(pad_ref, w_ref, taps, r0, rows)).astype(bf16)
            dm = _taps_apply(gpad_ref, w_ref, ttaps, r0, rows)
            dgc_ref[sl, :] = (dm * h_ref[sl, :]).astype(bf16)
            dh_ref[sl, :] = (dm * gc_ref[sl, :]).astype(bf16)
            dw = dw + _taps_wgrad(pad_ref, gpad_ref[PAD_ROWS + r0:PAD_ROWS + r0 + rows, :], taps, r0, rows, nw)
        dw_ref[...] = dw

    blk = SDS((t, 4 * LANE), bf16)
    return pl.pallas_call(
        body, out_shape=(blk, blk, blk, SDS((nw, 4 * LANE), f32)), grid=(4,),
        in_specs=[_col_spec(t, 0), _col_spec(t, 4), _col_spec(t, 8), _w_spec(nw), _col_spec(t, 0)],
        out_specs=(_col_spec(t, 0), _col_spec(t, 0), _col_spec(t, 0), _w_spec(nw)),
        scratch_shapes=[pltpu.VMEM((t + 2 * PAD_ROWS, LANE), f32)] * 2, name=name,
        compiler_params=_cparams(("parallel",)),
    )(p, p, p, w, dy)


def conf_conv(p, w, name):
    t = p.shape[0]
    nw = w.shape[0]
    taps = _taps_1d(31)

    def body(a_ref, b_ref, w_ref, o_ref, pad_ref):
        _fill_pad(pad_ref, a_ref[...] * jax.nn.sigmoid(b_ref[...]), t)
        for r0, rows in _row_chunks(t):
            o_ref[r0:r0 + rows, :] = _taps_apply(pad_ref, w_ref, taps, r0, rows)

    return pl.pallas_call(
        body, out_shape=SDS((t, 4 * LANE), f32), grid=(4,),
        in_specs=[_col_spec(t, 12), _col_spec(t, 16), _w_spec(nw)], out_specs=_col_spec(t, 0),
        scratch_shapes=[pltpu.VMEM((t + 2 * PAD_ROWS, LANE), f32)], name=name,
        compiler_params=_cparams(("parallel",)),
    )(p, p, w)


def conf_conv_bwd(p, w, dz, name):
    t = p.shape[0]
    nw = w.shape[0]
    taps = _taps_1d(31)
    ttaps = _transpose_taps(taps)

    def body(a_ref, b_ref, w_ref, dz_ref, da_ref, db_ref, dw_ref, pad_ref, gpad_ref):
        _fill_pad(pad_ref, a_ref[...] * jax.nn.sigmoid(b_ref[...]), t)
        _fill_pad(gpad_ref, dz_ref[...], t)
        dw = jnp.zeros((nw, LANE), f32)
        for r0, rows in _row_chunks(t):
            sl = slice(r0, r0 + rows)
            dm = _taps_apply(gpad_ref, w_ref, ttaps, r0, rows)
            sg = jax.nn.sigmoid(b_ref[sl, :])
            da_ref[sl, :] = (dm * sg).astype(bf16)
            db_ref[sl, :] = (dm * a_ref[sl, :] * sg * (1.0 - sg)).astype(bf16)
            dw = dw + _taps_wgrad(pad_ref, dz_ref[sl, :], taps, r0, rows, nw)
        dw_ref[...] = dw

    blk = SDS((t, 4 * LANE), bf16)
    return pl.pallas_call(
        body, out_shape=(blk, blk, SDS((nw, 4 * LANE), f32)), grid=(4,),
        in_specs=[_col_spec(t, 12), _col_spec(t, 16), _w_spec(nw), _col_spec(t, 0)],
        out_specs=(_col_spec(t, 0), _col_spec(t, 0), _w_spec(nw)),
        scratch_shapes=[pltpu.VMEM((t + 2 * PAD_ROWS, LANE), f32)] * 2, name=name,
        compiler_params=_cparams(("parallel",)),
    )(p, p, w, dz)


def ffn_conv(h, w, name):
    t = h.shape[0]
    nblk = D_FF // LANE
    nw = w.shape[0]
    taps = _taps_grid3()

    def body(a_ref, g_ref, w_ref, o_ref, pad_ref):
        _fill_pad(pad_ref, a_ref[...], t)
        for r0, rows in _row_chunks(t):
            o_ref[r0:r0 + rows, :] = (_silu(_taps_apply(pad_ref, w_ref, taps, r0, rows)) * g_ref[r0:r0 + rows, :]).astype(bf16)

    return pl.pallas_call(
        body, out_shape=SDS((t, D_FF), bf16), grid=(nblk,),
        in_specs=[_col_spec(t, 0), _col_spec(t, nblk), _w_spec(nw)], out_specs=_col_spec(t, 0),
        scratch_shapes=[pltpu.VMEM((t + 2 * PAD_ROWS, LANE), f32)], name=name,
        compiler_params=_cparams(("parallel",)),
    )(h, h, w)


def ffn_conv_bwd(h, w, df, name):
    t = h.shape[0]
    nblk = D_FF // LANE
    nw = w.shape[0]
    taps = _taps_grid3()
    ttaps = _transpose_taps(taps)

    def body(a_ref, g_ref, w_ref, df_ref, dh_ref, dw_ref, pad_ref, gpad_ref, pre_ref):
        half = pl.program_id(1)

        @pl.when(half == 0)
        def _():
            _fill_pad(pad_ref, a_ref[...], t)
            zeros = jnp.zeros((PAD_ROWS, LANE), f32)
            gpad_ref[0:PAD_ROWS, :] = zeros
            gpad_ref[PAD_ROWS + t:2 * PAD_ROWS + t, :] = zeros
            for r0, rows in _row_chunks(t):
                sl = slice(r0, r0 + rows)
                pre = _taps_apply(pad_ref, w_ref, taps, r0, rows)
                pre_ref[sl, :] = pre
                gpad_ref[PAD_ROWS + r0:PAD_ROWS + r0 + rows, :] = df_ref[sl, :] * g_ref[sl, :] * _dsilu(pre)
            dw = jnp.zeros((nw, LANE), f32)
            for r0, rows in _row_chunks(t):
                dh_ref[r0:r0 + rows, :] = _taps_apply(gpad_ref, w_ref, ttaps, r0, rows).astype(bf16)
                dw = dw + _taps_wgrad(pad_ref, gpad_ref[PAD_ROWS + r0:PAD_ROWS + r0 + rows, :], taps, r0, rows, nw)
            dw_ref[...] = dw

        @pl.when(half == 1)
        def _():
            for r0, rows in _row_chunks(t):
                sl = slice(r0, r0 + rows)
                dh_ref[sl, :] = (df_ref[sl, :] * _silu(pre_ref[sl, :])).astype(bf16)

    cspec = lambda off: pl.BlockSpec((t, LANE), lambda c, s: (0, c + off))
    return pl.pallas_call(
        body, out_shape=(SDS((t, 2 * D_FF), bf16), SDS((nw, D_FF), f32)), grid=(nblk, 2),
        in_specs=[cspec(0), cspec(nblk), pl.BlockSpec((nw, LANE), lambda c, s: (0, c)), cspec(0)],
        out_specs=(pl.BlockSpec((t, LANE), lambda c, s: (0, c + nblk * s)), pl.BlockSpec((nw, LANE), lambda c, s: (0, c))),
        scratch_shapes=[pltpu.VMEM((t + 2 * PAD_ROWS, LANE), f32)] * 2 + [pltpu.VMEM((t, LANE), f32)], name=name,
        compiler_params=_cparams(("parallel", "arbitrary")),
    )(h, h, w, df)


def _pool_count(r0, rows, win, t):
    pos = lax.broadcasted_iota(jnp.int32, (rows, 1), 0) + r0
    lo = jnp.clip(pos - win // 2, 0, t)
    hi = jnp.clip(pos - win // 2 + win, 0, t)
    return (hi - lo).astype(f32)


def _window_sum(pad_ref, r0, rows, lo, hi):
    acc = jnp.zeros((rows, LANE), f32)
    for off in range(lo, hi):
        acc = acc + pad_ref[PAD_ROWS + r0 + off:PAD_ROWS + r0 + off + rows, :]
    return acc


def pool_mix(p, pool_w, pool_scale, name):
    t = p.shape[0]

    def body(x_ref, w_ref, s_ref, o_ref, pad_ref):
        for gi, win in enumerate(POOL_WINDOWS):
            cs = slice(gi * LANE, (gi + 1) * LANE)
            _fill_pad(pad_ref, x_ref[:, cs], t)
            wg = w_ref[gi].astype(bf16)
            for r0, rows in _row_chunks(t):
                pooled = _window_sum(pad_ref, r0, rows, -(win // 2), win - win // 2) / _pool_count(r0, rows, win, t) - x_ref[r0:r0 + rows, cs]
                o_ref[r0:r0 + rows, cs] = (_dotb(pooled, wg) * s_ref[:, cs]).astype(bf16)

    return pl.pallas_call(
        body, out_shape=SDS((t, 512), bf16), grid=(1,),
        in_specs=[pl.BlockSpec((t, 512), lambda i: (0, 4)), pl.BlockSpec((4, LANE, LANE), lambda i: (0, 0, 0)),
                  pl.BlockSpec((1, 512), lambda i: (0, 0))],
        out_specs=pl.BlockSpec((t, 512), lambda i: (0, 0)),
        scratch_shapes=[pltpu.VMEM((t + 2 * PAD_ROWS, LANE), f32)], name=name,
        compiler_params=_cparams(("arbitrary",)),
    )(p, pool_w, pool_scale)


def pool_mix_bwd(p, pool_w, pool_scale, dmix, name):
    t = p.shape[0]

    def body(x_ref, w_ref, s_ref, dy_ref, dp_ref, dw_ref, ds_ref, pad_ref, gpad_ref, dpool_ref):
        for gi, win in enumerate(POOL_WINDOWS):
            cs = slice(gi * LANE, (gi + 1) * LANE)
            h = win // 2
            _fill_pad(pad_ref, x_ref[:, cs], t)
            wg = w_ref[gi].astype(bf16)
            dw = jnp.zeros((LANE, LANE), f32)
            ds = jnp.zeros((1, LANE), f32)
            zeros = jnp.zeros((PAD_ROWS, LANE), f32)
            gpad_ref[0:PAD_ROWS, :] = zeros
            gpad_ref[PAD_ROWS + t:2 * PAD_ROWS + t, :] = zeros
            for r0, rows in _row_chunks(t):
                cnt = _pool_count(r0, rows, win, t)
                pooled = _window_sum(pad_ref, r0, rows, -h, win - h) / cnt - x_ref[r0:r0 + rows, cs]
                dy = dy_ref[r0:r0 + rows, cs]
                ds = ds + jnp.sum(dy * _dotb(pooled, wg), axis=0, keepdims=True)
                dypre = dy * s_ref[:, cs]
                dw = dw + _dotb_tn(pooled, dypre)
                dpooled = _dotb_nt(dypre, wg)
                gpad_ref[PAD_ROWS + r0:PAD_ROWS + r0 + rows, :] = dpooled / cnt
                dpool_ref[r0:r0 + rows, :] = dpooled
            dw_ref[gi] = dw
            ds_ref[:, cs] = ds
            for r0, rows in _row_chunks(t):
                dx = _window_sum(gpad_ref, r0, rows, -h + 1, h + 1) - dpool_ref[r0:r0 + rows, :]
                dp_ref[r0:r0 + rows, cs] = dx.astype(bf16)

    return pl.pallas_call(
        body, out_shape=(SDS((t, 512), bf16), SDS((4, LANE, LANE), f32), SDS((1, 512), f32)), grid=(1,),
        in_specs=[pl.BlockSpec((t, 512), lambda i: (0, 4)), pl.BlockSpec((4, LANE, LANE), lambda i: (0, 0, 0)),
                  pl.BlockSpec((1, 512), lambda i: (0, 0)), pl.BlockSpec((t, 512), lambda i: (0, 1))],
        out_specs=(pl.BlockSpec((t, 512), lambda i: (0, 0)), pl.BlockSpec((4, LANE, LANE), lambda i: (0, 0, 0)),
                   pl.BlockSpec((1, 512), lambda i: (0, 0))),
        scratch_shapes=[pltpu.VMEM((t + 2 * PAD_ROWS, LANE), f32)] * 2 + [pltpu.VMEM((t, LANE), f32)], name=name,
        compiler_params=_cparams(("arbitrary",)),
    )(p, pool_w, pool_scale, dmix)


def gated_rmsnorm(o_a, o_b, p, norm_w, name):
    t = o_a.shape[0]
    tt = _row_tile(t)

    def body(oa_ref, ob_ref, g_ref, nw_ref, y_ref):
        for h in range(GDN_HEADS):
            cs = slice(h * LANE, (h + 1) * LANE)
            o = oa_ref[:, cs] + ob_ref[:, cs]
            r = lax.rsqrt(jnp.mean(o * o, axis=-1, keepdims=True) + RMS_EPS)
            y_ref[:, cs] = (o * r * nw_ref[...] * _silu(g_ref[:, cs])).astype(bf16)

    return pl.pallas_call(
        body, out_shape=SDS((t, 512), bf16), grid=(t // tt,),
        in_specs=[_row_spec(tt, 512), _row_spec(tt, 512), pl.BlockSpec((tt, 512), lambda i: (i, 3)), _vec_spec(LANE)],
        out_specs=_row_spec(tt, 512), name=name, compiler_params=_cparams(("parallel",)),
    )(o_a, o_b, p, norm_w)


def gated_rmsnorm_bwd(o_a, o_b, p, norm_w, dmix, name):
    t = o_a.shape[0]
    tt = _row_tile(t)

    def body(oa_ref, ob_ref, g_ref, nw_ref, dy_ref, do_ref, dg_ref, dnw_ref):
        dnw = jnp.zeros((1, LANE), f32)
        for h in range(GDN_HEADS):
            cs = slice(h * LANE, (h + 1) * LANE)
            o = oa_ref[:, cs] + ob_ref[:, cs]
            r = lax.rsqrt(jnp.mean(o * o, axis=-1, keepdims=True) + RMS_EPS)
            gate = g_ref[:, cs]
            dy = dy_ref[:, cs]
            dy1 = dy * _silu(gate)
            dg_ref[:, cs] = (dy * (o * r * nw_ref[...]) * _dsilu(gate)).astype(bf16)
            dnw = dnw + jnp.sum(dy1 * o * r, axis=0, keepdims=True)
            dn = dy1 * nw_ref[...]
            do_ref[:, cs] = r * dn - o * (r * r * r) * jnp.mean(dn * o, axis=-1, keepdims=True)
        _acc_rows(dnw_ref, dnw)

    return pl.pallas_call(
        body, out_shape=(SDS((t, 512), f32), SDS((t, 512), bf16), SDS((1, LANE), f32)), grid=(t // tt,),
        in_specs=[_row_spec(tt, 512), _row_spec(tt, 512), pl.BlockSpec((tt, 512), lambda i: (i, 3)), _vec_spec(LANE),
                  _row_spec(tt, 512)],
        out_specs=(_row_spec(tt, 512), _row_spec(tt, 512), _vec_spec(LANE)),
        name=name, compiler_params=_cparams(("arbitrary",)),
    )(o_a, o_b, p, norm_w, dmix)


def ln_silu(z, g, b, name):
    t, d = z.shape
    tt = _row_tile(t)

    def body(z_ref, g_ref, b_ref, o_ref):
        xhat, _ = _ln_stats(z_ref[...])
        o_ref[...] = _silu(xhat * g_ref[...] + b_ref[...]).astype(bf16)

    return pl.pallas_call(
        body, out_shape=SDS((t, d), bf16), grid=(t // tt,),
        in_specs=[_row_spec(tt, d), _vec_spec(d), _vec_spec(d)], out_specs=_row_spec(tt, d),
        name=name, compiler_params=_cparams(("parallel",)),
    )(z, g, b)


def ln_silu_bwd(z, g, b, dmix, name):
    t, d = z.shape
    tt = _row_tile(t)

    def body(z_ref, g_ref, b_ref, dy_ref, dz_ref, dg_ref, db_ref):
        xhat, rstd = _ln_stats(z_ref[...])
        dn = dy_ref[...] * _dsilu(xhat * g_ref[...] + b_ref[...])
        dz_ref[...] = _ln_bwd(dn * g_ref[...], xhat, rstd)
        _acc_rows(dg_ref, jnp.sum(dn * xhat, axis=0, keepdims=True))
        _acc_rows(db_ref, jnp.sum(dn, axis=0, keepdims=True))

    return pl.pallas_call(
        body, out_shape=(SDS((t, d), f32), SDS((1, d), f32), SDS((1, d), f32)), grid=(t // tt,),
        in_specs=[_row_spec(tt, d), _vec_spec(d), _vec_spec(d), pl.BlockSpec((tt, d), lambda i: (i, 1))],
        out_specs=(_row_spec(tt, d), _vec_spec(d), _vec_spec(d)),
        name=name, compiler_params=_cparams(("arbitrary",)),
    )(z, g, b, dmix)


def gdn_gates(p, neg_a, dt_bias, name):
    t = p.shape[0]
    tt = _row_tile(t)

    def body(s_ref, na_ref, dt_ref, o_ref):
        s = s_ref[...]
        col = lax.broadcasted_iota(jnp.int32, s.shape, 1)
        o_ref[...] = jnp.where(col < 8, jax.nn.sigmoid(s), na_ref[...] * jax.nn.softplus(s + dt_ref[...]))

    return pl.pallas_call(
        body, out_shape=SDS((t, LANE), f32), grid=(t // tt,),
        in_specs=[pl.BlockSpec((tt, LANE), lambda i: (i, 20)), _vec_spec(LANE), _vec_spec(LANE)],
        out_specs=_row_spec(tt, LANE), name=name, compiler_params=_cparams(("parallel",)),
    )(p, neg_a, dt_bias)


def gdn_gates_bwd(p, neg_a, dt_bias, dbg_a, dbg_b, name):
    t = p.shape[0]
    tt = _row_tile(t)

    def body(s_ref, na_ref, dt_ref, d_ref, d2_ref, ds_ref, da_ref, ddt_ref):
        s = s_ref[...]
        d = d_ref[...] + d2_ref[...]
        col = lax.broadcasted_iota(jnp.int32, s.shape, 1)
        sg = jax.nn.sigmoid(s)
        z = s + dt_ref[...]
        dz = jnp.where((col >= 8) & (col < 16), d * na_ref[...] * jax.nn.sigmoid(z), 0.0)
        ds_ref[...] = jnp.where(col < 8, d * sg * (1.0 - sg), dz).astype(bf16)
        dalog = jnp.where((col >= 8) & (col < 16), d * na_ref[...] * jax.nn.softplus(z), 0.0)
        _acc_rows(da_ref, jnp.sum(dalog, axis=0, keepdims=True))
        _acc_rows(ddt_ref, jnp.sum(dz, axis=0, keepdims=True))

    return pl.pallas_call(
        body, out_shape=(SDS((t, LANE), bf16), SDS((1, LANE), f32), SDS((1, LANE), f32)), grid=(t // tt,),
        in_specs=[pl.BlockSpec((tt, LANE), lambda i: (i, 20)), _vec_spec(LANE), _vec_spec(LANE), _row_spec(tt, LANE),
                  _row_spec(tt, LANE)],
        out_specs=(_row_spec(tt, LANE), _vec_spec(LANE), _vec_spec(LANE)),
        name=name, compiler_params=_cparams(("arbitrary",)),
    )(p, neg_a, dt_bias, dbg_a, dbg_b)


def _order_mask(d, strict):
    r = lax.broadcasted_iota(jnp.int32, (CHUNK, CHUNK), 0)
    c = lax.broadcasted_iota(jnp.int32, (CHUNK, CHUNK), 1)
    if d == 0:
        return r > c if strict else r >= c
    return r < c if strict else r <= c


def _chunk_common(bg, bgt, d):
    gcol = _dotf(_order_mask(d, False).astype(f32), bg)
    grow = _dotf(bgt, _order_mask(1 - d, False).astype(f32))
    return gcol, grow


def _chunk_terms(k, v, beta, gc, gr, d):
    lower, strict = _order_mask(d, False), _order_mask(d, True)
    last = CHUNK - 1 if d == 0 else 0
    e = jnp.exp(gc)
    g_last = gr[:, last:last + 1]
    f = jnp.exp(g_last - gc)
    dm = jnp.exp(jnp.where(lower, gc - gr, -1e30))
    kb = k * beta
    kk = _dotb_nt(kb, k)
    a = jnp.where(strict, kk * dm, 0.0)
    eye = (lax.broadcasted_iota(jnp.int32, (CHUNK, CHUNK), 0) == lax.broadcasted_iota(jnp.int32, (CHUNK, CHUNK), 1)).astype(f32)
    pw = -a
    tinv = eye + pw
    for _ in range(5):
        pw = _dotf(pw, pw)
        tinv = tinv + _dotf(tinv, pw)
    u = _dotb(tinv, v * beta)
    w = _dotb(tinv, kb * e)
    return dict(e=e, f=f, gl=jnp.exp(g_last), dm=dm, kb=kb, kk=kk, tinv=tinv, u=u, w=w, kd=k * f)


def _gdn_specs(nc, width, step_chunk):
    return [pl.BlockSpec((CHUNK, width), functools.partial(lambda i, d: (step_chunk(i, d), 0), d=d)) for d in (0, 1)]


def gdn_forward(q, k, v, bg, bgt, s0, with_out, name):
    t = k.shape[0]
    nc = t // CHUNK

    def body(qf_ref, qb_ref, kf_ref, kb_ref, vf_ref, vb_ref, bgf_ref, bgb_ref, bgtf_ref, bgtb_ref, s0_ref,
             of_ref, ob_ref, sallf_ref, sallb_ref, sfin_ref, s_ref):
        i = pl.program_id(0)

        @pl.when(i == 0)
        def _():
            s_ref[...] = s0_ref[...]

        per_dir = ((qf_ref, kf_ref, vf_ref, bgf_ref, bgtf_ref, of_ref, sallf_ref),
                   (qb_ref, kb_ref, vb_ref, bgb_ref, bgtb_ref, ob_ref, sallb_ref))
        for d, (q_ref, k_ref, v_ref, bg_ref, bgt_ref, o_ref, sall_ref) in enumerate(per_dir):
            bgv = bg_ref[...]
            gcol, grow = _chunk_common(bgv, bgt_ref[0], d)
            lower = _order_mask(d, False)
            for h in range(GDN_HEADS):
                cs = slice(h * LANE, (h + 1) * LANE)
                kh, vh = k_ref[:, cs], v_ref[:, cs]
                col = 8 + 4 * d + h
                c = _chunk_terms(kh, vh, bgv[:, 4 * d + h:4 * d + h + 1], gcol[:, col:col + 1], grow[col:col + 1, :], d)
                s = s_ref[d, h]
                sall_ref[0, h] = s
                vn = c["u"] - _dotb(c["w"], s)
                if with_out:
                    qh = q_ref[:, cs]
                    pm = jnp.where(lower, _dotb_nt(qh, kh) * c["dm"], 0.0)
                    o_ref[:, cs] = _dotb(qh * c["e"], s) + _dotb(pm, vn)
                else:
                    o_ref[:, cs] = jnp.zeros((CHUNK, LANE), f32)
                s_ref[d, h] = c["gl"] * s + _dotb_tn(c["kd"], vn)

        @pl.when(i == nc - 1)
        def _():
            sfin_ref[...] = s_ref[...]

    chunk_of = lambda i, d: i if d == 0 else nc - 1 - i
    seq = _gdn_specs(nc, 512, chunk_of)
    gate = _gdn_specs(nc, LANE, chunk_of)
    gate_t = [pl.BlockSpec((1, 16, CHUNK), functools.partial(lambda i, d: (chunk_of(i, d), 0, 0), d=d)) for d in (0, 1)]
    sall = [pl.BlockSpec((1, GDN_HEADS, LANE, LANE), functools.partial(lambda i, d: (chunk_of(i, d), 0, 0, 0), d=d)) for d in (0, 1)]
    st = pl.BlockSpec((2, GDN_HEADS, LANE, LANE), lambda i: (0, 0, 0, 0))
    o_shape, s_shape = SDS((t, 512), f32), SDS((nc, GDN_HEADS, LANE, LANE), f32)
    return pl.pallas_call(
        body, out_shape=(o_shape, o_shape, s_shape, s_shape, SDS((2, GDN_HEADS, LANE, LANE), f32)), grid=(nc,),
        in_specs=seq + seq + seq + gate + gate_t + [st], out_specs=tuple(seq + sall + [st]),
        scratch_shapes=[pltpu.VMEM((2, GDN_HEADS, LANE, LANE), f32)], name=name,
        compiler_params=_cparams(("arbitrary",)),
    )(q, q, k, k, v, v, bg, bg, bgt, bgt, s0)


def _gdn_head_bwd(qh, kh, vh, doh, beta, gc, gr, s, dsn, d):
    lower, strict = _order_mask(d, False), _order_mask(d, True)
    c = _chunk_terms(kh, vh, beta, gc, gr, d)
    e, f, gl, dm, kb, kk, tinv, u, w, kd = (c[n] for n in ("e", "f", "gl", "dm", "kb", "kk", "tinv", "u", "w", "kd"))
    vn = u - _dotb(w, s)
    ds = gl * dsn
    dgl = jnp.sum(jnp.sum(s * dsn, axis=1, keepdims=True), axis=0, keepdims=True)
    dkd = _dotb_nt(vn, dsn)
    dvn = _dotb(kd, dsn)
    dm_grad = jnp.zeros((CHUNK, CHUNK), f32)
    de = jnp.zeros((CHUNK, 1), f32)
    dq = None
    dk = jnp.zeros((CHUNK, LANE), f32)
    if qh is not None:
        qk = _dotb_nt(qh, kh)
        pm = jnp.where(lower, qk * dm, 0.0)
        dqd = _dotb_nt(doh, s)
        ds = ds + _dotb_tn(qh * e, doh)
        dpm = jnp.where(lower, _dotb_nt(doh, vn), 0.0)
        dvn = dvn + _dotb_tn(pm, doh)
        dqk = dpm * dm
        dm_grad = dm_grad + dpm * qk
        dq = _dotb(dqk, kh) + dqd * e
        dk = _dotb_tn(dqk, qh)
        de = de + jnp.sum(dqd * qh, axis=1, keepdims=True)
    dw = -_dotb_nt(dvn, s)
    ds = ds - _dotb_tn(w, dvn)
    drv = _dotb_tn(tinv, dvn)
    drk = _dotb_tn(tinv, dw)
    da = -jnp.where(strict, _dotb_nt(drv, u) + _dotb_nt(drk, w), 0.0)
    dbeta = jnp.sum(drv * vh, axis=1, keepdims=True)
    dv = drv * beta
    dkb = drk * e
    de = de + jnp.sum(drk * kb, axis=1, keepdims=True)
    dkk = da * dm
    dm_grad = dm_grad + da * kk
    dkb = dkb + _dotb(dkk, kh)
    dk = dk + _dotb_tn(dkk, kb) + dkd * f
    df = jnp.sum(dkd * kh, axis=1, keepdims=True)
    dbeta = dbeta + jnp.sum(dkb * kh, axis=1, keepdims=True)
    dk = dk + dkb * beta
    m = dm_grad * dm
    rsum = jnp.sum(m, axis=1, keepdims=True)
    csum = _dotf(m, jnp.ones((CHUNK, LANE), f32), (((0,), (0,)), ((), ())))[:, 0:1]
    dgl_tot = jnp.sum(df * f, axis=0, keepdims=True) + dgl * gl
    last = CHUNK - 1 if d == 0 else 0
    at_last = (lax.broadcasted_iota(jnp.int32, (CHUNK, 1), 0) == last).astype(f32)
    dgc = de * e - df * f + rsum - csum + at_last * dgl_tot
    return dq, dk, dv, dbeta, dgc, ds


def gdn_backward(q, k, v, bg, bgt, sall_f, sall_b, d_o, ds_fin, with_out, name):
    t = k.shape[0]
    nc = t // CHUNK

    def body(qf_ref, qb_ref, kf_ref, kb_ref, vf_ref, vb_ref, bgf_ref, bgb_ref, bgtf_ref, bgtb_ref,
             sallf_ref, sallb_ref, dof_ref, dob_ref, dsf_ref,
             dqf_ref, dqb_ref, dkf_ref, dkb_ref, dvf_ref, dvb_ref, dbgf_ref, dbgb_ref, ds0_ref, ds_ref):
        i = pl.program_id(0)

        @pl.when(i == 0)
        def _():
            ds_ref[...] = dsf_ref[...]

        lane = lax.broadcasted_iota(jnp.int32, (1, LANE), 1)
        per_dir = ((qf_ref, kf_ref, vf_ref, bgf_ref, bgtf_ref, sallf_ref, dof_ref, dqf_ref, dkf_ref, dvf_ref, dbgf_ref),
                   (qb_ref, kb_ref, vb_ref, bgb_ref, bgtb_ref, sallb_ref, dob_ref, dqb_ref, dkb_ref, dvb_ref, dbgb_ref))
        for d, (q_ref, k_ref, v_ref, bg_ref, bgt_ref, sall_ref, do_ref, dq_ref, dk_ref, dv_ref, dbg_ref) in enumerate(per_dir):
            bgv = bg_ref[...]
            gcol, grow = _chunk_common(bgv, bgt_ref[0], d)
            dbeta_all = jnp.zeros((CHUNK, LANE), f32)
            dgc_all = jnp.zeros((CHUNK, LANE), f32)
            for h in range(GDN_HEADS):
                cs = slice(h * LANE, (h + 1) * LANE)
                col = 8 + 4 * d + h
                dq, dk, dv, dbeta, dgc, ds = _gdn_head_bwd(
                    q_ref[:, cs] if with_out else None, k_ref[:, cs], v_ref[:, cs], do_ref[:, cs],
                    bgv[:, 4 * d + h:4 * d + h + 1], gcol[:, col:col + 1], grow[col:col + 1, :],
                    sall_ref[0, h], ds_ref[d, h], d)
                dq_ref[:, cs] = dq if with_out else jnp.zeros((CHUNK, LANE), f32)
                dk_ref[:, cs] = dk
                dv_ref[:, cs] = dv
                dbeta_all = dbeta_all + dbeta * (lane == 4 * d + h).astype(f32)
                dgc_all = dgc_all + dgc * (lane == col).astype(f32)
                ds_ref[d, h] = ds
            dbg_ref[...] = dbeta_all + _dotf(_order_mask(1 - d, False).astype(f32), dgc_all)

        @pl.when(i == nc - 1)
        def _():
            ds0_ref[...] = ds_ref[...]

    chunk_of = lambda i, d: nc - 1 - i if d == 0 else i
    seq = _gdn_specs(nc, 512, chunk_of)
    gate = _gdn_specs(nc, LANE, chunk_of)
    gate_t = [pl.BlockSpec((1, 16, CHUNK), functools.partial(lambda i, d: (chunk_of(i, d), 0, 0), d=d)) for d in (0, 1)]
    sall = [pl.BlockSpec((1, GDN_HEADS, LANE, LANE), functools.partial(lambda i, d: (chunk_of(i, d), 0, 0, 0), d=d)) for d in (0, 1)]
    st = pl.BlockSpec((2, GDN_HEADS, LANE, LANE), lambda i: (0, 0, 0, 0))
    o_shape, g_shape = SDS((t, 512), f32), SDS((t, LANE), f32)
    return pl.pallas_call(
        body, out_shape=(o_shape,) * 6 + (g_shape, g_shape, SDS((2, GDN_HEADS, LANE, LANE), f32)), grid=(nc,),
        in_specs=seq + seq + seq + gate + gate_t + sall + seq + [st], out_specs=tuple(seq + seq + seq + gate + [st]),
        scratch_shapes=[pltpu.VMEM((2, GDN_HEADS, LANE, LANE), f32)], name=name,
        compiler_params=_cparams(("arbitrary",)),
    )(q, q, k, k, v, v, bg, bg, bgt, bgt, sall_f, sall_b, d_o, d_o, ds_fin)


def _my_position():
    x, y, c = lax.axis_index("x"), lax.axis_index("y"), lax.axis_index("c")
    return x, y, c, 4 * x + 2 * y + c


def exchange(arrays, scatter, name):
    n = len(arrays)
    shapes = [a.shape[1:] if scatter else a.shape for a in arrays]

    def body(*refs):
        ins, outs = refs[:n], refs[n:2 * n]
        send_sems, recv_sems, local_sems = refs[2 * n:]
        x, y, c, me = _my_position()
        started = []
        for a in range(n):
            mine = pltpu.make_async_copy(ins[a].at[me] if scatter else ins[a], outs[a].at[me], local_sems.at[a])
            mine.start()
            started.append(mine)
        waits = []
        for r in range(1, N_DEV):
            px = 1 - x if r & 4 else x
            py = 1 - y if r & 2 else y
            pc = 1 - c if r & 1 else c
            pid = 4 * px + 2 * py + pc
            for a in range(n):
                cp = pltpu.make_async_remote_copy(
                    src_ref=ins[a].at[pid] if scatter else ins[a], dst_ref=outs[a].at[me],
                    send_sem=send_sems.at[a, r - 1], recv_sem=recv_sems.at[a, r - 1],
                    device_id=(px, py, pc), device_id_type=pl.DeviceIdType.MESH)
                cp.start()
                arrive = pltpu.make_async_remote_copy(
                    src_ref=ins[a].at[pid] if scatter else ins[a], dst_ref=outs[a].at[pid],
                    send_sem=send_sems.at[a, r - 1], recv_sem=recv_sems.at[a, r - 1],
                    device_id=(px, py, pc), device_id_type=pl.DeviceIdType.MESH)
                waits.append((cp, arrive))
        for cp, arrive in waits:
            cp.wait_send()
            arrive.wait_recv()
        for mine in started:
            mine.wait()

    any_spec = pl.BlockSpec(memory_space=pl.ANY)
    return pl.pallas_call(
        body, out_shape=tuple(SDS((N_DEV,) + tuple(s), a.dtype) for s, a in zip(shapes, arrays)),
        in_specs=[any_spec] * n, out_specs=tuple([any_spec] * n),
        scratch_shapes=[pltpu.SemaphoreType.DMA((n, N_DEV - 1)), pltpu.SemaphoreType.DMA((n, N_DEV - 1)),
                        pltpu.SemaphoreType.DMA((n,))],
        name=name,
    )(*arrays)


def ada_forward(a_raw, ada_w, ada_b_loc, name):
    def body(a_ref, w_ref, b_ref, o_ref):
        a = _silu(a_ref[...])
        for l in range(DEPTH):
            o_ref[l] = _dotf(a, w_ref[l]) + b_ref[l]

    return pl.pallas_call(body, out_shape=SDS((DEPTH, 16, ada_w.shape[2]), f32), name=name,
                          compiler_params=_cparams())(a_raw, ada_w, ada_b_loc)


def ada_backward(a_raw, ada_w, dm, name):
    def body(a_ref, w_ref, dm_ref, gw_ref, dcc_ref):
        a = _silu(a_ref[...])
        for l in range(DEPTH):
            gw_ref[l] = _dotf(a, dm_ref[l], (((0,), (0,)), ((), ())))
        dcc_ref[...] = _dotf(dm_ref[0, 8:16, :], w_ref[0], (((1,), (1,)), ((), ())))

    return pl.pallas_call(body, out_shape=(SDS(ada_w.shape, f32), SDS((8, ada_w.shape[1]), f32)), name=name,
                          compiler_params=_cparams())(a_raw, ada_w, dm)


def sum_parts(parts, name):
    _, r, c = parts.shape

    def body(p_ref, o_ref):
        acc = p_ref[0]
        for i in range(1, N_DEV):
            acc = acc + p_ref[i]
        o_ref[...] = acc

    return pl.pallas_call(body, out_shape=SDS((r, c), f32), name=name, compiler_params=_cparams())(parts)


def cctx_grad(parts, c_ctx, name):
    def body(p_ref, c_ref, o_ref):
        acc = p_ref[0, 0:1, :]
        for i in range(1, N_DEV):
            acc = acc + p_ref[i, 0:1, :]
        o_ref[...] = acc * _dsilu(c_ref[...])

    return pl.pallas_call(body, out_shape=SDS((1, c_ctx.shape[1]), f32), name=name, compiler_params=_cparams())(parts, c_ctx)


def _adamw_math(g, w, m, v):
    m = ADAM_B1 * m + (1.0 - ADAM_B1) * g
    v = ADAM_B2 * v + (1.0 - ADAM_B2) * (g * g)
    m_hat = m / (1.0 - ADAM_B1 ** ADAM_STEP)
    v_hat = v / (1.0 - ADAM_B2 ** ADAM_STEP)
    delta = -ADAM_LR * (m_hat / (jnp.sqrt(v_hat) + ADAM_EPS) + ADAM_WD * w)
    return delta, m, v


def adamw(parts, w, m, v, name):
    n, r, c = parts.shape
    tr = _pick(r, (256, 128, 64, 32, 16, 8))

    def body(p_ref, w_ref, m_ref, v_ref, g_ref, d_ref, nm_ref, nv_ref):
        g = p_ref[0].astype(f32)
        for i in range(1, n):
            g = g + p_ref[i].astype(f32)
        g_ref[...] = g
        d_ref[...], nm_ref[...], nv_ref[...] = _adamw_math(g, w_ref[...], m_ref[...], v_ref[...])

    blk = pl.BlockSpec((tr, c), lambda i: (i, 0))
    out = SDS((r, c), f32)
    return pl.pallas_call(
        body, out_shape=(out, out, out, out), grid=(r // tr,),
        in_specs=[pl.BlockSpec((n, tr, c), lambda i: (0, i, 0)), blk, blk, blk], out_specs=(blk, blk, blk, blk),
        name=name, compiler_params=_cparams(("parallel",)),
    )(parts, w, m, v)


def adamw_small(items, name):
    n = len(items)

    def body(*refs):
        ins, outs = refs[:4 * n], refs[4 * n:]
        for i in range(n):
            g, w, m, v = (ins[4 * i + j][...] for j in range(4))
            outs[3 * i][...], outs[3 * i + 1][...], outs[3 * i + 2][...] = _adamw_math(g, w, m, v)

    flat = [a for it in items for a in it]
    out_shape = tuple(SDS(it[1].shape, f32) for it in items for _ in range(3))
    res = pl.pallas_call(body, out_shape=out_shape, name=name, compiler_params=_cparams())(*flat)
    return [tuple(res[3 * i:3 * i + 3]) for i in range(n)]


def _unshard(g, axis):
    loc = g.shape[1:]
    return jnp.moveaxis(g, 0, axis).reshape(loc[:axis] + (N_DEV * loc[axis],) + loc[axis + 1:])


def _shard_major(full, axis):
    s = full.shape
    return jnp.moveaxis(full.reshape(s[:axis] + (N_DEV, s[axis] // N_DEV) + s[axis + 1:]), axis, 0)


def _my_block(full, axis, me):
    n = full.shape[axis] // N_DEV
    return lax.dynamic_slice_in_dim(full, me * n, n, axis)


def _pack(arrays):
    flat = [a.reshape(-1) for a in arrays]
    sizes = [f.shape[0] for f in flat]
    total = sum(sizes)
    padded = -(-total // (8 * LANE)) * (8 * LANE)
    flat.append(jnp.zeros((padded - total,), f32))
    offs = [sum(sizes[:i]) for i in range(len(sizes))]
    return jnp.concatenate(flat).reshape(padded // LANE, LANE), offs


def _pad_rows(w, n):
    return jnp.concatenate([w, jnp.zeros((n - w.shape[0],) + w.shape[1:], w.dtype)], 0)


def _gate_rows(bg):
    return bg[:, :16].reshape(bg.shape[0] // CHUNK, CHUNK, 16).transpose(0, 2, 1)


def _rows(vec, n):
    m = vec.reshape(n, 1, -1)
    return [m[i] for i in range(n)]


def kernel(x, c, ctx, c_ctx, ada_w, ada_b, ln_g, ln_b, even_w_in, even_w_out, gdn_conv_w, gdn_a_log, gdn_dt_bias, gdn_norm_w, pool_w, pool_scale, odd_w_in, odd_w_out, sconv_w, conf_conv_w, conf_ln_g, conf_ln_b, ffn_w_up, ffn_conv_w, ffn_w_down, loss_target, m_c_ctx, m_ada_w, m_ada_b, m_ln_g, m_ln_b, m_even_w_in, m_even_w_out, m_gdn_conv_w, m_gdn_a_log, m_gdn_dt_bias, m_gdn_norm_w, m_pool_w, m_pool_scale, m_odd_w_in, m_odd_w_out, m_sconv_w, m_conf_conv_w, m_conf_ln_g, m_conf_ln_b, m_ffn_w_up, m_ffn_conv_w, m_ffn_w_down, v_c_ctx, v_ada_w, v_ada_b, v_ln_g, v_ln_b, v_even_w_in, v_even_w_out, v_gdn_conv_w, v_gdn_a_log, v_gdn_dt_bias, v_gdn_norm_w, v_pool_w, v_pool_scale, v_odd_w_in, v_odd_w_out, v_sconv_w, v_conf_conv_w, v_conf_ln_g, v_conf_ln_b, v_ffn_w_up, v_ffn_conv_w, v_ffn_w_down):
    weights = dict(c_ctx=c_ctx, ada_w=ada_w, ada_b=ada_b, ln_g=ln_g, ln_b=ln_b, even_w_in=even_w_in, even_w_out=even_w_out, gdn_conv_w=gdn_conv_w, gdn_a_log=gdn_a_log, gdn_dt_bias=gdn_dt_bias, gdn_norm_w=gdn_norm_w, pool_w=pool_w, pool_scale=pool_scale, odd_w_in=odd_w_in, odd_w_out=odd_w_out, sconv_w=sconv_w, conf_conv_w=conf_conv_w, conf_ln_g=conf_ln_g, conf_ln_b=conf_ln_b, ffn_w_up=ffn_w_up, ffn_conv_w=ffn_conv_w, ffn_w_down=ffn_w_down)
    mom1 = dict(c_ctx=m_c_ctx, ada_w=m_ada_w, ada_b=m_ada_b, ln_g=m_ln_g, ln_b=m_ln_b, even_w_in=m_even_w_in, even_w_out=m_even_w_out, gdn_conv_w=m_gdn_conv_w, gdn_a_log=m_gdn_a_log, gdn_dt_bias=m_gdn_dt_bias, gdn_norm_w=m_gdn_norm_w, pool_w=m_pool_w, pool_scale=m_pool_scale, odd_w_in=m_odd_w_in, odd_w_out=m_odd_w_out, sconv_w=m_sconv_w, conf_conv_w=m_conf_conv_w, conf_ln_g=m_conf_ln_g, conf_ln_b=m_conf_ln_b, ffn_w_up=m_ffn_w_up, ffn_conv_w=m_ffn_conv_w, ffn_w_down=m_ffn_w_down)
    mom2 = dict(c_ctx=v_c_ctx, ada_w=v_ada_w, ada_b=v_ada_b, ln_g=v_ln_g, ln_b=v_ln_b, even_w_in=v_even_w_in, even_w_out=v_even_w_out, gdn_conv_w=v_gdn_conv_w, gdn_a_log=v_gdn_a_log, gdn_dt_bias=v_gdn_dt_bias, gdn_norm_w=v_gdn_norm_w, pool_w=v_pool_w, pool_scale=v_pool_scale, odd_w_in=v_odd_w_in, odd_w_out=v_odd_w_out, sconv_w=v_sconv_w, conf_conv_w=v_conf_conv_w, conf_ln_g=v_conf_ln_g, conf_ln_b=v_conf_ln_b, ffn_w_up=v_ffn_w_up, ffn_conv_w=v_ffn_conv_w, ffn_w_down=v_ffn_w_down)
    order = list(weights)
    me = 4 * lax.axis_index("x") + 2 * lax.axis_index("y") + lax.axis_index("c")
    x, ctx, target = x[0], ctx[0], loss_target[0]
    t, d = x.shape
    tc = ctx.shape[0]

    small_in = [ln_g, ln_b, gdn_conv_w, sconv_w, conf_conv_w, ffn_conv_w, c]
    small_axes = [2, 2, 1, 1, 1, 3, 0]
    small_pack, small_offs = _pack(small_in)
    wire = [w.astype(bf16) for w in (even_w_in, even_w_out, odd_w_in, odd_w_out, ffn_w_up, ffn_w_down)]
    gath = exchange(wire + [small_pack], False, "gather_weights")
    e_in = even_w_in.shape[1] * N_DEV
    e_pad = -(-e_in // LANE) * LANE
    win_e = jnp.pad(_unshard(gath[0], 1), ((0, 0), (0, e_pad - e_in)))
    wout_e = _unshard(gath[1], 0)
    win_o = _unshard(gath[2], 1)
    wout_o = _unshard(gath[3], 0)
    wup = _unshard(gath[4], 2)
    wdown = _unshard(gath[5], 1)
    sm = gath[6].reshape(N_DEV, -1)
    lng_f, lnb_f, gconv_f, sconv_f, cconv_f, fconv_f, c_all = [
        _unshard(sm[:, o:o + a.size].reshape((N_DEV,) + a.shape), ax) for a, o, ax in zip(small_in, small_offs, small_axes)]
    gw8 = _pad_rows(gconv_f, 8)
    sw8 = _pad_rows(sconv_f, 8)
    cw32 = _pad_rows(cconv_f, 32)
    fw16 = [_pad_rows(fconv_f[l].reshape(9, D_FF), 16) for l in range(DEPTH)]

    a_raw = jnp.concatenate([c_all, c_ctx[None], jnp.zeros((7, d), f32)], 0)
    ncol = ada_w.shape[2]
    ada_b_loc = lax.dynamic_slice_in_dim(ada_b, me * ncol, ncol, 1)[:, None, :]
    modpart = ada_forward(a_raw, ada_w, ada_b_loc, "ada_forward")
    mod_send = jnp.stack([jnp.transpose(modpart[:, :N_DEV], (1, 0, 2)),
                          jnp.broadcast_to(modpart[:, N_DEV][None], (N_DEV, DEPTH, ncol))], axis=2)
    mod_recv = exchange([mod_send], True, "scatter_mod")[0]
    mod = jnp.transpose(mod_recv[:, :, 0, :], (1, 0, 2)).reshape(DEPTH, 6 * d)
    modc = mod_recv[:, 0, 1, :].reshape(6 * d)
    sh_c, sc_c = modc[None, :d], modc[None, d:2 * d]
    mods = [_rows(mod[l], 6) for l in range(DEPTH)]
    lng = [[lng_f[l, j][None] for j in range(2)] for l in range(DEPTH)]
    lnb = [[lnb_f[l, j][None] for j in range(2)] for l in range(DEPTH)]

    neg_a = jnp.zeros((1, LANE), f32).at[0, 8:16].set(-jnp.exp(gdn_a_log).reshape(8))
    dt_row = jnp.zeros((1, LANE), f32).at[0, 8:16].set(gdn_dt_bias.reshape(8))
    nw_row, ps_row = gdn_norm_w[None], pool_scale[None]
    cg_row, cb_row = conf_ln_g[None], conf_ln_b[None]
    q_scale = GDN_DK ** -0.5

    sh_m, sc_m, gt_m, sh_f, sc_f, gt_f = mods[0]
    u0 = modulate(x, sc_m, sh_m, "mod_l0_mix")
    cu = modulate(ctx, sc_c, sh_c, "mod_ctx")
    p0 = matmul(u0, win_e, "nn", f32, "even_in")
    pc = matmul(cu, win_e, "nn", f32, "even_in_ctx")
    qn = gdn_conv(p0, gw8, 0, 4, q_scale, "gdn_conv_q")
    kn = gdn_conv(p0, gw8, 4, 4, 1.0, "gdn_conv_k")
    vv = gdn_conv(p0, gw8, 8, 4, None, "gdn_conv_v")
    kc = gdn_conv(pc, gw8, 4, 4, 1.0, "gdn_conv_k_ctx")
    vc = gdn_conv(pc, gw8, 8, 4, None, "gdn_conv_v_ctx")
    bg = gdn_gates(p0, neg_a, dt_row, "gdn_gates")
    bgc = gdn_gates(pc, neg_a, dt_row, "gdn_gates_ctx")
    bgt, bgtc = _gate_rows(bg), _gate_rows(bgc)
    zero_state = jnp.zeros((2, GDN_HEADS, LANE, LANE), f32)
    _, _, sallc_f, sallc_b, sfin_c = gdn_forward(kc, kc, vc, bgc, bgtc, zero_state, False, "gdn_fwd_ctx")
    o_f, o_b, sall_f, sall_b, _ = gdn_forward(qn, kn, vv, bg, bgt, sfin_c, True, "gdn_fwd")
    mix0 = jnp.concatenate([gated_rmsnorm(o_f, o_b, p0, nw_row, "gated_rmsnorm"),
                            pool_mix(p0, pool_w, ps_row, "pool_mix")], 1)
    y0 = matmul(mix0, wout_e, "nn", f32, "even_out")
    x1 = res_layernorm(x, y0, gt_m, lng[0][0], lnb[0][0], "resln_l0_mix")
    u1 = modulate(x1, sc_f, sh_f, "mod_l0_ffn")
    h0 = matmul(u1, wup[0], "nn", f32, "ffn_up_l0")
    f0 = ffn_conv(h0, fw16[0], "ffn_conv_l0")
    y0f = matmul(f0, wdown[0], "nn", f32, "ffn_down_l0")
    x2 = res_layernorm(x1, y0f, gt_f, lng[0][1], lnb[0][1], "resln_l0_ffn")

    sh_m1, sc_m1, gt_m1, sh_f1, sc_f1, gt_f1 = mods[1]
    u2 = modulate(x2, sc_m1, sh_m1, "mod_l1_mix")
    p1 = matmul(u2, win_o, "nn", f32, "odd_in")
    zc = conf_conv(p1, cw32, "conf_conv")
    mix1 = jnp.concatenate([short_conv(p1, sw8, "short_conv"), ln_silu(zc, cg_row, cb_row, "conf_ln_silu")], 1)
    y1 = matmul(mix1, wout_o, "nn", f32, "odd_out")
    x3 = res_layernorm(x2, y1, gt_m1, lng[1][0], lnb[1][0], "resln_l1_mix")
    u3 = modulate(x3, sc_f1, sh_f1, "mod_l1_ffn")
    h1 = matmul(u3, wup[1], "nn", f32, "ffn_up_l1")
    f1 = ffn_conv(h1, fw16[1], "ffn_conv_l1")
    y1f = matmul(f1, wdown[1], "nn", f32, "ffn_down_l1")
    x4 = res_layernorm(x3, y1f, gt_f1, lng[1][1], lnb[1][1], "resln_l1_ffn")

    loss_row, dx4 = loss_head(x4, target, "loss_head")
    loss = lax.psum(loss_row[0, 0], ("x", "y", "c"))

    def ffn_backward(dout, x_in, y, gate, g_row, scale, u, h, f, l):
        dxr, dy, dgt, dlg, dlb = res_layernorm_bwd(dout, x_in, y, gate, g_row, f"resln_bwd_l{l}_ffn")
        df = matmul(dy, wdown[l], "nt", f32, f"ffn_down_dgrad_l{l}")
        g_down = matmul(f, dy, "tn", bf16, f"ffn_down_wgrad_l{l}")
        dh, dcw = ffn_conv_bwd(h, fw16[l], df, f"ffn_conv_bwd_l{l}")
        du = matmul(dh, wup[l], "nt", f32, f"ffn_up_dgrad_l{l}")
        g_up = matmul(u, dh, "tn", bf16, f"ffn_up_wgrad_l{l}")
        dx_in, dsc, dsh = modulate_bwd(du, x_in, scale, dxr, f"mod_bwd_l{l}_ffn")
        return dx_in, (dsh, dsc, dgt), (dlg, dlb), dcw, g_up, g_down

    dx3, dmod_f1, dln_f1, dfcw1, g_up1, g_down1 = ffn_backward(dx4, x3, y1f, gt_f1, lng[1][1], sc_f1, u3, h1, f1, 1)

    dxr, dy, dgt, dlg, dlb = res_layernorm_bwd(dx3, x2, y1, gt_m1, lng[1][0], "resln_bwd_l1_mix")
    dln_m1 = (dlg, dlb)
    dmix = matmul(dy, wout_o, "nt", f32, "odd_out_dgrad")
    g_wout_o = matmul(mix1, dy, "tn", bf16, "odd_out_wgrad")
    dgb, dgc, dhh, d_sconv = short_conv_bwd(p1, sw8, dmix, "short_conv_bwd")
    dzc, d_cg, d_cb = ln_silu_bwd(zc, cg_row, cb_row, dmix, "conf_ln_silu_bwd")
    dga, dgbb, d_cconv = conf_conv_bwd(p1, cw32, dzc, "conf_conv_bwd")
    dp1 = jnp.concatenate([dgb, dgc, dhh, dga, dgbb], 1)
    du = matmul(dp1, win_o, "nt", f32, "odd_in_dgrad")
    g_win_o = matmul(u2, dp1, "tn", bf16, "odd_in_wgrad")
    dx2, dsc, dsh = modulate_bwd(du, x2, sc_m1, dxr, "mod_bwd_l1_mix")
    dmod_m1 = (dsh, dsc, dgt)

    dx1, dmod_f0, dln_f0, dfcw0, g_up0, g_down0 = ffn_backward(dx2, x1, y0f, gt_f, lng[0][1], sc_f, u1, h0, f0, 0)

    dxr, dy, dgt, dlg, dlb = res_layernorm_bwd(dx1, x, y0, gt_m, lng[0][0], "resln_bwd_l0_mix")
    dln_m0 = (dlg, dlb)
    dmix = matmul(dy, wout_e, "nt", f32, "even_out_dgrad")
    g_wout_e = matmul(mix0, dy, "tn", bf16, "even_out_wgrad")
    d_o, dgate, d_nw = gated_rmsnorm_bwd(o_f, o_b, p0, nw_row, dmix, "gated_rmsnorm_bwd")
    dpool, d_pw, d_ps = pool_mix_bwd(p0, pool_w, ps_row, dmix, "pool_mix_bwd")
    dq_f, dq_b, dk_f, dk_b, dv_f, dv_b, dbg_f, dbg_b, ds0 = gdn_backward(
        qn, kn, vv, bg, bgt, sall_f, sall_b, d_o, zero_state, True, "gdn_bwd")
    _, _, dkc_f, dkc_b, dvc_f, dvc_b, dbgc_f, dbgc_b, _ = gdn_backward(
        kc, kc, vc, bgc, bgtc, sallc_f, sallc_b, jnp.zeros((tc, 512), f32), ds0, False, "gdn_bwd_ctx")
    dqp, dwq = gdn_conv_bwd(p0, gw8, dq_f, dq_b, 0, 4, q_scale, "gdn_conv_q_bwd")
    dkp, dwk = gdn_conv_bwd(p0, gw8, dk_f, dk_b, 4, 4, 1.0, "gdn_conv_k_bwd")
    dvp, dwv = gdn_conv_bwd(p0, gw8, dv_f, dv_b, 8, 4, None, "gdn_conv_v_bwd")
    dkcp, dwkc = gdn_conv_bwd(pc, gw8, dkc_f, dkc_b, 4, 4, 1.0, "gdn_conv_k_ctx_bwd")
    dvcp, dwvc = gdn_conv_bwd(pc, gw8, dvc_f, dvc_b, 8, 4, None, "gdn_conv_v_ctx_bwd")
    ds_l, da_l, ddt_l = gdn_gates_bwd(p0, neg_a, dt_row, dbg_f, dbg_b, "gdn_gates_bwd")
    ds_c, da_c, ddt_c = gdn_gates_bwd(pc, neg_a, dt_row, dbgc_f, dbgc_b, "gdn_gates_ctx_bwd")
    zc512 = jnp.zeros((tc, 512), bf16)
    dp_all = jnp.concatenate([
        jnp.concatenate([dqp, dkp, dvp, dgate, dpool, ds_l], 1),
        jnp.concatenate([zc512, dkcp, dvcp, zc512, zc512, ds_c], 1)], 0)
    u_all = jnp.concatenate([u0, cu], 0)
    du_all = matmul(dp_all, win_e, "nt", f32, "even_in_dgrad")
    g_win_e = matmul(u_all, dp_all, "tn", bf16, "even_in_wgrad")[:, :e_in]
    grad_x, dsc, dsh = modulate_bwd(du_all, x, sc_m, dxr, "mod_bwd_l0_mix")
    dmod_m0 = (dsh, dsc, dgt)
    _, dsc_c, dsh_c = modulate_bwd(du_all, ctx, sc_c, jnp.zeros((tc, d), f32), "mod_bwd_ctx", du_row0=t)

    dmod0 = jnp.concatenate(dmod_m0 + dmod_f0, 1)
    dmod1 = jnp.concatenate(dmod_m1 + dmod_f1, 1)
    dmodc = jnp.concatenate([dsh_c, dsc_c], 1)
    d_gconv = jnp.concatenate([dwq, dwk + dwkc, dwv + dwvc], 1)[:5]
    small_g = [dmod0, dmod1, dmodc,
               jnp.concatenate([dln_m0[0], dln_f0[0], dln_m1[0], dln_f1[0]], 0),
               jnp.concatenate([dln_m0[1], dln_f0[1], dln_m1[1], dln_f1[1]], 0),
               d_gconv, (da_l + da_c)[0, 8:16], (ddt_l + ddt_c)[0, 8:16], d_nw, d_pw, d_ps,
               d_sconv[:3], d_cconv[:31], d_cg, d_cb, jnp.stack([dfcw0[:9], dfcw1[:9]])]
    gpack, goffs = _pack(small_g)
    gparts = exchange([gpack], False, "gather_small_grads")[0]
    gsum = sum_parts(gparts, "sum_small_grads").reshape(-1)
    gs = [gsum[o:o + a.size].reshape(a.shape) for a, o in zip(small_g, goffs)]
    gflat = gparts.reshape(N_DEV, -1)
    dmodc_cols = _my_block(jnp.pad(gs[2], ((0, 0), (0, 4 * d))), 1, me)
    dm = jnp.stack([
        jnp.concatenate([_my_block(gflat[:, goffs[0]:goffs[0] + 6 * d], 1, me), dmodc_cols, jnp.zeros((7, ncol), f32)], 0),
        jnp.concatenate([_my_block(gflat[:, goffs[1]:goffs[1] + 6 * d], 1, me), jnp.zeros((8, ncol), f32)], 0)])
    g_ada_w, dcc = ada_backward(a_raw, ada_w, dm, "ada_backward")
    g_cctx = cctx_grad(exchange([dcc], False, "gather_cctx")[0], c_ctx[None], "cctx_grad")

    grads = {}
    grads["c_ctx"] = g_cctx.reshape(c_ctx.shape)
    grads["ada_b"] = jnp.concatenate([gs[0] + jnp.pad(gs[2], ((0, 0), (0, 4 * d))), gs[1]], 0)
    grads["ln_g"] = _my_block(gs[3].reshape(DEPTH, 2, d), 2, me)
    grads["ln_b"] = _my_block(gs[4].reshape(DEPTH, 2, d), 2, me)
    grads["gdn_conv_w"] = _my_block(gs[5], 1, me)
    grads["gdn_a_log"] = gs[6].reshape(2, GDN_HEADS)
    grads["gdn_dt_bias"] = gs[7].reshape(2, GDN_HEADS)
    grads["gdn_norm_w"] = gs[8].reshape(LANE)
    grads["pool_w"] = gs[9]
    grads["pool_scale"] = gs[10].reshape(-1)
    grads["sconv_w"] = _my_block(gs[11], 1, me)
    grads["conf_conv_w"] = _my_block(gs[12], 1, me)
    grads["conf_ln_g"] = gs[13].reshape(-1)
    grads["conf_ln_b"] = gs[14].reshape(-1)
    grads["ffn_conv_w"] = _my_block(gs[15].reshape(DEPTH, 3, 3, D_FF), 3, me)

    def as2d(a):
        return a.reshape(-1, a.shape[-1]) if a.ndim > 1 else a.reshape(1, -1)

    small_names = [n for n in order if n in grads]
    res = adamw_small([(as2d(grads[n]), as2d(weights[n]), as2d(mom1[n]), as2d(mom2[n])) for n in small_names], "adamw_small")
    delta, new_m, new_v = {}, {}, {}
    for n, (dl, nm, nv) in zip(small_names, res):
        delta[n], new_m[n], new_v[n] = (a.reshape(weights[n].shape) for a in (dl, nm, nv))

    big = [("even_w_in", g_win_e, 1), ("even_w_out", g_wout_e, 0), ("odd_w_in", g_win_o, 1), ("odd_w_out", g_wout_o, 0),
           ("ffn_w_up", jnp.stack([g_up0, g_up1]), 2), ("ffn_w_down", jnp.stack([g_down0, g_down1]), 1)]
    recv = exchange([_shard_major(g, ax) for _, g, ax in big], True, "scatter_grads")
    big_parts = {n: r for (n, _, _), r in zip(big, recv)}
    big_parts["ada_w"] = g_ada_w[None]
    for n, parts in big_parts.items():
        w = weights[n]
        cols = w.shape[-1]
        out = adamw(parts.reshape(parts.shape[0], -1, cols), w.reshape(-1, cols), mom1[n].reshape(-1, cols),
                    mom2[n].reshape(-1, cols), f"adamw_{n}")
        grads[n], delta[n], new_m[n], new_v[n] = (a.reshape(w.shape) for a in out)

    return (loss, grad_x[None], *[grads[n] for n in order], *[delta[n] for n in order],
            *[new_m[n] for n in order], *[new_v[n] for n in order])
```

```python
import functools
import math

import jax
import jax.numpy as jnp
from jax import lax
from jax.experimental import pallas as pl
from jax.experimental.pallas import tpu as pltpu

f32 = jnp.float32
bf16 = jnp.bfloat16
SDS = jax.ShapeDtypeStruct

N_DEV = 8
D_MODEL = 1024
DEPTH = 2
GRID_W = 64
GDN_HEADS = 4
GDN_DK = 128
CHUNK = 64
POOL_WINDOWS = (2, 4, 8, 16)
D_FF = 2816
ALPHA = (2 * DEPTH) ** 0.25
LN_EPS = 1e-5
RMS_EPS = 1e-6
LANE = 128
PAD_ROWS = 72
CONV_ROWS = 256
VMEM_LIMIT = 56 * 2**20

ADAM_LR, ADAM_B1, ADAM_B2, ADAM_EPS, ADAM_WD, ADAM_STEP = 0.001, 0.9, 0.999, 1e-08, 0.01, 10

HI = lax.Precision.HIGHEST


def _cparams(sem=None):
    return pltpu.CompilerParams(dimension_semantics=sem, vmem_limit_bytes=VMEM_LIMIT)


def _silu(x):
    return x * jax.nn.sigmoid(x)


def _dsilu(x):
    s = jax.nn.sigmoid(x)
    return s * (1.0 + x * (1.0 - s))


def _dotb(a, b, dims=(((1,), (0,)), ((), ()))):
    return lax.dot_general(a.astype(bf16), b.astype(bf16), dims, preferred_element_type=f32)


def _dotb_nt(a, b):
    return _dotb(a, b, (((1,), (1,)), ((), ())))


def _dotb_tn(a, b):
    return _dotb(a, b, (((0,), (0,)), ((), ())))


def _dotf(a, b, dims=(((1,), (0,)), ((), ()))):
    return lax.dot_general(a, b, dims, preferred_element_type=f32, precision=HI)


def _pick(n, cands):
    for c in cands:
        if n % c == 0:
            return c
    return n


def matmul(a, b, mode, out_dtype, name):
    if mode == "nn":
        (M, K), N = a.shape, b.shape[1]
    elif mode == "nt":
        (M, K), N = a.shape, b.shape[0]
    else:
        (K, M), N = a.shape, b.shape[1]
    tm = _pick(M, (1024, 768, 512, 256, 128)) if mode != "tn" else _pick(M, (1024, 512, 256, 128))
    tn = _pick(N, (1024, 896, 768, 640, 512, 384, 256, 128))
    tk = _pick(K, (1024, 896, 768, 640, 512, 384, 256, 128)) if mode != "tn" else _pick(K, (1024, 512, 256))
    nk = K // tk
    dims = {"nn": (((1,), (0,)), ((), ())), "nt": (((1,), (1,)), ((), ())), "tn": (((0,), (0,)), ((), ()))}[mode]

    def body(a_ref, b_ref, o_ref, acc_ref):
        k = pl.program_id(2)
        part = lax.dot_general(a_ref[...].astype(bf16), b_ref[...].astype(bf16), dims, preferred_element_type=f32)

        @pl.when(k == 0)
        def _():
            acc_ref[...] = part

        @pl.when(k > 0)
        def _():
            acc_ref[...] += part

        @pl.when(k == nk - 1)
        def _():
            o_ref[...] = acc_ref[...].astype(out_dtype)

    a_spec = {"nn": pl.BlockSpec((tm, tk), lambda i, j, k: (i, k)),
              "nt": pl.BlockSpec((tm, tk), lambda i, j, k: (i, k)),
              "tn": pl.BlockSpec((tk, tm), lambda i, j, k: (k, i))}[mode]
    b_spec = {"nn": pl.BlockSpec((tk, tn), lambda i, j, k: (k, j)),
              "nt": pl.BlockSpec((tn, tk), lambda i, j, k: (j, k)),
              "tn": pl.BlockSpec((tk, tn), lambda i, j, k: (k, j))}[mode]
    return pl.pallas_call(
        body, out_shape=SDS((M, N), out_dtype), grid=(M // tm, N // tn, nk),
        in_specs=[a_spec, b_spec], out_specs=pl.BlockSpec((tm, tn), lambda i, j, k: (i, j)),
        scratch_shapes=[pltpu.VMEM((tm, tn), f32)], name=name,
        compiler_params=_cparams(("parallel", "parallel", "arbitrary")),
    )(a, b)


def _row_tile(t):
    return _pick(t, (512, 256, 128, 64, 32, 16, 8))


def _row_spec(tt, d):
    return pl.BlockSpec((tt, d), lambda i: (i, 0))


def _vec_spec(d):
    return pl.BlockSpec((1, d), lambda i: (0, 0))


def _acc_rows(ref, val):
    @pl.when(pl.program_id(0) == 0)
    def _():
        ref[...] = val

    @pl.when(pl.program_id(0) > 0)
    def _():
        ref[...] += val


def modulate(x, scale, shift, name):
    t, d = x.shape
    tt = _row_tile(t)

    def body(x_ref, sc_ref, sh_ref, o_ref):
        o_ref[...] = (x_ref[...] * (1.0 + sc_ref[...]) + sh_ref[...]).astype(bf16)

    return pl.pallas_call(
        body, out_shape=SDS((t, d), bf16), grid=(t // tt,),
        in_specs=[_row_spec(tt, d), _vec_spec(d), _vec_spec(d)], out_specs=_row_spec(tt, d),
        name=name, compiler_params=_cparams(("parallel",)),
    )(x, scale, shift)


def modulate_bwd(du, x, scale, dres, name, du_row0=0):
    t, d = x.shape
    tt = _row_tile(t)
    blk0 = du_row0 // tt

    def body(du_ref, x_ref, sc_ref, dres_ref, dx_ref, dsc_ref, dsh_ref):
        du_v = du_ref[...]
        dx_ref[...] = du_v * (1.0 + sc_ref[...]) + dres_ref[...]
        _acc_rows(dsc_ref, jnp.sum(du_v * x_ref[...], axis=0, keepdims=True))
        _acc_rows(dsh_ref, jnp.sum(du_v, axis=0, keepdims=True))

    return pl.pallas_call(
        body, out_shape=(SDS((t, d), f32), SDS((1, d), f32), SDS((1, d), f32)), grid=(t // tt,),
        in_specs=[pl.BlockSpec((tt, d), lambda i: (i + blk0, 0)), _row_spec(tt, d), _vec_spec(d), _row_spec(tt, d)],
        out_specs=(_row_spec(tt, d), _vec_spec(d), _vec_spec(d)),
        name=name, compiler_params=_cparams(("arbitrary",)),
    )(du, x, scale, dres)


def _ln_stats(z):
    mu = jnp.mean(z, axis=-1, keepdims=True)
    zc = z - mu
    var = jnp.mean(zc * zc, axis=-1, keepdims=True)
    rstd = lax.rsqrt(var + LN_EPS)
    return zc * rstd, rstd


def _ln_bwd(dxhat, xhat, rstd):
    m1 = jnp.mean(dxhat, axis=-1, keepdims=True)
    m2 = jnp.mean(dxhat * xhat, axis=-1, keepdims=True)
    return rstd * (dxhat - m1 - xhat * m2)


def res_layernorm(x, y, gate, g, b, name):
    t, d = x.shape
    tt = _row_tile(t)

    def body(x_ref, y_ref, gt_ref, g_ref, b_ref, o_ref):
        xhat, _ = _ln_stats(ALPHA * x_ref[...] + gt_ref[...] * y_ref[...])
        o_ref[...] = xhat * g_ref[...] + b_ref[...]

    return pl.pallas_call(
        body, out_shape=SDS((t, d), f32), grid=(t // tt,),
        in_specs=[_row_spec(tt, d), _row_spec(tt, d), _vec_spec(d), _vec_spec(d), _vec_spec(d)],
        out_specs=_row_spec(tt, d), name=name, compiler_params=_cparams(("parallel",)),
    )(x, y, gate, g, b)


def res_layernorm_bwd(dout, x, y, gate, g, name):
    t, d = x.shape
    tt = _row_tile(t)

    def body(do_ref, x_ref, y_ref, gt_ref, g_ref, dxr_ref, dy_ref, dgt_ref, dg_ref, db_ref):
        y_v = y_ref[...]
        do_v = do_ref[...]
        xhat, rstd = _ln_stats(ALPHA * x_ref[...] + gt_ref[...] * y_v)
        dz = _ln_bwd(do_v * g_ref[...], xhat, rstd)
        dxr_ref[...] = ALPHA * dz
        dy_ref[...] = (gt_ref[...] * dz).astype(bf16)
        _acc_rows(dgt_ref, jnp.sum(dz * y_v, axis=0, keepdims=True))
        _acc_rows(dg_ref, jnp.sum(do_v * xhat, axis=0, keepdims=True))
        _acc_rows(db_ref, jnp.sum(do_v, axis=0, keepdims=True))

    vec = SDS((1, d), f32)
    return pl.pallas_call(
        body, out_shape=(SDS((t, d), f32), SDS((t, d), bf16), vec, vec, vec), grid=(t // tt,),
        in_specs=[_row_spec(tt, d), _row_spec(tt, d), _row_spec(tt, d), _vec_spec(d), _vec_spec(d)],
        out_specs=(_row_spec(tt, d), _row_spec(tt, d), _vec_spec(d), _vec_spec(d), _vec_spec(d)),
        name=name, compiler_params=_cparams(("arbitrary",)),
    )(dout, x, y, gate, g)


def loss_head(y, target, name):
    t, d = y.shape
    tt = _row_tile(t)

    def body(y_ref, t_ref, l_ref, dy_ref):
        e = y_ref[...] - t_ref[...]
        dy_ref[...] = e * (1.0 / d)
        part = jnp.sum(jnp.sum(e * e, axis=1, keepdims=True), axis=0, keepdims=True) * (0.5 / d)
        _acc_rows(l_ref, jnp.broadcast_to(part, (1, LANE)))

    return pl.pallas_call(
        body, out_shape=(SDS((1, LANE), f32), SDS((t, d), f32)), grid=(t // tt,),
        in_specs=[_row_spec(tt, d), _row_spec(tt, d)],
        out_specs=(pl.BlockSpec((1, LANE), lambda i: (0, 0)), _row_spec(tt, d)),
        name=name, compiler_params=_cparams(("arbitrary",)),
    )(y, target)


def _fill_pad(pad_ref, val, t):
    zeros = jnp.zeros((PAD_ROWS, LANE), f32)
    pad_ref[0:PAD_ROWS, :] = zeros
    pad_ref[PAD_ROWS + t:2 * PAD_ROWS + t, :] = zeros
    pad_ref[PAD_ROWS:PAD_ROWS + t, :] = val


def _grid_mask(r0, rows, dc):
    col = (lax.broadcasted_iota(jnp.int32, (rows, 1), 0) + r0) % GRID_W
    return ((col + dc >= 0) & (col + dc < GRID_W)).astype(f32)


def _taps_apply(pad_ref, w_ref, taps, r0, rows):
    acc = jnp.zeros((rows, LANE), f32)
    for off, dc, wi in taps:
        xs = pad_ref[PAD_ROWS + r0 + off:PAD_ROWS + r0 + off + rows, :]
        if dc is not None and dc != 0:
            xs = xs * _grid_mask(r0, rows, dc)
        acc = acc + w_ref[wi:wi + 1, :] * xs
    return acc


def _taps_wgrad(pad_ref, dy, taps, r0, rows, nw):
    out = jnp.zeros((nw, LANE), f32)
    rid = lax.broadcasted_iota(jnp.int32, (nw, 1), 0)
    for off, dc, wi in taps:
        xs = pad_ref[PAD_ROWS + r0 + off:PAD_ROWS + r0 + off + rows, :]
        if dc is not None and dc != 0:
            xs = xs * _grid_mask(r0, rows, dc)
        s = jnp.sum(dy * xs, axis=0, keepdims=True)
        out = out + jnp.where(rid == wi, s, 0.0)
    return out


def _transpose_taps(taps):
    return [(-off, None if dc is None else -dc, wi) for off, dc, wi in taps]


def _taps_1d(width):
    return [(j - width // 2, None, j) for j in range(width)]


def _taps_grid3():
    return [(GRID_W * dr + dc, dc, 3 * (dr + 1) + (dc + 1)) for dr in (-1, 0, 1) for dc in (-1, 0, 1)]


def _row_chunks(t):
    r = min(CONV_ROWS, t)
    return [(i * r, r) for i in range(t // r)]


def _col_spec(t, off):
    return pl.BlockSpec((t, LANE), lambda c: (0, c + off))


def _w_spec(nw, off=0):
    return pl.BlockSpec((nw, LANE), lambda c: (0, c + off))


def gdn_conv(p, w, col0, nblk, norm_scale, name):
    t = p.shape[0]
    nw = w.shape[0]
    taps = _taps_1d(5)

    def body(p_ref, w_ref, o_ref, pad_ref):
        _fill_pad(pad_ref, p_ref[...], t)
        for r0, rows in _row_chunks(t):
            a = _silu(_taps_apply(pad_ref, w_ref, taps, r0, rows))
            if norm_scale is not None:
                a = a * (lax.rsqrt(jnp.sum(a * a, axis=-1, keepdims=True) + RMS_EPS) * norm_scale)
            o_ref[r0:r0 + rows, :] = a

    return pl.pallas_call(
        body, out_shape=SDS((t, nblk * LANE), f32), grid=(nblk,),
        in_specs=[_col_spec(t, col0), _w_spec(nw, col0)], out_specs=_col_spec(t, 0),
        scratch_shapes=[pltpu.VMEM((t + 2 * PAD_ROWS, LANE), f32)], name=name,
        compiler_params=_cparams(("parallel",)),
    )(p, w)


def gdn_conv_bwd(p, w, d_a, d_b, col0, nblk, norm_scale, name):
    t = p.shape[0]
    nw = w.shape[0]
    taps = _taps_1d(5)
    ttaps = _transpose_taps(taps)

    def body(p_ref, w_ref, da_ref, db_ref, dp_ref, dw_ref, pad_ref, gpad_ref):
        _fill_pad(pad_ref, p_ref[...], t)
        for r0, rows in _row_chunks(t):
            pre = _taps_apply(pad_ref, w_ref, taps, r0, rows)
            a = _silu(pre)
            dy = da_ref[r0:r0 + rows, :] + db_ref[r0:r0 + rows, :]
            if norm_scale is not None:
                r = lax.rsqrt(jnp.sum(a * a, axis=-1, keepdims=True) + RMS_EPS)
                da = norm_scale * (dy * r - a * (r * r * r) * jnp.sum(dy * a, axis=-1, keepdims=True))
            else:
                da = dy
            gpad_ref[PAD_ROWS + r0:PAD_ROWS + r0 + rows, :] = da * _dsilu(pre)
        zeros = jnp.zeros((PAD_ROWS, LANE), f32)
        gpad_ref[0:PAD_ROWS, :] = zeros
        gpad_ref[PAD_ROWS + t:2 * PAD_ROWS + t, :] = zeros
        dw = jnp.zeros((nw, LANE), f32)
        for r0, rows in _row_chunks(t):
            dp_ref[r0:r0 + rows, :] = _taps_apply(gpad_ref, w_ref, ttaps, r0, rows).astype(bf16)
            dw = dw + _taps_wgrad(pad_ref, gpad_ref[PAD_ROWS + r0:PAD_ROWS + r0 + rows, :], taps, r0, rows, nw)
        dw_ref[...] = dw

    return pl.pallas_call(
        body, out_shape=(SDS((t, nblk * LANE), bf16), SDS((nw, nblk * LANE), f32)), grid=(nblk,),
        in_specs=[_col_spec(t, col0), _w_spec(nw, col0), _col_spec(t, 0), _col_spec(t, 0)],
        out_specs=(_col_spec(t, 0), _w_spec(nw)),
        scratch_shapes=[pltpu.VMEM((t + 2 * PAD_ROWS, LANE), f32)] * 2, name=name,
        compiler_params=_cparams(("parallel",)),
    )(p, w, d_a, d_b)


def short_conv(p, w, name):
    t = p.shape[0]
    nw = w.shape[0]
    taps = _taps_1d(3)

    def body(gb_ref, gc_ref, h_ref, w_ref, o_ref, pad_ref):
        _fill_pad(pad_ref, gc_ref[...] * h_ref[...], t)
        for r0, rows in _row_chunks(t):
            o_ref[r0:r0 + rows, :] = (gb_ref[r0:r0 + rows, :] * _taps_apply(pad_ref, w_ref, taps, r0, rows)).astype(bf16)

    return pl.pallas_call(
        body, out_shape=SDS((t, 4 * LANE), bf16), grid=(4,),
        in_specs=[_col_spec(t, 0), _col_spec(t, 4), _col_spec(t, 8), _w_spec(nw)], out_specs=_col_spec(t, 0),
        scratch_shapes=[pltpu.VMEM((t + 2 * PAD_ROWS, LANE), f32)], name=name,
        compiler_params=_cparams(("parallel",)),
    )(p, p, p, w)


def short_conv_bwd(p, w, dy, name):
    t = p.shape[0]
    nw = w.shape[0]
    taps = _taps_1d(3)
    ttaps = _transpose_taps(taps)

    def body(gb_ref, gc_ref, h_ref, w_ref, dy_ref, dgb_ref, dgc_ref, dh_ref, dw_ref, pad_ref, gpad_ref):
        _fill_pad(pad_ref, gc_ref[...] * h_ref[...], t)
        _fill_pad(gpad_ref, dy_ref[...] * gb_ref[...], t)
        dw = jnp.zeros((nw, LANE), f32)
        for r0, rows in _row_chunks(t):
            sl = slice(r0, r0 + rows)
            dgb_ref[sl, :] = (dy_ref[sl, :] * _taps_apply(pad_ref, w_ref, taps, r0, rows)).astype(bf16)
            dm = _taps_apply(gpad_ref, w_ref, ttaps, r0, rows)
            dgc_ref[sl, :] = (dm * h_ref[sl, :]).astype(bf16)
            dh_ref[sl, :] = (dm * gc_ref[sl, :]).astype(bf16)
            dw = dw + _taps_wgrad(pad_ref, gpad_ref[PAD_ROWS + r0:PAD_ROWS + r0 + rows, :], taps, r0, rows, nw)
        dw_ref[...] = dw

    blk = SDS((t, 4 * LANE), bf16)
    return pl.pallas_call(
        body, out_shape=(blk, blk, blk, SDS((nw, 4 * LANE), f32)), grid=(4,),
        in_specs=[_col_spec(t, 0), _col_spec(t, 4), _col_spec(t, 8), _w_spec(nw), _col_spec(t, 0)],
        out_specs=(_col_spec(t, 0), _col_spec(t, 0), _col_spec(t, 0), _w_spec(nw)),
        scratch_shapes=[pltpu.VMEM((t + 2 * PAD_ROWS, LANE), f32)] * 2, name=name,
        compiler_params=_cparams(("parallel",)),
    )(p, p, p, w, dy)


def conf_conv(p, w, name):
    t = p.shape[0]
    nw = w.shape[0]
    taps = _taps_1d(31)

    def body(a_ref, b_ref, w_ref, o_ref, pad_ref):
        _fill_pad(pad_ref, a_ref[...] * jax.nn.sigmoid(b_ref[...]), t)
        for r0, rows in _row_chunks(t):
            o_ref[r0:r0 + rows, :] = _taps_apply(pad_ref, w_ref, taps, r0, rows)

    return pl.pallas_call(
        body, out_shape=SDS((t, 4 * LANE), f32), grid=(4,),
        in_specs=[_col_spec(t, 12), _col_spec(t, 16), _w_spec(nw)], out_specs=_col_spec(t, 0),
        scratch_shapes=[pltpu.VMEM((t + 2 * PAD_ROWS, LANE), f32)], name=name,
        compiler_params=_cparams(("parallel",)),
    )(p, p, w)


def conf_conv_bwd(p, w, dz, name):
    t = p.shape[0]
    nw = w.shape[0]
    taps = _taps_1d(31)
    ttaps = _transpose_taps(taps)

    def body(a_ref, b_ref, w_ref, dz_ref, da_ref, db_ref, dw_ref, pad_ref, gpad_ref):
        _fill_pad(pad_ref, a_ref[...] * jax.nn.sigmoid(b_ref[...]), t)
        _fill_pad(gpad_ref, dz_ref[...], t)
        dw = jnp.zeros((nw, LANE), f32)
        for r0, rows in _row_chunks(t):
            sl = slice(r0, r0 + rows)
            dm = _taps_apply(gpad_ref, w_ref, ttaps, r0, rows)
            sg = jax.nn.sigmoid(b_ref[sl, :])
            da_ref[sl, :] = (dm * sg).astype(bf16)
            db_ref[sl, :] = (dm * a_ref[sl, :] * sg * (1.0 - sg)).astype(bf16)
            dw = dw + _taps_wgrad(pad_ref, dz_ref[sl, :], taps, r0, rows, nw)
        dw_ref[...] = dw

    blk = SDS((t, 4 * LANE), bf16)
    return pl.pallas_call(
        body, out_shape=(blk, blk, SDS((nw, 4 * LANE), f32)), grid=(4,),
        in_specs=[_col_spec(t, 12), _col_spec(t, 16), _w_spec(nw), _col_spec(t, 0)],
        out_specs=(_col_spec(t, 0), _col_spec(t, 0), _w_spec(nw)),
        scratch_shapes=[pltpu.VMEM((t + 2 * PAD_ROWS, LANE), f32)] * 2, name=name,
        compiler_params=_cparams(("parallel",)),
    )(p, p, w, dz)


def ffn_conv(h, w, name):
    t = h.shape[0]
    nblk = D_FF // LANE
    nw = w.shape[0]
    taps = _taps_grid3()

    def body(a_ref, g_ref, w_ref, o_ref, pad_ref):
        _fill_pad(pad_ref, a_ref[...], t)
        for r0, rows in _row_chunks(t):
            o_ref[r0:r0 + rows, :] = (_silu(_taps_apply(pad_ref, w_ref, taps, r0, rows)) * g_ref[r0:r0 + rows, :]).astype(bf16)

    return pl.pallas_call(
        body, out_shape=SDS((t, D_FF), bf16), grid=(nblk,),
        in_specs=[_col_spec(t, 0), _col_spec(t, nblk), _w_spec(nw)], out_specs=_col_spec(t, 0),
        scratch_shapes=[pltpu.VMEM((t + 2 * PAD_ROWS, LANE), f32)], name=name,
        compiler_params=_cparams(("parallel",)),
    )(h, h, w)


def ffn_conv_bwd(h, w, df, name):
    t = h.shape[0]
    nblk = D_FF // LANE
    nw = w.shape[0]
    taps = _taps_grid3()
    ttaps = _transpose_taps(taps)

    def body(a_ref, g_ref, w_ref, df_ref, dh_ref, dw_ref, pad_ref, gpad_ref, pre_ref):
        half = pl.program_id(1)

        @pl.when(half == 0)
        def _():
            _fill_pad(pad_ref, a_ref[...], t)
            zeros = jnp.zeros((PAD_ROWS, LANE), f32)
            gpad_ref[0:PAD_ROWS, :] = zeros
            gpad_ref[PAD_ROWS + t:2 * PAD_ROWS + t, :] = zeros
            for r0, rows in _row_chunks(t):
                sl = slice(r0, r0 + rows)
                pre = _taps_apply(pad_ref, w_ref, taps, r0, rows)
                pre_ref[sl, :] = pre
                gpad_ref[PAD_ROWS + r0:PAD_ROWS + r0 + rows, :] = df_ref[sl, :] * g_ref[sl, :] * _dsilu(pre)
            dw = jnp.zeros((nw, LANE), f32)
            for r0, rows in _row_chunks(t):
                dh_ref[r0:r0 + rows, :] = _taps_apply(gpad_ref, w_ref, ttaps, r0, rows).astype(bf16)
                dw = dw + _taps_wgrad(pad_ref, gpad_ref[PAD_ROWS + r0:PAD_ROWS + r0 + rows, :], taps, r0, rows, nw)
            dw_ref[...] = dw

        @pl.when(half == 1)
        def _():
            for r0, rows in _row_chunks(t):
                sl = slice(r0, r0 + rows)
                dh_ref[sl, :] = (df_ref[sl, :] * _silu(pre_ref[sl, :])).astype(bf16)

    cspec = lambda off: pl.BlockSpec((t, LANE), lambda c, s: (0, c + off))
    return pl.pallas_call(
        body, out_shape=(SDS((t, 2 * D_FF), bf16), SDS((nw, D_FF), f32)), grid=(nblk, 2),
        in_specs=[cspec(0), cspec(nblk), pl.BlockSpec((nw, LANE), lambda c, s: (0, c)), cspec(0)],
        out_specs=(pl.BlockSpec((t, LANE), lambda c, s: (0, c + nblk * s)), pl.BlockSpec((nw, LANE), lambda c, s: (0, c))),
        scratch_shapes=[pltpu.VMEM((t + 2 * PAD_ROWS, LANE), f32)] * 2 + [pltpu.VMEM((t, LANE), f32)], name=name,
        compiler_params=_cparams(("parallel", "arbitrary")),
    )(h, h, w, df)


def _pool_count(r0, rows, win, t):
    pos = lax.broadcasted_iota(jnp.int32, (rows, 1), 0) + r0
    lo = jnp.clip(pos - win // 2, 0, t)
    hi = jnp.clip(pos - win // 2 + win, 0, t)
    return (hi - lo).astype(f32)


def _window_sum(pad_ref, r0, rows, lo, hi):
    acc = jnp.zeros((rows, LANE), f32)
    for off in range(lo, hi):
        acc = acc + pad_ref[PAD_ROWS + r0 + off:PAD_ROWS + r0 + off + rows, :]
    return acc


def pool_mix(p, pool_w, pool_scale, name):
    t = p.shape[0]

    def body(x_ref, w_ref, s_ref, o_ref, pad_ref):
        for gi, win in enumerate(POOL_WINDOWS):
            cs = slice(gi * LANE, (gi + 1) * LANE)
            _fill_pad(pad_ref, x_ref[:, cs], t)
            wg = w_ref[gi].astype(bf16)
            for r0, rows in _row_chunks(t):
                pooled = _window_sum(pad_ref, r0, rows, -(win // 2), win - win // 2) / _pool_count(r0, rows, win, t) - x_ref[r0:r0 + rows, cs]
                o_ref[r0:r0 + rows, cs] = (_dotb(pooled, wg) * s_ref[:, cs]).astype(bf16)

    return pl.pallas_call(
        body, out_shape=SDS((t, 512), bf16), grid=(1,),
        in_specs=[pl.BlockSpec((t, 512), lambda i: (0, 4)), pl.BlockSpec((4, LANE, LANE), lambda i: (0, 0, 0)),
                  pl.BlockSpec((1, 512), lambda i: (0, 0))],
        out_specs=pl.BlockSpec((t, 512), lambda i: (0, 0)),
        scratch_shapes=[pltpu.VMEM((t + 2 * PAD_ROWS, LANE), f32)], name=name,
        compiler_params=_cparams(("arbitrary",)),
    )(p, pool_w, pool_scale)


def pool_mix_bwd(p, pool_w, pool_scale, dmix, name):
    t = p.shape[0]

    def body(x_ref, w_ref, s_ref, dy_ref, dp_ref, dw_ref, ds_ref, pad_ref, gpad_ref, dpool_ref):
        for gi, win in enumerate(POOL_WINDOWS):
            cs = slice(gi * LANE, (gi + 1) * LANE)
            h = win // 2
            _fill_pad(pad_ref, x_ref[:, cs], t)
            wg = w_ref[gi].astype(bf16)
            dw = jnp.zeros((LANE, LANE), f32)
            ds = jnp.zeros((1, LANE), f32)
            zeros = jnp.zeros((PAD_ROWS, LANE), f32)
            gpad_ref[0:PAD_ROWS, :] = zeros
            gpad_ref[PAD_ROWS + t:2 * PAD_ROWS + t, :] = zeros
            for r0, rows in _row_chunks(t):
                cnt = _pool_count(r0, rows, win, t)
                pooled = _window_sum(pad_ref, r0, rows, -h, win - h) / cnt - x_ref[r0:r0 + rows, cs]
                dy = dy_ref[r0:r0 + rows, cs]
                ds = ds + jnp.sum(dy * _dotb(pooled, wg), axis=0, keepdims=True)
                dypre = dy * s_ref[:, cs]
                dw = dw + _dotb_tn(pooled, dypre)
                dpooled = _dotb_nt(dypre, wg)
                gpad_ref[PAD_ROWS + r0:PAD_ROWS + r0 + rows, :] = dpooled / cnt
                dpool_ref[r0:r0 + rows, :] = dpooled
            dw_ref[gi] = dw
            ds_ref[:, cs] = ds
            for r0, rows in _row_chunks(t):
                dx = _window_sum(gpad_ref, r0, rows, -h + 1, h + 1) - dpool_ref[r0:r0 + rows, :]
                dp_ref[r0:r0 + rows, cs] = dx.astype(bf16)

    return pl.pallas_call(
        body, out_shape=(SDS((t, 512), bf16), SDS((4, LANE, LANE), f32), SDS((1, 512), f32)), grid=(1,),
        in_specs=[pl.BlockSpec((t, 512), lambda i: (0, 4)), pl.BlockSpec((4, LANE, LANE), lambda i: (0, 0, 0)),
                  pl.BlockSpec((1, 512), lambda i: (0, 0)), pl.BlockSpec((t, 512), lambda i: (0, 1))],
        out_specs=(pl.BlockSpec((t, 512), lambda i: (0, 0)), pl.BlockSpec((4, LANE, LANE), lambda i: (0, 0, 0)),
                   pl.BlockSpec((1, 512), lambda i: (0, 0))),
        scratch_shapes=[pltpu.VMEM((t + 2 * PAD_ROWS, LANE), f32)] * 2 + [pltpu.VMEM((t, LANE), f32)], name=name,
        compiler_params=_cparams(("arbitrary",)),
    )(p, pool_w, pool_scale, dmix)


def gated_rmsnorm(o_a, o_b, p, norm_w, name):
    t = o_a.shape[0]
    tt = _row_tile(t)

    def body(oa_ref, ob_ref, g_ref, nw_ref, y_ref):
        for h in range(GDN_HEADS):
            cs = slice(h * LANE, (h + 1) * LANE)
            o = oa_ref[:, cs] + ob_ref[:, cs]
            r = lax.rsqrt(jnp.mean(o * o, axis=-1, keepdims=True) + RMS_EPS)
            y_ref[:, cs] = (o * r * nw_ref[...] * _silu(g_ref[:, cs])).astype(bf16)

    return pl.pallas_call(
        body, out_shape=SDS((t, 512), bf16), grid=(t // tt,),
        in_specs=[_row_spec(tt, 512), _row_spec(tt, 512), pl.BlockSpec((tt, 512), lambda i: (i, 3)), _vec_spec(LANE)],
        out_specs=_row_spec(tt, 512), name=name, compiler_params=_cparams(("parallel",)),
    )(o_a, o_b, p, norm_w)


def gated_rmsnorm_bwd(o_a, o_b, p, norm_w, dmix, name):
    t = o_a.shape[0]
    tt = _row_tile(t)

    def body(oa_ref, ob_ref, g_ref, nw_ref, dy_ref, do_ref, dg_ref, dnw_ref):
        dnw = jnp.zeros((1, LANE), f32)
        for h in range(GDN_HEADS):
            cs = slice(h * LANE, (h + 1) * LANE)
            o = oa_ref[:, cs] + ob_ref[:, cs]
            r = lax.rsqrt(jnp.mean(o * o, axis=-1, keepdims=True) + RMS_EPS)
            gate = g_ref[:, cs]
            dy = dy_ref[:, cs]
            dy1 = dy * _silu(gate)
            dg_ref[:, cs] = (dy * (o * r * nw_ref[...]) * _dsilu(gate)).astype(bf16)
            dnw = dnw + jnp.sum(dy1 * o * r, axis=0, keepdims=True)
            dn = dy1 * nw_ref[...]
            do_ref[:, cs] = r * dn - o * (r * r * r) * jnp.mean(dn * o, axis=-1, keepdims=True)
        _acc_rows(dnw_ref, dnw)

    return pl.pallas_call(
        body, out_shape=(SDS((t, 512), f32), SDS((t, 512), bf16), SDS((1, LANE), f32)), grid=(t // tt,),
        in_specs=[_row_spec(tt, 512), _row_spec(tt, 512), pl.BlockSpec((tt, 512), lambda i: (i, 3)), _vec_spec(LANE),
                  _row_spec(tt, 512)],
        out_specs=(_row_spec(tt, 512), _row_spec(tt, 512), _vec_spec(LANE)),
        name=name, compiler_params=_cparams(("arbitrary",)),
    )(o_a, o_b, p, norm_w, dmix)


def ln_silu(z, g, b, name):
    t, d = z.shape
    tt = _row_tile(t)

    def body(z_ref, g_ref, b_ref, o_ref):
        xhat, _ = _ln_stats(z_ref[...])
        o_ref[...] = _silu(xhat * g_ref[...] + b_ref[...]).astype(bf16)

    return pl.pallas_call(
        body, out_shape=SDS((t, d), bf16), grid=(t // tt,),
        in_specs=[_row_spec(tt, d), _vec_spec(d), _vec_spec(d)], out_specs=_row_spec(tt, d),
        name=name, compiler_params=_cparams(("parallel",)),
    )(z, g, b)


def ln_silu_bwd(z, g, b, dmix, name):
    t, d = z.shape
    tt = _row_tile(t)

    def body(z_ref, g_ref, b_ref, dy_ref, dz_ref, dg_ref, db_ref):
        xhat, rstd = _ln_stats(z_ref[...])
        dn = dy_ref[...] * _dsilu(xhat * g_ref[...] + b_ref[...])
        dz_ref[...] = _ln_bwd(dn * g_ref[...], xhat, rstd)
        _acc_rows(dg_ref, jnp.sum(dn * xhat, axis=0, keepdims=True))
        _acc_rows(db_ref, jnp.sum(dn, axis=0, keepdims=True))

    return pl.pallas_call(
        body, out_shape=(SDS((t, d), f32), SDS((1, d), f32), SDS((1, d), f32)), grid=(t // tt,),
        in_specs=[_row_spec(tt, d), _vec_spec(d), _vec_spec(d), pl.BlockSpec((tt, d), lambda i: (i, 1))],
        out_specs=(_row_spec(tt, d), _vec_spec(d), _vec_spec(d)),
        name=name, compiler_params=_cparams(("arbitrary",)),
    )(z, g, b, dmix)


def gdn_gates(p, neg_a, dt_bias, name):
    t = p.shape[0]
    tt = _row_tile(t)

    def body(s_ref, na_ref, dt_ref, o_ref):
        s = s_ref[...]
        col = lax.broadcasted_iota(jnp.int32, s.shape, 1)
        o_ref[...] = jnp.where(col < 8, jax.nn.sigmoid(s), na_ref[...] * jax.nn.softplus(s + dt_ref[...]))

    return pl.pallas_call(
        body, out_shape=SDS((t, LANE), f32), grid=(t // tt,),
        in_specs=[pl.BlockSpec((tt, LANE), lambda i: (i, 20)), _vec_spec(LANE), _vec_spec(LANE)],
        out_specs=_row_spec(tt, LANE), name=name, compiler_params=_cparams(("parallel",)),
    )(p, neg_a, dt_bias)


def gdn_gates_bwd(p, neg_a, dt_bias, dbg_a, dbg_b, name):
    t = p.shape[0]
    tt = _row_tile(t)

    def body(s_ref, na_ref, dt_ref, d_ref, d2_ref, ds_ref, da_ref, ddt_ref):
        s = s_ref[...]
        d = d_ref[...] + d2_ref[...]
        col = lax.broadcasted_iota(jnp.int32, s.shape, 1)
        sg = jax.nn.sigmoid(s)
        z = s + dt_ref[...]
        dz = jnp.where((col >= 8) & (col < 16), d * na_ref[...] * jax.nn.sigmoid(z), 0.0)
        ds_ref[...] = jnp.where(col < 8, d * sg * (1.0 - sg), dz).astype(bf16)
        dalog = jnp.where((col >= 8) & (col < 16), d * na_ref[...] * jax.nn.softplus(z), 0.0)
        _acc_rows(da_ref, jnp.sum(dalog, axis=0, keepdims=True))
        _acc_rows(ddt_ref, jnp.sum(dz, axis=0, keepdims=True))

    return pl.pallas_call(
        body, out_shape=(SDS((t, LANE), bf16), SDS((1, LANE), f32), SDS((1, LANE), f32)), grid=(t // tt,),
        in_specs=[pl.BlockSpec((tt, LANE), lambda i: (i, 20)), _vec_spec(LANE), _vec_spec(LANE), _row_spec(tt, LANE),
                  _row_spec(tt, LANE)],
        out_specs=(_row_spec(tt, LANE), _vec_spec(LANE), _vec_spec(LANE)),
        name=name, compiler_params=_cparams(("arbitrary",)),
    )(p, neg_a, dt_bias, dbg_a, dbg_b)


N_SCAN = 2 * GDN_HEADS


def _bdot(a, b, ca, cb, precision=None):
    if precision is None:
        a, b = a.astype(bf16), b.astype(bf16)
    return lax.dot_general(a, b, (((ca,), (cb,)), ((0,), (0,))), preferred_element_type=f32, precision=precision)


def _bdot_nn(a, b, precision=None):
    return _bdot(a, b, 2, 1, precision)


def _bdot_nt(a, b):
    return _bdot(a, b, 2, 2)


def _bdot_tn(a, b, precision=None):
    return _bdot(a, b, 1, 1, precision)


def _order_masks():
    shape = (N_SCAN, CHUNK, CHUNK)
    sign = jnp.where(lax.broadcasted_iota(jnp.int32, shape, 0) >= GDN_HEADS, -1, 1)
    ahead = (lax.broadcasted_iota(jnp.int32, shape, 1) - lax.broadcasted_iota(jnp.int32, shape, 2)) * sign
    lower, strict, lower_t = ahead >= 0, ahead > 0, ahead <= 0
    col_shape = (N_SCAN, CHUNK, 1)
    back1 = lax.broadcasted_iota(jnp.int32, col_shape, 0) >= GDN_HEADS
    row1 = lax.broadcasted_iota(jnp.int32, col_shape, 1)
    at_last = (row1 == jnp.where(back1, 0, CHUNK - 1)).astype(f32)
    return lower, strict, lower_t, at_last


def _stack_heads(f_ref, b_ref):
    return jnp.stack([ref[:, h * LANE:(h + 1) * LANE] for ref in (f_ref, b_ref) for h in range(GDN_HEADS)])


def _stack_gates(bgf, bgb, bgtf, bgtb):
    beta = jnp.stack([bg[:, 4 * d + h:4 * d + h + 1] for d, bg in enumerate((bgf, bgb)) for h in range(GDN_HEADS)])
    g_col = jnp.stack([bg[:, 8 + 4 * d + h:9 + 4 * d + h] for d, bg in enumerate((bgf, bgb)) for h in range(GDN_HEADS)])
    g_row = jnp.stack([bgt[8 + 4 * d + h:9 + 4 * d + h, :] for d, bgt in enumerate((bgtf, bgtb)) for h in range(GDN_HEADS)])
    return beta, g_col, g_row


def _chunk_terms(k, v, beta, g_col, g_row, masks):
    lower, strict, lower_t, at_last = masks
    gc = jnp.sum(lower.astype(f32) * g_row, axis=2, keepdims=True)
    gr = jnp.sum(lower_t.astype(f32) * g_col, axis=1, keepdims=True)
    g_last = jnp.sum(at_last * gc, axis=1, keepdims=True)
    e = jnp.exp(gc)
    f = jnp.exp(g_last - gc)
    dm = jnp.exp(jnp.where(lower, gc - gr, -1e30))
    kb = k * beta
    kk = _bdot_nt(kb, k)
    a = jnp.where(strict, kk * dm, 0.0)
    shape = (N_SCAN, CHUNK, CHUNK)
    eye = (lax.broadcasted_iota(jnp.int32, shape, 1) == lax.broadcasted_iota(jnp.int32, shape, 2)).astype(f32)
    pw = -a
    tinv = eye + pw
    for _ in range(5):
        pw = _bdot_nn(pw, pw, HI)
        tinv = tinv + _bdot_nn(tinv, pw, HI)
    u = _bdot_nn(tinv, v * beta)
    w = _bdot_nn(tinv, kb * e)
    return dict(e=e, f=f, gl=jnp.exp(g_last), dm=dm, kb=kb, kk=kk, tinv=tinv, u=u, w=w, kd=k * f)


def _gdn_specs(nc, width, step_chunk):
    return [pl.BlockSpec((CHUNK, width), functools.partial(lambda i, d: (step_chunk(i, d), 0), d=d)) for d in (0, 1)]


def gdn_forward(q, k, v, bg, bgt, s0, with_out, name):
    t = k.shape[0]
    nc = t // CHUNK

    def body(qf_ref, qb_ref, kf_ref, kb_ref, vf_ref, vb_ref, bgf_ref, bgb_ref, bgtf_ref, bgtb_ref, s0_ref,
             of_ref, ob_ref, sallf_ref, sallb_ref, sfin_ref, s_ref):
        i = pl.program_id(0)

        @pl.when(i == 0)
        def _():
            s_ref[...] = s0_ref[...]

        masks = _order_masks()
        k8, v8 = _stack_heads(kf_ref, kb_ref), _stack_heads(vf_ref, vb_ref)
        beta, g_col, g_row = _stack_gates(bgf_ref[...], bgb_ref[...], bgtf_ref[0], bgtb_ref[0])
        c = _chunk_terms(k8, v8, beta, g_col, g_row, masks)
        s = s_ref[...]
        sallf_ref[0] = s[:GDN_HEADS]
        sallb_ref[0] = s[GDN_HEADS:]
        vn = c["u"] - _bdot_nn(c["w"], s)
        if with_out:
            q8 = _stack_heads(qf_ref, qb_ref)
            pm = jnp.where(masks[0], _bdot_nt(q8, k8) * c["dm"], 0.0)
            o = _bdot_nn(q8 * c["e"], s) + _bdot_nn(pm, vn)
        for d, o_ref in enumerate((of_ref, ob_ref)):
            for h in range(GDN_HEADS):
                o_ref[:, h * LANE:(h + 1) * LANE] = o[GDN_HEADS * d + h] if with_out else jnp.zeros((CHUNK, LANE), f32)
        s_ref[...] = c["gl"] * s + _bdot_tn(c["kd"], vn)

        @pl.when(i == nc - 1)
        def _():
            sfin_ref[...] = s_ref[...]

    chunk_of = lambda i, d: i if d == 0 else nc - 1 - i
    seq = _gdn_specs(nc, 512, chunk_of)
    gate = _gdn_specs(nc, LANE, chunk_of)
    gate_t = [pl.BlockSpec((1, 16, CHUNK), functools.partial(lambda i, d: (chunk_of(i, d), 0, 0), d=d)) for d in (0, 1)]
    sall = [pl.BlockSpec((1, GDN_HEADS, LANE, LANE), functools.partial(lambda i, d: (chunk_of(i, d), 0, 0, 0), d=d)) for d in (0, 1)]
    st = pl.BlockSpec((N_SCAN, LANE, LANE), lambda i: (0, 0, 0))
    o_shape, s_shape = SDS((t, 512), f32), SDS((nc, GDN_HEADS, LANE, LANE), f32)
    o_f, o_b, sall_f, sall_b, s_fin = pl.pallas_call(
        body, out_shape=(o_shape, o_shape, s_shape, s_shape, SDS((N_SCAN, LANE, LANE), f32)), grid=(nc,),
        in_specs=seq + seq + seq + gate + gate_t + [st], out_specs=tuple(seq + sall + [st]),
        scratch_shapes=[pltpu.VMEM((N_SCAN, LANE, LANE), f32)], name=name,
        compiler_params=_cparams(("arbitrary",)),
    )(q, q, k, k, v, v, bg, bg, bgt, bgt, s0.reshape(N_SCAN, LANE, LANE))
    return o_f, o_b, sall_f, sall_b, s_fin.reshape(2, GDN_HEADS, LANE, LANE)


def _gdn_chunk_bwd(q, k, v, d_o, beta, g_col, g_row, s, dsn, masks):
    lower, strict, _, at_last = masks
    c = _chunk_terms(k, v, beta, g_col, g_row, masks)
    e, f, gl, dm, kb, kk, tinv, u, w, kd = (c[n] for n in ("e", "f", "gl", "dm", "kb", "kk", "tinv", "u", "w", "kd"))
    vn = u - _bdot_nn(w, s)
    ds = gl * dsn
    dgl = jnp.sum(jnp.sum(s * dsn, axis=2, keepdims=True), axis=1, keepdims=True)
    dkd = _bdot_nt(vn, dsn)
    dvn = _bdot_nn(kd, dsn)
    dm_grad = jnp.zeros((N_SCAN, CHUNK, CHUNK), f32)
    de = jnp.zeros((N_SCAN, CHUNK, 1), f32)
    dq = None
    dk = jnp.zeros((N_SCAN, CHUNK, LANE), f32)
    if q is not None:
        qk = _bdot_nt(q, k)
        pm = jnp.where(lower, qk * dm, 0.0)
        dqd = _bdot_nt(d_o, s)
        ds = ds + _bdot_tn(q * e, d_o)
        dpm = jnp.where(lower, _bdot_nt(d_o, vn), 0.0)
        dvn = dvn + _bdot_tn(pm, d_o)
        dqk = dpm * dm
        dm_grad = dm_grad + dpm * qk
        dq = _bdot_nn(dqk, k) + dqd * e
        dk = _bdot_tn(dqk, q)
        de = de + jnp.sum(dqd * q, axis=2, keepdims=True)
    dw = -_bdot_nt(dvn, s)
    ds = ds - _bdot_tn(w, dvn)
    drv = _bdot_tn(tinv, dvn)
    drk = _bdot_tn(tinv, dw)
    da = -jnp.where(strict, _bdot_nt(drv, u) + _bdot_nt(drk, w), 0.0)
    dbeta = jnp.sum(drv * v, axis=2, keepdims=True)
    dv = drv * beta
    dkb = drk * e
    de = de + jnp.sum(drk * kb, axis=2, keepdims=True)
    dkk = da * dm
    dm_grad = dm_grad + da * kk
    dkb = dkb + _bdot_nn(dkk, k)
    dk = dk + _bdot_tn(dkk, kb) + dkd * f
    df = jnp.sum(dkd * k, axis=2, keepdims=True)
    dbeta = dbeta + jnp.sum(dkb * k, axis=2, keepdims=True)
    dk = dk + dkb * beta
    m = dm_grad * dm
    ones = jnp.ones((N_SCAN, CHUNK, LANE), f32)
    rsum = jnp.sum(m, axis=2, keepdims=True)
    csum = _bdot_tn(m, ones, HI)[:, :, 0:1]
    dgl_tot = jnp.sum(df * f, axis=1, keepdims=True) + dgl * gl
    dgc = de * e - df * f + rsum - csum + at_last * dgl_tot
    dg = _bdot_tn(lower.astype(f32), dgc * ones, HI)[:, :, 0:1]
    return dq, dk, dv, dbeta, dg, ds


def gdn_backward(q, k, v, bg, bgt, sall_f, sall_b, d_o, ds_fin, with_out, name):
    t = k.shape[0]
    nc = t // CHUNK

    def body(qf_ref, qb_ref, kf_ref, kb_ref, vf_ref, vb_ref, bgf_ref, bgb_ref, bgtf_ref, bgtb_ref,
             sallf_ref, sallb_ref, dof_ref, dob_ref, dsf_ref,
             dqf_ref, dqb_ref, dkf_ref, dkb_ref, dvf_ref, dvb_ref, dbgf_ref, dbgb_ref, ds0_ref, ds_ref):
        i = pl.program_id(0)

        @pl.when(i == 0)
        def _():
            ds_ref[...] = dsf_ref[...]

        lane = lax.broadcasted_iota(jnp.int32, (1, LANE), 1)
        masks = _order_masks()
        beta, g_col, g_row = _stack_gates(bgf_ref[...], bgb_ref[...], bgtf_ref[0], bgtb_ref[0])
        s = jnp.concatenate([sallf_ref[0], sallb_ref[0]], 0)
        dq, dk, dv, dbeta, dg, ds = _gdn_chunk_bwd(
            _stack_heads(qf_ref, qb_ref) if with_out else None, _stack_heads(kf_ref, kb_ref), _stack_heads(vf_ref, vb_ref),
            _stack_heads(dof_ref, dob_ref), beta, g_col, g_row, s, ds_ref[...], masks)
        ds_ref[...] = ds
        for d, (dq_ref, dk_ref, dv_ref, dbg_ref) in enumerate(((dqf_ref, dkf_ref, dvf_ref, dbgf_ref), (dqb_ref, dkb_ref, dvb_ref, dbgb_ref))):
            dbg = jnp.zeros((CHUNK, LANE), f32)
            for h in range(GDN_HEADS):
                b = GDN_HEADS * d + h
                cs = slice(h * LANE, (h + 1) * LANE)
                dq_ref[:, cs] = dq[b] if with_out else jnp.zeros((CHUNK, LANE), f32)
                dk_ref[:, cs] = dk[b]
                dv_ref[:, cs] = dv[b]
                dbg = dbg + dbeta[b] * (lane == b).astype(f32) + dg[b] * (lane == 8 + b).astype(f32)
            dbg_ref[...] = dbg

        @pl.when(i == nc - 1)
        def _():
            ds0_ref[...] = ds_ref[...]

    chunk_of = lambda i, d: nc - 1 - i if d == 0 else i
    seq = _gdn_specs(nc, 512, chunk_of)
    gate = _gdn_specs(nc, LANE, chunk_of)
    gate_t = [pl.BlockSpec((1, 16, CHUNK), functools.partial(lambda i, d: (chunk_of(i, d), 0, 0), d=d)) for d in (0, 1)]
    sall = [pl.BlockSpec((1, GDN_HEADS, LANE, LANE), functools.partial(lambda i, d: (chunk_of(i, d), 0, 0, 0), d=d)) for d in (0, 1)]
    st = pl.BlockSpec((N_SCAN, LANE, LANE), lambda i: (0, 0, 0))
    o_shape, g_shape = SDS((t, 512), f32), SDS((t, LANE), f32)
    res = pl.pallas_call(
        body, out_shape=(o_shape,) * 6 + (g_shape, g_shape, SDS((N_SCAN, LANE, LANE), f32)), grid=(nc,),
        in_specs=seq + seq + seq + gate + gate_t + sall + seq + [st], out_specs=tuple(seq + seq + seq + gate + [st]),
        scratch_shapes=[pltpu.VMEM((N_SCAN, LANE, LANE), f32)], name=name,
        compiler_params=_cparams(("arbitrary",)),
    )(q, q, k, k, v, v, bg, bg, bgt, bgt, sall_f, sall_b, d_o, d_o, ds_fin.reshape(N_SCAN, LANE, LANE))
    return tuple(res[:8]) + (res[8].reshape(2, GDN_HEADS, LANE, LANE),)


def _my_position():
    x, y, c = lax.axis_index("x"), lax.axis_index("y"), lax.axis_index("c")
    return x, y, c, 4 * x + 2 * y + c


def exchange(arrays, scatter, name):
    n = len(arrays)
    shapes = [a.shape[1:] if scatter else a.shape for a in arrays]

    def body(*refs):
        ins, outs = refs[:n], refs[n:2 * n]
        send_sems, recv_sems, local_sems = refs[2 * n:]
        x, y, c, me = _my_position()
        started = []
        for a in range(n):
            mine = pltpu.make_async_copy(ins[a].at[me] if scatter else ins[a], outs[a].at[me], local_sems.at[a])
            mine.start()
            started.append(mine)
        waits = []
        for r in range(1, N_DEV):
            px = 1 - x if r & 4 else x
            py = 1 - y if r & 2 else y
            pc = 1 - c if r & 1 else c
            pid = 4 * px + 2 * py + pc
            for a in range(n):
                cp = pltpu.make_async_remote_copy(
                    src_ref=ins[a].at[pid] if scatter else ins[a], dst_ref=outs[a].at[me],
                    send_sem=send_sems.at[a, r - 1], recv_sem=recv_sems.at[a, r - 1],
                    device_id=(px, py, pc), device_id_type=pl.DeviceIdType.MESH)
                cp.start()
                arrive = pltpu.make_async_remote_copy(
                    src_ref=ins[a].at[pid] if scatter else ins[a], dst_ref=outs[a].at[pid],
                    send_sem=send_sems.at[a, r - 1], recv_sem=recv_sems.at[a, r - 1],
                    device_id=(px, py, pc), device_id_type=pl.DeviceIdType.MESH)
                waits.append((cp, arrive))
        for cp, arrive in waits:
            cp.wait_send()
            arrive.wait_recv()
        for mine in started:
            mine.wait()

    any_spec = pl.BlockSpec(memory_space=pl.ANY)
    return pl.pallas_call(
        body, out_shape=tuple(SDS((N_DEV,) + tuple(s), a.dtype) for s, a in zip(shapes, arrays)),
        in_specs=[any_spec] * n, out_specs=tuple([any_spec] * n),
        scratch_shapes=[pltpu.SemaphoreType.DMA((n, N_DEV - 1)), pltpu.SemaphoreType.DMA((n, N_DEV - 1)),
                        pltpu.SemaphoreType.DMA((n,))],
        name=name,
    )(*arrays)


def ada_forward(a_raw, ada_w, ada_b_loc, name):
    def body(a_ref, w_ref, b_ref, o_ref):
        a = _silu(a_ref[...])
        for l in range(DEPTH):
            o_ref[l] = _dotf(a, w_ref[l]) + b_ref[l]

    return pl.pallas_call(body, out_shape=SDS((DEPTH, 16, ada_w.shape[2]), f32), name=name,
                          compiler_params=_cparams())(a_raw, ada_w, ada_b_loc)


def ada_backward(a_raw, ada_w, dm, name):
    def body(a_ref, w_ref, dm_ref, gw_ref, dcc_ref):
        a = _silu(a_ref[...])
        for l in range(DEPTH):
            gw_ref[l] = _dotf(a, dm_ref[l], (((0,), (0,)), ((), ())))
        dcc_ref[...] = _dotf(dm_ref[0, 8:16, :], w_ref[0], (((1,), (1,)), ((), ())))

    return pl.pallas_call(body, out_shape=(SDS(ada_w.shape, f32), SDS((8, ada_w.shape[1]), f32)), name=name,
                          compiler_params=_cparams())(a_raw, ada_w, dm)


def sum_parts(parts, name):
    _, r, c = parts.shape

    def body(p_ref, o_ref):
        acc = p_ref[0]
        for i in range(1, N_DEV):
            acc = acc + p_ref[i]
        o_ref[...] = acc

    return pl.pallas_call(body, out_shape=SDS((r, c), f32), name=name, compiler_params=_cparams())(parts)


def cctx_grad(parts, c_ctx, name):
    def body(p_ref, c_ref, o_ref):
        acc = p_ref[0, 0:1, :]
        for i in range(1, N_DEV):
            acc = acc + p_ref[i, 0:1, :]
        o_ref[...] = acc * _dsilu(c_ref[...])

    return pl.pallas_call(body, out_shape=SDS((1, c_ctx.shape[1]), f32), name=name, compiler_params=_cparams())(parts, c_ctx)


def _adamw_math(g, w, m, v):
    m = ADAM_B1 * m + (1.0 - ADAM_B1) * g
    v = ADAM_B2 * v + (1.0 - ADAM_B2) * (g * g)
    m_hat = m / (1.0 - ADAM_B1 ** ADAM_STEP)
    v_hat = v / (1.0 - ADAM_B2 ** ADAM_STEP)
    delta = -ADAM_LR * (m_hat / (jnp.sqrt(v_hat) + ADAM_EPS) + ADAM_WD * w)
    return delta, m, v


def adamw(parts, w, m, v, name):
    n, r, c = parts.shape
    tr = _pick(r, (256, 128, 64, 32, 16, 8))

    def body(p_ref, w_ref, m_ref, v_ref, g_ref, d_ref, nm_ref, nv_ref):
        g = p_ref[0].astype(f32)
        for i in range(1, n):
            g = g + p_ref[i].astype(f32)
        g_ref[...] = g
        d_ref[...], nm_ref[...], nv_ref[...] = _adamw_math(g, w_ref[...], m_ref[...], v_ref[...])

    blk = pl.BlockSpec((tr, c), lambda i: (i, 0))
    out = SDS((r, c), f32)
    return pl.pallas_call(
        body, out_shape=(out, out, out, out), grid=(r // tr,),
        in_specs=[pl.BlockSpec((n, tr, c), lambda i: (0, i, 0)), blk, blk, blk], out_specs=(blk, blk, blk, blk),
        name=name, compiler_params=_cparams(("parallel",)),
    )(parts, w, m, v)


def adamw_small(items, name):
    n = len(items)

    def body(*refs):
        ins, outs = refs[:4 * n], refs[4 * n:]
        for i in range(n):
            g, w, m, v = (ins[4 * i + j][...] for j in range(4))
            outs[3 * i][...], outs[3 * i + 1][...], outs[3 * i + 2][...] = _adamw_math(g, w, m, v)

    flat = [a for it in items for a in it]
    out_shape = tuple(SDS(it[1].shape, f32) for it in items for _ in range(3))
    res = pl.pallas_call(body, out_shape=out_shape, name=name, compiler_params=_cparams())(*flat)
    return [tuple(res[3 * i:3 * i + 3]) for i in range(n)]


def _unshard(g, axis):
    loc = g.shape[1:]
    return jnp.moveaxis(g, 0, axis).reshape(loc[:axis] + (N_DEV * loc[axis],) + loc[axis + 1:])


def _shard_major(full, axis):
    s = full.shape
    return jnp.moveaxis(full.reshape(s[:axis] + (N_DEV, s[axis] // N_DEV) + s[axis + 1:]), axis, 0)


def _my_block(full, axis, me):
    n = full.shape[axis] // N_DEV
    return lax.dynamic_slice_in_dim(full, me * n, n, axis)


def _pack(arrays):
    flat = [a.reshape(-1) for a in arrays]
    sizes = [f.shape[0] for f in flat]
    total = sum(sizes)
    padded = -(-total // (8 * LANE)) * (8 * LANE)
    flat.append(jnp.zeros((padded - total,), f32))
    offs = [sum(sizes[:i]) for i in range(len(sizes))]
    return jnp.concatenate(flat).reshape(padded // LANE, LANE), offs


def _pad_rows(w, n):
    return jnp.concatenate([w, jnp.zeros((n - w.shape[0],) + w.shape[1:], w.dtype)], 0)


def _gate_rows(bg):
    return bg[:, :16].reshape(bg.shape[0] // CHUNK, CHUNK, 16).transpose(0, 2, 1)


def _rows(vec, n):
    m = vec.reshape(n, 1, -1)
    return [m[i] for i in range(n)]


def kernel(x, c, ctx, c_ctx, ada_w, ada_b, ln_g, ln_b, even_w_in, even_w_out, gdn_conv_w, gdn_a_log, gdn_dt_bias, gdn_norm_w, pool_w, pool_scale, odd_w_in, odd_w_out, sconv_w, conf_conv_w, conf_ln_g, conf_ln_b, ffn_w_up, ffn_conv_w, ffn_w_down, loss_target, m_c_ctx, m_ada_w, m_ada_b, m_ln_g, m_ln_b, m_even_w_in, m_even_w_out, m_gdn_conv_w, m_gdn_a_log, m_gdn_dt_bias, m_gdn_norm_w, m_pool_w, m_pool_scale, m_odd_w_in, m_odd_w_out, m_sconv_w, m_conf_conv_w, m_conf_ln_g, m_conf_ln_b, m_ffn_w_up, m_ffn_conv_w, m_ffn_w_down, v_c_ctx, v_ada_w, v_ada_b, v_ln_g, v_ln_b, v_even_w_in, v_even_w_out, v_gdn_conv_w, v_gdn_a_log, v_gdn_dt_bias, v_gdn_norm_w, v_pool_w, v_pool_scale, v_odd_w_in, v_odd_w_out, v_sconv_w, v_conf_conv_w, v_conf_ln_g, v_conf_ln_b, v_ffn_w_up, v_ffn_conv_w, v_ffn_w_down):
    weights = dict(c_ctx=c_ctx, ada_w=ada_w, ada_b=ada_b, ln_g=ln_g, ln_b=ln_b, even_w_in=even_w_in, even_w_out=even_w_out, gdn_conv_w=gdn_conv_w, gdn_a_log=gdn_a_log, gdn_dt_bias=gdn_dt_bias, gdn_norm_w=gdn_norm_w, pool_w=pool_w, pool_scale=pool_scale, odd_w_in=odd_w_in, odd_w_out=odd_w_out, sconv_w=sconv_w, conf_conv_w=conf_conv_w, conf_ln_g=conf_ln_g, conf_ln_b=conf_ln_b, ffn_w_up=ffn_w_up, ffn_conv_w=ffn_conv_w, ffn_w_down=ffn_w_down)
    mom1 = dict(c_ctx=m_c_ctx, ada_w=m_ada_w, ada_b=m_ada_b, ln_g=m_ln_g, ln_b=m_ln_b, even_w_in=m_even_w_in, even_w_out=m_even_w_out, gdn_conv_w=m_gdn_conv_w, gdn_a_log=m_gdn_a_log, gdn_dt_bias=m_gdn_dt_bias, gdn_norm_w=m_gdn_norm_w, pool_w=m_pool_w, pool_scale=m_pool_scale, odd_w_in=m_odd_w_in, odd_w_out=m_odd_w_out, sconv_w=m_sconv_w, conf_conv_w=m_conf_conv_w, conf_ln_g=m_conf_ln_g, conf_ln_b=m_conf_ln_b, ffn_w_up=m_ffn_w_up, ffn_conv_w=m_ffn_conv_w, ffn_w_down=m_ffn_w_down)
    mom2 = dict(c_ctx=v_c_ctx, ada_w=v_ada_w, ada_b=v_ada_b, ln_g=v_ln_g, ln_b=v_ln_b, even_w_in=v_even_w_in, even_w_out=v_even_w_out, gdn_conv_w=v_gdn_conv_w, gdn_a_log=v_gdn_a_log, gdn_dt_bias=v_gdn_dt_bias, gdn_norm_w=v_gdn_norm_w, pool_w=v_pool_w, pool_scale=v_pool_scale, odd_w_in=v_odd_w_in, odd_w_out=v_odd_w_out, sconv_w=v_sconv_w, conf_conv_w=v_conf_conv_w, conf_ln_g=v_conf_ln_g, conf_ln_b=v_conf_ln_b, ffn_w_up=v_ffn_w_up, ffn_conv_w=v_ffn_conv_w, ffn_w_down=v_ffn_w_down)
    order = list(weights)
    me = 4 * lax.axis_index("x") + 2 * lax.axis_index("y") + lax.axis_index("c")
    x, ctx, target = x[0], ctx[0], loss_target[0]
    t, d = x.shape
    tc = ctx.shape[0]

    small_in = [ln_g, ln_b, gdn_conv_w, sconv_w, conf_conv_w, ffn_conv_w, c]
    small_axes = [2, 2, 1, 1, 1, 3, 0]
    small_pack, small_offs = _pack(small_in)
    wire = [w.astype(bf16) for w in (even_w_in, even_w_out, odd_w_in, odd_w_out, ffn_w_up, ffn_w_down)]
    gath = exchange(wire + [small_pack], False, "gather_weights")
    e_in = even_w_in.shape[1] * N_DEV
    e_pad = -(-e_in // LANE) * LANE
    win_e = jnp.pad(_unshard(gath[0], 1), ((0, 0), (0, e_pad - e_in)))
    wout_e = _unshard(gath[1], 0)
    win_o = _unshard(gath[2], 1)
    wout_o = _unshard(gath[3], 0)
    wup = _unshard(gath[4], 2)
    wdown = _unshard(gath[5], 1)
    sm = gath[6].reshape(N_DEV, -1)
    lng_f, lnb_f, gconv_f, sconv_f, cconv_f, fconv_f, c_all = [
        _unshard(sm[:, o:o + a.size].reshape((N_DEV,) + a.shape), ax) for a, o, ax in zip(small_in, small_offs, small_axes)]
    gw8 = _pad_rows(gconv_f, 8)
    sw8 = _pad_rows(sconv_f, 8)
    cw32 = _pad_rows(cconv_f, 32)
    fw16 = [_pad_rows(fconv_f[l].reshape(9, D_FF), 16) for l in range(DEPTH)]

    a_raw = jnp.concatenate([c_all, c_ctx[None], jnp.zeros((7, d), f32)], 0)
    ncol = ada_w.shape[2]
    ada_b_loc = lax.dynamic_slice_in_dim(ada_b, me * ncol, ncol, 1)[:, None, :]
    modpart = ada_forward(a_raw, ada_w, ada_b_loc, "ada_forward")
    mod_send = jnp.stack([jnp.transpose(modpart[:, :N_DEV], (1, 0, 2)),
                          jnp.broadcast_to(modpart[:, N_DEV][None], (N_DEV, DEPTH, ncol))], axis=2)
    mod_recv = exchange([mod_send], True, "scatter_mod")[0]
    mod = jnp.transpose(mod_recv[:, :, 0, :], (1, 0, 2)).reshape(DEPTH, 6 * d)
    modc = mod_recv[:, 0, 1, :].reshape(6 * d)
    sh_c, sc_c = modc[None, :d], modc[None, d:2 * d]
    mods = [_rows(mod[l], 6) for l in range(DEPTH)]
    lng = [[lng_f[l, j][None] for j in range(2)] for l in range(DEPTH)]
    lnb = [[lnb_f[l, j][None] for j in range(2)] for l in range(DEPTH)]

    neg_a = jnp.zeros((1, LANE), f32).at[0, 8:16].set(-jnp.exp(gdn_a_log).reshape(8))
    dt_row = jnp.zeros((1, LANE), f32).at[0, 8:16].set(gdn_dt_bias.reshape(8))
    nw_row, ps_row = gdn_norm_w[None], pool_scale[None]
    cg_row, cb_row = conf_ln_g[None], conf_ln_b[None]
    q_scale = GDN_DK ** -0.5

    sh_m, sc_m, gt_m, sh_f, sc_f, gt_f = mods[0]
    u0 = modulate(x, sc_m, sh_m, "mod_l0_mix")
    cu = modulate(ctx, sc_c, sh_c, "mod_ctx")
    p0 = matmul(u0, win_e, "nn", f32, "even_in")
    pc = matmul(cu, win_e, "nn", f32, "even_in_ctx")
    qn = gdn_conv(p0, gw8, 0, 4, q_scale, "gdn_conv_q")
    kn = gdn_conv(p0, gw8, 4, 4, 1.0, "gdn_conv_k")
    vv = gdn_conv(p0, gw8, 8, 4, None, "gdn_conv_v")
    kc = gdn_conv(pc, gw8, 4, 4, 1.0, "gdn_conv_k_ctx")
    vc = gdn_conv(pc, gw8, 8, 4, None, "gdn_conv_v_ctx")
    bg = gdn_gates(p0, neg_a, dt_row, "gdn_gates")
    bgc = gdn_gates(pc, neg_a, dt_row, "gdn_gates_ctx")
    bgt, bgtc = _gate_rows(bg), _gate_rows(bgc)
    zero_state = jnp.zeros((2, GDN_HEADS, LANE, LANE), f32)
    _, _, sallc_f, sallc_b, sfin_c = gdn_forward(kc, kc, vc, bgc, bgtc, zero_state, False, "gdn_fwd_ctx")
    o_f, o_b, sall_f, sall_b, _ = gdn_forward(qn, kn, vv, bg, bgt, sfin_c, True, "gdn_fwd")
    mix0 = jnp.concatenate([gated_rmsnorm(o_f, o_b, p0, nw_row, "gated_rmsnorm"),
                            pool_mix(p0, pool_w, ps_row, "pool_mix")], 1)
    y0 = matmul(mix0, wout_e, "nn", f32, "even_out")
    x1 = res_layernorm(x, y0, gt_m, lng[0][0], lnb[0][0], "resln_l0_mix")
    u1 = modulate(x1, sc_f, sh_f, "mod_l0_ffn")
    h0 = matmul(u1, wup[0], "nn", f32, "ffn_up_l0")
    f0 = ffn_conv(h0, fw16[0], "ffn_conv_l0")
    y0f = matmul(f0, wdown[0], "nn", f32, "ffn_down_l0")
    x2 = res_layernorm(x1, y0f, gt_f, lng[0][1], lnb[0][1], "resln_l0_ffn")

    sh_m1, sc_m1, gt_m1, sh_f1, sc_f1, gt_f1 = mods[1]
    u2 = modulate(x2, sc_m1, sh_m1, "mod_l1_mix")
    p1 = matmul(u2, win_o, "nn", f32, "odd_in")
    zc = conf_conv(p1, cw32, "conf_conv")
    mix1 = jnp.concatenate([short_conv(p1, sw8, "short_conv"), ln_silu(zc, cg_row, cb_row, "conf_ln_silu")], 1)
    y1 = matmul(mix1, wout_o, "nn", f32, "odd_out")
    x3 = res_layernorm(x2, y1, gt_m1, lng[1][0], lnb[1][0], "resln_l1_mix")
    u3 = modulate(x3, sc_f1, sh_f1, "mod_l1_ffn")
    h1 = matmul(u3, wup[1], "nn", f32, "ffn_up_l1")
    f1 = ffn_conv(h1, fw16[1], "ffn_conv_l1")
    y1f = matmul(f1, wdown[1], "nn", f32, "ffn_down_l1")
    x4 = res_layernorm(x3, y1f, gt_f1, lng[1][1], lnb[1][1], "resln_l1_ffn")

    loss_row, dx4 = loss_head(x4, target, "loss_head")
    loss = lax.psum(loss_row[0, 0], ("x", "y", "c"))

    def ffn_backward(dout, x_in, y, gate, g_row, scale, u, h, f, l):
        dxr, dy, dgt, dlg, dlb = res_layernorm_bwd(dout, x_in, y, gate, g_row, f"resln_bwd_l{l}_ffn")
        df = matmul(dy, wdown[l], "nt", f32, f"ffn_down_dgrad_l{l}")
        g_down = matmul(f, dy, "tn", bf16, f"ffn_down_wgrad_l{l}")
        dh, dcw = ffn_conv_bwd(h, fw16[l], df, f"ffn_conv_bwd_l{l}")
        du = matmul(dh, wup[l], "nt", f32, f"ffn_up_dgrad_l{l}")
        g_up = matmul(u, dh, "tn", bf16, f"ffn_up_wgrad_l{l}")
        dx_in, dsc, dsh = modulate_bwd(du, x_in, scale, dxr, f"mod_bwd_l{l}_ffn")
        return dx_in, (dsh, dsc, dgt), (dlg, dlb), dcw, g_up, g_down

    dx3, dmod_f1, dln_f1, dfcw1, g_up1, g_down1 = ffn_backward(dx4, x3, y1f, gt_f1, lng[1][1], sc_f1, u3, h1, f1, 1)

    dxr, dy, dgt, dlg, dlb = res_layernorm_bwd(dx3, x2, y1, gt_m1, lng[1][0], "resln_bwd_l1_mix")
    dln_m1 = (dlg, dlb)
    dmix = matmul(dy, wout_o, "nt", f32, "odd_out_dgrad")
    g_wout_o = matmul(mix1, dy, "tn", bf16, "odd_out_wgrad")
    dgb, dgc, dhh, d_sconv = short_conv_bwd(p1, sw8, dmix, "short_conv_bwd")
    dzc, d_cg, d_cb = ln_silu_bwd(zc, cg_row, cb_row, dmix, "conf_ln_silu_bwd")
    dga, dgbb, d_cconv = conf_conv_bwd(p1, cw32, dzc, "conf_conv_bwd")
    dp1 = jnp.concatenate([dgb, dgc, dhh, dga, dgbb], 1)
    du = matmul(dp1, win_o, "nt", f32, "odd_in_dgrad")
    g_win_o = matmul(u2, dp1, "tn", bf16, "odd_in_wgrad")
    dx2, dsc, dsh = modulate_bwd(du, x2, sc_m1, dxr, "mod_bwd_l1_mix")
    dmod_m1 = (dsh, dsc, dgt)

    dx1, dmod_f0, dln_f0, dfcw0, g_up0, g_down0 = ffn_backward(dx2, x1, y0f, gt_f, lng[0][1], sc_f, u1, h0, f0, 0)

    dxr, dy, dgt, dlg, dlb = res_layernorm_bwd(dx1, x, y0, gt_m, lng[0][0], "resln_bwd_l0_mix")
    dln_m0 = (dlg, dlb)
    dmix = matmul(dy, wout_e, "nt", f32, "even_out_dgrad")
    g_wout_e = matmul(mix0, dy, "tn", bf16, "even_out_wgrad")
    d_o, dgate, d_nw = gated_rmsnorm_bwd(o_f, o_b, p0, nw_row, dmix, "gated_rmsnorm_bwd")
    dpool, d_pw, d_ps = pool_mix_bwd(p0, pool_w, ps_row, dmix, "pool_mix_bwd")
    dq_f, dq_b, dk_f, dk_b, dv_f, dv_b, dbg_f, dbg_b, ds0 = gdn_backward(
        qn, kn, vv, bg, bgt, sall_f, sall_b, d_o, zero_state, True, "gdn_bwd")
    _, _, dkc_f, dkc_b, dvc_f, dvc_b, dbgc_f, dbgc_b, _ = gdn_backward(
        kc, kc, vc, bgc, bgtc, sallc_f, sallc_b, jnp.zeros((tc, 512), f32), ds0, False, "gdn_bwd_ctx")
    dqp, dwq = gdn_conv_bwd(p0, gw8, dq_f, dq_b, 0, 4, q_scale, "gdn_conv_q_bwd")
    dkp, dwk = gdn_conv_bwd(p0, gw8, dk_f, dk_b, 4, 4, 1.0, "gdn_conv_k_bwd")
    dvp, dwv = gdn_conv_bwd(p0, gw8, dv_f, dv_b, 8, 4, None, "gdn_conv_v_bwd")
    dkcp, dwkc = gdn_conv_bwd(pc, gw8, dkc_f, dkc_b, 4, 4, 1.0, "gdn_conv_k_ctx_bwd")
    dvcp, dwvc = gdn_conv_bwd(pc, gw8, dvc_f, dvc_b, 8, 4, None, "gdn_conv_v_ctx_bwd")
    ds_l, da_l, ddt_l = gdn_gates_bwd(p0, neg_a, dt_row, dbg_f, dbg_b, "gdn_gates_bwd")
    ds_c, da_c, ddt_c = gdn_gates_bwd(pc, neg_a, dt_row, dbgc_f, dbgc_b, "gdn_gates_ctx_bwd")
    zc512 = jnp.zeros((tc, 512), bf16)
    dp_all = jnp.concatenate([
        jnp.concatenate([dqp, dkp, dvp, dgate, dpool, ds_l], 1),
        jnp.concatenate([zc512, dkcp, dvcp, zc512, zc512, ds_c], 1)], 0)
    u_all = jnp.concatenate([u0, cu], 0)
    du_all = matmul(dp_all, win_e, "nt", f32, "even_in_dgrad")
    g_win_e = matmul(u_all, dp_all, "tn", bf16, "even_in_wgrad")[:, :e_in]
    grad_x, dsc, dsh = modulate_bwd(du_all, x, sc_m, dxr, "mod_bwd_l0_mix")
    dmod_m0 = (dsh, dsc, dgt)
    _, dsc_c, dsh_c = modulate_bwd(du_all, ctx, sc_c, jnp.zeros((tc, d), f32), "mod_bwd_ctx", du_row0=t)

    dmod0 = jnp.concatenate(dmod_m0 + dmod_f0, 1)
    dmod1 = jnp.concatenate(dmod_m1 + dmod_f1, 1)
    dmodc = jnp.concatenate([dsh_c, dsc_c], 1)
    d_gconv = jnp.concatenate([dwq, dwk + dwkc, dwv + dwvc], 1)[:5]
    small_g = [dmod0, dmod1, dmodc,
               jnp.concatenate([dln_m0[0], dln_f0[0], dln_m1[0], dln_f1[0]], 0),
               jnp.concatenate([dln_m0[1], dln_f0[1], dln_m1[1], dln_f1[1]], 0),
               d_gconv, (da_l + da_c)[0, 8:16], (ddt_l + ddt_c)[0, 8:16], d_nw, d_pw, d_ps,
               d_sconv[:3], d_cconv[:31], d_cg, d_cb, jnp.stack([dfcw0[:9], dfcw1[:9]])]
    gpack, goffs = _pack(small_g)
    gparts = exchange([gpack], False, "gather_small_grads")[0]
    gsum = sum_parts(gparts, "sum_small_grads").reshape(-1)
    gs = [gsum[o:o + a.size].reshape(a.shape) for a, o in zip(small_g, goffs)]
    gflat = gparts.reshape(N_DEV, -1)
    dmodc_cols = _my_block(jnp.pad(gs[2], ((0, 0), (0, 4 * d))), 1, me)
    dm = jnp.stack([
        jnp.concatenate([_my_block(gflat[:, goffs[0]:goffs[0] + 6 * d], 1, me), dmodc_cols, jnp.zeros((7, ncol), f32)], 0),
        jnp.concatenate([_my_block(gflat[:, goffs[1]:goffs[1] + 6 * d], 1, me), jnp.zeros((8, ncol), f32)], 0)])
    g_ada_w, dcc = ada_backward(a_raw, ada_w, dm, "ada_backward")
    g_cctx = cctx_grad(exchange([dcc], False, "gather_cctx")[0], c_ctx[None], "cctx_grad")

    grads = {}
    grads["c_ctx"] = g_cctx.reshape(c_ctx.shape)
    grads["ada_b"] = jnp.concatenate([gs[0] + jnp.pad(gs[2], ((0, 0), (0, 4 * d))), gs[1]], 0)
    grads["ln_g"] = _my_block(gs[3].reshape(DEPTH, 2, d), 2, me)
    grads["ln_b"] = _my_block(gs[4].reshape(DEPTH, 2, d), 2, me)
    grads["gdn_conv_w"] = _my_block(gs[5], 1, me)
    grads["gdn_a_log"] = gs[6].reshape(2, GDN_HEADS)
    grads["gdn_dt_bias"] = gs[7].reshape(2, GDN_HEADS)
    grads["gdn_norm_w"] = gs[8].reshape(LANE)
    grads["pool_w"] = gs[9]
    grads["pool_scale"] = gs[10].reshape(-1)
    grads["sconv_w"] = _my_block(gs[11], 1, me)
    grads["conf_conv_w"] = _my_block(gs[12], 1, me)
    grads["conf_ln_g"] = gs[13].reshape(-1)
    grads["conf_ln_b"] = gs[14].reshape(-1)
    grads["ffn_conv_w"] = _my_block(gs[15].reshape(DEPTH, 3, 3, D_FF), 3, me)

    def as2d(a):
        return a.reshape(-1, a.shape[-1]) if a.ndim > 1 else a.reshape(1, -1)

    small_names = [n for n in order if n in grads]
    res = adamw_small([(as2d(grads[n]), as2d(weights[n]), as2d(mom1[n]), as2d(mom2[n])) for n in small_names], "adamw_small")
    delta, new_m, new_v = {}, {}, {}
    for n, (dl, nm, nv) in zip(small_names, res):
        delta[n], new_m[n], new_v[n] = (a.reshape(weights[n].shape) for a in (dl, nm, nv))

    big = [("even_w_in", g_win_e, 1), ("even_w_out", g_wout_e, 0), ("odd_w_in", g_win_o, 1), ("odd_w_out", g_wout_o, 0),
           ("ffn_w_up", jnp.stack([g_up0, g_up1]), 2), ("ffn_w_down", jnp.stack([g_down0, g_down1]), 1)]
    recv = exchange([_shard_major(g, ax) for _, g, ax in big], True, "scatter_grads")
    big_parts = {n: r for (n, _, _), r in zip(big, recv)}
    big_parts["ada_w"] = g_ada_w[None]
    for n, parts in big_parts.items():
        w = weights[n]
        cols = w.shape[-1]
        out = adamw(parts.reshape(parts.shape[0], -1, cols), w.reshape(-1, cols), mom1[n].reshape(-1, cols),
                    mom2[n].reshape(-1, cols), f"adamw_{n}")
        grads[n], delta[n], new_m[n], new_v[n] = (a.reshape(w.shape) for a in out)

    return (loss, grad_x[None], *[grads[n] for n in order], *[delta[n] for n in order],
            *[new_m[n] for n in order], *[new_v[n] for n in order])
```

```python
import functools
import math

import jax
import jax.numpy as jnp
from jax import lax
from jax.experimental import pallas as pl
from jax.experimental.pallas import tpu as pltpu

f32 = jnp.float32
bf16 = jnp.bfloat16
SDS = jax.ShapeDtypeStruct

N_DEV = 8
D_MODEL = 1024
DEPTH = 2
GRID_W = 64
GDN_HEADS = 4
GDN_DK = 128
CHUNK = 64
POOL_WINDOWS = (2, 4, 8, 16)
D_FF = 2816
ALPHA = (2 * DEPTH) ** 0.25
LN_EPS = 1e-5
RMS_EPS = 1e-6
LANE = 128
PAD_ROWS = 72
CONV_ROWS = 256
VMEM_LIMIT = 56 * 2**20

ADAM_LR, ADAM_B1, ADAM_B2, ADAM_EPS, ADAM_WD, ADAM_STEP = 0.001, 0.9, 0.999, 1e-08, 0.01, 10

HI = lax.Precision.HIGHEST


def _cparams(sem=None):
    return pltpu.CompilerParams(dimension_semantics=sem, vmem_limit_bytes=VMEM_LIMIT)


def _silu(x):
    return x * jax.nn.sigmoid(x)


def _dsilu(x):
    s = jax.nn.sigmoid(x)
    return s * (1.0 + x * (1.0 - s))


def _dotb(a, b, dims=(((1,), (0,)), ((), ()))):
    return lax.dot_general(a.astype(bf16), b.astype(bf16), dims, preferred_element_type=f32)


def _dotb_nt(a, b):
    return _dotb(a, b, (((1,), (1,)), ((), ())))


def _dotb_tn(a, b):
    return _dotb(a, b, (((0,), (0,)), ((), ())))


def _dotf(a, b, dims=(((1,), (0,)), ((), ()))):
    return lax.dot_general(a, b, dims, preferred_element_type=f32, precision=HI)


def _pick(n, cands):
    for c in cands:
        if n % c == 0:
            return c
    return n


def matmul(a, b, mode, out_dtype, name):
    if mode == "nn":
        (M, K), N = a.shape, b.shape[1]
    elif mode == "nt":
        (M, K), N = a.shape, b.shape[0]
    else:
        (K, M), N = a.shape, b.shape[1]
    tm = _pick(M, (1024, 768, 512, 256, 128)) if mode != "tn" else _pick(M, (1024, 512, 256, 128))
    tn = _pick(N, (1024, 896, 768, 640, 512, 384, 256, 128))
    tk = _pick(K, (1024, 896, 768, 640, 512, 384, 256, 128)) if mode != "tn" else _pick(K, (1024, 512, 256))
    nk = K // tk
    dims = {"nn": (((1,), (0,)), ((), ())), "nt": (((1,), (1,)), ((), ())), "tn": (((0,), (0,)), ((), ()))}[mode]

    def body(a_ref, b_ref, o_ref, acc_ref):
        k = pl.program_id(2)
        part = lax.dot_general(a_ref[...].astype(bf16), b_ref[...].astype(bf16), dims, preferred_element_type=f32)

        @pl.when(k == 0)
        def _():
            acc_ref[...] = part

        @pl.when(k > 0)
        def _():
            acc_ref[...] += part

        @pl.when(k == nk - 1)
        def _():
            o_ref[...] = acc_ref[...].astype(out_dtype)

    a_spec = {"nn": pl.BlockSpec((tm, tk), lambda i, j, k: (i, k)),
              "nt": pl.BlockSpec((tm, tk), lambda i, j, k: (i, k)),
              "tn": pl.BlockSpec((tk, tm), lambda i, j, k: (k, i))}[mode]
    b_spec = {"nn": pl.BlockSpec((tk, tn), lambda i, j, k: (k, j)),
              "nt": pl.BlockSpec((tn, tk), lambda i, j, k: (j, k)),
              "tn": pl.BlockSpec((tk, tn), lambda i, j, k: (k, j))}[mode]
    return pl.pallas_call(
        body, out_shape=SDS((M, N), out_dtype), grid=(M // tm, N // tn, nk),
        in_specs=[a_spec, b_spec], out_specs=pl.BlockSpec((tm, tn), lambda i, j, k: (i, j)),
        scratch_shapes=[pltpu.VMEM((tm, tn), f32)], name=name,
        compiler_params=_cparams(("parallel", "parallel", "arbitrary")),
    )(a, b)


def _row_tile(t):
    return _pick(t, (512, 256, 128, 64, 32, 16, 8))


def _row_spec(tt, d):
    return pl.BlockSpec((tt, d), lambda i: (i, 0))


def _vec_spec(d):
    return pl.BlockSpec((1, d), lambda i: (0, 0))


def _acc_rows(ref, val):
    @pl.when(pl.program_id(0) == 0)
    def _():
        ref[...] = val

    @pl.when(pl.program_id(0) > 0)
    def _():
        ref[...] += val


def modulate(x, scale, shift, name):
    t, d = x.shape
    tt = _row_tile(t)

    def body(x_ref, sc_ref, sh_ref, o_ref):
        o_ref[...] = (x_ref[...] * (1.0 + sc_ref[...]) + sh_ref[...]).astype(bf16)

    return pl.pallas_call(
        body, out_shape=SDS((t, d), bf16), grid=(t // tt,),
        in_specs=[_row_spec(tt, d), _vec_spec(d), _vec_spec(d)], out_specs=_row_spec(tt, d),
        name=name, compiler_params=_cparams(("parallel",)),
    )(x, scale, shift)


def modulate_bwd(du, x, scale, dres, name, du_row0=0):
    t, d = x.shape
    tt = _row_tile(t)
    blk0 = du_row0 // tt

    def body(du_ref, x_ref, sc_ref, dres_ref, dx_ref, dsc_ref, dsh_ref):
        du_v = du_ref[...]
        dx_ref[...] = du_v * (1.0 + sc_ref[...]) + dres_ref[...]
        _acc_rows(dsc_ref, jnp.sum(du_v * x_ref[...], axis=0, keepdims=True))
        _acc_rows(dsh_ref, jnp.sum(du_v, axis=0, keepdims=True))

    return pl.pallas_call(
        body, out_shape=(SDS((t, d), f32), SDS((1, d), f32), SDS((1, d), f32)), grid=(t // tt,),
        in_specs=[pl.BlockSpec((tt, d), lambda i: (i + blk0, 0)), _row_spec(tt, d), _vec_spec(d), _row_spec(tt, d)],
        out_specs=(_row_spec(tt, d), _vec_spec(d), _vec_spec(d)),
        name=name, compiler_params=_cparams(("arbitrary",)),
    )(du, x, scale, dres)


def _ln_stats(z):
    mu = jnp.mean(z, axis=-1, keepdims=True)
    zc = z - mu
    var = jnp.mean(zc * zc, axis=-1, keepdims=True)
    rstd = lax.rsqrt(var + LN_EPS)
    return zc * rstd, rstd


def _ln_bwd(dxhat, xhat, rstd):
    m1 = jnp.mean(dxhat, axis=-1, keepdims=True)
    m2 = jnp.mean(dxhat * xhat, axis=-1, keepdims=True)
    return rstd * (dxhat - m1 - xhat * m2)


def res_layernorm(x, y, gate, g, b, name):
    t, d = x.shape
    tt = _row_tile(t)

    def body(x_ref, y_ref, gt_ref, g_ref, b_ref, o_ref):
        xhat, _ = _ln_stats(ALPHA * x_ref[...] + gt_ref[...] * y_ref[...])
        o_ref[...] = xhat * g_ref[...] + b_ref[...]

    return pl.pallas_call(
        body, out_shape=SDS((t, d), f32), grid=(t // tt,),
        in_specs=[_row_spec(tt, d), _row_spec(tt, d), _vec_spec(d), _vec_spec(d), _vec_spec(d)],
        out_specs=_row_spec(tt, d), name=name, compiler_params=_cparams(("parallel",)),
    )(x, y, gate, g, b)


def res_layernorm_bwd(dout, x, y, gate, g, name):
    t, d = x.shape
    tt = _row_tile(t)

    def body(do_ref, x_ref, y_ref, gt_ref, g_ref, dxr_ref, dy_ref, dgt_ref, dg_ref, db_ref):
        y_v = y_ref[...]
        do_v = do_ref[...]
        xhat, rstd = _ln_stats(ALPHA * x_ref[...] + gt_ref[...] * y_v)
        dz = _ln_bwd(do_v * g_ref[...], xhat, rstd)
        dxr_ref[...] = ALPHA * dz
        dy_ref[...] = (gt_ref[...] * dz).astype(bf16)
        _acc_rows(dgt_ref, jnp.sum(dz * y_v, axis=0, keepdims=True))
        _acc_rows(dg_ref, jnp.sum(do_v * xhat, axis=0, keepdims=True))
        _acc_rows(db_ref, jnp.sum(do_v, axis=0, keepdims=True))

    vec = SDS((1, d), f32)
    return pl.pallas_call(
        body, out_shape=(SDS((t, d), f32), SDS((t, d), bf16), vec, vec, vec), grid=(t // tt,),
        in_specs=[_row_spec(tt, d), _row_spec(tt, d), _row_spec(tt, d), _vec_spec(d), _vec_spec(d)],
        out_specs=(_row_spec(tt, d), _row_spec(tt, d), _vec_spec(d), _vec_spec(d), _vec_spec(d)),
        name=name, compiler_params=_cparams(("arbitrary",)),
    )(dout, x, y, gate, g)


def loss_head(y, target, name):
    t, d = y.shape
    tt = _row_tile(t)

    def body(y_ref, t_ref, l_ref, dy_ref):
        e = y_ref[...] - t_ref[...]
        dy_ref[...] = e * (1.0 / d)
        part = jnp.sum(jnp.sum(e * e, axis=1, keepdims=True), axis=0, keepdims=True) * (0.5 / d)
        _acc_rows(l_ref, jnp.broadcast_to(part, (1, LANE)))

    return pl.pallas_call(
        body, out_shape=(SDS((1, LANE), f32), SDS((t, d), f32)), grid=(t // tt,),
        in_specs=[_row_spec(tt, d), _row_spec(tt, d)],
        out_specs=(pl.BlockSpec((1, LANE), lambda i: (0, 0)), _row_spec(tt, d)),
        name=name, compiler_params=_cparams(("arbitrary",)),
    )(y, target)


def _fill_pad(pad_ref, val, t):
    zeros = jnp.zeros((PAD_ROWS, LANE), f32)
    pad_ref[0:PAD_ROWS, :] = zeros
    pad_ref[PAD_ROWS + t:2 * PAD_ROWS + t, :] = zeros
    pad_ref[PAD_ROWS:PAD_ROWS + t, :] = val


def _grid_mask(r0, rows, dc):
    col = (lax.broadcasted_iota(jnp.int32, (rows, 1), 0) + r0) % GRID_W
    return ((col + dc >= 0) & (col + dc < GRID_W)).astype(f32)


def _taps_apply(pad_ref, w_ref, taps, r0, rows):
    acc = jnp.zeros((rows, LANE), f32)
    for off, dc, wi in taps:
        xs = pad_ref[PAD_ROWS + r0 + off:PAD_ROWS + r0 + off + rows, :]
        if dc is not None and dc != 0:
            xs = xs * _grid_mask(r0, rows, dc)
        acc = acc + w_ref[wi:wi + 1, :] * xs
    return acc


def _taps_wgrad(pad_ref, dy, taps, r0, rows, nw):
    out = jnp.zeros((nw, LANE), f32)
    rid = lax.broadcasted_iota(jnp.int32, (nw, 1), 0)
    for off, dc, wi in taps:
        xs = pad_ref[PAD_ROWS + r0 + off:PAD_ROWS + r0 + off + rows, :]
        if dc is not None and dc != 0:
            xs = xs * _grid_mask(r0, rows, dc)
        s = jnp.sum(dy * xs, axis=0, keepdims=True)
        out = out + jnp.where(rid == wi, s, 0.0)
    return out


def _transpose_taps(taps):
    return [(-off, None if dc is None else -dc, wi) for off, dc, wi in taps]


def _taps_1d(width):
    return [(j - width // 2, None, j) for j in range(width)]


def _taps_grid3():
    return [(GRID_W * dr + dc, dc, 3 * (dr + 1) + (dc + 1)) for dr in (-1, 0, 1) for dc in (-1, 0, 1)]


def _row_chunks(t):
    r = min(CONV_ROWS, t)
    return [(i * r, r) for i in range(t // r)]


def _col_spec(t, off):
    return pl.BlockSpec((t, LANE), lambda c: (0, c + off))


def _w_spec(nw, off=0):
    return pl.BlockSpec((nw, LANE), lambda c: (0, c + off))


def gdn_conv(p, w, col0, nblk, norm_scale, name):
    t = p.shape[0]
    nw = w.shape[0]
    taps = _taps_1d(5)

    def body(p_ref, w_ref, o_ref, pad_ref):
        _fill_pad(pad_ref, p_ref[...], t)
        for r0, rows in _row_chunks(t):
            a = _silu(_taps_apply(pad_ref, w_ref, taps, r0, rows))
            if norm_scale is not None:
                a = a * (lax.rsqrt(jnp.sum(a * a, axis=-1, keepdims=True) + RMS_EPS) * norm_scale)
            o_ref[r0:r0 + rows, :] = a

    return pl.pallas_call(
        body, out_shape=SDS((t, nblk * LANE), f32), grid=(nblk,),
        in_specs=[_col_spec(t, col0), _w_spec(nw, col0)], out_specs=_col_spec(t, 0),
        scratch_shapes=[pltpu.VMEM((t + 2 * PAD_ROWS, LANE), f32)], name=name,
        compiler_params=_cparams(("parallel",)),
    )(p, w)


def gdn_conv_bwd(p, w, d_a, d_b, col0, nblk, norm_scale, name):
    t = p.shape[0]
    nw = w.shape[0]
    taps = _taps_1d(5)
    ttaps = _transpose_taps(taps)

    def body(p_ref, w_ref, da_ref, db_ref, dp_ref, dw_ref, pad_ref, gpad_ref):
        _fill_pad(pad_ref, p_ref[...], t)
        for r0, rows in _row_chunks(t):
            pre = _taps_apply(pad_ref, w_ref, taps, r0, rows)
            a = _silu(pre)
            dy = da_ref[r0:r0 + rows, :] + db_ref[r0:r0 + rows, :]
            if norm_scale is not None:
                r = lax.rsqrt(jnp.sum(a * a, axis=-1, keepdims=True) + RMS_EPS)
                da = norm_scale * (dy * r - a * (r * r * r) * jnp.sum(dy * a, axis=-1, keepdims=True))
            else:
                da = dy
            gpad_ref[PAD_ROWS + r0:PAD_ROWS + r0 + rows, :] = da * _dsilu(pre)
        zeros = jnp.zeros((PAD_ROWS, LANE), f32)
        gpad_ref[0:PAD_ROWS, :] = zeros
        gpad_ref[PAD_ROWS + t:2 * PAD_ROWS + t, :] = zeros
        dw = jnp.zeros((nw, LANE), f32)
        for r0, rows in _row_chunks(t):
            dp_ref[r0:r0 + rows, :] = _taps_apply(gpad_ref, w_ref, ttaps, r0, rows).astype(bf16)
            dw = dw + _taps_wgrad(pad_ref, gpad_ref[PAD_ROWS + r0:PAD_ROWS + r0 + rows, :], taps, r0, rows, nw)
        dw_ref[...] = dw

    return pl.pallas_call(
        body, out_shape=(SDS((t, nblk * LANE), bf16), SDS((nw, nblk * LANE), f32)), grid=(nblk,),
        in_specs=[_col_spec(t, col0), _w_spec(nw, col0), _col_spec(t, 0), _col_spec(t, 0)],
        out_specs=(_col_spec(t, 0), _w_spec(nw)),
        scratch_shapes=[pltpu.VMEM((t + 2 * PAD_ROWS, LANE), f32)] * 2, name=name,
        compiler_params=_cparams(("parallel",)),
    )(p, w, d_a, d_b)


def short_conv(p, w, name):
    t = p.shape[0]
    nw = w.shape[0]
    taps = _taps_1d(3)

    def body(gb_ref, gc_ref, h_ref, w_ref, o_ref, pad_ref):
        _fill_pad(pad_ref, gc_ref[...] * h_ref[...], t)
        for r0, rows in _row_chunks(t):
            o_ref[r0:r0 + rows, :] = (gb_ref[r0:r0 + rows, :] * _taps_apply(pad_ref, w_ref, taps, r0, rows)).astype(bf16)

    return pl.pallas_call(
        body, out_shape=SDS((t, 4 * LANE), bf16), grid=(4,),
        in_specs=[_col_spec(t, 0), _col_spec(t, 4), _col_spec(t, 8), _w_spec(nw)], out_specs=_col_spec(t, 0),
        scratch_shapes=[pltpu.VMEM((t + 2 * PAD_ROWS, LANE), f32)], name=name,
        compiler_params=_cparams(("parallel",)),
    )(p, p, p, w)


def short_conv_bwd(p, w, dy, name):
    t = p.shape[0]
    nw = w.shape[0]
    taps = _taps_1d(3)
    ttaps = _transpose_taps(taps)

    def body(gb_ref, gc_ref, h_ref, w_ref, dy_ref, dgb_ref, dgc_ref, dh_ref, dw_ref, pad_ref, gpad_ref):
        _fill_pad(pad_ref, gc_ref[...] * h_ref[...], t)
        _fill_pad(gpad_ref, dy_ref[...] * gb_ref[...], t)
        dw = jnp.zeros((nw, LANE), f32)
        for r0, rows in _row_chunks(t):
            sl = slice(r0, r0 + rows)
            dgb_ref[sl, :] = (dy_ref[sl, :] * _taps_apply(pad_ref, w_ref, taps, r0, rows)).astype(bf16)
            dm = _taps_apply(gpad_ref, w_ref, ttaps, r0, rows)
            dgc_ref[sl, :] = (dm * h_ref[sl, :]).astype(bf16)
            dh_ref[sl, :] = (dm * gc_ref[sl, :]).astype(bf16)
            dw = dw + _taps_wgrad(pad_ref, gpad_ref[PAD_ROWS + r0:PAD_ROWS + r0 + rows, :], taps, r0, rows, nw)
        dw_ref[...] = dw

    blk = SDS((t, 4 * LANE), bf16)
    return pl.pallas_call(
        body, out_shape=(blk, blk, blk, SDS((nw, 4 * LANE), f32)), grid=(4,),
        in_specs=[_col_spec(t, 0), _col_spec(t, 4), _col_spec(t, 8), _w_spec(nw), _col_spec(t, 0)],
        out_specs=(_col_spec(t, 0), _col_spec(t, 0), _col_spec(t, 0), _w_spec(nw)),
        scratch_shapes=[pltpu.VMEM((t + 2 * PAD_ROWS, LANE), f32)] * 2, name=name,
        compiler_params=_cparams(("parallel",)),
    )(p, p, p, w, dy)


def conf_conv(p, w, name):
    t = p.shape[0]
    nw = w.shape[0]
    taps = _taps_1d(31)

    def body(a_ref, b_ref, w_ref, o_ref, pad_ref):
        _fill_pad(pad_ref, a_ref[...] * jax.nn.sigmoid(b_ref[...]), t)
        for r0, rows in _row_chunks(t):
            o_ref[r0:r0 + rows, :] = _taps_apply(pad_ref, w_ref, taps, r0, rows)

    return pl.pallas_call(
        body, out_shape=SDS((t, 4 * LANE), f32), grid=(4,),
        in_specs=[_col_spec(t, 12), _col_spec(t, 16), _w_spec(nw)], out_specs=_col_spec(t, 0),
        scratch_shapes=[pltpu.VMEM((t + 2 * PAD_ROWS, LANE), f32)], name=name,
        compiler_params=_cparams(("parallel",)),
    )(p, p, w)


def conf_conv_bwd(p, w, dz, name):
    t = p.shape[0]
    nw = w.shape[0]
    taps = _taps_1d(31)
    ttaps = _transpose_taps(taps)

    def body(a_ref, b_ref, w_ref, dz_ref, da_ref, db_ref, dw_ref, pad_ref, gpad_ref):
        _fill_pad(pad_ref, a_ref[...] * jax.nn.sigmoid(b_ref[...]), t)
        _fill_pad(gpad_ref, dz_ref[...], t)
        dw = jnp.zeros((nw, LANE), f32)
        for r0, rows in _row_chunks(t):
            sl = slice(r0, r0 + rows)
            dm = _taps_apply(gpad_ref, w_ref, ttaps, r0, rows)
            sg = jax.nn.sigmoid(b_ref[sl, :])
            da_ref[sl, :] = (dm * sg).astype(bf16)
            db_ref[sl, :] = (dm * a_ref[sl, :] * sg * (1.0 - sg)).astype(bf16)
            dw = dw + _taps_wgrad(pad_ref, dz_ref[sl, :], taps, r0, rows, nw)
        dw_ref[...] = dw

    blk = SDS((t, 4 * LANE), bf16)
    return pl.pallas_call(
        body, out_shape=(blk, blk, SDS((nw, 4 * LANE), f32)), grid=(4,),
        in_specs=[_col_spec(t, 12), _col_spec(t, 16), _w_spec(nw), _col_spec(t, 0)],
        out_specs=(_col_spec(t, 0), _col_spec(t, 0), _w_spec(nw)),
        scratch_shapes=[pltpu.VMEM((t + 2 * PAD_ROWS, LANE), f32)] * 2, name=name,
        compiler_params=_cparams(("parallel",)),
    )(p, p, w, dz)


def ffn_conv(h, w, name):
    t = h.shape[0]
    nblk = D_FF // LANE
    nw = w.shape[0]
    taps = _taps_grid3()

    def body(a_ref, g_ref, w_ref, o_ref, pad_ref):
        _fill_pad(pad_ref, a_ref[...], t)
        for r0, rows in _row_chunks(t):
            o_ref[r0:r0 + rows, :] = (_silu(_taps_apply(pad_ref, w_ref, taps, r0, rows)) * g_ref[r0:r0 + rows, :]).astype(bf16)

    return pl.pallas_call(
        body, out_shape=SDS((t, D_FF), bf16), grid=(nblk,),
        in_specs=[_col_spec(t, 0), _col_spec(t, nblk), _w_spec(nw)], out_specs=_col_spec(t, 0),
        scratch_shapes=[pltpu.VMEM((t + 2 * PAD_ROWS, LANE), f32)], name=name,
        compiler_params=_cparams(("parallel",)),
    )(h, h, w)


def ffn_conv_bwd(h, w, df, name):
    t = h.shape[0]
    nblk = D_FF // LANE
    nw = w.shape[0]
    taps = _taps_grid3()
    ttaps = _transpose_taps(taps)

    def body(a_ref, g_ref, w_ref, df_ref, dh_ref, dw_ref, pad_ref, gpad_ref, pre_ref):
        half = pl.program_id(1)

        @pl.when(half == 0)
        def _():
            _fill_pad(pad_ref, a_ref[...], t)
            zeros = jnp.zeros((PAD_ROWS, LANE), f32)
            gpad_ref[0:PAD_ROWS, :] = zeros
            gpad_ref[PAD_ROWS + t:2 * PAD_ROWS + t, :] = zeros
            for r0, rows in _row_chunks(t):
                sl = slice(r0, r0 + rows)
                pre = _taps_apply(pad_ref, w_ref, taps, r0, rows)
                pre_ref[sl, :] = pre
                gpad_ref[PAD_ROWS + r0:PAD_ROWS + r0 + rows, :] = df_ref[sl, :] * g_ref[sl, :] * _dsilu(pre)
            dw = jnp.zeros((nw, LANE), f32)
            for r0, rows in _row_chunks(t):
                dh_ref[r0:r0 + rows, :] = _taps_apply(gpad_ref, w_ref, ttaps, r0, rows).astype(bf16)
                dw = dw + _taps_wgrad(pad_ref, gpad_ref[PAD_ROWS + r0:PAD_ROWS + r0 + rows, :], taps, r0, rows, nw)
            dw_ref[...] = dw

        @pl.when(half == 1)
        def _():
            for r0, rows in _row_chunks(t):
                sl = slice(r0, r0 + rows)
                dh_ref[sl, :] = (df_ref[sl, :] * _silu(pre_ref[sl, :])).astype(bf16)

    cspec = lambda off: pl.BlockSpec((t, LANE), lambda c, s: (0, c + off))
    return pl.pallas_call(
        body, out_shape=(SDS((t, 2 * D_FF), bf16), SDS((nw, D_FF), f32)), grid=(nblk, 2),
        in_specs=[cspec(0), cspec(nblk), pl.BlockSpec((nw, LANE), lambda c, s: (0, c)), cspec(0)],
        out_specs=(pl.BlockSpec((t, LANE), lambda c, s: (0, c + nblk * s)), pl.BlockSpec((nw, LANE), lambda c, s: (0, c))),
        scratch_shapes=[pltpu.VMEM((t + 2 * PAD_ROWS, LANE), f32)] * 2 + [pltpu.VMEM((t, LANE), f32)], name=name,
        compiler_params=_cparams(("parallel", "arbitrary")),
    )(h, h, w, df)


def _pool_count(r0, rows, win, t):
    pos = lax.broadcasted_iota(jnp.int32, (rows, 1), 0) + r0
    lo = jnp.clip(pos - win // 2, 0, t)
    hi = jnp.clip(pos - win // 2 + win, 0, t)
    return (hi - lo).astype(f32)


def _window_sum(pad_ref, r0, rows, lo, hi):
    acc = jnp.zeros((rows, LANE), f32)
    for off in range(lo, hi):
        acc = acc + pad_ref[PAD_ROWS + r0 + off:PAD_ROWS + r0 + off + rows, :]
    return acc


def pool_mix(p, pool_w, pool_scale, name):
    t = p.shape[0]

    def body(x_ref, w_ref, s_ref, o_ref, pad_ref):
        for gi, win in enumerate(POOL_WINDOWS):
            cs = slice(gi * LANE, (gi + 1) * LANE)
            _fill_pad(pad_ref, x_ref[:, cs], t)
            wg = w_ref[gi].astype(bf16)
            for r0, rows in _row_chunks(t):
                pooled = _window_sum(pad_ref, r0, rows, -(win // 2), win - win // 2) / _pool_count(r0, rows, win, t) - x_ref[r0:r0 + rows, cs]
                o_ref[r0:r0 + rows, cs] = (_dotb(pooled, wg) * s_ref[:, cs]).astype(bf16)

    return pl.pallas_call(
        body, out_shape=SDS((t, 512), bf16), grid=(1,),
        in_specs=[pl.BlockSpec((t, 512), lambda i: (0, 4)), pl.BlockSpec((4, LANE, LANE), lambda i: (0, 0, 0)),
                  pl.BlockSpec((1, 512), lambda i: (0, 0))],
        out_specs=pl.BlockSpec((t, 512), lambda i: (0, 0)),
        scratch_shapes=[pltpu.VMEM((t + 2 * PAD_ROWS, LANE), f32)], name=name,
        compiler_params=_cparams(("arbitrary",)),
    )(p, pool_w, pool_scale)


def pool_mix_bwd(p, pool_w, pool_scale, dmix, name):
    t = p.shape[0]

    def body(x_ref, w_ref, s_ref, dy_ref, dp_ref, dw_ref, ds_ref, pad_ref, gpad_ref, dpool_ref):
        for gi, win in enumerate(POOL_WINDOWS):
            cs = slice(gi * LANE, (gi + 1) * LANE)
            h = win // 2
            _fill_pad(pad_ref, x_ref[:, cs], t)
            wg = w_ref[gi].astype(bf16)
            dw = jnp.zeros((LANE, LANE), f32)
            ds = jnp.zeros((1, LANE), f32)
            zeros = jnp.zeros((PAD_ROWS, LANE), f32)
            gpad_ref[0:PAD_ROWS, :] = zeros
            gpad_ref[PAD_ROWS + t:2 * PAD_ROWS + t, :] = zeros
            for r0, rows in _row_chunks(t):
                cnt = _pool_count(r0, rows, win, t)
                pooled = _window_sum(pad_ref, r0, rows, -h, win - h) / cnt - x_ref[r0:r0 + rows, cs]
                dy = dy_ref[r0:r0 + rows, cs]
                ds = ds + jnp.sum(dy * _dotb(pooled, wg), axis=0, keepdims=True)
                dypre = dy * s_ref[:, cs]
                dw = dw + _dotb_tn(pooled, dypre)
                dpooled = _dotb_nt(dypre, wg)
                gpad_ref[PAD_ROWS + r0:PAD_ROWS + r0 + rows, :] = dpooled / cnt
                dpool_ref[r0:r0 + rows, :] = dpooled
            dw_ref[gi] = dw
            ds_ref[:, cs] = ds
            for r0, rows in _row_chunks(t):
                dx = _window_sum(gpad_ref, r0, rows, -h + 1, h + 1) - dpool_ref[r0:r0 + rows, :]
                dp_ref[r0:r0 + rows, cs] = dx.astype(bf16)

    return pl.pallas_call(
        body, out_shape=(SDS((t, 512), bf16), SDS((4, LANE, LANE), f32), SDS((1, 512), f32)), grid=(1,),
        in_specs=[pl.BlockSpec((t, 512), lambda i: (0, 4)), pl.BlockSpec((4, LANE, LANE), lambda i: (0, 0, 0)),
                  pl.BlockSpec((1, 512), lambda i: (0, 0)), pl.BlockSpec((t, 512), lambda i: (0, 1))],
        out_specs=(pl.BlockSpec((t, 512), lambda i: (0, 0)), pl.BlockSpec((4, LANE, LANE), lambda i: (0, 0, 0)),
                   pl.BlockSpec((1, 512), lambda i: (0, 0))),
        scratch_shapes=[pltpu.VMEM((t + 2 * PAD_ROWS, LANE), f32)] * 2 + [pltpu.VMEM((t, LANE), f32)], name=name,
        compiler_params=_cparams(("arbitrary",)),
    )(p, pool_w, pool_scale, dmix)


def gated_rmsnorm(o_a, o_b, p, norm_w, name):
    t = o_a.shape[0]
    tt = _row_tile(t)

    def body(oa_ref, ob_ref, g_ref, nw_ref, y_ref):
        for h in range(GDN_HEADS):
            cs = slice(h * LANE, (h + 1) * LANE)
            o = oa_ref[:, cs] + ob_ref[:, cs]
            r = lax.rsqrt(jnp.mean(o * o, axis=-1, keepdims=True) + RMS_EPS)
            y_ref[:, cs] = (o * r * nw_ref[...] * _silu(g_ref[:, cs])).astype(bf16)

    return pl.pallas_call(
        body, out_shape=SDS((t, 512), bf16), grid=(t // tt,),
        in_specs=[_row_spec(tt, 512), _row_spec(tt, 512), pl.BlockSpec((tt, 512), lambda i: (i, 3)), _vec_spec(LANE)],
        out_specs=_row_spec(tt, 512), name=name, compiler_params=_cparams(("parallel",)),
    )(o_a, o_b, p, norm_w)


def gated_rmsnorm_bwd(o_a, o_b, p, norm_w, dmix, name):
    t = o_a.shape[0]
    tt = _row_tile(t)

    def body(oa_ref, ob_ref, g_ref, nw_ref, dy_ref, do_ref, dg_ref, dnw_ref):
        dnw = jnp.zeros((1, LANE), f32)
        for h in range(GDN_HEADS):
            cs = slice(h * LANE, (h + 1) * LANE)
            o = oa_ref[:, cs] + ob_ref[:, cs]
            r = lax.rsqrt(jnp.mean(o * o, axis=-1, keepdims=True) + RMS_EPS)
            gate = g_ref[:, cs]
            dy = dy_ref[:, cs]
            dy1 = dy * _silu(gate)
            dg_ref[:, cs] = (dy * (o * r * nw_ref[...]) * _dsilu(gate)).astype(bf16)
            dnw = dnw + jnp.sum(dy1 * o * r, axis=0, keepdims=True)
            dn = dy1 * nw_ref[...]
            do_ref[:, cs] = r * dn - o * (r * r * r) * jnp.mean(dn * o, axis=-1, keepdims=True)
        _acc_rows(dnw_ref, dnw)

    return pl.pallas_call(
        body, out_shape=(SDS((t, 512), f32), SDS((t, 512), bf16), SDS((1, LANE), f32)), grid=(t // tt,),
        in_specs=[_row_spec(tt, 512), _row_spec(tt, 512), pl.BlockSpec((tt, 512), lambda i: (i, 3)), _vec_spec(LANE),
                  _row_spec(tt, 512)],
        out_specs=(_row_spec(tt, 512), _row_spec(tt, 512), _vec_spec(LANE)),
        name=name, compiler_params=_cparams(("arbitrary",)),
    )(o_a, o_b, p, norm_w, dmix)


def ln_silu(z, g, b, name):
    t, d = z.shape
    tt = _row_tile(t)

    def body(z_ref, g_ref, b_ref, o_ref):
        xhat, _ = _ln_stats(z_ref[...])
        o_ref[...] = _silu(xhat * g_ref[...] + b_ref[...]).astype(bf16)

    return pl.pallas_call(
        body, out_shape=SDS((t, d), bf16), grid=(t // tt,),
        in_specs=[_row_spec(tt, d), _vec_spec(d), _vec_spec(d)], out_specs=_row_spec(tt, d),
        name=name, compiler_params=_cparams(("parallel",)),
    )(z, g, b)


def ln_silu_bwd(z, g, b, dmix, name):
    t, d = z.shape
    tt = _row_tile(t)

    def body(z_ref, g_ref, b_ref, dy_ref, dz_ref, dg_ref, db_ref):
        xhat, rstd = _ln_stats(z_ref[...])
        dn = dy_ref[...] * _dsilu(xhat * g_ref[...] + b_ref[...])
        dz_ref[...] = _ln_bwd(dn * g_ref[...], xhat, rstd)
        _acc_rows(dg_ref, jnp.sum(dn * xhat, axis=0, keepdims=True))
        _acc_rows(db_ref, jnp.sum(dn, axis=0, keepdims=True))

    return pl.pallas_call(
        body, out_shape=(SDS((t, d), f32), SDS((1, d), f32), SDS((1, d), f32)), grid=(t // tt,),
        in_specs=[_row_spec(tt, d), _vec_spec(d), _vec_spec(d), pl.BlockSpec((tt, d), lambda i: (i, 1))],
        out_specs=(_row_spec(tt, d), _vec_spec(d), _vec_spec(d)),
        name=name, compiler_params=_cparams(("arbitrary",)),
    )(z, g, b, dmix)


def gdn_gates(p, neg_a, dt_bias, name):
    t = p.shape[0]
    tt = _row_tile(t)

    def body(s_ref, na_ref, dt_ref, o_ref):
        s = s_ref[...]
        col = lax.broadcasted_iota(jnp.int32, s.shape, 1)
        o_ref[...] = jnp.where(col < 8, jax.nn.sigmoid(s), na_ref[...] * jax.nn.softplus(s + dt_ref[...]))

    return pl.pallas_call(
        body, out_shape=SDS((t, LANE), f32), grid=(t // tt,),
        in_specs=[pl.BlockSpec((tt, LANE), lambda i: (i, 20)), _vec_spec(LANE), _vec_spec(LANE)],
        out_specs=_row_spec(tt, LANE), name=name, compiler_params=_cparams(("parallel",)),
    )(p, neg_a, dt_bias)


def gdn_gates_bwd(p, neg_a, dt_bias, dbg_a, dbg_b, name):
    t = p.shape[0]
    tt = _row_tile(t)

    def body(s_ref, na_ref, dt_ref, d_ref, d2_ref, ds_ref, da_ref, ddt_ref):
        s = s_ref[...]
        d = d_ref[...] + d2_ref[...]
        col = lax.broadcasted_iota(jnp.int32, s.shape, 1)
        sg = jax.nn.sigmoid(s)
        z = s + dt_ref[...]
        dz = jnp.where((col >= 8) & (col < 16), d * na_ref[...] * jax.nn.sigmoid(z), 0.0)
        ds_ref[...] = jnp.where(col < 8, d * sg * (1.0 - sg), dz).astype(bf16)
        dalog = jnp.where((col >= 8) & (col < 16), d * na_ref[...] * jax.nn.softplus(z), 0.0)
        _acc_rows(da_ref, jnp.sum(dalog, axis=0, keepdims=True))
        _acc_rows(ddt_ref, jnp.sum(dz, axis=0, keepdims=True))

    return pl.pallas_call(
        body, out_shape=(SDS((t, LANE), bf16), SDS((1, LANE), f32), SDS((1, LANE), f32)), grid=(t // tt,),
        in_specs=[pl.BlockSpec((tt, LANE), lambda i: (i, 20)), _vec_spec(LANE), _vec_spec(LANE), _row_spec(tt, LANE),
                  _row_spec(tt, LANE)],
        out_specs=(_row_spec(tt, LANE), _vec_spec(LANE), _vec_spec(LANE)),
        name=name, compiler_params=_cparams(("arbitrary",)),
    )(p, neg_a, dt_bias, dbg_a, dbg_b)


N_SCAN = 2 * GDN_HEADS


def _bdot(a, b, ca, cb, precision=None):
    if precision is None:
        a, b = a.astype(bf16), b.astype(bf16)
    return lax.dot_general(a, b, (((ca,), (cb,)), ((0,), (0,))), preferred_element_type=f32, precision=precision)


def _bdot_nn(a, b, precision=None):
    return _bdot(a, b, 2, 1, precision)


def _bdot_nt(a, b):
    return _bdot(a, b, 2, 2)


def _bdot_tn(a, b, precision=None):
    return _bdot(a, b, 1, 1, precision)


def _order_masks():
    shape = (N_SCAN, CHUNK, CHUNK)
    sign = jnp.where(lax.broadcasted_iota(jnp.int32, shape, 0) >= GDN_HEADS, -1, 1)
    ahead = (lax.broadcasted_iota(jnp.int32, shape, 1) - lax.broadcasted_iota(jnp.int32, shape, 2)) * sign
    lower, strict, lower_t = ahead >= 0, ahead > 0, ahead <= 0
    col_shape = (N_SCAN, CHUNK, 1)
    back1 = lax.broadcasted_iota(jnp.int32, col_shape, 0) >= GDN_HEADS
    row1 = lax.broadcasted_iota(jnp.int32, col_shape, 1)
    at_last = (row1 == jnp.where(back1, 0, CHUNK - 1)).astype(f32)
    return lower, strict, lower_t, at_last


def _stack_heads(f_ref, b_ref):
    return jnp.stack([ref[:, h * LANE:(h + 1) * LANE] for ref in (f_ref, b_ref) for h in range(GDN_HEADS)])


def _stack_gates(bgf, bgb, bgtf, bgtb):
    beta = jnp.stack([bg[:, 4 * d + h:4 * d + h + 1] for d, bg in enumerate((bgf, bgb)) for h in range(GDN_HEADS)])
    g_col = jnp.stack([bg[:, 8 + 4 * d + h:9 + 4 * d + h] for d, bg in enumerate((bgf, bgb)) for h in range(GDN_HEADS)])
    g_row = jnp.stack([bgt[8 + 4 * d + h:9 + 4 * d + h, :] for d, bgt in enumerate((bgtf, bgtb)) for h in range(GDN_HEADS)])
    return beta, g_col, g_row


def _chunk_terms(k, v, beta, g_col, g_row, masks):
    lower, strict, lower_t, at_last = masks
    gc = jnp.sum(lower.astype(f32) * g_row, axis=2, keepdims=True)
    gr = jnp.sum(lower_t.astype(f32) * g_col, axis=1, keepdims=True)
    g_last = jnp.sum(at_last * gc, axis=1, keepdims=True)
    e = jnp.exp(gc)
    f = jnp.exp(g_last - gc)
    dm = jnp.exp(jnp.where(lower, gc - gr, -1e30))
    kb = k * beta
    kk = _bdot_nt(kb, k)
    a = jnp.where(strict, kk * dm, 0.0)
    shape = (N_SCAN, CHUNK, CHUNK)
    eye = (lax.broadcasted_iota(jnp.int32, shape, 1) == lax.broadcasted_iota(jnp.int32, shape, 2)).astype(f32)
    pw = -a
    tinv = eye + pw
    for _ in range(5):
        pw = _bdot_nn(pw, pw, HI)
        tinv = tinv + _bdot_nn(tinv, pw, HI)
    u = _bdot_nn(tinv, v * beta)
    w = _bdot_nn(tinv, kb * e)
    return dict(e=e, f=f, gl=jnp.exp(g_last), dm=dm, kb=kb, kk=kk, tinv=tinv, u=u, w=w, kd=k * f)


def _gdn_specs(nc, width, step_chunk):
    return [pl.BlockSpec((CHUNK, width), functools.partial(lambda i, d: (step_chunk(i, d), 0), d=d)) for d in (0, 1)]


def gdn_forward(q, k, v, bg, bgt, s0, with_out, name):
    t = k.shape[0]
    nc = t // CHUNK

    def body(qf_ref, qb_ref, kf_ref, kb_ref, vf_ref, vb_ref, bgf_ref, bgb_ref, bgtf_ref, bgtb_ref, s0_ref,
             of_ref, ob_ref, sallf_ref, sallb_ref, sfin_ref, s_ref):
        i = pl.program_id(0)

        @pl.when(i == 0)
        def _():
            s_ref[...] = s0_ref[...]

        masks = _order_masks()
        k8, v8 = _stack_heads(kf_ref, kb_ref), _stack_heads(vf_ref, vb_ref)
        beta, g_col, g_row = _stack_gates(bgf_ref[...], bgb_ref[...], bgtf_ref[0], bgtb_ref[0])
        c = _chunk_terms(k8, v8, beta, g_col, g_row, masks)
        s = s_ref[...]
        sallf_ref[0] = s[:GDN_HEADS]
        sallb_ref[0] = s[GDN_HEADS:]
        vn = c["u"] - _bdot_nn(c["w"], s)
        if with_out:
            q8 = _stack_heads(qf_ref, qb_ref)
            pm = jnp.where(masks[0], _bdot_nt(q8, k8) * c["dm"], 0.0)
            o = _bdot_nn(q8 * c["e"], s) + _bdot_nn(pm, vn)
        for d, o_ref in enumerate((of_ref, ob_ref)):
            for h in range(GDN_HEADS):
                o_ref[:, h * LANE:(h + 1) * LANE] = o[GDN_HEADS * d + h] if with_out else jnp.zeros((CHUNK, LANE), f32)
        s_ref[...] = c["gl"] * s + _bdot_tn(c["kd"], vn)

        @pl.when(i == nc - 1)
        def _():
            sfin_ref[...] = s_ref[...]

    chunk_of = lambda i, d: i if d == 0 else nc - 1 - i
    seq = _gdn_specs(nc, 512, chunk_of)
    gate = _gdn_specs(nc, LANE, chunk_of)
    gate_t = [pl.BlockSpec((1, 16, CHUNK), functools.partial(lambda i, d: (chunk_of(i, d), 0, 0), d=d)) for d in (0, 1)]
    sall = [pl.BlockSpec((1, GDN_HEADS, LANE, LANE), functools.partial(lambda i, d: (chunk_of(i, d), 0, 0, 0), d=d)) for d in (0, 1)]
    st = pl.BlockSpec((N_SCAN, LANE, LANE), lambda i: (0, 0, 0))
    o_shape, s_shape = SDS((t, 512), f32), SDS((nc, GDN_HEADS, LANE, LANE), f32)
    o_f, o_b, sall_f, sall_b, s_fin = pl.pallas_call(
        body, out_shape=(o_shape, o_shape, s_shape, s_shape, SDS((N_SCAN, LANE, LANE), f32)), grid=(nc,),
        in_specs=seq + seq + seq + gate + gate_t + [st], out_specs=tuple(seq + sall + [st]),
        scratch_shapes=[pltpu.VMEM((N_SCAN, LANE, LANE), f32)], name=name,
        compiler_params=_cparams(("arbitrary",)),
    )(q, q, k, k, v, v, bg, bg, bgt, bgt, s0.reshape(N_SCAN, LANE, LANE))
    return o_f, o_b, sall_f, sall_b, s_fin.reshape(2, GDN_HEADS, LANE, LANE)


def _gdn_chunk_bwd(q, k, v, d_o, beta, g_col, g_row, s, dsn, masks):
    lower, strict, _, at_last = masks
    c = _chunk_terms(k, v, beta, g_col, g_row, masks)
    e, f, gl, dm, kb, kk, tinv, u, w, kd = (c[n] for n in ("e", "f", "gl", "dm", "kb", "kk", "tinv", "u", "w", "kd"))
    vn = u - _bdot_nn(w, s)
    ds = gl * dsn
    dgl = jnp.sum(jnp.sum(s * dsn, axis=2, keepdims=True), axis=1, keepdims=True)
    dkd = _bdot_nt(vn, dsn)
    dvn = _bdot_nn(kd, dsn)
    dm_grad = jnp.zeros((N_SCAN, CHUNK, CHUNK), f32)
    de = jnp.zeros((N_SCAN, CHUNK, 1), f32)
    dq = None
    dk = jnp.zeros((N_SCAN, CHUNK, LANE), f32)
    if q is not None:
        qk = _bdot_nt(q, k)
        pm = jnp.where(lower, qk * dm, 0.0)
        dqd = _bdot_nt(d_o, s)
        ds = ds + _bdot_tn(q * e, d_o)
        dpm = jnp.where(lower, _bdot_nt(d_o, vn), 0.0)
        dvn = dvn + _bdot_tn(pm, d_o)
        dqk = dpm * dm
        dm_grad = dm_grad + dpm * qk
        dq = _bdot_nn(dqk, k) + dqd * e
        dk = _bdot_tn(dqk, q)
        de = de + jnp.sum(dqd * q, axis=2, keepdims=True)
    dw = -_bdot_nt(dvn, s)
    ds = ds - _bdot_tn(w, dvn)
    drv = _bdot_tn(tinv, dvn)
    drk = _bdot_tn(tinv, dw)
    da = -jnp.where(strict, _bdot_nt(drv, u) + _bdot_nt(drk, w), 0.0)
    dbeta = jnp.sum(drv * v, axis=2, keepdims=True)
    dv = drv * beta
    dkb = drk * e
    de = de + jnp.sum(drk * kb, axis=2, keepdims=True)
    dkk = da * dm
    dm_grad = dm_grad + da * kk
    dkb = dkb + _bdot_nn(dkk, k)
    dk = dk + _bdot_tn(dkk, kb) + dkd * f
    df = jnp.sum(dkd * k, axis=2, keepdims=True)
    dbeta = dbeta + jnp.sum(dkb * k, axis=2, keepdims=True)
    dk = dk + dkb * beta
    m = dm_grad * dm
    ones = jnp.ones((N_SCAN, CHUNK, LANE), f32)
    rsum = jnp.sum(m, axis=2, keepdims=True)
    csum = _bdot_tn(m, ones, HI)[:, :, 0:1]
    dgl_tot = jnp.sum(df * f, axis=1, keepdims=True) + dgl * gl
    dgc = de * e - df * f + rsum - csum + at_last * dgl_tot
    dg = _bdot_tn(lower.astype(f32), dgc * ones, HI)[:, :, 0:1]
    return dq, dk, dv, dbeta, dg, ds


def gdn_backward(q, k, v, bg, bgt, sall_f, sall_b, d_o, ds_fin, with_out, name):
    t = k.shape[0]
    nc = t // CHUNK

    def body(qf_ref, qb_ref, kf_ref, kb_ref, vf_ref, vb_ref, bgf_ref, bgb_ref, bgtf_ref, bgtb_ref,
             sallf_ref, sallb_ref, dof_ref, dob_ref, dsf_ref,
             dqf_ref, dqb_ref, dkf_ref, dkb_ref, dvf_ref, dvb_ref, dbgf_ref, dbgb_ref, ds0_ref, ds_ref):
        i = pl.program_id(0)

        @pl.when(i == 0)
        def _():
            ds_ref[...] = dsf_ref[...]

        lane = lax.broadcasted_iota(jnp.int32, (1, LANE), 1)
        masks = _order_masks()
        beta, g_col, g_row = _stack_gates(bgf_ref[...], bgb_ref[...], bgtf_ref[0], bgtb_ref[0])
        s = jnp.concatenate([sallf_ref[0], sallb_ref[0]], 0)
        dq, dk, dv, dbeta, dg, ds = _gdn_chunk_bwd(
            _stack_heads(qf_ref, qb_ref) if with_out else None, _stack_heads(kf_ref, kb_ref), _stack_heads(vf_ref, vb_ref),
            _stack_heads(dof_ref, dob_ref), beta, g_col, g_row, s, ds_ref[...], masks)
        ds_ref[...] = ds
        for d, (dq_ref, dk_ref, dv_ref, dbg_ref) in enumerate(((dqf_ref, dkf_ref, dvf_ref, dbgf_ref), (dqb_ref, dkb_ref, dvb_ref, dbgb_ref))):
            dbg = jnp.zeros((CHUNK, LANE), f32)
            for h in range(GDN_HEADS):
                b = GDN_HEADS * d + h
                cs = slice(h * LANE, (h + 1) * LANE)
                dq_ref[:, cs] = dq[b] if with_out else jnp.zeros((CHUNK, LANE), f32)
                dk_ref[:, cs] = dk[b]
                dv_ref[:, cs] = dv[b]
                dbg = dbg + dbeta[b] * (lane == b).astype(f32) + dg[b] * (lane == 8 + b).astype(f32)
            dbg_ref[...] = dbg

        @pl.when(i == nc - 1)
        def _():
            ds0_ref[...] = ds_ref[...]

    chunk_of = lambda i, d: nc - 1 - i if d == 0 else i
    seq = _gdn_specs(nc, 512, chunk_of)
    gate = _gdn_specs(nc, LANE, chunk_of)
    gate_t = [pl.BlockSpec((1, 16, CHUNK), functools.partial(lambda i, d: (chunk_of(i, d), 0, 0), d=d)) for d in (0, 1)]
    sall = [pl.BlockSpec((1, GDN_HEADS, LANE, LANE), functools.partial(lambda i, d: (chunk_of(i, d), 0, 0, 0), d=d)) for d in (0, 1)]
    st = pl.BlockSpec((N_SCAN, LANE, LANE), lambda i: (0, 0, 0))
    o_shape, g_shape = SDS((t, 512), f32), SDS((t, LANE), f32)
    res = pl.pallas_call(
        body, out_shape=(o_shape,) * 6 + (g_shape, g_shape, SDS((N_SCAN, LANE, LANE), f32)), grid=(nc,),
        in_specs=seq + seq + seq + gate + gate_t + sall + seq + [st], out_specs=tuple(seq + seq + seq + gate + [st]),
        scratch_shapes=[pltpu.VMEM((N_SCAN, LANE, LANE), f32)], name=name,
        compiler_params=_cparams(("arbitrary",)),
    )(q, q, k, k, v, v, bg, bg, bgt, bgt, sall_f, sall_b, d_o, d_o, ds_fin.reshape(N_SCAN, LANE, LANE))
    return tuple(res[:8]) + (res[8].reshape(2, GDN_HEADS, LANE, LANE),)


def _my_position():
    x, y, c = lax.axis_index("x"), lax.axis_index("y"), lax.axis_index("c")
    return x, y, c, 4 * x + 2 * y + c


def exchange(arrays, scatter, name):
    n = len(arrays)
    shapes = [a.shape[1:] if scatter else a.shape for a in arrays]

    def body(*refs):
        ins, outs = refs[:n], refs[n:2 * n]
        send_sems, recv_sems, local_sems = refs[2 * n:]
        x, y, c, me = _my_position()
        started = []
        for a in range(n):
            mine = pltpu.make_async_copy(ins[a].at[me] if scatter else ins[a], outs[a].at[me], local_sems.at[a])
            mine.start()
            started.append(mine)
        waits = []
        for r in range(1, N_DEV):
            px = 1 - x if r & 4 else x
            py = 1 - y if r & 2 else y
            pc = 1 - c if r & 1 else c
            pid = 4 * px + 2 * py + pc
            for a in range(n):
                cp = pltpu.make_async_remote_copy(
                    src_ref=ins[a].at[pid] if scatter else ins[a], dst_ref=outs[a].at[me],
                    send_sem=send_sems.at[a, r - 1], recv_sem=recv_sems.at[a, r - 1],
                    device_id=(px, py, pc), device_id_type=pl.DeviceIdType.MESH)
                cp.start()
                arrive = pltpu.make_async_remote_copy(
                    src_ref=ins[a].at[pid] if scatter else ins[a], dst_ref=outs[a].at[pid],
                    send_sem=send_sems.at[a, r - 1], recv_sem=recv_sems.at[a, r - 1],
                    device_id=(px, py, pc), device_id_type=pl.DeviceIdType.MESH)
                waits.append((cp, arrive))
        for cp, arrive in waits:
            cp.wait_send()
            arrive.wait_recv()
        for mine in started:
            mine.wait()

    any_spec = pl.BlockSpec(memory_space=pl.ANY)
    return pl.pallas_call(
        body, out_shape=tuple(SDS((N_DEV,) + tuple(s), a.dtype) for s, a in zip(shapes, arrays)),
        in_specs=[any_spec] * n, out_specs=tuple([any_spec] * n),
        scratch_shapes=[pltpu.SemaphoreType.DMA((n, N_DEV - 1)), pltpu.SemaphoreType.DMA((n, N_DEV - 1)),
                        pltpu.SemaphoreType.DMA((n,))],
        name=name,
    )(*arrays)


_HBM_SPEC = pl.BlockSpec(memory_space=pltpu.HBM)
_SEM_SPEC = pl.BlockSpec(memory_space=pltpu.SEMAPHORE)
_DATAFLOW = pltpu.SideEffectType.DATAFLOW_SIDE_EFFECTING


def _peers(x, y, c):
    out = []
    for r in range(1, N_DEV):
        px = 1 - x if r & 4 else x
        py = 1 - y if r & 2 else y
        pc = 1 - c if r & 1 else c
        out.append((r, (px, py, pc), 4 * px + 2 * py + pc))
    return out


def _exchange_copies(ins, lands, send_sems, recv_sems, scatter, arrivals):
    x, y, c, me = _my_position()
    pairs = []
    for r, peer, pid in _peers(x, y, c):
        for a in range(len(ins)):
            k = a * (N_DEV - 1) + r - 1
            kw = dict(send_sem=send_sems.at[k], recv_sem=recv_sems.at[k], device_id=peer, device_id_type=pl.DeviceIdType.MESH)
            src = ins[a].at[pid] if scatter else ins[a]
            send = pltpu.make_async_remote_copy(src_ref=src, dst_ref=lands[a].at[me], **kw)
            arrive = pltpu.make_async_remote_copy(src_ref=src, dst_ref=lands[a].at[pid], **kw) if arrivals else None
            pairs.append((send, arrive))
    return pairs


def exchange_start(arrays, scatter, name):
    n = len(arrays)
    shapes = [a.shape[1:] if scatter else a.shape for a in arrays]

    def body(*refs):
        ins, lands = refs[:n], refs[n:2 * n]
        send_sems, recv_sems = refs[2 * n], refs[2 * n + 1]
        token = refs[-1]
        for send, _ in _exchange_copies(ins, lands, send_sems, recv_sems, scatter, False):
            send.start()
        token[...] = jnp.zeros_like(token)

    sem = pltpu.SemaphoreType.DMA((n * (N_DEV - 1),))
    land_shapes = [(N_DEV,) + tuple(s) for s in shapes]
    res = pl.pallas_call(
        body, name=name,
        out_shape=(sem, sem, *[pltpu.HBM(a.shape, a.dtype) for a in arrays],
                   *[pltpu.HBM(s, a.dtype) for s, a in zip(land_shapes, arrays)], SDS((8, LANE), f32)),
        in_specs=[_HBM_SPEC] * (2 * n),
        out_specs=(_SEM_SPEC, _SEM_SPEC, *[_HBM_SPEC] * (2 * n), pl.BlockSpec(memory_space=pltpu.VMEM)),
        input_output_aliases={i: 2 + i for i in range(2 * n)},
        compiler_params=pltpu.CompilerParams(has_side_effects=_DATAFLOW),
    )(*[pltpu.with_memory_space_constraint(a, pltpu.HBM) for a in arrays],
      *[pltpu.with_memory_space_constraint(lax.empty(s, a.dtype), pltpu.HBM) for s, a in zip(land_shapes, arrays)])
    return (res[0], res[1], list(res[2:2 + n]), list(res[2 + n:2 + 2 * n]), scatter), res[-1]


def exchange_wait(handle, after, name):
    send_sems, recv_sems, ins, lands, scatter = handle
    n = len(ins)

    def body(*refs):
        in_refs, land_refs = refs[:n], refs[n:2 * n]
        for send, arrive in _exchange_copies(in_refs, land_refs, refs[2 * n], refs[2 * n + 1], scatter, True):
            send.wait_send()
            arrive.wait_recv()

    res = pl.pallas_call(
        body, name=name,
        out_shape=tuple(pltpu.HBM(a.shape, a.dtype) for a in ins + lands),
        in_specs=[_HBM_SPEC] * (2 * n) + [_SEM_SPEC, _SEM_SPEC, pl.BlockSpec(memory_space=pl.ANY)],
        out_specs=tuple([_HBM_SPEC] * (2 * n)),
        input_output_aliases={i: i for i in range(2 * n)},
        compiler_params=pltpu.CompilerParams(has_side_effects=_DATAFLOW),
    )(*ins, *lands, send_sems, recv_sems, after)
    return list(res[:n]), list(res[n:])


def place_own(lands, arrays, scatter, name):
    n = len(lands)

    def body(*refs):
        land_in, srcs, outs, sems = refs[:n], refs[n:2 * n], refs[2 * n:3 * n], refs[3 * n]
        _, _, _, me = _my_position()
        copies = [pltpu.make_async_copy(srcs[a].at[me] if scatter else srcs[a], outs[a].at[me], sems.at[a]) for a in range(n)]
        for cp in copies:
            cp.start()
        for cp in copies:
            cp.wait()

    any_spec = pl.BlockSpec(memory_space=pl.ANY)
    res = pl.pallas_call(
        body, name=name, out_shape=tuple(SDS(l.shape, l.dtype) for l in lands),
        in_specs=[any_spec] * (2 * n), out_specs=tuple([any_spec] * n),
        input_output_aliases={i: i for i in range(n)},
        scratch_shapes=[pltpu.SemaphoreType.DMA((n,))],
    )(*lands, *arrays)
    return list(res)


def ada_forward(a_raw, ada_w, ada_b_loc, name):
    def body(a_ref, w_ref, b_ref, o_ref):
        a = _silu(a_ref[...])
        for l in range(DEPTH):
            o_ref[l] = _dotf(a, w_ref[l]) + b_ref[l]

    return pl.pallas_call(body, out_shape=SDS((DEPTH, 16, ada_w.shape[2]), f32), name=name,
                          compiler_params=_cparams())(a_raw, ada_w, ada_b_loc)


def ada_backward(a_raw, ada_w, dm, name):
    def body(a_ref, w_ref, dm_ref, gw_ref, dcc_ref):
        a = _silu(a_ref[...])
        for l in range(DEPTH):
            gw_ref[l] = _dotf(a, dm_ref[l], (((0,), (0,)), ((), ())))
        dcc_ref[...] = _dotf(dm_ref[0, 8:16, :], w_ref[0], (((1,), (1,)), ((), ())))

    return pl.pallas_call(body, out_shape=(SDS(ada_w.shape, f32), SDS((8, ada_w.shape[1]), f32)), name=name,
                          compiler_params=_cparams())(a_raw, ada_w, dm)


def sum_parts(parts, name):
    _, r, c = parts.shape

    def body(p_ref, o_ref):
        acc = p_ref[0]
        for i in range(1, N_DEV):
            acc = acc + p_ref[i]
        o_ref[...] = acc

    return pl.pallas_call(body, out_shape=SDS((r, c), f32), name=name, compiler_params=_cparams())(parts)


def cctx_grad(parts, c_ctx, name):
    def body(p_ref, c_ref, o_ref):
        acc = p_ref[0, 0:1, :]
        for i in range(1, N_DEV):
            acc = acc + p_ref[i, 0:1, :]
        o_ref[...] = acc * _dsilu(c_ref[...])

    return pl.pallas_call(body, out_shape=SDS((1, c_ctx.shape[1]), f32), name=name, compiler_params=_cparams())(parts, c_ctx)


def _adamw_math(g, w, m, v):
    m = ADAM_B1 * m + (1.0 - ADAM_B1) * g
    v = ADAM_B2 * v + (1.0 - ADAM_B2) * (g * g)
    m_hat = m / (1.0 - ADAM_B1 ** ADAM_STEP)
    v_hat = v / (1.0 - ADAM_B2 ** ADAM_STEP)
    delta = -ADAM_LR * (m_hat / (jnp.sqrt(v_hat) + ADAM_EPS) + ADAM_WD * w)
    return delta, m, v


def adamw(parts, w, m, v, name):
    n, r, c = parts.shape
    tr = _pick(r, (256, 128, 64, 32, 16, 8))

    def body(p_ref, w_ref, m_ref, v_ref, g_ref, d_ref, nm_ref, nv_ref):
        g = p_ref[0].astype(f32)
        for i in range(1, n):
            g = g + p_ref[i].astype(f32)
        g_ref[...] = g
        d_ref[...], nm_ref[...], nv_ref[...] = _adamw_math(g, w_ref[...], m_ref[...], v_ref[...])

    blk = pl.BlockSpec((tr, c), lambda i: (i, 0))
    out = SDS((r, c), f32)
    return pl.pallas_call(
        body, out_shape=(out, out, out, out), grid=(r // tr,),
        in_specs=[pl.BlockSpec((n, tr, c), lambda i: (0, i, 0)), blk, blk, blk], out_specs=(blk, blk, blk, blk),
        name=name, compiler_params=_cparams(("parallel",)),
    )(parts, w, m, v)


def adamw_small(items, name):
    n = len(items)

    def body(*refs):
        ins, outs = refs[:4 * n], refs[4 * n:]
        for i in range(n):
            g, w, m, v = (ins[4 * i + j][...] for j in range(4))
            outs[3 * i][...], outs[3 * i + 1][...], outs[3 * i + 2][...] = _adamw_math(g, w, m, v)

    flat = [a for it in items for a in it]
    out_shape = tuple(SDS(it[1].shape, f32) for it in items for _ in range(3))
    res = pl.pallas_call(body, out_shape=out_shape, name=name, compiler_params=_cparams())(*flat)
    return [tuple(res[3 * i:3 * i + 3]) for i in range(n)]


def _unshard(g, axis):
    loc = g.shape[1:]
    return jnp.moveaxis(g, 0, axis).reshape(loc[:axis] + (N_DEV * loc[axis],) + loc[axis + 1:])


def _shard_major(full, axis):
    s = full.shape
    return jnp.moveaxis(full.reshape(s[:axis] + (N_DEV, s[axis] // N_DEV) + s[axis + 1:]), axis, 0)


def _my_block(full, axis, me):
    n = full.shape[axis] // N_DEV
    return lax.dynamic_slice_in_dim(full, me * n, n, axis)


def _pack(arrays):
    flat = [a.reshape(-1) for a in arrays]
    sizes = [f.shape[0] for f in flat]
    total = sum(sizes)
    padded = -(-total // (8 * LANE)) * (8 * LANE)
    flat.append(jnp.zeros((padded - total,), f32))
    offs = [sum(sizes[:i]) for i in range(len(sizes))]
    return jnp.concatenate(flat).reshape(padded // LANE, LANE), offs


def _pad_rows(w, n):
    return jnp.concatenate([w, jnp.zeros((n - w.shape[0],) + w.shape[1:], w.dtype)], 0)


def _gate_rows(bg):
    return bg[:, :16].reshape(bg.shape[0] // CHUNK, CHUNK, 16).transpose(0, 2, 1)


def _rows(vec, n):
    m = vec.reshape(n, 1, -1)
    return [m[i] for i in range(n)]


def kernel(x, c, ctx, c_ctx, ada_w, ada_b, ln_g, ln_b, even_w_in, even_w_out, gdn_conv_w, gdn_a_log, gdn_dt_bias, gdn_norm_w, pool_w, pool_scale, odd_w_in, odd_w_out, sconv_w, conf_conv_w, conf_ln_g, conf_ln_b, ffn_w_up, ffn_conv_w, ffn_w_down, loss_target, m_c_ctx, m_ada_w, m_ada_b, m_ln_g, m_ln_b, m_even_w_in, m_even_w_out, m_gdn_conv_w, m_gdn_a_log, m_gdn_dt_bias, m_gdn_norm_w, m_pool_w, m_pool_scale, m_odd_w_in, m_odd_w_out, m_sconv_w, m_conf_conv_w, m_conf_ln_g, m_conf_ln_b, m_ffn_w_up, m_ffn_conv_w, m_ffn_w_down, v_c_ctx, v_ada_w, v_ada_b, v_ln_g, v_ln_b, v_even_w_in, v_even_w_out, v_gdn_conv_w, v_gdn_a_log, v_gdn_dt_bias, v_gdn_norm_w, v_pool_w, v_pool_scale, v_odd_w_in, v_odd_w_out, v_sconv_w, v_conf_conv_w, v_conf_ln_g, v_conf_ln_b, v_ffn_w_up, v_ffn_conv_w, v_ffn_w_down):
    weights = dict(c_ctx=c_ctx, ada_w=ada_w, ada_b=ada_b, ln_g=ln_g, ln_b=ln_b, even_w_in=even_w_in, even_w_out=even_w_out, gdn_conv_w=gdn_conv_w, gdn_a_log=gdn_a_log, gdn_dt_bias=gdn_dt_bias, gdn_norm_w=gdn_norm_w, pool_w=pool_w, pool_scale=pool_scale, odd_w_in=odd_w_in, odd_w_out=odd_w_out, sconv_w=sconv_w, conf_conv_w=conf_conv_w, conf_ln_g=conf_ln_g, conf_ln_b=conf_ln_b, ffn_w_up=ffn_w_up, ffn_conv_w=ffn_conv_w, ffn_w_down=ffn_w_down)
    mom1 = dict(c_ctx=m_c_ctx, ada_w=m_ada_w, ada_b=m_ada_b, ln_g=m_ln_g, ln_b=m_ln_b, even_w_in=m_even_w_in, even_w_out=m_even_w_out, gdn_conv_w=m_gdn_conv_w, gdn_a_log=m_gdn_a_log, gdn_dt_bias=m_gdn_dt_bias, gdn_norm_w=m_gdn_norm_w, pool_w=m_pool_w, pool_scale=m_pool_scale, odd_w_in=m_odd_w_in, odd_w_out=m_odd_w_out, sconv_w=m_sconv_w, conf_conv_w=m_conf_conv_w, conf_ln_g=m_conf_ln_g, conf_ln_b=m_conf_ln_b, ffn_w_up=m_ffn_w_up, ffn_conv_w=m_ffn_conv_w, ffn_w_down=m_ffn_w_down)
    mom2 = dict(c_ctx=v_c_ctx, ada_w=v_ada_w, ada_b=v_ada_b, ln_g=v_ln_g, ln_b=v_ln_b, even_w_in=v_even_w_in, even_w_out=v_even_w_out, gdn_conv_w=v_gdn_conv_w, gdn_a_log=v_gdn_a_log, gdn_dt_bias=v_gdn_dt_bias, gdn_norm_w=v_gdn_norm_w, pool_w=v_pool_w, pool_scale=v_pool_scale, odd_w_in=v_odd_w_in, odd_w_out=v_odd_w_out, sconv_w=v_sconv_w, conf_conv_w=v_conf_conv_w, conf_ln_g=v_conf_ln_g, conf_ln_b=v_conf_ln_b, ffn_w_up=v_ffn_w_up, ffn_conv_w=v_ffn_conv_w, ffn_w_down=v_ffn_w_down)
    order = list(weights)
    me = 4 * lax.axis_index("x") + 2 * lax.axis_index("y") + lax.axis_index("c")
    x, ctx, target = x[0], ctx[0], loss_target[0]
    t, d = x.shape
    tc = ctx.shape[0]

    small_in = [ln_g, ln_b, gdn_conv_w, sconv_w, conf_conv_w, ffn_conv_w, c]
    small_axes = [2, 2, 1, 1, 1, 3, 0]
    small_pack, small_offs = _pack(small_in)
    wire_l0 = [even_w_out.astype(bf16), ffn_w_up[0].astype(bf16), ffn_w_down[0].astype(bf16)]
    wire_l1 = [odd_w_in.astype(bf16), odd_w_out.astype(bf16), ffn_w_up[1].astype(bf16), ffn_w_down[1].astype(bf16)]
    gather_l0, token_a = exchange_start(wire_l0, False, "gather_l0_start")
    gather_l1, token_b = exchange_start(wire_l1, False, "gather_l1_start")
    small_pack = small_pack + (token_a[0:1] + token_b[0:1])
    gath = exchange([even_w_in.astype(bf16), small_pack], False, "gather_first")
    e_in = even_w_in.shape[1] * N_DEV
    e_pad = -(-e_in // LANE) * LANE
    win_e = jnp.pad(_unshard(gath[0], 1), ((0, 0), (0, e_pad - e_in)))
    sm = gath[1].reshape(N_DEV, -1)
    lng_f, lnb_f, gconv_f, sconv_f, cconv_f, fconv_f, c_all = [
        _unshard(sm[:, o:o + a.size].reshape((N_DEV,) + a.shape), ax) for a, o, ax in zip(small_in, small_offs, small_axes)]
    gw8 = _pad_rows(gconv_f, 8)
    sw8 = _pad_rows(sconv_f, 8)
    cw32 = _pad_rows(cconv_f, 32)
    fw16 = [_pad_rows(fconv_f[l].reshape(9, D_FF), 16) for l in range(DEPTH)]

    a_raw = jnp.concatenate([c_all, c_ctx[None], jnp.zeros((7, d), f32)], 0)
    ncol = ada_w.shape[2]
    ada_b_loc = lax.dynamic_slice_in_dim(ada_b, me * ncol, ncol, 1)[:, None, :]
    modpart = ada_forward(a_raw, ada_w, ada_b_loc, "ada_forward")
    mod_send = jnp.stack([jnp.transpose(modpart[:, :N_DEV], (1, 0, 2)),
                          jnp.broadcast_to(modpart[:, N_DEV][None], (N_DEV, DEPTH, ncol))], axis=2)
    mod_recv = exchange([mod_send], True, "scatter_mod")[0]
    mod = jnp.transpose(mod_recv[:, :, 0, :], (1, 0, 2)).reshape(DEPTH, 6 * d)
    modc = mod_recv[:, 0, 1, :].reshape(6 * d)
    sh_c, sc_c = modc[None, :d], modc[None, d:2 * d]
    mods = [_rows(mod[l], 6) for l in range(DEPTH)]
    lng = [[lng_f[l, j][None] for j in range(2)] for l in range(DEPTH)]
    lnb = [[lnb_f[l, j][None] for j in range(2)] for l in range(DEPTH)]

    neg_a = jnp.zeros((1, LANE), f32).at[0, 8:16].set(-jnp.exp(gdn_a_log).reshape(8))
    dt_row = jnp.zeros((1, LANE), f32).at[0, 8:16].set(gdn_dt_bias.reshape(8))
    nw_row, ps_row = gdn_norm_w[None], pool_scale[None]
    cg_row, cb_row = conf_ln_g[None], conf_ln_b[None]
    q_scale = GDN_DK ** -0.5

    sh_m, sc_m, gt_m, sh_f, sc_f, gt_f = mods[0]
    u0 = modulate(x, sc_m, sh_m, "mod_l0_mix")
    cu = modulate(ctx, sc_c, sh_c, "mod_ctx")
    p0 = matmul(u0, win_e, "nn", f32, "even_in")
    pc = matmul(cu, win_e, "nn", f32, "even_in_ctx")
    qn = gdn_conv(p0, gw8, 0, 4, q_scale, "gdn_conv_q")
    kn = gdn_conv(p0, gw8, 4, 4, 1.0, "gdn_conv_k")
    vv = gdn_conv(p0, gw8, 8, 4, None, "gdn_conv_v")
    kc = gdn_conv(pc, gw8, 4, 4, 1.0, "gdn_conv_k_ctx")
    vc = gdn_conv(pc, gw8, 8, 4, None, "gdn_conv_v_ctx")
    bg = gdn_gates(p0, neg_a, dt_row, "gdn_gates")
    bgc = gdn_gates(pc, neg_a, dt_row, "gdn_gates_ctx")
    bgt, bgtc = _gate_rows(bg), _gate_rows(bgc)
    zero_state = jnp.zeros((2, GDN_HEADS, LANE, LANE), f32)
    _, _, sallc_f, sallc_b, sfin_c = gdn_forward(kc, kc, vc, bgc, bgtc, zero_state, False, "gdn_fwd_ctx")
    o_f, o_b, sall_f, sall_b, _ = gdn_forward(qn, kn, vv, bg, bgt, sfin_c, True, "gdn_fwd")
    mix0 = jnp.concatenate([gated_rmsnorm(o_f, o_b, p0, nw_row, "gated_rmsnorm"),
                            pool_mix(p0, pool_w, ps_row, "pool_mix")], 1)
    sent, landed = exchange_wait(gather_l0, mix0, "gather_l0_wait")
    full = place_own(landed, sent, False, "gather_l0_own")
    wout_e, wup, wdown = _unshard(full[0], 0), [_unshard(full[1], 1)], [_unshard(full[2], 0)]
    y0 = matmul(mix0, wout_e, "nn", f32, "even_out")
    x1 = res_layernorm(x, y0, gt_m, lng[0][0], lnb[0][0], "resln_l0_mix")
    u1 = modulate(x1, sc_f, sh_f, "mod_l0_ffn")
    h0 = matmul(u1, wup[0], "nn", f32, "ffn_up_l0")
    f0 = ffn_conv(h0, fw16[0], "ffn_conv_l0")
    y0f = matmul(f0, wdown[0], "nn", f32, "ffn_down_l0")
    x2 = res_layernorm(x1, y0f, gt_f, lng[0][1], lnb[0][1], "resln_l0_ffn")

    sh_m1, sc_m1, gt_m1, sh_f1, sc_f1, gt_f1 = mods[1]
    sent, landed = exchange_wait(gather_l1, x2, "gather_l1_wait")
    full = place_own(landed, sent, False, "gather_l1_own")
    win_o, wout_o = _unshard(full[0], 1), _unshard(full[1], 0)
    wup.append(_unshard(full[2], 1))
    wdown.append(_unshard(full[3], 0))
    u2 = modulate(x2, sc_m1, sh_m1, "mod_l1_mix")
    p1 = matmul(u2, win_o, "nn", f32, "odd_in")
    zc = conf_conv(p1, cw32, "conf_conv")
    mix1 = jnp.concatenate([short_conv(p1, sw8, "short_conv"), ln_silu(zc, cg_row, cb_row, "conf_ln_silu")], 1)
    y1 = matmul(mix1, wout_o, "nn", f32, "odd_out")
    x3 = res_layernorm(x2, y1, gt_m1, lng[1][0], lnb[1][0], "resln_l1_mix")
    u3 = modulate(x3, sc_f1, sh_f1, "mod_l1_ffn")
    h1 = matmul(u3, wup[1], "nn", f32, "ffn_up_l1")
    f1 = ffn_conv(h1, fw16[1], "ffn_conv_l1")
    y1f = matmul(f1, wdown[1], "nn", f32, "ffn_down_l1")
    x4 = res_layernorm(x3, y1f, gt_f1, lng[1][1], lnb[1][1], "resln_l1_ffn")

    loss_row, dx4 = loss_head(x4, target, "loss_head")
    loss = lax.psum(loss_row[0, 0], ("x", "y", "c"))

    def ffn_backward(dout, x_in, y, gate, g_row, scale, u, h, f, l):
        dxr, dy, dgt, dlg, dlb = res_layernorm_bwd(dout, x_in, y, gate, g_row, f"resln_bwd_l{l}_ffn")
        df = matmul(dy, wdown[l], "nt", f32, f"ffn_down_dgrad_l{l}")
        g_down = matmul(f, dy, "tn", bf16, f"ffn_down_wgrad_l{l}")
        dh, dcw = ffn_conv_bwd(h, fw16[l], df, f"ffn_conv_bwd_l{l}")
        du = matmul(dh, wup[l], "nt", f32, f"ffn_up_dgrad_l{l}")
        g_up = matmul(u, dh, "tn", bf16, f"ffn_up_wgrad_l{l}")
        dx_in, dsc, dsh = modulate_bwd(du, x_in, scale, dxr, f"mod_bwd_l{l}_ffn")
        return dx_in, (dsh, dsc, dgt), (dlg, dlb), dcw, g_up, g_down

    dx3, dmod_f1, dln_f1, dfcw1, g_up1, g_down1 = ffn_backward(dx4, x3, y1f, gt_f1, lng[1][1], sc_f1, u3, h1, f1, 1)

    scatter_a, token = exchange_start([_shard_major(g_up1, 1), _shard_major(g_down1, 0)], True, "scatter_l1_ffn_start")
    gt_m1 = gt_m1 + token[0:1, 0:1]

    dxr, dy, dgt, dlg, dlb = res_layernorm_bwd(dx3, x2, y1, gt_m1, lng[1][0], "resln_bwd_l1_mix")
    dln_m1 = (dlg, dlb)
    dmix = matmul(dy, wout_o, "nt", f32, "odd_out_dgrad")
    g_wout_o = matmul(mix1, dy, "tn", bf16, "odd_out_wgrad")
    dgb, dgc, dhh, d_sconv = short_conv_bwd(p1, sw8, dmix, "short_conv_bwd")
    dzc, d_cg, d_cb = ln_silu_bwd(zc, cg_row, cb_row, dmix, "conf_ln_silu_bwd")
    dga, dgbb, d_cconv = conf_conv_bwd(p1, cw32, dzc, "conf_conv_bwd")
    dp1 = jnp.concatenate([dgb, dgc, dhh, dga, dgbb], 1)
    du = matmul(dp1, win_o, "nt", f32, "odd_in_dgrad")
    g_win_o = matmul(u2, dp1, "tn", bf16, "odd_in_wgrad")
    dx2, dsc, dsh = modulate_bwd(du, x2, sc_m1, dxr, "mod_bwd_l1_mix")
    dmod_m1 = (dsh, dsc, dgt)

    dx1, dmod_f0, dln_f0, dfcw0, g_up0, g_down0 = ffn_backward(dx2, x1, y0f, gt_f, lng[0][1], sc_f, u1, h0, f0, 0)

    scatter_b, token = exchange_start(
        [_shard_major(g_win_o, 1), _shard_major(g_wout_o, 0), _shard_major(g_up0, 1), _shard_major(g_down0, 0)],
        True, "scatter_mid_start")
    gt_m = gt_m + token[0:1, 0:1]

    dxr, dy, dgt, dlg, dlb = res_layernorm_bwd(dx1, x, y0, gt_m, lng[0][0], "resln_bwd_l0_mix")
    dln_m0 = (dlg, dlb)
    dmix = matmul(dy, wout_e, "nt", f32, "even_out_dgrad")
    g_wout_e = matmul(mix0, dy, "tn", bf16, "even_out_wgrad")
    d_o, dgate, d_nw = gated_rmsnorm_bwd(o_f, o_b, p0, nw_row, dmix, "gated_rmsnorm_bwd")
    dpool, d_pw, d_ps = pool_mix_bwd(p0, pool_w, ps_row, dmix, "pool_mix_bwd")
    dq_f, dq_b, dk_f, dk_b, dv_f, dv_b, dbg_f, dbg_b, ds0 = gdn_backward(
        qn, kn, vv, bg, bgt, sall_f, sall_b, d_o, zero_state, True, "gdn_bwd")
    _, _, dkc_f, dkc_b, dvc_f, dvc_b, dbgc_f, dbgc_b, _ = gdn_backward(
        kc, kc, vc, bgc, bgtc, sallc_f, sallc_b, jnp.zeros((tc, 512), f32), ds0, False, "gdn_bwd_ctx")
    dqp, dwq = gdn_conv_bwd(p0, gw8, dq_f, dq_b, 0, 4, q_scale, "gdn_conv_q_bwd")
    dkp, dwk = gdn_conv_bwd(p0, gw8, dk_f, dk_b, 4, 4, 1.0, "gdn_conv_k_bwd")
    dvp, dwv = gdn_conv_bwd(p0, gw8, dv_f, dv_b, 8, 4, None, "gdn_conv_v_bwd")
    dkcp, dwkc = gdn_conv_bwd(pc, gw8, dkc_f, dkc_b, 4, 4, 1.0, "gdn_conv_k_ctx_bwd")
    dvcp, dwvc = gdn_conv_bwd(pc, gw8, dvc_f, dvc_b, 8, 4, None, "gdn_conv_v_ctx_bwd")
    ds_l, da_l, ddt_l = gdn_gates_bwd(p0, neg_a, dt_row, dbg_f, dbg_b, "gdn_gates_bwd")
    ds_c, da_c, ddt_c = gdn_gates_bwd(pc, neg_a, dt_row, dbgc_f, dbgc_b, "gdn_gates_ctx_bwd")
    zc512 = jnp.zeros((tc, 512), bf16)
    dp_all = jnp.concatenate([
        jnp.concatenate([dqp, dkp, dvp, dgate, dpool, ds_l], 1),
        jnp.concatenate([zc512, dkcp, dvcp, zc512, zc512, ds_c], 1)], 0)
    u_all = jnp.concatenate([u0, cu], 0)
    du_all = matmul(dp_all, win_e, "nt", f32, "even_in_dgrad")
    g_win_e = matmul(u_all, dp_all, "tn", bf16, "even_in_wgrad")[:, :e_in]
    grad_x, dsc, dsh = modulate_bwd(du_all, x, sc_m, dxr, "mod_bwd_l0_mix")
    dmod_m0 = (dsh, dsc, dgt)
    _, dsc_c, dsh_c = modulate_bwd(du_all, ctx, sc_c, jnp.zeros((tc, d), f32), "mod_bwd_ctx", du_row0=t)

    dmod0 = jnp.concatenate(dmod_m0 + dmod_f0, 1)
    dmod1 = jnp.concatenate(dmod_m1 + dmod_f1, 1)
    dmodc = jnp.concatenate([dsh_c, dsc_c], 1)
    d_gconv = jnp.concatenate([dwq, dwk + dwkc, dwv + dwvc], 1)[:5]
    small_g = [dmod0, dmod1, dmodc,
               jnp.concatenate([dln_m0[0], dln_f0[0], dln_m1[0], dln_f1[0]], 0),
               jnp.concatenate([dln_m0[1], dln_f0[1], dln_m1[1], dln_f1[1]], 0),
               d_gconv, (da_l + da_c)[0, 8:16], (ddt_l + ddt_c)[0, 8:16], d_nw, d_pw, d_ps,
               d_sconv[:3], d_cconv[:31], d_cg, d_cb, jnp.stack([dfcw0[:9], dfcw1[:9]])]
    gpack, goffs = _pack(small_g)
    gparts = exchange([gpack], False, "gather_small_grads")[0]
    gsum = sum_parts(gparts, "sum_small_grads").reshape(-1)
    gs = [gsum[o:o + a.size].reshape(a.shape) for a, o in zip(small_g, goffs)]
    gflat = gparts.reshape(N_DEV, -1)
    dmodc_cols = _my_block(jnp.pad(gs[2], ((0, 0), (0, 4 * d))), 1, me)
    dm = jnp.stack([
        jnp.concatenate([_my_block(gflat[:, goffs[0]:goffs[0] + 6 * d], 1, me), dmodc_cols, jnp.zeros((7, ncol), f32)], 0),
        jnp.concatenate([_my_block(gflat[:, goffs[1]:goffs[1] + 6 * d], 1, me), jnp.zeros((8, ncol), f32)], 0)])
    g_ada_w, dcc = ada_backward(a_raw, ada_w, dm, "ada_backward")
    g_cctx = cctx_grad(exchange([dcc], False, "gather_cctx")[0], c_ctx[None], "cctx_grad")

    grads = {}
    grads["c_ctx"] = g_cctx.reshape(c_ctx.shape)
    grads["ada_b"] = jnp.concatenate([gs[0] + jnp.pad(gs[2], ((0, 0), (0, 4 * d))), gs[1]], 0)
    grads["ln_g"] = _my_block(gs[3].reshape(DEPTH, 2, d), 2, me)
    grads["ln_b"] = _my_block(gs[4].reshape(DEPTH, 2, d), 2, me)
    grads["gdn_conv_w"] = _my_block(gs[5], 1, me)
    grads["gdn_a_log"] = gs[6].reshape(2, GDN_HEADS)
    grads["gdn_dt_bias"] = gs[7].reshape(2, GDN_HEADS)
    grads["gdn_norm_w"] = gs[8].reshape(LANE)
    grads["pool_w"] = gs[9]
    grads["pool_scale"] = gs[10].reshape(-1)
    grads["sconv_w"] = _my_block(gs[11], 1, me)
    grads["conf_conv_w"] = _my_block(gs[12], 1, me)
    grads["conf_ln_g"] = gs[13].reshape(-1)
    grads["conf_ln_b"] = gs[14].reshape(-1)
    grads["ffn_conv_w"] = _my_block(gs[15].reshape(DEPTH, 3, 3, D_FF), 3, me)

    def as2d(a):
        return a.reshape(-1, a.shape[-1]) if a.ndim > 1 else a.reshape(1, -1)

    small_names = [n for n in order if n in grads]
    res = adamw_small([(as2d(grads[n]), as2d(weights[n]), as2d(mom1[n]), as2d(mom2[n])) for n in small_names], "adamw_small")
    delta, new_m, new_v = {}, {}, {}
    for n, (dl, nm, nv) in zip(small_names, res):
        delta[n], new_m[n], new_v[n] = (a.reshape(weights[n].shape) for a in (dl, nm, nv))

    recv_c = exchange([_shard_major(g_win_e, 1), _shard_major(g_wout_e, 0)], True, "scatter_last")
    sent, landed = exchange_wait(scatter_a, recv_c[0], "scatter_l1_ffn_wait")
    recv_a = place_own(landed, sent, True, "scatter_l1_ffn_own")
    sent, landed = exchange_wait(scatter_b, recv_c[0], "scatter_mid_wait")
    recv_b = place_own(landed, sent, True, "scatter_mid_own")

    def update(n, parts, w, m, v):
        cols = w.shape[-1]
        out = adamw(parts.reshape(parts.shape[0], -1, cols), w.reshape(-1, cols), m.reshape(-1, cols), v.reshape(-1, cols), f"adamw_{n}")
        return [a.reshape(w.shape) for a in out]

    for n, parts in (("even_w_in", recv_c[0]), ("even_w_out", recv_c[1]), ("odd_w_in", recv_b[0]), ("odd_w_out", recv_b[1]),
                     ("ada_w", g_ada_w[None])):
        grads[n], delta[n], new_m[n], new_v[n] = update(n, parts, weights[n], mom1[n], mom2[n])
    for n, per_layer in (("ffn_w_up", (recv_b[2], recv_a[0])), ("ffn_w_down", (recv_b[3], recv_a[1]))):
        outs = [update(f"{n}_l{l}", per_layer[l], weights[n][l], mom1[n][l], mom2[n][l]) for l in range(DEPTH)]
        grads[n], delta[n], new_m[n], new_v[n] = (jnp.stack([outs[l][j] for l in range(DEPTH)]) for j in range(4))

    return (loss, grad_x[None], *[grads[n] for n in order], *[delta[n] for n in order],
            *[new_m[n] for n in order], *[new_v[n] for n in order])
```

```python
import functools
import math

import jax
import jax.numpy as jnp
from jax import lax
from jax.experimental import pallas as pl
from jax.experimental.pallas import tpu as pltpu

f32 = jnp.float32
bf16 = jnp.bfloat16
SDS = jax.ShapeDtypeStruct

N_DEV = 8
D_MODEL = 1024
DEPTH = 2
GRID_W = 64
GDN_HEADS = 4
GDN_DK = 128
CHUNK = 64
POOL_WINDOWS = (2, 4, 8, 16)
D_FF = 2816
ALPHA = (2 * DEPTH) ** 0.25
LN_EPS = 1e-5
RMS_EPS = 1e-6
LANE = 128
PAD_ROWS = 72
CONV_ROWS = 256
VMEM_LIMIT = 56 * 2**20

ADAM_LR, ADAM_B1, ADAM_B2, ADAM_EPS, ADAM_WD, ADAM_STEP = 0.001, 0.9, 0.999, 1e-08, 0.01, 10

HI = lax.Precision.HIGHEST


def _cparams(sem=None):
    return pltpu.CompilerParams(dimension_semantics=sem, vmem_limit_bytes=VMEM_LIMIT)


def _silu(x):
    return x * jax.nn.sigmoid(x)


def _dsilu(x):
    s = jax.nn.sigmoid(x)
    return s * (1.0 + x * (1.0 - s))


def _dotb(a, b, dims=(((1,), (0,)), ((), ()))):
    return lax.dot_general(a.astype(bf16), b.astype(bf16), dims, preferred_element_type=f32)


def _dotb_nt(a, b):
    return _dotb(a, b, (((1,), (1,)), ((), ())))


def _dotb_tn(a, b):
    return _dotb(a, b, (((0,), (0,)), ((), ())))


def _dotf(a, b, dims=(((1,), (0,)), ((), ()))):
    return lax.dot_general(a, b, dims, preferred_element_type=f32, precision=HI)


def _pick(n, cands):
    for c in cands:
        if n % c == 0:
            return c
    return n


def matmul(a, b, mode, out_dtype, name):
    if mode == "nn":
        (M, K), N = a.shape, b.shape[1]
    elif mode == "nt":
        (M, K), N = a.shape, b.shape[0]
    else:
        (K, M), N = a.shape, b.shape[1]
    tm = _pick(M, (1024, 768, 512, 256, 128)) if mode != "tn" else _pick(M, (1024, 1408, 512, 256, 128))
    tn = _pick(N, (1024, 1408, 896, 768, 640, 512, 384, 256, 128))
    tk = _pick(K, (1024, 1408, 896, 768, 640, 512, 384, 256, 128)) if mode != "tn" else _pick(K, (1024, 512, 256))
    nk = K // tk
    dims = {"nn": (((1,), (0,)), ((), ())), "nt": (((1,), (1,)), ((), ())), "tn": (((0,), (0,)), ((), ()))}[mode]

    def body(a_ref, b_ref, o_ref, acc_ref):
        k = pl.program_id(2)
        part = lax.dot_general(a_ref[...].astype(bf16), b_ref[...].astype(bf16), dims, preferred_element_type=f32)

        @pl.when(k == 0)
        def _():
            acc_ref[...] = part

        @pl.when(k > 0)
        def _():
            acc_ref[...] += part

        @pl.when(k == nk - 1)
        def _():
            o_ref[...] = acc_ref[...].astype(out_dtype)

    a_spec = {"nn": pl.BlockSpec((tm, tk), lambda i, j, k: (i, k)),
              "nt": pl.BlockSpec((tm, tk), lambda i, j, k: (i, k)),
              "tn": pl.BlockSpec((tk, tm), lambda i, j, k: (k, i))}[mode]
    b_spec = {"nn": pl.BlockSpec((tk, tn), lambda i, j, k: (k, j)),
              "nt": pl.BlockSpec((tn, tk), lambda i, j, k: (j, k)),
              "tn": pl.BlockSpec((tk, tn), lambda i, j, k: (k, j))}[mode]
    return pl.pallas_call(
        body, out_shape=SDS((M, N), out_dtype), grid=(M // tm, N // tn, nk),
        in_specs=[a_spec, b_spec], out_specs=pl.BlockSpec((tm, tn), lambda i, j, k: (i, j)),
        scratch_shapes=[pltpu.VMEM((tm, tn), f32)], name=name,
        compiler_params=_cparams(("parallel", "parallel", "arbitrary")),
    )(a, b)


def _row_tile(t):
    return _pick(t, (512, 256, 128, 64, 32, 16, 8))


def _row_spec(tt, d):
    return pl.BlockSpec((tt, d), lambda i: (i, 0))


def _vec_spec(d):
    return pl.BlockSpec((1, d), lambda i: (0, 0))


def _acc_rows(ref, val):
    @pl.when(pl.program_id(0) == 0)
    def _():
        ref[...] = val

    @pl.when(pl.program_id(0) > 0)
    def _():
        ref[...] += val


def modulate(x, scale, shift, name):
    t, d = x.shape
    tt = _row_tile(t)

    def body(x_ref, sc_ref, sh_ref, o_ref):
        o_ref[...] = (x_ref[...] * (1.0 + sc_ref[...]) + sh_ref[...]).astype(bf16)

    return pl.pallas_call(
        body, out_shape=SDS((t, d), bf16), grid=(t // tt,),
        in_specs=[_row_spec(tt, d), _vec_spec(d), _vec_spec(d)], out_specs=_row_spec(tt, d),
        name=name, compiler_params=_cparams(("parallel",)),
    )(x, scale, shift)


def modulate_bwd(du, x, scale, dres, name, du_row0=0):
    t, d = x.shape
    tt = _row_tile(t)
    blk0 = du_row0 // tt

    def body(du_ref, x_ref, sc_ref, dres_ref, dx_ref, dsc_ref, dsh_ref):
        du_v = du_ref[...]
        dx_ref[...] = du_v * (1.0 + sc_ref[...]) + dres_ref[...]
        _acc_rows(dsc_ref, jnp.sum(du_v * x_ref[...], axis=0, keepdims=True))
        _acc_rows(dsh_ref, jnp.sum(du_v, axis=0, keepdims=True))

    return pl.pallas_call(
        body, out_shape=(SDS((t, d), f32), SDS((1, d), f32), SDS((1, d), f32)), grid=(t // tt,),
        in_specs=[pl.BlockSpec((tt, d), lambda i: (i + blk0, 0)), _row_spec(tt, d), _vec_spec(d), _row_spec(tt, d)],
        out_specs=(_row_spec(tt, d), _vec_spec(d), _vec_spec(d)),
        name=name, compiler_params=_cparams(("arbitrary",)),
    )(du, x, scale, dres)


def _ln_stats(z):
    mu = jnp.mean(z, axis=-1, keepdims=True)
    zc = z - mu
    var = jnp.mean(zc * zc, axis=-1, keepdims=True)
    rstd = lax.rsqrt(var + LN_EPS)
    return zc * rstd, rstd


def _ln_bwd(dxhat, xhat, rstd):
    m1 = jnp.mean(dxhat, axis=-1, keepdims=True)
    m2 = jnp.mean(dxhat * xhat, axis=-1, keepdims=True)
    return rstd * (dxhat - m1 - xhat * m2)


def res_layernorm(x, y, gate, g, b, name):
    t, d = x.shape
    tt = _row_tile(t)

    def body(x_ref, y_ref, gt_ref, g_ref, b_ref, o_ref):
        xhat, _ = _ln_stats(ALPHA * x_ref[...] + gt_ref[...] * y_ref[...])
        o_ref[...] = xhat * g_ref[...] + b_ref[...]

    return pl.pallas_call(
        body, out_shape=SDS((t, d), f32), grid=(t // tt,),
        in_specs=[_row_spec(tt, d), _row_spec(tt, d), _vec_spec(d), _vec_spec(d), _vec_spec(d)],
        out_specs=_row_spec(tt, d), name=name, compiler_params=_cparams(("parallel",)),
    )(x, y, gate, g, b)


def res_layernorm_bwd(dout, x, y, gate, g, name):
    t, d = x.shape
    tt = _row_tile(t)

    def body(do_ref, x_ref, y_ref, gt_ref, g_ref, dxr_ref, dy_ref, dgt_ref, dg_ref, db_ref):
        y_v = y_ref[...]
        do_v = do_ref[...]
        xhat, rstd = _ln_stats(ALPHA * x_ref[...] + gt_ref[...] * y_v)
        dz = _ln_bwd(do_v * g_ref[...], xhat, rstd)
        dxr_ref[...] = ALPHA * dz
        dy_ref[...] = (gt_ref[...] * dz).astype(bf16)
        _acc_rows(dgt_ref, jnp.sum(dz * y_v, axis=0, keepdims=True))
        _acc_rows(dg_ref, jnp.sum(do_v * xhat, axis=0, keepdims=True))
        _acc_rows(db_ref, jnp.sum(do_v, axis=0, keepdims=True))

    vec = SDS((1, d), f32)
    return pl.pallas_call(
        body, out_shape=(SDS((t, d), f32), SDS((t, d), bf16), vec, vec, vec), grid=(t // tt,),
        in_specs=[_row_spec(tt, d), _row_spec(tt, d), _row_spec(tt, d), _vec_spec(d), _vec_spec(d)],
        out_specs=(_row_spec(tt, d), _row_spec(tt, d), _vec_spec(d), _vec_spec(d), _vec_spec(d)),
        name=name, compiler_params=_cparams(("arbitrary",)),
    )(dout, x, y, gate, g)


def loss_head(y, target, name):
    t, d = y.shape
    tt = _row_tile(t)

    def body(y_ref, t_ref, l_ref, dy_ref):
        e = y_ref[...] - t_ref[...]
        dy_ref[...] = e * (1.0 / d)
        part = jnp.sum(jnp.sum(e * e, axis=1, keepdims=True), axis=0, keepdims=True) * (0.5 / d)
        _acc_rows(l_ref, jnp.broadcast_to(part, (1, LANE)))

    return pl.pallas_call(
        body, out_shape=(SDS((1, LANE), f32), SDS((t, d), f32)), grid=(t // tt,),
        in_specs=[_row_spec(tt, d), _row_spec(tt, d)],
        out_specs=(pl.BlockSpec((1, LANE), lambda i: (0, 0)), _row_spec(tt, d)),
        name=name, compiler_params=_cparams(("arbitrary",)),
    )(y, target)


def _fill_pad(pad_ref, val, t):
    zeros = jnp.zeros((PAD_ROWS, LANE), f32)
    pad_ref[0:PAD_ROWS, :] = zeros
    pad_ref[PAD_ROWS + t:2 * PAD_ROWS + t, :] = zeros
    pad_ref[PAD_ROWS:PAD_ROWS + t, :] = val


def _grid_pads_set(pads, r0, val):
    rows = val.shape[0]
    col = (lax.broadcasted_iota(jnp.int32, (rows, 1), 0) + r0) % GRID_W
    sl = slice(PAD_ROWS + r0, PAD_ROWS + r0 + rows)
    pads[0][sl, :] = val * (col <= GRID_W - 2).astype(f32)
    pads[1][sl, :] = val
    pads[2][sl, :] = val * (col >= 1).astype(f32)


def _grid_pads_clear_edges(pads, t):
    zeros = jnp.zeros((PAD_ROWS, LANE), f32)
    for p in pads:
        p[0:PAD_ROWS, :] = zeros
        p[PAD_ROWS + t:2 * PAD_ROWS + t, :] = zeros


def _tap_source(pads, dc):
    return pads if dc is None else pads[dc + 1]


def _taps_apply(pads, w_ref, taps, r0, rows):
    acc = jnp.zeros((rows, LANE), f32)
    for off, dc, wi in taps:
        xs = _tap_source(pads, dc)[PAD_ROWS + r0 + off:PAD_ROWS + r0 + off + rows, :]
        acc = acc + w_ref[wi:wi + 1, :] * xs
    return acc


def _taps_wgrad(pads, dy, taps, r0, rows, nw):
    out = jnp.zeros((nw, LANE), f32)
    rid = lax.broadcasted_iota(jnp.int32, (nw, 1), 0)
    for off, dc, wi in taps:
        xs = _tap_source(pads, dc)[PAD_ROWS + r0 + off:PAD_ROWS + r0 + off + rows, :]
        s = jnp.sum(dy * xs, axis=0, keepdims=True)
        out = out + jnp.where(rid == wi, s, 0.0)
    return out


def _transpose_taps(taps):
    return [(-off, None if dc is None else -dc, wi) for off, dc, wi in taps]


def _taps_1d(width):
    return [(j - width // 2, None, j) for j in range(width)]


def _taps_grid3():
    return [(GRID_W * dr + dc, dc, 3 * (dr + 1) + (dc + 1)) for dr in (-1, 0, 1) for dc in (-1, 0, 1)]


def _row_chunks(t):
    r = min(CONV_ROWS, t)
    return [(i * r, r) for i in range(t // r)]


def _col_spec(t, off):
    return pl.BlockSpec((t, LANE), lambda c: (0, c + off))


def _w_spec(nw, off=0):
    return pl.BlockSpec((nw, LANE), lambda c: (0, c + off))


def gdn_conv(p, w, col0, nblk, norm_scale, name):
    t = p.shape[0]
    nw = w.shape[0]
    taps = _taps_1d(5)

    def body(p_ref, w_ref, o_ref, pad_ref):
        _fill_pad(pad_ref, p_ref[...], t)
        for r0, rows in _row_chunks(t):
            a = _silu(_taps_apply(pad_ref, w_ref, taps, r0, rows))
            if norm_scale is not None:
                a = a * (lax.rsqrt(jnp.sum(a * a, axis=-1, keepdims=True) + RMS_EPS) * norm_scale)
            o_ref[r0:r0 + rows, :] = a

    return pl.pallas_call(
        body, out_shape=SDS((t, nblk * LANE), f32), grid=(nblk,),
        in_specs=[_col_spec(t, col0), _w_spec(nw, col0)], out_specs=_col_spec(t, 0),
        scratch_shapes=[pltpu.VMEM((t + 2 * PAD_ROWS, LANE), f32)], name=name,
        compiler_params=_cparams(("parallel",)),
    )(p, w)


def gdn_conv_bwd(p, w, d_a, d_b, col0, nblk, norm_scale, name):
    t = p.shape[0]
    nw = w.shape[0]
    taps = _taps_1d(5)
    ttaps = _transpose_taps(taps)

    def body(p_ref, w_ref, da_ref, db_ref, dp_ref, dw_ref, pad_ref, gpad_ref):
        _fill_pad(pad_ref, p_ref[...], t)
        for r0, rows in _row_chunks(t):
            pre = _taps_apply(pad_ref, w_ref, taps, r0, rows)
            a = _silu(pre)
            dy = da_ref[r0:r0 + rows, :] + db_ref[r0:r0 + rows, :]
            if norm_scale is not None:
                r = lax.rsqrt(jnp.sum(a * a, axis=-1, keepdims=True) + RMS_EPS)
                da = norm_scale * (dy * r - a * (r * r * r) * jnp.sum(dy * a, axis=-1, keepdims=True))
            else:
                da = dy
            gpad_ref[PAD_ROWS + r0:PAD_ROWS + r0 + rows, :] = da * _dsilu(pre)
        zeros = jnp.zeros((PAD_ROWS, LANE), f32)
        gpad_ref[0:PAD_ROWS, :] = zeros
        gpad_ref[PAD_ROWS + t:2 * PAD_ROWS + t, :] = zeros
        dw = jnp.zeros((nw, LANE), f32)
        for r0, rows in _row_chunks(t):
            dp_ref[r0:r0 + rows, :] = _taps_apply(gpad_ref, w_ref, ttaps, r0, rows).astype(bf16)
            dw = dw + _taps_wgrad(pad_ref, gpad_ref[PAD_ROWS + r0:PAD_ROWS + r0 + rows, :], taps, r0, rows, nw)
        dw_ref[...] = dw

    return pl.pallas_call(
        body, out_shape=(SDS((t, nblk * LANE), bf16), SDS((nw, nblk * LANE), f32)), grid=(nblk,),
        in_specs=[_col_spec(t, col0), _w_spec(nw, col0), _col_spec(t, 0), _col_spec(t, 0)],
        out_specs=(_col_spec(t, 0), _w_spec(nw)),
        scratch_shapes=[pltpu.VMEM((t + 2 * PAD_ROWS, LANE), f32)] * 2, name=name,
        compiler_params=_cparams(("parallel",)),
    )(p, w, d_a, d_b)


def short_conv(p, w, name):
    t = p.shape[0]
    nw = w.shape[0]
    taps = _taps_1d(3)

    def body(gb_ref, gc_ref, h_ref, w_ref, o_ref, pad_ref):
        _fill_pad(pad_ref, gc_ref[...] * h_ref[...], t)
        for r0, rows in _row_chunks(t):
            o_ref[r0:r0 + rows, :] = (gb_ref[r0:r0 + rows, :] * _taps_apply(pad_ref, w_ref, taps, r0, rows)).astype(bf16)

    return pl.pallas_call(
        body, out_shape=SDS((t, 4 * LANE), bf16), grid=(4,),
        in_specs=[_col_spec(t, 0), _col_spec(t, 4), _col_spec(t, 8), _w_spec(nw)], out_specs=_col_spec(t, 0),
        scratch_shapes=[pltpu.VMEM((t + 2 * PAD_ROWS, LANE), f32)], name=name,
        compiler_params=_cparams(("parallel",)),
    )(p, p, p, w)


def short_conv_bwd(p, w, dy, name):
    t = p.shape[0]
    nw = w.shape[0]
    taps = _taps_1d(3)
    ttaps = _transpose_taps(taps)

    def body(gb_ref, gc_ref, h_ref, w_ref, dy_ref, dgb_ref, dgc_ref, dh_ref, dw_ref, pad_ref, gpad_ref):
        _fill_pad(pad_ref, gc_ref[...] * h_ref[...], t)
        _fill_pad(gpad_ref, dy_ref[...] * gb_ref[...], t)
        dw = jnp.zeros((nw, LANE), f32)
        for r0, rows in _row_chunks(t):
            sl = slice(r0, r0 + rows)
            dgb_ref[sl, :] = (dy_ref[sl, :] * _taps_apply(pad_ref, w_ref, taps, r0, rows)).astype(bf16)
            dm = _taps_apply(gpad_ref, w_ref, ttaps, r0, rows)
            dgc_ref[sl, :] = (dm * h_ref[sl, :]).astype(bf16)
            dh_ref[sl, :] = (dm * gc_ref[sl, :]).astype(bf16)
            dw = dw + _taps_wgrad(pad_ref, gpad_ref[PAD_ROWS + r0:PAD_ROWS + r0 + rows, :], taps, r0, rows, nw)
        dw_ref[...] = dw

    blk = SDS((t, 4 * LANE), bf16)
    return pl.pallas_call(
        body, out_shape=(blk, blk, blk, SDS((nw, 4 * LANE), f32)), grid=(4,),
        in_specs=[_col_spec(t, 0), _col_spec(t, 4), _col_spec(t, 8), _w_spec(nw), _col_spec(t, 0)],
        out_specs=(_col_spec(t, 0), _col_spec(t, 0), _col_spec(t, 0), _w_spec(nw)),
        scratch_shapes=[pltpu.VMEM((t + 2 * PAD_ROWS, LANE), f32)] * 2, name=name,
        compiler_params=_cparams(("parallel",)),
    )(p, p, p, w, dy)


def conf_conv(p, w, name):
    t = p.shape[0]
    nw = w.shape[0]
    taps = _taps_1d(31)

    def body(a_ref, b_ref, w_ref, o_ref, pad_ref):
        _fill_pad(pad_ref, a_ref[...] * jax.nn.sigmoid(b_ref[...]), t)
        for r0, rows in _row_chunks(t):
            o_ref[r0:r0 + rows, :] = _taps_apply(pad_ref, w_ref, taps, r0, rows)

    return pl.pallas_call(
        body, out_shape=SDS((t, 4 * LANE), f32), grid=(4,),
        in_specs=[_col_spec(t, 12), _col_spec(t, 16), _w_spec(nw)], out_specs=_col_spec(t, 0),
        scratch_shapes=[pltpu.VMEM((t + 2 * PAD_ROWS, LANE), f32)], name=name,
        compiler_params=_cparams(("parallel",)),
    )(p, p, w)


def conf_conv_bwd(p, w, dz, name):
    t = p.shape[0]
    nw = w.shape[0]
    taps = _taps_1d(31)
    ttaps = _transpose_taps(taps)

    def body(a_ref, b_ref, w_ref, dz_ref, da_ref, db_ref, dw_ref, pad_ref, gpad_ref):
        _fill_pad(pad_ref, a_ref[...] * jax.nn.sigmoid(b_ref[...]), t)
        _fill_pad(gpad_ref, dz_ref[...], t)
        dw = jnp.zeros((nw, LANE), f32)
        for r0, rows in _row_chunks(t):
            sl = slice(r0, r0 + rows)
            dm = _taps_apply(gpad_ref, w_ref, ttaps, r0, rows)
            sg = jax.nn.sigmoid(b_ref[sl, :])
            da_ref[sl, :] = (dm * sg).astype(bf16)
            db_ref[sl, :] = (dm * a_ref[sl, :] * sg * (1.0 - sg)).astype(bf16)
            dw = dw + _taps_wgrad(pad_ref, dz_ref[sl, :], taps, r0, rows, nw)
        dw_ref[...] = dw

    blk = SDS((t, 4 * LANE), bf16)
    return pl.pallas_call(
        body, out_shape=(blk, blk, SDS((nw, 4 * LANE), f32)), grid=(4,),
        in_specs=[_col_spec(t, 12), _col_spec(t, 16), _w_spec(nw), _col_spec(t, 0)],
        out_specs=(_col_spec(t, 0), _col_spec(t, 0), _w_spec(nw)),
        scratch_shapes=[pltpu.VMEM((t + 2 * PAD_ROWS, LANE), f32)] * 2, name=name,
        compiler_params=_cparams(("parallel",)),
    )(p, p, w, dz)


def ffn_conv(h, w, name):
    t = h.shape[0]
    nblk = D_FF // LANE
    nw = w.shape[0]
    taps = _taps_grid3()

    def body(a_ref, g_ref, w_ref, o_ref, *pads):
        _grid_pads_clear_edges(pads, t)
        for r0, rows in _row_chunks(t):
            _grid_pads_set(pads, r0, a_ref[r0:r0 + rows, :])
        for r0, rows in _row_chunks(t):
            o_ref[r0:r0 + rows, :] = (_silu(_taps_apply(pads, w_ref, taps, r0, rows)) * g_ref[r0:r0 + rows, :]).astype(bf16)

    return pl.pallas_call(
        body, out_shape=SDS((t, D_FF), bf16), grid=(nblk,),
        in_specs=[_col_spec(t, 0), _col_spec(t, nblk), _w_spec(nw)], out_specs=_col_spec(t, 0),
        scratch_shapes=[pltpu.VMEM((t + 2 * PAD_ROWS, LANE), f32)] * 3, name=name,
        compiler_params=_cparams(("parallel",)),
    )(h, h, w)


def ffn_conv_bwd(h, w, df, name):
    t = h.shape[0]
    nblk = D_FF // LANE
    nw = w.shape[0]
    taps = _taps_grid3()
    ttaps = _transpose_taps(taps)

    def body(a_ref, g_ref, w_ref, df_ref, dh_ref, dw_ref, pre_ref, *all_pads):
        half = pl.program_id(1)
        pads, gpads = all_pads[:3], all_pads[3:]

        @pl.when(half == 0)
        def _():
            _grid_pads_clear_edges(all_pads, t)
            for r0, rows in _row_chunks(t):
                _grid_pads_set(pads, r0, a_ref[r0:r0 + rows, :])
            for r0, rows in _row_chunks(t):
                sl = slice(r0, r0 + rows)
                pre = _taps_apply(pads, w_ref, taps, r0, rows)
                pre_ref[sl, :] = pre
                _grid_pads_set(gpads, r0, df_ref[sl, :] * g_ref[sl, :] * _dsilu(pre))
            dw = jnp.zeros((nw, LANE), f32)
            for r0, rows in _row_chunks(t):
                dh_ref[r0:r0 + rows, :] = _taps_apply(gpads, w_ref, ttaps, r0, rows).astype(bf16)
                dw = dw + _taps_wgrad(pads, gpads[1][PAD_ROWS + r0:PAD_ROWS + r0 + rows, :], taps, r0, rows, nw)
            dw_ref[...] = dw

        @pl.when(half == 1)
        def _():
            for r0, rows in _row_chunks(t):
                sl = slice(r0, r0 + rows)
                dh_ref[sl, :] = (df_ref[sl, :] * _silu(pre_ref[sl, :])).astype(bf16)

    cspec = lambda off: pl.BlockSpec((t, LANE), lambda c, s: (0, c + off))
    return pl.pallas_call(
        body, out_shape=(SDS((t, 2 * D_FF), bf16), SDS((nw, D_FF), f32)), grid=(nblk, 2),
        in_specs=[cspec(0), cspec(nblk), pl.BlockSpec((nw, LANE), lambda c, s: (0, c)), cspec(0)],
        out_specs=(pl.BlockSpec((t, LANE), lambda c, s: (0, c + nblk * s)), pl.BlockSpec((nw, LANE), lambda c, s: (0, c))),
        scratch_shapes=[pltpu.VMEM((t, LANE), f32)] + [pltpu.VMEM((t + 2 * PAD_ROWS, LANE), f32)] * 6, name=name,
        compiler_params=_cparams(("parallel", "arbitrary")),
    )(h, h, w, df)


def _pool_count(r0, rows, win, t):
    pos = lax.broadcasted_iota(jnp.int32, (rows, 1), 0) + r0
    lo = jnp.clip(pos - win // 2, 0, t)
    hi = jnp.clip(pos - win // 2 + win, 0, t)
    return (hi - lo).astype(f32)


def _window_sum(pad_ref, r0, rows, lo, hi):
    acc = jnp.zeros((rows, LANE), f32)
    for off in range(lo, hi):
        acc = acc + pad_ref[PAD_ROWS + r0 + off:PAD_ROWS + r0 + off + rows, :]
    return acc


def pool_mix(p, pool_w, pool_scale, name):
    t = p.shape[0]

    def body(x_ref, w_ref, s_ref, o_ref, pad_ref):
        for gi, win in enumerate(POOL_WINDOWS):
            cs = slice(gi * LANE, (gi + 1) * LANE)
            _fill_pad(pad_ref, x_ref[:, cs], t)
            wg = w_ref[gi].astype(bf16)
            for r0, rows in _row_chunks(t):
                pooled = _window_sum(pad_ref, r0, rows, -(win // 2), win - win // 2) / _pool_count(r0, rows, win, t) - x_ref[r0:r0 + rows, cs]
                o_ref[r0:r0 + rows, cs] = (_dotb(pooled, wg) * s_ref[:, cs]).astype(bf16)

    return pl.pallas_call(
        body, out_shape=SDS((t, 512), bf16), grid=(1,),
        in_specs=[pl.BlockSpec((t, 512), lambda i: (0, 4)), pl.BlockSpec((4, LANE, LANE), lambda i: (0, 0, 0)),
                  pl.BlockSpec((1, 512), lambda i: (0, 0))],
        out_specs=pl.BlockSpec((t, 512), lambda i: (0, 0)),
        scratch_shapes=[pltpu.VMEM((t + 2 * PAD_ROWS, LANE), f32)], name=name,
        compiler_params=_cparams(("arbitrary",)),
    )(p, pool_w, pool_scale)


def pool_mix_bwd(p, pool_w, pool_scale, dmix, name):
    t = p.shape[0]

    def body(x_ref, w_ref, s_ref, dy_ref, dp_ref, dw_ref, ds_ref, pad_ref, gpad_ref, dpool_ref):
        for gi, win in enumerate(POOL_WINDOWS):
            cs = slice(gi * LANE, (gi + 1) * LANE)
            h = win // 2
            _fill_pad(pad_ref, x_ref[:, cs], t)
            wg = w_ref[gi].astype(bf16)
            dw = jnp.zeros((LANE, LANE), f32)
            ds = jnp.zeros((1, LANE), f32)
            zeros = jnp.zeros((PAD_ROWS, LANE), f32)
            gpad_ref[0:PAD_ROWS, :] = zeros
            gpad_ref[PAD_ROWS + t:2 * PAD_ROWS + t, :] = zeros
            for r0, rows in _row_chunks(t):
                cnt = _pool_count(r0, rows, win, t)
                pooled = _window_sum(pad_ref, r0, rows, -h, win - h) / cnt - x_ref[r0:r0 + rows, cs]
                dy = dy_ref[r0:r0 + rows, cs]
                ds = ds + jnp.sum(dy * _dotb(pooled, wg), axis=0, keepdims=True)
                dypre = dy * s_ref[:, cs]
                dw = dw + _dotb_tn(pooled, dypre)
                dpooled = _dotb_nt(dypre, wg)
                gpad_ref[PAD_ROWS + r0:PAD_ROWS + r0 + rows, :] = dpooled / cnt
                dpool_ref[r0:r0 + rows, :] = dpooled
            dw_ref[gi] = dw
            ds_ref[:, cs] = ds
            for r0, rows in _row_chunks(t):
                dx = _window_sum(gpad_ref, r0, rows, -h + 1, h + 1) - dpool_ref[r0:r0 + rows, :]
                dp_ref[r0:r0 + rows, cs] = dx.astype(bf16)

    return pl.pallas_call(
        body, out_shape=(SDS((t, 512), bf16), SDS((4, LANE, LANE), f32), SDS((1, 512), f32)), grid=(1,),
        in_specs=[pl.BlockSpec((t, 512), lambda i: (0, 4)), pl.BlockSpec((4, LANE, LANE), lambda i: (0, 0, 0)),
                  pl.BlockSpec((1, 512), lambda i: (0, 0)), pl.BlockSpec((t, 512), lambda i: (0, 1))],
        out_specs=(pl.BlockSpec((t, 512), lambda i: (0, 0)), pl.BlockSpec((4, LANE, LANE), lambda i: (0, 0, 0)),
                   pl.BlockSpec((1, 512), lambda i: (0, 0))),
        scratch_shapes=[pltpu.VMEM((t + 2 * PAD_ROWS, LANE), f32)] * 2 + [pltpu.VMEM((t, LANE), f32)], name=name,
        compiler_params=_cparams(("arbitrary",)),
    )(p, pool_w, pool_scale, dmix)


def gated_rmsnorm(o_a, o_b, p, norm_w, name):
    t = o_a.shape[0]
    tt = _row_tile(t)

    def body(oa_ref, ob_ref, g_ref, nw_ref, y_ref):
        for h in range(GDN_HEADS):
            cs = slice(h * LANE, (h + 1) * LANE)
            o = oa_ref[:, cs] + ob_ref[:, cs]
            r = lax.rsqrt(jnp.mean(o * o, axis=-1, keepdims=True) + RMS_EPS)
            y_ref[:, cs] = (o * r * nw_ref[...] * _silu(g_ref[:, cs])).astype(bf16)

    return pl.pallas_call(
        body, out_shape=SDS((t, 512), bf16), grid=(t // tt,),
        in_specs=[_row_spec(tt, 512), _row_spec(tt, 512), pl.BlockSpec((tt, 512), lambda i: (i, 3)), _vec_spec(LANE)],
        out_specs=_row_spec(tt, 512), name=name, compiler_params=_cparams(("parallel",)),
    )(o_a, o_b, p, norm_w)


def gated_rmsnorm_bwd(o_a, o_b, p, norm_w, dmix, name):
    t = o_a.shape[0]
    tt = _row_tile(t)

    def body(oa_ref, ob_ref, g_ref, nw_ref, dy_ref, do_ref, dg_ref, dnw_ref):
        dnw = jnp.zeros((1, LANE), f32)
        for h in range(GDN_HEADS):
            cs = slice(h * LANE, (h + 1) * LANE)
            o = oa_ref[:, cs] + ob_ref[:, cs]
            r = lax.rsqrt(jnp.mean(o * o, axis=-1, keepdims=True) + RMS_EPS)
            gate = g_ref[:, cs]
            dy = dy_ref[:, cs]
            dy1 = dy * _silu(gate)
            dg_ref[:, cs] = (dy * (o * r * nw_ref[...]) * _dsilu(gate)).astype(bf16)
            dnw = dnw + jnp.sum(dy1 * o * r, axis=0, keepdims=True)
            dn = dy1 * nw_ref[...]
            do_ref[:, cs] = r * dn - o * (r * r * r) * jnp.mean(dn * o, axis=-1, keepdims=True)
        _acc_rows(dnw_ref, dnw)

    return pl.pallas_call(
        body, out_shape=(SDS((t, 512), f32), SDS((t, 512), bf16), SDS((1, LANE), f32)), grid=(t // tt,),
        in_specs=[_row_spec(tt, 512), _row_spec(tt, 512), pl.BlockSpec((tt, 512), lambda i: (i, 3)), _vec_spec(LANE),
                  _row_spec(tt, 512)],
        out_specs=(_row_spec(tt, 512), _row_spec(tt, 512), _vec_spec(LANE)),
        name=name, compiler_params=_cparams(("arbitrary",)),
    )(o_a, o_b, p, norm_w, dmix)


def ln_silu(z, g, b, name):
    t, d = z.shape
    tt = _row_tile(t)

    def body(z_ref, g_ref, b_ref, o_ref):
        xhat, _ = _ln_stats(z_ref[...])
        o_ref[...] = _silu(xhat * g_ref[...] + b_ref[...]).astype(bf16)

    return pl.pallas_call(
        body, out_shape=SDS((t, d), bf16), grid=(t // tt,),
        in_specs=[_row_spec(tt, d), _vec_spec(d), _vec_spec(d)], out_specs=_row_spec(tt, d),
        name=name, compiler_params=_cparams(("parallel",)),
    )(z, g, b)


def ln_silu_bwd(z, g, b, dmix, name):
    t, d = z.shape
    tt = _row_tile(t)

    def body(z_ref, g_ref, b_ref, dy_ref, dz_ref, dg_ref, db_ref):
        xhat, rstd = _ln_stats(z_ref[...])
        dn = dy_ref[...] * _dsilu(xhat * g_ref[...] + b_ref[...])
        dz_ref[...] = _ln_bwd(dn * g_ref[...], xhat, rstd)
        _acc_rows(dg_ref, jnp.sum(dn * xhat, axis=0, keepdims=True))
        _acc_rows(db_ref, jnp.sum(dn, axis=0, keepdims=True))

    return pl.pallas_call(
        body, out_shape=(SDS((t, d), f32), SDS((1, d), f32), SDS((1, d), f32)), grid=(t // tt,),
        in_specs=[_row_spec(tt, d), _vec_spec(d), _vec_spec(d), pl.BlockSpec((tt, d), lambda i: (i, 1))],
        out_specs=(_row_spec(tt, d), _vec_spec(d), _vec_spec(d)),
        name=name, compiler_params=_cparams(("arbitrary",)),
    )(z, g, b, dmix)


def gdn_gates(p, neg_a, dt_bias, name):
    t = p.shape[0]
    tt = _row_tile(t)

    def body(s_ref, na_ref, dt_ref, o_ref):
        s = s_ref[...]
        col = lax.broadcasted_iota(jnp.int32, s.shape, 1)
        o_ref[...] = jnp.where(col < 8, jax.nn.sigmoid(s), na_ref[...] * jax.nn.softplus(s + dt_ref[...]))

    return pl.pallas_call(
        body, out_shape=SDS((t, LANE), f32), grid=(t // tt,),
        in_specs=[pl.BlockSpec((tt, LANE), lambda i: (i, 20)), _vec_spec(LANE), _vec_spec(LANE)],
        out_specs=_row_spec(tt, LANE), name=name, compiler_params=_cparams(("parallel",)),
    )(p, neg_a, dt_bias)


def gdn_gates_bwd(p, neg_a, dt_bias, dbg_a, dbg_b, name):
    t = p.shape[0]
    tt = _row_tile(t)

    def body(s_ref, na_ref, dt_ref, d_ref, d2_ref, ds_ref, da_ref, ddt_ref):
        s = s_ref[...]
        d = d_ref[...] + d2_ref[...]
        col = lax.broadcasted_iota(jnp.int32, s.shape, 1)
        sg = jax.nn.sigmoid(s)
        z = s + dt_ref[...]
        dz = jnp.where((col >= 8) & (col < 16), d * na_ref[...] * jax.nn.sigmoid(z), 0.0)
        ds_ref[...] = jnp.where(col < 8, d * sg * (1.0 - sg), dz).astype(bf16)
        dalog = jnp.where((col >= 8) & (col < 16), d * na_ref[...] * jax.nn.softplus(z), 0.0)
        _acc_rows(da_ref, jnp.sum(dalog, axis=0, keepdims=True))
        _acc_rows(ddt_ref, jnp.sum(dz, axis=0, keepdims=True))

    return pl.pallas_call(
        body, out_shape=(SDS((t, LANE), bf16), SDS((1, LANE), f32), SDS((1, LANE), f32)), grid=(t // tt,),
        in_specs=[pl.BlockSpec((tt, LANE), lambda i: (i, 20)), _vec_spec(LANE), _vec_spec(LANE), _row_spec(tt, LANE),
                  _row_spec(tt, LANE)],
        out_specs=(_row_spec(tt, LANE), _vec_spec(LANE), _vec_spec(LANE)),
        name=name, compiler_params=_cparams(("arbitrary",)),
    )(p, neg_a, dt_bias, dbg_a, dbg_b)


N_SCAN = 2 * GDN_HEADS


def _bdot(a, b, ca, cb, precision=None):
    if precision is None:
        a, b = a.astype(bf16), b.astype(bf16)
    return lax.dot_general(a, b, (((ca,), (cb,)), ((0,), (0,))), preferred_element_type=f32, precision=precision)


def _bdot_nn(a, b, precision=None):
    return _bdot(a, b, 2, 1, precision)


def _bdot_nt(a, b):
    return _bdot(a, b, 2, 2)


def _bdot_tn(a, b, precision=None):
    return _bdot(a, b, 1, 1, precision)


def _order_masks():
    shape = (N_SCAN, CHUNK, CHUNK)
    sign = jnp.where(lax.broadcasted_iota(jnp.int32, shape, 0) >= GDN_HEADS, -1, 1)
    ahead = (lax.broadcasted_iota(jnp.int32, shape, 1) - lax.broadcasted_iota(jnp.int32, shape, 2)) * sign
    lower, strict, lower_t = ahead >= 0, ahead > 0, ahead <= 0
    col_shape = (N_SCAN, CHUNK, 1)
    back1 = lax.broadcasted_iota(jnp.int32, col_shape, 0) >= GDN_HEADS
    row1 = lax.broadcasted_iota(jnp.int32, col_shape, 1)
    at_last = (row1 == jnp.where(back1, 0, CHUNK - 1)).astype(f32)
    return lower, strict, lower_t, at_last


def _stack_heads(f_ref, b_ref):
    return jnp.stack([ref[:, h * LANE:(h + 1) * LANE] for ref in (f_ref, b_ref) for h in range(GDN_HEADS)])


def _stack_gates(bgf, bgb, bgtf, bgtb):
    beta = jnp.stack([bg[:, 4 * d + h:4 * d + h + 1] for d, bg in enumerate((bgf, bgb)) for h in range(GDN_HEADS)])
    g_col = jnp.stack([bg[:, 8 + 4 * d + h:9 + 4 * d + h] for d, bg in enumerate((bgf, bgb)) for h in range(GDN_HEADS)])
    g_row = jnp.stack([bgt[8 + 4 * d + h:9 + 4 * d + h, :] for d, bgt in enumerate((bgtf, bgtb)) for h in range(GDN_HEADS)])
    return beta, g_col, g_row


def _chunk_terms(k, v, beta, g_col, g_row, masks):
    lower, strict, lower_t, at_last = masks
    gc = jnp.sum(lower.astype(f32) * g_row, axis=2, keepdims=True)
    gr = jnp.sum(lower_t.astype(f32) * g_col, axis=1, keepdims=True)
    g_last = jnp.sum(at_last * gc, axis=1, keepdims=True)
    e = jnp.exp(gc)
    f = jnp.exp(g_last - gc)
    dm = jnp.exp(jnp.where(lower, gc - gr, -1e30))
    kb = k * beta
    kk = _bdot_nt(kb, k)
    a = jnp.where(strict, kk * dm, 0.0)
    shape = (N_SCAN, CHUNK, CHUNK)
    eye = (lax.broadcasted_iota(jnp.int32, shape, 1) == lax.broadcasted_iota(jnp.int32, shape, 2)).astype(f32)
    pw = -a
    tinv = eye + pw
    for _ in range(5):
        pw = _bdot_nn(pw, pw, HI)
        tinv = tinv + _bdot_nn(tinv, pw, HI)
    u = _bdot_nn(tinv, v * beta)
    w = _bdot_nn(tinv, kb * e)
    return dict(e=e, f=f, gl=jnp.exp(g_last), dm=dm, kb=kb, kk=kk, tinv=tinv, u=u, w=w, kd=k * f)


def _gdn_specs(nc, width, step_chunk):
    return [pl.BlockSpec((CHUNK, width), functools.partial(lambda i, d: (step_chunk(i, d), 0), d=d)) for d in (0, 1)]


def gdn_forward(q, k, v, bg, bgt, s0, with_out, name):
    t = k.shape[0]
    nc = t // CHUNK

    def body(qf_ref, qb_ref, kf_ref, kb_ref, vf_ref, vb_ref, bgf_ref, bgb_ref, bgtf_ref, bgtb_ref, s0_ref,
             of_ref, ob_ref, sallf_ref, sallb_ref, sfin_ref, s_ref):
        i = pl.program_id(0)

        @pl.when(i == 0)
        def _():
            s_ref[...] = s0_ref[...]

        masks = _order_masks()
        k8, v8 = _stack_heads(kf_ref, kb_ref), _stack_heads(vf_ref, vb_ref)
        beta, g_col, g_row = _stack_gates(bgf_ref[...], bgb_ref[...], bgtf_ref[0], bgtb_ref[0])
        c = _chunk_terms(k8, v8, beta, g_col, g_row, masks)
        s = s_ref[...]
        sallf_ref[0] = s[:GDN_HEADS]
        sallb_ref[0] = s[GDN_HEADS:]
        vn = c["u"] - _bdot_nn(c["w"], s)
        if with_out:
            q8 = _stack_heads(qf_ref, qb_ref)
            pm = jnp.where(masks[0], _bdot_nt(q8, k8) * c["dm"], 0.0)
            o = _bdot_nn(q8 * c["e"], s) + _bdot_nn(pm, vn)
        for d, o_ref in enumerate((of_ref, ob_ref)):
            for h in range(GDN_HEADS):
                o_ref[:, h * LANE:(h + 1) * LANE] = o[GDN_HEADS * d + h] if with_out else jnp.zeros((CHUNK, LANE), f32)
        s_ref[...] = c["gl"] * s + _bdot_tn(c["kd"], vn)

        @pl.when(i == nc - 1)
        def _():
            sfin_ref[...] = s_ref[...]

    chunk_of = lambda i, d: i if d == 0 else nc - 1 - i
    seq = _gdn_specs(nc, 512, chunk_of)
    gate = _gdn_specs(nc, LANE, chunk_of)
    gate_t = [pl.BlockSpec((1, 16, CHUNK), functools.partial(lambda i, d: (chunk_of(i, d), 0, 0), d=d)) for d in (0, 1)]
    sall = [pl.BlockSpec((1, GDN_HEADS, LANE, LANE), functools.partial(lambda i, d: (chunk_of(i, d), 0, 0, 0), d=d)) for d in (0, 1)]
    st = pl.BlockSpec((N_SCAN, LANE, LANE), lambda i: (0, 0, 0))
    o_shape, s_shape = SDS((t, 512), f32), SDS((nc, GDN_HEADS, LANE, LANE), f32)
    o_f, o_b, sall_f, sall_b, s_fin = pl.pallas_call(
        body, out_shape=(o_shape, o_shape, s_shape, s_shape, SDS((N_SCAN, LANE, LANE), f32)), grid=(nc,),
        in_specs=seq + seq + seq + gate + gate_t + [st], out_specs=tuple(seq + sall + [st]),
        scratch_shapes=[pltpu.VMEM((N_SCAN, LANE, LANE), f32)], name=name,
        compiler_params=_cparams(("arbitrary",)),
    )(q, q, k, k, v, v, bg, bg, bgt, bgt, s0.reshape(N_SCAN, LANE, LANE))
    return o_f, o_b, sall_f, sall_b, s_fin.reshape(2, GDN_HEADS, LANE, LANE)


def _gdn_chunk_bwd(q, k, v, d_o, beta, g_col, g_row, s, dsn, masks):
    lower, strict, _, at_last = masks
    c = _chunk_terms(k, v, beta, g_col, g_row, masks)
    e, f, gl, dm, kb, kk, tinv, u, w, kd = (c[n] for n in ("e", "f", "gl", "dm", "kb", "kk", "tinv", "u", "w", "kd"))
    vn = u - _bdot_nn(w, s)
    ds = gl * dsn
    dgl = jnp.sum(jnp.sum(s * dsn, axis=2, keepdims=True), axis=1, keepdims=True)
    dkd = _bdot_nt(vn, dsn)
    dvn = _bdot_nn(kd, dsn)
    dm_grad = jnp.zeros((N_SCAN, CHUNK, CHUNK), f32)
    de = jnp.zeros((N_SCAN, CHUNK, 1), f32)
    dq = None
    dk = jnp.zeros((N_SCAN, CHUNK, LANE), f32)
    if q is not None:
        qk = _bdot_nt(q, k)
        pm = jnp.where(lower, qk * dm, 0.0)
        dqd = _bdot_nt(d_o, s)
        ds = ds + _bdot_tn(q * e, d_o)
        dpm = jnp.where(lower, _bdot_nt(d_o, vn), 0.0)
        dvn = dvn + _bdot_tn(pm, d_o)
        dqk = dpm * dm
        dm_grad = dm_grad + dpm * qk
        dq = _bdot_nn(dqk, k) + dqd * e
        dk = _bdot_tn(dqk, q)
        de = de + jnp.sum(dqd * q, axis=2, keepdims=True)
    dw = -_bdot_nt(dvn, s)
    ds = ds - _bdot_tn(w, dvn)
    drv = _bdot_tn(tinv, dvn)
    drk = _bdot_tn(tinv, dw)
    da = -jnp.where(strict, _bdot_nt(drv, u) + _bdot_nt(drk, w), 0.0)
    dbeta = jnp.sum(drv * v, axis=2, keepdims=True)
    dv = drv * beta
    dkb = drk * e
    de = de + jnp.sum(drk * kb, axis=2, keepdims=True)
    dkk = da * dm
    dm_grad = dm_grad + da * kk
    dkb = dkb + _bdot_nn(dkk, k)
    dk = dk + _bdot_tn(dkk, kb) + dkd * f
    df = jnp.sum(dkd * k, axis=2, keepdims=True)
    dbeta = dbeta + jnp.sum(dkb * k, axis=2, keepdims=True)
    dk = dk + dkb * beta
    m = dm_grad * dm
    ones = jnp.ones((N_SCAN, CHUNK, LANE), f32)
    rsum = jnp.sum(m, axis=2, keepdims=True)
    csum = _bdot_tn(m, ones, HI)[:, :, 0:1]
    dgl_tot = jnp.sum(df * f, axis=1, keepdims=True) + dgl * gl
    dgc = de * e - df * f + rsum - csum + at_last * dgl_tot
    dg = _bdot_tn(lower.astype(f32), dgc * ones, HI)[:, :, 0:1]
    return dq, dk, dv, dbeta, dg, ds


def gdn_backward(q, k, v, bg, bgt, sall_f, sall_b, d_o, ds_fin, with_out, name):
    t = k.shape[0]
    nc = t // CHUNK

    def body(qf_ref, qb_ref, kf_ref, kb_ref, vf_ref, vb_ref, bgf_ref, bgb_ref, bgtf_ref, bgtb_ref,
             sallf_ref, sallb_ref, dof_ref, dob_ref, dsf_ref,
             dqf_ref, dqb_ref, dkf_ref, dkb_ref, dvf_ref, dvb_ref, dbgf_ref, dbgb_ref, ds0_ref, ds_ref):
        i = pl.program_id(0)

        @pl.when(i == 0)
        def _():
            ds_ref[...] = dsf_ref[...]

        lane = lax.broadcasted_iota(jnp.int32, (1, LANE), 1)
        masks = _order_masks()
        beta, g_col, g_row = _stack_gates(bgf_ref[...], bgb_ref[...], bgtf_ref[0], bgtb_ref[0])
        s = jnp.concatenate([sallf_ref[0], sallb_ref[0]], 0)
        dq, dk, dv, dbeta, dg, ds = _gdn_chunk_bwd(
            _stack_heads(qf_ref, qb_ref) if with_out else None, _stack_heads(kf_ref, kb_ref), _stack_heads(vf_ref, vb_ref),
            _stack_heads(dof_ref, dob_ref), beta, g_col, g_row, s, ds_ref[...], masks)
        ds_ref[...] = ds
        for d, (dq_ref, dk_ref, dv_ref, dbg_ref) in enumerate(((dqf_ref, dkf_ref, dvf_ref, dbgf_ref), (dqb_ref, dkb_ref, dvb_ref, dbgb_ref))):
            dbg = jnp.zeros((CHUNK, LANE), f32)
            for h in range(GDN_HEADS):
                b = GDN_HEADS * d + h
                cs = slice(h * LANE, (h + 1) * LANE)
                dq_ref[:, cs] = dq[b] if with_out else jnp.zeros((CHUNK, LANE), f32)
                dk_ref[:, cs] = dk[b]
                dv_ref[:, cs] = dv[b]
                dbg = dbg + dbeta[b] * (lane == b).astype(f32) + dg[b] * (lane == 8 + b).astype(f32)
            dbg_ref[...] = dbg

        @pl.when(i == nc - 1)
        def _():
            ds0_ref[...] = ds_ref[...]

    chunk_of = lambda i, d: nc - 1 - i if d == 0 else i
    seq = _gdn_specs(nc, 512, chunk_of)
    gate = _gdn_specs(nc, LANE, chunk_of)
    gate_t = [pl.BlockSpec((1, 16, CHUNK), functools.partial(lambda i, d: (chunk_of(i, d), 0, 0), d=d)) for d in (0, 1)]
    sall = [pl.BlockSpec((1, GDN_HEADS, LANE, LANE), functools.partial(lambda i, d: (chunk_of(i, d), 0, 0, 0), d=d)) for d in (0, 1)]
    st = pl.BlockSpec((N_SCAN, LANE, LANE), lambda i: (0, 0, 0))
    o_shape, g_shape = SDS((t, 512), f32), SDS((t, LANE), f32)
    res = pl.pallas_call(
        body, out_shape=(o_shape,) * 6 + (g_shape, g_shape, SDS((N_SCAN, LANE, LANE), f32)), grid=(nc,),
        in_specs=seq + seq + seq + gate + gate_t + sall + seq + [st], out_specs=tuple(seq + seq + seq + gate + [st]),
        scratch_shapes=[pltpu.VMEM((N_SCAN, LANE, LANE), f32)], name=name,
        compiler_params=_cparams(("arbitrary",)),
    )(q, q, k, k, v, v, bg, bg, bgt, bgt, sall_f, sall_b, d_o, d_o, ds_fin.reshape(N_SCAN, LANE, LANE))
    return tuple(res[:8]) + (res[8].reshape(2, GDN_HEADS, LANE, LANE),)


def _my_position():
    x, y, c = lax.axis_index("x"), lax.axis_index("y"), lax.axis_index("c")
    return x, y, c, 4 * x + 2 * y + c


def exchange(arrays, scatter, name):
    n = len(arrays)
    shapes = [a.shape[1:] if scatter else a.shape for a in arrays]

    def body(*refs):
        ins, outs, token = refs[:n], refs[n:2 * n], refs[2 * n]
        send_sems, recv_sems, local_sems = refs[2 * n + 1:]
        x, y, c, me = _my_position()
        token[...] = jnp.zeros_like(token)
        started = []
        for a in range(n):
            mine = pltpu.make_async_copy(ins[a].at[me] if scatter else ins[a], outs[a].at[me], local_sems.at[a])
            mine.start()
            started.append(mine)
        waits = []
        for r in range(1, N_DEV):
            px = 1 - x if r & 4 else x
            py = 1 - y if r & 2 else y
            pc = 1 - c if r & 1 else c
            pid = 4 * px + 2 * py + pc
            for a in range(n):
                cp = pltpu.make_async_remote_copy(
                    src_ref=ins[a].at[pid] if scatter else ins[a], dst_ref=outs[a].at[me],
                    send_sem=send_sems.at[a, r - 1], recv_sem=recv_sems.at[a, r - 1],
                    device_id=(px, py, pc), device_id_type=pl.DeviceIdType.MESH)
                cp.start()
                arrive = pltpu.make_async_remote_copy(
                    src_ref=ins[a].at[pid] if scatter else ins[a], dst_ref=outs[a].at[pid],
                    send_sem=send_sems.at[a, r - 1], recv_sem=recv_sems.at[a, r - 1],
                    device_id=(px, py, pc), device_id_type=pl.DeviceIdType.MESH)
                waits.append((cp, arrive))
        for cp, arrive in waits:
            cp.wait_send()
            arrive.wait_recv()
        for mine in started:
            mine.wait()

    any_spec = pl.BlockSpec(memory_space=pl.ANY)
    return pl.pallas_call(
        body, out_shape=tuple(SDS((N_DEV,) + tuple(s), a.dtype) for s, a in zip(shapes, arrays)) + (SDS((8, LANE), f32),),
        in_specs=[any_spec] * n, out_specs=tuple([any_spec] * n) + (pl.BlockSpec(memory_space=pltpu.VMEM),),
        scratch_shapes=[pltpu.SemaphoreType.DMA((n, N_DEV - 1)), pltpu.SemaphoreType.DMA((n, N_DEV - 1)),
                        pltpu.SemaphoreType.DMA((n,))],
        name=name,
    )(*arrays)


_HBM_SPEC = pl.BlockSpec(memory_space=pltpu.HBM)
_SEM_SPEC = pl.BlockSpec(memory_space=pltpu.SEMAPHORE)
_DATAFLOW = pltpu.SideEffectType.DATAFLOW_SIDE_EFFECTING


def _peers(x, y, c):
    out = []
    for r in range(1, N_DEV):
        px = 1 - x if r & 4 else x
        py = 1 - y if r & 2 else y
        pc = 1 - c if r & 1 else c
        out.append((r, (px, py, pc), 4 * px + 2 * py + pc))
    return out


def _exchange_copies(ins, lands, send_sems, recv_sems, scatter, arrivals):
    x, y, c, me = _my_position()
    pairs = []
    for r, peer, pid in _peers(x, y, c):
        for a in range(len(ins)):
            k = a * (N_DEV - 1) + r - 1
            kw = dict(send_sem=send_sems.at[k], recv_sem=recv_sems.at[k], device_id=peer, device_id_type=pl.DeviceIdType.MESH)
            src = ins[a].at[pid] if scatter else ins[a]
            send = pltpu.make_async_remote_copy(src_ref=src, dst_ref=lands[a].at[me], **kw)
            arrive = pltpu.make_async_remote_copy(src_ref=src, dst_ref=lands[a].at[pid], **kw) if arrivals else None
            pairs.append((send, arrive))
    return pairs


def exchange_start(arrays, scatter, name):
    n = len(arrays)
    shapes = [a.shape[1:] if scatter else a.shape for a in arrays]

    def body(*refs):
        ins, lands = refs[:n], refs[n:2 * n]
        send_sems, recv_sems = refs[2 * n], refs[2 * n + 1]
        token = refs[-1]
        for send, _ in _exchange_copies(ins, lands, send_sems, recv_sems, scatter, False):
            send.start()
        token[...] = jnp.zeros_like(token)

    sem = pltpu.SemaphoreType.DMA((n * (N_DEV - 1),))
    land_shapes = [(N_DEV,) + tuple(s) for s in shapes]
    res = pl.pallas_call(
        body, name=name,
        out_shape=(sem, sem, *[pltpu.HBM(a.shape, a.dtype) for a in arrays],
                   *[pltpu.HBM(s, a.dtype) for s, a in zip(land_shapes, arrays)], SDS((8, LANE), f32)),
        in_specs=[_HBM_SPEC] * (2 * n),
        out_specs=(_SEM_SPEC, _SEM_SPEC, *[_HBM_SPEC] * (2 * n), pl.BlockSpec(memory_space=pltpu.VMEM)),
        input_output_aliases={i: 2 + i for i in range(2 * n)},
        compiler_params=pltpu.CompilerParams(has_side_effects=_DATAFLOW),
    )(*[pltpu.with_memory_space_constraint(a, pltpu.HBM) for a in arrays],
      *[pltpu.with_memory_space_constraint(lax.empty(s, a.dtype), pltpu.HBM) for s, a in zip(land_shapes, arrays)])
    return (res[0], res[1], list(res[2:2 + n]), list(res[2 + n:2 + 2 * n]), scatter), res[-1]


def exchange_wait(handle, after, name):
    send_sems, recv_sems, ins, lands, scatter = handle
    n = len(ins)

    def body(*refs):
        in_refs, land_refs = refs[:n], refs[n:2 * n]
        for send, arrive in _exchange_copies(in_refs, land_refs, refs[2 * n], refs[2 * n + 1], scatter, True):
            send.wait_send()
            arrive.wait_recv()

    res = pl.pallas_call(
        body, name=name,
        out_shape=tuple(pltpu.HBM(a.shape, a.dtype) for a in ins + lands),
        in_specs=[_HBM_SPEC] * (2 * n) + [_SEM_SPEC, _SEM_SPEC, pl.BlockSpec(memory_space=pl.ANY)],
        out_specs=tuple([_HBM_SPEC] * (2 * n)),
        input_output_aliases={i: i for i in range(2 * n)},
        compiler_params=pltpu.CompilerParams(has_side_effects=_DATAFLOW),
    )(*ins, *lands, send_sems, recv_sems, after)
    return list(res[:n]), list(res[n:])


def place_own(lands, arrays, scatter, me):
    own = [lax.dynamic_index_in_dim(a, me, 0, keepdims=False) if scatter else a for a in arrays]
    return [lax.dynamic_update_index_in_dim(l, o, me, 0) for l, o in zip(lands, own)]


def ada_forward(a_raw, ada_w, ada_b_loc, name):
    def body(a_ref, w_ref, b_ref, o_ref):
        a = _silu(a_ref[...])
        for l in range(DEPTH):
            o_ref[l] = _dotf(a, w_ref[l]) + b_ref[l]

    return pl.pallas_call(body, out_shape=SDS((DEPTH, 16, ada_w.shape[2]), f32), name=name,
                          compiler_params=_cparams())(a_raw, ada_w, ada_b_loc)


def ada_backward(a_raw, ada_w, dm, name):
    def body(a_ref, w_ref, dm_ref, gw_ref, dcc_ref):
        a = _silu(a_ref[...])
        for l in range(DEPTH):
            gw_ref[l] = _dotf(a, dm_ref[l], (((0,), (0,)), ((), ())))
        dcc_ref[...] = _dotf(dm_ref[0, 8:16, :], w_ref[0], (((1,), (1,)), ((), ())))

    return pl.pallas_call(body, out_shape=(SDS(ada_w.shape, f32), SDS((8, ada_w.shape[1]), f32)), name=name,
                          compiler_params=_cparams())(a_raw, ada_w, dm)


def sum_parts(parts, name):
    _, r, c = parts.shape

    def body(p_ref, o_ref):
        acc = p_ref[0]
        for i in range(1, N_DEV):
            acc = acc + p_ref[i]
        o_ref[...] = acc

    return pl.pallas_call(body, out_shape=SDS((r, c), f32), name=name, compiler_params=_cparams())(parts)


def cctx_grad(parts, c_ctx, name):
    def body(p_ref, c_ref, o_ref):
        acc = p_ref[0, 0:1, :]
        for i in range(1, N_DEV):
            acc = acc + p_ref[i, 0:1, :]
        o_ref[...] = acc * _dsilu(c_ref[...])

    return pl.pallas_call(body, out_shape=SDS((1, c_ctx.shape[1]), f32), name=name, compiler_params=_cparams())(parts, c_ctx)


def _adamw_math(g, w, m, v):
    m = ADAM_B1 * m + (1.0 - ADAM_B1) * g
    v = ADAM_B2 * v + (1.0 - ADAM_B2) * (g * g)
    m_hat = m / (1.0 - ADAM_B1 ** ADAM_STEP)
    v_hat = v / (1.0 - ADAM_B2 ** ADAM_STEP)
    delta = -ADAM_LR * (m_hat / (jnp.sqrt(v_hat) + ADAM_EPS) + ADAM_WD * w)
    return delta, m, v


def adamw(parts, w, m, v, name):
    n, r, c = parts.shape
    tr = _pick(r, (256, 128, 64, 32, 16, 8))

    def body(p_ref, w_ref, m_ref, v_ref, g_ref, d_ref, nm_ref, nv_ref):
        g = p_ref[0].astype(f32)
        for i in range(1, n):
            g = g + p_ref[i].astype(f32)
        g_ref[...] = g
        d_ref[...], nm_ref[...], nv_ref[...] = _adamw_math(g, w_ref[...], m_ref[...], v_ref[...])

    blk = pl.BlockSpec((tr, c), lambda i: (i, 0))
    out = SDS((r, c), f32)
    return pl.pallas_call(
        body, out_shape=(out, out, out, out), grid=(r // tr,),
        in_specs=[pl.BlockSpec((n, tr, c), lambda i: (0, i, 0)), blk, blk, blk], out_specs=(blk, blk, blk, blk),
        name=name, compiler_params=_cparams(("parallel",)),
    )(parts, w, m, v)


def adamw_small(items, name):
    n = len(items)

    def body(*refs):
        ins, outs = refs[:4 * n], refs[4 * n:]
        for i in range(n):
            g, w, m, v = (ins[4 * i + j][...] for j in range(4))
            outs[3 * i][...], outs[3 * i + 1][...], outs[3 * i + 2][...] = _adamw_math(g, w, m, v)

    flat = [a for it in items for a in it]
    out_shape = tuple(SDS(it[1].shape, f32) for it in items for _ in range(3))
    res = pl.pallas_call(body, out_shape=out_shape, name=name, compiler_params=_cparams())(*flat)
    return [tuple(res[3 * i:3 * i + 3]) for i in range(n)]


def _unshard(g, axis):
    loc = g.shape[1:]
    return jnp.moveaxis(g, 0, axis).reshape(loc[:axis] + (N_DEV * loc[axis],) + loc[axis + 1:])


def _shard_major(full, axis):
    s = full.shape
    return jnp.moveaxis(full.reshape(s[:axis] + (N_DEV, s[axis] // N_DEV) + s[axis + 1:]), axis, 0)


def _my_block(full, axis, me):
    n = full.shape[axis] // N_DEV
    return lax.dynamic_slice_in_dim(full, me * n, n, axis)


def _pack(arrays):
    flat = [a.reshape(-1) for a in arrays]
    sizes = [f.shape[0] for f in flat]
    total = sum(sizes)
    padded = -(-total // (8 * LANE)) * (8 * LANE)
    flat.append(jnp.zeros((padded - total,), f32))
    offs = [sum(sizes[:i]) for i in range(len(sizes))]
    return jnp.concatenate(flat).reshape(padded // LANE, LANE), offs


def _pad_rows(w, n):
    return jnp.concatenate([w, jnp.zeros((n - w.shape[0],) + w.shape[1:], w.dtype)], 0)


def _gate_rows(bg):
    return bg[:, :16].reshape(bg.shape[0] // CHUNK, CHUNK, 16).transpose(0, 2, 1)


def _rows(vec, n):
    m = vec.reshape(n, 1, -1)
    return [m[i] for i in range(n)]


def kernel(x, c, ctx, c_ctx, ada_w, ada_b, ln_g, ln_b, even_w_in, even_w_out, gdn_conv_w, gdn_a_log, gdn_dt_bias, gdn_norm_w, pool_w, pool_scale, odd_w_in, odd_w_out, sconv_w, conf_conv_w, conf_ln_g, conf_ln_b, ffn_w_up, ffn_conv_w, ffn_w_down, loss_target, m_c_ctx, m_ada_w, m_ada_b, m_ln_g, m_ln_b, m_even_w_in, m_even_w_out, m_gdn_conv_w, m_gdn_a_log, m_gdn_dt_bias, m_gdn_norm_w, m_pool_w, m_pool_scale, m_odd_w_in, m_odd_w_out, m_sconv_w, m_conf_conv_w, m_conf_ln_g, m_conf_ln_b, m_ffn_w_up, m_ffn_conv_w, m_ffn_w_down, v_c_ctx, v_ada_w, v_ada_b, v_ln_g, v_ln_b, v_even_w_in, v_even_w_out, v_gdn_conv_w, v_gdn_a_log, v_gdn_dt_bias, v_gdn_norm_w, v_pool_w, v_pool_scale, v_odd_w_in, v_odd_w_out, v_sconv_w, v_conf_conv_w, v_conf_ln_g, v_conf_ln_b, v_ffn_w_up, v_ffn_conv_w, v_ffn_w_down):
    weights = dict(c_ctx=c_ctx, ada_w=ada_w, ada_b=ada_b, ln_g=ln_g, ln_b=ln_b, even_w_in=even_w_in, even_w_out=even_w_out, gdn_conv_w=gdn_conv_w, gdn_a_log=gdn_a_log, gdn_dt_bias=gdn_dt_bias, gdn_norm_w=gdn_norm_w, pool_w=pool_w, pool_scale=pool_scale, odd_w_in=odd_w_in, odd_w_out=odd_w_out, sconv_w=sconv_w, conf_conv_w=conf_conv_w, conf_ln_g=conf_ln_g, conf_ln_b=conf_ln_b, ffn_w_up=ffn_w_up, ffn_conv_w=ffn_conv_w, ffn_w_down=ffn_w_down)
    mom1 = dict(c_ctx=m_c_ctx, ada_w=m_ada_w, ada_b=m_ada_b, ln_g=m_ln_g, ln_b=m_ln_b, even_w_in=m_even_w_in, even_w_out=m_even_w_out, gdn_conv_w=m_gdn_conv_w, gdn_a_log=m_gdn_a_log, gdn_dt_bias=m_gdn_dt_bias, gdn_norm_w=m_gdn_norm_w, pool_w=m_pool_w, pool_scale=m_pool_scale, odd_w_in=m_odd_w_in, odd_w_out=m_odd_w_out, sconv_w=m_sconv_w, conf_conv_w=m_conf_conv_w, conf_ln_g=m_conf_ln_g, conf_ln_b=m_conf_ln_b, ffn_w_up=m_ffn_w_up, ffn_conv_w=m_ffn_conv_w, ffn_w_down=m_ffn_w_down)
    mom2 = dict(c_ctx=v_c_ctx, ada_w=v_ada_w, ada_b=v_ada_b, ln_g=v_ln_g, ln_b=v_ln_b, even_w_in=v_even_w_in, even_w_out=v_even_w_out, gdn_conv_w=v_gdn_conv_w, gdn_a_log=v_gdn_a_log, gdn_dt_bias=v_gdn_dt_bias, gdn_norm_w=v_gdn_norm_w, pool_w=v_pool_w, pool_scale=v_pool_scale, odd_w_in=v_odd_w_in, odd_w_out=v_odd_w_out, sconv_w=v_sconv_w, conf_conv_w=v_conf_conv_w, conf_ln_g=v_conf_ln_g, conf_ln_b=v_conf_ln_b, ffn_w_up=v_ffn_w_up, ffn_conv_w=v_ffn_conv_w, ffn_w_down=v_ffn_w_down)
    order = list(weights)
    me = 4 * lax.axis_index("x") + 2 * lax.axis_index("y") + lax.axis_index("c")
    x, ctx, target = x[0], ctx[0], loss_target[0]
    t, d = x.shape
    tc = ctx.shape[0]

    small_in = [ln_g, ln_b, gdn_conv_w, sconv_w, conf_conv_w, ffn_conv_w, c]
    small_axes = [2, 2, 1, 1, 1, 3, 0]
    small_pack, small_offs = _pack(small_in)
    gath = exchange([even_w_in.astype(bf16), small_pack], False, "gather_first")
    zero = gath[-1][0, 0].astype(bf16)
    wire_l0 = [even_w_out.astype(bf16) + zero, ffn_w_up[0].astype(bf16), ffn_w_down[0].astype(bf16)]
    gather_l0, token_a = exchange_start(wire_l0, False, "gather_l0_start")
    wire_l1 = [odd_w_in.astype(bf16) + token_a[0, 0].astype(bf16), odd_w_out.astype(bf16), ffn_w_up[1].astype(bf16),
               ffn_w_down[1].astype(bf16)]
    gather_l1, token_b = exchange_start(wire_l1, False, "gather_l1_start")
    e_in = even_w_in.shape[1] * N_DEV
    e_pad = -(-e_in // LANE) * LANE
    win_e = jnp.pad(_unshard(gath[0], 1), ((0, 0), (0, e_pad - e_in)))
    sm = gath[1].reshape(N_DEV, -1)
    lng_f, lnb_f, gconv_f, sconv_f, cconv_f, fconv_f, c_all = [
        _unshard(sm[:, o:o + a.size].reshape((N_DEV,) + a.shape), ax) for a, o, ax in zip(small_in, small_offs, small_axes)]
    gw8 = _pad_rows(gconv_f, 8)
    sw8 = _pad_rows(sconv_f, 8)
    cw32 = _pad_rows(cconv_f, 32)
    fw16 = [_pad_rows(fconv_f[l].reshape(9, D_FF), 16) for l in range(DEPTH)]

    a_raw = jnp.concatenate([c_all, c_ctx[None], jnp.zeros((7, d), f32)], 0) + token_b[0:1, 0:1]
    ncol = ada_w.shape[2]
    ada_b_loc = lax.dynamic_slice_in_dim(ada_b, me * ncol, ncol, 1)[:, None, :]
    modpart = ada_forward(a_raw, ada_w, ada_b_loc, "ada_forward")
    mod_send = jnp.stack([jnp.transpose(modpart[:, :N_DEV], (1, 0, 2)),
                          jnp.broadcast_to(modpart[:, N_DEV][None], (N_DEV, DEPTH, ncol))], axis=2)
    mod_recv = exchange([mod_send], True, "scatter_mod")[0]
    mod = jnp.transpose(mod_recv[:, :, 0, :], (1, 0, 2)).reshape(DEPTH, 6 * d)
    modc = mod_recv[:, 0, 1, :].reshape(6 * d)
    sh_c, sc_c = modc[None, :d], modc[None, d:2 * d]
    mods = [_rows(mod[l], 6) for l in range(DEPTH)]
    lng = [[lng_f[l, j][None] for j in range(2)] for l in range(DEPTH)]
    lnb = [[lnb_f[l, j][None] for j in range(2)] for l in range(DEPTH)]

    neg_a = jnp.zeros((1, LANE), f32).at[0, 8:16].set(-jnp.exp(gdn_a_log).reshape(8))
    dt_row = jnp.zeros((1, LANE), f32).at[0, 8:16].set(gdn_dt_bias.reshape(8))
    nw_row, ps_row = gdn_norm_w[None], pool_scale[None]
    cg_row, cb_row = conf_ln_g[None], conf_ln_b[None]
    q_scale = GDN_DK ** -0.5

    sh_m, sc_m, gt_m, sh_f, sc_f, gt_f = mods[0]
    u0 = modulate(x, sc_m, sh_m, "mod_l0_mix")
    cu = modulate(ctx, sc_c, sh_c, "mod_ctx")
    p0 = matmul(u0, win_e, "nn", f32, "even_in")
    pc = matmul(cu, win_e, "nn", f32, "even_in_ctx")
    qn = gdn_conv(p0, gw8, 0, 4, q_scale, "gdn_conv_q")
    kn = gdn_conv(p0, gw8, 4, 4, 1.0, "gdn_conv_k")
    vv = gdn_conv(p0, gw8, 8, 4, None, "gdn_conv_v")
    kc = gdn_conv(pc, gw8, 4, 4, 1.0, "gdn_conv_k_ctx")
    vc = gdn_conv(pc, gw8, 8, 4, None, "gdn_conv_v_ctx")
    bg = gdn_gates(p0, neg_a, dt_row, "gdn_gates")
    bgc = gdn_gates(pc, neg_a, dt_row, "gdn_gates_ctx")
    bgt, bgtc = _gate_rows(bg), _gate_rows(bgc)
    zero_state = jnp.zeros((2, GDN_HEADS, LANE, LANE), f32)
    _, _, sallc_f, sallc_b, sfin_c = gdn_forward(kc, kc, vc, bgc, bgtc, zero_state, False, "gdn_fwd_ctx")
    o_f, o_b, sall_f, sall_b, _ = gdn_forward(qn, kn, vv, bg, bgt, sfin_c, True, "gdn_fwd")
    mix0 = jnp.concatenate([gated_rmsnorm(o_f, o_b, p0, nw_row, "gated_rmsnorm"),
                            pool_mix(p0, pool_w, ps_row, "pool_mix")], 1)
    sent, landed = exchange_wait(gather_l0, mix0, "gather_l0_wait")
    full = place_own(landed, sent, False, me)
    wout_e, wup, wdown = _unshard(full[0], 0), [_unshard(full[1], 1)], [_unshard(full[2], 0)]
    y0 = matmul(mix0, wout_e, "nn", f32, "even_out")
    x1 = res_layernorm(x, y0, gt_m, lng[0][0], lnb[0][0], "resln_l0_mix")
    u1 = modulate(x1, sc_f, sh_f, "mod_l0_ffn")
    h0 = matmul(u1, wup[0], "nn", f32, "ffn_up_l0")
    f0 = ffn_conv(h0, fw16[0], "ffn_conv_l0")
    y0f = matmul(f0, wdown[0], "nn", f32, "ffn_down_l0")
    x2 = res_layernorm(x1, y0f, gt_f, lng[0][1], lnb[0][1], "resln_l0_ffn")

    sh_m1, sc_m1, gt_m1, sh_f1, sc_f1, gt_f1 = mods[1]
    sent, landed = exchange_wait(gather_l1, x2, "gather_l1_wait")
    full = place_own(landed, sent, False, me)
    win_o, wout_o = _unshard(full[0], 1), _unshard(full[1], 0)
    wup.append(_unshard(full[2], 1))
    wdown.append(_unshard(full[3], 0))
    u2 = modulate(x2, sc_m1, sh_m1, "mod_l1_mix")
    p1 = matmul(u2, win_o, "nn", f32, "odd_in")
    zc = conf_conv(p1, cw32, "conf_conv")
    mix1 = jnp.concatenate([short_conv(p1, sw8, "short_conv"), ln_silu(zc, cg_row, cb_row, "conf_ln_silu")], 1)
    y1 = matmul(mix1, wout_o, "nn", f32, "odd_out")
    x3 = res_layernorm(x2, y1, gt_m1, lng[1][0], lnb[1][0], "resln_l1_mix")
    u3 = modulate(x3, sc_f1, sh_f1, "mod_l1_ffn")
    h1 = matmul(u3, wup[1], "nn", f32, "ffn_up_l1")
    f1 = ffn_conv(h1, fw16[1], "ffn_conv_l1")
    y1f = matmul(f1, wdown[1], "nn", f32, "ffn_down_l1")
    x4 = res_layernorm(x3, y1f, gt_f1, lng[1][1], lnb[1][1], "resln_l1_ffn")

    loss_row, dx4 = loss_head(x4, target, "loss_head")
    loss = lax.psum(loss_row[0, 0], ("x", "y", "c"))

    def ffn_backward(dout, x_in, y, gate, g_row, scale, u, h, f, l):
        dxr, dy, dgt, dlg, dlb = res_layernorm_bwd(dout, x_in, y, gate, g_row, f"resln_bwd_l{l}_ffn")
        df = matmul(dy, wdown[l], "nt", f32, f"ffn_down_dgrad_l{l}")
        g_down = matmul(f, dy, "tn", bf16, f"ffn_down_wgrad_l{l}")
        dh, dcw = ffn_conv_bwd(h, fw16[l], df, f"ffn_conv_bwd_l{l}")
        du = matmul(dh, wup[l], "nt", f32, f"ffn_up_dgrad_l{l}")
        g_up = matmul(u, dh, "tn", bf16, f"ffn_up_wgrad_l{l}")
        dx_in, dsc, dsh = modulate_bwd(du, x_in, scale, dxr, f"mod_bwd_l{l}_ffn")
        return dx_in, (dsh, dsc, dgt), (dlg, dlb), dcw, g_up, g_down

    dx3, dmod_f1, dln_f1, dfcw1, g_up1, g_down1 = ffn_backward(dx4, x3, y1f, gt_f1, lng[1][1], sc_f1, u3, h1, f1, 1)

    scatter_a, token = exchange_start([_shard_major(g_up1, 1), _shard_major(g_down1, 0)], True, "scatter_l1_ffn_start")
    gt_m1 = gt_m1 + token[0:1, 0:1]

    dxr, dy, dgt, dlg, dlb = res_layernorm_bwd(dx3, x2, y1, gt_m1, lng[1][0], "resln_bwd_l1_mix")
    dln_m1 = (dlg, dlb)
    dmix = matmul(dy, wout_o, "nt", f32, "odd_out_dgrad")
    g_wout_o = matmul(mix1, dy, "tn", bf16, "odd_out_wgrad")
    dgb, dgc, dhh, d_sconv = short_conv_bwd(p1, sw8, dmix, "short_conv_bwd")
    dzc, d_cg, d_cb = ln_silu_bwd(zc, cg_row, cb_row, dmix, "conf_ln_silu_bwd")
    dga, dgbb, d_cconv = conf_conv_bwd(p1, cw32, dzc, "conf_conv_bwd")
    dp1 = jnp.concatenate([dgb, dgc, dhh, dga, dgbb], 1)
    du = matmul(dp1, win_o, "nt", f32, "odd_in_dgrad")
    g_win_o = matmul(u2, dp1, "tn", bf16, "odd_in_wgrad")
    dx2, dsc, dsh = modulate_bwd(du, x2, sc_m1, dxr, "mod_bwd_l1_mix")
    dmod_m1 = (dsh, dsc, dgt)

    dx1, dmod_f0, dln_f0, dfcw0, g_up0, g_down0 = ffn_backward(dx2, x1, y0f, gt_f, lng[0][1], sc_f, u1, h0, f0, 0)

    scatter_b, token = exchange_start(
        [_shard_major(g_win_o, 1), _shard_major(g_wout_o, 0), _shard_major(g_up0, 1), _shard_major(g_down0, 0)],
        True, "scatter_mid_start")
    gt_m = gt_m + token[0:1, 0:1]

    dxr, dy, dgt, dlg, dlb = res_layernorm_bwd(dx1, x, y0, gt_m, lng[0][0], "resln_bwd_l0_mix")
    dln_m0 = (dlg, dlb)
    dmix = matmul(dy, wout_e, "nt", f32, "even_out_dgrad")
    g_wout_e = matmul(mix0, dy, "tn", bf16, "even_out_wgrad")
    d_o, dgate, d_nw = gated_rmsnorm_bwd(o_f, o_b, p0, nw_row, dmix, "gated_rmsnorm_bwd")
    dpool, d_pw, d_ps = pool_mix_bwd(p0, pool_w, ps_row, dmix, "pool_mix_bwd")
    dq_f, dq_b, dk_f, dk_b, dv_f, dv_b, dbg_f, dbg_b, ds0 = gdn_backward(
        qn, kn, vv, bg, bgt, sall_f, sall_b, d_o, zero_state, True, "gdn_bwd")
    _, _, dkc_f, dkc_b, dvc_f, dvc_b, dbgc_f, dbgc_b, _ = gdn_backward(
        kc, kc, vc, bgc, bgtc, sallc_f, sallc_b, jnp.zeros((tc, 512), f32), ds0, False, "gdn_bwd_ctx")
    dqp, dwq = gdn_conv_bwd(p0, gw8, dq_f, dq_b, 0, 4, q_scale, "gdn_conv_q_bwd")
    dkp, dwk = gdn_conv_bwd(p0, gw8, dk_f, dk_b, 4, 4, 1.0, "gdn_conv_k_bwd")
    dvp, dwv = gdn_conv_bwd(p0, gw8, dv_f, dv_b, 8, 4, None, "gdn_conv_v_bwd")
    dkcp, dwkc = gdn_conv_bwd(pc, gw8, dkc_f, dkc_b, 4, 4, 1.0, "gdn_conv_k_ctx_bwd")
    dvcp, dwvc = gdn_conv_bwd(pc, gw8, dvc_f, dvc_b, 8, 4, None, "gdn_conv_v_ctx_bwd")
    ds_l, da_l, ddt_l = gdn_gates_bwd(p0, neg_a, dt_row, dbg_f, dbg_b, "gdn_gates_bwd")
    ds_c, da_c, ddt_c = gdn_gates_bwd(pc, neg_a, dt_row, dbgc_f, dbgc_b, "gdn_gates_ctx_bwd")
    zc512 = jnp.zeros((tc, 512), bf16)
    dp_all = jnp.concatenate([
        jnp.concatenate([dqp, dkp, dvp, dgate, dpool, ds_l], 1),
        jnp.concatenate([zc512, dkcp, dvcp, zc512, zc512, ds_c], 1)], 0)
    u_all = jnp.concatenate([u0, cu], 0)
    du_all = matmul(dp_all, win_e, "nt", f32, "even_in_dgrad")
    g_win_e = matmul(u_all, dp_all, "tn", bf16, "even_in_wgrad")[:, :e_in]
    grad_x, dsc, dsh = modulate_bwd(du_all, x, sc_m, dxr, "mod_bwd_l0_mix")
    dmod_m0 = (dsh, dsc, dgt)
    _, dsc_c, dsh_c = modulate_bwd(du_all, ctx, sc_c, jnp.zeros((tc, d), f32), "mod_bwd_ctx", du_row0=t)

    dmod0 = jnp.concatenate(dmod_m0 + dmod_f0, 1)
    dmod1 = jnp.concatenate(dmod_m1 + dmod_f1, 1)
    dmodc = jnp.concatenate([dsh_c, dsc_c], 1)
    d_gconv = jnp.concatenate([dwq, dwk + dwkc, dwv + dwvc], 1)[:5]
    small_g = [dmod0, dmod1, dmodc,
               jnp.concatenate([dln_m0[0], dln_f0[0], dln_m1[0], dln_f1[0]], 0),
               jnp.concatenate([dln_m0[1], dln_f0[1], dln_m1[1], dln_f1[1]], 0),
               d_gconv, (da_l + da_c)[0, 8:16], (ddt_l + ddt_c)[0, 8:16], d_nw, d_pw, d_ps,
               d_sconv[:3], d_cconv[:31], d_cg, d_cb, jnp.stack([dfcw0[:9], dfcw1[:9]])]
    gpack, goffs = _pack(small_g)
    gparts = exchange([gpack], False, "gather_small_grads")[0]
    gsum = sum_parts(gparts, "sum_small_grads").reshape(-1)
    gs = [gsum[o:o + a.size].reshape(a.shape) for a, o in zip(small_g, goffs)]
    gflat = gparts.reshape(N_DEV, -1)
    dmodc_cols = _my_block(jnp.pad(gs[2], ((0, 0), (0, 4 * d))), 1, me)
    dm = jnp.stack([
        jnp.concatenate([_my_block(gflat[:, goffs[0]:goffs[0] + 6 * d], 1, me), dmodc_cols, jnp.zeros((7, ncol), f32)], 0),
        jnp.concatenate([_my_block(gflat[:, goffs[1]:goffs[1] + 6 * d], 1, me), jnp.zeros((8, ncol), f32)], 0)])
    g_ada_w, dcc = ada_backward(a_raw, ada_w, dm, "ada_backward")
    g_cctx = cctx_grad(exchange([dcc], False, "gather_cctx")[0], c_ctx[None], "cctx_grad")

    grads = {}
    grads["c_ctx"] = g_cctx.reshape(c_ctx.shape)
    grads["ada_b"] = jnp.concatenate([gs[0] + jnp.pad(gs[2], ((0, 0), (0, 4 * d))), gs[1]], 0)
    grads["ln_g"] = _my_block(gs[3].reshape(DEPTH, 2, d), 2, me)
    grads["ln_b"] = _my_block(gs[4].reshape(DEPTH, 2, d), 2, me)
    grads["gdn_conv_w"] = _my_block(gs[5], 1, me)
    grads["gdn_a_log"] = gs[6].reshape(2, GDN_HEADS)
    grads["gdn_dt_bias"] = gs[7].reshape(2, GDN_HEADS)
    grads["gdn_norm_w"] = gs[8].reshape(LANE)
    grads["pool_w"] = gs[9]
    grads["pool_scale"] = gs[10].reshape(-1)
    grads["sconv_w"] = _my_block(gs[11], 1, me)
    grads["conf_conv_w"] = _my_block(gs[12], 1, me)
    grads["conf_ln_g"] = gs[13].reshape(-1)
    grads["conf_ln_b"] = gs[14].reshape(-1)
    grads["ffn_conv_w"] = _my_block(gs[15].reshape(DEPTH, 3, 3, D_FF), 3, me)

    def as2d(a):
        return a.reshape(-1, a.shape[-1]) if a.ndim > 1 else a.reshape(1, -1)

    small_names = [n for n in order if n in grads]
    res = adamw_small([(as2d(grads[n]), as2d(weights[n]), as2d(mom1[n]), as2d(mom2[n])) for n in small_names], "adamw_small")
    delta, new_m, new_v = {}, {}, {}
    for n, (dl, nm, nv) in zip(small_names, res):
        delta[n], new_m[n], new_v[n] = (a.reshape(weights[n].shape) for a in (dl, nm, nv))

    recv_c = exchange([_shard_major(g_win_e, 1), _shard_major(g_wout_e, 0)], True, "scatter_last")
    sent, landed = exchange_wait(scatter_a, recv_c[0], "scatter_l1_ffn_wait")
    recv_a = place_own(landed, sent, True, me)
    sent, landed = exchange_wait(scatter_b, recv_c[0], "scatter_mid_wait")
    recv_b = place_own(landed, sent, True, me)

    def update(n, parts, w, m, v):
        cols = w.shape[-1]
        out = adamw(parts.reshape(parts.shape[0], -1, cols), w.reshape(-1, cols), m.reshape(-1, cols), v.reshape(-1, cols), f"adamw_{n}")
        return [a.reshape(w.shape) for a in out]

    for n, parts in (("even_w_in", recv_c[0]), ("even_w_out", recv_c[1]), ("odd_w_in", recv_b[0]), ("odd_w_out", recv_b[1]),
                     ("ada_w", g_ada_w[None])):
        grads[n], delta[n], new_m[n], new_v[n] = update(n, parts, weights[n], mom1[n], mom2[n])
    for n, per_layer in (("ffn_w_up", (recv_b[2], recv_a[0])), ("ffn_w_down", (recv_b[3], recv_a[1]))):
        outs = [update(f"{n}_l{l}", per_layer[l], weights[n][l], mom1[n][l], mom2[n][l]) for l in range(DEPTH)]
        grads[n], delta[n], new_m[n], new_v[n] = (jnp.stack([outs[l][j] for l in range(DEPTH)]) for j in range(4))

    return (loss, grad_x[None], *[grads[n] for n in order], *[delta[n] for n in order],
            *[new_m[n] for n in order], *[new_v[n] for n in order])
```

```python
import functools
import math

import jax
import jax.numpy as jnp
from jax import lax
from jax.experimental import pallas as pl
from jax.experimental.pallas import tpu as pltpu

f32 = jnp.float32
bf16 = jnp.bfloat16
SDS = jax.ShapeDtypeStruct

N_DEV = 8
D_MODEL = 1024
DEPTH = 2
GRID_W = 64
GDN_HEADS = 4
GDN_DK = 128
CHUNK = 64
POOL_WINDOWS = (2, 4, 8, 16)
D_FF = 2816
ALPHA = (2 * DEPTH) ** 0.25
LN_EPS = 1e-5
RMS_EPS = 1e-6
LANE = 128
PAD_ROWS = 72
CONV_ROWS = 256
VMEM_LIMIT = 56 * 2**20

ADAM_LR, ADAM_B1, ADAM_B2, ADAM_EPS, ADAM_WD, ADAM_STEP = 0.001, 0.9, 0.999, 1e-08, 0.01, 10

HI = lax.Precision.HIGHEST


def _cparams(sem=None):
    return pltpu.CompilerParams(dimension_semantics=sem, vmem_limit_bytes=VMEM_LIMIT)


def _silu(x):
    return x * jax.nn.sigmoid(x)


def _dsilu(x):
    s = jax.nn.sigmoid(x)
    return s * (1.0 + x * (1.0 - s))


def _dotb(a, b, dims=(((1,), (0,)), ((), ()))):
    return lax.dot_general(a.astype(bf16), b.astype(bf16), dims, preferred_element_type=f32)


def _dotb_nt(a, b):
    return _dotb(a, b, (((1,), (1,)), ((), ())))


def _dotb_tn(a, b):
    return _dotb(a, b, (((0,), (0,)), ((), ())))


def _dotf(a, b, dims=(((1,), (0,)), ((), ()))):
    return lax.dot_general(a, b, dims, preferred_element_type=f32, precision=HI)


def _pick(n, cands):
    for c in cands:
        if n % c == 0:
            return c
    return n


def matmul(a, b, mode, out_dtype, name):
    if mode == "nn":
        (M, K), N = a.shape, b.shape[1]
    elif mode == "nt":
        (M, K), N = a.shape, b.shape[0]
    else:
        (K, M), N = a.shape, b.shape[1]
    tm = _pick(M, (1024, 768, 512, 256, 128)) if mode != "tn" else _pick(M, (1024, 1408, 512, 256, 128))
    tn = _pick(N, (1024, 1408, 896, 768, 640, 512, 384, 256, 128))
    tk = _pick(K, (1024, 1408, 896, 768, 640, 512, 384, 256, 128)) if mode != "tn" else _pick(K, (1024, 512, 256))
    nk = K // tk
    dims = {"nn": (((1,), (0,)), ((), ())), "nt": (((1,), (1,)), ((), ())), "tn": (((0,), (0,)), ((), ()))}[mode]

    def body(a_ref, b_ref, o_ref, acc_ref):
        k = pl.program_id(2)
        part = lax.dot_general(a_ref[...].astype(bf16), b_ref[...].astype(bf16), dims, preferred_element_type=f32)

        @pl.when(k == 0)
        def _():
            acc_ref[...] = part

        @pl.when(k > 0)
        def _():
            acc_ref[...] += part

        @pl.when(k == nk - 1)
        def _():
            o_ref[...] = acc_ref[...].astype(out_dtype)

    a_spec = {"nn": pl.BlockSpec((tm, tk), lambda i, j, k: (i, k)),
              "nt": pl.BlockSpec((tm, tk), lambda i, j, k: (i, k)),
              "tn": pl.BlockSpec((tk, tm), lambda i, j, k: (k, i))}[mode]
    b_spec = {"nn": pl.BlockSpec((tk, tn), lambda i, j, k: (k, j)),
              "nt": pl.BlockSpec((tn, tk), lambda i, j, k: (j, k)),
              "tn": pl.BlockSpec((tk, tn), lambda i, j, k: (k, j))}[mode]
    return pl.pallas_call(
        body, out_shape=SDS((M, N), out_dtype), grid=(M // tm, N // tn, nk),
        in_specs=[a_spec, b_spec], out_specs=pl.BlockSpec((tm, tn), lambda i, j, k: (i, j)),
        scratch_shapes=[pltpu.VMEM((tm, tn), f32)], name=name,
        compiler_params=_cparams(("parallel", "parallel", "arbitrary")),
    )(a, b)


def _row_tile(t):
    return _pick(t, (512, 256, 128, 64, 32, 16, 8))


def _row_spec(tt, d):
    return pl.BlockSpec((tt, d), lambda i: (i, 0))


def _vec_spec(d):
    return pl.BlockSpec((1, d), lambda i: (0, 0))


def _acc_rows(ref, val):
    @pl.when(pl.program_id(0) == 0)
    def _():
        ref[...] = val

    @pl.when(pl.program_id(0) > 0)
    def _():
        ref[...] += val


def modulate(x, scale, shift, name):
    t, d = x.shape
    tt = _row_tile(t)

    def body(x_ref, sc_ref, sh_ref, o_ref):
        o_ref[...] = (x_ref[...] * (1.0 + sc_ref[...]) + sh_ref[...]).astype(bf16)

    return pl.pallas_call(
        body, out_shape=SDS((t, d), bf16), grid=(t // tt,),
        in_specs=[_row_spec(tt, d), _vec_spec(d), _vec_spec(d)], out_specs=_row_spec(tt, d),
        name=name, compiler_params=_cparams(("parallel",)),
    )(x, scale, shift)


def modulate_bwd(du, x, scale, dres, name, du_row0=0):
    t, d = x.shape
    tt = _row_tile(t)
    blk0 = du_row0 // tt

    def body(du_ref, x_ref, sc_ref, dres_ref, dx_ref, dsc_ref, dsh_ref):
        du_v = du_ref[...]
        dx_ref[...] = du_v * (1.0 + sc_ref[...]) + dres_ref[...]
        _acc_rows(dsc_ref, jnp.sum(du_v * x_ref[...], axis=0, keepdims=True))
        _acc_rows(dsh_ref, jnp.sum(du_v, axis=0, keepdims=True))

    return pl.pallas_call(
        body, out_shape=(SDS((t, d), f32), SDS((1, d), f32), SDS((1, d), f32)), grid=(t // tt,),
        in_specs=[pl.BlockSpec((tt, d), lambda i: (i + blk0, 0)), _row_spec(tt, d), _vec_spec(d), _row_spec(tt, d)],
        out_specs=(_row_spec(tt, d), _vec_spec(d), _vec_spec(d)),
        name=name, compiler_params=_cparams(("arbitrary",)),
    )(du, x, scale, dres)


def _ln_stats(z):
    mu = jnp.mean(z, axis=-1, keepdims=True)
    zc = z - mu
    var = jnp.mean(zc * zc, axis=-1, keepdims=True)
    rstd = lax.rsqrt(var + LN_EPS)
    return zc * rstd, rstd


def _ln_bwd(dxhat, xhat, rstd):
    m1 = jnp.mean(dxhat, axis=-1, keepdims=True)
    m2 = jnp.mean(dxhat * xhat, axis=-1, keepdims=True)
    return rstd * (dxhat - m1 - xhat * m2)


def res_layernorm(x, y, gate, g, b, name):
    t, d = x.shape
    tt = _row_tile(t)

    def body(x_ref, y_ref, gt_ref, g_ref, b_ref, o_ref):
        xhat, _ = _ln_stats(ALPHA * x_ref[...] + gt_ref[...] * y_ref[...])
        o_ref[...] = xhat * g_ref[...] + b_ref[...]

    return pl.pallas_call(
        body, out_shape=SDS((t, d), f32), grid=(t // tt,),
        in_specs=[_row_spec(tt, d), _row_spec(tt, d), _vec_spec(d), _vec_spec(d), _vec_spec(d)],
        out_specs=_row_spec(tt, d), name=name, compiler_params=_cparams(("parallel",)),
    )(x, y, gate, g, b)


def res_layernorm_bwd(dout, x, y, gate, g, name):
    t, d = x.shape
    tt = _row_tile(t)

    def body(do_ref, x_ref, y_ref, gt_ref, g_ref, dxr_ref, dy_ref, dgt_ref, dg_ref, db_ref):
        y_v = y_ref[...]
        do_v = do_ref[...]
        xhat, rstd = _ln_stats(ALPHA * x_ref[...] + gt_ref[...] * y_v)
        dz = _ln_bwd(do_v * g_ref[...], xhat, rstd)
        dxr_ref[...] = ALPHA * dz
        dy_ref[...] = (gt_ref[...] * dz).astype(bf16)
        _acc_rows(dgt_ref, jnp.sum(dz * y_v, axis=0, keepdims=True))
        _acc_rows(dg_ref, jnp.sum(do_v * xhat, axis=0, keepdims=True))
        _acc_rows(db_ref, jnp.sum(do_v, axis=0, keepdims=True))

    vec = SDS((1, d), f32)
    return pl.pallas_call(
        body, out_shape=(SDS((t, d), f32), SDS((t, d), bf16), vec, vec, vec), grid=(t // tt,),
        in_specs=[_row_spec(tt, d), _row_spec(tt, d), _row_spec(tt, d), _vec_spec(d), _vec_spec(d)],
        out_specs=(_row_spec(tt, d), _row_spec(tt, d), _vec_spec(d), _vec_spec(d), _vec_spec(d)),
        name=name, compiler_params=_cparams(("arbitrary",)),
    )(dout, x, y, gate, g)


def loss_head(y, target, name):
    t, d = y.shape
    tt = _row_tile(t)

    def body(y_ref, t_ref, l_ref, dy_ref):
        e = y_ref[...] - t_ref[...]
        dy_ref[...] = e * (1.0 / d)
        part = jnp.sum(jnp.sum(e * e, axis=1, keepdims=True), axis=0, keepdims=True) * (0.5 / d)
        _acc_rows(l_ref, jnp.broadcast_to(part, (1, LANE)))

    return pl.pallas_call(
        body, out_shape=(SDS((1, LANE), f32), SDS((t, d), f32)), grid=(t // tt,),
        in_specs=[_row_spec(tt, d), _row_spec(tt, d)],
        out_specs=(pl.BlockSpec((1, LANE), lambda i: (0, 0)), _row_spec(tt, d)),
        name=name, compiler_params=_cparams(("arbitrary",)),
    )(y, target)


def _fill_pad(pad_ref, val, t):
    zeros = jnp.zeros((PAD_ROWS, LANE), f32)
    pad_ref[0:PAD_ROWS, :] = zeros
    pad_ref[PAD_ROWS + t:2 * PAD_ROWS + t, :] = zeros
    pad_ref[PAD_ROWS:PAD_ROWS + t, :] = val


def _grid_pads_set(pads, r0, val):
    rows = val.shape[0]
    col = (lax.broadcasted_iota(jnp.int32, (rows, 1), 0) + r0) % GRID_W
    sl = slice(PAD_ROWS + r0, PAD_ROWS + r0 + rows)
    pads[0][sl, :] = val * (col <= GRID_W - 2).astype(f32)
    pads[1][sl, :] = val
    pads[2][sl, :] = val * (col >= 1).astype(f32)


def _grid_pads_clear_edges(pads, t):
    zeros = jnp.zeros((PAD_ROWS, LANE), f32)
    for p in pads:
        p[0:PAD_ROWS, :] = zeros
        p[PAD_ROWS + t:2 * PAD_ROWS + t, :] = zeros


def _tap_source(pads, dc):
    return pads if dc is None else pads[dc + 1]


def _taps_apply(pads, w_ref, taps, r0, rows):
    acc = jnp.zeros((rows, LANE), f32)
    for off, dc, wi in taps:
        xs = _tap_source(pads, dc)[PAD_ROWS + r0 + off:PAD_ROWS + r0 + off + rows, :]
        acc = acc + w_ref[wi:wi + 1, :] * xs
    return acc


def _taps_wgrad(pads, dy, taps, r0, rows, nw):
    out = jnp.zeros((nw, LANE), f32)
    rid = lax.broadcasted_iota(jnp.int32, (nw, 1), 0)
    for off, dc, wi in taps:
        xs = _tap_source(pads, dc)[PAD_ROWS + r0 + off:PAD_ROWS + r0 + off + rows, :]
        s = jnp.sum(dy * xs, axis=0, keepdims=True)
        out = out + jnp.where(rid == wi, s, 0.0)
    return out


def _transpose_taps(taps):
    return [(-off, None if dc is None else -dc, wi) for off, dc, wi in taps]


def _taps_1d(width):
    return [(j - width // 2, None, j) for j in range(width)]


def _taps_grid3():
    return [(GRID_W * dr + dc, dc, 3 * (dr + 1) + (dc + 1)) for dr in (-1, 0, 1) for dc in (-1, 0, 1)]


def _row_chunks(t):
    r = min(CONV_ROWS, t)
    return [(i * r, r) for i in range(t // r)]


def _col_spec(t, off):
    return pl.BlockSpec((t, LANE), lambda c: (0, c + off))


def _w_spec(nw, off=0):
    return pl.BlockSpec((nw, LANE), lambda c: (0, c + off))


def gdn_conv(p, w, col0, nblk, norm_scale, name):
    t = p.shape[0]
    nw = w.shape[0]
    taps = _taps_1d(5)

    def body(p_ref, w_ref, o_ref, pad_ref):
        _fill_pad(pad_ref, p_ref[...], t)
        for r0, rows in _row_chunks(t):
            a = _silu(_taps_apply(pad_ref, w_ref, taps, r0, rows))
            if norm_scale is not None:
                a = a * (lax.rsqrt(jnp.sum(a * a, axis=-1, keepdims=True) + RMS_EPS) * norm_scale)
            o_ref[r0:r0 + rows, :] = a

    return pl.pallas_call(
        body, out_shape=SDS((t, nblk * LANE), f32), grid=(nblk,),
        in_specs=[_col_spec(t, col0), _w_spec(nw, col0)], out_specs=_col_spec(t, 0),
        scratch_shapes=[pltpu.VMEM((t + 2 * PAD_ROWS, LANE), f32)], name=name,
        compiler_params=_cparams(("parallel",)),
    )(p, w)


def gdn_conv_bwd(p, w, d_a, d_b, col0, nblk, norm_scale, name):
    t = p.shape[0]
    nw = w.shape[0]
    taps = _taps_1d(5)
    ttaps = _transpose_taps(taps)

    def body(p_ref, w_ref, da_ref, db_ref, dp_ref, dw_ref, pad_ref, gpad_ref):
        _fill_pad(pad_ref, p_ref[...], t)
        for r0, rows in _row_chunks(t):
            pre = _taps_apply(pad_ref, w_ref, taps, r0, rows)
            a = _silu(pre)
            dy = da_ref[r0:r0 + rows, :] + db_ref[r0:r0 + rows, :]
            if norm_scale is not None:
                r = lax.rsqrt(jnp.sum(a * a, axis=-1, keepdims=True) + RMS_EPS)
                da = norm_scale * (dy * r - a * (r * r * r) * jnp.sum(dy * a, axis=-1, keepdims=True))
            else:
                da = dy
            gpad_ref[PAD_ROWS + r0:PAD_ROWS + r0 + rows, :] = da * _dsilu(pre)
        zeros = jnp.zeros((PAD_ROWS, LANE), f32)
        gpad_ref[0:PAD_ROWS, :] = zeros
        gpad_ref[PAD_ROWS + t:2 * PAD_ROWS + t, :] = zeros
        dw = jnp.zeros((nw, LANE), f32)
        for r0, rows in _row_chunks(t):
            dp_ref[r0:r0 + rows, :] = _taps_apply(gpad_ref, w_ref, ttaps, r0, rows).astype(bf16)
            dw = dw + _taps_wgrad(pad_ref, gpad_ref[PAD_ROWS + r0:PAD_ROWS + r0 + rows, :], taps, r0, rows, nw)
        dw_ref[...] = dw

    return pl.pallas_call(
        body, out_shape=(SDS((t, nblk * LANE), bf16), SDS((nw, nblk * LANE), f32)), grid=(nblk,),
        in_specs=[_col_spec(t, col0), _w_spec(nw, col0), _col_spec(t, 0), _col_spec(t, 0)],
        out_specs=(_col_spec(t, 0), _w_spec(nw)),
        scratch_shapes=[pltpu.VMEM((t + 2 * PAD_ROWS, LANE), f32)] * 2, name=name,
        compiler_params=_cparams(("parallel",)),
    )(p, w, d_a, d_b)


def short_conv(p, w, name):
    t = p.shape[0]
    nw = w.shape[0]
    taps = _taps_1d(3)

    def body(gb_ref, gc_ref, h_ref, w_ref, o_ref, pad_ref):
        _fill_pad(pad_ref, gc_ref[...] * h_ref[...], t)
        for r0, rows in _row_chunks(t):
            o_ref[r0:r0 + rows, :] = (gb_ref[r0:r0 + rows, :] * _taps_apply(pad_ref, w_ref, taps, r0, rows)).astype(bf16)

    return pl.pallas_call(
        body, out_shape=SDS((t, 4 * LANE), bf16), grid=(4,),
        in_specs=[_col_spec(t, 0), _col_spec(t, 4), _col_spec(t, 8), _w_spec(nw)], out_specs=_col_spec(t, 0),
        scratch_shapes=[pltpu.VMEM((t + 2 * PAD_ROWS, LANE), f32)], name=name,
        compiler_params=_cparams(("parallel",)),
    )(p, p, p, w)


def short_conv_bwd(p, w, dy, name):
    t = p.shape[0]
    nw = w.shape[0]
    taps = _taps_1d(3)
    ttaps = _transpose_taps(taps)

    def body(gb_ref, gc_ref, h_ref, w_ref, dy_ref, dgb_ref, dgc_ref, dh_ref, dw_ref, pad_ref, gpad_ref):
        _fill_pad(pad_ref, gc_ref[...] * h_ref[...], t)
        _fill_pad(gpad_ref, dy_ref[...] * gb_ref[...], t)
        dw = jnp.zeros((nw, LANE), f32)
        for r0, rows in _row_chunks(t):
            sl = slice(r0, r0 + rows)
            dgb_ref[sl, :] = (dy_ref[sl, :] * _taps_apply(pad_ref, w_ref, taps, r0, rows)).astype(bf16)
            dm = _taps_apply(gpad_ref, w_ref, ttaps, r0, rows)
            dgc_ref[sl, :] = (dm * h_ref[sl, :]).astype(bf16)
            dh_ref[sl, :] = (dm * gc_ref[sl, :]).astype(bf16)
            dw = dw + _taps_wgrad(pad_ref, gpad_ref[PAD_ROWS + r0:PAD_ROWS + r0 + rows, :], taps, r0, rows, nw)
        dw_ref[...] = dw

    blk = SDS((t, 4 * LANE), bf16)
    return pl.pallas_call(
        body, out_shape=(blk, blk, blk, SDS((nw, 4 * LANE), f32)), grid=(4,),
        in_specs=[_col_spec(t, 0), _col_spec(t, 4), _col_spec(t, 8), _w_spec(nw), _col_spec(t, 0)],
        out_specs=(_col_spec(t, 0), _col_spec(t, 0), _col_spec(t, 0), _w_spec(nw)),
        scratch_shapes=[pltpu.VMEM((t + 2 * PAD_ROWS, LANE), f32)] * 2, name=name,
        compiler_params=_cparams(("parallel",)),
    )(p, p, p, w, dy)


def conf_conv(p, w, name):
    t = p.shape[0]
    nw = w.shape[0]
    taps = _taps_1d(31)

    def body(a_ref, b_ref, w_ref, o_ref, pad_ref):
        _fill_pad(pad_ref, a_ref[...] * jax.nn.sigmoid(b_ref[...]), t)
        for r0, rows in _row_chunks(t):
            o_ref[r0:r0 + rows, :] = _taps_apply(pad_ref, w_ref, taps, r0, rows)

    return pl.pallas_call(
        body, out_shape=SDS((t, 4 * LANE), f32), grid=(4,),
        in_specs=[_col_spec(t, 12), _col_spec(t, 16), _w_spec(nw)], out_specs=_col_spec(t, 0),
        scratch_shapes=[pltpu.VMEM((t + 2 * PAD_ROWS, LANE), f32)], name=name,
        compiler_params=_cparams(("parallel",)),
    )(p, p, w)


def conf_conv_bwd(p, w, dz, name):
    t = p.shape[0]
    nw = w.shape[0]
    taps = _taps_1d(31)
    ttaps = _transpose_taps(taps)

    def body(a_ref, b_ref, w_ref, dz_ref, da_ref, db_ref, dw_ref, pad_ref, gpad_ref):
        _fill_pad(pad_ref, a_ref[...] * jax.nn.sigmoid(b_ref[...]), t)
        _fill_pad(gpad_ref, dz_ref[...], t)
        dw = jnp.zeros((nw, LANE), f32)
        for r0, rows in _row_chunks(t):
            sl = slice(r0, r0 + rows)
            dm = _taps_apply(gpad_ref, w_ref, ttaps, r0, rows)
            sg = jax.nn.sigmoid(b_ref[sl, :])
            da_ref[sl, :] = (dm * sg).astype(bf16)
            db_ref[sl, :] = (dm * a_ref[sl, :] * sg * (1.0 - sg)).astype(bf16)
            dw = dw + _taps_wgrad(pad_ref, dz_ref[sl, :], taps, r0, rows, nw)
        dw_ref[...] = dw

    blk = SDS((t, 4 * LANE), bf16)
    return pl.pallas_call(
        body, out_shape=(blk, blk, SDS((nw, 4 * LANE), f32)), grid=(4,),
        in_specs=[_col_spec(t, 12), _col_spec(t, 16), _w_spec(nw), _col_spec(t, 0)],
        out_specs=(_col_spec(t, 0), _col_spec(t, 0), _w_spec(nw)),
        scratch_shapes=[pltpu.VMEM((t + 2 * PAD_ROWS, LANE), f32)] * 2, name=name,
        compiler_params=_cparams(("parallel",)),
    )(p, p, w, dz)


def ffn_conv(h, w, name):
    t = h.shape[0]
    nblk = D_FF // LANE
    nw = w.shape[0]
    taps = _taps_grid3()

    def body(a_ref, g_ref, w_ref, o_ref, *pads):
        _grid_pads_clear_edges(pads, t)
        for r0, rows in _row_chunks(t):
            _grid_pads_set(pads, r0, a_ref[r0:r0 + rows, :])
        for r0, rows in _row_chunks(t):
            o_ref[r0:r0 + rows, :] = (_silu(_taps_apply(pads, w_ref, taps, r0, rows)) * g_ref[r0:r0 + rows, :]).astype(bf16)

    return pl.pallas_call(
        body, out_shape=SDS((t, D_FF), bf16), grid=(nblk,),
        in_specs=[_col_spec(t, 0), _col_spec(t, nblk), _w_spec(nw)], out_specs=_col_spec(t, 0),
        scratch_shapes=[pltpu.VMEM((t + 2 * PAD_ROWS, LANE), f32)] * 3, name=name,
        compiler_params=_cparams(("parallel",)),
    )(h, h, w)


def ffn_conv_bwd(h, w, df, name):
    t = h.shape[0]
    nblk = D_FF // LANE
    nw = w.shape[0]
    taps = _taps_grid3()
    ttaps = _transpose_taps(taps)

    def body(a_ref, g_ref, w_ref, df_ref, dh_ref, dw_ref, pre_ref, *all_pads):
        half = pl.program_id(1)
        pads, gpads = all_pads[:3], all_pads[3:]

        @pl.when(half == 0)
        def _():
            _grid_pads_clear_edges(all_pads, t)
            for r0, rows in _row_chunks(t):
                _grid_pads_set(pads, r0, a_ref[r0:r0 + rows, :])
            for r0, rows in _row_chunks(t):
                sl = slice(r0, r0 + rows)
                pre = _taps_apply(pads, w_ref, taps, r0, rows)
                pre_ref[sl, :] = pre
                _grid_pads_set(gpads, r0, df_ref[sl, :] * g_ref[sl, :] * _dsilu(pre))
            dw = jnp.zeros((nw, LANE), f32)
            for r0, rows in _row_chunks(t):
                dh_ref[r0:r0 + rows, :] = _taps_apply(gpads, w_ref, ttaps, r0, rows).astype(bf16)
                dw = dw + _taps_wgrad(pads, gpads[1][PAD_ROWS + r0:PAD_ROWS + r0 + rows, :], taps, r0, rows, nw)
            dw_ref[...] = dw

        @pl.when(half == 1)
        def _():
            for r0, rows in _row_chunks(t):
                sl = slice(r0, r0 + rows)
                dh_ref[sl, :] = (df_ref[sl, :] * _silu(pre_ref[sl, :])).astype(bf16)

    cspec = lambda off: pl.BlockSpec((t, LANE), lambda c, s: (0, c + off))
    return pl.pallas_call(
        body, out_shape=(SDS((t, 2 * D_FF), bf16), SDS((nw, D_FF), f32)), grid=(nblk, 2),
        in_specs=[cspec(0), cspec(nblk), pl.BlockSpec((nw, LANE), lambda c, s: (0, c)), cspec(0)],
        out_specs=(pl.BlockSpec((t, LANE), lambda c, s: (0, c + nblk * s)), pl.BlockSpec((nw, LANE), lambda c, s: (0, c))),
        scratch_shapes=[pltpu.VMEM((t, LANE), f32)] + [pltpu.VMEM((t + 2 * PAD_ROWS, LANE), f32)] * 6, name=name,
        compiler_params=_cparams(("parallel", "arbitrary")),
    )(h, h, w, df)


def _pool_count(r0, rows, win, t):
    pos = lax.broadcasted_iota(jnp.int32, (rows, 1), 0) + r0
    lo = jnp.clip(pos - win // 2, 0, t)
    hi = jnp.clip(pos - win // 2 + win, 0, t)
    return (hi - lo).astype(f32)


def _window_sum(pad_ref, r0, rows, lo, hi):
    acc = jnp.zeros((rows, LANE), f32)
    for off in range(lo, hi):
        acc = acc + pad_ref[PAD_ROWS + r0 + off:PAD_ROWS + r0 + off + rows, :]
    return acc


def pool_mix(p, pool_w, pool_scale, name):
    t = p.shape[0]

    def body(x_ref, w_ref, s_ref, o_ref, pad_ref):
        for gi, win in enumerate(POOL_WINDOWS):
            cs = slice(gi * LANE, (gi + 1) * LANE)
            _fill_pad(pad_ref, x_ref[:, cs], t)
            wg = w_ref[gi].astype(bf16)
            for r0, rows in _row_chunks(t):
                pooled = _window_sum(pad_ref, r0, rows, -(win // 2), win - win // 2) / _pool_count(r0, rows, win, t) - x_ref[r0:r0 + rows, cs]
                o_ref[r0:r0 + rows, cs] = (_dotb(pooled, wg) * s_ref[:, cs]).astype(bf16)

    return pl.pallas_call(
        body, out_shape=SDS((t, 512), bf16), grid=(1,),
        in_specs=[pl.BlockSpec((t, 512), lambda i: (0, 4)), pl.BlockSpec((4, LANE, LANE), lambda i: (0, 0, 0)),
                  pl.BlockSpec((1, 512), lambda i: (0, 0))],
        out_specs=pl.BlockSpec((t, 512), lambda i: (0, 0)),
        scratch_shapes=[pltpu.VMEM((t + 2 * PAD_ROWS, LANE), f32)], name=name,
        compiler_params=_cparams(("arbitrary",)),
    )(p, pool_w, pool_scale)


def pool_mix_bwd(p, pool_w, pool_scale, dmix, name):
    t = p.shape[0]

    def body(x_ref, w_ref, s_ref, dy_ref, dp_ref, dw_ref, ds_ref, pad_ref, gpad_ref, dpool_ref):
        for gi, win in enumerate(POOL_WINDOWS):
            cs = slice(gi * LANE, (gi + 1) * LANE)
            h = win // 2
            _fill_pad(pad_ref, x_ref[:, cs], t)
            wg = w_ref[gi].astype(bf16)
            dw = jnp.zeros((LANE, LANE), f32)
            ds = jnp.zeros((1, LANE), f32)
            zeros = jnp.zeros((PAD_ROWS, LANE), f32)
            gpad_ref[0:PAD_ROWS, :] = zeros
            gpad_ref[PAD_ROWS + t:2 * PAD_ROWS + t, :] = zeros
            for r0, rows in _row_chunks(t):
                cnt = _pool_count(r0, rows, win, t)
                pooled = _window_sum(pad_ref, r0, rows, -h, win - h) / cnt - x_ref[r0:r0 + rows, cs]
                dy = dy_ref[r0:r0 + rows, cs]
                ds = ds + jnp.sum(dy * _dotb(pooled, wg), axis=0, keepdims=True)
                dypre = dy * s_ref[:, cs]
                dw = dw + _dotb_tn(pooled, dypre)
                dpooled = _dotb_nt(dypre, wg)
                gpad_ref[PAD_ROWS + r0:PAD_ROWS + r0 + rows, :] = dpooled / cnt
                dpool_ref[r0:r0 + rows, :] = dpooled
            dw_ref[gi] = dw
            ds_ref[:, cs] = ds
            for r0, rows in _row_chunks(t):
                dx = _window_sum(gpad_ref, r0, rows, -h + 1, h + 1) - dpool_ref[r0:r0 + rows, :]
                dp_ref[r0:r0 + rows, cs] = dx.astype(bf16)

    return pl.pallas_call(
        body, out_shape=(SDS((t, 512), bf16), SDS((4, LANE, LANE), f32), SDS((1, 512), f32)), grid=(1,),
        in_specs=[pl.BlockSpec((t, 512), lambda i: (0, 4)), pl.BlockSpec((4, LANE, LANE), lambda i: (0, 0, 0)),
                  pl.BlockSpec((1, 512), lambda i: (0, 0)), pl.BlockSpec((t, 512), lambda i: (0, 1))],
        out_specs=(pl.BlockSpec((t, 512), lambda i: (0, 0)), pl.BlockSpec((4, LANE, LANE), lambda i: (0, 0, 0)),
                   pl.BlockSpec((1, 512), lambda i: (0, 0))),
        scratch_shapes=[pltpu.VMEM((t + 2 * PAD_ROWS, LANE), f32)] * 2 + [pltpu.VMEM((t, LANE), f32)], name=name,
        compiler_params=_cparams(("arbitrary",)),
    )(p, pool_w, pool_scale, dmix)


def gated_rmsnorm(o_a, o_b, p, norm_w, name):
    t = o_a.shape[0]
    tt = _row_tile(t)

    def body(oa_ref, ob_ref, g_ref, nw_ref, y_ref):
        for h in range(GDN_HEADS):
            cs = slice(h * LANE, (h + 1) * LANE)
            o = oa_ref[:, cs] + ob_ref[:, cs]
            r = lax.rsqrt(jnp.mean(o * o, axis=-1, keepdims=True) + RMS_EPS)
            y_ref[:, cs] = (o * r * nw_ref[...] * _silu(g_ref[:, cs])).astype(bf16)

    return pl.pallas_call(
        body, out_shape=SDS((t, 512), bf16), grid=(t // tt,),
        in_specs=[_row_spec(tt, 512), _row_spec(tt, 512), pl.BlockSpec((tt, 512), lambda i: (i, 3)), _vec_spec(LANE)],
        out_specs=_row_spec(tt, 512), name=name, compiler_params=_cparams(("parallel",)),
    )(o_a, o_b, p, norm_w)


def gated_rmsnorm_bwd(o_a, o_b, p, norm_w, dmix, name):
    t = o_a.shape[0]
    tt = _row_tile(t)

    def body(oa_ref, ob_ref, g_ref, nw_ref, dy_ref, do_ref, dg_ref, dnw_ref):
        dnw = jnp.zeros((1, LANE), f32)
        for h in range(GDN_HEADS):
            cs = slice(h * LANE, (h + 1) * LANE)
            o = oa_ref[:, cs] + ob_ref[:, cs]
            r = lax.rsqrt(jnp.mean(o * o, axis=-1, keepdims=True) + RMS_EPS)
            gate = g_ref[:, cs]
            dy = dy_ref[:, cs]
            dy1 = dy * _silu(gate)
            dg_ref[:, cs] = (dy * (o * r * nw_ref[...]) * _dsilu(gate)).astype(bf16)
            dnw = dnw + jnp.sum(dy1 * o * r, axis=0, keepdims=True)
            dn = dy1 * nw_ref[...]
            do_ref[:, cs] = r * dn - o * (r * r * r) * jnp.mean(dn * o, axis=-1, keepdims=True)
        _acc_rows(dnw_ref, dnw)

    return pl.pallas_call(
        body, out_shape=(SDS((t, 512), f32), SDS((t, 512), bf16), SDS((1, LANE), f32)), grid=(t // tt,),
        in_specs=[_row_spec(tt, 512), _row_spec(tt, 512), pl.BlockSpec((tt, 512), lambda i: (i, 3)), _vec_spec(LANE),
                  _row_spec(tt, 512)],
        out_specs=(_row_spec(tt, 512), _row_spec(tt, 512), _vec_spec(LANE)),
        name=name, compiler_params=_cparams(("arbitrary",)),
    )(o_a, o_b, p, norm_w, dmix)


def ln_silu(z, g, b, name):
    t, d = z.shape
    tt = _row_tile(t)

    def body(z_ref, g_ref, b_ref, o_ref):
        xhat, _ = _ln_stats(z_ref[...])
        o_ref[...] = _silu(xhat * g_ref[...] + b_ref[...]).astype(bf16)

    return pl.pallas_call(
        body, out_shape=SDS((t, d), bf16), grid=(t // tt,),
        in_specs=[_row_spec(tt, d), _vec_spec(d), _vec_spec(d)], out_specs=_row_spec(tt, d),
        name=name, compiler_params=_cparams(("parallel",)),
    )(z, g, b)


def ln_silu_bwd(z, g, b, dmix, name):
    t, d = z.shape
    tt = _row_tile(t)

    def body(z_ref, g_ref, b_ref, dy_ref, dz_ref, dg_ref, db_ref):
        xhat, rstd = _ln_stats(z_ref[...])
        dn = dy_ref[...] * _dsilu(xhat * g_ref[...] + b_ref[...])
        dz_ref[...] = _ln_bwd(dn * g_ref[...], xhat, rstd)
        _acc_rows(dg_ref, jnp.sum(dn * xhat, axis=0, keepdims=True))
        _acc_rows(db_ref, jnp.sum(dn, axis=0, keepdims=True))

    return pl.pallas_call(
        body, out_shape=(SDS((t, d), f32), SDS((1, d), f32), SDS((1, d), f32)), grid=(t // tt,),
        in_specs=[_row_spec(tt, d), _vec_spec(d), _vec_spec(d), pl.BlockSpec((tt, d), lambda i: (i, 1))],
        out_specs=(_row_spec(tt, d), _vec_spec(d), _vec_spec(d)),
        name=name, compiler_params=_cparams(("arbitrary",)),
    )(z, g, b, dmix)


def gdn_gates(p, neg_a, dt_bias, name):
    t = p.shape[0]
    tt = _row_tile(t)

    def body(s_ref, na_ref, dt_ref, o_ref):
        s = s_ref[...]
        col = lax.broadcasted_iota(jnp.int32, s.shape, 1)
        o_ref[...] = jnp.where(col < 8, jax.nn.sigmoid(s), na_ref[...] * jax.nn.softplus(s + dt_ref[...]))

    return pl.pallas_call(
        body, out_shape=SDS((t, LANE), f32), grid=(t // tt,),
        in_specs=[pl.BlockSpec((tt, LANE), lambda i: (i, 20)), _vec_spec(LANE), _vec_spec(LANE)],
        out_specs=_row_spec(tt, LANE), name=name, compiler_params=_cparams(("parallel",)),
    )(p, neg_a, dt_bias)


def gdn_gates_bwd(p, neg_a, dt_bias, dbg_a, dbg_b, name):
    t = p.shape[0]
    tt = _row_tile(t)

    def body(s_ref, na_ref, dt_ref, d_ref, d2_ref, ds_ref, da_ref, ddt_ref):
        s = s_ref[...]
        d = d_ref[...] + d2_ref[...]
        col = lax.broadcasted_iota(jnp.int32, s.shape, 1)
        sg = jax.nn.sigmoid(s)
        z = s + dt_ref[...]
        dz = jnp.where((col >= 8) & (col < 16), d * na_ref[...] * jax.nn.sigmoid(z), 0.0)
        ds_ref[...] = jnp.where(col < 8, d * sg * (1.0 - sg), dz).astype(bf16)
        dalog = jnp.where((col >= 8) & (col < 16), d * na_ref[...] * jax.nn.softplus(z), 0.0)
        _acc_rows(da_ref, jnp.sum(dalog, axis=0, keepdims=True))
        _acc_rows(ddt_ref, jnp.sum(dz, axis=0, keepdims=True))

    return pl.pallas_call(
        body, out_shape=(SDS((t, LANE), bf16), SDS((1, LANE), f32), SDS((1, LANE), f32)), grid=(t // tt,),
        in_specs=[pl.BlockSpec((tt, LANE), lambda i: (i, 20)), _vec_spec(LANE), _vec_spec(LANE), _row_spec(tt, LANE),
                  _row_spec(tt, LANE)],
        out_specs=(_row_spec(tt, LANE), _vec_spec(LANE), _vec_spec(LANE)),
        name=name, compiler_params=_cparams(("arbitrary",)),
    )(p, neg_a, dt_bias, dbg_a, dbg_b)


N_SCAN = 2 * GDN_HEADS


def _bdot(a, b, ca, cb, precision=None):
    if precision is None:
        a, b = a.astype(bf16), b.astype(bf16)
    return lax.dot_general(a, b, (((ca,), (cb,)), ((0,), (0,))), preferred_element_type=f32, precision=precision)


def _bdot_nn(a, b, precision=None):
    return _bdot(a, b, 2, 1, precision)


def _bdot_nt(a, b):
    return _bdot(a, b, 2, 2)


def _bdot_tn(a, b, precision=None):
    return _bdot(a, b, 1, 1, precision)


def _order_masks():
    shape = (N_SCAN, CHUNK, CHUNK)
    sign = jnp.where(lax.broadcasted_iota(jnp.int32, shape, 0) >= GDN_HEADS, -1, 1)
    ahead = (lax.broadcasted_iota(jnp.int32, shape, 1) - lax.broadcasted_iota(jnp.int32, shape, 2)) * sign
    lower, strict, lower_t = ahead >= 0, ahead > 0, ahead <= 0
    col_shape = (N_SCAN, CHUNK, 1)
    back1 = lax.broadcasted_iota(jnp.int32, col_shape, 0) >= GDN_HEADS
    row1 = lax.broadcasted_iota(jnp.int32, col_shape, 1)
    at_last = (row1 == jnp.where(back1, 0, CHUNK - 1)).astype(f32)
    return lower, strict, lower_t, at_last


def _stack_heads(f_ref, b_ref):
    return jnp.stack([ref[:, h * LANE:(h + 1) * LANE] for ref in (f_ref, b_ref) for h in range(GDN_HEADS)])


def _stack_gates(bgf, bgb, bgtf, bgtb):
    beta = jnp.stack([bg[:, 4 * d + h:4 * d + h + 1] for d, bg in enumerate((bgf, bgb)) for h in range(GDN_HEADS)])
    g_col = jnp.stack([bg[:, 8 + 4 * d + h:9 + 4 * d + h] for d, bg in enumerate((bgf, bgb)) for h in range(GDN_HEADS)])
    g_row = jnp.stack([bgt[8 + 4 * d + h:9 + 4 * d + h, :] for d, bgt in enumerate((bgtf, bgtb)) for h in range(GDN_HEADS)])
    return beta, g_col, g_row


def _chunk_terms(k, v, beta, g_col, g_row, masks, tinv=None):
    lower, strict, lower_t, at_last = masks
    gc = jnp.sum(lower.astype(f32) * g_row, axis=2, keepdims=True)
    gr = jnp.sum(lower_t.astype(f32) * g_col, axis=1, keepdims=True)
    g_last = jnp.sum(at_last * gc, axis=1, keepdims=True)
    e = jnp.exp(gc)
    f = jnp.exp(g_last - gc)
    dm = jnp.exp(jnp.where(lower, gc - gr, -1e30))
    kb = k * beta
    kk = _bdot_nt(kb, k)
    if tinv is None:
        shape = (N_SCAN, CHUNK, CHUNK)
        eye = (lax.broadcasted_iota(jnp.int32, shape, 1) == lax.broadcasted_iota(jnp.int32, shape, 2)).astype(f32)
        pw = -jnp.where(strict, kk * dm, 0.0)
        tinv = eye + pw
        for _ in range(5):
            pw = _bdot_nn(pw, pw, HI)
            tinv = tinv + _bdot_nn(tinv, pw, HI)
    u = _bdot_nn(tinv, v * beta)
    w = _bdot_nn(tinv, kb * e)
    return dict(e=e, f=f, gl=jnp.exp(g_last), dm=dm, kb=kb, kk=kk, tinv=tinv, u=u, w=w, kd=k * f)


def _gdn_specs(nc, width, step_chunk):
    return [pl.BlockSpec((CHUNK, width), functools.partial(lambda i, d: (step_chunk(i, d), 0), d=d)) for d in (0, 1)]


def gdn_forward(q, k, v, bg, bgt, s0, with_out, name):
    t = k.shape[0]
    nc = t // CHUNK

    def body(qf_ref, qb_ref, kf_ref, kb_ref, vf_ref, vb_ref, bgf_ref, bgb_ref, bgtf_ref, bgtb_ref, s0_ref,
             of_ref, ob_ref, sallf_ref, sallb_ref, tinvf_ref, tinvb_ref, sfin_ref, s_ref):
        i = pl.program_id(0)

        @pl.when(i == 0)
        def _():
            s_ref[...] = s0_ref[...]

        masks = _order_masks()
        k8, v8 = _stack_heads(kf_ref, kb_ref), _stack_heads(vf_ref, vb_ref)
        beta, g_col, g_row = _stack_gates(bgf_ref[...], bgb_ref[...], bgtf_ref[0], bgtb_ref[0])
        c = _chunk_terms(k8, v8, beta, g_col, g_row, masks)
        s = s_ref[...]
        sallf_ref[0] = s[:GDN_HEADS]
        sallb_ref[0] = s[GDN_HEADS:]
        tinvf_ref[0] = c["tinv"][:GDN_HEADS]
        tinvb_ref[0] = c["tinv"][GDN_HEADS:]
        vn = c["u"] - _bdot_nn(c["w"], s)
        if with_out:
            q8 = _stack_heads(qf_ref, qb_ref)
            pm = jnp.where(masks[0], _bdot_nt(q8, k8) * c["dm"], 0.0)
            o = _bdot_nn(q8 * c["e"], s) + _bdot_nn(pm, vn)
        for d, o_ref in enumerate((of_ref, ob_ref)):
            for h in range(GDN_HEADS):
                o_ref[:, h * LANE:(h + 1) * LANE] = o[GDN_HEADS * d + h] if with_out else jnp.zeros((CHUNK, LANE), f32)
        s_ref[...] = c["gl"] * s + _bdot_tn(c["kd"], vn)

        @pl.when(i == nc - 1)
        def _():
            sfin_ref[...] = s_ref[...]

    chunk_of = lambda i, d: i if d == 0 else nc - 1 - i
    seq = _gdn_specs(nc, 512, chunk_of)
    gate = _gdn_specs(nc, LANE, chunk_of)
    gate_t = [pl.BlockSpec((1, 16, CHUNK), functools.partial(lambda i, d: (chunk_of(i, d), 0, 0), d=d)) for d in (0, 1)]
    sall = [pl.BlockSpec((1, GDN_HEADS, LANE, LANE), functools.partial(lambda i, d: (chunk_of(i, d), 0, 0, 0), d=d)) for d in (0, 1)]
    tinv = [pl.BlockSpec((1, GDN_HEADS, CHUNK, CHUNK), functools.partial(lambda i, d: (chunk_of(i, d), 0, 0, 0), d=d)) for d in (0, 1)]
    st = pl.BlockSpec((N_SCAN, LANE, LANE), lambda i: (0, 0, 0))
    o_shape, s_shape, t_shape = SDS((t, 512), f32), SDS((nc, GDN_HEADS, LANE, LANE), f32), SDS((nc, GDN_HEADS, CHUNK, CHUNK), f32)
    o_f, o_b, sall_f, sall_b, tinv_f, tinv_b, s_fin = pl.pallas_call(
        body, out_shape=(o_shape, o_shape, s_shape, s_shape, t_shape, t_shape, SDS((N_SCAN, LANE, LANE), f32)), grid=(nc,),
        in_specs=seq + seq + seq + gate + gate_t + [st], out_specs=tuple(seq + sall + tinv + [st]),
        scratch_shapes=[pltpu.VMEM((N_SCAN, LANE, LANE), f32)], name=name,
        compiler_params=_cparams(("arbitrary",)),
    )(q, q, k, k, v, v, bg, bg, bgt, bgt, s0.reshape(N_SCAN, LANE, LANE))
    return o_f, o_b, (sall_f, sall_b, tinv_f, tinv_b), s_fin.reshape(2, GDN_HEADS, LANE, LANE)


def _gdn_chunk_bwd(q, k, v, d_o, beta, g_col, g_row, s, tinv, dsn, masks):
    lower, strict, _, at_last = masks
    c = _chunk_terms(k, v, beta, g_col, g_row, masks, tinv)
    e, f, gl, dm, kb, kk, tinv, u, w, kd = (c[n] for n in ("e", "f", "gl", "dm", "kb", "kk", "tinv", "u", "w", "kd"))
    vn = u - _bdot_nn(w, s)
    ds = gl * dsn
    dgl = jnp.sum(jnp.sum(s * dsn, axis=2, keepdims=True), axis=1, keepdims=True)
    dkd = _bdot_nt(vn, dsn)
    dvn = _bdot_nn(kd, dsn)
    dm_grad = jnp.zeros((N_SCAN, CHUNK, CHUNK), f32)
    de = jnp.zeros((N_SCAN, CHUNK, 1), f32)
    dq = None
    dk = jnp.zeros((N_SCAN, CHUNK, LANE), f32)
    if q is not None:
        qk = _bdot_nt(q, k)
        pm = jnp.where(lower, qk * dm, 0.0)
        dqd = _bdot_nt(d_o, s)
        ds = ds + _bdot_tn(q * e, d_o)
        dpm = jnp.where(lower, _bdot_nt(d_o, vn), 0.0)
        dvn = dvn + _bdot_tn(pm, d_o)
        dqk = dpm * dm
        dm_grad = dm_grad + dpm * qk
        dq = _bdot_nn(dqk, k) + dqd * e
        dk = _bdot_tn(dqk, q)
        de = de + jnp.sum(dqd * q, axis=2, keepdims=True)
    dw = -_bdot_nt(dvn, s)
    ds = ds - _bdot_tn(w, dvn)
    drv = _bdot_tn(tinv, dvn)
    drk = _bdot_tn(tinv, dw)
    da = -jnp.where(strict, _bdot_nt(drv, u) + _bdot_nt(drk, w), 0.0)
    dbeta = jnp.sum(drv * v, axis=2, keepdims=True)
    dv = drv * beta
    dkb = drk * e
    de = de + jnp.sum(drk * kb, axis=2, keepdims=True)
    dkk = da * dm
    dm_grad = dm_grad + da * kk
    dkb = dkb + _bdot_nn(dkk, k)
    dk = dk + _bdot_tn(dkk, kb) + dkd * f
    df = jnp.sum(dkd * k, axis=2, keepdims=True)
    dbeta = dbeta + jnp.sum(dkb * k, axis=2, keepdims=True)
    dk = dk + dkb * beta
    m = dm_grad * dm
    ones = jnp.ones((N_SCAN, CHUNK, LANE), f32)
    rsum = jnp.sum(m, axis=2, keepdims=True)
    csum = _bdot_tn(m, ones, HI)[:, :, 0:1]
    dgl_tot = jnp.sum(df * f, axis=1, keepdims=True) + dgl * gl
    dgc = de * e - df * f + rsum - csum + at_last * dgl_tot
    dg = _bdot_tn(lower.astype(f32), dgc * ones, HI)[:, :, 0:1]
    return dq, dk, dv, dbeta, dg, ds


def gdn_backward(q, k, v, bg, bgt, saved, d_o, ds_fin, with_out, name):
    t = k.shape[0]
    nc = t // CHUNK

    def body(qf_ref, qb_ref, kf_ref, kb_ref, vf_ref, vb_ref, bgf_ref, bgb_ref, bgtf_ref, bgtb_ref,
             sallf_ref, sallb_ref, tinvf_ref, tinvb_ref, dof_ref, dob_ref, dsf_ref,
             dqf_ref, dqb_ref, dkf_ref, dkb_ref, dvf_ref, dvb_ref, dbgf_ref, dbgb_ref, ds0_ref, ds_ref):
        i = pl.program_id(0)

        @pl.when(i == 0)
        def _():
            ds_ref[...] = dsf_ref[...]

        lane = lax.broadcasted_iota(jnp.int32, (1, LANE), 1)
        masks = _order_masks()
        beta, g_col, g_row = _stack_gates(bgf_ref[...], bgb_ref[...], bgtf_ref[0], bgtb_ref[0])
        s = jnp.concatenate([sallf_ref[0], sallb_ref[0]], 0)
        tinv = jnp.concatenate([tinvf_ref[0], tinvb_ref[0]], 0)
        dq, dk, dv, dbeta, dg, ds = _gdn_chunk_bwd(
            _stack_heads(qf_ref, qb_ref) if with_out else None, _stack_heads(kf_ref, kb_ref), _stack_heads(vf_ref, vb_ref),
            _stack_heads(dof_ref, dob_ref), beta, g_col, g_row, s, tinv, ds_ref[...], masks)
        ds_ref[...] = ds
        for d, (dq_ref, dk_ref, dv_ref, dbg_ref) in enumerate(((dqf_ref, dkf_ref, dvf_ref, dbgf_ref), (dqb_ref, dkb_ref, dvb_ref, dbgb_ref))):
            dbg = jnp.zeros((CHUNK, LANE), f32)
            for h in range(GDN_HEADS):
                b = GDN_HEADS * d + h
                cs = slice(h * LANE, (h + 1) * LANE)
                dq_ref[:, cs] = dq[b] if with_out else jnp.zeros((CHUNK, LANE), f32)
                dk_ref[:, cs] = dk[b]
                dv_ref[:, cs] = dv[b]
                dbg = dbg + dbeta[b] * (lane == b).astype(f32) + dg[b] * (lane == 8 + b).astype(f32)
            dbg_ref[...] = dbg

        @pl.when(i == nc - 1)
        def _():
            ds0_ref[...] = ds_ref[...]

    chunk_of = lambda i, d: nc - 1 - i if d == 0 else i
    seq = _gdn_specs(nc, 512, chunk_of)
    gate = _gdn_specs(nc, LANE, chunk_of)
    gate_t = [pl.BlockSpec((1, 16, CHUNK), functools.partial(lambda i, d: (chunk_of(i, d), 0, 0), d=d)) for d in (0, 1)]
    sall = [pl.BlockSpec((1, GDN_HEADS, LANE, LANE), functools.partial(lambda i, d: (chunk_of(i, d), 0, 0, 0), d=d)) for d in (0, 1)]
    tinv = [pl.BlockSpec((1, GDN_HEADS, CHUNK, CHUNK), functools.partial(lambda i, d: (chunk_of(i, d), 0, 0, 0), d=d)) for d in (0, 1)]
    st = pl.BlockSpec((N_SCAN, LANE, LANE), lambda i: (0, 0, 0))
    o_shape, g_shape = SDS((t, 512), f32), SDS((t, LANE), f32)
    res = pl.pallas_call(
        body, out_shape=(o_shape,) * 6 + (g_shape, g_shape, SDS((N_SCAN, LANE, LANE), f32)), grid=(nc,),
        in_specs=seq + seq + seq + gate + gate_t + sall + tinv + seq + [st], out_specs=tuple(seq + seq + seq + gate + [st]),
        scratch_shapes=[pltpu.VMEM((N_SCAN, LANE, LANE), f32)], name=name,
        compiler_params=_cparams(("arbitrary",)),
    )(q, q, k, k, v, v, bg, bg, bgt, bgt, *saved, d_o, d_o, ds_fin.reshape(N_SCAN, LANE, LANE))
    return tuple(res[:8]) + (res[8].reshape(2, GDN_HEADS, LANE, LANE),)


def _my_position():
    x, y, c = lax.axis_index("x"), lax.axis_index("y"), lax.axis_index("c")
    return x, y, c, 4 * x + 2 * y + c


def exchange(arrays, scatter, name):
    n = len(arrays)
    shapes = [a.shape[1:] if scatter else a.shape for a in arrays]

    def body(*refs):
        ins, outs, token = refs[:n], refs[n:2 * n], refs[2 * n]
        send_sems, recv_sems, local_sems = refs[2 * n + 1:]
        x, y, c, me = _my_position()
        token[...] = jnp.zeros_like(token)
        started = []
        for a in range(n):
            mine = pltpu.make_async_copy(ins[a].at[me] if scatter else ins[a], outs[a].at[me], local_sems.at[a])
            mine.start()
            started.append(mine)
        waits = []
        for r in range(1, N_DEV):
            px = 1 - x if r & 4 else x
            py = 1 - y if r & 2 else y
            pc = 1 - c if r & 1 else c
            pid = 4 * px + 2 * py + pc
            for a in range(n):
                cp = pltpu.make_async_remote_copy(
                    src_ref=ins[a].at[pid] if scatter else ins[a], dst_ref=outs[a].at[me],
                    send_sem=send_sems.at[a, r - 1], recv_sem=recv_sems.at[a, r - 1],
                    device_id=(px, py, pc), device_id_type=pl.DeviceIdType.MESH)
                cp.start()
                arrive = pltpu.make_async_remote_copy(
                    src_ref=ins[a].at[pid] if scatter else ins[a], dst_ref=outs[a].at[pid],
                    send_sem=send_sems.at[a, r - 1], recv_sem=recv_sems.at[a, r - 1],
                    device_id=(px, py, pc), device_id_type=pl.DeviceIdType.MESH)
                waits.append((cp, arrive))
        for cp, arrive in waits:
            cp.wait_send()
            arrive.wait_recv()
        for mine in started:
            mine.wait()

    any_spec = pl.BlockSpec(memory_space=pl.ANY)
    return pl.pallas_call(
        body, out_shape=tuple(SDS((N_DEV,) + tuple(s), a.dtype) for s, a in zip(shapes, arrays)) + (SDS((8, LANE), f32),),
        in_specs=[any_spec] * n, out_specs=tuple([any_spec] * n) + (pl.BlockSpec(memory_space=pltpu.VMEM),),
        scratch_shapes=[pltpu.SemaphoreType.DMA((n, N_DEV - 1)), pltpu.SemaphoreType.DMA((n, N_DEV - 1)),
                        pltpu.SemaphoreType.DMA((n,))],
        name=name,
    )(*arrays)


_HBM_SPEC = pl.BlockSpec(memory_space=pltpu.HBM)
_SEM_SPEC = pl.BlockSpec(memory_space=pltpu.SEMAPHORE)
_DATAFLOW = pltpu.SideEffectType.DATAFLOW_SIDE_EFFECTING


def _peers(x, y, c):
    out = []
    for r in range(1, N_DEV):
        px = 1 - x if r & 4 else x
        py = 1 - y if r & 2 else y
        pc = 1 - c if r & 1 else c
        out.append((r, (px, py, pc), 4 * px + 2 * py + pc))
    return out


def _exchange_copies(ins, lands, send_sems, recv_sems, scatter, arrivals):
    x, y, c, me = _my_position()
    pairs = []
    for r, peer, pid in _peers(x, y, c):
        for a in range(len(ins)):
            k = a * (N_DEV - 1) + r - 1
            kw = dict(send_sem=send_sems.at[k], recv_sem=recv_sems.at[k], device_id=peer, device_id_type=pl.DeviceIdType.MESH)
            src = ins[a].at[pid] if scatter else ins[a]
            send = pltpu.make_async_remote_copy(src_ref=src, dst_ref=lands[a].at[me], **kw)
            arrive = pltpu.make_async_remote_copy(src_ref=src, dst_ref=lands[a].at[pid], **kw) if arrivals else None
            pairs.append((send, arrive))
    return pairs


def exchange_start(arrays, scatter, name):
    n = len(arrays)
    shapes = [a.shape[1:] if scatter else a.shape for a in arrays]

    def body(*refs):
        ins, lands = refs[:n], refs[n:2 * n]
        send_sems, recv_sems = refs[2 * n], refs[2 * n + 1]
        token = refs[-1]
        for send, _ in _exchange_copies(ins, lands, send_sems, recv_sems, scatter, False):
            send.start()
        token[...] = jnp.zeros_like(token)

    sem = pltpu.SemaphoreType.DMA((n * (N_DEV - 1),))
    land_shapes = [(N_DEV,) + tuple(s) for s in shapes]
    res = pl.pallas_call(
        body, name=name,
        out_shape=(sem, sem, *[pltpu.HBM(a.shape, a.dtype) for a in arrays],
                   *[pltpu.HBM(s, a.dtype) for s, a in zip(land_shapes, arrays)], SDS((8, LANE), f32)),
        in_specs=[_HBM_SPEC] * (2 * n),
        out_specs=(_SEM_SPEC, _SEM_SPEC, *[_HBM_SPEC] * (2 * n), pl.BlockSpec(memory_space=pltpu.VMEM)),
        input_output_aliases={i: 2 + i for i in range(2 * n)},
        compiler_params=pltpu.CompilerParams(has_side_effects=_DATAFLOW),
    )(*[pltpu.with_memory_space_constraint(a, pltpu.HBM) for a in arrays],
      *[pltpu.with_memory_space_constraint(lax.empty(s, a.dtype), pltpu.HBM) for s, a in zip(land_shapes, arrays)])
    return (res[0], res[1], list(res[2:2 + n]), list(res[2 + n:2 + 2 * n]), scatter), res[-1]


def exchange_wait(handle, after, name):
    send_sems, recv_sems, ins, lands, scatter = handle
    n = len(ins)

    def body(*refs):
        in_refs, land_refs = refs[:n], refs[n:2 * n]
        for send, arrive in _exchange_copies(in_refs, land_refs, refs[2 * n], refs[2 * n + 1], scatter, True):
            send.wait_send()
            arrive.wait_recv()
        refs[-1][...] = jnp.zeros_like(refs[-1])

    res = pl.pallas_call(
        body, name=name,
        out_shape=tuple(pltpu.HBM(a.shape, a.dtype) for a in ins + lands) + (SDS((8, LANE), f32),),
        in_specs=[_HBM_SPEC] * (2 * n) + [_SEM_SPEC, _SEM_SPEC, pl.BlockSpec(memory_space=pl.ANY)],
        out_specs=tuple([_HBM_SPEC] * (2 * n)) + (pl.BlockSpec(memory_space=pltpu.VMEM),),
        input_output_aliases={i: i for i in range(2 * n)},
        compiler_params=pltpu.CompilerParams(has_side_effects=_DATAFLOW),
    )(*ins, *lands, send_sems, recv_sems, after)
    return list(res[:n]), list(res[n:2 * n]), res[-1]


def place_own(lands, arrays, scatter, me):
    own = [lax.dynamic_index_in_dim(a, me, 0, keepdims=False) if scatter else a for a in arrays]
    return [lax.dynamic_update_index_in_dim(l, o, me, 0) for l, o in zip(lands, own)]


def ada_forward(a_raw, ada_w, ada_b_loc, name):
    def body(a_ref, w_ref, b_ref, o_ref):
        a = _silu(a_ref[...])
        for l in range(DEPTH):
            o_ref[l] = _dotf(a, w_ref[l]) + b_ref[l]

    return pl.pallas_call(body, out_shape=SDS((DEPTH, 16, ada_w.shape[2]), f32), name=name,
                          compiler_params=_cparams())(a_raw, ada_w, ada_b_loc)


def ada_backward(a_raw, ada_w, dm, name):
    def body(a_ref, w_ref, dm_ref, gw_ref, dcc_ref):
        a = _silu(a_ref[...])
        for l in range(DEPTH):
            gw_ref[l] = _dotf(a, dm_ref[l], (((0,), (0,)), ((), ())))
        dcc_ref[...] = _dotf(dm_ref[0, 8:16, :], w_ref[0], (((1,), (1,)), ((), ())))

    return pl.pallas_call(body, out_shape=(SDS(ada_w.shape, f32), SDS((8, ada_w.shape[1]), f32)), name=name,
                          compiler_params=_cparams())(a_raw, ada_w, dm)


def sum_parts(parts, name):
    _, r, c = parts.shape

    def body(p_ref, o_ref):
        acc = p_ref[0]
        for i in range(1, N_DEV):
            acc = acc + p_ref[i]
        o_ref[...] = acc

    return pl.pallas_call(body, out_shape=SDS((r, c), f32), name=name, compiler_params=_cparams())(parts)


def cctx_grad(parts, c_ctx, name):
    def body(p_ref, c_ref, o_ref):
        acc = p_ref[0, 0:1, :]
        for i in range(1, N_DEV):
            acc = acc + p_ref[i, 0:1, :]
        o_ref[...] = acc * _dsilu(c_ref[...])

    return pl.pallas_call(body, out_shape=SDS((1, c_ctx.shape[1]), f32), name=name, compiler_params=_cparams())(parts, c_ctx)


def _adamw_math(g, w, m, v):
    m = ADAM_B1 * m + (1.0 - ADAM_B1) * g
    v = ADAM_B2 * v + (1.0 - ADAM_B2) * (g * g)
    m_hat = m / (1.0 - ADAM_B1 ** ADAM_STEP)
    v_hat = v / (1.0 - ADAM_B2 ** ADAM_STEP)
    delta = -ADAM_LR * (m_hat / (jnp.sqrt(v_hat) + ADAM_EPS) + ADAM_WD * w)
    return delta, m, v


def adamw(parts, w, m, v, name):
    n, r, c = parts.shape
    tr = _pick(r, (256, 128, 64, 32, 16, 8))

    def body(p_ref, w_ref, m_ref, v_ref, g_ref, d_ref, nm_ref, nv_ref):
        g = p_ref[0].astype(f32)
        for i in range(1, n):
            g = g + p_ref[i].astype(f32)
        g_ref[...] = g
        d_ref[...], nm_ref[...], nv_ref[...] = _adamw_math(g, w_ref[...], m_ref[...], v_ref[...])

    blk = pl.BlockSpec((tr, c), lambda i: (i, 0))
    out = SDS((r, c), f32)
    return pl.pallas_call(
        body, out_shape=(out, out, out, out), grid=(r // tr,),
        in_specs=[pl.BlockSpec((n, tr, c), lambda i: (0, i, 0)), blk, blk, blk], out_specs=(blk, blk, blk, blk),
        name=name, compiler_params=_cparams(("parallel",)),
    )(parts, w, m, v)


def adamw_small(items, name):
    n = len(items)

    def body(*refs):
        ins, outs = refs[:4 * n], refs[4 * n:]
        for i in range(n):
            g, w, m, v = (ins[4 * i + j][...] for j in range(4))
            outs[3 * i][...], outs[3 * i + 1][...], outs[3 * i + 2][...] = _adamw_math(g, w, m, v)

    flat = [a for it in items for a in it]
    out_shape = tuple(SDS(it[1].shape, f32) for it in items for _ in range(3))
    res = pl.pallas_call(body, out_shape=out_shape, name=name, compiler_params=_cparams())(*flat)
    return [tuple(res[3 * i:3 * i + 3]) for i in range(n)]


def _unshard(g, axis):
    loc = g.shape[1:]
    return jnp.moveaxis(g, 0, axis).reshape(loc[:axis] + (N_DEV * loc[axis],) + loc[axis + 1:])


def _shard_major(full, axis):
    s = full.shape
    return jnp.moveaxis(full.reshape(s[:axis] + (N_DEV, s[axis] // N_DEV) + s[axis + 1:]), axis, 0)


def _my_block(full, axis, me):
    n = full.shape[axis] // N_DEV
    return lax.dynamic_slice_in_dim(full, me * n, n, axis)


def _pack(arrays):
    flat = [a.reshape(-1) for a in arrays]
    sizes = [f.shape[0] for f in flat]
    total = sum(sizes)
    padded = -(-total // (8 * LANE)) * (8 * LANE)
    flat.append(jnp.zeros((padded - total,), f32))
    offs = [sum(sizes[:i]) for i in range(len(sizes))]
    return jnp.concatenate(flat).reshape(padded // LANE, LANE), offs


def _pad_rows(w, n):
    return jnp.concatenate([w, jnp.zeros((n - w.shape[0],) + w.shape[1:], w.dtype)], 0)


def _gate_rows(bg):
    return bg[:, :16].reshape(bg.shape[0] // CHUNK, CHUNK, 16).transpose(0, 2, 1)


def _rows(vec, n):
    m = vec.reshape(n, 1, -1)
    return [m[i] for i in range(n)]


def kernel(x, c, ctx, c_ctx, ada_w, ada_b, ln_g, ln_b, even_w_in, even_w_out, gdn_conv_w, gdn_a_log, gdn_dt_bias, gdn_norm_w, pool_w, pool_scale, odd_w_in, odd_w_out, sconv_w, conf_conv_w, conf_ln_g, conf_ln_b, ffn_w_up, ffn_conv_w, ffn_w_down, loss_target, m_c_ctx, m_ada_w, m_ada_b, m_ln_g, m_ln_b, m_even_w_in, m_even_w_out, m_gdn_conv_w, m_gdn_a_log, m_gdn_dt_bias, m_gdn_norm_w, m_pool_w, m_pool_scale, m_odd_w_in, m_odd_w_out, m_sconv_w, m_conf_conv_w, m_conf_ln_g, m_conf_ln_b, m_ffn_w_up, m_ffn_conv_w, m_ffn_w_down, v_c_ctx, v_ada_w, v_ada_b, v_ln_g, v_ln_b, v_even_w_in, v_even_w_out, v_gdn_conv_w, v_gdn_a_log, v_gdn_dt_bias, v_gdn_norm_w, v_pool_w, v_pool_scale, v_odd_w_in, v_odd_w_out, v_sconv_w, v_conf_conv_w, v_conf_ln_g, v_conf_ln_b, v_ffn_w_up, v_ffn_conv_w, v_ffn_w_down):
    weights = dict(c_ctx=c_ctx, ada_w=ada_w, ada_b=ada_b, ln_g=ln_g, ln_b=ln_b, even_w_in=even_w_in, even_w_out=even_w_out, gdn_conv_w=gdn_conv_w, gdn_a_log=gdn_a_log, gdn_dt_bias=gdn_dt_bias, gdn_norm_w=gdn_norm_w, pool_w=pool_w, pool_scale=pool_scale, odd_w_in=odd_w_in, odd_w_out=odd_w_out, sconv_w=sconv_w, conf_conv_w=conf_conv_w, conf_ln_g=conf_ln_g, conf_ln_b=conf_ln_b, ffn_w_up=ffn_w_up, ffn_conv_w=ffn_conv_w, ffn_w_down=ffn_w_down)
    mom1 = dict(c_ctx=m_c_ctx, ada_w=m_ada_w, ada_b=m_ada_b, ln_g=m_ln_g, ln_b=m_ln_b, even_w_in=m_even_w_in, even_w_out=m_even_w_out, gdn_conv_w=m_gdn_conv_w, gdn_a_log=m_gdn_a_log, gdn_dt_bias=m_gdn_dt_bias, gdn_norm_w=m_gdn_norm_w, pool_w=m_pool_w, pool_scale=m_pool_scale, odd_w_in=m_odd_w_in, odd_w_out=m_odd_w_out, sconv_w=m_sconv_w, conf_conv_w=m_conf_conv_w, conf_ln_g=m_conf_ln_g, conf_ln_b=m_conf_ln_b, ffn_w_up=m_ffn_w_up, ffn_conv_w=m_ffn_conv_w, ffn_w_down=m_ffn_w_down)
    mom2 = dict(c_ctx=v_c_ctx, ada_w=v_ada_w, ada_b=v_ada_b, ln_g=v_ln_g, ln_b=v_ln_b, even_w_in=v_even_w_in, even_w_out=v_even_w_out, gdn_conv_w=v_gdn_conv_w, gdn_a_log=v_gdn_a_log, gdn_dt_bias=v_gdn_dt_bias, gdn_norm_w=v_gdn_norm_w, pool_w=v_pool_w, pool_scale=v_pool_scale, odd_w_in=v_odd_w_in, odd_w_out=v_odd_w_out, sconv_w=v_sconv_w, conf_conv_w=v_conf_conv_w, conf_ln_g=v_conf_ln_g, conf_ln_b=v_conf_ln_b, ffn_w_up=v_ffn_w_up, ffn_conv_w=v_ffn_conv_w, ffn_w_down=v_ffn_w_down)
    order = list(weights)
    me = 4 * lax.axis_index("x") + 2 * lax.axis_index("y") + lax.axis_index("c")
    x, ctx, target = x[0], ctx[0], loss_target[0]
    t, d = x.shape
    tc = ctx.shape[0]

    small_in = [ln_g, ln_b, gdn_conv_w, sconv_w, conf_conv_w, ffn_conv_w, c]
    small_axes = [2, 2, 1, 1, 1, 3, 0]
    small_pack, small_offs = _pack(small_in)
    gath = exchange([even_w_in.astype(bf16), small_pack], False, "gather_first")
    e_in = even_w_in.shape[1] * N_DEV
    e_pad = -(-e_in // LANE) * LANE
    win_e = jnp.pad(_unshard(gath[0], 1), ((0, 0), (0, e_pad - e_in)))
    sm = gath[1].reshape(N_DEV, -1)
    lng_f, lnb_f, gconv_f, sconv_f, cconv_f, fconv_f, c_all = [
        _unshard(sm[:, o:o + a.size].reshape((N_DEV,) + a.shape), ax) for a, o, ax in zip(small_in, small_offs, small_axes)]
    gw8 = _pad_rows(gconv_f, 8)
    sw8 = _pad_rows(sconv_f, 8)
    cw32 = _pad_rows(cconv_f, 32)
    fw16 = [_pad_rows(fconv_f[l].reshape(9, D_FF), 16) for l in range(DEPTH)]

    a_raw = jnp.concatenate([c_all, c_ctx[None], jnp.zeros((7, d), f32)], 0)
    ncol = ada_w.shape[2]
    ada_b_loc = lax.dynamic_slice_in_dim(ada_b, me * ncol, ncol, 1)[:, None, :]
    modpart = ada_forward(a_raw, ada_w, ada_b_loc, "ada_forward")
    mod_send = jnp.stack([jnp.transpose(modpart[:, :N_DEV], (1, 0, 2)),
                          jnp.broadcast_to(modpart[:, N_DEV][None], (N_DEV, DEPTH, ncol))], axis=2)
    mod_recv, token = exchange([mod_send], True, "scatter_mod")
    wire_l0 = [even_w_out.astype(bf16) + token[0, 0].astype(bf16), ffn_w_up[0].astype(bf16), ffn_w_down[0].astype(bf16)]
    gather_l0, token = exchange_start(wire_l0, False, "gather_l0_start")
    wire_l1 = [odd_w_in.astype(bf16) + token[0, 0].astype(bf16), odd_w_out.astype(bf16), ffn_w_up[1].astype(bf16),
               ffn_w_down[1].astype(bf16)]
    gather_l1, token = exchange_start(wire_l1, False, "gather_l1_start")
    mod_recv = mod_recv + token[0, 0]
    mod = jnp.transpose(mod_recv[:, :, 0, :], (1, 0, 2)).reshape(DEPTH, 6 * d)
    modc = mod_recv[:, 0, 1, :].reshape(6 * d)
    sh_c, sc_c = modc[None, :d], modc[None, d:2 * d]
    mods = [_rows(mod[l], 6) for l in range(DEPTH)]
    lng = [[lng_f[l, j][None] for j in range(2)] for l in range(DEPTH)]
    lnb = [[lnb_f[l, j][None] for j in range(2)] for l in range(DEPTH)]

    neg_a = jnp.zeros((1, LANE), f32).at[0, 8:16].set(-jnp.exp(gdn_a_log).reshape(8))
    dt_row = jnp.zeros((1, LANE), f32).at[0, 8:16].set(gdn_dt_bias.reshape(8))
    nw_row, ps_row = gdn_norm_w[None], pool_scale[None]
    cg_row, cb_row = conf_ln_g[None], conf_ln_b[None]
    q_scale = GDN_DK ** -0.5

    sh_m, sc_m, gt_m, sh_f, sc_f, gt_f = mods[0]
    u0 = modulate(x, sc_m, sh_m, "mod_l0_mix")
    cu = modulate(ctx, sc_c, sh_c, "mod_ctx")
    p0 = matmul(u0, win_e, "nn", f32, "even_in")
    pc = matmul(cu, win_e, "nn", f32, "even_in_ctx")
    qn = gdn_conv(p0, gw8, 0, 4, q_scale, "gdn_conv_q")
    kn = gdn_conv(p0, gw8, 4, 4, 1.0, "gdn_conv_k")
    vv = gdn_conv(p0, gw8, 8, 4, None, "gdn_conv_v")
    kc = gdn_conv(pc, gw8, 4, 4, 1.0, "gdn_conv_k_ctx")
    vc = gdn_conv(pc, gw8, 8, 4, None, "gdn_conv_v_ctx")
    bg = gdn_gates(p0, neg_a, dt_row, "gdn_gates")
    bgc = gdn_gates(pc, neg_a, dt_row, "gdn_gates_ctx")
    bgt, bgtc = _gate_rows(bg), _gate_rows(bgc)
    zero_state = jnp.zeros((2, GDN_HEADS, LANE, LANE), f32)
    _, _, saved_c, sfin_c = gdn_forward(kc, kc, vc, bgc, bgtc, zero_state, False, "gdn_fwd_ctx")
    o_f, o_b, saved, _ = gdn_forward(qn, kn, vv, bg, bgt, sfin_c, True, "gdn_fwd")
    mix0 = jnp.concatenate([gated_rmsnorm(o_f, o_b, p0, nw_row, "gated_rmsnorm"),
                            pool_mix(p0, pool_w, ps_row, "pool_mix")], 1)
    sent, landed, _ = exchange_wait(gather_l0, mix0, "gather_l0_wait")
    full = place_own(landed, sent, False, me)
    wout_e, wup, wdown = _unshard(full[0], 0), [_unshard(full[1], 1)], [_unshard(full[2], 0)]
    y0 = matmul(mix0, wout_e, "nn", f32, "even_out")
    x1 = res_layernorm(x, y0, gt_m, lng[0][0], lnb[0][0], "resln_l0_mix")
    u1 = modulate(x1, sc_f, sh_f, "mod_l0_ffn")
    h0 = matmul(u1, wup[0], "nn", f32, "ffn_up_l0")
    f0 = ffn_conv(h0, fw16[0], "ffn_conv_l0")
    y0f = matmul(f0, wdown[0], "nn", f32, "ffn_down_l0")
    x2 = res_layernorm(x1, y0f, gt_f, lng[0][1], lnb[0][1], "resln_l0_ffn")

    sh_m1, sc_m1, gt_m1, sh_f1, sc_f1, gt_f1 = mods[1]
    sent, landed, _ = exchange_wait(gather_l1, x2, "gather_l1_wait")
    full = place_own(landed, sent, False, me)
    win_o, wout_o = _unshard(full[0], 1), _unshard(full[1], 0)
    wup.append(_unshard(full[2], 1))
    wdown.append(_unshard(full[3], 0))
    u2 = modulate(x2, sc_m1, sh_m1, "mod_l1_mix")
    p1 = matmul(u2, win_o, "nn", f32, "odd_in")
    zc = conf_conv(p1, cw32, "conf_conv")
    mix1 = jnp.concatenate([short_conv(p1, sw8, "short_conv"), ln_silu(zc, cg_row, cb_row, "conf_ln_silu")], 1)
    y1 = matmul(mix1, wout_o, "nn", f32, "odd_out")
    x3 = res_layernorm(x2, y1, gt_m1, lng[1][0], lnb[1][0], "resln_l1_mix")
    u3 = modulate(x3, sc_f1, sh_f1, "mod_l1_ffn")
    h1 = matmul(u3, wup[1], "nn", f32, "ffn_up_l1")
    f1 = ffn_conv(h1, fw16[1], "ffn_conv_l1")
    y1f = matmul(f1, wdown[1], "nn", f32, "ffn_down_l1")
    x4 = res_layernorm(x3, y1f, gt_f1, lng[1][1], lnb[1][1], "resln_l1_ffn")

    loss_row, dx4 = loss_head(x4, target, "loss_head")
    loss = lax.psum(loss_row[0, 0], ("x", "y", "c"))

    def ffn_backward(dout, x_in, y, gate, g_row, scale, u, h, f, l):
        dxr, dy, dgt, dlg, dlb = res_layernorm_bwd(dout, x_in, y, gate, g_row, f"resln_bwd_l{l}_ffn")
        df = matmul(dy, wdown[l], "nt", f32, f"ffn_down_dgrad_l{l}")
        g_down = matmul(f, dy, "tn", bf16, f"ffn_down_wgrad_l{l}")
        dh, dcw = ffn_conv_bwd(h, fw16[l], df, f"ffn_conv_bwd_l{l}")
        du = matmul(dh, wup[l], "nt", f32, f"ffn_up_dgrad_l{l}")
        g_up = matmul(u, dh, "tn", bf16, f"ffn_up_wgrad_l{l}")
        dx_in, dsc, dsh = modulate_bwd(du, x_in, scale, dxr, f"mod_bwd_l{l}_ffn")
        return dx_in, (dsh, dsc, dgt), (dlg, dlb), dcw, g_up, g_down

    dx3, dmod_f1, dln_f1, dfcw1, g_up1, g_down1 = ffn_backward(dx4, x3, y1f, gt_f1, lng[1][1], sc_f1, u3, h1, f1, 1)

    scatter_a, token = exchange_start([_shard_major(g_up1, 1), _shard_major(g_down1, 0)], True, "scatter_l1_ffn_start")
    gt_m1 = gt_m1 + token[0:1, 0:1]

    dxr, dy, dgt, dlg, dlb = res_layernorm_bwd(dx3, x2, y1, gt_m1, lng[1][0], "resln_bwd_l1_mix")
    dln_m1 = (dlg, dlb)
    dmix = matmul(dy, wout_o, "nt", f32, "odd_out_dgrad")
    g_wout_o = matmul(mix1, dy, "tn", bf16, "odd_out_wgrad")
    dgb, dgc, dhh, d_sconv = short_conv_bwd(p1, sw8, dmix, "short_conv_bwd")
    dzc, d_cg, d_cb = ln_silu_bwd(zc, cg_row, cb_row, dmix, "conf_ln_silu_bwd")
    dga, dgbb, d_cconv = conf_conv_bwd(p1, cw32, dzc, "conf_conv_bwd")
    dp1 = jnp.concatenate([dgb, dgc, dhh, dga, dgbb], 1)
    du = matmul(dp1, win_o, "nt", f32, "odd_in_dgrad")
    g_win_o = matmul(u2, dp1, "tn", bf16, "odd_in_wgrad")
    dx2, dsc, dsh = modulate_bwd(du, x2, sc_m1, dxr, "mod_bwd_l1_mix")
    dmod_m1 = (dsh, dsc, dgt)

    dx1, dmod_f0, dln_f0, dfcw0, g_up0, g_down0 = ffn_backward(dx2, x1, y0f, gt_f, lng[0][1], sc_f, u1, h0, f0, 0)

    scatter_b, token = exchange_start(
        [_shard_major(g_win_o, 1), _shard_major(g_wout_o, 0), _shard_major(g_up0, 1), _shard_major(g_down0, 0)],
        True, "scatter_mid_start")
    gt_m = gt_m + token[0:1, 0:1]

    dxr, dy, dgt, dlg, dlb = res_layernorm_bwd(dx1, x, y0, gt_m, lng[0][0], "resln_bwd_l0_mix")
    dln_m0 = (dlg, dlb)
    dmix = matmul(dy, wout_e, "nt", f32, "even_out_dgrad")
    g_wout_e = matmul(mix0, dy, "tn", bf16, "even_out_wgrad")
    d_o, dgate, d_nw = gated_rmsnorm_bwd(o_f, o_b, p0, nw_row, dmix, "gated_rmsnorm_bwd")
    dpool, d_pw, d_ps = pool_mix_bwd(p0, pool_w, ps_row, dmix, "pool_mix_bwd")
    dq_f, dq_b, dk_f, dk_b, dv_f, dv_b, dbg_f, dbg_b, ds0 = gdn_backward(
        qn, kn, vv, bg, bgt, saved, d_o, zero_state, True, "gdn_bwd")
    _, _, dkc_f, dkc_b, dvc_f, dvc_b, dbgc_f, dbgc_b, _ = gdn_backward(
        kc, kc, vc, bgc, bgtc, saved_c, jnp.zeros((tc, 512), f32), ds0, False, "gdn_bwd_ctx")
    dqp, dwq = gdn_conv_bwd(p0, gw8, dq_f, dq_b, 0, 4, q_scale, "gdn_conv_q_bwd")
    dkp, dwk = gdn_conv_bwd(p0, gw8, dk_f, dk_b, 4, 4, 1.0, "gdn_conv_k_bwd")
    dvp, dwv = gdn_conv_bwd(p0, gw8, dv_f, dv_b, 8, 4, None, "gdn_conv_v_bwd")
    dkcp, dwkc = gdn_conv_bwd(pc, gw8, dkc_f, dkc_b, 4, 4, 1.0, "gdn_conv_k_ctx_bwd")
    dvcp, dwvc = gdn_conv_bwd(pc, gw8, dvc_f, dvc_b, 8, 4, None, "gdn_conv_v_ctx_bwd")
    ds_l, da_l, ddt_l = gdn_gates_bwd(p0, neg_a, dt_row, dbg_f, dbg_b, "gdn_gates_bwd")
    ds_c, da_c, ddt_c = gdn_gates_bwd(pc, neg_a, dt_row, dbgc_f, dbgc_b, "gdn_gates_ctx_bwd")
    zc512 = jnp.zeros((tc, 512), bf16)
    dp_all = jnp.concatenate([
        jnp.concatenate([dqp, dkp, dvp, dgate, dpool, ds_l], 1),
        jnp.concatenate([zc512, dkcp, dvcp, zc512, zc512, ds_c], 1)], 0)
    u_all = jnp.concatenate([u0, cu], 0)
    du_all = matmul(dp_all, win_e, "nt", f32, "even_in_dgrad")
    g_win_e = matmul(u_all, dp_all, "tn", bf16, "even_in_wgrad")[:, :e_in]
    scatter_c, token = exchange_start([_shard_major(g_win_e, 1), _shard_major(g_wout_e, 0)], True, "scatter_last_start")
    grad_x, dsc, dsh = modulate_bwd(du_all, x, sc_m + token[0:1, 0:1], dxr, "mod_bwd_l0_mix")
    dmod_m0 = (dsh, dsc, dgt)
    _, dsc_c, dsh_c = modulate_bwd(du_all, ctx, sc_c, jnp.zeros((tc, d), f32), "mod_bwd_ctx", du_row0=t)

    grads, delta, new_m, new_v = {}, {}, {}, {}

    def update(n, parts, w, m, v):
        cols = w.shape[-1]
        out = adamw(parts.reshape(parts.shape[0], -1, cols), w.reshape(-1, cols), m.reshape(-1, cols), v.reshape(-1, cols), f"adamw_{n}")
        return [a.reshape(w.shape) for a in out]

    sent, landed, _ = exchange_wait(scatter_a, grad_x, "scatter_l1_ffn_wait")
    recv_a = place_own(landed, sent, True, me)
    sent, landed, _ = exchange_wait(scatter_b, grad_x, "scatter_mid_wait")
    recv_b = place_own(landed, sent, True, me)
    for n, parts in (("odd_w_in", recv_b[0]), ("odd_w_out", recv_b[1])):
        grads[n], delta[n], new_m[n], new_v[n] = update(n, parts, weights[n], mom1[n], mom2[n])
    for n, per_layer in (("ffn_w_up", (recv_b[2], recv_a[0])), ("ffn_w_down", (recv_b[3], recv_a[1]))):
        outs = [update(f"{n}_l{l}", per_layer[l], weights[n][l], mom1[n][l], mom2[n][l]) for l in range(DEPTH)]
        grads[n], delta[n], new_m[n], new_v[n] = (jnp.stack([outs[l][j] for l in range(DEPTH)]) for j in range(4))
    sent, landed, token = exchange_wait(scatter_c, new_v["ffn_w_down"], "scatter_last_wait")
    recv_c = place_own(landed, sent, True, me)
    for n, parts in (("even_w_in", recv_c[0]), ("even_w_out", recv_c[1])):
        grads[n], delta[n], new_m[n], new_v[n] = update(n, parts, weights[n], mom1[n], mom2[n])

    dmod0 = jnp.concatenate(dmod_m0 + dmod_f0, 1)
    dmod1 = jnp.concatenate(dmod_m1 + dmod_f1, 1)
    dmodc = jnp.concatenate([dsh_c, dsc_c], 1)
    d_gconv = jnp.concatenate([dwq, dwk + dwkc, dwv + dwvc], 1)[:5]
    small_g = [dmod0, dmod1, dmodc,
               jnp.concatenate([dln_m0[0], dln_f0[0], dln_m1[0], dln_f1[0]], 0),
               jnp.concatenate([dln_m0[1], dln_f0[1], dln_m1[1], dln_f1[1]], 0),
               d_gconv, (da_l + da_c)[0, 8:16], (ddt_l + ddt_c)[0, 8:16], d_nw, d_pw, d_ps,
               d_sconv[:3], d_cconv[:31], d_cg, d_cb, jnp.stack([dfcw0[:9], dfcw1[:9]])]
    gpack, goffs = _pack(small_g)
    gparts = exchange([gpack + token[0:1]], False, "gather_small_grads")[0]
    gsum = sum_parts(gparts, "sum_small_grads").reshape(-1)
    gs = [gsum[o:o + a.size].reshape(a.shape) for a, o in zip(small_g, goffs)]
    gflat = gparts.reshape(N_DEV, -1)
    dmodc_cols = _my_block(jnp.pad(gs[2], ((0, 0), (0, 4 * d))), 1, me)
    dm = jnp.stack([
        jnp.concatenate([_my_block(gflat[:, goffs[0]:goffs[0] + 6 * d], 1, me), dmodc_cols, jnp.zeros((7, ncol), f32)], 0),
        jnp.concatenate([_my_block(gflat[:, goffs[1]:goffs[1] + 6 * d], 1, me), jnp.zeros((8, ncol), f32)], 0)])
    g_ada_w, dcc = ada_backward(a_raw, ada_w, dm, "ada_backward")
    g_cctx = cctx_grad(exchange([dcc], False, "gather_cctx")[0], c_ctx[None], "cctx_grad")

    grads["c_ctx"] = g_cctx.reshape(c_ctx.shape)
    grads["ada_b"] = jnp.concatenate([gs[0] + jnp.pad(gs[2], ((0, 0), (0, 4 * d))), gs[1]], 0)
    grads["ln_g"] = _my_block(gs[3].reshape(DEPTH, 2, d), 2, me)
    grads["ln_b"] = _my_block(gs[4].reshape(DEPTH, 2, d), 2, me)
    grads["gdn_conv_w"] = _my_block(gs[5], 1, me)
    grads["gdn_a_log"] = gs[6].reshape(2, GDN_HEADS)
    grads["gdn_dt_bias"] = gs[7].reshape(2, GDN_HEADS)
    grads["gdn_norm_w"] = gs[8].reshape(LANE)
    grads["pool_w"] = gs[9]
    grads["pool_scale"] = gs[10].reshape(-1)
    grads["sconv_w"] = _my_block(gs[11], 1, me)
    grads["conf_conv_w"] = _my_block(gs[12], 1, me)
    grads["conf_ln_g"] = gs[13].reshape(-1)
    grads["conf_ln_b"] = gs[14].reshape(-1)
    grads["ffn_conv_w"] = _my_block(gs[15].reshape(DEPTH, 3, 3, D_FF), 3, me)

    def as2d(a):
        return a.reshape(-1, a.shape[-1]) if a.ndim > 1 else a.reshape(1, -1)

    small_names = [n for n in order if n in grads and n not in delta]
    res = adamw_small([(as2d(grads[n]), as2d(weights[n]), as2d(mom1[n]), as2d(mom2[n])) for n in small_names], "adamw_small")
    for n, (dl, nm, nv) in zip(small_names, res):
        delta[n], new_m[n], new_v[n] = (a.reshape(weights[n].shape) for a in (dl, nm, nv))
    grads["ada_w"], delta["ada_w"], new_m["ada_w"], new_v["ada_w"] = update("ada_w", g_ada_w[None], ada_w, m_ada_w, v_ada_w)

    return (loss, grad_x[None], *[grads[n] for n in order], *[delta[n] for n in order],
            *[new_m[n] for n in order], *[new_v[n] for n in order])
```

```python
import functools
import math

import jax
import jax.numpy as jnp
from jax import lax
from jax.experimental import pallas as pl
from jax.experimental.pallas import tpu as pltpu

f32 = jnp.float32
bf16 = jnp.bfloat16
SDS = jax.ShapeDtypeStruct

N_DEV = 8
D_MODEL = 1024
DEPTH = 2
GRID_W = 64
GDN_HEADS = 4
GDN_DK = 128
CHUNK = 64
POOL_WINDOWS = (2, 4, 8, 16)
D_FF = 2816
ALPHA = (2 * DEPTH) ** 0.25
LN_EPS = 1e-5
RMS_EPS = 1e-6
LANE = 128
PAD_ROWS = 72
CONV_ROWS = 256
VMEM_LIMIT = 56 * 2**20

ADAM_LR, ADAM_B1, ADAM_B2, ADAM_EPS, ADAM_WD, ADAM_STEP = 0.001, 0.9, 0.999, 1e-08, 0.01, 10

HI = lax.Precision.HIGHEST


def _cparams(sem=None):
    return pltpu.CompilerParams(dimension_semantics=sem, vmem_limit_bytes=VMEM_LIMIT)


def _silu(x):
    return x * jax.nn.sigmoid(x)


def _dsilu(x):
    s = jax.nn.sigmoid(x)
    return s * (1.0 + x * (1.0 - s))


def _dotb(a, b, dims=(((1,), (0,)), ((), ()))):
    return lax.dot_general(a.astype(bf16), b.astype(bf16), dims, preferred_element_type=f32)


def _dotb_nt(a, b):
    return _dotb(a, b, (((1,), (1,)), ((), ())))


def _dotb_tn(a, b):
    return _dotb(a, b, (((0,), (0,)), ((), ())))


def _dotf(a, b, dims=(((1,), (0,)), ((), ()))):
    return lax.dot_general(a, b, dims, preferred_element_type=f32, precision=HI)


def _pick(n, cands):
    for c in cands:
        if n % c == 0:
            return c
    return n


def matmul(a, b, mode, out_dtype, name, init=None):
    if mode == "nn":
        (M, K), N = a.shape, b.shape[1]
    elif mode == "nt":
        (M, K), N = a.shape, b.shape[0]
    else:
        (K, M), N = a.shape, b.shape[1]
    tm = _pick(M, (1024, 768, 512, 256, 128)) if mode != "tn" else _pick(M, (1024, 1408, 512, 256, 128))
    tn = _pick(N, (1024, 1408, 896, 768, 640, 512, 384, 256, 128))
    tk = _pick(K, (1024, 1408, 896, 768, 640, 512, 384, 256, 128)) if mode != "tn" else _pick(K, (1024, 512, 256))
    nk = K // tk
    dims = {"nn": (((1,), (0,)), ((), ())), "nt": (((1,), (1,)), ((), ())), "tn": (((0,), (0,)), ((), ()))}[mode]

    def body(a_ref, b_ref, *rest):
        o_ref, acc_ref = rest[-2:]
        k = pl.program_id(2)
        part = lax.dot_general(a_ref[...].astype(bf16), b_ref[...].astype(bf16), dims, preferred_element_type=f32)

        @pl.when(k == 0)
        def _():
            acc_ref[...] = part if init is None else part + rest[0][...]

        @pl.when(k > 0)
        def _():
            acc_ref[...] += part

        @pl.when(k == nk - 1)
        def _():
            o_ref[...] = acc_ref[...].astype(out_dtype)

    a_spec = {"nn": pl.BlockSpec((tm, tk), lambda i, j, k: (i, k)),
              "nt": pl.BlockSpec((tm, tk), lambda i, j, k: (i, k)),
              "tn": pl.BlockSpec((tk, tm), lambda i, j, k: (k, i))}[mode]
    b_spec = {"nn": pl.BlockSpec((tk, tn), lambda i, j, k: (k, j)),
              "nt": pl.BlockSpec((tn, tk), lambda i, j, k: (j, k)),
              "tn": pl.BlockSpec((tk, tn), lambda i, j, k: (k, j))}[mode]
    o_spec = pl.BlockSpec((tm, tn), lambda i, j, k: (i, j))
    return pl.pallas_call(
        body, out_shape=SDS((M, N), out_dtype), grid=(M // tm, N // tn, nk),
        in_specs=[a_spec, b_spec] + ([] if init is None else [o_spec]), out_specs=o_spec,
        scratch_shapes=[pltpu.VMEM((tm, tn), f32)], name=name,
        compiler_params=_cparams(("parallel", "parallel", "arbitrary")),
    )(*((a, b) if init is None else (a, b, init)))


def _row_tile(t):
    return _pick(t, (512, 256, 128, 64, 32, 16, 8))


def _row_spec(tt, d):
    return pl.BlockSpec((tt, d), lambda i: (i, 0))


def _vec_spec(d):
    return pl.BlockSpec((1, d), lambda i: (0, 0))


def _acc_rows(ref, val):
    @pl.when(pl.program_id(0) == 0)
    def _():
        ref[...] = val

    @pl.when(pl.program_id(0) > 0)
    def _():
        ref[...] += val


def modulate(x, scale, shift, name):
    t, d = x.shape
    tt = _row_tile(t)

    def body(x_ref, sc_ref, sh_ref, o_ref):
        o_ref[...] = (x_ref[...] * (1.0 + sc_ref[...]) + sh_ref[...]).astype(bf16)

    return pl.pallas_call(
        body, out_shape=SDS((t, d), bf16), grid=(t // tt,),
        in_specs=[_row_spec(tt, d), _vec_spec(d), _vec_spec(d)], out_specs=_row_spec(tt, d),
        name=name, compiler_params=_cparams(("parallel",)),
    )(x, scale, shift)


def modulate_bwd(du, x, scale, dres, name, du_row0=0):
    t, d = x.shape
    tt = _row_tile(t)
    blk0 = du_row0 // tt

    def body(du_ref, x_ref, sc_ref, dres_ref, dx_ref, dsc_ref, dsh_ref):
        du_v = du_ref[...]
        dx_ref[...] = du_v * (1.0 + sc_ref[...]) + dres_ref[...]
        _acc_rows(dsc_ref, jnp.sum(du_v * x_ref[...], axis=0, keepdims=True))
        _acc_rows(dsh_ref, jnp.sum(du_v, axis=0, keepdims=True))

    return pl.pallas_call(
        body, out_shape=(SDS((t, d), f32), SDS((1, d), f32), SDS((1, d), f32)), grid=(t // tt,),
        in_specs=[pl.BlockSpec((tt, d), lambda i: (i + blk0, 0)), _row_spec(tt, d), _vec_spec(d), _row_spec(tt, d)],
        out_specs=(_row_spec(tt, d), _vec_spec(d), _vec_spec(d)),
        name=name, compiler_params=_cparams(("arbitrary",)),
    )(du, x, scale, dres)


def _ln_stats(z):
    mu = jnp.mean(z, axis=-1, keepdims=True)
    zc = z - mu
    var = jnp.mean(zc * zc, axis=-1, keepdims=True)
    rstd = lax.rsqrt(var + LN_EPS)
    return zc * rstd, rstd


def _ln_bwd(dxhat, xhat, rstd):
    m1 = jnp.mean(dxhat, axis=-1, keepdims=True)
    m2 = jnp.mean(dxhat * xhat, axis=-1, keepdims=True)
    return rstd * (dxhat - m1 - xhat * m2)


def res_layernorm(x, y, gate, g, b, name):
    t, d = x.shape
    tt = _row_tile(t)

    def body(x_ref, y_ref, gt_ref, g_ref, b_ref, o_ref):
        xhat, _ = _ln_stats(ALPHA * x_ref[...] + gt_ref[...] * y_ref[...])
        o_ref[...] = xhat * g_ref[...] + b_ref[...]

    return pl.pallas_call(
        body, out_shape=SDS((t, d), f32), grid=(t // tt,),
        in_specs=[_row_spec(tt, d), _row_spec(tt, d), _vec_spec(d), _vec_spec(d), _vec_spec(d)],
        out_specs=_row_spec(tt, d), name=name, compiler_params=_cparams(("parallel",)),
    )(x, y, gate, g, b)


def res_layernorm_bwd(dout, x, y, gate, g, name):
    t, d = x.shape
    tt = _row_tile(t)

    def body(do_ref, x_ref, y_ref, gt_ref, g_ref, dxr_ref, dy_ref, dgt_ref, dg_ref, db_ref):
        y_v = y_ref[...]
        do_v = do_ref[...]
        xhat, rstd = _ln_stats(ALPHA * x_ref[...] + gt_ref[...] * y_v)
        dz = _ln_bwd(do_v * g_ref[...], xhat, rstd)
        dxr_ref[...] = ALPHA * dz
        dy_ref[...] = (gt_ref[...] * dz).astype(bf16)
        _acc_rows(dgt_ref, jnp.sum(dz * y_v, axis=0, keepdims=True))
        _acc_rows(dg_ref, jnp.sum(do_v * xhat, axis=0, keepdims=True))
        _acc_rows(db_ref, jnp.sum(do_v, axis=0, keepdims=True))

    vec = SDS((1, d), f32)
    return pl.pallas_call(
        body, out_shape=(SDS((t, d), f32), SDS((t, d), bf16), vec, vec, vec), grid=(t // tt,),
        in_specs=[_row_spec(tt, d), _row_spec(tt, d), _row_spec(tt, d), _vec_spec(d), _vec_spec(d)],
        out_specs=(_row_spec(tt, d), _row_spec(tt, d), _vec_spec(d), _vec_spec(d), _vec_spec(d)),
        name=name, compiler_params=_cparams(("arbitrary",)),
    )(dout, x, y, gate, g)


def loss_head(y, target, name):
    t, d = y.shape
    tt = _row_tile(t)

    def body(y_ref, t_ref, l_ref, dy_ref):
        e = y_ref[...] - t_ref[...]
        dy_ref[...] = e * (1.0 / d)
        part = jnp.sum(jnp.sum(e * e, axis=1, keepdims=True), axis=0, keepdims=True) * (0.5 / d)
        _acc_rows(l_ref, jnp.broadcast_to(part, (1, LANE)))

    return pl.pallas_call(
        body, out_shape=(SDS((1, LANE), f32), SDS((t, d), f32)), grid=(t // tt,),
        in_specs=[_row_spec(tt, d), _row_spec(tt, d)],
        out_specs=(pl.BlockSpec((1, LANE), lambda i: (0, 0)), _row_spec(tt, d)),
        name=name, compiler_params=_cparams(("arbitrary",)),
    )(y, target)


def _fill_pad(pad_ref, val, t):
    zeros = jnp.zeros((PAD_ROWS, LANE), f32)
    pad_ref[0:PAD_ROWS, :] = zeros
    pad_ref[PAD_ROWS + t:2 * PAD_ROWS + t, :] = zeros
    pad_ref[PAD_ROWS:PAD_ROWS + t, :] = val


def _grid_pads_set(pads, r0, val):
    rows = val.shape[0]
    col = (lax.broadcasted_iota(jnp.int32, (rows, 1), 0) + r0) % GRID_W
    base = PAD_ROWS + r0
    pads[0][base + 1:base + 1 + rows, :] = val * (col <= GRID_W - 2).astype(f32)
    pads[1][base:base + rows, :] = val
    pads[2][base - 1:base - 1 + rows, :] = val * (col >= 1).astype(f32)


def _grid_pads_clear_edges(pads, t):
    zeros = jnp.zeros((PAD_ROWS + 8, LANE), f32)
    for p in pads:
        p[0:PAD_ROWS + 8, :] = zeros
        p[PAD_ROWS + t - 8:2 * PAD_ROWS + t, :] = zeros


def _tap_source(pads, dc):
    return pads if dc is None else pads[dc + 1]


def _taps_apply(pads, w_ref, taps, r0, rows):
    acc = jnp.zeros((rows, LANE), f32)
    for off, dc, wi in taps:
        xs = _tap_source(pads, dc)[PAD_ROWS + r0 + off:PAD_ROWS + r0 + off + rows, :]
        acc = acc + w_ref[wi:wi + 1, :] * xs
    return acc


def _taps_wgrad(pads, dy, taps, r0, rows, nw):
    out = jnp.zeros((nw, LANE), f32)
    rid = lax.broadcasted_iota(jnp.int32, (nw, 1), 0)
    for off, dc, wi in taps:
        xs = _tap_source(pads, dc)[PAD_ROWS + r0 + off:PAD_ROWS + r0 + off + rows, :]
        s = jnp.sum(dy * xs, axis=0, keepdims=True)
        out = out + jnp.where(rid == wi, s, 0.0)
    return out


def _transpose_taps(taps):
    return [(-off, None if dc is None else -dc, wi) for off, dc, wi in taps]


def _taps_1d(width):
    return [(j - width // 2, None, j) for j in range(width)]


def _taps_grid3():
    return [(GRID_W * dr, dc, 3 * (dr + 1) + (dc + 1)) for dr in (-1, 0, 1) for dc in (-1, 0, 1)]


def _row_chunks(t):
    r = min(CONV_ROWS, t)
    return [(i * r, r) for i in range(t // r)]


def _col_spec(t, off):
    return pl.BlockSpec((t, LANE), lambda c: (0, c + off))


def _w_spec(nw, off=0):
    return pl.BlockSpec((nw, LANE), lambda c: (0, c + off))


def gdn_conv(p, w, col0, nblk, norm_scale, name):
    t = p.shape[0]
    nw = w.shape[0]
    taps = _taps_1d(5)

    def body(p_ref, w_ref, o_ref, pad_ref):
        _fill_pad(pad_ref, p_ref[...], t)
        for r0, rows in _row_chunks(t):
            a = _silu(_taps_apply(pad_ref, w_ref, taps, r0, rows))
            if norm_scale is not None:
                a = a * (lax.rsqrt(jnp.sum(a * a, axis=-1, keepdims=True) + RMS_EPS) * norm_scale)
            o_ref[r0:r0 + rows, :] = a

    return pl.pallas_call(
        body, out_shape=SDS((t, nblk * LANE), f32), grid=(nblk,),
        in_specs=[_col_spec(t, col0), _w_spec(nw, col0)], out_specs=_col_spec(t, 0),
        scratch_shapes=[pltpu.VMEM((t + 2 * PAD_ROWS, LANE), f32)], name=name,
        compiler_params=_cparams(("parallel",)),
    )(p, w)


def gdn_conv_bwd(p, w, d_a, d_b, col0, nblk, norm_scale, name):
    t = p.shape[0]
    nw = w.shape[0]
    taps = _taps_1d(5)
    ttaps = _transpose_taps(taps)

    def body(p_ref, w_ref, da_ref, db_ref, dp_ref, dw_ref, pad_ref, gpad_ref):
        _fill_pad(pad_ref, p_ref[...], t)
        for r0, rows in _row_chunks(t):
            pre = _taps_apply(pad_ref, w_ref, taps, r0, rows)
            a = _silu(pre)
            dy = da_ref[r0:r0 + rows, :] + db_ref[r0:r0 + rows, :]
            if norm_scale is not None:
                r = lax.rsqrt(jnp.sum(a * a, axis=-1, keepdims=True) + RMS_EPS)
                da = norm_scale * (dy * r - a * (r * r * r) * jnp.sum(dy * a, axis=-1, keepdims=True))
            else:
                da = dy
            gpad_ref[PAD_ROWS + r0:PAD_ROWS + r0 + rows, :] = da * _dsilu(pre)
        zeros = jnp.zeros((PAD_ROWS, LANE), f32)
        gpad_ref[0:PAD_ROWS, :] = zeros
        gpad_ref[PAD_ROWS + t:2 * PAD_ROWS + t, :] = zeros
        dw = jnp.zeros((nw, LANE), f32)
        for r0, rows in _row_chunks(t):
            dp_ref[r0:r0 + rows, :] = _taps_apply(gpad_ref, w_ref, ttaps, r0, rows).astype(bf16)
            dw = dw + _taps_wgrad(pad_ref, gpad_ref[PAD_ROWS + r0:PAD_ROWS + r0 + rows, :], taps, r0, rows, nw)
        dw_ref[...] = dw

    return pl.pallas_call(
        body, out_shape=(SDS((t, nblk * LANE), bf16), SDS((nw, nblk * LANE), f32)), grid=(nblk,),
        in_specs=[_col_spec(t, col0), _w_spec(nw, col0), _col_spec(t, 0), _col_spec(t, 0)],
        out_specs=(_col_spec(t, 0), _w_spec(nw)),
        scratch_shapes=[pltpu.VMEM((t + 2 * PAD_ROWS, LANE), f32)] * 2, name=name,
        compiler_params=_cparams(("parallel",)),
    )(p, w, d_a, d_b)


def short_conv(p, w, name):
    t = p.shape[0]
    nw = w.shape[0]
    taps = _taps_1d(3)

    def body(gb_ref, gc_ref, h_ref, w_ref, o_ref, pad_ref):
        _fill_pad(pad_ref, gc_ref[...] * h_ref[...], t)
        for r0, rows in _row_chunks(t):
            o_ref[r0:r0 + rows, :] = (gb_ref[r0:r0 + rows, :] * _taps_apply(pad_ref, w_ref, taps, r0, rows)).astype(bf16)

    return pl.pallas_call(
        body, out_shape=SDS((t, 4 * LANE), bf16), grid=(4,),
        in_specs=[_col_spec(t, 0), _col_spec(t, 4), _col_spec(t, 8), _w_spec(nw)], out_specs=_col_spec(t, 0),
        scratch_shapes=[pltpu.VMEM((t + 2 * PAD_ROWS, LANE), f32)], name=name,
        compiler_params=_cparams(("parallel",)),
    )(p, p, p, w)


def short_conv_bwd(p, w, dy, name):
    t = p.shape[0]
    nw = w.shape[0]
    taps = _taps_1d(3)
    ttaps = _transpose_taps(taps)

    def body(gb_ref, gc_ref, h_ref, w_ref, dy_ref, dgb_ref, dgc_ref, dh_ref, dw_ref, pad_ref, gpad_ref):
        _fill_pad(pad_ref, gc_ref[...] * h_ref[...], t)
        _fill_pad(gpad_ref, dy_ref[...] * gb_ref[...], t)
        dw = jnp.zeros((nw, LANE), f32)
        for r0, rows in _row_chunks(t):
            sl = slice(r0, r0 + rows)
            dgb_ref[sl, :] = (dy_ref[sl, :] * _taps_apply(pad_ref, w_ref, taps, r0, rows)).astype(bf16)
            dm = _taps_apply(gpad_ref, w_ref, ttaps, r0, rows)
            dgc_ref[sl, :] = (dm * h_ref[sl, :]).astype(bf16)
            dh_ref[sl, :] = (dm * gc_ref[sl, :]).astype(bf16)
            dw = dw + _taps_wgrad(pad_ref, gpad_ref[PAD_ROWS + r0:PAD_ROWS + r0 + rows, :], taps, r0, rows, nw)
        dw_ref[...] = dw

    blk = SDS((t, 4 * LANE), bf16)
    return pl.pallas_call(
        body, out_shape=(blk, blk, blk, SDS((nw, 4 * LANE), f32)), grid=(4,),
        in_specs=[_col_spec(t, 0), _col_spec(t, 4), _col_spec(t, 8), _w_spec(nw), _col_spec(t, 0)],
        out_specs=(_col_spec(t, 0), _col_spec(t, 0), _col_spec(t, 0), _w_spec(nw)),
        scratch_shapes=[pltpu.VMEM((t + 2 * PAD_ROWS, LANE), f32)] * 2, name=name,
        compiler_params=_cparams(("parallel",)),
    )(p, p, p, w, dy)


def conf_conv(p, w, name):
    t = p.shape[0]
    nw = w.shape[0]
    taps = _taps_1d(31)

    def body(a_ref, b_ref, w_ref, o_ref, pad_ref):
        _fill_pad(pad_ref, a_ref[...] * jax.nn.sigmoid(b_ref[...]), t)
        for r0, rows in _row_chunks(t):
            o_ref[r0:r0 + rows, :] = _taps_apply(pad_ref, w_ref, taps, r0, rows)

    return pl.pallas_call(
        body, out_shape=SDS((t, 4 * LANE), f32), grid=(4,),
        in_specs=[_col_spec(t, 12), _col_spec(t, 16), _w_spec(nw)], out_specs=_col_spec(t, 0),
        scratch_shapes=[pltpu.VMEM((t + 2 * PAD_ROWS, LANE), f32)], name=name,
        compiler_params=_cparams(("parallel",)),
    )(p, p, w)


def conf_conv_bwd(p, w, dz, name):
    t = p.shape[0]
    nw = w.shape[0]
    taps = _taps_1d(31)
    ttaps = _transpose_taps(taps)

    def body(a_ref, b_ref, w_ref, dz_ref, da_ref, db_ref, dw_ref, pad_ref, gpad_ref):
        _fill_pad(pad_ref, a_ref[...] * jax.nn.sigmoid(b_ref[...]), t)
        _fill_pad(gpad_ref, dz_ref[...], t)
        dw = jnp.zeros((nw, LANE), f32)
        for r0, rows in _row_chunks(t):
            sl = slice(r0, r0 + rows)
            dm = _taps_apply(gpad_ref, w_ref, ttaps, r0, rows)
            sg = jax.nn.sigmoid(b_ref[sl, :])
            da_ref[sl, :] = (dm * sg).astype(bf16)
            db_ref[sl, :] = (dm * a_ref[sl, :] * sg * (1.0 - sg)).astype(bf16)
            dw = dw + _taps_wgrad(pad_ref, dz_ref[sl, :], taps, r0, rows, nw)
        dw_ref[...] = dw

    blk = SDS((t, 4 * LANE), bf16)
    return pl.pallas_call(
        body, out_shape=(blk, blk, SDS((nw, 4 * LANE), f32)), grid=(4,),
        in_specs=[_col_spec(t, 12), _col_spec(t, 16), _w_spec(nw), _col_spec(t, 0)],
        out_specs=(_col_spec(t, 0), _col_spec(t, 0), _w_spec(nw)),
        scratch_shapes=[pltpu.VMEM((t + 2 * PAD_ROWS, LANE), f32)] * 2, name=name,
        compiler_params=_cparams(("parallel",)),
    )(p, p, w, dz)


def ffn_conv(h, w, name):
    t = h.shape[0]
    nblk = D_FF // LANE
    nw = w.shape[0]
    taps = _taps_grid3()

    def body(a_ref, g_ref, w_ref, o_ref, *pads):
        _grid_pads_clear_edges(pads, t)
        for r0, rows in _row_chunks(t):
            _grid_pads_set(pads, r0, a_ref[r0:r0 + rows, :])
        for r0, rows in _row_chunks(t):
            o_ref[r0:r0 + rows, :] = (_silu(_taps_apply(pads, w_ref, taps, r0, rows)) * g_ref[r0:r0 + rows, :]).astype(bf16)

    return pl.pallas_call(
        body, out_shape=SDS((t, D_FF), bf16), grid=(nblk,),
        in_specs=[_col_spec(t, 0), _col_spec(t, nblk), _w_spec(nw)], out_specs=_col_spec(t, 0),
        scratch_shapes=[pltpu.VMEM((t + 2 * PAD_ROWS, LANE), f32)] * 3, name=name,
        compiler_params=_cparams(("parallel",)),
    )(h, h, w)


def ffn_conv_bwd(h, w, df, name):
    t = h.shape[0]
    nblk = D_FF // LANE
    nw = w.shape[0]
    taps = _taps_grid3()
    ttaps = _transpose_taps(taps)

    def body(a_ref, g_ref, w_ref, df_ref, dh_ref, dw_ref, pre_ref, *all_pads):
        half = pl.program_id(1)
        pads, gpads = all_pads[:3], all_pads[3:]

        @pl.when(half == 0)
        def _():
            _grid_pads_clear_edges(all_pads, t)
            for r0, rows in _row_chunks(t):
                _grid_pads_set(pads, r0, a_ref[r0:r0 + rows, :])
            for r0, rows in _row_chunks(t):
                sl = slice(r0, r0 + rows)
                pre = _taps_apply(pads, w_ref, taps, r0, rows)
                pre_ref[sl, :] = pre
                _grid_pads_set(gpads, r0, df_ref[sl, :] * g_ref[sl, :] * _dsilu(pre))
            dw = jnp.zeros((nw, LANE), f32)
            for r0, rows in _row_chunks(t):
                dh_ref[r0:r0 + rows, :] = _taps_apply(gpads, w_ref, ttaps, r0, rows).astype(bf16)
                dw = dw + _taps_wgrad(pads, gpads[1][PAD_ROWS + r0:PAD_ROWS + r0 + rows, :], taps, r0, rows, nw)
            dw_ref[...] = dw

        @pl.when(half == 1)
        def _():
            for r0, rows in _row_chunks(t):
                sl = slice(r0, r0 + rows)
                dh_ref[sl, :] = (df_ref[sl, :] * _silu(pre_ref[sl, :])).astype(bf16)

    cspec = lambda off: pl.BlockSpec((t, LANE), lambda c, s: (0, c + off))
    return pl.pallas_call(
        body, out_shape=(SDS((t, 2 * D_FF), bf16), SDS((nw, D_FF), f32)), grid=(nblk, 2),
        in_specs=[cspec(0), cspec(nblk), pl.BlockSpec((nw, LANE), lambda c, s: (0, c)), cspec(0)],
        out_specs=(pl.BlockSpec((t, LANE), lambda c, s: (0, c + nblk * s)), pl.BlockSpec((nw, LANE), lambda c, s: (0, c))),
        scratch_shapes=[pltpu.VMEM((t, LANE), f32)] + [pltpu.VMEM((t + 2 * PAD_ROWS, LANE), f32)] * 6, name=name,
        compiler_params=_cparams(("parallel", "arbitrary")),
    )(h, h, w, df)


def _pool_count(r0, rows, win, t):
    pos = lax.broadcasted_iota(jnp.int32, (rows, 1), 0) + r0
    lo = jnp.clip(pos - win // 2, 0, t)
    hi = jnp.clip(pos - win // 2 + win, 0, t)
    return (hi - lo).astype(f32)


def _window_sum(pad_ref, r0, rows, lo, hi):
    acc = jnp.zeros((rows, LANE), f32)
    for off in range(lo, hi):
        acc = acc + pad_ref[PAD_ROWS + r0 + off:PAD_ROWS + r0 + off + rows, :]
    return acc


def pool_mix(p, pool_w, pool_scale, name):
    t = p.shape[0]

    def body(x_ref, w_ref, s_ref, o_ref, pad_ref):
        for gi, win in enumerate(POOL_WINDOWS):
            cs = slice(gi * LANE, (gi + 1) * LANE)
            _fill_pad(pad_ref, x_ref[:, cs], t)
            wg = w_ref[gi].astype(bf16)
            for r0, rows in _row_chunks(t):
                pooled = _window_sum(pad_ref, r0, rows, -(win // 2), win - win // 2) / _pool_count(r0, rows, win, t) - x_ref[r0:r0 + rows, cs]
                o_ref[r0:r0 + rows, cs] = (_dotb(pooled, wg) * s_ref[:, cs]).astype(bf16)

    return pl.pallas_call(
        body, out_shape=SDS((t, 512), bf16), grid=(1,),
        in_specs=[pl.BlockSpec((t, 512), lambda i: (0, 4)), pl.BlockSpec((4, LANE, LANE), lambda i: (0, 0, 0)),
                  pl.BlockSpec((1, 512), lambda i: (0, 0))],
        out_specs=pl.BlockSpec((t, 512), lambda i: (0, 0)),
        scratch_shapes=[pltpu.VMEM((t + 2 * PAD_ROWS, LANE), f32)], name=name,
        compiler_params=_cparams(("arbitrary",)),
    )(p, pool_w, pool_scale)


def pool_mix_bwd(p, pool_w, pool_scale, dmix, name):
    t = p.shape[0]

    def body(x_ref, w_ref, s_ref, dy_ref, dp_ref, dw_ref, ds_ref, pad_ref, gpad_ref, dpool_ref):
        for gi, win in enumerate(POOL_WINDOWS):
            cs = slice(gi * LANE, (gi + 1) * LANE)
            h = win // 2
            _fill_pad(pad_ref, x_ref[:, cs], t)
            wg = w_ref[gi].astype(bf16)
            dw = jnp.zeros((LANE, LANE), f32)
            ds = jnp.zeros((1, LANE), f32)
            zeros = jnp.zeros((PAD_ROWS, LANE), f32)
            gpad_ref[0:PAD_ROWS, :] = zeros
            gpad_ref[PAD_ROWS + t:2 * PAD_ROWS + t, :] = zeros
            for r0, rows in _row_chunks(t):
                cnt = _pool_count(r0, rows, win, t)
                pooled = _window_sum(pad_ref, r0, rows, -h, win - h) / cnt - x_ref[r0:r0 + rows, cs]
                dy = dy_ref[r0:r0 + rows, cs]
                ds = ds + jnp.sum(dy * _dotb(pooled, wg), axis=0, keepdims=True)
                dypre = dy * s_ref[:, cs]
                dw = dw + _dotb_tn(pooled, dypre)
                dpooled = _dotb_nt(dypre, wg)
                gpad_ref[PAD_ROWS + r0:PAD_ROWS + r0 + rows, :] = dpooled / cnt
                dpool_ref[r0:r0 + rows, :] = dpooled
            dw_ref[gi] = dw
            ds_ref[:, cs] = ds
            for r0, rows in _row_chunks(t):
                dx = _window_sum(gpad_ref, r0, rows, -h + 1, h + 1) - dpool_ref[r0:r0 + rows, :]
                dp_ref[r0:r0 + rows, cs] = dx.astype(bf16)

    return pl.pallas_call(
        body, out_shape=(SDS((t, 512), bf16), SDS((4, LANE, LANE), f32), SDS((1, 512), f32)), grid=(1,),
        in_specs=[pl.BlockSpec((t, 512), lambda i: (0, 4)), pl.BlockSpec((4, LANE, LANE), lambda i: (0, 0, 0)),
                  pl.BlockSpec((1, 512), lambda i: (0, 0)), pl.BlockSpec((t, 512), lambda i: (0, 1))],
        out_specs=(pl.BlockSpec((t, 512), lambda i: (0, 0)), pl.BlockSpec((4, LANE, LANE), lambda i: (0, 0, 0)),
                   pl.BlockSpec((1, 512), lambda i: (0, 0))),
        scratch_shapes=[pltpu.VMEM((t + 2 * PAD_ROWS, LANE), f32)] * 2 + [pltpu.VMEM((t, LANE), f32)], name=name,
        compiler_params=_cparams(("arbitrary",)),
    )(p, pool_w, pool_scale, dmix)


def gated_rmsnorm(o_a, o_b, p, norm_w, name):
    t = o_a.shape[0]
    tt = _row_tile(t)

    def body(oa_ref, ob_ref, g_ref, nw_ref, y_ref):
        for h in range(GDN_HEADS):
            cs = slice(h * LANE, (h + 1) * LANE)
            o = oa_ref[:, cs] + ob_ref[:, cs]
            r = lax.rsqrt(jnp.mean(o * o, axis=-1, keepdims=True) + RMS_EPS)
            y_ref[:, cs] = (o * r * nw_ref[...] * _silu(g_ref[:, cs])).astype(bf16)

    return pl.pallas_call(
        body, out_shape=SDS((t, 512), bf16), grid=(t // tt,),
        in_specs=[_row_spec(tt, 512), _row_spec(tt, 512), pl.BlockSpec((tt, 512), lambda i: (i, 3)), _vec_spec(LANE)],
        out_specs=_row_spec(tt, 512), name=name, compiler_params=_cparams(("parallel",)),
    )(o_a, o_b, p, norm_w)


def gated_rmsnorm_bwd(o_a, o_b, p, norm_w, dmix, name):
    t = o_a.shape[0]
    tt = _row_tile(t)

    def body(oa_ref, ob_ref, g_ref, nw_ref, dy_ref, do_ref, dg_ref, dnw_ref):
        dnw = jnp.zeros((1, LANE), f32)
        for h in range(GDN_HEADS):
            cs = slice(h * LANE, (h + 1) * LANE)
            o = oa_ref[:, cs] + ob_ref[:, cs]
            r = lax.rsqrt(jnp.mean(o * o, axis=-1, keepdims=True) + RMS_EPS)
            gate = g_ref[:, cs]
            dy = dy_ref[:, cs]
            dy1 = dy * _silu(gate)
            dg_ref[:, cs] = (dy * (o * r * nw_ref[...]) * _dsilu(gate)).astype(bf16)
            dnw = dnw + jnp.sum(dy1 * o * r, axis=0, keepdims=True)
            dn = dy1 * nw_ref[...]
            do_ref[:, cs] = r * dn - o * (r * r * r) * jnp.mean(dn * o, axis=-1, keepdims=True)
        _acc_rows(dnw_ref, dnw)

    return pl.pallas_call(
        body, out_shape=(SDS((t, 512), f32), SDS((t, 512), bf16), SDS((1, LANE), f32)), grid=(t // tt,),
        in_specs=[_row_spec(tt, 512), _row_spec(tt, 512), pl.BlockSpec((tt, 512), lambda i: (i, 3)), _vec_spec(LANE),
                  _row_spec(tt, 512)],
        out_specs=(_row_spec(tt, 512), _row_spec(tt, 512), _vec_spec(LANE)),
        name=name, compiler_params=_cparams(("arbitrary",)),
    )(o_a, o_b, p, norm_w, dmix)


def ln_silu(z, g, b, name):
    t, d = z.shape
    tt = _row_tile(t)

    def body(z_ref, g_ref, b_ref, o_ref):
        xhat, _ = _ln_stats(z_ref[...])
        o_ref[...] = _silu(xhat * g_ref[...] + b_ref[...]).astype(bf16)

    return pl.pallas_call(
        body, out_shape=SDS((t, d), bf16), grid=(t // tt,),
        in_specs=[_row_spec(tt, d), _vec_spec(d), _vec_spec(d)], out_specs=_row_spec(tt, d),
        name=name, compiler_params=_cparams(("parallel",)),
    )(z, g, b)


def ln_silu_bwd(z, g, b, dmix, name):
    t, d = z.shape
    tt = _row_tile(t)

    def body(z_ref, g_ref, b_ref, dy_ref, dz_ref, dg_ref, db_ref):
        xhat, rstd = _ln_stats(z_ref[...])
        dn = dy_ref[...] * _dsilu(xhat * g_ref[...] + b_ref[...])
        dz_ref[...] = _ln_bwd(dn * g_ref[...], xhat, rstd)
        _acc_rows(dg_ref, jnp.sum(dn * xhat, axis=0, keepdims=True))
        _acc_rows(db_ref, jnp.sum(dn, axis=0, keepdims=True))

    return pl.pallas_call(
        body, out_shape=(SDS((t, d), f32), SDS((1, d), f32), SDS((1, d), f32)), grid=(t // tt,),
        in_specs=[_row_spec(tt, d), _vec_spec(d), _vec_spec(d), pl.BlockSpec((tt, d), lambda i: (i, 1))],
        out_specs=(_row_spec(tt, d), _vec_spec(d), _vec_spec(d)),
        name=name, compiler_params=_cparams(("arbitrary",)),
    )(z, g, b, dmix)


def gdn_gates(p, neg_a, dt_bias, name):
    t = p.shape[0]
    tt = _row_tile(t)

    def body(s_ref, na_ref, dt_ref, o_ref):
        s = s_ref[...]
        col = lax.broadcasted_iota(jnp.int32, s.shape, 1)
        o_ref[...] = jnp.where(col < 8, jax.nn.sigmoid(s), na_ref[...] * jax.nn.softplus(s + dt_ref[...]))

    return pl.pallas_call(
        body, out_shape=SDS((t, LANE), f32), grid=(t // tt,),
        in_specs=[pl.BlockSpec((tt, LANE), lambda i: (i, 20)), _vec_spec(LANE), _vec_spec(LANE)],
        out_specs=_row_spec(tt, LANE), name=name, compiler_params=_cparams(("parallel",)),
    )(p, neg_a, dt_bias)


def gdn_gates_bwd(p, neg_a, dt_bias, dbg_a, dbg_b, name):
    t = p.shape[0]
    tt = _row_tile(t)

    def body(s_ref, na_ref, dt_ref, d_ref, d2_ref, ds_ref, da_ref, ddt_ref):
        s = s_ref[...]
        d = d_ref[...] + d2_ref[...]
        col = lax.broadcasted_iota(jnp.int32, s.shape, 1)
        sg = jax.nn.sigmoid(s)
        z = s + dt_ref[...]
        dz = jnp.where((col >= 8) & (col < 16), d * na_ref[...] * jax.nn.sigmoid(z), 0.0)
        ds_ref[...] = jnp.where(col < 8, d * sg * (1.0 - sg), dz).astype(bf16)
        dalog = jnp.where((col >= 8) & (col < 16), d * na_ref[...] * jax.nn.softplus(z), 0.0)
        _acc_rows(da_ref, jnp.sum(dalog, axis=0, keepdims=True))
        _acc_rows(ddt_ref, jnp.sum(dz, axis=0, keepdims=True))

    return pl.pallas_call(
        body, out_shape=(SDS((t, LANE), bf16), SDS((1, LANE), f32), SDS((1, LANE), f32)), grid=(t // tt,),
        in_specs=[pl.BlockSpec((tt, LANE), lambda i: (i, 20)), _vec_spec(LANE), _vec_spec(LANE), _row_spec(tt, LANE),
                  _row_spec(tt, LANE)],
        out_specs=(_row_spec(tt, LANE), _vec_spec(LANE), _vec_spec(LANE)),
        name=name, compiler_params=_cparams(("arbitrary",)),
    )(p, neg_a, dt_bias, dbg_a, dbg_b)


N_SCAN = 2 * GDN_HEADS


def _bdot(a, b, ca, cb, precision=None):
    if precision is None:
        a, b = a.astype(bf16), b.astype(bf16)
    return lax.dot_general(a, b, (((ca,), (cb,)), ((0,), (0,))), preferred_element_type=f32, precision=precision)


def _bdot_nn(a, b, precision=None):
    return _bdot(a, b, 2, 1, precision)


def _bdot_nt(a, b):
    return _bdot(a, b, 2, 2)


def _bdot_tn(a, b, precision=None):
    return _bdot(a, b, 1, 1, precision)


def _order_masks():
    shape = (N_SCAN, CHUNK, CHUNK)
    sign = jnp.where(lax.broadcasted_iota(jnp.int32, shape, 0) >= GDN_HEADS, -1, 1)
    ahead = (lax.broadcasted_iota(jnp.int32, shape, 1) - lax.broadcasted_iota(jnp.int32, shape, 2)) * sign
    lower, strict, lower_t = ahead >= 0, ahead > 0, ahead <= 0
    col_shape = (N_SCAN, CHUNK, 1)
    back1 = lax.broadcasted_iota(jnp.int32, col_shape, 0) >= GDN_HEADS
    row1 = lax.broadcasted_iota(jnp.int32, col_shape, 1)
    at_last = (row1 == jnp.where(back1, 0, CHUNK - 1)).astype(f32)
    return lower, strict, lower_t, at_last


def _stack_heads(f_ref, b_ref):
    return jnp.stack([ref[:, h * LANE:(h + 1) * LANE] for ref in (f_ref, b_ref) for h in range(GDN_HEADS)])


def _stack_gates(bgf, bgb, bgtf, bgtb):
    beta = jnp.stack([bg[:, 4 * d + h:4 * d + h + 1] for d, bg in enumerate((bgf, bgb)) for h in range(GDN_HEADS)])
    g_col = jnp.stack([bg[:, 8 + 4 * d + h:9 + 4 * d + h] for d, bg in enumerate((bgf, bgb)) for h in range(GDN_HEADS)])
    g_row = jnp.stack([bgt[8 + 4 * d + h:9 + 4 * d + h, :] for d, bgt in enumerate((bgtf, bgtb)) for h in range(GDN_HEADS)])
    return beta, g_col, g_row


def _chunk_terms(k, v, beta, g_col, g_row, masks, tinv=None):
    lower, strict, lower_t, at_last = masks
    gc = jnp.sum(lower.astype(f32) * g_row, axis=2, keepdims=True)
    gr = jnp.sum(lower_t.astype(f32) * g_col, axis=1, keepdims=True)
    g_last = jnp.sum(at_last * gc, axis=1, keepdims=True)
    e = jnp.exp(gc)
    f = jnp.exp(g_last - gc)
    dm = jnp.exp(jnp.where(lower, gc - gr, -1e30))
    kb = k * beta
    kk = _bdot_nt(kb, k)
    if tinv is None:
        shape = (N_SCAN, CHUNK, CHUNK)
        eye = (lax.broadcasted_iota(jnp.int32, shape, 1) == lax.broadcasted_iota(jnp.int32, shape, 2)).astype(f32)
        pw = -jnp.where(strict, kk * dm, 0.0)
        tinv = eye + pw
        for _ in range(5):
            pw = _bdot_nn(pw, pw, lax.Precision.HIGH)
            tinv = tinv + _bdot_nn(tinv, pw, lax.Precision.HIGH)
    u = _bdot_nn(tinv, v * beta)
    w = _bdot_nn(tinv, kb * e)
    return dict(e=e, f=f, gl=jnp.exp(g_last), dm=dm, kb=kb, kk=kk, tinv=tinv, u=u, w=w, kd=k * f)


def _gdn_specs(nc, width, step_chunk):
    return [pl.BlockSpec((CHUNK, width), functools.partial(lambda i, d: (step_chunk(i, d), 0), d=d)) for d in (0, 1)]


def gdn_forward(q, k, v, bg, bgt, s0, with_out, name):
    t = k.shape[0]
    nc = t // CHUNK

    def body(qf_ref, qb_ref, kf_ref, kb_ref, vf_ref, vb_ref, bgf_ref, bgb_ref, bgtf_ref, bgtb_ref, s0_ref,
             of_ref, ob_ref, sallf_ref, sallb_ref, tinvf_ref, tinvb_ref, sfin_ref, s_ref):
        i = pl.program_id(0)

        @pl.when(i == 0)
        def _():
            s_ref[...] = s0_ref[...]

        masks = _order_masks()
        k8, v8 = _stack_heads(kf_ref, kb_ref), _stack_heads(vf_ref, vb_ref)
        beta, g_col, g_row = _stack_gates(bgf_ref[...], bgb_ref[...], bgtf_ref[0], bgtb_ref[0])
        c = _chunk_terms(k8, v8, beta, g_col, g_row, masks)
        s = s_ref[...]
        sallf_ref[0] = s[:GDN_HEADS]
        sallb_ref[0] = s[GDN_HEADS:]
        tinvf_ref[0] = c["tinv"][:GDN_HEADS]
        tinvb_ref[0] = c["tinv"][GDN_HEADS:]
        vn = c["u"] - _bdot_nn(c["w"], s)
        if with_out:
            q8 = _stack_heads(qf_ref, qb_ref)
            pm = jnp.where(masks[0], _bdot_nt(q8, k8) * c["dm"], 0.0)
            o = _bdot_nn(q8 * c["e"], s) + _bdot_nn(pm, vn)
        for d, o_ref in enumerate((of_ref, ob_ref)):
            for h in range(GDN_HEADS):
                o_ref[:, h * LANE:(h + 1) * LANE] = o[GDN_HEADS * d + h] if with_out else jnp.zeros((CHUNK, LANE), f32)
        s_ref[...] = c["gl"] * s + _bdot_tn(c["kd"], vn)

        @pl.when(i == nc - 1)
        def _():
            sfin_ref[...] = s_ref[...]

    chunk_of = lambda i, d: i if d == 0 else nc - 1 - i
    seq = _gdn_specs(nc, 512, chunk_of)
    gate = _gdn_specs(nc, LANE, chunk_of)
    gate_t = [pl.BlockSpec((1, 16, CHUNK), functools.partial(lambda i, d: (chunk_of(i, d), 0, 0), d=d)) for d in (0, 1)]
    sall = [pl.BlockSpec((1, GDN_HEADS, LANE, LANE), functools.partial(lambda i, d: (chunk_of(i, d), 0, 0, 0), d=d)) for d in (0, 1)]
    tinv = [pl.BlockSpec((1, GDN_HEADS, CHUNK, CHUNK), functools.partial(lambda i, d: (chunk_of(i, d), 0, 0, 0), d=d)) for d in (0, 1)]
    st = pl.BlockSpec((N_SCAN, LANE, LANE), lambda i: (0, 0, 0))
    o_shape, s_shape, t_shape = SDS((t, 512), f32), SDS((nc, GDN_HEADS, LANE, LANE), f32), SDS((nc, GDN_HEADS, CHUNK, CHUNK), f32)
    o_f, o_b, sall_f, sall_b, tinv_f, tinv_b, s_fin = pl.pallas_call(
        body, out_shape=(o_shape, o_shape, s_shape, s_shape, t_shape, t_shape, SDS((N_SCAN, LANE, LANE), f32)), grid=(nc,),
        in_specs=seq + seq + seq + gate + gate_t + [st], out_specs=tuple(seq + sall + tinv + [st]),
        scratch_shapes=[pltpu.VMEM((N_SCAN, LANE, LANE), f32)], name=name,
        compiler_params=_cparams(("arbitrary",)),
    )(q, q, k, k, v, v, bg, bg, bgt, bgt, s0.reshape(N_SCAN, LANE, LANE))
    return o_f, o_b, (sall_f, sall_b, tinv_f, tinv_b), s_fin.reshape(2, GDN_HEADS, LANE, LANE)


def _gdn_chunk_bwd(q, k, v, d_o, beta, g_col, g_row, s, tinv, dsn, masks):
    lower, strict, _, at_last = masks
    c = _chunk_terms(k, v, beta, g_col, g_row, masks, tinv)
    e, f, gl, dm, kb, kk, tinv, u, w, kd = (c[n] for n in ("e", "f", "gl", "dm", "kb", "kk", "tinv", "u", "w", "kd"))
    vn = u - _bdot_nn(w, s)
    ds = gl * dsn
    dgl = jnp.sum(jnp.sum(s * dsn, axis=2, keepdims=True), axis=1, keepdims=True)
    dkd = _bdot_nt(vn, dsn)
    dvn = _bdot_nn(kd, dsn)
    dm_grad = jnp.zeros((N_SCAN, CHUNK, CHUNK), f32)
    de = jnp.zeros((N_SCAN, CHUNK, 1), f32)
    dq = None
    dk = jnp.zeros((N_SCAN, CHUNK, LANE), f32)
    if q is not None:
        qk = _bdot_nt(q, k)
        pm = jnp.where(lower, qk * dm, 0.0)
        dqd = _bdot_nt(d_o, s)
        ds = ds + _bdot_tn(q * e, d_o)
        dpm = jnp.where(lower, _bdot_nt(d_o, vn), 0.0)
        dvn = dvn + _bdot_tn(pm, d_o)
        dqk = dpm * dm
        dm_grad = dm_grad + dpm * qk
        dq = _bdot_nn(dqk, k) + dqd * e
        dk = _bdot_tn(dqk, q)
        de = de + jnp.sum(dqd * q, axis=2, keepdims=True)
    dw = -_bdot_nt(dvn, s)
    ds = ds - _bdot_tn(w, dvn)
    drv = _bdot_tn(tinv, dvn)
    drk = _bdot_tn(tinv, dw)
    da = -jnp.where(strict, _bdot_nt(drv, u) + _bdot_nt(drk, w), 0.0)
    dbeta = jnp.sum(drv * v, axis=2, keepdims=True)
    dv = drv * beta
    dkb = drk * e
    de = de + jnp.sum(drk * kb, axis=2, keepdims=True)
    dkk = da * dm
    dm_grad = dm_grad + da * kk
    dkb = dkb + _bdot_nn(dkk, k)
    dk = dk + _bdot_tn(dkk, kb) + dkd * f
    df = jnp.sum(dkd * k, axis=2, keepdims=True)
    dbeta = dbeta + jnp.sum(dkb * k, axis=2, keepdims=True)
    dk = dk + dkb * beta
    m = dm_grad * dm
    ones = jnp.ones((N_SCAN, CHUNK, LANE), f32)
    rsum = jnp.sum(m, axis=2, keepdims=True)
    csum = _bdot_tn(m, ones, HI)[:, :, 0:1]
    dgl_tot = jnp.sum(df * f, axis=1, keepdims=True) + dgl * gl
    dgc = de * e - df * f + rsum - csum + at_last * dgl_tot
    dg = _bdot_tn(lower.astype(f32), dgc * ones, HI)[:, :, 0:1]
    return dq, dk, dv, dbeta, dg, ds


def gdn_backward(q, k, v, bg, bgt, saved, d_o, ds_fin, with_out, name):
    t = k.shape[0]
    nc = t // CHUNK

    def body(qf_ref, qb_ref, kf_ref, kb_ref, vf_ref, vb_ref, bgf_ref, bgb_ref, bgtf_ref, bgtb_ref,
             sallf_ref, sallb_ref, tinvf_ref, tinvb_ref, dof_ref, dob_ref, dsf_ref,
             dqf_ref, dqb_ref, dkf_ref, dkb_ref, dvf_ref, dvb_ref, dbgf_ref, dbgb_ref, ds0_ref, ds_ref):
        i = pl.program_id(0)

        @pl.when(i == 0)
        def _():
            ds_ref[...] = dsf_ref[...]

        lane = lax.broadcasted_iota(jnp.int32, (1, LANE), 1)
        masks = _order_masks()
        beta, g_col, g_row = _stack_gates(bgf_ref[...], bgb_ref[...], bgtf_ref[0], bgtb_ref[0])
        s = jnp.concatenate([sallf_ref[0], sallb_ref[0]], 0)
        tinv = jnp.concatenate([tinvf_ref[0], tinvb_ref[0]], 0)
        dq, dk, dv, dbeta, dg, ds = _gdn_chunk_bwd(
            _stack_heads(qf_ref, qb_ref) if with_out else None, _stack_heads(kf_ref, kb_ref), _stack_heads(vf_ref, vb_ref),
            _stack_heads(dof_ref, dob_ref), beta, g_col, g_row, s, tinv, ds_ref[...], masks)
        ds_ref[...] = ds
        for d, (dq_ref, dk_ref, dv_ref, dbg_ref) in enumerate(((dqf_ref, dkf_ref, dvf_ref, dbgf_ref), (dqb_ref, dkb_ref, dvb_ref, dbgb_ref))):
            dbg = jnp.zeros((CHUNK, LANE), f32)
            for h in range(GDN_HEADS):
                b = GDN_HEADS * d + h
                cs = slice(h * LANE, (h + 1) * LANE)
                dq_ref[:, cs] = dq[b] if with_out else jnp.zeros((CHUNK, LANE), f32)
                dk_ref[:, cs] = dk[b]
                dv_ref[:, cs] = dv[b]
                dbg = dbg + dbeta[b] * (lane == b).astype(f32) + dg[b] * (lane == 8 + b).astype(f32)
            dbg_ref[...] = dbg

        @pl.when(i == nc - 1)
        def _():
            ds0_ref[...] = ds_ref[...]

    chunk_of = lambda i, d: nc - 1 - i if d == 0 else i
    seq = _gdn_specs(nc, 512, chunk_of)
    gate = _gdn_specs(nc, LANE, chunk_of)
    gate_t = [pl.BlockSpec((1, 16, CHUNK), functools.partial(lambda i, d: (chunk_of(i, d), 0, 0), d=d)) for d in (0, 1)]
    sall = [pl.BlockSpec((1, GDN_HEADS, LANE, LANE), functools.partial(lambda i, d: (chunk_of(i, d), 0, 0, 0), d=d)) for d in (0, 1)]
    tinv = [pl.BlockSpec((1, GDN_HEADS, CHUNK, CHUNK), functools.partial(lambda i, d: (chunk_of(i, d), 0, 0, 0), d=d)) for d in (0, 1)]
    st = pl.BlockSpec((N_SCAN, LANE, LANE), lambda i: (0, 0, 0))
    o_shape, g_shape = SDS((t, 512), f32), SDS((t, LANE), f32)
    res = pl.pallas_call(
        body, out_shape=(o_shape,) * 6 + (g_shape, g_shape, SDS((N_SCAN, LANE, LANE), f32)), grid=(nc,),
        in_specs=seq + seq + seq + gate + gate_t + sall + tinv + seq + [st], out_specs=tuple(seq + seq + seq + gate + [st]),
        scratch_shapes=[pltpu.VMEM((N_SCAN, LANE, LANE), f32)], name=name,
        compiler_params=_cparams(("arbitrary",)),
    )(q, q, k, k, v, v, bg, bg, bgt, bgt, *saved, d_o, d_o, ds_fin.reshape(N_SCAN, LANE, LANE))
    return tuple(res[:8]) + (res[8].reshape(2, GDN_HEADS, LANE, LANE),)


def _my_position():
    x, y, c = lax.axis_index("x"), lax.axis_index("y"), lax.axis_index("c")
    return x, y, c, 4 * x + 2 * y + c


def exchange(arrays, scatter, name):
    n = len(arrays)
    shapes = [a.shape[1:] if scatter else a.shape for a in arrays]

    def body(*refs):
        ins, outs, token = refs[:n], refs[n:2 * n], refs[2 * n]
        send_sems, recv_sems, local_sems = refs[2 * n + 1:]
        x, y, c, me = _my_position()
        token[...] = jnp.zeros_like(token)
        started = []
        for a in range(n):
            mine = pltpu.make_async_copy(ins[a].at[me] if scatter else ins[a], outs[a].at[me], local_sems.at[a])
            mine.start()
            started.append(mine)
        waits = []
        for r in range(1, N_DEV):
            px = 1 - x if r & 4 else x
            py = 1 - y if r & 2 else y
            pc = 1 - c if r & 1 else c
            pid = 4 * px + 2 * py + pc
            for a in range(n):
                cp = pltpu.make_async_remote_copy(
                    src_ref=ins[a].at[pid] if scatter else ins[a], dst_ref=outs[a].at[me],
                    send_sem=send_sems.at[a, r - 1], recv_sem=recv_sems.at[a, r - 1],
                    device_id=(px, py, pc), device_id_type=pl.DeviceIdType.MESH)
                cp.start()
                arrive = pltpu.make_async_remote_copy(
                    src_ref=ins[a].at[pid] if scatter else ins[a], dst_ref=outs[a].at[pid],
                    send_sem=send_sems.at[a, r - 1], recv_sem=recv_sems.at[a, r - 1],
                    device_id=(px, py, pc), device_id_type=pl.DeviceIdType.MESH)
                waits.append((cp, arrive))
        for cp, arrive in waits:
            cp.wait_send()
            arrive.wait_recv()
        for mine in started:
            mine.wait()

    any_spec = pl.BlockSpec(memory_space=pl.ANY)
    return pl.pallas_call(
        body, out_shape=tuple(SDS((N_DEV,) + tuple(s), a.dtype) for s, a in zip(shapes, arrays)) + (SDS((8, LANE), f32),),
        in_specs=[any_spec] * n, out_specs=tuple([any_spec] * n) + (pl.BlockSpec(memory_space=pltpu.VMEM),),
        scratch_shapes=[pltpu.SemaphoreType.DMA((n, N_DEV - 1)), pltpu.SemaphoreType.DMA((n, N_DEV - 1)),
                        pltpu.SemaphoreType.DMA((n,))],
        name=name,
    )(*arrays)


_HBM_SPEC = pl.BlockSpec(memory_space=pltpu.HBM)
_SEM_SPEC = pl.BlockSpec(memory_space=pltpu.SEMAPHORE)
_DATAFLOW = pltpu.SideEffectType.DATAFLOW_SIDE_EFFECTING


def _peers(x, y, c):
    out = []
    for r in range(1, N_DEV):
        px = 1 - x if r & 4 else x
        py = 1 - y if r & 2 else y
        pc = 1 - c if r & 1 else c
        out.append((r, (px, py, pc), 4 * px + 2 * py + pc))
    return out


def _exchange_copies(ins, lands, send_sems, recv_sems, scatter, arrivals):
    x, y, c, me = _my_position()
    pairs = []
    for r, peer, pid in _peers(x, y, c):
        for a in range(len(ins)):
            k = a * (N_DEV - 1) + r - 1
            kw = dict(send_sem=send_sems.at[k], recv_sem=recv_sems.at[k], device_id=peer, device_id_type=pl.DeviceIdType.MESH)
            src = ins[a].at[pid] if scatter else ins[a]
            send = pltpu.make_async_remote_copy(src_ref=src, dst_ref=lands[a].at[me], **kw)
            arrive = pltpu.make_async_remote_copy(src_ref=src, dst_ref=lands[a].at[pid], **kw) if arrivals else None
            pairs.append((send, arrive))
    return pairs


def exchange_start(arrays, scatter, name):
    n = len(arrays)
    shapes = [a.shape[1:] if scatter else a.shape for a in arrays]

    def body(*refs):
        ins, lands = refs[:n], refs[n:2 * n]
        send_sems, recv_sems = refs[2 * n], refs[2 * n + 1]
        token = refs[-1]
        for send, _ in _exchange_copies(ins, lands, send_sems, recv_sems, scatter, False):
            send.start()
        token[...] = jnp.zeros_like(token)

    sem = pltpu.SemaphoreType.DMA((n * (N_DEV - 1),))
    land_shapes = [(N_DEV,) + tuple(s) for s in shapes]
    res = pl.pallas_call(
        body, name=name,
        out_shape=(sem, sem, *[pltpu.HBM(a.shape, a.dtype) for a in arrays],
                   *[pltpu.HBM(s, a.dtype) for s, a in zip(land_shapes, arrays)], SDS((8, LANE), f32)),
        in_specs=[_HBM_SPEC] * (2 * n),
        out_specs=(_SEM_SPEC, _SEM_SPEC, *[_HBM_SPEC] * (2 * n), pl.BlockSpec(memory_space=pltpu.VMEM)),
        input_output_aliases={i: 2 + i for i in range(2 * n)},
        compiler_params=pltpu.CompilerParams(has_side_effects=_DATAFLOW),
    )(*[pltpu.with_memory_space_constraint(a, pltpu.HBM) for a in arrays],
      *[pltpu.with_memory_space_constraint(lax.empty(s, a.dtype), pltpu.HBM) for s, a in zip(land_shapes, arrays)])
    return (res[0], res[1], list(res[2:2 + n]), list(res[2 + n:2 + 2 * n]), scatter), res[-1]


def exchange_wait(handle, after, name):
    send_sems, recv_sems, ins, lands, scatter = handle
    n = len(ins)

    def body(*refs):
        in_refs, land_refs = refs[:n], refs[n:2 * n]
        for send, arrive in _exchange_copies(in_refs, land_refs, refs[2 * n], refs[2 * n + 1], scatter, True):
            send.wait_send()
            arrive.wait_recv()
        refs[-1][...] = jnp.zeros_like(refs[-1])

    res = pl.pallas_call(
        body, name=name,
        out_shape=tuple(pltpu.HBM(a.shape, a.dtype) for a in ins + lands) + (SDS((8, LANE), f32),),
        in_specs=[_HBM_SPEC] * (2 * n) + [_SEM_SPEC, _SEM_SPEC, pl.BlockSpec(memory_space=pl.ANY)],
        out_specs=tuple([_HBM_SPEC] * (2 * n)) + (pl.BlockSpec(memory_space=pltpu.VMEM),),
        input_output_aliases={i: i for i in range(2 * n)},
        compiler_params=pltpu.CompilerParams(has_side_effects=_DATAFLOW),
    )(*ins, *lands, send_sems, recv_sems, after)
    return list(res[:n]), list(res[n:2 * n]), res[-1]


def place_own(lands, arrays, scatter, me):
    own = [lax.dynamic_index_in_dim(a, me, 0, keepdims=False) if scatter else a for a in arrays]
    return [lax.dynamic_update_index_in_dim(l, o, me, 0) for l, o in zip(lands, own)]


def ada_forward(a_raw, ada_w, ada_b_loc, name):
    def body(a_ref, w_ref, b_ref, o_ref):
        a = _silu(a_ref[...])
        for l in range(DEPTH):
            o_ref[l] = _dotf(a, w_ref[l]) + b_ref[l]

    return pl.pallas_call(body, out_shape=SDS((DEPTH, 16, ada_w.shape[2]), f32), name=name,
                          compiler_params=_cparams())(a_raw, ada_w, ada_b_loc)


def ada_backward(a_raw, ada_w, dm, name):
    def body(a_ref, w_ref, dm_ref, gw_ref, dcc_ref):
        a = _silu(a_ref[...])
        for l in range(DEPTH):
            gw_ref[l] = _dotf(a, dm_ref[l], (((0,), (0,)), ((), ())))
        dcc_ref[...] = _dotf(dm_ref[0, 8:16, :], w_ref[0], (((1,), (1,)), ((), ())))

    return pl.pallas_call(body, out_shape=(SDS(ada_w.shape, f32), SDS((8, ada_w.shape[1]), f32)), name=name,
                          compiler_params=_cparams())(a_raw, ada_w, dm)


def sum_parts(parts, name):
    _, r, c = parts.shape

    def body(p_ref, o_ref):
        acc = p_ref[0]
        for i in range(1, N_DEV):
            acc = acc + p_ref[i]
        o_ref[...] = acc

    return pl.pallas_call(body, out_shape=SDS((r, c), f32), name=name, compiler_params=_cparams())(parts)


def cctx_grad(parts, c_ctx, name):
    def body(p_ref, c_ref, o_ref):
        acc = p_ref[0, 0:1, :]
        for i in range(1, N_DEV):
            acc = acc + p_ref[i, 0:1, :]
        o_ref[...] = acc * _dsilu(c_ref[...])

    return pl.pallas_call(body, out_shape=SDS((1, c_ctx.shape[1]), f32), name=name, compiler_params=_cparams())(parts, c_ctx)


def _adamw_math(g, w, m, v):
    m = ADAM_B1 * m + (1.0 - ADAM_B1) * g
    v = ADAM_B2 * v + (1.0 - ADAM_B2) * (g * g)
    m_hat = m / (1.0 - ADAM_B1 ** ADAM_STEP)
    v_hat = v / (1.0 - ADAM_B2 ** ADAM_STEP)
    delta = -ADAM_LR * (m_hat / (jnp.sqrt(v_hat) + ADAM_EPS) + ADAM_WD * w)
    return delta, m, v


def adamw(parts, w, m, v, name):
    n, r, c = parts.shape
    tr = _pick(r, (256, 128, 64, 32, 16, 8))

    def body(p_ref, w_ref, m_ref, v_ref, g_ref, d_ref, nm_ref, nv_ref):
        g = p_ref[0].astype(f32)
        for i in range(1, n):
            g = g + p_ref[i].astype(f32)
        g_ref[...] = g
        d_ref[...], nm_ref[...], nv_ref[...] = _adamw_math(g, w_ref[...], m_ref[...], v_ref[...])

    blk = pl.BlockSpec((tr, c), lambda i: (i, 0))
    out = SDS((r, c), f32)
    return pl.pallas_call(
        body, out_shape=(out, out, out, out), grid=(r // tr,),
        in_specs=[pl.BlockSpec((n, tr, c), lambda i: (0, i, 0)), blk, blk, blk], out_specs=(blk, blk, blk, blk),
        name=name, compiler_params=_cparams(("parallel",)),
    )(parts, w, m, v)


def adamw_small(items, name):
    n = len(items)

    def body(*refs):
        ins, outs = refs[:4 * n], refs[4 * n:]
        for i in range(n):
            g, w, m, v = (ins[4 * i + j][...] for j in range(4))
            outs[3 * i][...], outs[3 * i + 1][...], outs[3 * i + 2][...] = _adamw_math(g, w, m, v)

    flat = [a for it in items for a in it]
    out_shape = tuple(SDS(it[1].shape, f32) for it in items for _ in range(3))
    res = pl.pallas_call(body, out_shape=out_shape, name=name, compiler_params=_cparams())(*flat)
    return [tuple(res[3 * i:3 * i + 3]) for i in range(n)]


def _unshard(g, axis):
    loc = g.shape[1:]
    return jnp.moveaxis(g, 0, axis).reshape(loc[:axis] + (N_DEV * loc[axis],) + loc[axis + 1:])


def _shard_major(full, axis):
    s = full.shape
    return jnp.moveaxis(full.reshape(s[:axis] + (N_DEV, s[axis] // N_DEV) + s[axis + 1:]), axis, 0)


def _my_block(full, axis, me):
    n = full.shape[axis] // N_DEV
    return lax.dynamic_slice_in_dim(full, me * n, n, axis)


def _pack(arrays):
    flat = [a.reshape(-1) for a in arrays]
    sizes = [f.shape[0] for f in flat]
    total = sum(sizes)
    padded = -(-total // (8 * LANE)) * (8 * LANE)
    flat.append(jnp.zeros((padded - total,), f32))
    offs = [sum(sizes[:i]) for i in range(len(sizes))]
    return jnp.concatenate(flat).reshape(padded // LANE, LANE), offs


def _pad_rows(w, n):
    return jnp.concatenate([w, jnp.zeros((n - w.shape[0],) + w.shape[1:], w.dtype)], 0)


def _gate_rows(bg):
    return bg[:, :16].reshape(bg.shape[0] // CHUNK, CHUNK, 16).transpose(0, 2, 1)


def _rows(vec, n):
    m = vec.reshape(n, 1, -1)
    return [m[i] for i in range(n)]


def kernel(x, c, ctx, c_ctx, ada_w, ada_b, ln_g, ln_b, even_w_in, even_w_out, gdn_conv_w, gdn_a_log, gdn_dt_bias, gdn_norm_w, pool_w, pool_scale, odd_w_in, odd_w_out, sconv_w, conf_conv_w, conf_ln_g, conf_ln_b, ffn_w_up, ffn_conv_w, ffn_w_down, loss_target, m_c_ctx, m_ada_w, m_ada_b, m_ln_g, m_ln_b, m_even_w_in, m_even_w_out, m_gdn_conv_w, m_gdn_a_log, m_gdn_dt_bias, m_gdn_norm_w, m_pool_w, m_pool_scale, m_odd_w_in, m_odd_w_out, m_sconv_w, m_conf_conv_w, m_conf_ln_g, m_conf_ln_b, m_ffn_w_up, m_ffn_conv_w, m_ffn_w_down, v_c_ctx, v_ada_w, v_ada_b, v_ln_g, v_ln_b, v_even_w_in, v_even_w_out, v_gdn_conv_w, v_gdn_a_log, v_gdn_dt_bias, v_gdn_norm_w, v_pool_w, v_pool_scale, v_odd_w_in, v_odd_w_out, v_sconv_w, v_conf_conv_w, v_conf_ln_g, v_conf_ln_b, v_ffn_w_up, v_ffn_conv_w, v_ffn_w_down):
    weights = dict(c_ctx=c_ctx, ada_w=ada_w, ada_b=ada_b, ln_g=ln_g, ln_b=ln_b, even_w_in=even_w_in, even_w_out=even_w_out, gdn_conv_w=gdn_conv_w, gdn_a_log=gdn_a_log, gdn_dt_bias=gdn_dt_bias, gdn_norm_w=gdn_norm_w, pool_w=pool_w, pool_scale=pool_scale, odd_w_in=odd_w_in, odd_w_out=odd_w_out, sconv_w=sconv_w, conf_conv_w=conf_conv_w, conf_ln_g=conf_ln_g, conf_ln_b=conf_ln_b, ffn_w_up=ffn_w_up, ffn_conv_w=ffn_conv_w, ffn_w_down=ffn_w_down)
    mom1 = dict(c_ctx=m_c_ctx, ada_w=m_ada_w, ada_b=m_ada_b, ln_g=m_ln_g, ln_b=m_ln_b, even_w_in=m_even_w_in, even_w_out=m_even_w_out, gdn_conv_w=m_gdn_conv_w, gdn_a_log=m_gdn_a_log, gdn_dt_bias=m_gdn_dt_bias, gdn_norm_w=m_gdn_norm_w, pool_w=m_pool_w, pool_scale=m_pool_scale, odd_w_in=m_odd_w_in, odd_w_out=m_odd_w_out, sconv_w=m_sconv_w, conf_conv_w=m_conf_conv_w, conf_ln_g=m_conf_ln_g, conf_ln_b=m_conf_ln_b, ffn_w_up=m_ffn_w_up, ffn_conv_w=m_ffn_conv_w, ffn_w_down=m_ffn_w_down)
    mom2 = dict(c_ctx=v_c_ctx, ada_w=v_ada_w, ada_b=v_ada_b, ln_g=v_ln_g, ln_b=v_ln_b, even_w_in=v_even_w_in, even_w_out=v_even_w_out, gdn_conv_w=v_gdn_conv_w, gdn_a_log=v_gdn_a_log, gdn_dt_bias=v_gdn_dt_bias, gdn_norm_w=v_gdn_norm_w, pool_w=v_pool_w, pool_scale=v_pool_scale, odd_w_in=v_odd_w_in, odd_w_out=v_odd_w_out, sconv_w=v_sconv_w, conf_conv_w=v_conf_conv_w, conf_ln_g=v_conf_ln_g, conf_ln_b=v_conf_ln_b, ffn_w_up=v_ffn_w_up, ffn_conv_w=v_ffn_conv_w, ffn_w_down=v_ffn_w_down)
    order = list(weights)
    me = 4 * lax.axis_index("x") + 2 * lax.axis_index("y") + lax.axis_index("c")
    x, ctx, target = x[0], ctx[0], loss_target[0]
    t, d = x.shape
    tc = ctx.shape[0]

    small_in = [ln_g, ln_b, gdn_conv_w, sconv_w, conf_conv_w, ffn_conv_w, c]
    small_axes = [2, 2, 1, 1, 1, 3, 0]
    small_pack, small_offs = _pack(small_in)
    gath = exchange([even_w_in.astype(bf16), small_pack], False, "gather_first")
    e_in = even_w_in.shape[1] * N_DEV
    e_pad = -(-e_in // LANE) * LANE
    win_e = jnp.pad(_unshard(gath[0], 1), ((0, 0), (0, e_pad - e_in)))
    sm = gath[1].reshape(N_DEV, -1)
    lng_f, lnb_f, gconv_f, sconv_f, cconv_f, fconv_f, c_all = [
        _unshard(sm[:, o:o + a.size].reshape((N_DEV,) + a.shape), ax) for a, o, ax in zip(small_in, small_offs, small_axes)]
    gw8 = _pad_rows(gconv_f, 8)
    sw8 = _pad_rows(sconv_f, 8)
    cw32 = _pad_rows(cconv_f, 32)
    fw16 = [_pad_rows(fconv_f[l].reshape(9, D_FF), 16) for l in range(DEPTH)]

    a_raw = jnp.concatenate([c_all, c_ctx[None], jnp.zeros((7, d), f32)], 0)
    ncol = ada_w.shape[2]
    ada_b_loc = lax.dynamic_slice_in_dim(ada_b, me * ncol, ncol, 1)[:, None, :]
    modpart = ada_forward(a_raw, ada_w, ada_b_loc, "ada_forward")
    mod_send = jnp.stack([jnp.transpose(modpart[:, :N_DEV], (1, 0, 2)),
                          jnp.broadcast_to(modpart[:, N_DEV][None], (N_DEV, DEPTH, ncol))], axis=2)
    mod_recv, token = exchange([mod_send], True, "scatter_mod")
    wire_l0 = [even_w_out.astype(bf16) + token[0, 0].astype(bf16), ffn_w_up[0].astype(bf16), ffn_w_down[0].astype(bf16)]
    gather_l0, token = exchange_start(wire_l0, False, "gather_l0_start")
    wire_l1 = [odd_w_in.astype(bf16) + token[0, 0].astype(bf16), odd_w_out.astype(bf16), ffn_w_up[1].astype(bf16),
               ffn_w_down[1].astype(bf16)]
    gather_l1, token = exchange_start(wire_l1, False, "gather_l1_start")
    mod_recv = mod_recv + token[0, 0]
    mod = jnp.transpose(mod_recv[:, :, 0, :], (1, 0, 2)).reshape(DEPTH, 6 * d)
    modc = mod_recv[:, 0, 1, :].reshape(6 * d)
    sh_c, sc_c = modc[None, :d], modc[None, d:2 * d]
    mods = [_rows(mod[l], 6) for l in range(DEPTH)]
    lng = [[lng_f[l, j][None] for j in range(2)] for l in range(DEPTH)]
    lnb = [[lnb_f[l, j][None] for j in range(2)] for l in range(DEPTH)]

    neg_a = jnp.zeros((1, LANE), f32).at[0, 8:16].set(-jnp.exp(gdn_a_log).reshape(8))
    dt_row = jnp.zeros((1, LANE), f32).at[0, 8:16].set(gdn_dt_bias.reshape(8))
    nw_row, ps_row = gdn_norm_w[None], pool_scale[None]
    cg_row, cb_row = conf_ln_g[None], conf_ln_b[None]
    q_scale = GDN_DK ** -0.5

    sh_m, sc_m, gt_m, sh_f, sc_f, gt_f = mods[0]
    u0 = modulate(x, sc_m, sh_m, "mod_l0_mix")
    cu = modulate(ctx, sc_c, sh_c, "mod_ctx")
    p0 = matmul(u0, win_e, "nn", f32, "even_in")
    pc = matmul(cu, win_e, "nn", f32, "even_in_ctx")
    qn = gdn_conv(p0, gw8, 0, 4, q_scale, "gdn_conv_q")
    kn = gdn_conv(p0, gw8, 4, 4, 1.0, "gdn_conv_k")
    vv = gdn_conv(p0, gw8, 8, 4, None, "gdn_conv_v")
    kc = gdn_conv(pc, gw8, 4, 4, 1.0, "gdn_conv_k_ctx")
    vc = gdn_conv(pc, gw8, 8, 4, None, "gdn_conv_v_ctx")
    bg = gdn_gates(p0, neg_a, dt_row, "gdn_gates")
    bgc = gdn_gates(pc, neg_a, dt_row, "gdn_gates_ctx")
    bgt, bgtc = _gate_rows(bg), _gate_rows(bgc)
    zero_state = jnp.zeros((2, GDN_HEADS, LANE, LANE), f32)
    _, _, saved_c, sfin_c = gdn_forward(kc, kc, vc, bgc, bgtc, zero_state, False, "gdn_fwd_ctx")
    o_f, o_b, saved, _ = gdn_forward(qn, kn, vv, bg, bgt, sfin_c, True, "gdn_fwd")
    mix0 = jnp.concatenate([gated_rmsnorm(o_f, o_b, p0, nw_row, "gated_rmsnorm"),
                            pool_mix(p0, pool_w, ps_row, "pool_mix")], 1)
    sent, landed, _ = exchange_wait(gather_l0, mix0, "gather_l0_wait")
    full = place_own(landed, sent, False, me)
    wout_e, wup, wdown = _unshard(full[0], 0), [_unshard(full[1], 1)], [_unshard(full[2], 0)]
    y0 = matmul(mix0, wout_e, "nn", f32, "even_out")
    x1 = res_layernorm(x, y0, gt_m, lng[0][0], lnb[0][0], "resln_l0_mix")
    u1 = modulate(x1, sc_f, sh_f, "mod_l0_ffn")
    h0 = matmul(u1, wup[0], "nn", f32, "ffn_up_l0")
    f0 = ffn_conv(h0, fw16[0], "ffn_conv_l0")
    y0f = matmul(f0, wdown[0], "nn", f32, "ffn_down_l0")
    x2 = res_layernorm(x1, y0f, gt_f, lng[0][1], lnb[0][1], "resln_l0_ffn")

    sh_m1, sc_m1, gt_m1, sh_f1, sc_f1, gt_f1 = mods[1]
    sent, landed, _ = exchange_wait(gather_l1, x2, "gather_l1_wait")
    full = place_own(landed, sent, False, me)
    win_o, wout_o = _unshard(full[0], 1), _unshard(full[1], 0)
    wup.append(_unshard(full[2], 1))
    wdown.append(_unshard(full[3], 0))
    u2 = modulate(x2, sc_m1, sh_m1, "mod_l1_mix")
    p1 = matmul(u2, win_o, "nn", f32, "odd_in")
    zc = conf_conv(p1, cw32, "conf_conv")
    mix1 = jnp.concatenate([short_conv(p1, sw8, "short_conv"), ln_silu(zc, cg_row, cb_row, "conf_ln_silu")], 1)
    y1 = matmul(mix1, wout_o, "nn", f32, "odd_out")
    x3 = res_layernorm(x2, y1, gt_m1, lng[1][0], lnb[1][0], "resln_l1_mix")
    u3 = modulate(x3, sc_f1, sh_f1, "mod_l1_ffn")
    h1 = matmul(u3, wup[1], "nn", f32, "ffn_up_l1")
    f1 = ffn_conv(h1, fw16[1], "ffn_conv_l1")
    y1f = matmul(f1, wdown[1], "nn", f32, "ffn_down_l1")
    x4 = res_layernorm(x3, y1f, gt_f1, lng[1][1], lnb[1][1], "resln_l1_ffn")

    loss_row, dx4 = loss_head(x4, target, "loss_head")
    loss = lax.psum(loss_row[0, 0], ("x", "y", "c"))

    def ffn_backward(dout, x_in, y, gate, g_row, scale, u, h, f, l):
        dxr, dy, dgt, dlg, dlb = res_layernorm_bwd(dout, x_in, y, gate, g_row, f"resln_bwd_l{l}_ffn")
        df = matmul(dy, wdown[l], "nt", f32, f"ffn_down_dgrad_l{l}")
        g_down = matmul(f, dy, "tn", bf16, f"ffn_down_wgrad_l{l}")
        dh, dcw = ffn_conv_bwd(h, fw16[l], df, f"ffn_conv_bwd_l{l}")
        du = matmul(dh, wup[l], "nt", f32, f"ffn_up_dgrad_l{l}")
        g_up = matmul(u, dh, "tn", bf16, f"ffn_up_wgrad_l{l}")
        dx_in, dsc, dsh = modulate_bwd(du, x_in, scale, dxr, f"mod_bwd_l{l}_ffn")
        return dx_in, (dsh, dsc, dgt), (dlg, dlb), dcw, g_up, g_down

    dx3, dmod_f1, dln_f1, dfcw1, g_up1, g_down1 = ffn_backward(dx4, x3, y1f, gt_f1, lng[1][1], sc_f1, u3, h1, f1, 1)

    scatter_a, token = exchange_start([_shard_major(g_up1, 1), _shard_major(g_down1, 0)], True, "scatter_l1_ffn_start")
    gt_m1 = gt_m1 + token[0:1, 0:1]

    dxr, dy, dgt, dlg, dlb = res_layernorm_bwd(dx3, x2, y1, gt_m1, lng[1][0], "resln_bwd_l1_mix")
    dln_m1 = (dlg, dlb)
    dmix = matmul(dy, wout_o, "nt", f32, "odd_out_dgrad")
    g_wout_o = matmul(mix1, dy, "tn", bf16, "odd_out_wgrad")
    dgb, dgc, dhh, d_sconv = short_conv_bwd(p1, sw8, dmix, "short_conv_bwd")
    dzc, d_cg, d_cb = ln_silu_bwd(zc, cg_row, cb_row, dmix, "conf_ln_silu_bwd")
    dga, dgbb, d_cconv = conf_conv_bwd(p1, cw32, dzc, "conf_conv_bwd")
    dp1 = jnp.concatenate([dgb, dgc, dhh, dga, dgbb], 1)
    du = matmul(dp1, win_o, "nt", f32, "odd_in_dgrad")
    g_win_o = matmul(u2, dp1, "tn", bf16, "odd_in_wgrad")
    dx2, dsc, dsh = modulate_bwd(du, x2, sc_m1, dxr, "mod_bwd_l1_mix")
    dmod_m1 = (dsh, dsc, dgt)

    dx1, dmod_f0, dln_f0, dfcw0, g_up0, g_down0 = ffn_backward(dx2, x1, y0f, gt_f, lng[0][1], sc_f, u1, h0, f0, 0)

    scatter_b, token = exchange_start(
        [_shard_major(g_win_o, 1), _shard_major(g_wout_o, 0), _shard_major(g_up0, 1), _shard_major(g_down0, 0)],
        True, "scatter_mid_start")
    gt_m = gt_m + token[0:1, 0:1]

    dxr, dy, dgt, dlg, dlb = res_layernorm_bwd(dx1, x, y0, gt_m, lng[0][0], "resln_bwd_l0_mix")
    dln_m0 = (dlg, dlb)
    dmix = matmul(dy, wout_e, "nt", f32, "even_out_dgrad")
    g_wout_e = matmul(mix0, dy, "tn", bf16, "even_out_wgrad")
    d_o, dgate, d_nw = gated_rmsnorm_bwd(o_f, o_b, p0, nw_row, dmix, "gated_rmsnorm_bwd")
    dpool, d_pw, d_ps = pool_mix_bwd(p0, pool_w, ps_row, dmix, "pool_mix_bwd")
    dq_f, dq_b, dk_f, dk_b, dv_f, dv_b, dbg_f, dbg_b, ds0 = gdn_backward(
        qn, kn, vv, bg, bgt, saved, d_o, zero_state, True, "gdn_bwd")
    _, _, dkc_f, dkc_b, dvc_f, dvc_b, dbgc_f, dbgc_b, _ = gdn_backward(
        kc, kc, vc, bgc, bgtc, saved_c, jnp.zeros((tc, 512), f32), ds0, False, "gdn_bwd_ctx")
    dqp, dwq = gdn_conv_bwd(p0, gw8, dq_f, dq_b, 0, 4, q_scale, "gdn_conv_q_bwd")
    dkp, dwk = gdn_conv_bwd(p0, gw8, dk_f, dk_b, 4, 4, 1.0, "gdn_conv_k_bwd")
    dvp, dwv = gdn_conv_bwd(p0, gw8, dv_f, dv_b, 8, 4, None, "gdn_conv_v_bwd")
    dkcp, dwkc = gdn_conv_bwd(pc, gw8, dkc_f, dkc_b, 4, 4, 1.0, "gdn_conv_k_ctx_bwd")
    dvcp, dwvc = gdn_conv_bwd(pc, gw8, dvc_f, dvc_b, 8, 4, None, "gdn_conv_v_ctx_bwd")
    ds_l, da_l, ddt_l = gdn_gates_bwd(p0, neg_a, dt_row, dbg_f, dbg_b, "gdn_gates_bwd")
    ds_c, da_c, ddt_c = gdn_gates_bwd(pc, neg_a, dt_row, dbgc_f, dbgc_b, "gdn_gates_ctx_bwd")
    zc512 = jnp.zeros((tc, 512), bf16)
    dp0 = jnp.concatenate([dqp, dkp, dvp, dgate, dpool, ds_l], 1)
    dpc = jnp.concatenate([zc512, dkcp, dvcp, zc512, zc512, ds_c], 1)
    du0 = matmul(dp0, win_e, "nt", f32, "even_in_dgrad")
    duc = matmul(dpc, win_e, "nt", f32, "even_in_ctx_dgrad")
    g_win_e = matmul(u0, dp0, "tn", bf16, "even_in_wgrad", init=matmul(cu, dpc, "tn", f32, "even_in_ctx_wgrad"))[:, :e_in]
    scatter_c, token = exchange_start([_shard_major(g_win_e, 1), _shard_major(g_wout_e, 0)], True, "scatter_last_start")
    grad_x, dsc, dsh = modulate_bwd(du0, x, sc_m + token[0:1, 0:1], dxr, "mod_bwd_l0_mix")
    dmod_m0 = (dsh, dsc, dgt)
    _, dsc_c, dsh_c = modulate_bwd(duc, ctx, sc_c, jnp.zeros((tc, d), f32), "mod_bwd_ctx")

    grads, delta, new_m, new_v = {}, {}, {}, {}

    def update(n, parts, w, m, v):
        cols = w.shape[-1]
        out = adamw(parts.reshape(parts.shape[0], -1, cols), w.reshape(-1, cols), m.reshape(-1, cols), v.reshape(-1, cols), f"adamw_{n}")
        return [a.reshape(w.shape) for a in out]

    sent, landed, _ = exchange_wait(scatter_a, grad_x, "scatter_l1_ffn_wait")
    recv_a = place_own(landed, sent, True, me)
    sent, landed, _ = exchange_wait(scatter_b, grad_x, "scatter_mid_wait")
    recv_b = place_own(landed, sent, True, me)
    for n, parts in (("odd_w_in", recv_b[0]), ("odd_w_out", recv_b[1])):
        grads[n], delta[n], new_m[n], new_v[n] = update(n, parts, weights[n], mom1[n], mom2[n])
    for n, per_layer in (("ffn_w_up", (recv_b[2], recv_a[0])), ("ffn_w_down", (recv_b[3], recv_a[1]))):
        outs = [update(f"{n}_l{l}", per_layer[l], weights[n][l], mom1[n][l], mom2[n][l]) for l in range(DEPTH)]
        grads[n], delta[n], new_m[n], new_v[n] = (jnp.stack([outs[l][j] for l in range(DEPTH)]) for j in range(4))
    sent, landed, token = exchange_wait(scatter_c, new_v["ffn_w_down"], "scatter_last_wait")
    recv_c = place_own(landed, sent, True, me)
    for n, parts in (("even_w_in", recv_c[0]), ("even_w_out", recv_c[1])):
        grads[n], delta[n], new_m[n], new_v[n] = update(n, parts, weights[n], mom1[n], mom2[n])

    dmod0 = jnp.concatenate(dmod_m0 + dmod_f0, 1)
    dmod1 = jnp.concatenate(dmod_m1 + dmod_f1, 1)
    dmodc = jnp.concatenate([dsh_c, dsc_c], 1)
    d_gconv = jnp.concatenate([dwq, dwk + dwkc, dwv + dwvc], 1)[:5]
    small_g = [dmod0, dmod1, dmodc,
               jnp.concatenate([dln_m0[0], dln_f0[0], dln_m1[0], dln_f1[0]], 0),
               jnp.concatenate([dln_m0[1], dln_f0[1], dln_m1[1], dln_f1[1]], 0),
               d_gconv, (da_l + da_c)[0, 8:16], (ddt_l + ddt_c)[0, 8:16], d_nw, d_pw, d_ps,
               d_sconv[:3], d_cconv[:31], d_cg, d_cb, jnp.stack([dfcw0[:9], dfcw1[:9]])]
    gpack, goffs = _pack(small_g)
    gparts = exchange([gpack + token[0:1]], False, "gather_small_grads")[0]
    gsum = sum_parts(gparts, "sum_small_grads").reshape(-1)
    gs = [gsum[o:o + a.size].reshape(a.shape) for a, o in zip(small_g, goffs)]
    gflat = gparts.reshape(N_DEV, -1)
    dmodc_cols = _my_block(jnp.pad(gs[2], ((0, 0), (0, 4 * d))), 1, me)
    dm = jnp.stack([
        jnp.concatenate([_my_block(gflat[:, goffs[0]:goffs[0] + 6 * d], 1, me), dmodc_cols, jnp.zeros((7, ncol), f32)], 0),
        jnp.concatenate([_my_block(gflat[:, goffs[1]:goffs[1] + 6 * d], 1, me), jnp.zeros((8, ncol), f32)], 0)])
    g_ada_w, dcc = ada_backward(a_raw, ada_w, dm, "ada_backward")
    g_cctx = cctx_grad(exchange([dcc], False, "gather_cctx")[0], c_ctx[None], "cctx_grad")

    grads["c_ctx"] = g_cctx.reshape(c_ctx.shape)
    grads["ada_b"] = jnp.concatenate([gs[0] + jnp.pad(gs[2], ((0, 0), (0, 4 * d))), gs[1]], 0)
    grads["ln_g"] = _my_block(gs[3].reshape(DEPTH, 2, d), 2, me)
    grads["ln_b"] = _my_block(gs[4].reshape(DEPTH, 2, d), 2, me)
    grads["gdn_conv_w"] = _my_block(gs[5], 1, me)
    grads["gdn_a_log"] = gs[6].reshape(2, GDN_HEADS)
    grads["gdn_dt_bias"] = gs[7].reshape(2, GDN_HEADS)
    grads["gdn_norm_w"] = gs[8].reshape(LANE)
    grads["pool_w"] = gs[9]
    grads["pool_scale"] = gs[10].reshape(-1)
    grads["sconv_w"] = _my_block(gs[11], 1, me)
    grads["conf_conv_w"] = _my_block(gs[12], 1, me)
    grads["conf_ln_g"] = gs[13].reshape(-1)
    grads["conf_ln_b"] = gs[14].reshape(-1)
    grads["ffn_conv_w"] = _my_block(gs[15].reshape(DEPTH, 3, 3, D_FF), 3, me)

    def as2d(a):
        return a.reshape(-1, a.shape[-1]) if a.ndim > 1 else a.reshape(1, -1)

    small_names = [n for n in order if n in grads and n not in delta]
    res = adamw_small([(as2d(grads[n]), as2d(weights[n]), as2d(mom1[n]), as2d(mom2[n])) for n in small_names], "adamw_small")
    for n, (dl, nm, nv) in zip(small_names, res):
        delta[n], new_m[n], new_v[n] = (a.reshape(weights[n].shape) for a in (dl, nm, nv))
    grads["ada_w"], delta["ada_w"], new_m["ada_w"], new_v["ada_w"] = update("ada_w", g_ada_w[None], ada_w, m_ada_w, v_ada_w)

    return (loss, grad_x[None], *[grads[n] for n in order], *[delta[n] for n in order],
            *[new_m[n] for n in order], *[new_v[n] for n in order])
```

```python
import functools
import math

import jax
import jax.numpy as jnp
from jax import lax
from jax.experimental import pallas as pl
from jax.experimental.pallas import tpu as pltpu

f32 = jnp.float32
bf16 = jnp.bfloat16
SDS = jax.ShapeDtypeStruct

N_DEV = 8
D_MODEL = 1024
DEPTH = 2
GRID_W = 64
GDN_HEADS = 4
GDN_DK = 128
CHUNK = 64
POOL_WINDOWS = (2, 4, 8, 16)
D_FF = 2816
ALPHA = (2 * DEPTH) ** 0.25
LN_EPS = 1e-5
RMS_EPS = 1e-6
LANE = 128
PAD_ROWS = 72
CONV_ROWS = 256
VMEM_LIMIT = 56 * 2**20

ADAM_LR, ADAM_B1, ADAM_B2, ADAM_EPS, ADAM_WD, ADAM_STEP = 0.001, 0.9, 0.999, 1e-08, 0.01, 10

HI = lax.Precision.HIGHEST


def _cparams(sem=None):
    return pltpu.CompilerParams(dimension_semantics=sem, vmem_limit_bytes=VMEM_LIMIT)


def _silu(x):
    return x * jax.nn.sigmoid(x)


def _dsilu(x):
    s = jax.nn.sigmoid(x)
    return s * (1.0 + x * (1.0 - s))


def _dotb(a, b, dims=(((1,), (0,)), ((), ()))):
    return lax.dot_general(a.astype(bf16), b.astype(bf16), dims, preferred_element_type=f32)


def _dotb_nt(a, b):
    return _dotb(a, b, (((1,), (1,)), ((), ())))


def _dotb_tn(a, b):
    return _dotb(a, b, (((0,), (0,)), ((), ())))


def _dotf(a, b, dims=(((1,), (0,)), ((), ()))):
    return lax.dot_general(a, b, dims, preferred_element_type=f32, precision=HI)


def _pick(n, cands):
    for c in cands:
        if n % c == 0:
            return c
    return n


def matmul(a, b, mode, out_dtype, name, init=None):
    if mode == "nn":
        (M, K), N = a.shape, b.shape[1]
    elif mode == "nt":
        (M, K), N = a.shape, b.shape[0]
    else:
        (K, M), N = a.shape, b.shape[1]
    tm = _pick(M, (1024, 768, 512, 256, 128)) if mode != "tn" else _pick(M, (1024, 1408, 512, 256, 128))
    tn = _pick(N, (1024, 1408, 896, 768, 640, 512, 384, 256, 128))
    tk = _pick(K, (1024, 1408, 896, 768, 640, 512, 384, 256, 128)) if mode != "tn" else _pick(K, (1024, 512, 256))
    nk = K // tk
    dims = {"nn": (((1,), (0,)), ((), ())), "nt": (((1,), (1,)), ((), ())), "tn": (((0,), (0,)), ((), ()))}[mode]

    def body(a_ref, b_ref, *rest):
        o_ref, acc_ref = rest[-2:]
        k = pl.program_id(2)
        part = lax.dot_general(a_ref[...].astype(bf16), b_ref[...].astype(bf16), dims, preferred_element_type=f32)

        @pl.when(k == 0)
        def _():
            acc_ref[...] = part if init is None else part + rest[0][...]

        @pl.when(k > 0)
        def _():
            acc_ref[...] += part

        @pl.when(k == nk - 1)
        def _():
            o_ref[...] = acc_ref[...].astype(out_dtype)

    a_spec = {"nn": pl.BlockSpec((tm, tk), lambda i, j, k: (i, k)),
              "nt": pl.BlockSpec((tm, tk), lambda i, j, k: (i, k)),
              "tn": pl.BlockSpec((tk, tm), lambda i, j, k: (k, i))}[mode]
    b_spec = {"nn": pl.BlockSpec((tk, tn), lambda i, j, k: (k, j)),
              "nt": pl.BlockSpec((tn, tk), lambda i, j, k: (j, k)),
              "tn": pl.BlockSpec((tk, tn), lambda i, j, k: (k, j))}[mode]
    o_spec = pl.BlockSpec((tm, tn), lambda i, j, k: (i, j))
    return pl.pallas_call(
        body, out_shape=SDS((M, N), out_dtype), grid=(M // tm, N // tn, nk),
        in_specs=[a_spec, b_spec] + ([] if init is None else [o_spec]), out_specs=o_spec,
        scratch_shapes=[pltpu.VMEM((tm, tn), f32)], name=name,
        compiler_params=_cparams(("parallel", "parallel", "arbitrary")),
    )(*((a, b) if init is None else (a, b, init)))


def _row_tile(t):
    return _pick(t, (512, 256, 128, 64, 32, 16, 8))


def _row_spec(tt, d):
    return pl.BlockSpec((tt, d), lambda i: (i, 0))


def _vec_spec(d):
    return pl.BlockSpec((1, d), lambda i: (0, 0))


def _acc_rows(ref, val):
    @pl.when(pl.program_id(0) == 0)
    def _():
        ref[...] = val

    @pl.when(pl.program_id(0) > 0)
    def _():
        ref[...] += val


def modulate(x, scale, shift, name):
    t, d = x.shape
    tt = _row_tile(t)

    def body(x_ref, sc_ref, sh_ref, o_ref):
        o_ref[...] = (x_ref[...] * (1.0 + sc_ref[...]) + sh_ref[...]).astype(bf16)

    return pl.pallas_call(
        body, out_shape=SDS((t, d), bf16), grid=(t // tt,),
        in_specs=[_row_spec(tt, d), _vec_spec(d), _vec_spec(d)], out_specs=_row_spec(tt, d),
        name=name, compiler_params=_cparams(("parallel",)),
    )(x, scale, shift)


def modulate_bwd(du, x, scale, dres, name, du_row0=0):
    t, d = x.shape
    tt = _row_tile(t)
    blk0 = du_row0 // tt

    def body(du_ref, x_ref, sc_ref, dres_ref, dx_ref, dsc_ref, dsh_ref):
        du_v = du_ref[...]
        dx_ref[...] = du_v * (1.0 + sc_ref[...]) + dres_ref[...]
        _acc_rows(dsc_ref, jnp.sum(du_v * x_ref[...], axis=0, keepdims=True))
        _acc_rows(dsh_ref, jnp.sum(du_v, axis=0, keepdims=True))

    return pl.pallas_call(
        body, out_shape=(SDS((t, d), f32), SDS((1, d), f32), SDS((1, d), f32)), grid=(t // tt,),
        in_specs=[pl.BlockSpec((tt, d), lambda i: (i + blk0, 0)), _row_spec(tt, d), _vec_spec(d), _row_spec(tt, d)],
        out_specs=(_row_spec(tt, d), _vec_spec(d), _vec_spec(d)),
        name=name, compiler_params=_cparams(("arbitrary",)),
    )(du, x, scale, dres)


def _ln_stats(z):
    mu = jnp.mean(z, axis=-1, keepdims=True)
    zc = z - mu
    var = jnp.mean(zc * zc, axis=-1, keepdims=True)
    rstd = lax.rsqrt(var + LN_EPS)
    return zc * rstd, rstd


def _ln_bwd(dxhat, xhat, rstd):
    m1 = jnp.mean(dxhat, axis=-1, keepdims=True)
    m2 = jnp.mean(dxhat * xhat, axis=-1, keepdims=True)
    return rstd * (dxhat - m1 - xhat * m2)


def res_layernorm(x, y, gate, g, b, name):
    t, d = x.shape
    tt = _row_tile(t)

    def body(x_ref, y_ref, gt_ref, g_ref, b_ref, o_ref):
        xhat, _ = _ln_stats(ALPHA * x_ref[...] + gt_ref[...] * y_ref[...])
        o_ref[...] = xhat * g_ref[...] + b_ref[...]

    return pl.pallas_call(
        body, out_shape=SDS((t, d), f32), grid=(t // tt,),
        in_specs=[_row_spec(tt, d), _row_spec(tt, d), _vec_spec(d), _vec_spec(d), _vec_spec(d)],
        out_specs=_row_spec(tt, d), name=name, compiler_params=_cparams(("parallel",)),
    )(x, y, gate, g, b)


def res_layernorm_bwd(dout, x, y, gate, g, name):
    t, d = x.shape
    tt = _row_tile(t)

    def body(do_ref, x_ref, y_ref, gt_ref, g_ref, dxr_ref, dy_ref, dgt_ref, dg_ref, db_ref):
        y_v = y_ref[...]
        do_v = do_ref[...]
        xhat, rstd = _ln_stats(ALPHA * x_ref[...] + gt_ref[...] * y_v)
        dz = _ln_bwd(do_v * g_ref[...], xhat, rstd)
        dxr_ref[...] = ALPHA * dz
        dy_ref[...] = (gt_ref[...] * dz).astype(bf16)
        _acc_rows(dgt_ref, jnp.sum(dz * y_v, axis=0, keepdims=True))
        _acc_rows(dg_ref, jnp.sum(do_v * xhat, axis=0, keepdims=True))
        _acc_rows(db_ref, jnp.sum(do_v, axis=0, keepdims=True))

    vec = SDS((1, d), f32)
    return pl.pallas_call(
        body, out_shape=(SDS((t, d), f32), SDS((t, d), bf16), vec, vec, vec), grid=(t // tt,),
        in_specs=[_row_spec(tt, d), _row_spec(tt, d), _row_spec(tt, d), _vec_spec(d), _vec_spec(d)],
        out_specs=(_row_spec(tt, d), _row_spec(tt, d), _vec_spec(d), _vec_spec(d), _vec_spec(d)),
        name=name, compiler_params=_cparams(("arbitrary",)),
    )(dout, x, y, gate, g)


def loss_head(y, target, name):
    t, d = y.shape
    tt = _row_tile(t)

    def body(y_ref, t_ref, l_ref, dy_ref):
        e = y_ref[...] - t_ref[...]
        dy_ref[...] = e * (1.0 / d)
        part = jnp.sum(jnp.sum(e * e, axis=1, keepdims=True), axis=0, keepdims=True) * (0.5 / d)
        _acc_rows(l_ref, jnp.broadcast_to(part, (1, LANE)))

    return pl.pallas_call(
        body, out_shape=(SDS((1, LANE), f32), SDS((t, d), f32)), grid=(t // tt,),
        in_specs=[_row_spec(tt, d), _row_spec(tt, d)],
        out_specs=(pl.BlockSpec((1, LANE), lambda i: (0, 0)), _row_spec(tt, d)),
        name=name, compiler_params=_cparams(("arbitrary",)),
    )(y, target)


def _fill_pad(pad_ref, val, t):
    zeros = jnp.zeros((PAD_ROWS, LANE), f32)
    pad_ref[0:PAD_ROWS, :] = zeros
    pad_ref[PAD_ROWS + t:2 * PAD_ROWS + t, :] = zeros
    pad_ref[PAD_ROWS:PAD_ROWS + t, :] = val


def _grid_pads_set(pads, r0, val):
    rows = val.shape[0]
    col = (lax.broadcasted_iota(jnp.int32, (rows, 1), 0) + r0) % GRID_W
    base = PAD_ROWS + r0
    pads[0][base + 1:base + 1 + rows, :] = val * (col <= GRID_W - 2).astype(f32)
    pads[1][base:base + rows, :] = val
    pads[2][base - 1:base - 1 + rows, :] = val * (col >= 1).astype(f32)


def _grid_pads_clear_edges(pads, t):
    zeros = jnp.zeros((PAD_ROWS + 8, LANE), f32)
    for p in pads:
        p[0:PAD_ROWS + 8, :] = zeros
        p[PAD_ROWS + t - 8:2 * PAD_ROWS + t, :] = zeros


def _tap_source(pads, dc):
    return pads if dc is None else pads[dc + 1]


def _taps_apply(pads, w_ref, taps, r0, rows):
    acc = jnp.zeros((rows, LANE), f32)
    for off, dc, wi in taps:
        xs = _tap_source(pads, dc)[PAD_ROWS + r0 + off:PAD_ROWS + r0 + off + rows, :]
        acc = acc + w_ref[wi:wi + 1, :] * xs
    return acc


def _taps_wgrad(pads, dy, taps, r0, rows, nw):
    out = jnp.zeros((nw, LANE), f32)
    rid = lax.broadcasted_iota(jnp.int32, (nw, 1), 0)
    for off, dc, wi in taps:
        xs = _tap_source(pads, dc)[PAD_ROWS + r0 + off:PAD_ROWS + r0 + off + rows, :]
        s = jnp.sum(dy * xs, axis=0, keepdims=True)
        out = out + jnp.where(rid == wi, s, 0.0)
    return out


def _transpose_taps(taps):
    return [(-off, None if dc is None else -dc, wi) for off, dc, wi in taps]


def _taps_1d(width):
    return [(j - width // 2, None, j) for j in range(width)]


def _taps_grid3():
    return [(GRID_W * dr, dc, 3 * (dr + 1) + (dc + 1)) for dr in (-1, 0, 1) for dc in (-1, 0, 1)]


def _row_chunks(t):
    r = min(CONV_ROWS, t)
    return [(i * r, r) for i in range(t // r)]


def _col_spec(t, off):
    return pl.BlockSpec((t, LANE), lambda c: (0, c + off))


def _w_spec(nw, off=0):
    return pl.BlockSpec((nw, LANE), lambda c: (0, c + off))


def gdn_conv(p, w, col0, nblk, norm_scale, name):
    t = p.shape[0]
    nw = w.shape[0]
    taps = _taps_1d(5)

    def body(p_ref, w_ref, o_ref, pad_ref):
        _fill_pad(pad_ref, p_ref[...], t)
        for r0, rows in _row_chunks(t):
            a = _silu(_taps_apply(pad_ref, w_ref, taps, r0, rows))
            if norm_scale is not None:
                a = a * (lax.rsqrt(jnp.sum(a * a, axis=-1, keepdims=True) + RMS_EPS) * norm_scale)
            o_ref[r0:r0 + rows, :] = a

    return pl.pallas_call(
        body, out_shape=SDS((t, nblk * LANE), f32), grid=(nblk,),
        in_specs=[_col_spec(t, col0), _w_spec(nw, col0)], out_specs=_col_spec(t, 0),
        scratch_shapes=[pltpu.VMEM((t + 2 * PAD_ROWS, LANE), f32)], name=name,
        compiler_params=_cparams(("parallel",)),
    )(p, w)


def gdn_conv_bwd(p, w, d_a, d_b, col0, nblk, norm_scale, name):
    t = p.shape[0]
    nw = w.shape[0]
    taps = _taps_1d(5)
    ttaps = _transpose_taps(taps)

    def body(p_ref, w_ref, da_ref, db_ref, dp_ref, dw_ref, pad_ref, gpad_ref):
        _fill_pad(pad_ref, p_ref[...], t)
        for r0, rows in _row_chunks(t):
            pre = _taps_apply(pad_ref, w_ref, taps, r0, rows)
            a = _silu(pre)
            dy = da_ref[r0:r0 + rows, :] + db_ref[r0:r0 + rows, :]
            if norm_scale is not None:
                r = lax.rsqrt(jnp.sum(a * a, axis=-1, keepdims=True) + RMS_EPS)
                da = norm_scale * (dy * r - a * (r * r * r) * jnp.sum(dy * a, axis=-1, keepdims=True))
            else:
                da = dy
            gpad_ref[PAD_ROWS + r0:PAD_ROWS + r0 + rows, :] = da * _dsilu(pre)
        zeros = jnp.zeros((PAD_ROWS, LANE), f32)
        gpad_ref[0:PAD_ROWS, :] = zeros
        gpad_ref[PAD_ROWS + t:2 * PAD_ROWS + t, :] = zeros
        dw = jnp.zeros((nw, LANE), f32)
        for r0, rows in _row_chunks(t):
            dp_ref[r0:r0 + rows, :] = _taps_apply(gpad_ref, w_ref, ttaps, r0, rows).astype(bf16)
            dw = dw + _taps_wgrad(pad_ref, gpad_ref[PAD_ROWS + r0:PAD_ROWS + r0 + rows, :], taps, r0, rows, nw)
        dw_ref[...] = dw

    return pl.pallas_call(
        body, out_shape=(SDS((t, nblk * LANE), bf16), SDS((nw, nblk * LANE), f32)), grid=(nblk,),
        in_specs=[_col_spec(t, col0), _w_spec(nw, col0), _col_spec(t, 0), _col_spec(t, 0)],
        out_specs=(_col_spec(t, 0), _w_spec(nw)),
        scratch_shapes=[pltpu.VMEM((t + 2 * PAD_ROWS, LANE), f32)] * 2, name=name,
        compiler_params=_cparams(("parallel",)),
    )(p, w, d_a, d_b)


def short_conv(p, w, name):
    t = p.shape[0]
    nw = w.shape[0]
    taps = _taps_1d(3)

    def body(gb_ref, gc_ref, h_ref, w_ref, o_ref, pad_ref):
        _fill_pad(pad_ref, gc_ref[...] * h_ref[...], t)
        for r0, rows in _row_chunks(t):
            o_ref[r0:r0 + rows, :] = (gb_ref[r0:r0 + rows, :] * _taps_apply(pad_ref, w_ref, taps, r0, rows)).astype(bf16)

    return pl.pallas_call(
        body, out_shape=SDS((t, 4 * LANE), bf16), grid=(4,),
        in_specs=[_col_spec(t, 0), _col_spec(t, 4), _col_spec(t, 8), _w_spec(nw)], out_specs=_col_spec(t, 0),
        scratch_shapes=[pltpu.VMEM((t + 2 * PAD_ROWS, LANE), f32)], name=name,
        compiler_params=_cparams(("parallel",)),
    )(p, p, p, w)


def short_conv_bwd(p, w, dy, name):
    t = p.shape[0]
    nw = w.shape[0]
    taps = _taps_1d(3)
    ttaps = _transpose_taps(taps)

    def body(gb_ref, gc_ref, h_ref, w_ref, dy_ref, dgb_ref, dgc_ref, dh_ref, dw_ref, pad_ref, gpad_ref):
        _fill_pad(pad_ref, gc_ref[...] * h_ref[...], t)
        _fill_pad(gpad_ref, dy_ref[...] * gb_ref[...], t)
        dw = jnp.zeros((nw, LANE), f32)
        for r0, rows in _row_chunks(t):
            sl = slice(r0, r0 + rows)
            dgb_ref[sl, :] = (dy_ref[sl, :] * _taps_apply(pad_ref, w_ref, taps, r0, rows)).astype(bf16)
            dm = _taps_apply(gpad_ref, w_ref, ttaps, r0, rows)
            dgc_ref[sl, :] = (dm * h_ref[sl, :]).astype(bf16)
            dh_ref[sl, :] = (dm * gc_ref[sl, :]).astype(bf16)
            dw = dw + _taps_wgrad(pad_ref, gpad_ref[PAD_ROWS + r0:PAD_ROWS + r0 + rows, :], taps, r0, rows, nw)
        dw_ref[...] = dw

    blk = SDS((t, 4 * LANE), bf16)
    return pl.pallas_call(
        body, out_shape=(blk, blk, blk, SDS((nw, 4 * LANE), f32)), grid=(4,),
        in_specs=[_col_spec(t, 0), _col_spec(t, 4), _col_spec(t, 8), _w_spec(nw), _col_spec(t, 0)],
        out_specs=(_col_spec(t, 0), _col_spec(t, 0), _col_spec(t, 0), _w_spec(nw)),
        scratch_shapes=[pltpu.VMEM((t + 2 * PAD_ROWS, LANE), f32)] * 2, name=name,
        compiler_params=_cparams(("parallel",)),
    )(p, p, p, w, dy)


def conf_conv(p, w, name):
    t = p.shape[0]
    nw = w.shape[0]
    taps = _taps_1d(31)

    def body(a_ref, b_ref, w_ref, o_ref, pad_ref):
        _fill_pad(pad_ref, a_ref[...] * jax.nn.sigmoid(b_ref[...]), t)
        for r0, rows in _row_chunks(t):
            o_ref[r0:r0 + rows, :] = _taps_apply(pad_ref, w_ref, taps, r0, rows)

    return pl.pallas_call(
        body, out_shape=SDS((t, 4 * LANE), f32), grid=(4,),
        in_specs=[_col_spec(t, 12), _col_spec(t, 16), _w_spec(nw)], out_specs=_col_spec(t, 0),
        scratch_shapes=[pltpu.VMEM((t + 2 * PAD_ROWS, LANE), f32)], name=name,
        compiler_params=_cparams(("parallel",)),
    )(p, p, w)


def conf_conv_bwd(p, w, dz, name):
    t = p.shape[0]
    nw = w.shape[0]
    taps = _taps_1d(31)
    ttaps = _transpose_taps(taps)

    def body(a_ref, b_ref, w_ref, dz_ref, da_ref, db_ref, dw_ref, pad_ref, gpad_ref):
        _fill_pad(pad_ref, a_ref[...] * jax.nn.sigmoid(b_ref[...]), t)
        _fill_pad(gpad_ref, dz_ref[...], t)
        dw = jnp.zeros((nw, LANE), f32)
        for r0, rows in _row_chunks(t):
            sl = slice(r0, r0 + rows)
            dm = _taps_apply(gpad_ref, w_ref, ttaps, r0, rows)
            sg = jax.nn.sigmoid(b_ref[sl, :])
            da_ref[sl, :] = (dm * sg).astype(bf16)
            db_ref[sl, :] = (dm * a_ref[sl, :] * sg * (1.0 - sg)).astype(bf16)
            dw = dw + _taps_wgrad(pad_ref, dz_ref[sl, :], taps, r0, rows, nw)
        dw_ref[...] = dw

    blk = SDS((t, 4 * LANE), bf16)
    return pl.pallas_call(
        body, out_shape=(blk, blk, SDS((nw, 4 * LANE), f32)), grid=(4,),
        in_specs=[_col_spec(t, 12), _col_spec(t, 16), _w_spec(nw), _col_spec(t, 0)],
        out_specs=(_col_spec(t, 0), _col_spec(t, 0), _w_spec(nw)),
        scratch_shapes=[pltpu.VMEM((t + 2 * PAD_ROWS, LANE), f32)] * 2, name=name,
        compiler_params=_cparams(("parallel",)),
    )(p, p, w, dz)


def ffn_conv(h, w, name):
    t = h.shape[0]
    width = 2 * LANE
    nblk = D_FF // width
    nw = w.shape[0]
    taps = _taps_grid3()

    def body(a_ref, g_ref, w_ref, o_ref, *pads):
        for s in range(width // LANE):
            ls = slice(s * LANE, (s + 1) * LANE)
            _grid_pads_clear_edges(pads, t)
            for r0, rows in _row_chunks(t):
                _grid_pads_set(pads, r0, a_ref[r0:r0 + rows, ls])
            for r0, rows in _row_chunks(t):
                conv = _taps_apply(pads, w_ref.at[:, ls], taps, r0, rows)
                o_ref[r0:r0 + rows, ls] = (_silu(conv) * g_ref[r0:r0 + rows, ls]).astype(bf16)

    spec = lambda off: pl.BlockSpec((t, width), lambda c: (0, c + off))
    return pl.pallas_call(
        body, out_shape=SDS((t, D_FF), bf16), grid=(nblk,),
        in_specs=[spec(0), spec(nblk), pl.BlockSpec((nw, width), lambda c: (0, c))], out_specs=spec(0),
        scratch_shapes=[pltpu.VMEM((t + 2 * PAD_ROWS, LANE), f32)] * 3, name=name,
        compiler_params=_cparams(("parallel",)),
    )(h, h, w)


def ffn_conv_bwd(h, w, df, name):
    t = h.shape[0]
    nblk = D_FF // LANE
    nw = w.shape[0]
    taps = _taps_grid3()
    ttaps = _transpose_taps(taps)

    def body(a_ref, g_ref, w_ref, df_ref, dh_ref, dw_ref, *all_pads):
        half = pl.program_id(1)
        pads, gpads = all_pads[:3], all_pads[3:]

        @pl.when(half == 0)
        def _():
            _grid_pads_clear_edges(all_pads, t)
            for r0, rows in _row_chunks(t):
                _grid_pads_set(pads, r0, a_ref[r0:r0 + rows, :])
            for r0, rows in _row_chunks(t):
                sl = slice(r0, r0 + rows)
                pre = _taps_apply(pads, w_ref, taps, r0, rows)
                _grid_pads_set(gpads, r0, df_ref[sl, :] * g_ref[sl, :] * _dsilu(pre))
                dh_ref[sl, :] = (df_ref[sl, :] * _silu(pre)).astype(bf16)

        @pl.when(half == 1)
        def _():
            dw = jnp.zeros((nw, LANE), f32)
            for r0, rows in _row_chunks(t):
                dh_ref[r0:r0 + rows, :] = _taps_apply(gpads, w_ref, ttaps, r0, rows).astype(bf16)
                dw = dw + _taps_wgrad(pads, gpads[1][PAD_ROWS + r0:PAD_ROWS + r0 + rows, :], taps, r0, rows, nw)
            dw_ref[...] = dw

    cspec = lambda off: pl.BlockSpec((t, LANE), lambda c, s: (0, c + off))
    return pl.pallas_call(
        body, out_shape=(SDS((t, 2 * D_FF), bf16), SDS((nw, D_FF), f32)), grid=(nblk, 2),
        in_specs=[cspec(0), cspec(nblk), pl.BlockSpec((nw, LANE), lambda c, s: (0, c)), cspec(0)],
        out_specs=(pl.BlockSpec((t, LANE), lambda c, s: (0, c + nblk * (1 - s))), pl.BlockSpec((nw, LANE), lambda c, s: (0, c))),
        scratch_shapes=[pltpu.VMEM((t + 2 * PAD_ROWS, LANE), f32)] * 6, name=name,
        compiler_params=_cparams(("parallel", "arbitrary")),
    )(h, h, w, df)


def _pool_count(r0, rows, win, t):
    pos = lax.broadcasted_iota(jnp.int32, (rows, 1), 0) + r0
    lo = jnp.clip(pos - win // 2, 0, t)
    hi = jnp.clip(pos - win // 2 + win, 0, t)
    return (hi - lo).astype(f32)


def _window_sum(pad_ref, r0, rows, lo, hi):
    acc = jnp.zeros((rows, LANE), f32)
    for off in range(lo, hi):
        acc = acc + pad_ref[PAD_ROWS + r0 + off:PAD_ROWS + r0 + off + rows, :]
    return acc


def pool_mix(p, pool_w, pool_scale, name):
    t = p.shape[0]

    def body(x_ref, w_ref, s_ref, o_ref, pad_ref):
        for gi, win in enumerate(POOL_WINDOWS):
            cs = slice(gi * LANE, (gi + 1) * LANE)
            _fill_pad(pad_ref, x_ref[:, cs], t)
            wg = w_ref[gi].astype(bf16)
            for r0, rows in _row_chunks(t):
                pooled = _window_sum(pad_ref, r0, rows, -(win // 2), win - win // 2) / _pool_count(r0, rows, win, t) - x_ref[r0:r0 + rows, cs]
                o_ref[r0:r0 + rows, cs] = (_dotb(pooled, wg) * s_ref[:, cs]).astype(bf16)

    return pl.pallas_call(
        body, out_shape=SDS((t, 512), bf16), grid=(1,),
        in_specs=[pl.BlockSpec((t, 512), lambda i: (0, 4)), pl.BlockSpec((4, LANE, LANE), lambda i: (0, 0, 0)),
                  pl.BlockSpec((1, 512), lambda i: (0, 0))],
        out_specs=pl.BlockSpec((t, 512), lambda i: (0, 0)),
        scratch_shapes=[pltpu.VMEM((t + 2 * PAD_ROWS, LANE), f32)], name=name,
        compiler_params=_cparams(("arbitrary",)),
    )(p, pool_w, pool_scale)


def pool_mix_bwd(p, pool_w, pool_scale, dmix, name):
    t = p.shape[0]

    def body(x_ref, w_ref, s_ref, dy_ref, dp_ref, dw_ref, ds_ref, pad_ref, gpad_ref, dpool_ref):
        for gi, win in enumerate(POOL_WINDOWS):
            cs = slice(gi * LANE, (gi + 1) * LANE)
            h = win // 2
            _fill_pad(pad_ref, x_ref[:, cs], t)
            wg = w_ref[gi].astype(bf16)
            dw = jnp.zeros((LANE, LANE), f32)
            ds = jnp.zeros((1, LANE), f32)
            zeros = jnp.zeros((PAD_ROWS, LANE), f32)
            gpad_ref[0:PAD_ROWS, :] = zeros
            gpad_ref[PAD_ROWS + t:2 * PAD_ROWS + t, :] = zeros
            for r0, rows in _row_chunks(t):
                cnt = _pool_count(r0, rows, win, t)
                pooled = _window_sum(pad_ref, r0, rows, -h, win - h) / cnt - x_ref[r0:r0 + rows, cs]
                dy = dy_ref[r0:r0 + rows, cs]
                ds = ds + jnp.sum(dy * _dotb(pooled, wg), axis=0, keepdims=True)
                dypre = dy * s_ref[:, cs]
                dw = dw + _dotb_tn(pooled, dypre)
                dpooled = _dotb_nt(dypre, wg)
                gpad_ref[PAD_ROWS + r0:PAD_ROWS + r0 + rows, :] = dpooled / cnt
                dpool_ref[r0:r0 + rows, :] = dpooled
            dw_ref[gi] = dw
            ds_ref[:, cs] = ds
            for r0, rows in _row_chunks(t):
                dx = _window_sum(gpad_ref, r0, rows, -h + 1, h + 1) - dpool_ref[r0:r0 + rows, :]
                dp_ref[r0:r0 + rows, cs] = dx.astype(bf16)

    return pl.pallas_call(
        body, out_shape=(SDS((t, 512), bf16), SDS((4, LANE, LANE), f32), SDS((1, 512), f32)), grid=(1,),
        in_specs=[pl.BlockSpec((t, 512), lambda i: (0, 4)), pl.BlockSpec((4, LANE, LANE), lambda i: (0, 0, 0)),
                  pl.BlockSpec((1, 512), lambda i: (0, 0)), pl.BlockSpec((t, 512), lambda i: (0, 1))],
        out_specs=(pl.BlockSpec((t, 512), lambda i: (0, 0)), pl.BlockSpec((4, LANE, LANE), lambda i: (0, 0, 0)),
                   pl.BlockSpec((1, 512), lambda i: (0, 0))),
        scratch_shapes=[pltpu.VMEM((t + 2 * PAD_ROWS, LANE), f32)] * 2 + [pltpu.VMEM((t, LANE), f32)], name=name,
        compiler_params=_cparams(("arbitrary",)),
    )(p, pool_w, pool_scale, dmix)


def gated_rmsnorm(o_a, o_b, p, norm_w, name):
    t = o_a.shape[0]
    tt = _row_tile(t)

    def body(oa_ref, ob_ref, g_ref, nw_ref, y_ref):
        for h in range(GDN_HEADS):
            cs = slice(h * LANE, (h + 1) * LANE)
            o = oa_ref[:, cs] + ob_ref[:, cs]
            r = lax.rsqrt(jnp.mean(o * o, axis=-1, keepdims=True) + RMS_EPS)
            y_ref[:, cs] = (o * r * nw_ref[...] * _silu(g_ref[:, cs])).astype(bf16)

    return pl.pallas_call(
        body, out_shape=SDS((t, 512), bf16), grid=(t // tt,),
        in_specs=[_row_spec(tt, 512), _row_spec(tt, 512), pl.BlockSpec((tt, 512), lambda i: (i, 3)), _vec_spec(LANE)],
        out_specs=_row_spec(tt, 512), name=name, compiler_params=_cparams(("parallel",)),
    )(o_a, o_b, p, norm_w)


def gated_rmsnorm_bwd(o_a, o_b, p, norm_w, dmix, name):
    t = o_a.shape[0]
    tt = _row_tile(t)

    def body(oa_ref, ob_ref, g_ref, nw_ref, dy_ref, do_ref, dg_ref, dnw_ref):
        dnw = jnp.zeros((1, LANE), f32)
        for h in range(GDN_HEADS):
            cs = slice(h * LANE, (h + 1) * LANE)
            o = oa_ref[:, cs] + ob_ref[:, cs]
            r = lax.rsqrt(jnp.mean(o * o, axis=-1, keepdims=True) + RMS_EPS)
            gate = g_ref[:, cs]
            dy = dy_ref[:, cs]
            dy1 = dy * _silu(gate)
            dg_ref[:, cs] = (dy * (o * r * nw_ref[...]) * _dsilu(gate)).astype(bf16)
            dnw = dnw + jnp.sum(dy1 * o * r, axis=0, keepdims=True)
            dn = dy1 * nw_ref[...]
            do_ref[:, cs] = r * dn - o * (r * r * r) * jnp.mean(dn * o, axis=-1, keepdims=True)
        _acc_rows(dnw_ref, dnw)

    return pl.pallas_call(
        body, out_shape=(SDS((t, 512), f32), SDS((t, 512), bf16), SDS((1, LANE), f32)), grid=(t // tt,),
        in_specs=[_row_spec(tt, 512), _row_spec(tt, 512), pl.BlockSpec((tt, 512), lambda i: (i, 3)), _vec_spec(LANE),
                  _row_spec(tt, 512)],
        out_specs=(_row_spec(tt, 512), _row_spec(tt, 512), _vec_spec(LANE)),
        name=name, compiler_params=_cparams(("arbitrary",)),
    )(o_a, o_b, p, norm_w, dmix)


def ln_silu(z, g, b, name):
    t, d = z.shape
    tt = _row_tile(t)

    def body(z_ref, g_ref, b_ref, o_ref):
        xhat, _ = _ln_stats(z_ref[...])
        o_ref[...] = _silu(xhat * g_ref[...] + b_ref[...]).astype(bf16)

    return pl.pallas_call(
        body, out_shape=SDS((t, d), bf16), grid=(t // tt,),
        in_specs=[_row_spec(tt, d), _vec_spec(d), _vec_spec(d)], out_specs=_row_spec(tt, d),
        name=name, compiler_params=_cparams(("parallel",)),
    )(z, g, b)


def ln_silu_bwd(z, g, b, dmix, name):
    t, d = z.shape
    tt = _row_tile(t)

    def body(z_ref, g_ref, b_ref, dy_ref, dz_ref, dg_ref, db_ref):
        xhat, rstd = _ln_stats(z_ref[...])
        dn = dy_ref[...] * _dsilu(xhat * g_ref[...] + b_ref[...])
        dz_ref[...] = _ln_bwd(dn * g_ref[...], xhat, rstd)
        _acc_rows(dg_ref, jnp.sum(dn * xhat, axis=0, keepdims=True))
        _acc_rows(db_ref, jnp.sum(dn, axis=0, keepdims=True))

    return pl.pallas_call(
        body, out_shape=(SDS((t, d), f32), SDS((1, d), f32), SDS((1, d), f32)), grid=(t // tt,),
        in_specs=[_row_spec(tt, d), _vec_spec(d), _vec_spec(d), pl.BlockSpec((tt, d), lambda i: (i, 1))],
        out_specs=(_row_spec(tt, d), _vec_spec(d), _vec_spec(d)),
        name=name, compiler_params=_cparams(("arbitrary",)),
    )(z, g, b, dmix)


def gdn_gates(p, neg_a, dt_bias, name):
    t = p.shape[0]
    tt = _row_tile(t)

    def body(s_ref, na_ref, dt_ref, o_ref):
        s = s_ref[...]
        col = lax.broadcasted_iota(jnp.int32, s.shape, 1)
        o_ref[...] = jnp.where(col < 8, jax.nn.sigmoid(s), na_ref[...] * jax.nn.softplus(s + dt_ref[...]))

    return pl.pallas_call(
        body, out_shape=SDS((t, LANE), f32), grid=(t // tt,),
        in_specs=[pl.BlockSpec((tt, LANE), lambda i: (i, 20)), _vec_spec(LANE), _vec_spec(LANE)],
        out_specs=_row_spec(tt, LANE), name=name, compiler_params=_cparams(("parallel",)),
    )(p, neg_a, dt_bias)


def gdn_gates_bwd(p, neg_a, dt_bias, dbg_a, dbg_b, name):
    t = p.shape[0]
    tt = _row_tile(t)

    def body(s_ref, na_ref, dt_ref, d_ref, d2_ref, ds_ref, da_ref, ddt_ref):
        s = s_ref[...]
        d = d_ref[...] + d2_ref[...]
        col = lax.broadcasted_iota(jnp.int32, s.shape, 1)
        sg = jax.nn.sigmoid(s)
        z = s + dt_ref[...]
        dz = jnp.where((col >= 8) & (col < 16), d * na_ref[...] * jax.nn.sigmoid(z), 0.0)
        ds_ref[...] = jnp.where(col < 8, d * sg * (1.0 - sg), dz).astype(bf16)
        dalog = jnp.where((col >= 8) & (col < 16), d * na_ref[...] * jax.nn.softplus(z), 0.0)
        _acc_rows(da_ref, jnp.sum(dalog, axis=0, keepdims=True))
        _acc_rows(ddt_ref, jnp.sum(dz, axis=0, keepdims=True))

    return pl.pallas_call(
        body, out_shape=(SDS((t, LANE), bf16), SDS((1, LANE), f32), SDS((1, LANE), f32)), grid=(t // tt,),
        in_specs=[pl.BlockSpec((tt, LANE), lambda i: (i, 20)), _vec_spec(LANE), _vec_spec(LANE), _row_spec(tt, LANE),
                  _row_spec(tt, LANE)],
        out_specs=(_row_spec(tt, LANE), _vec_spec(LANE), _vec_spec(LANE)),
        name=name, compiler_params=_cparams(("arbitrary",)),
    )(p, neg_a, dt_bias, dbg_a, dbg_b)


N_SCAN = 2 * GDN_HEADS


def _bdot(a, b, ca, cb, precision=None):
    if precision is None:
        a, b = a.astype(bf16), b.astype(bf16)
    return lax.dot_general(a, b, (((ca,), (cb,)), ((0,), (0,))), preferred_element_type=f32, precision=precision)


def _bdot_nn(a, b, precision=None):
    return _bdot(a, b, 2, 1, precision)


def _bdot_nt(a, b):
    return _bdot(a, b, 2, 2)


def _bdot_tn(a, b, precision=None):
    return _bdot(a, b, 1, 1, precision)


def _order_masks():
    shape = (N_SCAN, CHUNK, CHUNK)
    sign = jnp.where(lax.broadcasted_iota(jnp.int32, shape, 0) >= GDN_HEADS, -1, 1)
    ahead = (lax.broadcasted_iota(jnp.int32, shape, 1) - lax.broadcasted_iota(jnp.int32, shape, 2)) * sign
    lower, strict, lower_t = ahead >= 0, ahead > 0, ahead <= 0
    col_shape = (N_SCAN, CHUNK, 1)
    back1 = lax.broadcasted_iota(jnp.int32, col_shape, 0) >= GDN_HEADS
    row1 = lax.broadcasted_iota(jnp.int32, col_shape, 1)
    at_last = (row1 == jnp.where(back1, 0, CHUNK - 1)).astype(f32)
    return lower, strict, lower_t, at_last


def _stack_heads(f_ref, b_ref):
    return jnp.stack([ref[:, h * LANE:(h + 1) * LANE] for ref in (f_ref, b_ref) for h in range(GDN_HEADS)])


def _stack_gates(bgf, bgb, bgtf, bgtb):
    beta = jnp.stack([bg[:, 4 * d + h:4 * d + h + 1] for d, bg in enumerate((bgf, bgb)) for h in range(GDN_HEADS)])
    g_col = jnp.stack([bg[:, 8 + 4 * d + h:9 + 4 * d + h] for d, bg in enumerate((bgf, bgb)) for h in range(GDN_HEADS)])
    g_row = jnp.stack([bgt[8 + 4 * d + h:9 + 4 * d + h, :] for d, bgt in enumerate((bgtf, bgtb)) for h in range(GDN_HEADS)])
    return beta, g_col, g_row


def _chunk_terms(k, v, beta, g_col, g_row, masks, tinv=None):
    lower, strict, lower_t, at_last = masks
    gc = jnp.sum(lower.astype(f32) * g_row, axis=2, keepdims=True)
    gr = jnp.sum(lower_t.astype(f32) * g_col, axis=1, keepdims=True)
    g_last = jnp.sum(at_last * gc, axis=1, keepdims=True)
    e = jnp.exp(gc)
    f = jnp.exp(g_last - gc)
    dm = jnp.exp(jnp.where(lower, gc - gr, -1e30))
    kb = k * beta
    kk = _bdot_nt(kb, k)
    if tinv is None:
        shape = (N_SCAN, CHUNK, CHUNK)
        eye = (lax.broadcasted_iota(jnp.int32, shape, 1) == lax.broadcasted_iota(jnp.int32, shape, 2)).astype(f32)
        pw = -jnp.where(strict, kk * dm, 0.0)
        tinv = eye + pw
        for _ in range(5):
            pw = _bdot_nn(pw, pw, lax.Precision.HIGH)
            tinv = tinv + _bdot_nn(tinv, pw, lax.Precision.HIGH)
    u = _bdot_nn(tinv, v * beta)
    w = _bdot_nn(tinv, kb * e)
    return dict(e=e, f=f, gl=jnp.exp(g_last), dm=dm, kb=kb, kk=kk, tinv=tinv, u=u, w=w, kd=k * f)


def _gdn_specs(nc, width, step_chunk):
    return [pl.BlockSpec((CHUNK, width), functools.partial(lambda i, d: (step_chunk(i, d), 0), d=d)) for d in (0, 1)]


def gdn_forward(q, k, v, bg, bgt, s0, with_out, name):
    t = k.shape[0]
    nc = t // CHUNK

    def body(qf_ref, qb_ref, kf_ref, kb_ref, vf_ref, vb_ref, bgf_ref, bgb_ref, bgtf_ref, bgtb_ref, s0_ref,
             of_ref, ob_ref, sallf_ref, sallb_ref, tinvf_ref, tinvb_ref, sfin_ref, s_ref):
        i = pl.program_id(0)

        @pl.when(i == 0)
        def _():
            s_ref[...] = s0_ref[...]

        masks = _order_masks()
        k8, v8 = _stack_heads(kf_ref, kb_ref), _stack_heads(vf_ref, vb_ref)
        beta, g_col, g_row = _stack_gates(bgf_ref[...], bgb_ref[...], bgtf_ref[0], bgtb_ref[0])
        c = _chunk_terms(k8, v8, beta, g_col, g_row, masks)
        s = s_ref[...]
        sallf_ref[0] = s[:GDN_HEADS]
        sallb_ref[0] = s[GDN_HEADS:]
        tinvf_ref[0] = c["tinv"][:GDN_HEADS]
        tinvb_ref[0] = c["tinv"][GDN_HEADS:]
        vn = c["u"] - _bdot_nn(c["w"], s)
        if with_out:
            q8 = _stack_heads(qf_ref, qb_ref)
            pm = jnp.where(masks[0], _bdot_nt(q8, k8) * c["dm"], 0.0)
            o = _bdot_nn(q8 * c["e"], s) + _bdot_nn(pm, vn)
        for d, o_ref in enumerate((of_ref, ob_ref)):
            for h in range(GDN_HEADS):
                o_ref[:, h * LANE:(h + 1) * LANE] = o[GDN_HEADS * d + h] if with_out else jnp.zeros((CHUNK, LANE), f32)
        s_ref[...] = c["gl"] * s + _bdot_tn(c["kd"], vn)

        @pl.when(i == nc - 1)
        def _():
            sfin_ref[...] = s_ref[...]

    chunk_of = lambda i, d: i if d == 0 else nc - 1 - i
    seq = _gdn_specs(nc, 512, chunk_of)
    gate = _gdn_specs(nc, LANE, chunk_of)
    gate_t = [pl.BlockSpec((1, 16, CHUNK), functools.partial(lambda i, d: (chunk_of(i, d), 0, 0), d=d)) for d in (0, 1)]
    sall = [pl.BlockSpec((1, GDN_HEADS, LANE, LANE), functools.partial(lambda i, d: (chunk_of(i, d), 0, 0, 0), d=d)) for d in (0, 1)]
    tinv = [pl.BlockSpec((1, GDN_HEADS, CHUNK, CHUNK), functools.partial(lambda i, d: (chunk_of(i, d), 0, 0, 0), d=d)) for d in (0, 1)]
    st = pl.BlockSpec((N_SCAN, LANE, LANE), lambda i: (0, 0, 0))
    o_shape, s_shape, t_shape = SDS((t, 512), f32), SDS((nc, GDN_HEADS, LANE, LANE), f32), SDS((nc, GDN_HEADS, CHUNK, CHUNK), f32)
    o_f, o_b, sall_f, sall_b, tinv_f, tinv_b, s_fin = pl.pallas_call(
        body, out_shape=(o_shape, o_shape, s_shape, s_shape, t_shape, t_shape, SDS((N_SCAN, LANE, LANE), f32)), grid=(nc,),
        in_specs=seq + seq + seq + gate + gate_t + [st], out_specs=tuple(seq + sall + tinv + [st]),
        scratch_shapes=[pltpu.VMEM((N_SCAN, LANE, LANE), f32)], name=name,
        compiler_params=_cparams(("arbitrary",)),
    )(q, q, k, k, v, v, bg, bg, bgt, bgt, s0.reshape(N_SCAN, LANE, LANE))
    return o_f, o_b, (sall_f, sall_b, tinv_f, tinv_b), s_fin.reshape(2, GDN_HEADS, LANE, LANE)


def _gdn_chunk_bwd(q, k, v, d_o, beta, g_col, g_row, s, tinv, dsn, masks):
    lower, strict, _, at_last = masks
    c = _chunk_terms(k, v, beta, g_col, g_row, masks, tinv)
    e, f, gl, dm, kb, kk, tinv, u, w, kd = (c[n] for n in ("e", "f", "gl", "dm", "kb", "kk", "tinv", "u", "w", "kd"))
    vn = u - _bdot_nn(w, s)
    ds = gl * dsn
    dgl = jnp.sum(jnp.sum(s * dsn, axis=2, keepdims=True), axis=1, keepdims=True)
    dkd = _bdot_nt(vn, dsn)
    dvn = _bdot_nn(kd, dsn)
    dm_grad = jnp.zeros((N_SCAN, CHUNK, CHUNK), f32)
    de = jnp.zeros((N_SCAN, CHUNK, 1), f32)
    dq = None
    dk = jnp.zeros((N_SCAN, CHUNK, LANE), f32)
    if q is not None:
        qk = _bdot_nt(q, k)
        pm = jnp.where(lower, qk * dm, 0.0)
        dqd = _bdot_nt(d_o, s)
        ds = ds + _bdot_tn(q * e, d_o)
        dpm = jnp.where(lower, _bdot_nt(d_o, vn), 0.0)
        dvn = dvn + _bdot_tn(pm, d_o)
        dqk = dpm * dm
        dm_grad = dm_grad + dpm * qk
        dq = _bdot_nn(dqk, k) + dqd * e
        dk = _bdot_tn(dqk, q)
        de = de + jnp.sum(dqd * q, axis=2, keepdims=True)
    dw = -_bdot_nt(dvn, s)
    ds = ds - _bdot_tn(w, dvn)
    drv = _bdot_tn(tinv, dvn)
    drk = _bdot_tn(tinv, dw)
    da = -jnp.where(strict, _bdot_nt(drv, u) + _bdot_nt(drk, w), 0.0)
    dbeta = jnp.sum(drv * v, axis=2, keepdims=True)
    dv = drv * beta
    dkb = drk * e
    de = de + jnp.sum(drk * kb, axis=2, keepdims=True)
    dkk = da * dm
    dm_grad = dm_grad + da * kk
    dkb = dkb + _bdot_nn(dkk, k)
    dk = dk + _bdot_tn(dkk, kb) + dkd * f
    df = jnp.sum(dkd * k, axis=2, keepdims=True)
    dbeta = dbeta + jnp.sum(dkb * k, axis=2, keepdims=True)
    dk = dk + dkb * beta
    m = dm_grad * dm
    shape = (N_SCAN, CHUNK, CHUNK)
    eye = (lax.broadcasted_iota(jnp.int32, shape, 1) == lax.broadcasted_iota(jnp.int32, shape, 2)).astype(f32)

    def as_col(row):
        return jnp.sum(eye * row, axis=2, keepdims=True)

    rsum = jnp.sum(m, axis=2, keepdims=True)
    csum = as_col(jnp.sum(m, axis=1, keepdims=True))
    dgl_tot = jnp.sum(df * f, axis=1, keepdims=True) + dgl * gl
    dgc = de * e - df * f + rsum - csum + at_last * dgl_tot
    dg = as_col(jnp.sum(lower.astype(f32) * dgc, axis=1, keepdims=True))
    return dq, dk, dv, dbeta, dg, ds


def gdn_backward(q, k, v, bg, bgt, saved, d_o, ds_fin, with_out, name):
    t = k.shape[0]
    nc = t // CHUNK

    def body(qf_ref, qb_ref, kf_ref, kb_ref, vf_ref, vb_ref, bgf_ref, bgb_ref, bgtf_ref, bgtb_ref,
             sallf_ref, sallb_ref, tinvf_ref, tinvb_ref, dof_ref, dob_ref, dsf_ref,
             dqf_ref, dqb_ref, dkf_ref, dkb_ref, dvf_ref, dvb_ref, dbgf_ref, dbgb_ref, ds0_ref, ds_ref):
        i = pl.program_id(0)

        @pl.when(i == 0)
        def _():
            ds_ref[...] = dsf_ref[...]

        lane = lax.broadcasted_iota(jnp.int32, (1, LANE), 1)
        masks = _order_masks()
        beta, g_col, g_row = _stack_gates(bgf_ref[...], bgb_ref[...], bgtf_ref[0], bgtb_ref[0])
        s = jnp.concatenate([sallf_ref[0], sallb_ref[0]], 0)
        tinv = jnp.concatenate([tinvf_ref[0], tinvb_ref[0]], 0)
        dq, dk, dv, dbeta, dg, ds = _gdn_chunk_bwd(
            _stack_heads(qf_ref, qb_ref) if with_out else None, _stack_heads(kf_ref, kb_ref), _stack_heads(vf_ref, vb_ref),
            _stack_heads(dof_ref, dob_ref), beta, g_col, g_row, s, tinv, ds_ref[...], masks)
        ds_ref[...] = ds
        for d, (dq_ref, dk_ref, dv_ref, dbg_ref) in enumerate(((dqf_ref, dkf_ref, dvf_ref, dbgf_ref), (dqb_ref, dkb_ref, dvb_ref, dbgb_ref))):
            dbg = jnp.zeros((CHUNK, LANE), f32)
            for h in range(GDN_HEADS):
                b = GDN_HEADS * d + h
                cs = slice(h * LANE, (h + 1) * LANE)
                dq_ref[:, cs] = dq[b] if with_out else jnp.zeros((CHUNK, LANE), f32)
                dk_ref[:, cs] = dk[b]
                dv_ref[:, cs] = dv[b]
                dbg = dbg + dbeta[b] * (lane == b).astype(f32) + dg[b] * (lane == 8 + b).astype(f32)
            dbg_ref[...] = dbg

        @pl.when(i == nc - 1)
        def _():
            ds0_ref[...] = ds_ref[...]

    chunk_of = lambda i, d: nc - 1 - i if d == 0 else i
    seq = _gdn_specs(nc, 512, chunk_of)
    gate = _gdn_specs(nc, LANE, chunk_of)
    gate_t = [pl.BlockSpec((1, 16, CHUNK), functools.partial(lambda i, d: (chunk_of(i, d), 0, 0), d=d)) for d in (0, 1)]
    sall = [pl.BlockSpec((1, GDN_HEADS, LANE, LANE), functools.partial(lambda i, d: (chunk_of(i, d), 0, 0, 0), d=d)) for d in (0, 1)]
    tinv = [pl.BlockSpec((1, GDN_HEADS, CHUNK, CHUNK), functools.partial(lambda i, d: (chunk_of(i, d), 0, 0, 0), d=d)) for d in (0, 1)]
    st = pl.BlockSpec((N_SCAN, LANE, LANE), lambda i: (0, 0, 0))
    o_shape, g_shape = SDS((t, 512), f32), SDS((t, LANE), f32)
    res = pl.pallas_call(
        body, out_shape=(o_shape,) * 6 + (g_shape, g_shape, SDS((N_SCAN, LANE, LANE), f32)), grid=(nc,),
        in_specs=seq + seq + seq + gate + gate_t + sall + tinv + seq + [st], out_specs=tuple(seq + seq + seq + gate + [st]),
        scratch_shapes=[pltpu.VMEM((N_SCAN, LANE, LANE), f32)], name=name,
        compiler_params=_cparams(("arbitrary",)),
    )(q, q, k, k, v, v, bg, bg, bgt, bgt, *saved, d_o, d_o, ds_fin.reshape(N_SCAN, LANE, LANE))
    return tuple(res[:8]) + (res[8].reshape(2, GDN_HEADS, LANE, LANE),)


def _my_position():
    x, y, c = lax.axis_index("x"), lax.axis_index("y"), lax.axis_index("c")
    return x, y, c, 4 * x + 2 * y + c


def exchange(arrays, scatter, name):
    n = len(arrays)
    shapes = [a.shape[1:] if scatter else a.shape for a in arrays]

    def body(*refs):
        ins, outs, token = refs[:n], refs[n:2 * n], refs[2 * n]
        send_sems, recv_sems, local_sems = refs[2 * n + 1:]
        x, y, c, me = _my_position()
        token[...] = jnp.zeros_like(token)
        started = []
        for a in range(n):
            mine = pltpu.make_async_copy(ins[a].at[me] if scatter else ins[a], outs[a].at[me], local_sems.at[a])
            mine.start()
            started.append(mine)
        waits = []
        for r in range(1, N_DEV):
            px = 1 - x if r & 4 else x
            py = 1 - y if r & 2 else y
            pc = 1 - c if r & 1 else c
            pid = 4 * px + 2 * py + pc
            for a in range(n):
                cp = pltpu.make_async_remote_copy(
                    src_ref=ins[a].at[pid] if scatter else ins[a], dst_ref=outs[a].at[me],
                    send_sem=send_sems.at[a, r - 1], recv_sem=recv_sems.at[a, r - 1],
                    device_id=(px, py, pc), device_id_type=pl.DeviceIdType.MESH)
                cp.start()
                arrive = pltpu.make_async_remote_copy(
                    src_ref=ins[a].at[pid] if scatter else ins[a], dst_ref=outs[a].at[pid],
                    send_sem=send_sems.at[a, r - 1], recv_sem=recv_sems.at[a, r - 1],
                    device_id=(px, py, pc), device_id_type=pl.DeviceIdType.MESH)
                waits.append((cp, arrive))
        for cp, arrive in waits:
            cp.wait_send()
            arrive.wait_recv()
        for mine in started:
            mine.wait()

    any_spec = pl.BlockSpec(memory_space=pl.ANY)
    return pl.pallas_call(
        body, out_shape=tuple(SDS((N_DEV,) + tuple(s), a.dtype) for s, a in zip(shapes, arrays)) + (SDS((8, LANE), f32),),
        in_specs=[any_spec] * n, out_specs=tuple([any_spec] * n) + (pl.BlockSpec(memory_space=pltpu.VMEM),),
        scratch_shapes=[pltpu.SemaphoreType.DMA((n, N_DEV - 1)), pltpu.SemaphoreType.DMA((n, N_DEV - 1)),
                        pltpu.SemaphoreType.DMA((n,))],
        name=name,
    )(*arrays)


_HBM_SPEC = pl.BlockSpec(memory_space=pltpu.HBM)
_SEM_SPEC = pl.BlockSpec(memory_space=pltpu.SEMAPHORE)
_DATAFLOW = pltpu.SideEffectType.DATAFLOW_SIDE_EFFECTING


def _peers(x, y, c):
    out = []
    for r in range(1, N_DEV):
        px = 1 - x if r & 4 else x
        py = 1 - y if r & 2 else y
        pc = 1 - c if r & 1 else c
        out.append((r, (px, py, pc), 4 * px + 2 * py + pc))
    return out


def _exchange_copies(ins, lands, send_sems, recv_sems, scatter, arrivals):
    x, y, c, me = _my_position()
    pairs = []
    for r, peer, pid in _peers(x, y, c):
        for a in range(len(ins)):
            k = a * (N_DEV - 1) + r - 1
            kw = dict(send_sem=send_sems.at[k], recv_sem=recv_sems.at[k], device_id=peer, device_id_type=pl.DeviceIdType.MESH)
            src = ins[a].at[pid] if scatter else ins[a]
            send = pltpu.make_async_remote_copy(src_ref=src, dst_ref=lands[a].at[me], **kw)
            arrive = pltpu.make_async_remote_copy(src_ref=src, dst_ref=lands[a].at[pid], **kw) if arrivals else None
            pairs.append((send, arrive))
    return pairs


def exchange_start(arrays, scatter, name):
    n = len(arrays)
    shapes = [a.shape[1:] if scatter else a.shape for a in arrays]

    def body(*refs):
        ins, lands = refs[:n], refs[n:2 * n]
        send_sems, recv_sems = refs[2 * n], refs[2 * n + 1]
        token = refs[-1]
        for send, _ in _exchange_copies(ins, lands, send_sems, recv_sems, scatter, False):
            send.start()
        token[...] = jnp.zeros_like(token)

    sem = pltpu.SemaphoreType.DMA((n * (N_DEV - 1),))
    land_shapes = [(N_DEV,) + tuple(s) for s in shapes]
    res = pl.pallas_call(
        body, name=name,
        out_shape=(sem, sem, *[pltpu.HBM(a.shape, a.dtype) for a in arrays],
                   *[pltpu.HBM(s, a.dtype) for s, a in zip(land_shapes, arrays)], SDS((8, LANE), f32)),
        in_specs=[_HBM_SPEC] * (2 * n),
        out_specs=(_SEM_SPEC, _SEM_SPEC, *[_HBM_SPEC] * (2 * n), pl.BlockSpec(memory_space=pltpu.VMEM)),
        input_output_aliases={i: 2 + i for i in range(2 * n)},
        compiler_params=pltpu.CompilerParams(has_side_effects=_DATAFLOW),
    )(*[pltpu.with_memory_space_constraint(a, pltpu.HBM) for a in arrays],
      *[pltpu.with_memory_space_constraint(lax.empty(s, a.dtype), pltpu.HBM) for s, a in zip(land_shapes, arrays)])
    return (res[0], res[1], list(res[2:2 + n]), list(res[2 + n:2 + 2 * n]), scatter), res[-1]


def exchange_wait(handle, after, name):
    send_sems, recv_sems, ins, lands, scatter = handle
    n = len(ins)

    def body(*refs):
        in_refs, land_refs = refs[:n], refs[n:2 * n]
        for send, arrive in _exchange_copies(in_refs, land_refs, refs[2 * n], refs[2 * n + 1], scatter, True):
            send.wait_send()
            arrive.wait_recv()
        refs[-1][...] = jnp.zeros_like(refs[-1])

    res = pl.pallas_call(
        body, name=name,
        out_shape=tuple(pltpu.HBM(a.shape, a.dtype) for a in ins + lands) + (SDS((8, LANE), f32),),
        in_specs=[_HBM_SPEC] * (2 * n) + [_SEM_SPEC, _SEM_SPEC, pl.BlockSpec(memory_space=pl.ANY)],
        out_specs=tuple([_HBM_SPEC] * (2 * n)) + (pl.BlockSpec(memory_space=pltpu.VMEM),),
        input_output_aliases={i: i for i in range(2 * n)},
        compiler_params=pltpu.CompilerParams(has_side_effects=_DATAFLOW),
    )(*ins, *lands, send_sems, recv_sems, after)
    return list(res[:n]), list(res[n:2 * n]), res[-1]


def place_own(lands, arrays, scatter, me):
    own = [lax.dynamic_index_in_dim(a, me, 0, keepdims=False) if scatter else a for a in arrays]
    return [lax.dynamic_update_index_in_dim(l, o, me, 0) for l, o in zip(lands, own)]


def ada_forward(a_raw, ada_w, ada_b_loc, name):
    def body(a_ref, w_ref, b_ref, o_ref):
        a = _silu(a_ref[...])
        for l in range(DEPTH):
            o_ref[l] = _dotf(a, w_ref[l]) + b_ref[l]

    return pl.pallas_call(body, out_shape=SDS((DEPTH, 16, ada_w.shape[2]), f32), name=name,
                          compiler_params=_cparams())(a_raw, ada_w, ada_b_loc)


def ada_backward(a_raw, ada_w, dm, name):
    def body(a_ref, w_ref, dm_ref, gw_ref, dcc_ref):
        a = _silu(a_ref[...])
        for l in range(DEPTH):
            gw_ref[l] = _dotf(a, dm_ref[l], (((0,), (0,)), ((), ())))
        dcc_ref[...] = _dotf(dm_ref[0, 8:16, :], w_ref[0], (((1,), (1,)), ((), ())))

    return pl.pallas_call(body, out_shape=(SDS(ada_w.shape, f32), SDS((8, ada_w.shape[1]), f32)), name=name,
                          compiler_params=_cparams())(a_raw, ada_w, dm)


def sum_parts(parts, name):
    _, r, c = parts.shape

    def body(p_ref, o_ref):
        acc = p_ref[0]
        for i in range(1, N_DEV):
            acc = acc + p_ref[i]
        o_ref[...] = acc

    return pl.pallas_call(body, out_shape=SDS((r, c), f32), name=name, compiler_params=_cparams())(parts)


def cctx_grad(parts, c_ctx, name):
    def body(p_ref, c_ref, o_ref):
        acc = p_ref[0, 0:1, :]
        for i in range(1, N_DEV):
            acc = acc + p_ref[i, 0:1, :]
        o_ref[...] = acc * _dsilu(c_ref[...])

    return pl.pallas_call(body, out_shape=SDS((1, c_ctx.shape[1]), f32), name=name, compiler_params=_cparams())(parts, c_ctx)


def _adamw_math(g, w, m, v):
    m = ADAM_B1 * m + (1.0 - ADAM_B1) * g
    v = ADAM_B2 * v + (1.0 - ADAM_B2) * (g * g)
    m_hat = m / (1.0 - ADAM_B1 ** ADAM_STEP)
    v_hat = v / (1.0 - ADAM_B2 ** ADAM_STEP)
    delta = -ADAM_LR * (m_hat / (jnp.sqrt(v_hat) + ADAM_EPS) + ADAM_WD * w)
    return delta, m, v


def adamw(parts, w, m, v, name):
    n, r, c = parts.shape
    tr = _pick(r, (256, 128, 64, 32, 16, 8))

    def body(p_ref, w_ref, m_ref, v_ref, g_ref, d_ref, nm_ref, nv_ref):
        g = p_ref[0].astype(f32)
        for i in range(1, n):
            g = g + p_ref[i].astype(f32)
        g_ref[...] = g
        d_ref[...], nm_ref[...], nv_ref[...] = _adamw_math(g, w_ref[...], m_ref[...], v_ref[...])

    blk = pl.BlockSpec((tr, c), lambda i: (i, 0))
    out = SDS((r, c), f32)
    return pl.pallas_call(
        body, out_shape=(out, out, out, out), grid=(r // tr,),
        in_specs=[pl.BlockSpec((n, tr, c), lambda i: (0, i, 0)), blk, blk, blk], out_specs=(blk, blk, blk, blk),
        name=name, compiler_params=_cparams(("parallel",)),
    )(parts, w, m, v)


def adamw_small(items, name):
    n = len(items)

    def body(*refs):
        ins, outs = refs[:4 * n], refs[4 * n:]
        for i in range(n):
            g, w, m, v = (ins[4 * i + j][...] for j in range(4))
            outs[3 * i][...], outs[3 * i + 1][...], outs[3 * i + 2][...] = _adamw_math(g, w, m, v)

    flat = [a for it in items for a in it]
    out_shape = tuple(SDS(it[1].shape, f32) for it in items for _ in range(3))
    res = pl.pallas_call(body, out_shape=out_shape, name=name, compiler_params=_cparams())(*flat)
    return [tuple(res[3 * i:3 * i + 3]) for i in range(n)]


def _unshard(g, axis):
    loc = g.shape[1:]
    return jnp.moveaxis(g, 0, axis).reshape(loc[:axis] + (N_DEV * loc[axis],) + loc[axis + 1:])


def _shard_major(full, axis):
    s = full.shape
    return jnp.moveaxis(full.reshape(s[:axis] + (N_DEV, s[axis] // N_DEV) + s[axis + 1:]), axis, 0)


def _my_block(full, axis, me):
    n = full.shape[axis] // N_DEV
    return lax.dynamic_slice_in_dim(full, me * n, n, axis)


def _pack(arrays):
    flat = [a.reshape(-1) for a in arrays]
    sizes = [f.shape[0] for f in flat]
    total = sum(sizes)
    padded = -(-total // (8 * LANE)) * (8 * LANE)
    flat.append(jnp.zeros((padded - total,), f32))
    offs = [sum(sizes[:i]) for i in range(len(sizes))]
    return jnp.concatenate(flat).reshape(padded // LANE, LANE), offs


def _pad_rows(w, n):
    return jnp.concatenate([w, jnp.zeros((n - w.shape[0],) + w.shape[1:], w.dtype)], 0)


def _gate_rows(bg):
    return bg[:, :16].reshape(bg.shape[0] // CHUNK, CHUNK, 16).transpose(0, 2, 1)


def _rows(vec, n):
    m = vec.reshape(n, 1, -1)
    return [m[i] for i in range(n)]


def kernel(x, c, ctx, c_ctx, ada_w, ada_b, ln_g, ln_b, even_w_in, even_w_out, gdn_conv_w, gdn_a_log, gdn_dt_bias, gdn_norm_w, pool_w, pool_scale, odd_w_in, odd_w_out, sconv_w, conf_conv_w, conf_ln_g, conf_ln_b, ffn_w_up, ffn_conv_w, ffn_w_down, loss_target, m_c_ctx, m_ada_w, m_ada_b, m_ln_g, m_ln_b, m_even_w_in, m_even_w_out, m_gdn_conv_w, m_gdn_a_log, m_gdn_dt_bias, m_gdn_norm_w, m_pool_w, m_pool_scale, m_odd_w_in, m_odd_w_out, m_sconv_w, m_conf_conv_w, m_conf_ln_g, m_conf_ln_b, m_ffn_w_up, m_ffn_conv_w, m_ffn_w_down, v_c_ctx, v_ada_w, v_ada_b, v_ln_g, v_ln_b, v_even_w_in, v_even_w_out, v_gdn_conv_w, v_gdn_a_log, v_gdn_dt_bias, v_gdn_norm_w, v_pool_w, v_pool_scale, v_odd_w_in, v_odd_w_out, v_sconv_w, v_conf_conv_w, v_conf_ln_g, v_conf_ln_b, v_ffn_w_up, v_ffn_conv_w, v_ffn_w_down):
    weights = dict(c_ctx=c_ctx, ada_w=ada_w, ada_b=ada_b, ln_g=ln_g, ln_b=ln_b, even_w_in=even_w_in, even_w_out=even_w_out, gdn_conv_w=gdn_conv_w, gdn_a_log=gdn_a_log, gdn_dt_bias=gdn_dt_bias, gdn_norm_w=gdn_norm_w, pool_w=pool_w, pool_scale=pool_scale, odd_w_in=odd_w_in, odd_w_out=odd_w_out, sconv_w=sconv_w, conf_conv_w=conf_conv_w, conf_ln_g=conf_ln_g, conf_ln_b=conf_ln_b, ffn_w_up=ffn_w_up, ffn_conv_w=ffn_conv_w, ffn_w_down=ffn_w_down)
    mom1 = dict(c_ctx=m_c_ctx, ada_w=m_ada_w, ada_b=m_ada_b, ln_g=m_ln_g, ln_b=m_ln_b, even_w_in=m_even_w_in, even_w_out=m_even_w_out, gdn_conv_w=m_gdn_conv_w, gdn_a_log=m_gdn_a_log, gdn_dt_bias=m_gdn_dt_bias, gdn_norm_w=m_gdn_norm_w, pool_w=m_pool_w, pool_scale=m_pool_scale, odd_w_in=m_odd_w_in, odd_w_out=m_odd_w_out, sconv_w=m_sconv_w, conf_conv_w=m_conf_conv_w, conf_ln_g=m_conf_ln_g, conf_ln_b=m_conf_ln_b, ffn_w_up=m_ffn_w_up, ffn_conv_w=m_ffn_conv_w, ffn_w_down=m_ffn_w_down)
    mom2 = dict(c_ctx=v_c_ctx, ada_w=v_ada_w, ada_b=v_ada_b, ln_g=v_ln_g, ln_b=v_ln_b, even_w_in=v_even_w_in, even_w_out=v_even_w_out, gdn_conv_w=v_gdn_conv_w, gdn_a_log=v_gdn_a_log, gdn_dt_bias=v_gdn_dt_bias, gdn_norm_w=v_gdn_norm_w, pool_w=v_pool_w, pool_scale=v_pool_scale, odd_w_in=v_odd_w_in, odd_w_out=v_odd_w_out, sconv_w=v_sconv_w, conf_conv_w=v_conf_conv_w, conf_ln_g=v_conf_ln_g, conf_ln_b=v_conf_ln_b, ffn_w_up=v_ffn_w_up, ffn_conv_w=v_ffn_conv_w, ffn_w_down=v_ffn_w_down)
    order = list(weights)
    me = 4 * lax.axis_index("x") + 2 * lax.axis_index("y") + lax.axis_index("c")
    x, ctx, target = x[0], ctx[0], loss_target[0]
    t, d = x.shape
    tc = ctx.shape[0]

    small_in = [ln_g, ln_b, gdn_conv_w, sconv_w, conf_conv_w, ffn_conv_w, c]
    small_axes = [2, 2, 1, 1, 1, 3, 0]
    small_pack, small_offs = _pack(small_in)
    gath = exchange([even_w_in.astype(bf16), small_pack], False, "gather_first")
    e_in = even_w_in.shape[1] * N_DEV
    e_pad = -(-e_in // LANE) * LANE
    win_e = jnp.pad(_unshard(gath[0], 1), ((0, 0), (0, e_pad - e_in)))
    sm = gath[1].reshape(N_DEV, -1)
    lng_f, lnb_f, gconv_f, sconv_f, cconv_f, fconv_f, c_all = [
        _unshard(sm[:, o:o + a.size].reshape((N_DEV,) + a.shape), ax) for a, o, ax in zip(small_in, small_offs, small_axes)]
    gw8 = _pad_rows(gconv_f, 8)
    sw8 = _pad_rows(sconv_f, 8)
    cw32 = _pad_rows(cconv_f, 32)
    fw16 = [_pad_rows(fconv_f[l].reshape(9, D_FF), 16) for l in range(DEPTH)]

    a_raw = jnp.concatenate([c_all, c_ctx[None], jnp.zeros((7, d), f32)], 0)
    ncol = ada_w.shape[2]
    ada_b_loc = lax.dynamic_slice_in_dim(ada_b, me * ncol, ncol, 1)[:, None, :]
    modpart = ada_forward(a_raw, ada_w, ada_b_loc, "ada_forward")
    mod_send = jnp.stack([jnp.transpose(modpart[:, :N_DEV], (1, 0, 2)),
                          jnp.broadcast_to(modpart[:, N_DEV][None], (N_DEV, DEPTH, ncol))], axis=2)
    mod_recv, token = exchange([mod_send], True, "scatter_mod")
    wire_l0 = [even_w_out.astype(bf16) + token[0, 0].astype(bf16), ffn_w_up[0].astype(bf16), ffn_w_down[0].astype(bf16)]
    gather_l0, token = exchange_start(wire_l0, False, "gather_l0_start")
    wire_l1 = [odd_w_in.astype(bf16) + token[0, 0].astype(bf16), odd_w_out.astype(bf16), ffn_w_up[1].astype(bf16),
               ffn_w_down[1].astype(bf16)]
    gather_l1, token = exchange_start(wire_l1, False, "gather_l1_start")
    mod_recv = mod_recv + token[0, 0]
    mod = jnp.transpose(mod_recv[:, :, 0, :], (1, 0, 2)).reshape(DEPTH, 6 * d)
    modc = mod_recv[:, 0, 1, :].reshape(6 * d)
    sh_c, sc_c = modc[None, :d], modc[None, d:2 * d]
    mods = [_rows(mod[l], 6) for l in range(DEPTH)]
    lng = [[lng_f[l, j][None] for j in range(2)] for l in range(DEPTH)]
    lnb = [[lnb_f[l, j][None] for j in range(2)] for l in range(DEPTH)]

    neg_a = jnp.zeros((1, LANE), f32).at[0, 8:16].set(-jnp.exp(gdn_a_log).reshape(8))
    dt_row = jnp.zeros((1, LANE), f32).at[0, 8:16].set(gdn_dt_bias.reshape(8))
    nw_row, ps_row = gdn_norm_w[None], pool_scale[None]
    cg_row, cb_row = conf_ln_g[None], conf_ln_b[None]
    q_scale = GDN_DK ** -0.5

    sh_m, sc_m, gt_m, sh_f, sc_f, gt_f = mods[0]
    u0 = modulate(x, sc_m, sh_m, "mod_l0_mix")
    cu = modulate(ctx, sc_c, sh_c, "mod_ctx")
    p0 = matmul(u0, win_e, "nn", f32, "even_in")
    pc = matmul(cu, win_e, "nn", f32, "even_in_ctx")
    qn = gdn_conv(p0, gw8, 0, 4, q_scale, "gdn_conv_q")
    kn = gdn_conv(p0, gw8, 4, 4, 1.0, "gdn_conv_k")
    vv = gdn_conv(p0, gw8, 8, 4, None, "gdn_conv_v")
    kc = gdn_conv(pc, gw8, 4, 4, 1.0, "gdn_conv_k_ctx")
    vc = gdn_conv(pc, gw8, 8, 4, None, "gdn_conv_v_ctx")
    bg = gdn_gates(p0, neg_a, dt_row, "gdn_gates")
    bgc = gdn_gates(pc, neg_a, dt_row, "gdn_gates_ctx")
    bgt, bgtc = _gate_rows(bg), _gate_rows(bgc)
    zero_state = jnp.zeros((2, GDN_HEADS, LANE, LANE), f32)
    _, _, saved_c, sfin_c = gdn_forward(kc, kc, vc, bgc, bgtc, zero_state, False, "gdn_fwd_ctx")
    o_f, o_b, saved, _ = gdn_forward(qn, kn, vv, bg, bgt, sfin_c, True, "gdn_fwd")
    mix0 = jnp.concatenate([gated_rmsnorm(o_f, o_b, p0, nw_row, "gated_rmsnorm"),
                            pool_mix(p0, pool_w, ps_row, "pool_mix")], 1)
    sent, landed, _ = exchange_wait(gather_l0, mix0, "gather_l0_wait")
    full = place_own(landed, sent, False, me)
    wout_e, wup, wdown = _unshard(full[0], 0), [_unshard(full[1], 1)], [_unshard(full[2], 0)]
    y0 = matmul(mix0, wout_e, "nn", f32, "even_out")
    x1 = res_layernorm(x, y0, gt_m, lng[0][0], lnb[0][0], "resln_l0_mix")
    u1 = modulate(x1, sc_f, sh_f, "mod_l0_ffn")
    h0 = matmul(u1, wup[0], "nn", f32, "ffn_up_l0")
    f0 = ffn_conv(h0, fw16[0], "ffn_conv_l0")
    y0f = matmul(f0, wdown[0], "nn", f32, "ffn_down_l0")
    x2 = res_layernorm(x1, y0f, gt_f, lng[0][1], lnb[0][1], "resln_l0_ffn")

    sh_m1, sc_m1, gt_m1, sh_f1, sc_f1, gt_f1 = mods[1]
    sent, landed, _ = exchange_wait(gather_l1, x2, "gather_l1_wait")
    full = place_own(landed, sent, False, me)
    win_o, wout_o = _unshard(full[0], 1), _unshard(full[1], 0)
    wup.append(_unshard(full[2], 1))
    wdown.append(_unshard(full[3], 0))
    u2 = modulate(x2, sc_m1, sh_m1, "mod_l1_mix")
    p1 = matmul(u2, win_o, "nn", f32, "odd_in")
    zc = conf_conv(p1, cw32, "conf_conv")
    mix1 = jnp.concatenate([short_conv(p1, sw8, "short_conv"), ln_silu(zc, cg_row, cb_row, "conf_ln_silu")], 1)
    y1 = matmul(mix1, wout_o, "nn", f32, "odd_out")
    x3 = res_layernorm(x2, y1, gt_m1, lng[1][0], lnb[1][0], "resln_l1_mix")
    u3 = modulate(x3, sc_f1, sh_f1, "mod_l1_ffn")
    h1 = matmul(u3, wup[1], "nn", f32, "ffn_up_l1")
    f1 = ffn_conv(h1, fw16[1], "ffn_conv_l1")
    y1f = matmul(f1, wdown[1], "nn", f32, "ffn_down_l1")
    x4 = res_layernorm(x3, y1f, gt_f1, lng[1][1], lnb[1][1], "resln_l1_ffn")

    loss_row, dx4 = loss_head(x4, target, "loss_head")
    loss = lax.psum(loss_row[0, 0], ("x", "y", "c"))

    def ffn_backward(dout, x_in, y, gate, g_row, scale, u, h, f, l):
        dxr, dy, dgt, dlg, dlb = res_layernorm_bwd(dout, x_in, y, gate, g_row, f"resln_bwd_l{l}_ffn")
        df = matmul(dy, wdown[l], "nt", f32, f"ffn_down_dgrad_l{l}")
        g_down = matmul(f, dy, "tn", bf16, f"ffn_down_wgrad_l{l}")
        dh, dcw = ffn_conv_bwd(h, fw16[l], df, f"ffn_conv_bwd_l{l}")
        du = matmul(dh, wup[l], "nt", f32, f"ffn_up_dgrad_l{l}")
        g_up = matmul(u, dh, "tn", bf16, f"ffn_up_wgrad_l{l}")
        dx_in, dsc, dsh = modulate_bwd(du, x_in, scale, dxr, f"mod_bwd_l{l}_ffn")
        return dx_in, (dsh, dsc, dgt), (dlg, dlb), dcw, g_up, g_down

    dx3, dmod_f1, dln_f1, dfcw1, g_up1, g_down1 = ffn_backward(dx4, x3, y1f, gt_f1, lng[1][1], sc_f1, u3, h1, f1, 1)

    scatter_a, token = exchange_start([_shard_major(g_up1, 1), _shard_major(g_down1, 0)], True, "scatter_l1_ffn_start")
    gt_m1 = gt_m1 + token[0:1, 0:1]

    dxr, dy, dgt, dlg, dlb = res_layernorm_bwd(dx3, x2, y1, gt_m1, lng[1][0], "resln_bwd_l1_mix")
    dln_m1 = (dlg, dlb)
    dmix = matmul(dy, wout_o, "nt", f32, "odd_out_dgrad")
    g_wout_o = matmul(mix1, dy, "tn", bf16, "odd_out_wgrad")
    dgb, dgc, dhh, d_sconv = short_conv_bwd(p1, sw8, dmix, "short_conv_bwd")
    dzc, d_cg, d_cb = ln_silu_bwd(zc, cg_row, cb_row, dmix, "conf_ln_silu_bwd")
    dga, dgbb, d_cconv = conf_conv_bwd(p1, cw32, dzc, "conf_conv_bwd")
    dp1 = jnp.concatenate([dgb, dgc, dhh, dga, dgbb], 1)
    du = matmul(dp1, win_o, "nt", f32, "odd_in_dgrad")
    g_win_o = matmul(u2, dp1, "tn", bf16, "odd_in_wgrad")
    dx2, dsc, dsh = modulate_bwd(du, x2, sc_m1, dxr, "mod_bwd_l1_mix")
    dmod_m1 = (dsh, dsc, dgt)

    dx1, dmod_f0, dln_f0, dfcw0, g_up0, g_down0 = ffn_backward(dx2, x1, y0f, gt_f, lng[0][1], sc_f, u1, h0, f0, 0)

    scatter_b, token = exchange_start(
        [_shard_major(g_win_o, 1), _shard_major(g_wout_o, 0), _shard_major(g_up0, 1), _shard_major(g_down0, 0)],
        True, "scatter_mid_start")
    gt_m = gt_m + token[0:1, 0:1]

    dxr, dy, dgt, dlg, dlb = res_layernorm_bwd(dx1, x, y0, gt_m, lng[0][0], "resln_bwd_l0_mix")
    dln_m0 = (dlg, dlb)
    dmix = matmul(dy, wout_e, "nt", f32, "even_out_dgrad")
    g_wout_e = matmul(mix0, dy, "tn", bf16, "even_out_wgrad")
    d_o, dgate, d_nw = gated_rmsnorm_bwd(o_f, o_b, p0, nw_row, dmix, "gated_rmsnorm_bwd")
    dpool, d_pw, d_ps = pool_mix_bwd(p0, pool_w, ps_row, dmix, "pool_mix_bwd")
    dq_f, dq_b, dk_f, dk_b, dv_f, dv_b, dbg_f, dbg_b, ds0 = gdn_backward(
        qn, kn, vv, bg, bgt, saved, d_o, zero_state, True, "gdn_bwd")
    _, _, dkc_f, dkc_b, dvc_f, dvc_b, dbgc_f, dbgc_b, _ = gdn_backward(
        kc, kc, vc, bgc, bgtc, saved_c, jnp.zeros((tc, 512), f32), ds0, False, "gdn_bwd_ctx")
    dqp, dwq = gdn_conv_bwd(p0, gw8, dq_f, dq_b, 0, 4, q_scale, "gdn_conv_q_bwd")
    dkp, dwk = gdn_conv_bwd(p0, gw8, dk_f, dk_b, 4, 4, 1.0, "gdn_conv_k_bwd")
    dvp, dwv = gdn_conv_bwd(p0, gw8, dv_f, dv_b, 8, 4, None, "gdn_conv_v_bwd")
    dkcp, dwkc = gdn_conv_bwd(pc, gw8, dkc_f, dkc_b, 4, 4, 1.0, "gdn_conv_k_ctx_bwd")
    dvcp, dwvc = gdn_conv_bwd(pc, gw8, dvc_f, dvc_b, 8, 4, None, "gdn_conv_v_ctx_bwd")
    ds_l, da_l, ddt_l = gdn_gates_bwd(p0, neg_a, dt_row, dbg_f, dbg_b, "gdn_gates_bwd")
    ds_c, da_c, ddt_c = gdn_gates_bwd(pc, neg_a, dt_row, dbgc_f, dbgc_b, "gdn_gates_ctx_bwd")
    zc512 = jnp.zeros((tc, 512), bf16)
    dp0 = jnp.concatenate([dqp, dkp, dvp, dgate, dpool, ds_l], 1)
    dpc = jnp.concatenate([zc512, dkcp, dvcp, zc512, zc512, ds_c], 1)
    du0 = matmul(dp0, win_e, "nt", f32, "even_in_dgrad")
    duc = matmul(dpc, win_e, "nt", f32, "even_in_ctx_dgrad")
    g_win_e = matmul(u0, dp0, "tn", bf16, "even_in_wgrad", init=matmul(cu, dpc, "tn", f32, "even_in_ctx_wgrad"))[:, :e_in]
    scatter_c, token = exchange_start([_shard_major(g_win_e, 1), _shard_major(g_wout_e, 0)], True, "scatter_last_start")
    grad_x, dsc, dsh = modulate_bwd(du0, x, sc_m + token[0:1, 0:1], dxr, "mod_bwd_l0_mix")
    dmod_m0 = (dsh, dsc, dgt)
    _, dsc_c, dsh_c = modulate_bwd(duc, ctx, sc_c, jnp.zeros((tc, d), f32), "mod_bwd_ctx")

    grads, delta, new_m, new_v = {}, {}, {}, {}

    def update(n, parts, w, m, v):
        cols = w.shape[-1]
        out = adamw(parts.reshape(parts.shape[0], -1, cols), w.reshape(-1, cols), m.reshape(-1, cols), v.reshape(-1, cols), f"adamw_{n}")
        return [a.reshape(w.shape) for a in out]

    sent, landed, _ = exchange_wait(scatter_a, grad_x, "scatter_l1_ffn_wait")
    recv_a = place_own(landed, sent, True, me)
    sent, landed, _ = exchange_wait(scatter_b, grad_x, "scatter_mid_wait")
    recv_b = place_own(landed, sent, True, me)
    for n, parts in (("odd_w_in", recv_b[0]), ("odd_w_out", recv_b[1])):
        grads[n], delta[n], new_m[n], new_v[n] = update(n, parts, weights[n], mom1[n], mom2[n])
    for n, per_layer in (("ffn_w_up", (recv_b[2], recv_a[0])), ("ffn_w_down", (recv_b[3], recv_a[1]))):
        outs = [update(f"{n}_l{l}", per_layer[l], weights[n][l], mom1[n][l], mom2[n][l]) for l in range(DEPTH)]
        grads[n], delta[n], new_m[n], new_v[n] = (jnp.stack([outs[l][j] for l in range(DEPTH)]) for j in range(4))
    sent, landed, token = exchange_wait(scatter_c, new_v["ffn_w_down"], "scatter_last_wait")
    recv_c = place_own(landed, sent, True, me)
    for n, parts in (("even_w_in", recv_c[0]), ("even_w_out", recv_c[1])):
        grads[n], delta[n], new_m[n], new_v[n] = update(n, parts, weights[n], mom1[n], mom2[n])

    dmod0 = jnp.concatenate(dmod_m0 + dmod_f0, 1)
    dmod1 = jnp.concatenate(dmod_m1 + dmod_f1, 1)
    dmodc = jnp.concatenate([dsh_c, dsc_c], 1)
    d_gconv = jnp.concatenate([dwq, dwk + dwkc, dwv + dwvc], 1)[:5]
    small_g = [dmod0, dmod1, dmodc,
               jnp.concatenate([dln_m0[0], dln_f0[0], dln_m1[0], dln_f1[0]], 0),
               jnp.concatenate([dln_m0[1], dln_f0[1], dln_m1[1], dln_f1[1]], 0),
               d_gconv, (da_l + da_c)[0, 8:16], (ddt_l + ddt_c)[0, 8:16], d_nw, d_pw, d_ps,
               d_sconv[:3], d_cconv[:31], d_cg, d_cb, jnp.stack([dfcw0[:9], dfcw1[:9]])]
    gpack, goffs = _pack(small_g)
    gparts = exchange([gpack + token[0:1]], False, "gather_small_grads")[0]
    gsum = sum_parts(gparts, "sum_small_grads").reshape(-1)
    gs = [gsum[o:o + a.size].reshape(a.shape) for a, o in zip(small_g, goffs)]
    gflat = gparts.reshape(N_DEV, -1)
    dmodc_cols = _my_block(jnp.pad(gs[2], ((0, 0), (0, 4 * d))), 1, me)
    dm = jnp.stack([
        jnp.concatenate([_my_block(gflat[:, goffs[0]:goffs[0] + 6 * d], 1, me), dmodc_cols, jnp.zeros((7, ncol), f32)], 0),
        jnp.concatenate([_my_block(gflat[:, goffs[1]:goffs[1] + 6 * d], 1, me), jnp.zeros((8, ncol), f32)], 0)])
    g_ada_w, dcc = ada_backward(a_raw, ada_w, dm, "ada_backward")
    g_cctx = cctx_grad(exchange([dcc], False, "gather_cctx")[0], c_ctx[None], "cctx_grad")

    grads["c_ctx"] = g_cctx.reshape(c_ctx.shape)
    grads["ada_b"] = jnp.concatenate([gs[0] + jnp.pad(gs[2], ((0, 0), (0, 4 * d))), gs[1]], 0)
    grads["ln_g"] = _my_block(gs[3].reshape(DEPTH, 2, d), 2, me)
    grads["ln_b"] = _my_block(gs[4].reshape(DEPTH, 2, d), 2, me)
    grads["gdn_conv_w"] = _my_block(gs[5], 1, me)
    grads["gdn_a_log"] = gs[6].reshape(2, GDN_HEADS)
    grads["gdn_dt_bias"] = gs[7].reshape(2, GDN_HEADS)
    grads["gdn_norm_w"] = gs[8].reshape(LANE)
    grads["pool_w"] = gs[9]
    grads["pool_scale"] = gs[10].reshape(-1)
    grads["sconv_w"] = _my_block(gs[11], 1, me)
    grads["conf_conv_w"] = _my_block(gs[12], 1, me)
    grads["conf_ln_g"] = gs[13].reshape(-1)
    grads["conf_ln_b"] = gs[14].reshape(-1)
    grads["ffn_conv_w"] = _my_block(gs[15].reshape(DEPTH, 3, 3, D_FF), 3, me)

    def as2d(a):
        return a.reshape(-1, a.shape[-1]) if a.ndim > 1 else a.reshape(1, -1)

    small_names = [n for n in order if n in grads and n not in delta]
    res = adamw_small([(as2d(grads[n]), as2d(weights[n]), as2d(mom1[n]), as2d(mom2[n])) for n in small_names], "adamw_small")
    for n, (dl, nm, nv) in zip(small_names, res):
        delta[n], new_m[n], new_v[n] = (a.reshape(weights[n].shape) for a in (dl, nm, nv))
    grads["ada_w"], delta["ada_w"], new_m["ada_w"], new_v["ada_w"] = update("ada_w", g_ada_w[None], ada_w, m_ada_w, v_ada_w)

    return (loss, grad_x[None], *[grads[n] for n in order], *[delta[n] for n in order],
            *[new_m[n] for n in order], *[new_v[n] for n in order])
```

```python
import functools
import math

import jax
import jax.numpy as jnp
from jax import lax
from jax.experimental import pallas as pl
from jax.experimental.pallas import tpu as pltpu

f32 = jnp.float32
bf16 = jnp.bfloat16
SDS = jax.ShapeDtypeStruct

N_DEV = 8
D_MODEL = 1024
DEPTH = 2
GRID_W = 64
GDN_HEADS = 4
GDN_DK = 128
CHUNK = 64
POOL_WINDOWS = (2, 4, 8, 16)
D_FF = 2816
ALPHA = (2 * DEPTH) ** 0.25
LN_EPS = 1e-5
RMS_EPS = 1e-6
LANE = 128
PAD_ROWS = 72
CONV_ROWS = 256
VMEM_LIMIT = 56 * 2**20

ADAM_LR, ADAM_B1, ADAM_B2, ADAM_EPS, ADAM_WD, ADAM_STEP = 0.001, 0.9, 0.999, 1e-08, 0.01, 10

HI = lax.Precision.HIGHEST


def _cparams(sem=None):
    return pltpu.CompilerParams(dimension_semantics=sem, vmem_limit_bytes=VMEM_LIMIT)


def _silu(x):
    return x * jax.nn.sigmoid(x)


def _dsilu(x):
    s = jax.nn.sigmoid(x)
    return s * (1.0 + x * (1.0 - s))


def _dotb(a, b, dims=(((1,), (0,)), ((), ()))):
    return lax.dot_general(a.astype(bf16), b.astype(bf16), dims, preferred_element_type=f32)


def _dotb_nt(a, b):
    return _dotb(a, b, (((1,), (1,)), ((), ())))


def _dotb_tn(a, b):
    return _dotb(a, b, (((0,), (0,)), ((), ())))


def _dotf(a, b, dims=(((1,), (0,)), ((), ()))):
    return lax.dot_general(a, b, dims, preferred_element_type=f32, precision=HI)


def _pick(n, cands):
    for c in cands:
        if n % c == 0:
            return c
    return n


def matmul(a, b, mode, out_dtype, name, init=None):
    if mode == "nn":
        (M, K), N = a.shape, b.shape[1]
    elif mode == "nt":
        (M, K), N = a.shape, b.shape[0]
    else:
        (K, M), N = a.shape, b.shape[1]
    tm = _pick(M, (1024, 768, 512, 256, 128)) if mode != "tn" else _pick(M, (1024, 1408, 512, 256, 128))
    tn = _pick(N, (1024, 1408, 896, 768, 640, 512, 384, 256, 128))
    tk = _pick(K, (1024, 1408, 896, 768, 640, 512, 384, 256, 128)) if mode != "tn" else _pick(K, (1024, 512, 256))
    nk = K // tk
    dims = {"nn": (((1,), (0,)), ((), ())), "nt": (((1,), (1,)), ((), ())), "tn": (((0,), (0,)), ((), ()))}[mode]

    def body(a_ref, b_ref, *rest):
        o_ref, acc_ref = rest[-2:]
        k = pl.program_id(2)
        part = lax.dot_general(a_ref[...].astype(bf16), b_ref[...].astype(bf16), dims, preferred_element_type=f32)

        @pl.when(k == 0)
        def _():
            acc_ref[...] = part if init is None else part + rest[0][...]

        @pl.when(k > 0)
        def _():
            acc_ref[...] += part

        @pl.when(k == nk - 1)
        def _():
            o_ref[...] = acc_ref[...].astype(out_dtype)

    a_spec = {"nn": pl.BlockSpec((tm, tk), lambda i, j, k: (i, k)),
              "nt": pl.BlockSpec((tm, tk), lambda i, j, k: (i, k)),
              "tn": pl.BlockSpec((tk, tm), lambda i, j, k: (k, i))}[mode]
    b_spec = {"nn": pl.BlockSpec((tk, tn), lambda i, j, k: (k, j)),
              "nt": pl.BlockSpec((tn, tk), lambda i, j, k: (j, k)),
              "tn": pl.BlockSpec((tk, tn), lambda i, j, k: (k, j))}[mode]
    o_spec = pl.BlockSpec((tm, tn), lambda i, j, k: (i, j))
    return pl.pallas_call(
        body, out_shape=SDS((M, N), out_dtype), grid=(M // tm, N // tn, nk),
        in_specs=[a_spec, b_spec] + ([] if init is None else [o_spec]), out_specs=o_spec,
        scratch_shapes=[pltpu.VMEM((tm, tn), f32)], name=name,
        compiler_params=_cparams(("parallel", "parallel", "arbitrary")),
    )(*((a, b) if init is None else (a, b, init)))


def _row_tile(t):
    return _pick(t, (512, 256, 128, 64, 32, 16, 8))


def _row_spec(tt, d):
    return pl.BlockSpec((tt, d), lambda i: (i, 0))


def _vec_spec(d):
    return pl.BlockSpec((1, d), lambda i: (0, 0))


def _acc_rows(ref, val):
    @pl.when(pl.program_id(0) == 0)
    def _():
        ref[...] = val

    @pl.when(pl.program_id(0) > 0)
    def _():
        ref[...] += val


def modulate(x, scale, shift, name):
    t, d = x.shape
    tt = _row_tile(t)

    def body(x_ref, sc_ref, sh_ref, o_ref):
        o_ref[...] = (x_ref[...] * (1.0 + sc_ref[...]) + sh_ref[...]).astype(bf16)

    return pl.pallas_call(
        body, out_shape=SDS((t, d), bf16), grid=(t // tt,),
        in_specs=[_row_spec(tt, d), _vec_spec(d), _vec_spec(d)], out_specs=_row_spec(tt, d),
        name=name, compiler_params=_cparams(("parallel",)),
    )(x, scale, shift)


def modulate_bwd(du, x, scale, dres, name, du_row0=0):
    t, d = x.shape
    tt = _row_tile(t)
    blk0 = du_row0 // tt

    def body(du_ref, x_ref, sc_ref, dres_ref, dx_ref, dsc_ref, dsh_ref):
        du_v = du_ref[...]
        dx_ref[...] = du_v * (1.0 + sc_ref[...]) + dres_ref[...]
        _acc_rows(dsc_ref, jnp.sum(du_v * x_ref[...], axis=0, keepdims=True))
        _acc_rows(dsh_ref, jnp.sum(du_v, axis=0, keepdims=True))

    return pl.pallas_call(
        body, out_shape=(SDS((t, d), f32), SDS((1, d), f32), SDS((1, d), f32)), grid=(t // tt,),
        in_specs=[pl.BlockSpec((tt, d), lambda i: (i + blk0, 0)), _row_spec(tt, d), _vec_spec(d), _row_spec(tt, d)],
        out_specs=(_row_spec(tt, d), _vec_spec(d), _vec_spec(d)),
        name=name, compiler_params=_cparams(("arbitrary",)),
    )(du, x, scale, dres)


def _ln_stats(z):
    mu = jnp.mean(z, axis=-1, keepdims=True)
    zc = z - mu
    var = jnp.mean(zc * zc, axis=-1, keepdims=True)
    rstd = lax.rsqrt(var + LN_EPS)
    return zc * rstd, rstd


def _ln_bwd(dxhat, xhat, rstd):
    m1 = jnp.mean(dxhat, axis=-1, keepdims=True)
    m2 = jnp.mean(dxhat * xhat, axis=-1, keepdims=True)
    return rstd * (dxhat - m1 - xhat * m2)


def res_layernorm(x, y, gate, g, b, name, scale=None, shift=None):
    t, d = x.shape
    tt = _row_tile(t)
    with_mod = scale is not None

    def body(x_ref, y_ref, gt_ref, g_ref, b_ref, *rest):
        xhat, _ = _ln_stats(ALPHA * x_ref[...] + gt_ref[...] * y_ref[...])
        out = xhat * g_ref[...] + b_ref[...]
        if with_mod:
            sc_ref, sh_ref, o_ref, u_ref = rest
            u_ref[...] = (out * (1.0 + sc_ref[...]) + sh_ref[...]).astype(bf16)
        else:
            o_ref, = rest
        o_ref[...] = out

    rows, vec = _row_spec(tt, d), _vec_spec(d)
    return pl.pallas_call(
        body, out_shape=(SDS((t, d), f32), SDS((t, d), bf16)) if with_mod else SDS((t, d), f32), grid=(t // tt,),
        in_specs=[rows, rows, vec, vec, vec] + ([vec, vec] if with_mod else []),
        out_specs=(rows, rows) if with_mod else rows, name=name, compiler_params=_cparams(("parallel",)),
    )(*((x, y, gate, g, b) + ((scale, shift) if with_mod else ())))


def res_layernorm_bwd(dout, x, y, gate, g, name):
    t, d = x.shape
    tt = _row_tile(t)

    def body(do_ref, x_ref, y_ref, gt_ref, g_ref, dxr_ref, dy_ref, dgt_ref, dg_ref, db_ref):
        y_v = y_ref[...]
        do_v = do_ref[...]
        xhat, rstd = _ln_stats(ALPHA * x_ref[...] + gt_ref[...] * y_v)
        dz = _ln_bwd(do_v * g_ref[...], xhat, rstd)
        dxr_ref[...] = ALPHA * dz
        dy_ref[...] = (gt_ref[...] * dz).astype(bf16)
        _acc_rows(dgt_ref, jnp.sum(dz * y_v, axis=0, keepdims=True))
        _acc_rows(dg_ref, jnp.sum(do_v * xhat, axis=0, keepdims=True))
        _acc_rows(db_ref, jnp.sum(do_v, axis=0, keepdims=True))

    vec = SDS((1, d), f32)
    return pl.pallas_call(
        body, out_shape=(SDS((t, d), f32), SDS((t, d), bf16), vec, vec, vec), grid=(t // tt,),
        in_specs=[_row_spec(tt, d), _row_spec(tt, d), _row_spec(tt, d), _vec_spec(d), _vec_spec(d)],
        out_specs=(_row_spec(tt, d), _row_spec(tt, d), _vec_spec(d), _vec_spec(d), _vec_spec(d)),
        name=name, compiler_params=_cparams(("arbitrary",)),
    )(dout, x, y, gate, g)


def modulate_res_layernorm_bwd(du, scale, dres, x, y, gate, g, b, name):
    t, d = x.shape
    tt = _row_tile(t)

    def body(du_ref, sc_ref, dres_ref, x_ref, y_ref, gt_ref, g_ref, b_ref,
             dxr_ref, dy_ref, dsc_ref, dsh_ref, dgt_ref, dg_ref, db_ref):
        y_v, du_v = y_ref[...], du_ref[...]
        xhat, rstd = _ln_stats(ALPHA * x_ref[...] + gt_ref[...] * y_v)
        do_v = du_v * (1.0 + sc_ref[...]) + dres_ref[...]
        dz = _ln_bwd(do_v * g_ref[...], xhat, rstd)
        dxr_ref[...] = ALPHA * dz
        dy_ref[...] = (gt_ref[...] * dz).astype(bf16)
        _acc_rows(dsc_ref, jnp.sum(du_v * (xhat * g_ref[...] + b_ref[...]), axis=0, keepdims=True))
        _acc_rows(dsh_ref, jnp.sum(du_v, axis=0, keepdims=True))
        _acc_rows(dgt_ref, jnp.sum(dz * y_v, axis=0, keepdims=True))
        _acc_rows(dg_ref, jnp.sum(do_v * xhat, axis=0, keepdims=True))
        _acc_rows(db_ref, jnp.sum(do_v, axis=0, keepdims=True))

    rows, vec, vshape = _row_spec(tt, d), _vec_spec(d), SDS((1, d), f32)
    return pl.pallas_call(
        body, out_shape=(SDS((t, d), f32), SDS((t, d), bf16)) + (vshape,) * 5, grid=(t // tt,),
        in_specs=[rows, vec, rows, rows, rows, vec, vec, vec], out_specs=(rows, rows) + (vec,) * 5,
        name=name, compiler_params=_cparams(("arbitrary",)),
    )(du, scale, dres, x, y, gate, g, b)


def loss_head(y, target, name):
    t, d = y.shape
    tt = _row_tile(t)

    def body(y_ref, t_ref, l_ref, dy_ref):
        e = y_ref[...] - t_ref[...]
        dy_ref[...] = e * (1.0 / d)
        part = jnp.sum(jnp.sum(e * e, axis=1, keepdims=True), axis=0, keepdims=True) * (0.5 / d)
        _acc_rows(l_ref, jnp.broadcast_to(part, (1, LANE)))

    return pl.pallas_call(
        body, out_shape=(SDS((1, LANE), f32), SDS((t, d), f32)), grid=(t // tt,),
        in_specs=[_row_spec(tt, d), _row_spec(tt, d)],
        out_specs=(pl.BlockSpec((1, LANE), lambda i: (0, 0)), _row_spec(tt, d)),
        name=name, compiler_params=_cparams(("arbitrary",)),
    )(y, target)


def _fill_pad(pad_ref, val, t):
    zeros = jnp.zeros((PAD_ROWS, LANE), f32)
    pad_ref[0:PAD_ROWS, :] = zeros
    pad_ref[PAD_ROWS + t:2 * PAD_ROWS + t, :] = zeros
    pad_ref[PAD_ROWS:PAD_ROWS + t, :] = val


def _grid_pads_set(pads, r0, val):
    rows = val.shape[0]
    col = (lax.broadcasted_iota(jnp.int32, (rows, 1), 0) + r0) % GRID_W
    base = PAD_ROWS + r0
    pads[0][base + 1:base + 1 + rows, :] = val * (col <= GRID_W - 2).astype(f32)
    pads[1][base:base + rows, :] = val
    pads[2][base - 1:base - 1 + rows, :] = val * (col >= 1).astype(f32)


def _grid_pads_clear_edges(pads, t):
    zeros = jnp.zeros((PAD_ROWS + 8, LANE), f32)
    for p in pads:
        p[0:PAD_ROWS + 8, :] = zeros
        p[PAD_ROWS + t - 8:2 * PAD_ROWS + t, :] = zeros


def _tap_source(pads, dc):
    return pads if dc is None else pads[dc + 1]


def _taps_apply(pads, w_ref, taps, r0, rows):
    acc = jnp.zeros((rows, LANE), f32)
    for off, dc, wi in taps:
        xs = _tap_source(pads, dc)[PAD_ROWS + r0 + off:PAD_ROWS + r0 + off + rows, :]
        acc = acc + w_ref[wi:wi + 1, :] * xs
    return acc


def _taps_wgrad(pads, dy, taps, r0, rows, nw):
    out = jnp.zeros((nw, LANE), f32)
    rid = lax.broadcasted_iota(jnp.int32, (nw, 1), 0)
    for off, dc, wi in taps:
        xs = _tap_source(pads, dc)[PAD_ROWS + r0 + off:PAD_ROWS + r0 + off + rows, :]
        s = jnp.sum(dy * xs, axis=0, keepdims=True)
        out = out + jnp.where(rid == wi, s, 0.0)
    return out


def _transpose_taps(taps):
    return [(-off, None if dc is None else -dc, wi) for off, dc, wi in taps]


def _taps_1d(width):
    return [(j - width // 2, None, j) for j in range(width)]


def _taps_grid3():
    return [(GRID_W * dr, dc, 3 * (dr + 1) + (dc + 1)) for dr in (-1, 0, 1) for dc in (-1, 0, 1)]


def _row_chunks(t):
    r = min(CONV_ROWS, t)
    return [(i * r, r) for i in range(t // r)]


def _col_spec(t, off):
    return pl.BlockSpec((t, LANE), lambda c: (0, c + off))


def _w_spec(nw, off=0):
    return pl.BlockSpec((nw, LANE), lambda c: (0, c + off))


def gdn_conv(p, w, col0, nblk, norm_scale, name):
    t = p.shape[0]
    nw = w.shape[0]
    taps = _taps_1d(5)

    def body(p_ref, w_ref, o_ref, pad_ref):
        _fill_pad(pad_ref, p_ref[...], t)
        for r0, rows in _row_chunks(t):
            a = _silu(_taps_apply(pad_ref, w_ref, taps, r0, rows))
            if norm_scale is not None:
                a = a * (lax.rsqrt(jnp.sum(a * a, axis=-1, keepdims=True) + RMS_EPS) * norm_scale)
            o_ref[r0:r0 + rows, :] = a

    return pl.pallas_call(
        body, out_shape=SDS((t, nblk * LANE), f32), grid=(nblk,),
        in_specs=[_col_spec(t, col0), _w_spec(nw, col0)], out_specs=_col_spec(t, 0),
        scratch_shapes=[pltpu.VMEM((t + 2 * PAD_ROWS, LANE), f32)], name=name,
        compiler_params=_cparams(("parallel",)),
    )(p, w)


def gdn_conv_bwd(p, w, d_a, d_b, col0, nblk, norm_scale, name):
    t = p.shape[0]
    nw = w.shape[0]
    taps = _taps_1d(5)
    ttaps = _transpose_taps(taps)

    def body(p_ref, w_ref, da_ref, db_ref, dp_ref, dw_ref, pad_ref, gpad_ref):
        _fill_pad(pad_ref, p_ref[...], t)
        for r0, rows in _row_chunks(t):
            pre = _taps_apply(pad_ref, w_ref, taps, r0, rows)
            a = _silu(pre)
            dy = da_ref[r0:r0 + rows, :] + db_ref[r0:r0 + rows, :]
            if norm_scale is not None:
                r = lax.rsqrt(jnp.sum(a * a, axis=-1, keepdims=True) + RMS_EPS)
                da = norm_scale * (dy * r - a * (r * r * r) * jnp.sum(dy * a, axis=-1, keepdims=True))
            else:
                da = dy
            gpad_ref[PAD_ROWS + r0:PAD_ROWS + r0 + rows, :] = da * _dsilu(pre)
        zeros = jnp.zeros((PAD_ROWS, LANE), f32)
        gpad_ref[0:PAD_ROWS, :] = zeros
        gpad_ref[PAD_ROWS + t:2 * PAD_ROWS + t, :] = zeros
        dw = jnp.zeros((nw, LANE), f32)
        for r0, rows in _row_chunks(t):
            dp_ref[r0:r0 + rows, :] = _taps_apply(gpad_ref, w_ref, ttaps, r0, rows).astype(bf16)
            dw = dw + _taps_wgrad(pad_ref, gpad_ref[PAD_ROWS + r0:PAD_ROWS + r0 + rows, :], taps, r0, rows, nw)
        dw_ref[...] = dw

    return pl.pallas_call(
        body, out_shape=(SDS((t, nblk * LANE), bf16), SDS((nw, nblk * LANE), f32)), grid=(nblk,),
        in_specs=[_col_spec(t, col0), _w_spec(nw, col0), _col_spec(t, 0), _col_spec(t, 0)],
        out_specs=(_col_spec(t, 0), _w_spec(nw)),
        scratch_shapes=[pltpu.VMEM((t + 2 * PAD_ROWS, LANE), f32)] * 2, name=name,
        compiler_params=_cparams(("parallel",)),
    )(p, w, d_a, d_b)


def short_conv(p, w, name):
    t = p.shape[0]
    nw = w.shape[0]
    taps = _taps_1d(3)

    def body(gb_ref, gc_ref, h_ref, w_ref, o_ref, pad_ref):
        _fill_pad(pad_ref, gc_ref[...] * h_ref[...], t)
        for r0, rows in _row_chunks(t):
            o_ref[r0:r0 + rows, :] = (gb_ref[r0:r0 + rows, :] * _taps_apply(pad_ref, w_ref, taps, r0, rows)).astype(bf16)

    return pl.pallas_call(
        body, out_shape=SDS((t, 4 * LANE), bf16), grid=(4,),
        in_specs=[_col_spec(t, 0), _col_spec(t, 4), _col_spec(t, 8), _w_spec(nw)], out_specs=_col_spec(t, 0),
        scratch_shapes=[pltpu.VMEM((t + 2 * PAD_ROWS, LANE), f32)], name=name,
        compiler_params=_cparams(("parallel",)),
    )(p, p, p, w)


def short_conv_bwd(p, w, dy, name):
    t = p.shape[0]
    nw = w.shape[0]
    taps = _taps_1d(3)
    ttaps = _transpose_taps(taps)

    def body(gb_ref, gc_ref, h_ref, w_ref, dy_ref, dgb_ref, dgc_ref, dh_ref, dw_ref, pad_ref, gpad_ref):
        _fill_pad(pad_ref, gc_ref[...] * h_ref[...], t)
        _fill_pad(gpad_ref, dy_ref[...] * gb_ref[...], t)
        dw = jnp.zeros((nw, LANE), f32)
        for r0, rows in _row_chunks(t):
            sl = slice(r0, r0 + rows)
            dgb_ref[sl, :] = (dy_ref[sl, :] * _taps_apply(pad_ref, w_ref, taps, r0, rows)).astype(bf16)
            dm = _taps_apply(gpad_ref, w_ref, ttaps, r0, rows)
            dgc_ref[sl, :] = (dm * h_ref[sl, :]).astype(bf16)
            dh_ref[sl, :] = (dm * gc_ref[sl, :]).astype(bf16)
            dw = dw + _taps_wgrad(pad_ref, gpad_ref[PAD_ROWS + r0:PAD_ROWS + r0 + rows, :], taps, r0, rows, nw)
        dw_ref[...] = dw

    blk = SDS((t, 4 * LANE), bf16)
    return pl.pallas_call(
        body, out_shape=(blk, blk, blk, SDS((nw, 4 * LANE), f32)), grid=(4,),
        in_specs=[_col_spec(t, 0), _col_spec(t, 4), _col_spec(t, 8), _w_spec(nw), _col_spec(t, 0)],
        out_specs=(_col_spec(t, 0), _col_spec(t, 0), _col_spec(t, 0), _w_spec(nw)),
        scratch_shapes=[pltpu.VMEM((t + 2 * PAD_ROWS, LANE), f32)] * 2, name=name,
        compiler_params=_cparams(("parallel",)),
    )(p, p, p, w, dy)


def conf_conv(p, w, name):
    t = p.shape[0]
    nw = w.shape[0]
    taps = _taps_1d(31)

    def body(a_ref, b_ref, w_ref, o_ref, pad_ref):
        _fill_pad(pad_ref, a_ref[...] * jax.nn.sigmoid(b_ref[...]), t)
        for r0, rows in _row_chunks(t):
            o_ref[r0:r0 + rows, :] = _taps_apply(pad_ref, w_ref, taps, r0, rows)

    return pl.pallas_call(
        body, out_shape=SDS((t, 4 * LANE), f32), grid=(4,),
        in_specs=[_col_spec(t, 12), _col_spec(t, 16), _w_spec(nw)], out_specs=_col_spec(t, 0),
        scratch_shapes=[pltpu.VMEM((t + 2 * PAD_ROWS, LANE), f32)], name=name,
        compiler_params=_cparams(("parallel",)),
    )(p, p, w)


def conf_conv_bwd(p, w, dz, name):
    t = p.shape[0]
    nw = w.shape[0]
    taps = _taps_1d(31)
    ttaps = _transpose_taps(taps)

    def body(a_ref, b_ref, w_ref, dz_ref, da_ref, db_ref, dw_ref, pad_ref, gpad_ref):
        _fill_pad(pad_ref, a_ref[...] * jax.nn.sigmoid(b_ref[...]), t)
        _fill_pad(gpad_ref, dz_ref[...], t)
        dw = jnp.zeros((nw, LANE), f32)
        for r0, rows in _row_chunks(t):
            sl = slice(r0, r0 + rows)
            dm = _taps_apply(gpad_ref, w_ref, ttaps, r0, rows)
            sg = jax.nn.sigmoid(b_ref[sl, :])
            da_ref[sl, :] = (dm * sg).astype(bf16)
            db_ref[sl, :] = (dm * a_ref[sl, :] * sg * (1.0 - sg)).astype(bf16)
            dw = dw + _taps_wgrad(pad_ref, dz_ref[sl, :], taps, r0, rows, nw)
        dw_ref[...] = dw

    blk = SDS((t, 4 * LANE), bf16)
    return pl.pallas_call(
        body, out_shape=(blk, blk, SDS((nw, 4 * LANE), f32)), grid=(4,),
        in_specs=[_col_spec(t, 12), _col_spec(t, 16), _w_spec(nw), _col_spec(t, 0)],
        out_specs=(_col_spec(t, 0), _col_spec(t, 0), _w_spec(nw)),
        scratch_shapes=[pltpu.VMEM((t + 2 * PAD_ROWS, LANE), f32)] * 2, name=name,
        compiler_params=_cparams(("parallel",)),
    )(p, p, w, dz)


def ffn_conv(h, w, name):
    t = h.shape[0]
    width = 2 * LANE
    nblk = D_FF // width
    nw = w.shape[0]
    taps = _taps_grid3()

    def body(a_ref, g_ref, w_ref, o_ref, *pads):
        for s in range(width // LANE):
            ls = slice(s * LANE, (s + 1) * LANE)
            _grid_pads_clear_edges(pads, t)
            for r0, rows in _row_chunks(t):
                _grid_pads_set(pads, r0, a_ref[r0:r0 + rows, ls])
            for r0, rows in _row_chunks(t):
                conv = _taps_apply(pads, w_ref.at[:, ls], taps, r0, rows)
                o_ref[r0:r0 + rows, ls] = (_silu(conv) * g_ref[r0:r0 + rows, ls]).astype(bf16)

    spec = lambda off: pl.BlockSpec((t, width), lambda c: (0, c + off))
    return pl.pallas_call(
        body, out_shape=SDS((t, D_FF), bf16), grid=(nblk,),
        in_specs=[spec(0), spec(nblk), pl.BlockSpec((nw, width), lambda c: (0, c))], out_specs=spec(0),
        scratch_shapes=[pltpu.VMEM((t + 2 * PAD_ROWS, LANE), f32)] * 3, name=name,
        compiler_params=_cparams(("parallel",)),
    )(h, h, w)


def ffn_conv_bwd(h, w, df, name):
    t = h.shape[0]
    nblk = D_FF // LANE
    nw = w.shape[0]
    taps = _taps_grid3()
    ttaps = _transpose_taps(taps)

    def body(a_ref, g_ref, w_ref, df_ref, dh_ref, dw_ref, *all_pads):
        half = pl.program_id(1)
        pads, gpads = all_pads[:3], all_pads[3:]

        @pl.when(half == 0)
        def _():
            _grid_pads_clear_edges(all_pads, t)
            for r0, rows in _row_chunks(t):
                _grid_pads_set(pads, r0, a_ref[r0:r0 + rows, :])
            for r0, rows in _row_chunks(t):
                sl = slice(r0, r0 + rows)
                pre = _taps_apply(pads, w_ref, taps, r0, rows)
                _grid_pads_set(gpads, r0, df_ref[sl, :] * g_ref[sl, :] * _dsilu(pre))
                dh_ref[sl, :] = (df_ref[sl, :] * _silu(pre)).astype(bf16)

        @pl.when(half == 1)
        def _():
            dw = jnp.zeros((nw, LANE), f32)
            for r0, rows in _row_chunks(t):
                dh_ref[r0:r0 + rows, :] = _taps_apply(gpads, w_ref, ttaps, r0, rows).astype(bf16)
                dw = dw + _taps_wgrad(pads, gpads[1][PAD_ROWS + r0:PAD_ROWS + r0 + rows, :], taps, r0, rows, nw)
            dw_ref[...] = dw

    cspec = lambda off: pl.BlockSpec((t, LANE), lambda c, s: (0, c + off))
    return pl.pallas_call(
        body, out_shape=(SDS((t, 2 * D_FF), bf16), SDS((nw, D_FF), f32)), grid=(nblk, 2),
        in_specs=[cspec(0), cspec(nblk), pl.BlockSpec((nw, LANE), lambda c, s: (0, c)), cspec(0)],
        out_specs=(pl.BlockSpec((t, LANE), lambda c, s: (0, c + nblk * (1 - s))), pl.BlockSpec((nw, LANE), lambda c, s: (0, c))),
        scratch_shapes=[pltpu.VMEM((t + 2 * PAD_ROWS, LANE), f32)] * 6, name=name,
        compiler_params=_cparams(("parallel", "arbitrary")),
    )(h, h, w, df)


def _pool_count(r0, rows, win, t):
    pos = lax.broadcasted_iota(jnp.int32, (rows, 1), 0) + r0
    lo = jnp.clip(pos - win // 2, 0, t)
    hi = jnp.clip(pos - win // 2 + win, 0, t)
    return (hi - lo).astype(f32)


def _window_sum(pad_ref, r0, rows, lo, hi):
    acc = jnp.zeros((rows, LANE), f32)
    for off in range(lo, hi):
        acc = acc + pad_ref[PAD_ROWS + r0 + off:PAD_ROWS + r0 + off + rows, :]
    return acc


def pool_mix(p, pool_w, pool_scale, name):
    t = p.shape[0]

    def body(x_ref, w_ref, s_ref, o_ref, pad_ref):
        for gi, win in enumerate(POOL_WINDOWS):
            cs = slice(gi * LANE, (gi + 1) * LANE)
            _fill_pad(pad_ref, x_ref[:, cs], t)
            wg = w_ref[gi].astype(bf16)
            for r0, rows in _row_chunks(t):
                pooled = _window_sum(pad_ref, r0, rows, -(win // 2), win - win // 2) / _pool_count(r0, rows, win, t) - x_ref[r0:r0 + rows, cs]
                o_ref[r0:r0 + rows, cs] = (_dotb(pooled, wg) * s_ref[:, cs]).astype(bf16)

    return pl.pallas_call(
        body, out_shape=SDS((t, 512), bf16), grid=(1,),
        in_specs=[pl.BlockSpec((t, 512), lambda i: (0, 4)), pl.BlockSpec((4, LANE, LANE), lambda i: (0, 0, 0)),
                  pl.BlockSpec((1, 512), lambda i: (0, 0))],
        out_specs=pl.BlockSpec((t, 512), lambda i: (0, 0)),
        scratch_shapes=[pltpu.VMEM((t + 2 * PAD_ROWS, LANE), f32)], name=name,
        compiler_params=_cparams(("arbitrary",)),
    )(p, pool_w, pool_scale)


def pool_mix_bwd(p, pool_w, pool_scale, dmix, name):
    t = p.shape[0]

    def body(x_ref, w_ref, s_ref, dy_ref, dp_ref, dw_ref, ds_ref, pad_ref, gpad_ref, dpool_ref):
        for gi, win in enumerate(POOL_WINDOWS):
            cs = slice(gi * LANE, (gi + 1) * LANE)
            h = win // 2
            _fill_pad(pad_ref, x_ref[:, cs], t)
            wg = w_ref[gi].astype(bf16)
            dw = jnp.zeros((LANE, LANE), f32)
            ds = jnp.zeros((1, LANE), f32)
            zeros = jnp.zeros((PAD_ROWS, LANE), f32)
            gpad_ref[0:PAD_ROWS, :] = zeros
            gpad_ref[PAD_ROWS + t:2 * PAD_ROWS + t, :] = zeros
            for r0, rows in _row_chunks(t):
                cnt = _pool_count(r0, rows, win, t)
                pooled = _window_sum(pad_ref, r0, rows, -h, win - h) / cnt - x_ref[r0:r0 + rows, cs]
                dy = dy_ref[r0:r0 + rows, cs]
                ds = ds + jnp.sum(dy * _dotb(pooled, wg), axis=0, keepdims=True)
                dypre = dy * s_ref[:, cs]
                dw = dw + _dotb_tn(pooled, dypre)
                dpooled = _dotb_nt(dypre, wg)
                gpad_ref[PAD_ROWS + r0:PAD_ROWS + r0 + rows, :] = dpooled / cnt
                dpool_ref[r0:r0 + rows, :] = dpooled
            dw_ref[gi] = dw
            ds_ref[:, cs] = ds
            for r0, rows in _row_chunks(t):
                dx = _window_sum(gpad_ref, r0, rows, -h + 1, h + 1) - dpool_ref[r0:r0 + rows, :]
                dp_ref[r0:r0 + rows, cs] = dx.astype(bf16)

    return pl.pallas_call(
        body, out_shape=(SDS((t, 512), bf16), SDS((4, LANE, LANE), f32), SDS((1, 512), f32)), grid=(1,),
        in_specs=[pl.BlockSpec((t, 512), lambda i: (0, 4)), pl.BlockSpec((4, LANE, LANE), lambda i: (0, 0, 0)),
                  pl.BlockSpec((1, 512), lambda i: (0, 0)), pl.BlockSpec((t, 512), lambda i: (0, 1))],
        out_specs=(pl.BlockSpec((t, 512), lambda i: (0, 0)), pl.BlockSpec((4, LANE, LANE), lambda i: (0, 0, 0)),
                   pl.BlockSpec((1, 512), lambda i: (0, 0))),
        scratch_shapes=[pltpu.VMEM((t + 2 * PAD_ROWS, LANE), f32)] * 2 + [pltpu.VMEM((t, LANE), f32)], name=name,
        compiler_params=_cparams(("arbitrary",)),
    )(p, pool_w, pool_scale, dmix)


def gated_rmsnorm(o_a, o_b, p, norm_w, name):
    t = o_a.shape[0]
    tt = _row_tile(t)

    def body(oa_ref, ob_ref, g_ref, nw_ref, y_ref):
        for h in range(GDN_HEADS):
            cs = slice(h * LANE, (h + 1) * LANE)
            o = oa_ref[:, cs] + ob_ref[:, cs]
            r = lax.rsqrt(jnp.mean(o * o, axis=-1, keepdims=True) + RMS_EPS)
            y_ref[:, cs] = (o * r * nw_ref[...] * _silu(g_ref[:, cs])).astype(bf16)

    return pl.pallas_call(
        body, out_shape=SDS((t, 512), bf16), grid=(t // tt,),
        in_specs=[_row_spec(tt, 512), _row_spec(tt, 512), pl.BlockSpec((tt, 512), lambda i: (i, 3)), _vec_spec(LANE)],
        out_specs=_row_spec(tt, 512), name=name, compiler_params=_cparams(("parallel",)),
    )(o_a, o_b, p, norm_w)


def gated_rmsnorm_bwd(o_a, o_b, p, norm_w, dmix, name):
    t = o_a.shape[0]
    tt = _row_tile(t)

    def body(oa_ref, ob_ref, g_ref, nw_ref, dy_ref, do_ref, dg_ref, dnw_ref):
        dnw = jnp.zeros((1, LANE), f32)
        for h in range(GDN_HEADS):
            cs = slice(h * LANE, (h + 1) * LANE)
            o = oa_ref[:, cs] + ob_ref[:, cs]
            r = lax.rsqrt(jnp.mean(o * o, axis=-1, keepdims=True) + RMS_EPS)
            gate = g_ref[:, cs]
            dy = dy_ref[:, cs]
            dy1 = dy * _silu(gate)
            dg_ref[:, cs] = (dy * (o * r * nw_ref[...]) * _dsilu(gate)).astype(bf16)
            dnw = dnw + jnp.sum(dy1 * o * r, axis=0, keepdims=True)
            dn = dy1 * nw_ref[...]
            do_ref[:, cs] = r * dn - o * (r * r * r) * jnp.mean(dn * o, axis=-1, keepdims=True)
        _acc_rows(dnw_ref, dnw)

    return pl.pallas_call(
        body, out_shape=(SDS((t, 512), f32), SDS((t, 512), bf16), SDS((1, LANE), f32)), grid=(t // tt,),
        in_specs=[_row_spec(tt, 512), _row_spec(tt, 512), pl.BlockSpec((tt, 512), lambda i: (i, 3)), _vec_spec(LANE),
                  _row_spec(tt, 512)],
        out_specs=(_row_spec(tt, 512), _row_spec(tt, 512), _vec_spec(LANE)),
        name=name, compiler_params=_cparams(("arbitrary",)),
    )(o_a, o_b, p, norm_w, dmix)


def ln_silu(z, g, b, name):
    t, d = z.shape
    tt = _row_tile(t)

    def body(z_ref, g_ref, b_ref, o_ref):
        xhat, _ = _ln_stats(z_ref[...])
        o_ref[...] = _silu(xhat * g_ref[...] + b_ref[...]).astype(bf16)

    return pl.pallas_call(
        body, out_shape=SDS((t, d), bf16), grid=(t // tt,),
        in_specs=[_row_spec(tt, d), _vec_spec(d), _vec_spec(d)], out_specs=_row_spec(tt, d),
        name=name, compiler_params=_cparams(("parallel",)),
    )(z, g, b)


def ln_silu_bwd(z, g, b, dmix, name):
    t, d = z.shape
    tt = _row_tile(t)

    def body(z_ref, g_ref, b_ref, dy_ref, dz_ref, dg_ref, db_ref):
        xhat, rstd = _ln_stats(z_ref[...])
        dn = dy_ref[...] * _dsilu(xhat * g_ref[...] + b_ref[...])
        dz_ref[...] = _ln_bwd(dn * g_ref[...], xhat, rstd)
        _acc_rows(dg_ref, jnp.sum(dn * xhat, axis=0, keepdims=True))
        _acc_rows(db_ref, jnp.sum(dn, axis=0, keepdims=True))

    return pl.pallas_call(
        body, out_shape=(SDS((t, d), f32), SDS((1, d), f32), SDS((1, d), f32)), grid=(t // tt,),
        in_specs=[_row_spec(tt, d), _vec_spec(d), _vec_spec(d), pl.BlockSpec((tt, d), lambda i: (i, 1))],
        out_specs=(_row_spec(tt, d), _vec_spec(d), _vec_spec(d)),
        name=name, compiler_params=_cparams(("arbitrary",)),
    )(z, g, b, dmix)


def gdn_gates(p, neg_a, dt_bias, name):
    t = p.shape[0]
    tt = _row_tile(t)

    def body(s_ref, na_ref, dt_ref, o_ref):
        s = s_ref[...]
        col = lax.broadcasted_iota(jnp.int32, s.shape, 1)
        o_ref[...] = jnp.where(col < 8, jax.nn.sigmoid(s), na_ref[...] * jax.nn.softplus(s + dt_ref[...]))

    return pl.pallas_call(
        body, out_shape=SDS((t, LANE), f32), grid=(t // tt,),
        in_specs=[pl.BlockSpec((tt, LANE), lambda i: (i, 20)), _vec_spec(LANE), _vec_spec(LANE)],
        out_specs=_row_spec(tt, LANE), name=name, compiler_params=_cparams(("parallel",)),
    )(p, neg_a, dt_bias)


def gdn_gates_bwd(p, neg_a, dt_bias, dbg_a, dbg_b, name):
    t = p.shape[0]
    tt = _row_tile(t)

    def body(s_ref, na_ref, dt_ref, d_ref, d2_ref, ds_ref, da_ref, ddt_ref):
        s = s_ref[...]
        d = d_ref[...] + d2_ref[...]
        col = lax.broadcasted_iota(jnp.int32, s.shape, 1)
        sg = jax.nn.sigmoid(s)
        z = s + dt_ref[...]
        dz = jnp.where((col >= 8) & (col < 16), d * na_ref[...] * jax.nn.sigmoid(z), 0.0)
        ds_ref[...] = jnp.where(col < 8, d * sg * (1.0 - sg), dz).astype(bf16)
        dalog = jnp.where((col >= 8) & (col < 16), d * na_ref[...] * jax.nn.softplus(z), 0.0)
        _acc_rows(da_ref, jnp.sum(dalog, axis=0, keepdims=True))
        _acc_rows(ddt_ref, jnp.sum(dz, axis=0, keepdims=True))

    return pl.pallas_call(
        body, out_shape=(SDS((t, LANE), bf16), SDS((1, LANE), f32), SDS((1, LANE), f32)), grid=(t // tt,),
        in_specs=[pl.BlockSpec((tt, LANE), lambda i: (i, 20)), _vec_spec(LANE), _vec_spec(LANE), _row_spec(tt, LANE),
                  _row_spec(tt, LANE)],
        out_specs=(_row_spec(tt, LANE), _vec_spec(LANE), _vec_spec(LANE)),
        name=name, compiler_params=_cparams(("arbitrary",)),
    )(p, neg_a, dt_bias, dbg_a, dbg_b)


N_SCAN = 2 * GDN_HEADS


def _bdot(a, b, ca, cb, precision=None):
    if precision is None:
        a, b = a.astype(bf16), b.astype(bf16)
    return lax.dot_general(a, b, (((ca,), (cb,)), ((0,), (0,))), preferred_element_type=f32, precision=precision)


def _bdot_nn(a, b, precision=None):
    return _bdot(a, b, 2, 1, precision)


def _bdot_nt(a, b):
    return _bdot(a, b, 2, 2)


def _bdot_tn(a, b, precision=None):
    return _bdot(a, b, 1, 1, precision)


def _order_masks():
    shape = (N_SCAN, CHUNK, CHUNK)
    sign = jnp.where(lax.broadcasted_iota(jnp.int32, shape, 0) >= GDN_HEADS, -1, 1)
    ahead = (lax.broadcasted_iota(jnp.int32, shape, 1) - lax.broadcasted_iota(jnp.int32, shape, 2)) * sign
    lower, strict, lower_t = ahead >= 0, ahead > 0, ahead <= 0
    col_shape = (N_SCAN, CHUNK, 1)
    back1 = lax.broadcasted_iota(jnp.int32, col_shape, 0) >= GDN_HEADS
    row1 = lax.broadcasted_iota(jnp.int32, col_shape, 1)
    at_last = (row1 == jnp.where(back1, 0, CHUNK - 1)).astype(f32)
    return lower, strict, lower_t, at_last


def _stack_heads(f_ref, b_ref):
    return jnp.stack([ref[:, h * LANE:(h + 1) * LANE] for ref in (f_ref, b_ref) for h in range(GDN_HEADS)])


def _stack_gates(bgf, bgb, bgtf, bgtb):
    beta = jnp.stack([bg[:, 4 * d + h:4 * d + h + 1] for d, bg in enumerate((bgf, bgb)) for h in range(GDN_HEADS)])
    g_col = jnp.stack([bg[:, 8 + 4 * d + h:9 + 4 * d + h] for d, bg in enumerate((bgf, bgb)) for h in range(GDN_HEADS)])
    g_row = jnp.stack([bgt[8 + 4 * d + h:9 + 4 * d + h, :] for d, bgt in enumerate((bgtf, bgtb)) for h in range(GDN_HEADS)])
    return beta, g_col, g_row


def _chunk_terms(k, v, beta, g_col, g_row, masks, tinv=None):
    lower, strict, lower_t, at_last = masks
    gc = jnp.sum(lower.astype(f32) * g_row, axis=2, keepdims=True)
    gr = jnp.sum(lower_t.astype(f32) * g_col, axis=1, keepdims=True)
    g_last = jnp.sum(at_last * gc, axis=1, keepdims=True)
    e = jnp.exp(gc)
    f = jnp.exp(g_last - gc)
    dm = jnp.exp(jnp.where(lower, gc - gr, -1e30))
    kb = k * beta
    kk = _bdot_nt(kb, k)
    if tinv is None:
        shape = (N_SCAN, CHUNK, CHUNK)
        eye = (lax.broadcasted_iota(jnp.int32, shape, 1) == lax.broadcasted_iota(jnp.int32, shape, 2)).astype(f32)
        pw = -jnp.where(strict, kk * dm, 0.0)
        tinv = eye + pw
        for _ in range(5):
            pw = _bdot_nn(pw, pw, lax.Precision.HIGH)
            tinv = tinv + _bdot_nn(tinv, pw, lax.Precision.HIGH)
    u = _bdot_nn(tinv, v * beta)
    w = _bdot_nn(tinv, kb * e)
    return dict(e=e, f=f, gl=jnp.exp(g_last), dm=dm, kb=kb, kk=kk, tinv=tinv, u=u, w=w, kd=k * f)


def _gdn_specs(nc, width, step_chunk):
    return [pl.BlockSpec((CHUNK, width), functools.partial(lambda i, d: (step_chunk(i, d), 0), d=d)) for d in (0, 1)]


def gdn_forward(q, k, v, bg, bgt, s0, with_out, name):
    t = k.shape[0]
    nc = t // CHUNK

    def body(qf_ref, qb_ref, kf_ref, kb_ref, vf_ref, vb_ref, bgf_ref, bgb_ref, bgtf_ref, bgtb_ref, s0_ref,
             of_ref, ob_ref, sallf_ref, sallb_ref, tinvf_ref, tinvb_ref, sfin_ref, s_ref):
        i = pl.program_id(0)

        @pl.when(i == 0)
        def _():
            s_ref[...] = s0_ref[...]

        masks = _order_masks()
        k8, v8 = _stack_heads(kf_ref, kb_ref), _stack_heads(vf_ref, vb_ref)
        beta, g_col, g_row = _stack_gates(bgf_ref[...], bgb_ref[...], bgtf_ref[0], bgtb_ref[0])
        c = _chunk_terms(k8, v8, beta, g_col, g_row, masks)
        s = s_ref[...]
        sallf_ref[0] = s[:GDN_HEADS]
        sallb_ref[0] = s[GDN_HEADS:]
        tinvf_ref[0] = c["tinv"][:GDN_HEADS]
        tinvb_ref[0] = c["tinv"][GDN_HEADS:]
        vn = c["u"] - _bdot_nn(c["w"], s)
        if with_out:
            q8 = _stack_heads(qf_ref, qb_ref)
            pm = jnp.where(masks[0], _bdot_nt(q8, k8) * c["dm"], 0.0)
            o = _bdot_nn(q8 * c["e"], s) + _bdot_nn(pm, vn)
        for d, o_ref in enumerate((of_ref, ob_ref)):
            for h in range(GDN_HEADS):
                o_ref[:, h * LANE:(h + 1) * LANE] = o[GDN_HEADS * d + h] if with_out else jnp.zeros((CHUNK, LANE), f32)
        s_ref[...] = c["gl"] * s + _bdot_tn(c["kd"], vn)

        @pl.when(i == nc - 1)
        def _():
            sfin_ref[...] = s_ref[...]

    chunk_of = lambda i, d: i if d == 0 else nc - 1 - i
    seq = _gdn_specs(nc, 512, chunk_of)
    gate = _gdn_specs(nc, LANE, chunk_of)
    gate_t = [pl.BlockSpec((1, 16, CHUNK), functools.partial(lambda i, d: (chunk_of(i, d), 0, 0), d=d)) for d in (0, 1)]
    sall = [pl.BlockSpec((1, GDN_HEADS, LANE, LANE), functools.partial(lambda i, d: (chunk_of(i, d), 0, 0, 0), d=d)) for d in (0, 1)]
    tinv = [pl.BlockSpec((1, GDN_HEADS, CHUNK, CHUNK), functools.partial(lambda i, d: (chunk_of(i, d), 0, 0, 0), d=d)) for d in (0, 1)]
    st = pl.BlockSpec((N_SCAN, LANE, LANE), lambda i: (0, 0, 0))
    o_shape, s_shape, t_shape = SDS((t, 512), f32), SDS((nc, GDN_HEADS, LANE, LANE), f32), SDS((nc, GDN_HEADS, CHUNK, CHUNK), f32)
    o_f, o_b, sall_f, sall_b, tinv_f, tinv_b, s_fin = pl.pallas_call(
        body, out_shape=(o_shape, o_shape, s_shape, s_shape, t_shape, t_shape, SDS((N_SCAN, LANE, LANE), f32)), grid=(nc,),
        in_specs=seq + seq + seq + gate + gate_t + [st], out_specs=tuple(seq + sall + tinv + [st]),
        scratch_shapes=[pltpu.VMEM((N_SCAN, LANE, LANE), f32)], name=name,
        compiler_params=_cparams(("arbitrary",)),
    )(q, q, k, k, v, v, bg, bg, bgt, bgt, s0.reshape(N_SCAN, LANE, LANE))
    return o_f, o_b, (sall_f, sall_b, tinv_f, tinv_b), s_fin.reshape(2, GDN_HEADS, LANE, LANE)


def _gdn_chunk_bwd(q, k, v, d_o, beta, g_col, g_row, s, tinv, dsn, masks):
    lower, strict, _, at_last = masks
    c = _chunk_terms(k, v, beta, g_col, g_row, masks, tinv)
    e, f, gl, dm, kb, kk, tinv, u, w, kd = (c[n] for n in ("e", "f", "gl", "dm", "kb", "kk", "tinv", "u", "w", "kd"))
    vn = u - _bdot_nn(w, s)
    ds = gl * dsn
    dgl = jnp.sum(jnp.sum(s * dsn, axis=2, keepdims=True), axis=1, keepdims=True)
    dkd = _bdot_nt(vn, dsn)
    dvn = _bdot_nn(kd, dsn)
    dm_grad = jnp.zeros((N_SCAN, CHUNK, CHUNK), f32)
    de = jnp.zeros((N_SCAN, CHUNK, 1), f32)
    dq = None
    dk = jnp.zeros((N_SCAN, CHUNK, LANE), f32)
    if q is not None:
        qk = _bdot_nt(q, k)
        pm = jnp.where(lower, qk * dm, 0.0)
        dqd = _bdot_nt(d_o, s)
        ds = ds + _bdot_tn(q * e, d_o)
        dpm = jnp.where(lower, _bdot_nt(d_o, vn), 0.0)
        dvn = dvn + _bdot_tn(pm, d_o)
        dqk = dpm * dm
        dm_grad = dm_grad + dpm * qk
        dq = _bdot_nn(dqk, k) + dqd * e
        dk = _bdot_tn(dqk, q)
        de = de + jnp.sum(dqd * q, axis=2, keepdims=True)
    dw = -_bdot_nt(dvn, s)
    ds = ds - _bdot_tn(w, dvn)
    drv = _bdot_tn(tinv, dvn)
    drk = _bdot_tn(tinv, dw)
    da = -jnp.where(strict, _bdot_nt(drv, u) + _bdot_nt(drk, w), 0.0)
    dbeta = jnp.sum(drv * v, axis=2, keepdims=True)
    dv = drv * beta
    dkb = drk * e
    de = de + jnp.sum(drk * kb, axis=2, keepdims=True)
    dkk = da * dm
    dm_grad = dm_grad + da * kk
    dkb = dkb + _bdot_nn(dkk, k)
    dk = dk + _bdot_tn(dkk, kb) + dkd * f
    df = jnp.sum(dkd * k, axis=2, keepdims=True)
    dbeta = dbeta + jnp.sum(dkb * k, axis=2, keepdims=True)
    dk = dk + dkb * beta
    m = dm_grad * dm
    shape = (N_SCAN, CHUNK, CHUNK)
    eye = (lax.broadcasted_iota(jnp.int32, shape, 1) == lax.broadcasted_iota(jnp.int32, shape, 2)).astype(f32)

    def as_col(row):
        return jnp.sum(eye * row, axis=2, keepdims=True)

    rsum = jnp.sum(m, axis=2, keepdims=True)
    csum = as_col(jnp.sum(m, axis=1, keepdims=True))
    dgl_tot = jnp.sum(df * f, axis=1, keepdims=True) + dgl * gl
    dgc = de * e - df * f + rsum - csum + at_last * dgl_tot
    dg = as_col(jnp.sum(lower.astype(f32) * dgc, axis=1, keepdims=True))
    return dq, dk, dv, dbeta, dg, ds


def gdn_backward(q, k, v, bg, bgt, saved, d_o, ds_fin, with_out, name):
    t = k.shape[0]
    nc = t // CHUNK

    def body(qf_ref, qb_ref, kf_ref, kb_ref, vf_ref, vb_ref, bgf_ref, bgb_ref, bgtf_ref, bgtb_ref,
             sallf_ref, sallb_ref, tinvf_ref, tinvb_ref, dof_ref, dob_ref, dsf_ref,
             dqf_ref, dqb_ref, dkf_ref, dkb_ref, dvf_ref, dvb_ref, dbgf_ref, dbgb_ref, ds0_ref, ds_ref):
        i = pl.program_id(0)

        @pl.when(i == 0)
        def _():
            ds_ref[...] = dsf_ref[...]

        lane = lax.broadcasted_iota(jnp.int32, (1, LANE), 1)
        masks = _order_masks()
        beta, g_col, g_row = _stack_gates(bgf_ref[...], bgb_ref[...], bgtf_ref[0], bgtb_ref[0])
        s = jnp.concatenate([sallf_ref[0], sallb_ref[0]], 0)
        tinv = jnp.concatenate([tinvf_ref[0], tinvb_ref[0]], 0)
        dq, dk, dv, dbeta, dg, ds = _gdn_chunk_bwd(
            _stack_heads(qf_ref, qb_ref) if with_out else None, _stack_heads(kf_ref, kb_ref), _stack_heads(vf_ref, vb_ref),
            _stack_heads(dof_ref, dob_ref), beta, g_col, g_row, s, tinv, ds_ref[...], masks)
        ds_ref[...] = ds
        for d, (dq_ref, dk_ref, dv_ref, dbg_ref) in enumerate(((dqf_ref, dkf_ref, dvf_ref, dbgf_ref), (dqb_ref, dkb_ref, dvb_ref, dbgb_ref))):
            dbg = jnp.zeros((CHUNK, LANE), f32)
            for h in range(GDN_HEADS):
                b = GDN_HEADS * d + h
                cs = slice(h * LANE, (h + 1) * LANE)
                dq_ref[:, cs] = dq[b] if with_out else jnp.zeros((CHUNK, LANE), f32)
                dk_ref[:, cs] = dk[b]
                dv_ref[:, cs] = dv[b]
                dbg = dbg + dbeta[b] * (lane == b).astype(f32) + dg[b] * (lane == 8 + b).astype(f32)
            dbg_ref[...] = dbg

        @pl.when(i == nc - 1)
        def _():
            ds0_ref[...] = ds_ref[...]

    chunk_of = lambda i, d: nc - 1 - i if d == 0 else i
    seq = _gdn_specs(nc, 512, chunk_of)
    gate = _gdn_specs(nc, LANE, chunk_of)
    gate_t = [pl.BlockSpec((1, 16, CHUNK), functools.partial(lambda i, d: (chunk_of(i, d), 0, 0), d=d)) for d in (0, 1)]
    sall = [pl.BlockSpec((1, GDN_HEADS, LANE, LANE), functools.partial(lambda i, d: (chunk_of(i, d), 0, 0, 0), d=d)) for d in (0, 1)]
    tinv = [pl.BlockSpec((1, GDN_HEADS, CHUNK, CHUNK), functools.partial(lambda i, d: (chunk_of(i, d), 0, 0, 0), d=d)) for d in (0, 1)]
    st = pl.BlockSpec((N_SCAN, LANE, LANE), lambda i: (0, 0, 0))
    o_shape, g_shape = SDS((t, 512), f32), SDS((t, LANE), f32)
    res = pl.pallas_call(
        body, out_shape=(o_shape,) * 6 + (g_shape, g_shape, SDS((N_SCAN, LANE, LANE), f32)), grid=(nc,),
        in_specs=seq + seq + seq + gate + gate_t + sall + tinv + seq + [st], out_specs=tuple(seq + seq + seq + gate + [st]),
        scratch_shapes=[pltpu.VMEM((N_SCAN, LANE, LANE), f32)], name=name,
        compiler_params=_cparams(("arbitrary",)),
    )(q, q, k, k, v, v, bg, bg, bgt, bgt, *saved, d_o, d_o, ds_fin.reshape(N_SCAN, LANE, LANE))
    return tuple(res[:8]) + (res[8].reshape(2, GDN_HEADS, LANE, LANE),)


def _my_position():
    x, y, c = lax.axis_index("x"), lax.axis_index("y"), lax.axis_index("c")
    return x, y, c, 4 * x + 2 * y + c


def exchange(arrays, scatter, name):
    n = len(arrays)
    shapes = [a.shape[1:] if scatter else a.shape for a in arrays]

    def body(*refs):
        ins, outs, token = refs[:n], refs[n:2 * n], refs[2 * n]
        send_sems, recv_sems, local_sems = refs[2 * n + 1:]
        x, y, c, me = _my_position()
        token[...] = jnp.zeros_like(token)
        started = []
        for a in range(n):
            mine = pltpu.make_async_copy(ins[a].at[me] if scatter else ins[a], outs[a].at[me], local_sems.at[a])
            mine.start()
            started.append(mine)
        waits = []
        for r in range(1, N_DEV):
            px = 1 - x if r & 4 else x
            py = 1 - y if r & 2 else y
            pc = 1 - c if r & 1 else c
            pid = 4 * px + 2 * py + pc
            for a in range(n):
                cp = pltpu.make_async_remote_copy(
                    src_ref=ins[a].at[pid] if scatter else ins[a], dst_ref=outs[a].at[me],
                    send_sem=send_sems.at[a, r - 1], recv_sem=recv_sems.at[a, r - 1],
                    device_id=(px, py, pc), device_id_type=pl.DeviceIdType.MESH)
                cp.start()
                arrive = pltpu.make_async_remote_copy(
                    src_ref=ins[a].at[pid] if scatter else ins[a], dst_ref=outs[a].at[pid],
                    send_sem=send_sems.at[a, r - 1], recv_sem=recv_sems.at[a, r - 1],
                    device_id=(px, py, pc), device_id_type=pl.DeviceIdType.MESH)
                waits.append((cp, arrive))
        for cp, arrive in waits:
            cp.wait_send()
            arrive.wait_recv()
        for mine in started:
            mine.wait()

    any_spec = pl.BlockSpec(memory_space=pl.ANY)
    return pl.pallas_call(
        body, out_shape=tuple(SDS((N_DEV,) + tuple(s), a.dtype) for s, a in zip(shapes, arrays)) + (SDS((8, LANE), f32),),
        in_specs=[any_spec] * n, out_specs=tuple([any_spec] * n) + (pl.BlockSpec(memory_space=pltpu.VMEM),),
        scratch_shapes=[pltpu.SemaphoreType.DMA((n, N_DEV - 1)), pltpu.SemaphoreType.DMA((n, N_DEV - 1)),
                        pltpu.SemaphoreType.DMA((n,))],
        name=name,
    )(*arrays)


_HBM_SPEC = pl.BlockSpec(memory_space=pltpu.HBM)
_SEM_SPEC = pl.BlockSpec(memory_space=pltpu.SEMAPHORE)
_DATAFLOW = pltpu.SideEffectType.DATAFLOW_SIDE_EFFECTING


def _peers(x, y, c):
    out = []
    for r in range(1, N_DEV):
        px = 1 - x if r & 4 else x
        py = 1 - y if r & 2 else y
        pc = 1 - c if r & 1 else c
        out.append((r, (px, py, pc), 4 * px + 2 * py + pc))
    return out


def _exchange_copies(ins, lands, send_sems, recv_sems, scatter, arrivals):
    x, y, c, me = _my_position()
    pairs = []
    for r, peer, pid in _peers(x, y, c):
        for a in range(len(ins)):
            k = a * (N_DEV - 1) + r - 1
            kw = dict(send_sem=send_sems.at[k], recv_sem=recv_sems.at[k], device_id=peer, device_id_type=pl.DeviceIdType.MESH)
            src = ins[a].at[pid] if scatter else ins[a]
            send = pltpu.make_async_remote_copy(src_ref=src, dst_ref=lands[a].at[me], **kw)
            arrive = pltpu.make_async_remote_copy(src_ref=src, dst_ref=lands[a].at[pid], **kw) if arrivals else None
            pairs.append((send, arrive))
    return pairs


def exchange_start(arrays, scatter, name):
    n = len(arrays)
    shapes = [a.shape[1:] if scatter else a.shape for a in arrays]

    def body(*refs):
        ins, lands = refs[:n], refs[n:2 * n]
        send_sems, recv_sems = refs[2 * n], refs[2 * n + 1]
        token = refs[-1]
        for send, _ in _exchange_copies(ins, lands, send_sems, recv_sems, scatter, False):
            send.start()
        token[...] = jnp.zeros_like(token)

    sem = pltpu.SemaphoreType.DMA((n * (N_DEV - 1),))
    land_shapes = [(N_DEV,) + tuple(s) for s in shapes]
    res = pl.pallas_call(
        body, name=name,
        out_shape=(sem, sem, *[pltpu.HBM(a.shape, a.dtype) for a in arrays],
                   *[pltpu.HBM(s, a.dtype) for s, a in zip(land_shapes, arrays)], SDS((8, LANE), f32)),
        in_specs=[_HBM_SPEC] * (2 * n),
        out_specs=(_SEM_SPEC, _SEM_SPEC, *[_HBM_SPEC] * (2 * n), pl.BlockSpec(memory_space=pltpu.VMEM)),
        input_output_aliases={i: 2 + i for i in range(2 * n)},
        compiler_params=pltpu.CompilerParams(has_side_effects=_DATAFLOW),
    )(*[pltpu.with_memory_space_constraint(a, pltpu.HBM) for a in arrays],
      *[pltpu.with_memory_space_constraint(lax.empty(s, a.dtype), pltpu.HBM) for s, a in zip(land_shapes, arrays)])
    return (res[0], res[1], list(res[2:2 + n]), list(res[2 + n:2 + 2 * n]), scatter), res[-1]


def exchange_wait(handle, after, name):
    send_sems, recv_sems, ins, lands, scatter = handle
    n = len(ins)

    def body(*refs):
        in_refs, land_refs = refs[:n], refs[n:2 * n]
        for send, arrive in _exchange_copies(in_refs, land_refs, refs[2 * n], refs[2 * n + 1], scatter, True):
            send.wait_send()
            arrive.wait_recv()
        refs[-1][...] = jnp.zeros_like(refs[-1])

    res = pl.pallas_call(
        body, name=name,
        out_shape=tuple(pltpu.HBM(a.shape, a.dtype) for a in ins + lands) + (SDS((8, LANE), f32),),
        in_specs=[_HBM_SPEC] * (2 * n) + [_SEM_SPEC, _SEM_SPEC, pl.BlockSpec(memory_space=pl.ANY)],
        out_specs=tuple([_HBM_SPEC] * (2 * n)) + (pl.BlockSpec(memory_space=pltpu.VMEM),),
        input_output_aliases={i: i for i in range(2 * n)},
        compiler_params=pltpu.CompilerParams(has_side_effects=_DATAFLOW),
    )(*ins, *lands, send_sems, recv_sems, after)
    return list(res[:n]), list(res[n:2 * n]), res[-1]


def place_own(lands, arrays, scatter, me):
    own = [lax.dynamic_index_in_dim(a, me, 0, keepdims=False) if scatter else a for a in arrays]
    return [lax.dynamic_update_index_in_dim(l, o, me, 0) for l, o in zip(lands, own)]


def ada_forward(a_raw, ada_w, ada_b_loc, name):
    def body(a_ref, w_ref, b_ref, o_ref):
        a = _silu(a_ref[...])
        for l in range(DEPTH):
            o_ref[l] = _dotf(a, w_ref[l]) + b_ref[l]

    return pl.pallas_call(body, out_shape=SDS((DEPTH, 16, ada_w.shape[2]), f32), name=name,
                          compiler_params=_cparams())(a_raw, ada_w, ada_b_loc)


def ada_backward(a_raw, ada_w, dm, name):
    def body(a_ref, w_ref, dm_ref, gw_ref, dcc_ref):
        a = _silu(a_ref[...])
        for l in range(DEPTH):
            gw_ref[l] = _dotf(a, dm_ref[l], (((0,), (0,)), ((), ())))
        dcc_ref[...] = _dotf(dm_ref[0, 8:16, :], w_ref[0], (((1,), (1,)), ((), ())))

    return pl.pallas_call(body, out_shape=(SDS(ada_w.shape, f32), SDS((8, ada_w.shape[1]), f32)), name=name,
                          compiler_params=_cparams())(a_raw, ada_w, dm)


def sum_parts(parts, name):
    _, r, c = parts.shape

    def body(p_ref, o_ref):
        acc = p_ref[0]
        for i in range(1, N_DEV):
            acc = acc + p_ref[i]
        o_ref[...] = acc

    return pl.pallas_call(body, out_shape=SDS((r, c), f32), name=name, compiler_params=_cparams())(parts)


def cctx_grad(parts, c_ctx, name):
    def body(p_ref, c_ref, o_ref):
        acc = p_ref[0, 0:1, :]
        for i in range(1, N_DEV):
            acc = acc + p_ref[i, 0:1, :]
        o_ref[...] = acc * _dsilu(c_ref[...])

    return pl.pallas_call(body, out_shape=SDS((1, c_ctx.shape[1]), f32), name=name, compiler_params=_cparams())(parts, c_ctx)


def _adamw_math(g, w, m, v):
    m = ADAM_B1 * m + (1.0 - ADAM_B1) * g
    v = ADAM_B2 * v + (1.0 - ADAM_B2) * (g * g)
    m_hat = m / (1.0 - ADAM_B1 ** ADAM_STEP)
    v_hat = v / (1.0 - ADAM_B2 ** ADAM_STEP)
    delta = -ADAM_LR * (m_hat / (jnp.sqrt(v_hat) + ADAM_EPS) + ADAM_WD * w)
    return delta, m, v


def adamw(parts, w, m, v, name):
    n, r, c = parts.shape
    tr = _pick(r, (256, 128, 64, 32, 16, 8))

    def body(p_ref, w_ref, m_ref, v_ref, g_ref, d_ref, nm_ref, nv_ref):
        g = p_ref[0].astype(f32)
        for i in range(1, n):
            g = g + p_ref[i].astype(f32)
        g_ref[...] = g
        d_ref[...], nm_ref[...], nv_ref[...] = _adamw_math(g, w_ref[...], m_ref[...], v_ref[...])

    blk = pl.BlockSpec((tr, c), lambda i: (i, 0))
    out = SDS((r, c), f32)
    return pl.pallas_call(
        body, out_shape=(out, out, out, out), grid=(r // tr,),
        in_specs=[pl.BlockSpec((n, tr, c), lambda i: (0, i, 0)), blk, blk, blk], out_specs=(blk, blk, blk, blk),
        name=name, compiler_params=_cparams(("parallel",)),
    )(parts, w, m, v)


def adamw_small(items, name):
    n = len(items)

    def body(*refs):
        ins, outs = refs[:4 * n], refs[4 * n:]
        for i in range(n):
            g, w, m, v = (ins[4 * i + j][...] for j in range(4))
            outs[3 * i][...], outs[3 * i + 1][...], outs[3 * i + 2][...] = _adamw_math(g, w, m, v)

    flat = [a for it in items for a in it]
    out_shape = tuple(SDS(it[1].shape, f32) for it in items for _ in range(3))
    res = pl.pallas_call(body, out_shape=out_shape, name=name, compiler_params=_cparams())(*flat)
    return [tuple(res[3 * i:3 * i + 3]) for i in range(n)]


def _unshard(g, axis):
    loc = g.shape[1:]
    return jnp.moveaxis(g, 0, axis).reshape(loc[:axis] + (N_DEV * loc[axis],) + loc[axis + 1:])


def _shard_major(full, axis):
    s = full.shape
    return jnp.moveaxis(full.reshape(s[:axis] + (N_DEV, s[axis] // N_DEV) + s[axis + 1:]), axis, 0)


def _my_block(full, axis, me):
    n = full.shape[axis] // N_DEV
    return lax.dynamic_slice_in_dim(full, me * n, n, axis)


def _pack(arrays):
    flat = [a.reshape(-1) for a in arrays]
    sizes = [f.shape[0] for f in flat]
    total = sum(sizes)
    padded = -(-total // (8 * LANE)) * (8 * LANE)
    flat.append(jnp.zeros((padded - total,), f32))
    offs = [sum(sizes[:i]) for i in range(len(sizes))]
    return jnp.concatenate(flat).reshape(padded // LANE, LANE), offs


def _pad_rows(w, n):
    return jnp.concatenate([w, jnp.zeros((n - w.shape[0],) + w.shape[1:], w.dtype)], 0)


def _gate_rows(bg):
    return bg[:, :16].reshape(bg.shape[0] // CHUNK, CHUNK, 16).transpose(0, 2, 1)


def _rows(vec, n):
    m = vec.reshape(n, 1, -1)
    return [m[i] for i in range(n)]


def kernel(x, c, ctx, c_ctx, ada_w, ada_b, ln_g, ln_b, even_w_in, even_w_out, gdn_conv_w, gdn_a_log, gdn_dt_bias, gdn_norm_w, pool_w, pool_scale, odd_w_in, odd_w_out, sconv_w, conf_conv_w, conf_ln_g, conf_ln_b, ffn_w_up, ffn_conv_w, ffn_w_down, loss_target, m_c_ctx, m_ada_w, m_ada_b, m_ln_g, m_ln_b, m_even_w_in, m_even_w_out, m_gdn_conv_w, m_gdn_a_log, m_gdn_dt_bias, m_gdn_norm_w, m_pool_w, m_pool_scale, m_odd_w_in, m_odd_w_out, m_sconv_w, m_conf_conv_w, m_conf_ln_g, m_conf_ln_b, m_ffn_w_up, m_ffn_conv_w, m_ffn_w_down, v_c_ctx, v_ada_w, v_ada_b, v_ln_g, v_ln_b, v_even_w_in, v_even_w_out, v_gdn_conv_w, v_gdn_a_log, v_gdn_dt_bias, v_gdn_norm_w, v_pool_w, v_pool_scale, v_odd_w_in, v_odd_w_out, v_sconv_w, v_conf_conv_w, v_conf_ln_g, v_conf_ln_b, v_ffn_w_up, v_ffn_conv_w, v_ffn_w_down):
    weights = dict(c_ctx=c_ctx, ada_w=ada_w, ada_b=ada_b, ln_g=ln_g, ln_b=ln_b, even_w_in=even_w_in, even_w_out=even_w_out, gdn_conv_w=gdn_conv_w, gdn_a_log=gdn_a_log, gdn_dt_bias=gdn_dt_bias, gdn_norm_w=gdn_norm_w, pool_w=pool_w, pool_scale=pool_scale, odd_w_in=odd_w_in, odd_w_out=odd_w_out, sconv_w=sconv_w, conf_conv_w=conf_conv_w, conf_ln_g=conf_ln_g, conf_ln_b=conf_ln_b, ffn_w_up=ffn_w_up, ffn_conv_w=ffn_conv_w, ffn_w_down=ffn_w_down)
    mom1 = dict(c_ctx=m_c_ctx, ada_w=m_ada_w, ada_b=m_ada_b, ln_g=m_ln_g, ln_b=m_ln_b, even_w_in=m_even_w_in, even_w_out=m_even_w_out, gdn_conv_w=m_gdn_conv_w, gdn_a_log=m_gdn_a_log, gdn_dt_bias=m_gdn_dt_bias, gdn_norm_w=m_gdn_norm_w, pool_w=m_pool_w, pool_scale=m_pool_scale, odd_w_in=m_odd_w_in, odd_w_out=m_odd_w_out, sconv_w=m_sconv_w, conf_conv_w=m_conf_conv_w, conf_ln_g=m_conf_ln_g, conf_ln_b=m_conf_ln_b, ffn_w_up=m_ffn_w_up, ffn_conv_w=m_ffn_conv_w, ffn_w_down=m_ffn_w_down)
    mom2 = dict(c_ctx=v_c_ctx, ada_w=v_ada_w, ada_b=v_ada_b, ln_g=v_ln_g, ln_b=v_ln_b, even_w_in=v_even_w_in, even_w_out=v_even_w_out, gdn_conv_w=v_gdn_conv_w, gdn_a_log=v_gdn_a_log, gdn_dt_bias=v_gdn_dt_bias, gdn_norm_w=v_gdn_norm_w, pool_w=v_pool_w, pool_scale=v_pool_scale, odd_w_in=v_odd_w_in, odd_w_out=v_odd_w_out, sconv_w=v_sconv_w, conf_conv_w=v_conf_conv_w, conf_ln_g=v_conf_ln_g, conf_ln_b=v_conf_ln_b, ffn_w_up=v_ffn_w_up, ffn_conv_w=v_ffn_conv_w, ffn_w_down=v_ffn_w_down)
    order = list(weights)
    me = 4 * lax.axis_index("x") + 2 * lax.axis_index("y") + lax.axis_index("c")
    x, ctx, target = x[0], ctx[0], loss_target[0]
    t, d = x.shape
    tc = ctx.shape[0]

    small_in = [ln_g, ln_b, gdn_conv_w, sconv_w, conf_conv_w, ffn_conv_w, c]
    small_axes = [2, 2, 1, 1, 1, 3, 0]
    small_pack, small_offs = _pack(small_in)
    gath = exchange([even_w_in.astype(bf16), small_pack], False, "gather_first")
    e_in = even_w_in.shape[1] * N_DEV
    e_pad = -(-e_in // LANE) * LANE
    win_e = jnp.pad(_unshard(gath[0], 1), ((0, 0), (0, e_pad - e_in)))
    sm = gath[1].reshape(N_DEV, -1)
    lng_f, lnb_f, gconv_f, sconv_f, cconv_f, fconv_f, c_all = [
        _unshard(sm[:, o:o + a.size].reshape((N_DEV,) + a.shape), ax) for a, o, ax in zip(small_in, small_offs, small_axes)]
    gw8 = _pad_rows(gconv_f, 8)
    sw8 = _pad_rows(sconv_f, 8)
    cw32 = _pad_rows(cconv_f, 32)
    fw16 = [_pad_rows(fconv_f[l].reshape(9, D_FF), 16) for l in range(DEPTH)]

    a_raw = jnp.concatenate([c_all, c_ctx[None], jnp.zeros((7, d), f32)], 0)
    ncol = ada_w.shape[2]
    ada_b_loc = lax.dynamic_slice_in_dim(ada_b, me * ncol, ncol, 1)[:, None, :]
    modpart = ada_forward(a_raw, ada_w, ada_b_loc, "ada_forward")
    mod_send = jnp.stack([jnp.transpose(modpart[:, :N_DEV], (1, 0, 2)),
                          jnp.broadcast_to(modpart[:, N_DEV][None], (N_DEV, DEPTH, ncol))], axis=2)
    mod_recv, token = exchange([mod_send], True, "scatter_mod")
    wire_l0 = [even_w_out.astype(bf16) + token[0, 0].astype(bf16), ffn_w_up[0].astype(bf16), ffn_w_down[0].astype(bf16)]
    gather_l0, token = exchange_start(wire_l0, False, "gather_l0_start")
    wire_l1 = [odd_w_in.astype(bf16) + token[0, 0].astype(bf16), odd_w_out.astype(bf16), ffn_w_up[1].astype(bf16),
               ffn_w_down[1].astype(bf16)]
    gather_l1, token = exchange_start(wire_l1, False, "gather_l1_start")
    mod_recv = mod_recv + token[0, 0]
    mod = jnp.transpose(mod_recv[:, :, 0, :], (1, 0, 2)).reshape(DEPTH, 6 * d)
    modc = mod_recv[:, 0, 1, :].reshape(6 * d)
    sh_c, sc_c = modc[None, :d], modc[None, d:2 * d]
    mods = [_rows(mod[l], 6) for l in range(DEPTH)]
    lng = [[lng_f[l, j][None] for j in range(2)] for l in range(DEPTH)]
    lnb = [[lnb_f[l, j][None] for j in range(2)] for l in range(DEPTH)]

    neg_a = jnp.zeros((1, LANE), f32).at[0, 8:16].set(-jnp.exp(gdn_a_log).reshape(8))
    dt_row = jnp.zeros((1, LANE), f32).at[0, 8:16].set(gdn_dt_bias.reshape(8))
    nw_row, ps_row = gdn_norm_w[None], pool_scale[None]
    cg_row, cb_row = conf_ln_g[None], conf_ln_b[None]
    q_scale = GDN_DK ** -0.5

    sh_m, sc_m, gt_m, sh_f, sc_f, gt_f = mods[0]
    u0 = modulate(x, sc_m, sh_m, "mod_l0_mix")
    cu = modulate(ctx, sc_c, sh_c, "mod_ctx")
    p0 = matmul(u0, win_e, "nn", f32, "even_in")
    pc = matmul(cu, win_e, "nn", f32, "even_in_ctx")
    qn = gdn_conv(p0, gw8, 0, 4, q_scale, "gdn_conv_q")
    kn = gdn_conv(p0, gw8, 4, 4, 1.0, "gdn_conv_k")
    vv = gdn_conv(p0, gw8, 8, 4, None, "gdn_conv_v")
    kc = gdn_conv(pc, gw8, 4, 4, 1.0, "gdn_conv_k_ctx")
    vc = gdn_conv(pc, gw8, 8, 4, None, "gdn_conv_v_ctx")
    bg = gdn_gates(p0, neg_a, dt_row, "gdn_gates")
    bgc = gdn_gates(pc, neg_a, dt_row, "gdn_gates_ctx")
    bgt, bgtc = _gate_rows(bg), _gate_rows(bgc)
    zero_state = jnp.zeros((2, GDN_HEADS, LANE, LANE), f32)
    _, _, saved_c, sfin_c = gdn_forward(kc, kc, vc, bgc, bgtc, zero_state, False, "gdn_fwd_ctx")
    o_f, o_b, saved, _ = gdn_forward(qn, kn, vv, bg, bgt, sfin_c, True, "gdn_fwd")
    mix0 = jnp.concatenate([gated_rmsnorm(o_f, o_b, p0, nw_row, "gated_rmsnorm"),
                            pool_mix(p0, pool_w, ps_row, "pool_mix")], 1)
    sent, landed, _ = exchange_wait(gather_l0, mix0, "gather_l0_wait")
    full = place_own(landed, sent, False, me)
    wout_e, wup, wdown = _unshard(full[0], 0), [_unshard(full[1], 1)], [_unshard(full[2], 0)]
    y0 = matmul(mix0, wout_e, "nn", f32, "even_out")
    x1, u1 = res_layernorm(x, y0, gt_m, lng[0][0], lnb[0][0], "resln_l0_mix", sc_f, sh_f)
    h0 = matmul(u1, wup[0], "nn", f32, "ffn_up_l0")
    f0 = ffn_conv(h0, fw16[0], "ffn_conv_l0")
    y0f = matmul(f0, wdown[0], "nn", f32, "ffn_down_l0")
    sh_m1, sc_m1, gt_m1, sh_f1, sc_f1, gt_f1 = mods[1]
    x2, u2 = res_layernorm(x1, y0f, gt_f, lng[0][1], lnb[0][1], "resln_l0_ffn", sc_m1, sh_m1)

    sent, landed, _ = exchange_wait(gather_l1, x2, "gather_l1_wait")
    full = place_own(landed, sent, False, me)
    win_o, wout_o = _unshard(full[0], 1), _unshard(full[1], 0)
    wup.append(_unshard(full[2], 1))
    wdown.append(_unshard(full[3], 0))
    p1 = matmul(u2, win_o, "nn", f32, "odd_in")
    zc = conf_conv(p1, cw32, "conf_conv")
    mix1 = jnp.concatenate([short_conv(p1, sw8, "short_conv"), ln_silu(zc, cg_row, cb_row, "conf_ln_silu")], 1)
    y1 = matmul(mix1, wout_o, "nn", f32, "odd_out")
    x3, u3 = res_layernorm(x2, y1, gt_m1, lng[1][0], lnb[1][0], "resln_l1_mix", sc_f1, sh_f1)
    h1 = matmul(u3, wup[1], "nn", f32, "ffn_up_l1")
    f1 = ffn_conv(h1, fw16[1], "ffn_conv_l1")
    y1f = matmul(f1, wdown[1], "nn", f32, "ffn_down_l1")
    x4 = res_layernorm(x3, y1f, gt_f1, lng[1][1], lnb[1][1], "resln_l1_ffn")

    loss_row, dx4 = loss_head(x4, target, "loss_head")
    loss = lax.psum(loss_row[0, 0], ("x", "y", "c"))

    def ffn_backward(dy, u, h, f, l):
        df = matmul(dy, wdown[l], "nt", f32, f"ffn_down_dgrad_l{l}")
        g_down = matmul(f, dy, "tn", bf16, f"ffn_down_wgrad_l{l}")
        dh, dcw = ffn_conv_bwd(h, fw16[l], df, f"ffn_conv_bwd_l{l}")
        du = matmul(dh, wup[l], "nt", f32, f"ffn_up_dgrad_l{l}")
        g_up = matmul(u, dh, "tn", bf16, f"ffn_up_wgrad_l{l}")
        return du, dcw, g_up, g_down

    dxr, dy, dgt_f1, dlg, dlb = res_layernorm_bwd(dx4, x3, y1f, gt_f1, lng[1][1], "resln_bwd_l1_ffn")
    dln_f1 = (dlg, dlb)
    du, dfcw1, g_up1, g_down1 = ffn_backward(dy, u3, h1, f1, 1)

    scatter_a, token = exchange_start([_shard_major(g_up1, 1), _shard_major(g_down1, 0)], True, "scatter_l1_ffn_start")
    gt_m1 = gt_m1 + token[0:1, 0:1]

    dxr, dy, dsc, dsh, dgt, dlg, dlb = modulate_res_layernorm_bwd(
        du, sc_f1, dxr, x2, y1, gt_m1, lng[1][0], lnb[1][0], "mod_resln_bwd_l1_mix")
    dmod_f1 = (dsh, dsc, dgt_f1)
    dln_m1 = (dlg, dlb)
    dmix = matmul(dy, wout_o, "nt", f32, "odd_out_dgrad")
    g_wout_o = matmul(mix1, dy, "tn", bf16, "odd_out_wgrad")
    dgb, dgc, dhh, d_sconv = short_conv_bwd(p1, sw8, dmix, "short_conv_bwd")
    dzc, d_cg, d_cb = ln_silu_bwd(zc, cg_row, cb_row, dmix, "conf_ln_silu_bwd")
    dga, dgbb, d_cconv = conf_conv_bwd(p1, cw32, dzc, "conf_conv_bwd")
    dp1 = jnp.concatenate([dgb, dgc, dhh, dga, dgbb], 1)
    du = matmul(dp1, win_o, "nt", f32, "odd_in_dgrad")
    g_win_o = matmul(u2, dp1, "tn", bf16, "odd_in_wgrad")
    dgt_m1 = dgt
    dxr, dy, dsc, dsh, dgt_f0, dlg, dlb = modulate_res_layernorm_bwd(
        du, sc_m1, dxr, x1, y0f, gt_f, lng[0][1], lnb[0][1], "mod_resln_bwd_l0_ffn")
    dmod_m1 = (dsh, dsc, dgt_m1)
    dln_f0 = (dlg, dlb)
    du, dfcw0, g_up0, g_down0 = ffn_backward(dy, u1, h0, f0, 0)

    scatter_b, token = exchange_start(
        [_shard_major(g_win_o, 1), _shard_major(g_wout_o, 0), _shard_major(g_up0, 1), _shard_major(g_down0, 0)],
        True, "scatter_mid_start")
    gt_m = gt_m + token[0:1, 0:1]

    dxr, dy, dsc, dsh, dgt, dlg, dlb = modulate_res_layernorm_bwd(
        du, sc_f, dxr, x, y0, gt_m, lng[0][0], lnb[0][0], "mod_resln_bwd_l0_mix")
    dmod_f0 = (dsh, dsc, dgt_f0)
    dln_m0 = (dlg, dlb)
    dmix = matmul(dy, wout_e, "nt", f32, "even_out_dgrad")
    g_wout_e = matmul(mix0, dy, "tn", bf16, "even_out_wgrad")
    d_o, dgate, d_nw = gated_rmsnorm_bwd(o_f, o_b, p0, nw_row, dmix, "gated_rmsnorm_bwd")
    dpool, d_pw, d_ps = pool_mix_bwd(p0, pool_w, ps_row, dmix, "pool_mix_bwd")
    dq_f, dq_b, dk_f, dk_b, dv_f, dv_b, dbg_f, dbg_b, ds0 = gdn_backward(
        qn, kn, vv, bg, bgt, saved, d_o, zero_state, True, "gdn_bwd")
    _, _, dkc_f, dkc_b, dvc_f, dvc_b, dbgc_f, dbgc_b, _ = gdn_backward(
        kc, kc, vc, bgc, bgtc, saved_c, jnp.zeros((tc, 512), f32), ds0, False, "gdn_bwd_ctx")
    dqp, dwq = gdn_conv_bwd(p0, gw8, dq_f, dq_b, 0, 4, q_scale, "gdn_conv_q_bwd")
    dkp, dwk = gdn_conv_bwd(p0, gw8, dk_f, dk_b, 4, 4, 1.0, "gdn_conv_k_bwd")
    dvp, dwv = gdn_conv_bwd(p0, gw8, dv_f, dv_b, 8, 4, None, "gdn_conv_v_bwd")
    dkcp, dwkc = gdn_conv_bwd(pc, gw8, dkc_f, dkc_b, 4, 4, 1.0, "gdn_conv_k_ctx_bwd")
    dvcp, dwvc = gdn_conv_bwd(pc, gw8, dvc_f, dvc_b, 8, 4, None, "gdn_conv_v_ctx_bwd")
    ds_l, da_l, ddt_l = gdn_gates_bwd(p0, neg_a, dt_row, dbg_f, dbg_b, "gdn_gates_bwd")
    ds_c, da_c, ddt_c = gdn_gates_bwd(pc, neg_a, dt_row, dbgc_f, dbgc_b, "gdn_gates_ctx_bwd")
    zc512 = jnp.zeros((tc, 512), bf16)
    dp0 = jnp.concatenate([dqp, dkp, dvp, dgate, dpool, ds_l], 1)
    dpc = jnp.concatenate([zc512, dkcp, dvcp, zc512, zc512, ds_c], 1)
    du0 = matmul(dp0, win_e, "nt", f32, "even_in_dgrad")
    duc = matmul(dpc, win_e, "nt", f32, "even_in_ctx_dgrad")
    g_win_e = matmul(u0, dp0, "tn", bf16, "even_in_wgrad", init=matmul(cu, dpc, "tn", f32, "even_in_ctx_wgrad"))[:, :e_in]
    scatter_c, token = exchange_start([_shard_major(g_win_e, 1), _shard_major(g_wout_e, 0)], True, "scatter_last_start")
    grad_x, dsc, dsh = modulate_bwd(du0, x, sc_m + token[0:1, 0:1], dxr, "mod_bwd_l0_mix")
    dmod_m0 = (dsh, dsc, dgt)
    _, dsc_c, dsh_c = modulate_bwd(duc, ctx, sc_c, jnp.zeros((tc, d), f32), "mod_bwd_ctx")

    grads, delta, new_m, new_v = {}, {}, {}, {}

    def update(n, parts, w, m, v):
        cols = w.shape[-1]
        out = adamw(parts.reshape(parts.shape[0], -1, cols), w.reshape(-1, cols), m.reshape(-1, cols), v.reshape(-1, cols), f"adamw_{n}")
        return [a.reshape(w.shape) for a in out]

    sent, landed, _ = exchange_wait(scatter_a, grad_x, "scatter_l1_ffn_wait")
    recv_a = place_own(landed, sent, True, me)
    sent, landed, _ = exchange_wait(scatter_b, grad_x, "scatter_mid_wait")
    recv_b = place_own(landed, sent, True, me)
    for n, parts in (("odd_w_in", recv_b[0]), ("odd_w_out", recv_b[1])):
        grads[n], delta[n], new_m[n], new_v[n] = update(n, parts, weights[n], mom1[n], mom2[n])
    for n, per_layer in (("ffn_w_up", (recv_b[2], recv_a[0])), ("ffn_w_down", (recv_b[3], recv_a[1]))):
        outs = [update(f"{n}_l{l}", per_layer[l], weights[n][l], mom1[n][l], mom2[n][l]) for l in range(DEPTH)]
        grads[n], delta[n], new_m[n], new_v[n] = (jnp.stack([outs[l][j] for l in range(DEPTH)]) for j in range(4))
    sent, landed, token = exchange_wait(scatter_c, new_v["ffn_w_down"], "scatter_last_wait")
    recv_c = place_own(landed, sent, True, me)
    for n, parts in (("even_w_in", recv_c[0]), ("even_w_out", recv_c[1])):
        grads[n], delta[n], new_m[n], new_v[n] = update(n, parts, weights[n], mom1[n], mom2[n])

    dmod0 = jnp.concatenate(dmod_m0 + dmod_f0, 1)
    dmod1 = jnp.concatenate(dmod_m1 + dmod_f1, 1)
    dmodc = jnp.concatenate([dsh_c, dsc_c], 1)
    d_gconv = jnp.concatenate([dwq, dwk + dwkc, dwv + dwvc], 1)[:5]
    small_g = [dmod0, dmod1, dmodc,
               jnp.concatenate([dln_m0[0], dln_f0[0], dln_m1[0], dln_f1[0]], 0),
               jnp.concatenate([dln_m0[1], dln_f0[1], dln_m1[1], dln_f1[1]], 0),
               d_gconv, (da_l + da_c)[0, 8:16], (ddt_l + ddt_c)[0, 8:16], d_nw, d_pw, d_ps,
               d_sconv[:3], d_cconv[:31], d_cg, d_cb, jnp.stack([dfcw0[:9], dfcw1[:9]])]
    gpack, goffs = _pack(small_g)
    gparts = exchange([gpack + token[0:1]], False, "gather_small_grads")[0]
    gsum = sum_parts(gparts, "sum_small_grads").reshape(-1)
    gs = [gsum[o:o + a.size].reshape(a.shape) for a, o in zip(small_g, goffs)]
    gflat = gparts.reshape(N_DEV, -1)
    dmodc_cols = _my_block(jnp.pad(gs[2], ((0, 0), (0, 4 * d))), 1, me)
    dm = jnp.stack([
        jnp.concatenate([_my_block(gflat[:, goffs[0]:goffs[0] + 6 * d], 1, me), dmodc_cols, jnp.zeros((7, ncol), f32)], 0),
        jnp.concatenate([_my_block(gflat[:, goffs[1]:goffs[1] + 6 * d], 1, me), jnp.zeros((8, ncol), f32)], 0)])
    g_ada_w, dcc = ada_backward(a_raw, ada_w, dm, "ada_backward")
    g_cctx = cctx_grad(exchange([dcc], False, "gather_cctx")[0], c_ctx[None], "cctx_grad")

    grads["c_ctx"] = g_cctx.reshape(c_ctx.shape)
    grads["ada_b"] = jnp.concatenate([gs[0] + jnp.pad(gs[2], ((0, 0), (0, 4 * d))), gs[1]], 0)
    grads["ln_g"] = _my_block(gs[3].reshape(DEPTH, 2, d), 2, me)
    grads["ln_b"] = _my_block(gs[4].reshape(DEPTH, 2, d), 2, me)
    grads["gdn_conv_w"] = _my_block(gs[5], 1, me)
    grads["gdn_a_log"] = gs[6].reshape(2, GDN_HEADS)
    grads["gdn_dt_bias"] = gs[7].reshape(2, GDN_HEADS)
    grads["gdn_norm_w"] = gs[8].reshape(LANE)
    grads["pool_w"] = gs[9]
    grads["pool_scale"] = gs[10].reshape(-1)
    grads["sconv_w"] = _my_block(gs[11], 1, me)
    grads["conf_conv_w"] = _my_block(gs[12], 1, me)
    grads["conf_ln_g"] = gs[13].reshape(-1)
    grads["conf_ln_b"] = gs[14].reshape(-1)
    grads["ffn_conv_w"] = _my_block(gs[15].reshape(DEPTH, 3, 3, D_FF), 3, me)

    def as2d(a):
        return a.reshape(-1, a.shape[-1]) if a.ndim > 1 else a.reshape(1, -1)

    small_names = [n for n in order if n in grads and n not in delta]
    res = adamw_small([(as2d(grads[n]), as2d(weights[n]), as2d(mom1[n]), as2d(mom2[n])) for n in small_names], "adamw_small")
    for n, (dl, nm, nv) in zip(small_names, res):
        delta[n], new_m[n], new_v[n] = (a.reshape(weights[n].shape) for a in (dl, nm, nv))
    grads["ada_w"], delta["ada_w"], new_m["ada_w"], new_v["ada_w"] = update("ada_w", g_ada_w[None], ada_w, m_ada_w, v_ada_w)

    return (loss, grad_x[None], *[grads[n] for n in order], *[delta[n] for n in order],
            *[new_m[n] for n in order], *[new_v[n] for n in order])
```

```python
import functools
import math

import jax
import jax.numpy as jnp
from jax import lax
from jax.experimental import pallas as pl
from jax.experimental.pallas import tpu as pltpu

f32 = jnp.float32
bf16 = jnp.bfloat16
SDS = jax.ShapeDtypeStruct

N_DEV = 8
D_MODEL = 1024
DEPTH = 2
GRID_W = 64
GDN_HEADS = 4
GDN_DK = 128
CHUNK = 64
POOL_WINDOWS = (2, 4, 8, 16)
D_FF = 2816
ALPHA = (2 * DEPTH) ** 0.25
LN_EPS = 1e-5
RMS_EPS = 1e-6
LANE = 128
PAD_ROWS = 72
CONV_ROWS = 256
VMEM_LIMIT = 56 * 2**20

ADAM_LR, ADAM_B1, ADAM_B2, ADAM_EPS, ADAM_WD, ADAM_STEP = 0.001, 0.9, 0.999, 1e-08, 0.01, 10

HI = lax.Precision.HIGHEST


def _cparams(sem=None):
    return pltpu.CompilerParams(dimension_semantics=sem, vmem_limit_bytes=VMEM_LIMIT)


def _silu(x):
    return x * jax.nn.sigmoid(x)


def _dsilu(x):
    s = jax.nn.sigmoid(x)
    return s * (1.0 + x * (1.0 - s))


def _dotb(a, b, dims=(((1,), (0,)), ((), ()))):
    return lax.dot_general(a.astype(bf16), b.astype(bf16), dims, preferred_element_type=f32)


def _dotb_nt(a, b):
    return _dotb(a, b, (((1,), (1,)), ((), ())))


def _dotb_tn(a, b):
    return _dotb(a, b, (((0,), (0,)), ((), ())))


def _dotf(a, b, dims=(((1,), (0,)), ((), ()))):
    return lax.dot_general(a, b, dims, preferred_element_type=f32, precision=HI)


def _pick(n, cands):
    for c in cands:
        if n % c == 0:
            return c
    return n


def matmul(a, b, mode, out_dtype, name, init=None):
    if mode == "nn":
        (M, K), N = a.shape, b.shape[1]
    elif mode == "nt":
        (M, K), N = a.shape, b.shape[0]
    else:
        (K, M), N = a.shape, b.shape[1]
    tm = _pick(M, (1024, 768, 512, 256, 128)) if mode != "tn" else _pick(M, (1024, 1408, 512, 256, 128))
    tn = _pick(N, (1024, 1408, 896, 768, 640, 512, 384, 256, 128))
    tk = _pick(K, (1024, 1408, 896, 768, 640, 512, 384, 256, 128)) if mode != "tn" else _pick(K, (1024, 512, 256))
    nk = K // tk
    dims = {"nn": (((1,), (0,)), ((), ())), "nt": (((1,), (1,)), ((), ())), "tn": (((0,), (0,)), ((), ()))}[mode]

    def body(a_ref, b_ref, *rest):
        o_ref, acc_ref = rest[-2:]
        k = pl.program_id(2)
        part = lax.dot_general(a_ref[...].astype(bf16), b_ref[...].astype(bf16), dims, preferred_element_type=f32)

        @pl.when(k == 0)
        def _():
            acc_ref[...] = part if init is None else part + rest[0][...]

        @pl.when(k > 0)
        def _():
            acc_ref[...] += part

        @pl.when(k == nk - 1)
        def _():
            o_ref[...] = acc_ref[...].astype(out_dtype)

    a_spec = {"nn": pl.BlockSpec((tm, tk), lambda i, j, k: (i, k)),
              "nt": pl.BlockSpec((tm, tk), lambda i, j, k: (i, k)),
              "tn": pl.BlockSpec((tk, tm), lambda i, j, k: (k, i))}[mode]
    b_spec = {"nn": pl.BlockSpec((tk, tn), lambda i, j, k: (k, j)),
              "nt": pl.BlockSpec((tn, tk), lambda i, j, k: (j, k)),
              "tn": pl.BlockSpec((tk, tn), lambda i, j, k: (k, j))}[mode]
    o_spec = pl.BlockSpec((tm, tn), lambda i, j, k: (i, j))
    return pl.pallas_call(
        body, out_shape=SDS((M, N), out_dtype), grid=(M // tm, N // tn, nk),
        in_specs=[a_spec, b_spec] + ([] if init is None else [o_spec]), out_specs=o_spec,
        scratch_shapes=[pltpu.VMEM((tm, tn), f32)], name=name,
        compiler_params=_cparams(("parallel", "parallel", "arbitrary")),
    )(*((a, b) if init is None else (a, b, init)))


def _row_tile(t):
    return _pick(t, (512, 256, 128, 64, 32, 16, 8))


def _row_spec(tt, d):
    return pl.BlockSpec((tt, d), lambda i: (i, 0))


def _vec_spec(d):
    return pl.BlockSpec((1, d), lambda i: (0, 0))


def _acc_rows(ref, val):
    @pl.when(pl.program_id(0) == 0)
    def _():
        ref[...] = val

    @pl.when(pl.program_id(0) > 0)
    def _():
        ref[...] += val


def modulate(x, scale, shift, name):
    t, d = x.shape
    tt = _row_tile(t)

    def body(x_ref, sc_ref, sh_ref, o_ref):
        o_ref[...] = (x_ref[...] * (1.0 + sc_ref[...]) + sh_ref[...]).astype(bf16)

    return pl.pallas_call(
        body, out_shape=SDS((t, d), bf16), grid=(t // tt,),
        in_specs=[_row_spec(tt, d), _vec_spec(d), _vec_spec(d)], out_specs=_row_spec(tt, d),
        name=name, compiler_params=_cparams(("parallel",)),
    )(x, scale, shift)


def modulate_bwd(du, x, scale, dres, name, du_row0=0):
    t, d = x.shape
    tt = _row_tile(t)
    blk0 = du_row0 // tt

    def body(du_ref, x_ref, sc_ref, dres_ref, dx_ref, dsc_ref, dsh_ref):
        du_v = du_ref[...]
        dx_ref[...] = du_v * (1.0 + sc_ref[...]) + dres_ref[...]
        _acc_rows(dsc_ref, jnp.sum(du_v * x_ref[...], axis=0, keepdims=True))
        _acc_rows(dsh_ref, jnp.sum(du_v, axis=0, keepdims=True))

    return pl.pallas_call(
        body, out_shape=(SDS((t, d), f32), SDS((1, d), f32), SDS((1, d), f32)), grid=(t // tt,),
        in_specs=[pl.BlockSpec((tt, d), lambda i: (i + blk0, 0)), _row_spec(tt, d), _vec_spec(d), _row_spec(tt, d)],
        out_specs=(_row_spec(tt, d), _vec_spec(d), _vec_spec(d)),
        name=name, compiler_params=_cparams(("arbitrary",)),
    )(du, x, scale, dres)


def _ln_stats(z):
    mu = jnp.mean(z, axis=-1, keepdims=True)
    zc = z - mu
    var = jnp.mean(zc * zc, axis=-1, keepdims=True)
    rstd = lax.rsqrt(var + LN_EPS)
    return zc * rstd, rstd


def _ln_bwd(dxhat, xhat, rstd):
    m1 = jnp.mean(dxhat, axis=-1, keepdims=True)
    m2 = jnp.mean(dxhat * xhat, axis=-1, keepdims=True)
    return rstd * (dxhat - m1 - xhat * m2)


def res_layernorm(x, y, gate, g, b, name, scale=None, shift=None):
    t, d = x.shape
    tt = _row_tile(t)
    with_mod = scale is not None

    def body(x_ref, y_ref, gt_ref, g_ref, b_ref, *rest):
        xhat, _ = _ln_stats(ALPHA * x_ref[...] + gt_ref[...] * y_ref[...])
        out = xhat * g_ref[...] + b_ref[...]
        if with_mod:
            sc_ref, sh_ref, o_ref, u_ref = rest
            u_ref[...] = (out * (1.0 + sc_ref[...]) + sh_ref[...]).astype(bf16)
        else:
            o_ref, = rest
        o_ref[...] = out

    rows, vec = _row_spec(tt, d), _vec_spec(d)
    return pl.pallas_call(
        body, out_shape=(SDS((t, d), f32), SDS((t, d), bf16)) if with_mod else SDS((t, d), f32), grid=(t // tt,),
        in_specs=[rows, rows, vec, vec, vec] + ([vec, vec] if with_mod else []),
        out_specs=(rows, rows) if with_mod else rows, name=name, compiler_params=_cparams(("parallel",)),
    )(*((x, y, gate, g, b) + ((scale, shift) if with_mod else ())))


def res_layernorm_bwd(dout, x, y, gate, g, name):
    t, d = x.shape
    tt = _row_tile(t)

    def body(do_ref, x_ref, y_ref, gt_ref, g_ref, dxr_ref, dy_ref, dgt_ref, dg_ref, db_ref):
        y_v = y_ref[...]
        do_v = do_ref[...]
        xhat, rstd = _ln_stats(ALPHA * x_ref[...] + gt_ref[...] * y_v)
        dz = _ln_bwd(do_v * g_ref[...], xhat, rstd)
        dxr_ref[...] = ALPHA * dz
        dy_ref[...] = (gt_ref[...] * dz).astype(bf16)
        _acc_rows(dgt_ref, jnp.sum(dz * y_v, axis=0, keepdims=True))
        _acc_rows(dg_ref, jnp.sum(do_v * xhat, axis=0, keepdims=True))
        _acc_rows(db_ref, jnp.sum(do_v, axis=0, keepdims=True))

    vec = SDS((1, d), f32)
    return pl.pallas_call(
        body, out_shape=(SDS((t, d), f32), SDS((t, d), bf16), vec, vec, vec), grid=(t // tt,),
        in_specs=[_row_spec(tt, d), _row_spec(tt, d), _row_spec(tt, d), _vec_spec(d), _vec_spec(d)],
        out_specs=(_row_spec(tt, d), _row_spec(tt, d), _vec_spec(d), _vec_spec(d), _vec_spec(d)),
        name=name, compiler_params=_cparams(("arbitrary",)),
    )(dout, x, y, gate, g)


def modulate_res_layernorm_bwd(du, scale, dres, x, y, gate, g, b, name):
    t, d = x.shape
    tt = _row_tile(t)

    def body(du_ref, sc_ref, dres_ref, x_ref, y_ref, gt_ref, g_ref, b_ref,
             dxr_ref, dy_ref, dsc_ref, dsh_ref, dgt_ref, dg_ref, db_ref):
        y_v, du_v = y_ref[...], du_ref[...]
        xhat, rstd = _ln_stats(ALPHA * x_ref[...] + gt_ref[...] * y_v)
        do_v = du_v * (1.0 + sc_ref[...]) + dres_ref[...]
        dz = _ln_bwd(do_v * g_ref[...], xhat, rstd)
        dxr_ref[...] = ALPHA * dz
        dy_ref[...] = (gt_ref[...] * dz).astype(bf16)
        _acc_rows(dsc_ref, jnp.sum(du_v * (xhat * g_ref[...] + b_ref[...]), axis=0, keepdims=True))
        _acc_rows(dsh_ref, jnp.sum(du_v, axis=0, keepdims=True))
        _acc_rows(dgt_ref, jnp.sum(dz * y_v, axis=0, keepdims=True))
        _acc_rows(dg_ref, jnp.sum(do_v * xhat, axis=0, keepdims=True))
        _acc_rows(db_ref, jnp.sum(do_v, axis=0, keepdims=True))

    rows, vec, vshape = _row_spec(tt, d), _vec_spec(d), SDS((1, d), f32)
    return pl.pallas_call(
        body, out_shape=(SDS((t, d), f32), SDS((t, d), bf16)) + (vshape,) * 5, grid=(t // tt,),
        in_specs=[rows, vec, rows, rows, rows, vec, vec, vec], out_specs=(rows, rows) + (vec,) * 5,
        name=name, compiler_params=_cparams(("arbitrary",)),
    )(du, scale, dres, x, y, gate, g, b)


def res_layernorm_loss(x, y, gate, g, b, target, name):
    t, d = x.shape
    tt = _row_tile(t)

    def body(x_ref, y_ref, gt_ref, g_ref, b_ref, t_ref, l_ref, dxr_ref, dy_ref, dgt_ref, dg_ref, db_ref):
        y_v = y_ref[...]
        xhat, rstd = _ln_stats(ALPHA * x_ref[...] + gt_ref[...] * y_v)
        e = xhat * g_ref[...] + b_ref[...] - t_ref[...]
        part = jnp.sum(jnp.sum(e * e, axis=1, keepdims=True), axis=0, keepdims=True) * (0.5 / d)
        _acc_rows(l_ref, jnp.broadcast_to(part, (1, LANE)))
        do_v = e * (1.0 / d)
        dz = _ln_bwd(do_v * g_ref[...], xhat, rstd)
        dxr_ref[...] = ALPHA * dz
        dy_ref[...] = (gt_ref[...] * dz).astype(bf16)
        _acc_rows(dgt_ref, jnp.sum(dz * y_v, axis=0, keepdims=True))
        _acc_rows(dg_ref, jnp.sum(do_v * xhat, axis=0, keepdims=True))
        _acc_rows(db_ref, jnp.sum(do_v, axis=0, keepdims=True))

    rows, vec, vshape = _row_spec(tt, d), _vec_spec(d), SDS((1, d), f32)
    return pl.pallas_call(
        body, out_shape=(SDS((1, LANE), f32), SDS((t, d), f32), SDS((t, d), bf16)) + (vshape,) * 3, grid=(t // tt,),
        in_specs=[rows, rows, vec, vec, vec, rows],
        out_specs=(pl.BlockSpec((1, LANE), lambda i: (0, 0)), rows, rows) + (vec,) * 3,
        name=name, compiler_params=_cparams(("arbitrary",)),
    )(x, y, gate, g, b, target)


def loss_head(y, target, name):
    t, d = y.shape
    tt = _row_tile(t)

    def body(y_ref, t_ref, l_ref, dy_ref):
        e = y_ref[...] - t_ref[...]
        dy_ref[...] = e * (1.0 / d)
        part = jnp.sum(jnp.sum(e * e, axis=1, keepdims=True), axis=0, keepdims=True) * (0.5 / d)
        _acc_rows(l_ref, jnp.broadcast_to(part, (1, LANE)))

    return pl.pallas_call(
        body, out_shape=(SDS((1, LANE), f32), SDS((t, d), f32)), grid=(t // tt,),
        in_specs=[_row_spec(tt, d), _row_spec(tt, d)],
        out_specs=(pl.BlockSpec((1, LANE), lambda i: (0, 0)), _row_spec(tt, d)),
        name=name, compiler_params=_cparams(("arbitrary",)),
    )(y, target)


def _fill_pad(pad_ref, val, t):
    zeros = jnp.zeros((PAD_ROWS, LANE), f32)
    pad_ref[0:PAD_ROWS, :] = zeros
    pad_ref[PAD_ROWS + t:2 * PAD_ROWS + t, :] = zeros
    pad_ref[PAD_ROWS:PAD_ROWS + t, :] = val


def _grid_pads_set(pads, r0, val):
    rows = val.shape[0]
    col = (lax.broadcasted_iota(jnp.int32, (rows, 1), 0) + r0) % GRID_W
    base = PAD_ROWS + r0
    pads[0][base + 1:base + 1 + rows, :] = val * (col <= GRID_W - 2).astype(f32)
    pads[1][base:base + rows, :] = val
    pads[2][base - 1:base - 1 + rows, :] = val * (col >= 1).astype(f32)


def _grid_pads_clear_edges(pads, t):
    zeros = jnp.zeros((PAD_ROWS + 8, LANE), f32)
    for p in pads:
        p[0:PAD_ROWS + 8, :] = zeros
        p[PAD_ROWS + t - 8:2 * PAD_ROWS + t, :] = zeros


def _tap_source(pads, dc):
    return pads if dc is None else pads[dc + 1]


def _taps_apply(pads, w_ref, taps, r0, rows):
    acc = jnp.zeros((rows, LANE), f32)
    for off, dc, wi in taps:
        xs = _tap_source(pads, dc)[PAD_ROWS + r0 + off:PAD_ROWS + r0 + off + rows, :]
        acc = acc + w_ref[wi:wi + 1, :] * xs
    return acc


def _taps_wgrad(pads, dy, taps, r0, rows, nw):
    out = jnp.zeros((nw, LANE), f32)
    rid = lax.broadcasted_iota(jnp.int32, (nw, 1), 0)
    for off, dc, wi in taps:
        xs = _tap_source(pads, dc)[PAD_ROWS + r0 + off:PAD_ROWS + r0 + off + rows, :]
        s = jnp.sum(dy * xs, axis=0, keepdims=True)
        out = out + jnp.where(rid == wi, s, 0.0)
    return out


def _transpose_taps(taps):
    return [(-off, None if dc is None else -dc, wi) for off, dc, wi in taps]


def _taps_1d(width):
    return [(j - width // 2, None, j) for j in range(width)]


def _taps_grid3():
    return [(GRID_W * dr, dc, 3 * (dr + 1) + (dc + 1)) for dr in (-1, 0, 1) for dc in (-1, 0, 1)]


def _row_chunks(t):
    r = min(CONV_ROWS, t)
    return [(i * r, r) for i in range(t // r)]


def _col_spec(t, off):
    return pl.BlockSpec((t, LANE), lambda c: (0, c + off))


def _w_spec(nw, off=0):
    return pl.BlockSpec((nw, LANE), lambda c: (0, c + off))


def gdn_conv(p, w, col0, nblk, norm_scale, name):
    t = p.shape[0]
    nw = w.shape[0]
    taps = _taps_1d(5)

    def body(p_ref, w_ref, o_ref, pad_ref):
        _fill_pad(pad_ref, p_ref[...], t)
        for r0, rows in _row_chunks(t):
            a = _silu(_taps_apply(pad_ref, w_ref, taps, r0, rows))
            if norm_scale is not None:
                a = a * (lax.rsqrt(jnp.sum(a * a, axis=-1, keepdims=True) + RMS_EPS) * norm_scale)
            o_ref[r0:r0 + rows, :] = a

    return pl.pallas_call(
        body, out_shape=SDS((t, nblk * LANE), f32), grid=(nblk,),
        in_specs=[_col_spec(t, col0), _w_spec(nw, col0)], out_specs=_col_spec(t, 0),
        scratch_shapes=[pltpu.VMEM((t + 2 * PAD_ROWS, LANE), f32)], name=name,
        compiler_params=_cparams(("parallel",)),
    )(p, w)


def gdn_conv_bwd(p, w, d_a, d_b, col0, nblk, norm_scale, name):
    t = p.shape[0]
    nw = w.shape[0]
    taps = _taps_1d(5)
    ttaps = _transpose_taps(taps)

    def body(p_ref, w_ref, da_ref, db_ref, dp_ref, dw_ref, pad_ref, gpad_ref):
        _fill_pad(pad_ref, p_ref[...], t)
        for r0, rows in _row_chunks(t):
            pre = _taps_apply(pad_ref, w_ref, taps, r0, rows)
            a = _silu(pre)
            dy = da_ref[r0:r0 + rows, :] + db_ref[r0:r0 + rows, :]
            if norm_scale is not None:
                r = lax.rsqrt(jnp.sum(a * a, axis=-1, keepdims=True) + RMS_EPS)
                da = norm_scale * (dy * r - a * (r * r * r) * jnp.sum(dy * a, axis=-1, keepdims=True))
            else:
                da = dy
            gpad_ref[PAD_ROWS + r0:PAD_ROWS + r0 + rows, :] = da * _dsilu(pre)
        zeros = jnp.zeros((PAD_ROWS, LANE), f32)
        gpad_ref[0:PAD_ROWS, :] = zeros
        gpad_ref[PAD_ROWS + t:2 * PAD_ROWS + t, :] = zeros
        dw = jnp.zeros((nw, LANE), f32)
        for r0, rows in _row_chunks(t):
            dp_ref[r0:r0 + rows, :] = _taps_apply(gpad_ref, w_ref, ttaps, r0, rows).astype(bf16)
            dw = dw + _taps_wgrad(pad_ref, gpad_ref[PAD_ROWS + r0:PAD_ROWS + r0 + rows, :], taps, r0, rows, nw)
        dw_ref[...] = dw

    return pl.pallas_call(
        body, out_shape=(SDS((t, nblk * LANE), bf16), SDS((nw, nblk * LANE), f32)), grid=(nblk,),
        in_specs=[_col_spec(t, col0), _w_spec(nw, col0), _col_spec(t, 0), _col_spec(t, 0)],
        out_specs=(_col_spec(t, 0), _w_spec(nw)),
        scratch_shapes=[pltpu.VMEM((t + 2 * PAD_ROWS, LANE), f32)] * 2, name=name,
        compiler_params=_cparams(("parallel",)),
    )(p, w, d_a, d_b)


def short_conv(p, w, name):
    t = p.shape[0]
    nw = w.shape[0]
    taps = _taps_1d(3)

    def body(gb_ref, gc_ref, h_ref, w_ref, o_ref, pad_ref):
        _fill_pad(pad_ref, gc_ref[...] * h_ref[...], t)
        for r0, rows in _row_chunks(t):
            o_ref[r0:r0 + rows, :] = (gb_ref[r0:r0 + rows, :] * _taps_apply(pad_ref, w_ref, taps, r0, rows)).astype(bf16)

    return pl.pallas_call(
        body, out_shape=SDS((t, 4 * LANE), bf16), grid=(4,),
        in_specs=[_col_spec(t, 0), _col_spec(t, 4), _col_spec(t, 8), _w_spec(nw)], out_specs=_col_spec(t, 0),
        scratch_shapes=[pltpu.VMEM((t + 2 * PAD_ROWS, LANE), f32)], name=name,
        compiler_params=_cparams(("parallel",)),
    )(p, p, p, w)


def short_conv_bwd(p, w, dy, name):
    t = p.shape[0]
    nw = w.shape[0]
    taps = _taps_1d(3)
    ttaps = _transpose_taps(taps)

    def body(gb_ref, gc_ref, h_ref, w_ref, dy_ref, dgb_ref, dgc_ref, dh_ref, dw_ref, pad_ref, gpad_ref):
        _fill_pad(pad_ref, gc_ref[...] * h_ref[...], t)
        _fill_pad(gpad_ref, dy_ref[...] * gb_ref[...], t)
        dw = jnp.zeros((nw, LANE), f32)
        for r0, rows in _row_chunks(t):
            sl = slice(r0, r0 + rows)
            dgb_ref[sl, :] = (dy_ref[sl, :] * _taps_apply(pad_ref, w_ref, taps, r0, rows)).astype(bf16)
            dm = _taps_apply(gpad_ref, w_ref, ttaps, r0, rows)
            dgc_ref[sl, :] = (dm * h_ref[sl, :]).astype(bf16)
            dh_ref[sl, :] = (dm * gc_ref[sl, :]).astype(bf16)
            dw = dw + _taps_wgrad(pad_ref, gpad_ref[PAD_ROWS + r0:PAD_ROWS + r0 + rows, :], taps, r0, rows, nw)
        dw_ref[...] = dw

    blk = SDS((t, 4 * LANE), bf16)
    return pl.pallas_call(
        body, out_shape=(blk, blk, blk, SDS((nw, 4 * LANE), f32)), grid=(4,),
        in_specs=[_col_spec(t, 0), _col_spec(t, 4), _col_spec(t, 8), _w_spec(nw), _col_spec(t, 0)],
        out_specs=(_col_spec(t, 0), _col_spec(t, 0), _col_spec(t, 0), _w_spec(nw)),
        scratch_shapes=[pltpu.VMEM((t + 2 * PAD_ROWS, LANE), f32)] * 2, name=name,
        compiler_params=_cparams(("parallel",)),
    )(p, p, p, w, dy)


def conf_conv(p, w, name):
    t = p.shape[0]
    nw = w.shape[0]
    taps = _taps_1d(31)

    def body(a_ref, b_ref, w_ref, o_ref, pad_ref):
        _fill_pad(pad_ref, a_ref[...] * jax.nn.sigmoid(b_ref[...]), t)
        for r0, rows in _row_chunks(t):
            o_ref[r0:r0 + rows, :] = _taps_apply(pad_ref, w_ref, taps, r0, rows)

    return pl.pallas_call(
        body, out_shape=SDS((t, 4 * LANE), f32), grid=(4,),
        in_specs=[_col_spec(t, 12), _col_spec(t, 16), _w_spec(nw)], out_specs=_col_spec(t, 0),
        scratch_shapes=[pltpu.VMEM((t + 2 * PAD_ROWS, LANE), f32)], name=name,
        compiler_params=_cparams(("parallel",)),
    )(p, p, w)


def conf_conv_bwd(p, w, dz, name):
    t = p.shape[0]
    nw = w.shape[0]
    taps = _taps_1d(31)
    ttaps = _transpose_taps(taps)

    def body(a_ref, b_ref, w_ref, dz_ref, da_ref, db_ref, dw_ref, pad_ref, gpad_ref):
        _fill_pad(pad_ref, a_ref[...] * jax.nn.sigmoid(b_ref[...]), t)
        _fill_pad(gpad_ref, dz_ref[...], t)
        dw = jnp.zeros((nw, LANE), f32)
        for r0, rows in _row_chunks(t):
            sl = slice(r0, r0 + rows)
            dm = _taps_apply(gpad_ref, w_ref, ttaps, r0, rows)
            sg = jax.nn.sigmoid(b_ref[sl, :])
            da_ref[sl, :] = (dm * sg).astype(bf16)
            db_ref[sl, :] = (dm * a_ref[sl, :] * sg * (1.0 - sg)).astype(bf16)
            dw = dw + _taps_wgrad(pad_ref, dz_ref[sl, :], taps, r0, rows, nw)
        dw_ref[...] = dw

    blk = SDS((t, 4 * LANE), bf16)
    return pl.pallas_call(
        body, out_shape=(blk, blk, SDS((nw, 4 * LANE), f32)), grid=(4,),
        in_specs=[_col_spec(t, 12), _col_spec(t, 16), _w_spec(nw), _col_spec(t, 0)],
        out_specs=(_col_spec(t, 0), _col_spec(t, 0), _w_spec(nw)),
        scratch_shapes=[pltpu.VMEM((t + 2 * PAD_ROWS, LANE), f32)] * 2, name=name,
        compiler_params=_cparams(("parallel",)),
    )(p, p, w, dz)


def ffn_conv(h, w, name):
    t = h.shape[0]
    width = 2 * LANE
    nblk = D_FF // width
    nw = w.shape[0]
    taps = _taps_grid3()

    def body(a_ref, g_ref, w_ref, o_ref, *pads):
        for s in range(width // LANE):
            ls = slice(s * LANE, (s + 1) * LANE)
            _grid_pads_clear_edges(pads, t)
            for r0, rows in _row_chunks(t):
                _grid_pads_set(pads, r0, a_ref[r0:r0 + rows, ls])
            for r0, rows in _row_chunks(t):
                conv = _taps_apply(pads, w_ref.at[:, ls], taps, r0, rows)
                o_ref[r0:r0 + rows, ls] = (_silu(conv) * g_ref[r0:r0 + rows, ls]).astype(bf16)

    spec = lambda off: pl.BlockSpec((t, width), lambda c: (0, c + off))
    return pl.pallas_call(
        body, out_shape=SDS((t, D_FF), bf16), grid=(nblk,),
        in_specs=[spec(0), spec(nblk), pl.BlockSpec((nw, width), lambda c: (0, c))], out_specs=spec(0),
        scratch_shapes=[pltpu.VMEM((t + 2 * PAD_ROWS, LANE), f32)] * 3, name=name,
        compiler_params=_cparams(("parallel",)),
    )(h, h, w)


def ffn_conv_bwd(h, w, df, name):
    t = h.shape[0]
    nblk = D_FF // LANE
    nw = w.shape[0]
    taps = _taps_grid3()
    ttaps = _transpose_taps(taps)

    def body(a_ref, g_ref, w_ref, df_ref, dh_ref, dw_ref, *all_pads):
        half = pl.program_id(1)
        pads, gpads = all_pads[:3], all_pads[3:]

        @pl.when(half == 0)
        def _():
            _grid_pads_clear_edges(all_pads, t)
            for r0, rows in _row_chunks(t):
                _grid_pads_set(pads, r0, a_ref[r0:r0 + rows, :])
            for r0, rows in _row_chunks(t):
                sl = slice(r0, r0 + rows)
                pre = _taps_apply(pads, w_ref, taps, r0, rows)
                _grid_pads_set(gpads, r0, df_ref[sl, :] * g_ref[sl, :] * _dsilu(pre))
                dh_ref[sl, :] = (df_ref[sl, :] * _silu(pre)).astype(bf16)

        @pl.when(half == 1)
        def _():
            dw = jnp.zeros((nw, LANE), f32)
            for r0, rows in _row_chunks(t):
                dh_ref[r0:r0 + rows, :] = _taps_apply(gpads, w_ref, ttaps, r0, rows).astype(bf16)
                dw = dw + _taps_wgrad(pads, gpads[1][PAD_ROWS + r0:PAD_ROWS + r0 + rows, :], taps, r0, rows, nw)
            dw_ref[...] = dw

    cspec = lambda off: pl.BlockSpec((t, LANE), lambda c, s: (0, c + off))
    return pl.pallas_call(
        body, out_shape=(SDS((t, 2 * D_FF), bf16), SDS((nw, D_FF), f32)), grid=(nblk, 2),
        in_specs=[cspec(0), cspec(nblk), pl.BlockSpec((nw, LANE), lambda c, s: (0, c)), cspec(0)],
        out_specs=(pl.BlockSpec((t, LANE), lambda c, s: (0, c + nblk * (1 - s))), pl.BlockSpec((nw, LANE), lambda c, s: (0, c))),
        scratch_shapes=[pltpu.VMEM((t + 2 * PAD_ROWS, LANE), f32)] * 6, name=name,
        compiler_params=_cparams(("parallel", "arbitrary")),
    )(h, h, w, df)


def _pool_count(r0, rows, win, t):
    pos = lax.broadcasted_iota(jnp.int32, (rows, 1), 0) + r0
    lo = jnp.clip(pos - win // 2, 0, t)
    hi = jnp.clip(pos - win // 2 + win, 0, t)
    return (hi - lo).astype(f32)


def _window_sum(pad_ref, r0, rows, lo, hi):
    acc = jnp.zeros((rows, LANE), f32)
    for off in range(lo, hi):
        acc = acc + pad_ref[PAD_ROWS + r0 + off:PAD_ROWS + r0 + off + rows, :]
    return acc


def pool_mix(p, pool_w, pool_scale, name):
    t = p.shape[0]

    def body(x_ref, w_ref, s_ref, o_ref, pad_ref):
        for gi, win in enumerate(POOL_WINDOWS):
            cs = slice(gi * LANE, (gi + 1) * LANE)
            _fill_pad(pad_ref, x_ref[:, cs], t)
            wg = w_ref[gi].astype(bf16)
            for r0, rows in _row_chunks(t):
                pooled = _window_sum(pad_ref, r0, rows, -(win // 2), win - win // 2) / _pool_count(r0, rows, win, t) - x_ref[r0:r0 + rows, cs]
                o_ref[r0:r0 + rows, cs] = (_dotb(pooled, wg) * s_ref[:, cs]).astype(bf16)

    return pl.pallas_call(
        body, out_shape=SDS((t, 512), bf16), grid=(1,),
        in_specs=[pl.BlockSpec((t, 512), lambda i: (0, 4)), pl.BlockSpec((4, LANE, LANE), lambda i: (0, 0, 0)),
                  pl.BlockSpec((1, 512), lambda i: (0, 0))],
        out_specs=pl.BlockSpec((t, 512), lambda i: (0, 0)),
        scratch_shapes=[pltpu.VMEM((t + 2 * PAD_ROWS, LANE), f32)], name=name,
        compiler_params=_cparams(("arbitrary",)),
    )(p, pool_w, pool_scale)


def pool_mix_bwd(p, pool_w, pool_scale, dmix, name):
    t = p.shape[0]

    def body(x_ref, w_ref, s_ref, dy_ref, dp_ref, dw_ref, ds_ref, pad_ref, gpad_ref, dpool_ref):
        for gi, win in enumerate(POOL_WINDOWS):
            cs = slice(gi * LANE, (gi + 1) * LANE)
            h = win // 2
            _fill_pad(pad_ref, x_ref[:, cs], t)
            wg = w_ref[gi].astype(bf16)
            dw = jnp.zeros((LANE, LANE), f32)
            ds = jnp.zeros((1, LANE), f32)
            zeros = jnp.zeros((PAD_ROWS, LANE), f32)
            gpad_ref[0:PAD_ROWS, :] = zeros
            gpad_ref[PAD_ROWS + t:2 * PAD_ROWS + t, :] = zeros
            for r0, rows in _row_chunks(t):
                cnt = _pool_count(r0, rows, win, t)
                pooled = _window_sum(pad_ref, r0, rows, -h, win - h) / cnt - x_ref[r0:r0 + rows, cs]
                dy = dy_ref[r0:r0 + rows, cs]
                ds = ds + jnp.sum(dy * _dotb(pooled, wg), axis=0, keepdims=True)
                dypre = dy * s_ref[:, cs]
                dw = dw + _dotb_tn(pooled, dypre)
                dpooled = _dotb_nt(dypre, wg)
                gpad_ref[PAD_ROWS + r0:PAD_ROWS + r0 + rows, :] = dpooled / cnt
                dpool_ref[r0:r0 + rows, :] = dpooled
            dw_ref[gi] = dw
            ds_ref[:, cs] = ds
            for r0, rows in _row_chunks(t):
                dx = _window_sum(gpad_ref, r0, rows, -h + 1, h + 1) - dpool_ref[r0:r0 + rows, :]
                dp_ref[r0:r0 + rows, cs] = dx.astype(bf16)

    return pl.pallas_call(
        body, out_shape=(SDS((t, 512), bf16), SDS((4, LANE, LANE), f32), SDS((1, 512), f32)), grid=(1,),
        in_specs=[pl.BlockSpec((t, 512), lambda i: (0, 4)), pl.BlockSpec((4, LANE, LANE), lambda i: (0, 0, 0)),
                  pl.BlockSpec((1, 512), lambda i: (0, 0)), pl.BlockSpec((t, 512), lambda i: (0, 1))],
        out_specs=(pl.BlockSpec((t, 512), lambda i: (0, 0)), pl.BlockSpec((4, LANE, LANE), lambda i: (0, 0, 0)),
                   pl.BlockSpec((1, 512), lambda i: (0, 0))),
        scratch_shapes=[pltpu.VMEM((t + 2 * PAD_ROWS, LANE), f32)] * 2 + [pltpu.VMEM((t, LANE), f32)], name=name,
        compiler_params=_cparams(("arbitrary",)),
    )(p, pool_w, pool_scale, dmix)


def gated_rmsnorm(o_a, o_b, p, norm_w, name):
    t = o_a.shape[0]
    tt = _row_tile(t)

    def body(oa_ref, ob_ref, g_ref, nw_ref, y_ref):
        for h in range(GDN_HEADS):
            cs = slice(h * LANE, (h + 1) * LANE)
            o = oa_ref[:, cs] + ob_ref[:, cs]
            r = lax.rsqrt(jnp.mean(o * o, axis=-1, keepdims=True) + RMS_EPS)
            y_ref[:, cs] = (o * r * nw_ref[...] * _silu(g_ref[:, cs])).astype(bf16)

    return pl.pallas_call(
        body, out_shape=SDS((t, 512), bf16), grid=(t // tt,),
        in_specs=[_row_spec(tt, 512), _row_spec(tt, 512), pl.BlockSpec((tt, 512), lambda i: (i, 3)), _vec_spec(LANE)],
        out_specs=_row_spec(tt, 512), name=name, compiler_params=_cparams(("parallel",)),
    )(o_a, o_b, p, norm_w)


def gated_rmsnorm_bwd(o_a, o_b, p, norm_w, dmix, name):
    t = o_a.shape[0]
    tt = _row_tile(t)

    def body(oa_ref, ob_ref, g_ref, nw_ref, dy_ref, do_ref, dg_ref, dnw_ref):
        dnw = jnp.zeros((1, LANE), f32)
        for h in range(GDN_HEADS):
            cs = slice(h * LANE, (h + 1) * LANE)
            o = oa_ref[:, cs] + ob_ref[:, cs]
            r = lax.rsqrt(jnp.mean(o * o, axis=-1, keepdims=True) + RMS_EPS)
            gate = g_ref[:, cs]
            dy = dy_ref[:, cs]
            dy1 = dy * _silu(gate)
            dg_ref[:, cs] = (dy * (o * r * nw_ref[...]) * _dsilu(gate)).astype(bf16)
            dnw = dnw + jnp.sum(dy1 * o * r, axis=0, keepdims=True)
            dn = dy1 * nw_ref[...]
            do_ref[:, cs] = r * dn - o * (r * r * r) * jnp.mean(dn * o, axis=-1, keepdims=True)
        _acc_rows(dnw_ref, dnw)

    return pl.pallas_call(
        body, out_shape=(SDS((t, 512), f32), SDS((t, 512), bf16), SDS((1, LANE), f32)), grid=(t // tt,),
        in_specs=[_row_spec(tt, 512), _row_spec(tt, 512), pl.BlockSpec((tt, 512), lambda i: (i, 3)), _vec_spec(LANE),
                  _row_spec(tt, 512)],
        out_specs=(_row_spec(tt, 512), _row_spec(tt, 512), _vec_spec(LANE)),
        name=name, compiler_params=_cparams(("arbitrary",)),
    )(o_a, o_b, p, norm_w, dmix)


def ln_silu(z, g, b, name):
    t, d = z.shape
    tt = _row_tile(t)

    def body(z_ref, g_ref, b_ref, o_ref):
        xhat, _ = _ln_stats(z_ref[...])
        o_ref[...] = _silu(xhat * g_ref[...] + b_ref[...]).astype(bf16)

    return pl.pallas_call(
        body, out_shape=SDS((t, d), bf16), grid=(t // tt,),
        in_specs=[_row_spec(tt, d), _vec_spec(d), _vec_spec(d)], out_specs=_row_spec(tt, d),
        name=name, compiler_params=_cparams(("parallel",)),
    )(z, g, b)


def ln_silu_bwd(z, g, b, dmix, name):
    t, d = z.shape
    tt = _row_tile(t)

    def body(z_ref, g_ref, b_ref, dy_ref, dz_ref, dg_ref, db_ref):
        xhat, rstd = _ln_stats(z_ref[...])
        dn = dy_ref[...] * _dsilu(xhat * g_ref[...] + b_ref[...])
        dz_ref[...] = _ln_bwd(dn * g_ref[...], xhat, rstd)
        _acc_rows(dg_ref, jnp.sum(dn * xhat, axis=0, keepdims=True))
        _acc_rows(db_ref, jnp.sum(dn, axis=0, keepdims=True))

    return pl.pallas_call(
        body, out_shape=(SDS((t, d), f32), SDS((1, d), f32), SDS((1, d), f32)), grid=(t // tt,),
        in_specs=[_row_spec(tt, d), _vec_spec(d), _vec_spec(d), pl.BlockSpec((tt, d), lambda i: (i, 1))],
        out_specs=(_row_spec(tt, d), _vec_spec(d), _vec_spec(d)),
        name=name, compiler_params=_cparams(("arbitrary",)),
    )(z, g, b, dmix)


def gdn_gates(p, neg_a, dt_bias, name):
    t = p.shape[0]
    tt = _row_tile(t)

    def body(s_ref, na_ref, dt_ref, o_ref):
        s = s_ref[...]
        col = lax.broadcasted_iota(jnp.int32, s.shape, 1)
        o_ref[...] = jnp.where(col < 8, jax.nn.sigmoid(s), na_ref[...] * jax.nn.softplus(s + dt_ref[...]))

    return pl.pallas_call(
        body, out_shape=SDS((t, LANE), f32), grid=(t // tt,),
        in_specs=[pl.BlockSpec((tt, LANE), lambda i: (i, 20)), _vec_spec(LANE), _vec_spec(LANE)],
        out_specs=_row_spec(tt, LANE), name=name, compiler_params=_cparams(("parallel",)),
    )(p, neg_a, dt_bias)


def gdn_gates_bwd(p, neg_a, dt_bias, dbg_a, dbg_b, name):
    t = p.shape[0]
    tt = _row_tile(t)

    def body(s_ref, na_ref, dt_ref, d_ref, d2_ref, ds_ref, da_ref, ddt_ref):
        s = s_ref[...]
        d = d_ref[...] + d2_ref[...]
        col = lax.broadcasted_iota(jnp.int32, s.shape, 1)
        sg = jax.nn.sigmoid(s)
        z = s + dt_ref[...]
        dz = jnp.where((col >= 8) & (col < 16), d * na_ref[...] * jax.nn.sigmoid(z), 0.0)
        ds_ref[...] = jnp.where(col < 8, d * sg * (1.0 - sg), dz).astype(bf16)
        dalog = jnp.where((col >= 8) & (col < 16), d * na_ref[...] * jax.nn.softplus(z), 0.0)
        _acc_rows(da_ref, jnp.sum(dalog, axis=0, keepdims=True))
        _acc_rows(ddt_ref, jnp.sum(dz, axis=0, keepdims=True))

    return pl.pallas_call(
        body, out_shape=(SDS((t, LANE), bf16), SDS((1, LANE), f32), SDS((1, LANE), f32)), grid=(t // tt,),
        in_specs=[pl.BlockSpec((tt, LANE), lambda i: (i, 20)), _vec_spec(LANE), _vec_spec(LANE), _row_spec(tt, LANE),
                  _row_spec(tt, LANE)],
        out_specs=(_row_spec(tt, LANE), _vec_spec(LANE), _vec_spec(LANE)),
        name=name, compiler_params=_cparams(("arbitrary",)),
    )(p, neg_a, dt_bias, dbg_a, dbg_b)


N_SCAN = 2 * GDN_HEADS


def _bdot(a, b, ca, cb, precision=None):
    if precision is None:
        a, b = a.astype(bf16), b.astype(bf16)
    return lax.dot_general(a, b, (((ca,), (cb,)), ((0,), (0,))), preferred_element_type=f32, precision=precision)


def _bdot_nn(a, b, precision=None):
    return _bdot(a, b, 2, 1, precision)


def _bdot_nt(a, b):
    return _bdot(a, b, 2, 2)


def _bdot_tn(a, b, precision=None):
    return _bdot(a, b, 1, 1, precision)


def _order_masks():
    shape = (N_SCAN, CHUNK, CHUNK)
    sign = jnp.where(lax.broadcasted_iota(jnp.int32, shape, 0) >= GDN_HEADS, -1, 1)
    ahead = (lax.broadcasted_iota(jnp.int32, shape, 1) - lax.broadcasted_iota(jnp.int32, shape, 2)) * sign
    lower, strict, lower_t = ahead >= 0, ahead > 0, ahead <= 0
    col_shape = (N_SCAN, CHUNK, 1)
    back1 = lax.broadcasted_iota(jnp.int32, col_shape, 0) >= GDN_HEADS
    row1 = lax.broadcasted_iota(jnp.int32, col_shape, 1)
    at_last = (row1 == jnp.where(back1, 0, CHUNK - 1)).astype(f32)
    return lower, strict, lower_t, at_last


def _stack_heads(f_ref, b_ref):
    return jnp.stack([ref[:, h * LANE:(h + 1) * LANE] for ref in (f_ref, b_ref) for h in range(GDN_HEADS)])


def _stack_gates(bgf, bgb, bgtf, bgtb):
    beta = jnp.stack([bg[:, 4 * d + h:4 * d + h + 1] for d, bg in enumerate((bgf, bgb)) for h in range(GDN_HEADS)])
    g_col = jnp.stack([bg[:, 8 + 4 * d + h:9 + 4 * d + h] for d, bg in enumerate((bgf, bgb)) for h in range(GDN_HEADS)])
    g_row = jnp.stack([bgt[8 + 4 * d + h:9 + 4 * d + h, :] for d, bgt in enumerate((bgtf, bgtb)) for h in range(GDN_HEADS)])
    return beta, g_col, g_row


def _chunk_terms(k, v, beta, g_col, g_row, masks, tinv=None):
    lower, strict, lower_t, at_last = masks
    gc = jnp.sum(lower.astype(f32) * g_row, axis=2, keepdims=True)
    gr = jnp.sum(lower_t.astype(f32) * g_col, axis=1, keepdims=True)
    g_last = jnp.sum(at_last * gc, axis=1, keepdims=True)
    e = jnp.exp(gc)
    f = jnp.exp(g_last - gc)
    dm = jnp.exp(jnp.where(lower, gc - gr, -1e30))
    kb = k * beta
    kk = _bdot_nt(kb, k)
    if tinv is None:
        shape = (N_SCAN, CHUNK, CHUNK)
        eye = (lax.broadcasted_iota(jnp.int32, shape, 1) == lax.broadcasted_iota(jnp.int32, shape, 2)).astype(f32)
        pw = -jnp.where(strict, kk * dm, 0.0)
        tinv = eye + pw
        for _ in range(5):
            pw = _bdot_nn(pw, pw, lax.Precision.HIGH)
            tinv = tinv + _bdot_nn(tinv, pw, lax.Precision.HIGH)
    u = _bdot_nn(tinv, v * beta)
    w = _bdot_nn(tinv, kb * e)
    return dict(e=e, f=f, gl=jnp.exp(g_last), dm=dm, kb=kb, kk=kk, tinv=tinv, u=u, w=w, kd=k * f)


def _gdn_specs(nc, width, step_chunk):
    return [pl.BlockSpec((CHUNK, width), functools.partial(lambda i, d: (step_chunk(i, d), 0), d=d)) for d in (0, 1)]


def gdn_forward(q, k, v, bg, bgt, s0, with_out, name):
    t = k.shape[0]
    nc = t // CHUNK

    def body(qf_ref, qb_ref, kf_ref, kb_ref, vf_ref, vb_ref, bgf_ref, bgb_ref, bgtf_ref, bgtb_ref, s0_ref,
             of_ref, ob_ref, sallf_ref, sallb_ref, tinvf_ref, tinvb_ref, sfin_ref, s_ref):
        i = pl.program_id(0)

        @pl.when(i == 0)
        def _():
            s_ref[...] = s0_ref[...]

        masks = _order_masks()
        k8, v8 = _stack_heads(kf_ref, kb_ref), _stack_heads(vf_ref, vb_ref)
        beta, g_col, g_row = _stack_gates(bgf_ref[...], bgb_ref[...], bgtf_ref[0], bgtb_ref[0])
        c = _chunk_terms(k8, v8, beta, g_col, g_row, masks)
        s = s_ref[...]
        sallf_ref[0] = s[:GDN_HEADS]
        sallb_ref[0] = s[GDN_HEADS:]
        tinvf_ref[0] = c["tinv"][:GDN_HEADS]
        tinvb_ref[0] = c["tinv"][GDN_HEADS:]
        vn = c["u"] - _bdot_nn(c["w"], s)
        if with_out:
            q8 = _stack_heads(qf_ref, qb_ref)
            pm = jnp.where(masks[0], _bdot_nt(q8, k8) * c["dm"], 0.0)
            o = _bdot_nn(q8 * c["e"], s) + _bdot_nn(pm, vn)
        for d, o_ref in enumerate((of_ref, ob_ref)):
            for h in range(GDN_HEADS):
                o_ref[:, h * LANE:(h + 1) * LANE] = o[GDN_HEADS * d + h] if with_out else jnp.zeros((CHUNK, LANE), f32)
        s_ref[...] = c["gl"] * s + _bdot_tn(c["kd"], vn)

        @pl.when(i == nc - 1)
        def _():
            sfin_ref[...] = s_ref[...]

    chunk_of = lambda i, d: i if d == 0 else nc - 1 - i
    seq = _gdn_specs(nc, 512, chunk_of)
    gate = _gdn_specs(nc, LANE, chunk_of)
    gate_t = [pl.BlockSpec((1, 16, CHUNK), functools.partial(lambda i, d: (chunk_of(i, d), 0, 0), d=d)) for d in (0, 1)]
    sall = [pl.BlockSpec((1, GDN_HEADS, LANE, LANE), functools.partial(lambda i, d: (chunk_of(i, d), 0, 0, 0), d=d)) for d in (0, 1)]
    tinv = [pl.BlockSpec((1, GDN_HEADS, CHUNK, CHUNK), functools.partial(lambda i, d: (chunk_of(i, d), 0, 0, 0), d=d)) for d in (0, 1)]
    st = pl.BlockSpec((N_SCAN, LANE, LANE), lambda i: (0, 0, 0))
    o_shape, s_shape, t_shape = SDS((t, 512), f32), SDS((nc, GDN_HEADS, LANE, LANE), f32), SDS((nc, GDN_HEADS, CHUNK, CHUNK), f32)
    o_f, o_b, sall_f, sall_b, tinv_f, tinv_b, s_fin = pl.pallas_call(
        body, out_shape=(o_shape, o_shape, s_shape, s_shape, t_shape, t_shape, SDS((N_SCAN, LANE, LANE), f32)), grid=(nc,),
        in_specs=seq + seq + seq + gate + gate_t + [st], out_specs=tuple(seq + sall + tinv + [st]),
        scratch_shapes=[pltpu.VMEM((N_SCAN, LANE, LANE), f32)], name=name,
        compiler_params=_cparams(("arbitrary",)),
    )(q, q, k, k, v, v, bg, bg, bgt, bgt, s0.reshape(N_SCAN, LANE, LANE))
    return o_f, o_b, (sall_f, sall_b, tinv_f, tinv_b), s_fin.reshape(2, GDN_HEADS, LANE, LANE)


def _gdn_chunk_bwd(q, k, v, d_o, beta, g_col, g_row, s, tinv, dsn, masks):
    lower, strict, _, at_last = masks
    c = _chunk_terms(k, v, beta, g_col, g_row, masks, tinv)
    e, f, gl, dm, kb, kk, tinv, u, w, kd = (c[n] for n in ("e", "f", "gl", "dm", "kb", "kk", "tinv", "u", "w", "kd"))
    vn = u - _bdot_nn(w, s)
    ds = gl * dsn
    dgl = jnp.sum(jnp.sum(s * dsn, axis=2, keepdims=True), axis=1, keepdims=True)
    dkd = _bdot_nt(vn, dsn)
    dvn = _bdot_nn(kd, dsn)
    dm_grad = jnp.zeros((N_SCAN, CHUNK, CHUNK), f32)
    de = jnp.zeros((N_SCAN, CHUNK, 1), f32)
    dq = None
    dk = jnp.zeros((N_SCAN, CHUNK, LANE), f32)
    if q is not None:
        qk = _bdot_nt(q, k)
        pm = jnp.where(lower, qk * dm, 0.0)
        dqd = _bdot_nt(d_o, s)
        ds = ds + _bdot_tn(q * e, d_o)
        dpm = jnp.where(lower, _bdot_nt(d_o, vn), 0.0)
        dvn = dvn + _bdot_tn(pm, d_o)
        dqk = dpm * dm
        dm_grad = dm_grad + dpm * qk
        dq = _bdot_nn(dqk, k) + dqd * e
        dk = _bdot_tn(dqk, q)
        de = de + jnp.sum(dqd * q, axis=2, keepdims=True)
    dw = -_bdot_nt(dvn, s)
    ds = ds - _bdot_tn(w, dvn)
    drv = _bdot_tn(tinv, dvn)
    drk = _bdot_tn(tinv, dw)
    da = -jnp.where(strict, _bdot_nt(drv, u) + _bdot_nt(drk, w), 0.0)
    dbeta = jnp.sum(drv * v, axis=2, keepdims=True)
    dv = drv * beta
    dkb = drk * e
    de = de + jnp.sum(drk * kb, axis=2, keepdims=True)
    dkk = da * dm
    dm_grad = dm_grad + da * kk
    dkb = dkb + _bdot_nn(dkk, k)
    dk = dk + _bdot_tn(dkk, kb) + dkd * f
    df = jnp.sum(dkd * k, axis=2, keepdims=True)
    dbeta = dbeta + jnp.sum(dkb * k, axis=2, keepdims=True)
    dk = dk + dkb * beta
    m = dm_grad * dm
    shape = (N_SCAN, CHUNK, CHUNK)
    eye = (lax.broadcasted_iota(jnp.int32, shape, 1) == lax.broadcasted_iota(jnp.int32, shape, 2)).astype(f32)

    def as_col(row):
        return jnp.sum(eye * row, axis=2, keepdims=True)

    rsum = jnp.sum(m, axis=2, keepdims=True)
    csum = as_col(jnp.sum(m, axis=1, keepdims=True))
    dgl_tot = jnp.sum(df * f, axis=1, keepdims=True) + dgl * gl
    dgc = de * e - df * f + rsum - csum + at_last * dgl_tot
    dg = as_col(jnp.sum(lower.astype(f32) * dgc, axis=1, keepdims=True))
    return dq, dk, dv, dbeta, dg, ds


def gdn_backward(q, k, v, bg, bgt, saved, d_o, ds_fin, with_out, name):
    t = k.shape[0]
    nc = t // CHUNK

    def body(qf_ref, qb_ref, kf_ref, kb_ref, vf_ref, vb_ref, bgf_ref, bgb_ref, bgtf_ref, bgtb_ref,
             sallf_ref, sallb_ref, tinvf_ref, tinvb_ref, dof_ref, dob_ref, dsf_ref,
             dqf_ref, dqb_ref, dkf_ref, dkb_ref, dvf_ref, dvb_ref, dbgf_ref, dbgb_ref, ds0_ref, ds_ref):
        i = pl.program_id(0)

        @pl.when(i == 0)
        def _():
            ds_ref[...] = dsf_ref[...]

        lane = lax.broadcasted_iota(jnp.int32, (1, LANE), 1)
        masks = _order_masks()
        beta, g_col, g_row = _stack_gates(bgf_ref[...], bgb_ref[...], bgtf_ref[0], bgtb_ref[0])
        s = jnp.concatenate([sallf_ref[0], sallb_ref[0]], 0)
        tinv = jnp.concatenate([tinvf_ref[0], tinvb_ref[0]], 0)
        dq, dk, dv, dbeta, dg, ds = _gdn_chunk_bwd(
            _stack_heads(qf_ref, qb_ref) if with_out else None, _stack_heads(kf_ref, kb_ref), _stack_heads(vf_ref, vb_ref),
            _stack_heads(dof_ref, dob_ref), beta, g_col, g_row, s, tinv, ds_ref[...], masks)
        ds_ref[...] = ds
        for d, (dq_ref, dk_ref, dv_ref, dbg_ref) in enumerate(((dqf_ref, dkf_ref, dvf_ref, dbgf_ref), (dqb_ref, dkb_ref, dvb_ref, dbgb_ref))):
            dbg = jnp.zeros((CHUNK, LANE), f32)
            for h in range(GDN_HEADS):
                b = GDN_HEADS * d + h
                cs = slice(h * LANE, (h + 1) * LANE)
                dq_ref[:, cs] = dq[b] if with_out else jnp.zeros((CHUNK, LANE), f32)
                dk_ref[:, cs] = dk[b]
                dv_ref[:, cs] = dv[b]
                dbg = dbg + dbeta[b] * (lane == b).astype(f32) + dg[b] * (lane == 8 + b).astype(f32)
            dbg_ref[...] = dbg

        @pl.when(i == nc - 1)
        def _():
            ds0_ref[...] = ds_ref[...]

    chunk_of = lambda i, d: nc - 1 - i if d == 0 else i
    seq = _gdn_specs(nc, 512, chunk_of)
    gate = _gdn_specs(nc, LANE, chunk_of)
    gate_t = [pl.BlockSpec((1, 16, CHUNK), functools.partial(lambda i, d: (chunk_of(i, d), 0, 0), d=d)) for d in (0, 1)]
    sall = [pl.BlockSpec((1, GDN_HEADS, LANE, LANE), functools.partial(lambda i, d: (chunk_of(i, d), 0, 0, 0), d=d)) for d in (0, 1)]
    tinv = [pl.BlockSpec((1, GDN_HEADS, CHUNK, CHUNK), functools.partial(lambda i, d: (chunk_of(i, d), 0, 0, 0), d=d)) for d in (0, 1)]
    st = pl.BlockSpec((N_SCAN, LANE, LANE), lambda i: (0, 0, 0))
    o_shape, g_shape = SDS((t, 512), f32), SDS((t, LANE), f32)
    res = pl.pallas_call(
        body, out_shape=(o_shape,) * 6 + (g_shape, g_shape, SDS((N_SCAN, LANE, LANE), f32)), grid=(nc,),
        in_specs=seq + seq + seq + gate + gate_t + sall + tinv + seq + [st], out_specs=tuple(seq + seq + seq + gate + [st]),
        scratch_shapes=[pltpu.VMEM((N_SCAN, LANE, LANE), f32)], name=name,
        compiler_params=_cparams(("arbitrary",)),
    )(q, q, k, k, v, v, bg, bg, bgt, bgt, *saved, d_o, d_o, ds_fin.reshape(N_SCAN, LANE, LANE))
    return tuple(res[:8]) + (res[8].reshape(2, GDN_HEADS, LANE, LANE),)


def _my_position():
    x, y, c = lax.axis_index("x"), lax.axis_index("y"), lax.axis_index("c")
    return x, y, c, 4 * x + 2 * y + c


def exchange(arrays, scatter, name):
    n = len(arrays)
    shapes = [a.shape[1:] if scatter else a.shape for a in arrays]

    def body(*refs):
        ins, outs, token = refs[:n], refs[n:2 * n], refs[2 * n]
        send_sems, recv_sems, local_sems = refs[2 * n + 1:]
        x, y, c, me = _my_position()
        token[...] = jnp.zeros_like(token)
        started = []
        for a in range(n):
            mine = pltpu.make_async_copy(ins[a].at[me] if scatter else ins[a], outs[a].at[me], local_sems.at[a])
            mine.start()
            started.append(mine)
        waits = []
        for r in range(1, N_DEV):
            px = 1 - x if r & 4 else x
            py = 1 - y if r & 2 else y
            pc = 1 - c if r & 1 else c
            pid = 4 * px + 2 * py + pc
            for a in range(n):
                cp = pltpu.make_async_remote_copy(
                    src_ref=ins[a].at[pid] if scatter else ins[a], dst_ref=outs[a].at[me],
                    send_sem=send_sems.at[a, r - 1], recv_sem=recv_sems.at[a, r - 1],
                    device_id=(px, py, pc), device_id_type=pl.DeviceIdType.MESH)
                cp.start()
                arrive = pltpu.make_async_remote_copy(
                    src_ref=ins[a].at[pid] if scatter else ins[a], dst_ref=outs[a].at[pid],
                    send_sem=send_sems.at[a, r - 1], recv_sem=recv_sems.at[a, r - 1],
                    device_id=(px, py, pc), device_id_type=pl.DeviceIdType.MESH)
                waits.append((cp, arrive))
        for cp, arrive in waits:
            cp.wait_send()
            arrive.wait_recv()
        for mine in started:
            mine.wait()

    any_spec = pl.BlockSpec(memory_space=pl.ANY)
    return pl.pallas_call(
        body, out_shape=tuple(SDS((N_DEV,) + tuple(s), a.dtype) for s, a in zip(shapes, arrays)) + (SDS((8, LANE), f32),),
        in_specs=[any_spec] * n, out_specs=tuple([any_spec] * n) + (pl.BlockSpec(memory_space=pltpu.VMEM),),
        scratch_shapes=[pltpu.SemaphoreType.DMA((n, N_DEV - 1)), pltpu.SemaphoreType.DMA((n, N_DEV - 1)),
                        pltpu.SemaphoreType.DMA((n,))],
        name=name,
    )(*arrays)


_HBM_SPEC = pl.BlockSpec(memory_space=pltpu.HBM)
_SEM_SPEC = pl.BlockSpec(memory_space=pltpu.SEMAPHORE)
_DATAFLOW = pltpu.SideEffectType.DATAFLOW_SIDE_EFFECTING


def _peers(x, y, c):
    out = []
    for r in range(1, N_DEV):
        px = 1 - x if r & 4 else x
        py = 1 - y if r & 2 else y
        pc = 1 - c if r & 1 else c
        out.append((r, (px, py, pc), 4 * px + 2 * py + pc))
    return out


def _exchange_copies(ins, lands, send_sems, recv_sems, scatter, arrivals):
    x, y, c, me = _my_position()
    pairs = []
    for r, peer, pid in _peers(x, y, c):
        for a in range(len(ins)):
            k = a * (N_DEV - 1) + r - 1
            kw = dict(send_sem=send_sems.at[k], recv_sem=recv_sems.at[k], device_id=peer, device_id_type=pl.DeviceIdType.MESH)
            src = ins[a].at[pid] if scatter else ins[a]
            send = pltpu.make_async_remote_copy(src_ref=src, dst_ref=lands[a].at[me], **kw)
            arrive = pltpu.make_async_remote_copy(src_ref=src, dst_ref=lands[a].at[pid], **kw) if arrivals else None
            pairs.append((send, arrive))
    return pairs


def exchange_start(arrays, scatter, name):
    n = len(arrays)
    shapes = [a.shape[1:] if scatter else a.shape for a in arrays]

    def body(*refs):
        ins, lands = refs[:n], refs[n:2 * n]
        send_sems, recv_sems = refs[2 * n], refs[2 * n + 1]
        token = refs[-1]
        for send, _ in _exchange_copies(ins, lands, send_sems, recv_sems, scatter, False):
            send.start()
        token[...] = jnp.zeros_like(token)

    sem = pltpu.SemaphoreType.DMA((n * (N_DEV - 1),))
    land_shapes = [(N_DEV,) + tuple(s) for s in shapes]
    res = pl.pallas_call(
        body, name=name,
        out_shape=(sem, sem, *[pltpu.HBM(a.shape, a.dtype) for a in arrays],
                   *[pltpu.HBM(s, a.dtype) for s, a in zip(land_shapes, arrays)], SDS((8, LANE), f32)),
        in_specs=[_HBM_SPEC] * (2 * n),
        out_specs=(_SEM_SPEC, _SEM_SPEC, *[_HBM_SPEC] * (2 * n), pl.BlockSpec(memory_space=pltpu.VMEM)),
        input_output_aliases={i: 2 + i for i in range(2 * n)},
        compiler_params=pltpu.CompilerParams(has_side_effects=_DATAFLOW),
    )(*[pltpu.with_memory_space_constraint(a, pltpu.HBM) for a in arrays],
      *[pltpu.with_memory_space_constraint(lax.empty(s, a.dtype), pltpu.HBM) for s, a in zip(land_shapes, arrays)])
    return (res[0], res[1], list(res[2:2 + n]), list(res[2 + n:2 + 2 * n]), scatter), res[-1]


def exchange_wait(handle, after, name):
    send_sems, recv_sems, ins, lands, scatter = handle
    n = len(ins)

    def body(*refs):
        in_refs, land_refs = refs[:n], refs[n:2 * n]
        for send, arrive in _exchange_copies(in_refs, land_refs, refs[2 * n], refs[2 * n + 1], scatter, True):
            send.wait_send()
            arrive.wait_recv()
        refs[-1][...] = jnp.zeros_like(refs[-1])

    res = pl.pallas_call(
        body, name=name,
        out_shape=tuple(pltpu.HBM(a.shape, a.dtype) for a in ins + lands) + (SDS((8, LANE), f32),),
        in_specs=[_HBM_SPEC] * (2 * n) + [_SEM_SPEC, _SEM_SPEC, pl.BlockSpec(memory_space=pl.ANY)],
        out_specs=tuple([_HBM_SPEC] * (2 * n)) + (pl.BlockSpec(memory_space=pltpu.VMEM),),
        input_output_aliases={i: i for i in range(2 * n)},
        compiler_params=pltpu.CompilerParams(has_side_effects=_DATAFLOW),
    )(*ins, *lands, send_sems, recv_sems, after)
    return list(res[:n]), list(res[n:2 * n]), res[-1]


def place_own(lands, arrays, scatter, me):
    own = [lax.dynamic_index_in_dim(a, me, 0, keepdims=False) if scatter else a for a in arrays]
    return [lax.dynamic_update_index_in_dim(l, o, me, 0) for l, o in zip(lands, own)]


def ada_forward(a_raw, ada_w, ada_b_loc, name):
    def body(a_ref, w_ref, b_ref, o_ref):
        a = _silu(a_ref[...])
        for l in range(DEPTH):
            o_ref[l] = _dotf(a, w_ref[l]) + b_ref[l]

    return pl.pallas_call(body, out_shape=SDS((DEPTH, 16, ada_w.shape[2]), f32), name=name,
                          compiler_params=_cparams())(a_raw, ada_w, ada_b_loc)


def ada_backward(a_raw, ada_w, dm, name):
    def body(a_ref, w_ref, dm_ref, gw_ref, dcc_ref):
        a = _silu(a_ref[...])
        for l in range(DEPTH):
            gw_ref[l] = _dotf(a, dm_ref[l], (((0,), (0,)), ((), ())))
        dcc_ref[...] = _dotf(dm_ref[0, 8:16, :], w_ref[0], (((1,), (1,)), ((), ())))

    return pl.pallas_call(body, out_shape=(SDS(ada_w.shape, f32), SDS((8, ada_w.shape[1]), f32)), name=name,
                          compiler_params=_cparams())(a_raw, ada_w, dm)


def sum_parts(parts, name):
    _, r, c = parts.shape

    def body(p_ref, o_ref):
        acc = p_ref[0]
        for i in range(1, N_DEV):
            acc = acc + p_ref[i]
        o_ref[...] = acc

    return pl.pallas_call(body, out_shape=SDS((r, c), f32), name=name, compiler_params=_cparams())(parts)


def cctx_grad(parts, c_ctx, name):
    def body(p_ref, c_ref, o_ref):
        acc = p_ref[0, 0:1, :]
        for i in range(1, N_DEV):
            acc = acc + p_ref[i, 0:1, :]
        o_ref[...] = acc * _dsilu(c_ref[...])

    return pl.pallas_call(body, out_shape=SDS((1, c_ctx.shape[1]), f32), name=name, compiler_params=_cparams())(parts, c_ctx)


def _adamw_math(g, w, m, v):
    m = ADAM_B1 * m + (1.0 - ADAM_B1) * g
    v = ADAM_B2 * v + (1.0 - ADAM_B2) * (g * g)
    m_hat = m / (1.0 - ADAM_B1 ** ADAM_STEP)
    v_hat = v / (1.0 - ADAM_B2 ** ADAM_STEP)
    delta = -ADAM_LR * (m_hat / (jnp.sqrt(v_hat) + ADAM_EPS) + ADAM_WD * w)
    return delta, m, v


def adamw(parts, w, m, v, name):
    n, r, c = parts.shape
    tr = _pick(r, (256, 128, 64, 32, 16, 8))

    def body(p_ref, w_ref, m_ref, v_ref, g_ref, d_ref, nm_ref, nv_ref):
        g = p_ref[0].astype(f32)
        for i in range(1, n):
            g = g + p_ref[i].astype(f32)
        g_ref[...] = g
        d_ref[...], nm_ref[...], nv_ref[...] = _adamw_math(g, w_ref[...], m_ref[...], v_ref[...])

    blk = pl.BlockSpec((tr, c), lambda i: (i, 0))
    out = SDS((r, c), f32)
    return pl.pallas_call(
        body, out_shape=(out, out, out, out), grid=(r // tr,),
        in_specs=[pl.BlockSpec((n, tr, c), lambda i: (0, i, 0)), blk, blk, blk], out_specs=(blk, blk, blk, blk),
        name=name, compiler_params=_cparams(("parallel",)),
    )(parts, w, m, v)


def adamw_small(items, name):
    n = len(items)

    def body(*refs):
        ins, outs = refs[:4 * n], refs[4 * n:]
        for i in range(n):
            g, w, m, v = (ins[4 * i + j][...] for j in range(4))
            outs[3 * i][...], outs[3 * i + 1][...], outs[3 * i + 2][...] = _adamw_math(g, w, m, v)

    flat = [a for it in items for a in it]
    out_shape = tuple(SDS(it[1].shape, f32) for it in items for _ in range(3))
    res = pl.pallas_call(body, out_shape=out_shape, name=name, compiler_params=_cparams())(*flat)
    return [tuple(res[3 * i:3 * i + 3]) for i in range(n)]


def _unshard(g, axis):
    loc = g.shape[1:]
    return jnp.moveaxis(g, 0, axis).reshape(loc[:axis] + (N_DEV * loc[axis],) + loc[axis + 1:])


def _shard_major(full, axis):
    s = full.shape
    return jnp.moveaxis(full.reshape(s[:axis] + (N_DEV, s[axis] // N_DEV) + s[axis + 1:]), axis, 0)


def _my_block(full, axis, me):
    n = full.shape[axis] // N_DEV
    return lax.dynamic_slice_in_dim(full, me * n, n, axis)


def _pack(arrays):
    flat = [a.reshape(-1) for a in arrays]
    sizes = [f.shape[0] for f in flat]
    total = sum(sizes)
    padded = -(-total // (8 * LANE)) * (8 * LANE)
    flat.append(jnp.zeros((padded - total,), f32))
    offs = [sum(sizes[:i]) for i in range(len(sizes))]
    return jnp.concatenate(flat).reshape(padded // LANE, LANE), offs


def _pad_rows(w, n):
    return jnp.concatenate([w, jnp.zeros((n - w.shape[0],) + w.shape[1:], w.dtype)], 0)


def _gate_rows(bg):
    return bg[:, :16].reshape(bg.shape[0] // CHUNK, CHUNK, 16).transpose(0, 2, 1)


def _rows(vec, n):
    m = vec.reshape(n, 1, -1)
    return [m[i] for i in range(n)]


def kernel(x, c, ctx, c_ctx, ada_w, ada_b, ln_g, ln_b, even_w_in, even_w_out, gdn_conv_w, gdn_a_log, gdn_dt_bias, gdn_norm_w, pool_w, pool_scale, odd_w_in, odd_w_out, sconv_w, conf_conv_w, conf_ln_g, conf_ln_b, ffn_w_up, ffn_conv_w, ffn_w_down, loss_target, m_c_ctx, m_ada_w, m_ada_b, m_ln_g, m_ln_b, m_even_w_in, m_even_w_out, m_gdn_conv_w, m_gdn_a_log, m_gdn_dt_bias, m_gdn_norm_w, m_pool_w, m_pool_scale, m_odd_w_in, m_odd_w_out, m_sconv_w, m_conf_conv_w, m_conf_ln_g, m_conf_ln_b, m_ffn_w_up, m_ffn_conv_w, m_ffn_w_down, v_c_ctx, v_ada_w, v_ada_b, v_ln_g, v_ln_b, v_even_w_in, v_even_w_out, v_gdn_conv_w, v_gdn_a_log, v_gdn_dt_bias, v_gdn_norm_w, v_pool_w, v_pool_scale, v_odd_w_in, v_odd_w_out, v_sconv_w, v_conf_conv_w, v_conf_ln_g, v_conf_ln_b, v_ffn_w_up, v_ffn_conv_w, v_ffn_w_down):
    weights = dict(c_ctx=c_ctx, ada_w=ada_w, ada_b=ada_b, ln_g=ln_g, ln_b=ln_b, even_w_in=even_w_in, even_w_out=even_w_out, gdn_conv_w=gdn_conv_w, gdn_a_log=gdn_a_log, gdn_dt_bias=gdn_dt_bias, gdn_norm_w=gdn_norm_w, pool_w=pool_w, pool_scale=pool_scale, odd_w_in=odd_w_in, odd_w_out=odd_w_out, sconv_w=sconv_w, conf_conv_w=conf_conv_w, conf_ln_g=conf_ln_g, conf_ln_b=conf_ln_b, ffn_w_up=ffn_w_up, ffn_conv_w=ffn_conv_w, ffn_w_down=ffn_w_down)
    mom1 = dict(c_ctx=m_c_ctx, ada_w=m_ada_w, ada_b=m_ada_b, ln_g=m_ln_g, ln_b=m_ln_b, even_w_in=m_even_w_in, even_w_out=m_even_w_out, gdn_conv_w=m_gdn_conv_w, gdn_a_log=m_gdn_a_log, gdn_dt_bias=m_gdn_dt_bias, gdn_norm_w=m_gdn_norm_w, pool_w=m_pool_w, pool_scale=m_pool_scale, odd_w_in=m_odd_w_in, odd_w_out=m_odd_w_out, sconv_w=m_sconv_w, conf_conv_w=m_conf_conv_w, conf_ln_g=m_conf_ln_g, conf_ln_b=m_conf_ln_b, ffn_w_up=m_ffn_w_up, ffn_conv_w=m_ffn_conv_w, ffn_w_down=m_ffn_w_down)
    mom2 = dict(c_ctx=v_c_ctx, ada_w=v_ada_w, ada_b=v_ada_b, ln_g=v_ln_g, ln_b=v_ln_b, even_w_in=v_even_w_in, even_w_out=v_even_w_out, gdn_conv_w=v_gdn_conv_w, gdn_a_log=v_gdn_a_log, gdn_dt_bias=v_gdn_dt_bias, gdn_norm_w=v_gdn_norm_w, pool_w=v_pool_w, pool_scale=v_pool_scale, odd_w_in=v_odd_w_in, odd_w_out=v_odd_w_out, sconv_w=v_sconv_w, conf_conv_w=v_conf_conv_w, conf_ln_g=v_conf_ln_g, conf_ln_b=v_conf_ln_b, ffn_w_up=v_ffn_w_up, ffn_conv_w=v_ffn_conv_w, ffn_w_down=v_ffn_w_down)
    order = list(weights)
    me = 4 * lax.axis_index("x") + 2 * lax.axis_index("y") + lax.axis_index("c")
    x, ctx, target = x[0], ctx[0], loss_target[0]
    t, d = x.shape
    tc = ctx.shape[0]

    small_in = [ln_g, ln_b, gdn_conv_w, sconv_w, conf_conv_w, ffn_conv_w, c]
    small_axes = [2, 2, 1, 1, 1, 3, 0]
    small_pack, small_offs = _pack(small_in)
    gath = exchange([even_w_in.astype(bf16), small_pack], False, "gather_first")
    e_in = even_w_in.shape[1] * N_DEV
    e_pad = -(-e_in // LANE) * LANE
    win_e = jnp.pad(_unshard(gath[0], 1), ((0, 0), (0, e_pad - e_in)))
    sm = gath[1].reshape(N_DEV, -1)
    lng_f, lnb_f, gconv_f, sconv_f, cconv_f, fconv_f, c_all = [
        _unshard(sm[:, o:o + a.size].reshape((N_DEV,) + a.shape), ax) for a, o, ax in zip(small_in, small_offs, small_axes)]
    gw8 = _pad_rows(gconv_f, 8)
    sw8 = _pad_rows(sconv_f, 8)
    cw32 = _pad_rows(cconv_f, 32)
    fw16 = [_pad_rows(fconv_f[l].reshape(9, D_FF), 16) for l in range(DEPTH)]

    a_raw = jnp.concatenate([c_all, c_ctx[None], jnp.zeros((7, d), f32)], 0)
    ncol = ada_w.shape[2]
    ada_b_loc = lax.dynamic_slice_in_dim(ada_b, me * ncol, ncol, 1)[:, None, :]
    modpart = ada_forward(a_raw, ada_w, ada_b_loc, "ada_forward")
    mod_send = jnp.stack([jnp.transpose(modpart[:, :N_DEV], (1, 0, 2)),
                          jnp.broadcast_to(modpart[:, N_DEV][None], (N_DEV, DEPTH, ncol))], axis=2)
    mod_recv, token = exchange([mod_send], True, "scatter_mod")
    wire_l0 = [even_w_out.astype(bf16) + token[0, 0].astype(bf16), ffn_w_up[0].astype(bf16), ffn_w_down[0].astype(bf16)]
    gather_l0, token = exchange_start(wire_l0, False, "gather_l0_start")
    wire_l1 = [odd_w_in.astype(bf16) + token[0, 0].astype(bf16), odd_w_out.astype(bf16), ffn_w_up[1].astype(bf16),
               ffn_w_down[1].astype(bf16)]
    gather_l1, token = exchange_start(wire_l1, False, "gather_l1_start")
    mod_recv = mod_recv + token[0, 0]
    mod = jnp.transpose(mod_recv[:, :, 0, :], (1, 0, 2)).reshape(DEPTH, 6 * d)
    modc = mod_recv[:, 0, 1, :].reshape(6 * d)
    sh_c, sc_c = modc[None, :d], modc[None, d:2 * d]
    mods = [_rows(mod[l], 6) for l in range(DEPTH)]
    lng = [[lng_f[l, j][None] for j in range(2)] for l in range(DEPTH)]
    lnb = [[lnb_f[l, j][None] for j in range(2)] for l in range(DEPTH)]

    neg_a = jnp.zeros((1, LANE), f32).at[0, 8:16].set(-jnp.exp(gdn_a_log).reshape(8))
    dt_row = jnp.zeros((1, LANE), f32).at[0, 8:16].set(gdn_dt_bias.reshape(8))
    nw_row, ps_row = gdn_norm_w[None], pool_scale[None]
    cg_row, cb_row = conf_ln_g[None], conf_ln_b[None]
    q_scale = GDN_DK ** -0.5

    sh_m, sc_m, gt_m, sh_f, sc_f, gt_f = mods[0]
    u0 = modulate(x, sc_m, sh_m, "mod_l0_mix")
    cu = modulate(ctx, sc_c, sh_c, "mod_ctx")
    p0 = matmul(u0, win_e, "nn", f32, "even_in")
    pc = matmul(cu, win_e, "nn", f32, "even_in_ctx")
    qn = gdn_conv(p0, gw8, 0, 4, q_scale, "gdn_conv_q")
    kn = gdn_conv(p0, gw8, 4, 4, 1.0, "gdn_conv_k")
    vv = gdn_conv(p0, gw8, 8, 4, None, "gdn_conv_v")
    kc = gdn_conv(pc, gw8, 4, 4, 1.0, "gdn_conv_k_ctx")
    vc = gdn_conv(pc, gw8, 8, 4, None, "gdn_conv_v_ctx")
    bg = gdn_gates(p0, neg_a, dt_row, "gdn_gates")
    bgc = gdn_gates(pc, neg_a, dt_row, "gdn_gates_ctx")
    bgt, bgtc = _gate_rows(bg), _gate_rows(bgc)
    zero_state = jnp.zeros((2, GDN_HEADS, LANE, LANE), f32)
    _, _, saved_c, sfin_c = gdn_forward(kc, kc, vc, bgc, bgtc, zero_state, False, "gdn_fwd_ctx")
    o_f, o_b, saved, _ = gdn_forward(qn, kn, vv, bg, bgt, sfin_c, True, "gdn_fwd")
    mix0 = jnp.concatenate([gated_rmsnorm(o_f, o_b, p0, nw_row, "gated_rmsnorm"),
                            pool_mix(p0, pool_w, ps_row, "pool_mix")], 1)
    sent, landed, _ = exchange_wait(gather_l0, mix0, "gather_l0_wait")
    full = place_own(landed, sent, False, me)
    wout_e, wup, wdown = _unshard(full[0], 0), [_unshard(full[1], 1)], [_unshard(full[2], 0)]
    y0 = matmul(mix0, wout_e, "nn", f32, "even_out")
    x1, u1 = res_layernorm(x, y0, gt_m, lng[0][0], lnb[0][0], "resln_l0_mix", sc_f, sh_f)
    h0 = matmul(u1, wup[0], "nn", f32, "ffn_up_l0")
    f0 = ffn_conv(h0, fw16[0], "ffn_conv_l0")
    y0f = matmul(f0, wdown[0], "nn", f32, "ffn_down_l0")
    sh_m1, sc_m1, gt_m1, sh_f1, sc_f1, gt_f1 = mods[1]
    x2, u2 = res_layernorm(x1, y0f, gt_f, lng[0][1], lnb[0][1], "resln_l0_ffn", sc_m1, sh_m1)

    sent, landed, _ = exchange_wait(gather_l1, x2, "gather_l1_wait")
    full = place_own(landed, sent, False, me)
    win_o, wout_o = _unshard(full[0], 1), _unshard(full[1], 0)
    wup.append(_unshard(full[2], 1))
    wdown.append(_unshard(full[3], 0))
    p1 = matmul(u2, win_o, "nn", f32, "odd_in")
    zc = conf_conv(p1, cw32, "conf_conv")
    mix1 = jnp.concatenate([short_conv(p1, sw8, "short_conv"), ln_silu(zc, cg_row, cb_row, "conf_ln_silu")], 1)
    y1 = matmul(mix1, wout_o, "nn", f32, "odd_out")
    x3, u3 = res_layernorm(x2, y1, gt_m1, lng[1][0], lnb[1][0], "resln_l1_mix", sc_f1, sh_f1)
    h1 = matmul(u3, wup[1], "nn", f32, "ffn_up_l1")
    f1 = ffn_conv(h1, fw16[1], "ffn_conv_l1")
    y1f = matmul(f1, wdown[1], "nn", f32, "ffn_down_l1")
    loss_row, dxr, dy, dgt_f1, dlg, dlb = res_layernorm_loss(x3, y1f, gt_f1, lng[1][1], lnb[1][1], target, "resln_l1_ffn_loss")
    loss = lax.psum(loss_row[0, 0], ("x", "y", "c"))

    def ffn_backward(dy, u, h, f, l):
        df = matmul(dy, wdown[l], "nt", f32, f"ffn_down_dgrad_l{l}")
        g_down = matmul(f, dy, "tn", bf16, f"ffn_down_wgrad_l{l}")
        dh, dcw = ffn_conv_bwd(h, fw16[l], df, f"ffn_conv_bwd_l{l}")
        du = matmul(dh, wup[l], "nt", f32, f"ffn_up_dgrad_l{l}")
        g_up = matmul(u, dh, "tn", bf16, f"ffn_up_wgrad_l{l}")
        return du, dcw, g_up, g_down

    dln_f1 = (dlg, dlb)
    du, dfcw1, g_up1, g_down1 = ffn_backward(dy, u3, h1, f1, 1)

    scatter_a, token = exchange_start([_shard_major(g_up1, 1), _shard_major(g_down1, 0)], True, "scatter_l1_ffn_start")
    gt_m1 = gt_m1 + token[0:1, 0:1]

    dxr, dy, dsc, dsh, dgt, dlg, dlb = modulate_res_layernorm_bwd(
        du, sc_f1, dxr, x2, y1, gt_m1, lng[1][0], lnb[1][0], "mod_resln_bwd_l1_mix")
    dmod_f1 = (dsh, dsc, dgt_f1)
    dln_m1 = (dlg, dlb)
    dmix = matmul(dy, wout_o, "nt", f32, "odd_out_dgrad")
    g_wout_o = matmul(mix1, dy, "tn", bf16, "odd_out_wgrad")
    dgb, dgc, dhh, d_sconv = short_conv_bwd(p1, sw8, dmix, "short_conv_bwd")
    dzc, d_cg, d_cb = ln_silu_bwd(zc, cg_row, cb_row, dmix, "conf_ln_silu_bwd")
    dga, dgbb, d_cconv = conf_conv_bwd(p1, cw32, dzc, "conf_conv_bwd")
    dp1 = jnp.concatenate([dgb, dgc, dhh, dga, dgbb], 1)
    du = matmul(dp1, win_o, "nt", f32, "odd_in_dgrad")
    g_win_o = matmul(u2, dp1, "tn", bf16, "odd_in_wgrad")
    dgt_m1 = dgt
    dxr, dy, dsc, dsh, dgt_f0, dlg, dlb = modulate_res_layernorm_bwd(
        du, sc_m1, dxr, x1, y0f, gt_f, lng[0][1], lnb[0][1], "mod_resln_bwd_l0_ffn")
    dmod_m1 = (dsh, dsc, dgt_m1)
    dln_f0 = (dlg, dlb)
    du, dfcw0, g_up0, g_down0 = ffn_backward(dy, u1, h0, f0, 0)

    scatter_b, token = exchange_start(
        [_shard_major(g_win_o, 1), _shard_major(g_wout_o, 0), _shard_major(g_up0, 1), _shard_major(g_down0, 0)],
        True, "scatter_mid_start")
    gt_m = gt_m + token[0:1, 0:1]

    dxr, dy, dsc, dsh, dgt, dlg, dlb = modulate_res_layernorm_bwd(
        du, sc_f, dxr, x, y0, gt_m, lng[0][0], lnb[0][0], "mod_resln_bwd_l0_mix")
    dmod_f0 = (dsh, dsc, dgt_f0)
    dln_m0 = (dlg, dlb)
    dmix = matmul(dy, wout_e, "nt", f32, "even_out_dgrad")
    g_wout_e = matmul(mix0, dy, "tn", bf16, "even_out_wgrad")
    d_o, dgate, d_nw = gated_rmsnorm_bwd(o_f, o_b, p0, nw_row, dmix, "gated_rmsnorm_bwd")
    dpool, d_pw, d_ps = pool_mix_bwd(p0, pool_w, ps_row, dmix, "pool_mix_bwd")
    small_early = [d_nw, d_pw, d_ps, d_sconv[:3], d_cconv[:31], d_cg, d_cb, jnp.stack([dfcw0[:9], dfcw1[:9]])]
    epack, eoffs = _pack(small_early)
    gather_early, token = exchange_start([epack], False, "gather_small_early_start")
    dq_f, dq_b, dk_f, dk_b, dv_f, dv_b, dbg_f, dbg_b, ds0 = gdn_backward(
        qn, kn, vv, bg, bgt, saved, d_o, zero_state + token[0, 0], True, "gdn_bwd")
    _, _, dkc_f, dkc_b, dvc_f, dvc_b, dbgc_f, dbgc_b, _ = gdn_backward(
        kc, kc, vc, bgc, bgtc, saved_c, jnp.zeros((tc, 512), f32), ds0, False, "gdn_bwd_ctx")
    dqp, dwq = gdn_conv_bwd(p0, gw8, dq_f, dq_b, 0, 4, q_scale, "gdn_conv_q_bwd")
    dkp, dwk = gdn_conv_bwd(p0, gw8, dk_f, dk_b, 4, 4, 1.0, "gdn_conv_k_bwd")
    dvp, dwv = gdn_conv_bwd(p0, gw8, dv_f, dv_b, 8, 4, None, "gdn_conv_v_bwd")
    dkcp, dwkc = gdn_conv_bwd(pc, gw8, dkc_f, dkc_b, 4, 4, 1.0, "gdn_conv_k_ctx_bwd")
    dvcp, dwvc = gdn_conv_bwd(pc, gw8, dvc_f, dvc_b, 8, 4, None, "gdn_conv_v_ctx_bwd")
    ds_l, da_l, ddt_l = gdn_gates_bwd(p0, neg_a, dt_row, dbg_f, dbg_b, "gdn_gates_bwd")
    ds_c, da_c, ddt_c = gdn_gates_bwd(pc, neg_a, dt_row, dbgc_f, dbgc_b, "gdn_gates_ctx_bwd")
    zc512 = jnp.zeros((tc, 512), bf16)
    dp0 = jnp.concatenate([dqp, dkp, dvp, dgate, dpool, ds_l], 1)
    dpc = jnp.concatenate([zc512, dkcp, dvcp, zc512, zc512, ds_c], 1)
    du0 = matmul(dp0, win_e, "nt", f32, "even_in_dgrad")
    duc = matmul(dpc, win_e, "nt", f32, "even_in_ctx_dgrad")
    g_win_e = matmul(u0, dp0, "tn", bf16, "even_in_wgrad", init=matmul(cu, dpc, "tn", f32, "even_in_ctx_wgrad"))[:, :e_in]
    scatter_c, token = exchange_start([_shard_major(g_win_e, 1), _shard_major(g_wout_e, 0)], True, "scatter_last_start")
    grad_x, dsc, dsh = modulate_bwd(du0, x, sc_m + token[0:1, 0:1], dxr, "mod_bwd_l0_mix")
    dmod_m0 = (dsh, dsc, dgt)
    _, dsc_c, dsh_c = modulate_bwd(duc, ctx, sc_c, jnp.zeros((tc, d), f32), "mod_bwd_ctx")

    grads, delta, new_m, new_v = {}, {}, {}, {}

    def update(n, parts, w, m, v):
        cols = w.shape[-1]
        out = adamw(parts.reshape(parts.shape[0], -1, cols), w.reshape(-1, cols), m.reshape(-1, cols), v.reshape(-1, cols), f"adamw_{n}")
        return [a.reshape(w.shape) for a in out]

    sent, landed, _ = exchange_wait(scatter_a, grad_x, "scatter_l1_ffn_wait")
    recv_a = place_own(landed, sent, True, me)
    sent, landed, _ = exchange_wait(scatter_b, grad_x, "scatter_mid_wait")
    recv_b = place_own(landed, sent, True, me)
    for n, parts in (("odd_w_in", recv_b[0]), ("odd_w_out", recv_b[1])):
        grads[n], delta[n], new_m[n], new_v[n] = update(n, parts, weights[n], mom1[n], mom2[n])
    for n, per_layer in (("ffn_w_up", (recv_b[2], recv_a[0])), ("ffn_w_down", (recv_b[3], recv_a[1]))):
        outs = [update(f"{n}_l{l}", per_layer[l], weights[n][l], mom1[n][l], mom2[n][l]) for l in range(DEPTH)]
        grads[n], delta[n], new_m[n], new_v[n] = (jnp.stack([outs[l][j] for l in range(DEPTH)]) for j in range(4))
    sent, landed, token = exchange_wait(scatter_c, new_v["ffn_w_down"], "scatter_last_wait")
    recv_c = place_own(landed, sent, True, me)
    for n, parts in (("even_w_in", recv_c[0]), ("even_w_out", recv_c[1])):
        grads[n], delta[n], new_m[n], new_v[n] = update(n, parts, weights[n], mom1[n], mom2[n])

    dmod0 = jnp.concatenate(dmod_m0 + dmod_f0, 1)
    dmod1 = jnp.concatenate(dmod_m1 + dmod_f1, 1)
    dmodc = jnp.concatenate([dsh_c, dsc_c], 1)
    d_gconv = jnp.concatenate([dwq, dwk + dwkc, dwv + dwvc], 1)[:5]
    small_late = [dmod0, dmod1, dmodc,
                  jnp.concatenate([dln_m0[0], dln_f0[0], dln_m1[0], dln_f1[0]], 0),
                  jnp.concatenate([dln_m0[1], dln_f0[1], dln_m1[1], dln_f1[1]], 0),
                  d_gconv, (da_l + da_c)[0, 8:16], (ddt_l + ddt_c)[0, 8:16]]
    gpack, goffs = _pack(small_late)
    gparts = exchange([gpack + token[0:1]], False, "gather_small_grads")[0]
    sent, landed, _ = exchange_wait(gather_early, gparts, "gather_small_early_wait")
    eparts = place_own(landed, sent, False, me)[0]
    gsum = sum_parts(gparts, "sum_small_grads").reshape(-1)
    esum = sum_parts(eparts, "sum_small_early").reshape(-1)
    gs = ([gsum[o:o + a.size].reshape(a.shape) for a, o in zip(small_late, goffs)]
          + [esum[o:o + a.size].reshape(a.shape) for a, o in zip(small_early, eoffs)])
    gflat = gparts.reshape(N_DEV, -1)
    dmodc_cols = _my_block(jnp.pad(gs[2], ((0, 0), (0, 4 * d))), 1, me)
    dm = jnp.stack([
        jnp.concatenate([_my_block(gflat[:, goffs[0]:goffs[0] + 6 * d], 1, me), dmodc_cols, jnp.zeros((7, ncol), f32)], 0),
        jnp.concatenate([_my_block(gflat[:, goffs[1]:goffs[1] + 6 * d], 1, me), jnp.zeros((8, ncol), f32)], 0)])
    g_ada_w, dcc = ada_backward(a_raw, ada_w, dm, "ada_backward")
    g_cctx = cctx_grad(exchange([dcc], False, "gather_cctx")[0], c_ctx[None], "cctx_grad")

    grads["c_ctx"] = g_cctx.reshape(c_ctx.shape)
    grads["ada_b"] = jnp.concatenate([gs[0] + jnp.pad(gs[2], ((0, 0), (0, 4 * d))), gs[1]], 0)
    grads["ln_g"] = _my_block(gs[3].reshape(DEPTH, 2, d), 2, me)
    grads["ln_b"] = _my_block(gs[4].reshape(DEPTH, 2, d), 2, me)
    grads["gdn_conv_w"] = _my_block(gs[5], 1, me)
    grads["gdn_a_log"] = gs[6].reshape(2, GDN_HEADS)
    grads["gdn_dt_bias"] = gs[7].reshape(2, GDN_HEADS)
    grads["gdn_norm_w"] = gs[8].reshape(LANE)
    grads["pool_w"] = gs[9]
    grads["pool_scale"] = gs[10].reshape(-1)
    grads["sconv_w"] = _my_block(gs[11], 1, me)
    grads["conf_conv_w"] = _my_block(gs[12], 1, me)
    grads["conf_ln_g"] = gs[13].reshape(-1)
    grads["conf_ln_b"] = gs[14].reshape(-1)
    grads["ffn_conv_w"] = _my_block(gs[15].reshape(DEPTH, 3, 3, D_FF), 3, me)

    def as2d(a):
        return a.reshape(-1, a.shape[-1]) if a.ndim > 1 else a.reshape(1, -1)

    small_names = [n for n in order if n in grads and n not in delta]
    res = adamw_small([(as2d(grads[n]), as2d(weights[n]), as2d(mom1[n]), as2d(mom2[n])) for n in small_names], "adamw_small")
    for n, (dl, nm, nv) in zip(small_names, res):
        delta[n], new_m[n], new_v[n] = (a.reshape(weights[n].shape) for a in (dl, nm, nv))
    grads["ada_w"], delta["ada_w"], new_m["ada_w"], new_v["ada_w"] = update("ada_w", g_ada_w[None], ada_w, m_ada_w, v_ada_w)

    return (loss, grad_x[None], *[grads[n] for n in order], *[delta[n] for n in order],
            *[new_m[n] for n in order], *[new_v[n] for n in order])
```

```python
import functools
import math

import jax
import jax.numpy as jnp
from jax import lax
from jax.experimental import pallas as pl
from jax.experimental.pallas import tpu as pltpu

f32 = jnp.float32
bf16 = jnp.bfloat16
SDS = jax.ShapeDtypeStruct

N_DEV = 8
D_MODEL = 1024
DEPTH = 2
GRID_W = 64
GDN_HEADS = 4
GDN_DK = 128
CHUNK = 64
POOL_WINDOWS = (2, 4, 8, 16)
D_FF = 2816
ALPHA = (2 * DEPTH) ** 0.25
LN_EPS = 1e-5
RMS_EPS = 1e-6
LANE = 128
PAD_ROWS = 72
CONV_ROWS = 256
VMEM_LIMIT = 56 * 2**20

ADAM_LR, ADAM_B1, ADAM_B2, ADAM_EPS, ADAM_WD, ADAM_STEP = 0.001, 0.9, 0.999, 1e-08, 0.01, 10

HI = lax.Precision.HIGHEST


def _cparams(sem=None):
    return pltpu.CompilerParams(dimension_semantics=sem, vmem_limit_bytes=VMEM_LIMIT)


def _silu(x):
    return x * jax.nn.sigmoid(x)


def _dsilu(x):
    s = jax.nn.sigmoid(x)
    return s * (1.0 + x * (1.0 - s))


def _dotb(a, b, dims=(((1,), (0,)), ((), ()))):
    return lax.dot_general(a.astype(bf16), b.astype(bf16), dims, preferred_element_type=f32)


def _dotb_nt(a, b):
    return _dotb(a, b, (((1,), (1,)), ((), ())))


def _dotb_tn(a, b):
    return _dotb(a, b, (((0,), (0,)), ((), ())))


def _dotf(a, b, dims=(((1,), (0,)), ((), ()))):
    return lax.dot_general(a, b, dims, preferred_element_type=f32, precision=HI)


def _pick(n, cands):
    for c in cands:
        if n % c == 0:
            return c
    return n


def matmul(a, b, mode, out_dtype, name, init=None):
    if mode == "nn":
        (M, K), N = a.shape, b.shape[1]
    elif mode == "nt":
        (M, K), N = a.shape, b.shape[0]
    else:
        (K, M), N = a.shape, b.shape[1]
    tm = _pick(M, (1024, 768, 512, 256, 128)) if mode != "tn" else _pick(M, (1024, 1408, 512, 256, 128))
    tn = _pick(N, (1024, 1408, 896, 768, 640, 512, 384, 256, 128))
    tk = _pick(K, (1024, 1408, 896, 768, 640, 512, 384, 256, 128)) if mode != "tn" else _pick(K, (1024, 512, 256))
    nk = K // tk
    dims = {"nn": (((1,), (0,)), ((), ())), "nt": (((1,), (1,)), ((), ())), "tn": (((0,), (0,)), ((), ()))}[mode]

    def body(a_ref, b_ref, *rest):
        o_ref, acc_ref = rest[-2:]
        k = pl.program_id(2)
        part = lax.dot_general(a_ref[...].astype(bf16), b_ref[...].astype(bf16), dims, preferred_element_type=f32)

        @pl.when(k == 0)
        def _():
            acc_ref[...] = part if init is None else part + rest[0][...]

        @pl.when(k > 0)
        def _():
            acc_ref[...] += part

        @pl.when(k == nk - 1)
        def _():
            o_ref[...] = acc_ref[...].astype(out_dtype)

    a_spec = {"nn": pl.BlockSpec((tm, tk), lambda i, j, k: (i, k)),
              "nt": pl.BlockSpec((tm, tk), lambda i, j, k: (i, k)),
              "tn": pl.BlockSpec((tk, tm), lambda i, j, k: (k, i))}[mode]
    b_spec = {"nn": pl.BlockSpec((tk, tn), lambda i, j, k: (k, j)),
              "nt": pl.BlockSpec((tn, tk), lambda i, j, k: (j, k)),
              "tn": pl.BlockSpec((tk, tn), lambda i, j, k: (k, j))}[mode]
    o_spec = pl.BlockSpec((tm, tn), lambda i, j, k: (i, j))
    return pl.pallas_call(
        body, out_shape=SDS((M, N), out_dtype), grid=(M // tm, N // tn, nk),
        in_specs=[a_spec, b_spec] + ([] if init is None else [o_spec]), out_specs=o_spec,
        scratch_shapes=[pltpu.VMEM((tm, tn), f32)], name=name,
        compiler_params=_cparams(("parallel", "parallel", "arbitrary")),
    )(*((a, b) if init is None else (a, b, init)))


def _row_tile(t):
    return _pick(t, (512, 256, 128, 64, 32, 16, 8))


def _row_spec(tt, d):
    return pl.BlockSpec((tt, d), lambda i: (i, 0))


def _vec_spec(d):
    return pl.BlockSpec((1, d), lambda i: (0, 0))


def _acc_rows(ref, val):
    @pl.when(pl.program_id(0) == 0)
    def _():
        ref[...] = val

    @pl.when(pl.program_id(0) > 0)
    def _():
        ref[...] += val


def modulate(x, scale, shift, name):
    t, d = x.shape
    tt = _row_tile(t)

    def body(x_ref, sc_ref, sh_ref, o_ref):
        o_ref[...] = (x_ref[...] * (1.0 + sc_ref[...]) + sh_ref[...]).astype(bf16)

    return pl.pallas_call(
        body, out_shape=SDS((t, d), bf16), grid=(t // tt,),
        in_specs=[_row_spec(tt, d), _vec_spec(d), _vec_spec(d)], out_specs=_row_spec(tt, d),
        name=name, compiler_params=_cparams(("parallel",)),
    )(x, scale, shift)


def modulate_bwd(du, x, scale, dres, name, du_row0=0):
    t, d = x.shape
    tt = _row_tile(t)
    blk0 = du_row0 // tt

    def body(du_ref, x_ref, sc_ref, dres_ref, dx_ref, dsc_ref, dsh_ref):
        du_v = du_ref[...]
        dx_ref[...] = du_v * (1.0 + sc_ref[...]) + dres_ref[...]
        _acc_rows(dsc_ref, jnp.sum(du_v * x_ref[...], axis=0, keepdims=True))
        _acc_rows(dsh_ref, jnp.sum(du_v, axis=0, keepdims=True))

    return pl.pallas_call(
        body, out_shape=(SDS((t, d), f32), SDS((1, d), f32), SDS((1, d), f32)), grid=(t // tt,),
        in_specs=[pl.BlockSpec((tt, d), lambda i: (i + blk0, 0)), _row_spec(tt, d), _vec_spec(d), _row_spec(tt, d)],
        out_specs=(_row_spec(tt, d), _vec_spec(d), _vec_spec(d)),
        name=name, compiler_params=_cparams(("arbitrary",)),
    )(du, x, scale, dres)


def _ln_stats(z):
    mu = jnp.mean(z, axis=-1, keepdims=True)
    zc = z - mu
    var = jnp.mean(zc * zc, axis=-1, keepdims=True)
    rstd = lax.rsqrt(var + LN_EPS)
    return zc * rstd, rstd


def _ln_bwd(dxhat, xhat, rstd):
    m1 = jnp.mean(dxhat, axis=-1, keepdims=True)
    m2 = jnp.mean(dxhat * xhat, axis=-1, keepdims=True)
    return rstd * (dxhat - m1 - xhat * m2)


def res_layernorm(x, y, gate, g, b, name, scale=None, shift=None):
    t, d = x.shape
    tt = _row_tile(t)
    with_mod = scale is not None

    def body(x_ref, y_ref, gt_ref, g_ref, b_ref, *rest):
        xhat, _ = _ln_stats(ALPHA * x_ref[...] + gt_ref[...] * y_ref[...])
        out = xhat * g_ref[...] + b_ref[...]
        if with_mod:
            sc_ref, sh_ref, o_ref, u_ref = rest
            u_ref[...] = (out * (1.0 + sc_ref[...]) + sh_ref[...]).astype(bf16)
        else:
            o_ref, = rest
        o_ref[...] = out

    rows, vec = _row_spec(tt, d), _vec_spec(d)
    return pl.pallas_call(
        body, out_shape=(SDS((t, d), f32), SDS((t, d), bf16)) if with_mod else SDS((t, d), f32), grid=(t // tt,),
        in_specs=[rows, rows, vec, vec, vec] + ([vec, vec] if with_mod else []),
        out_specs=(rows, rows) if with_mod else rows, name=name, compiler_params=_cparams(("parallel",)),
    )(*((x, y, gate, g, b) + ((scale, shift) if with_mod else ())))


def modulate_res_layernorm_bwd(du, scale, dres, x, y, gate, g, b, name):
    t, d = x.shape
    tt = _row_tile(t)

    def body(du_ref, sc_ref, dres_ref, x_ref, y_ref, gt_ref, g_ref, b_ref,
             dxr_ref, dy_ref, dsc_ref, dsh_ref, dgt_ref, dg_ref, db_ref):
        y_v, du_v = y_ref[...], du_ref[...]
        xhat, rstd = _ln_stats(ALPHA * x_ref[...] + gt_ref[...] * y_v)
        do_v = du_v * (1.0 + sc_ref[...]) + dres_ref[...]
        dz = _ln_bwd(do_v * g_ref[...], xhat, rstd)
        dxr_ref[...] = ALPHA * dz
        dy_ref[...] = (gt_ref[...] * dz).astype(bf16)
        _acc_rows(dsc_ref, jnp.sum(du_v * (xhat * g_ref[...] + b_ref[...]), axis=0, keepdims=True))
        _acc_rows(dsh_ref, jnp.sum(du_v, axis=0, keepdims=True))
        _acc_rows(dgt_ref, jnp.sum(dz * y_v, axis=0, keepdims=True))
        _acc_rows(dg_ref, jnp.sum(do_v * xhat, axis=0, keepdims=True))
        _acc_rows(db_ref, jnp.sum(do_v, axis=0, keepdims=True))

    rows, vec, vshape = _row_spec(tt, d), _vec_spec(d), SDS((1, d), f32)
    return pl.pallas_call(
        body, out_shape=(SDS((t, d), f32), SDS((t, d), bf16)) + (vshape,) * 5, grid=(t // tt,),
        in_specs=[rows, vec, rows, rows, rows, vec, vec, vec], out_specs=(rows, rows) + (vec,) * 5,
        name=name, compiler_params=_cparams(("arbitrary",)),
    )(du, scale, dres, x, y, gate, g, b)


def res_layernorm_loss(x, y, gate, g, b, target, name):
    t, d = x.shape
    tt = _row_tile(t)

    def body(x_ref, y_ref, gt_ref, g_ref, b_ref, t_ref, l_ref, dxr_ref, dy_ref, dgt_ref, dg_ref, db_ref):
        y_v = y_ref[...]
        xhat, rstd = _ln_stats(ALPHA * x_ref[...] + gt_ref[...] * y_v)
        e = xhat * g_ref[...] + b_ref[...] - t_ref[...]
        part = jnp.sum(jnp.sum(e * e, axis=1, keepdims=True), axis=0, keepdims=True) * (0.5 / d)
        _acc_rows(l_ref, jnp.broadcast_to(part, (1, LANE)))
        do_v = e * (1.0 / d)
        dz = _ln_bwd(do_v * g_ref[...], xhat, rstd)
        dxr_ref[...] = ALPHA * dz
        dy_ref[...] = (gt_ref[...] * dz).astype(bf16)
        _acc_rows(dgt_ref, jnp.sum(dz * y_v, axis=0, keepdims=True))
        _acc_rows(dg_ref, jnp.sum(do_v * xhat, axis=0, keepdims=True))
        _acc_rows(db_ref, jnp.sum(do_v, axis=0, keepdims=True))

    rows, vec, vshape = _row_spec(tt, d), _vec_spec(d), SDS((1, d), f32)
    return pl.pallas_call(
        body, out_shape=(SDS((1, LANE), f32), SDS((t, d), f32), SDS((t, d), bf16)) + (vshape,) * 3, grid=(t // tt,),
        in_specs=[rows, rows, vec, vec, vec, rows],
        out_specs=(pl.BlockSpec((1, LANE), lambda i: (0, 0)), rows, rows) + (vec,) * 3,
        name=name, compiler_params=_cparams(("arbitrary",)),
    )(x, y, gate, g, b, target)


def _fill_pad(pad_ref, val, t):
    zeros = jnp.zeros((PAD_ROWS, LANE), f32)
    pad_ref[0:PAD_ROWS, :] = zeros
    pad_ref[PAD_ROWS + t:2 * PAD_ROWS + t, :] = zeros
    pad_ref[PAD_ROWS:PAD_ROWS + t, :] = val


def _grid_pads_set(pads, r0, val):
    rows = val.shape[0]
    col = (lax.broadcasted_iota(jnp.int32, (rows, 1), 0) + r0) % GRID_W
    base = PAD_ROWS + r0
    pads[0][base + 1:base + 1 + rows, :] = val * (col <= GRID_W - 2).astype(f32)
    pads[1][base:base + rows, :] = val
    pads[2][base - 1:base - 1 + rows, :] = val * (col >= 1).astype(f32)


def _grid_pads_clear_edges(pads, t):
    zeros = jnp.zeros((PAD_ROWS + 8, LANE), f32)
    for p in pads:
        p[0:PAD_ROWS + 8, :] = zeros
        p[PAD_ROWS + t - 8:2 * PAD_ROWS + t, :] = zeros


def _tap_source(pads, dc):
    return pads if dc is None else pads[dc + 1]


def _taps_apply(pads, w_ref, taps, r0, rows):
    acc = jnp.zeros((rows, LANE), f32)
    for off, dc, wi in taps:
        xs = _tap_source(pads, dc)[PAD_ROWS + r0 + off:PAD_ROWS + r0 + off + rows, :]
        acc = acc + w_ref[wi:wi + 1, :] * xs
    return acc


def _taps_wgrad(pads, dy, taps, r0, rows, nw):
    out = jnp.zeros((nw, LANE), f32)
    rid = lax.broadcasted_iota(jnp.int32, (nw, 1), 0)
    for off, dc, wi in taps:
        xs = _tap_source(pads, dc)[PAD_ROWS + r0 + off:PAD_ROWS + r0 + off + rows, :]
        s = jnp.sum(dy * xs, axis=0, keepdims=True)
        out = out + jnp.where(rid == wi, s, 0.0)
    return out


def _transpose_taps(taps):
    return [(-off, None if dc is None else -dc, wi) for off, dc, wi in taps]


def _taps_1d(width):
    return [(j - width // 2, None, j) for j in range(width)]


def _taps_grid3():
    return [(GRID_W * dr, dc, 3 * (dr + 1) + (dc + 1)) for dr in (-1, 0, 1) for dc in (-1, 0, 1)]


def _row_chunks(t):
    r = min(CONV_ROWS, t)
    return [(i * r, r) for i in range(t // r)]


def _col_spec(t, off):
    return pl.BlockSpec((t, LANE), lambda c: (0, c + off))


def _w_spec(nw, off=0):
    return pl.BlockSpec((nw, LANE), lambda c: (0, c + off))


def gdn_conv(p, w, col0, nblk, norm_scale, name):
    t = p.shape[0]
    nw = w.shape[0]
    taps = _taps_1d(5)

    def body(p_ref, w_ref, o_ref, pad_ref):
        _fill_pad(pad_ref, p_ref[...], t)
        for r0, rows in _row_chunks(t):
            a = _silu(_taps_apply(pad_ref, w_ref, taps, r0, rows))
            if norm_scale is not None:
                a = a * (lax.rsqrt(jnp.sum(a * a, axis=-1, keepdims=True) + RMS_EPS) * norm_scale)
            o_ref[r0:r0 + rows, :] = a

    return pl.pallas_call(
        body, out_shape=SDS((t, nblk * LANE), f32), grid=(nblk,),
        in_specs=[_col_spec(t, col0), _w_spec(nw, col0)], out_specs=_col_spec(t, 0),
        scratch_shapes=[pltpu.VMEM((t + 2 * PAD_ROWS, LANE), f32)], name=name,
        compiler_params=_cparams(("parallel",)),
    )(p, w)


def gdn_conv_bwd(p, w, d_a, d_b, col0, nblk, norm_scale, name):
    t = p.shape[0]
    nw = w.shape[0]
    taps = _taps_1d(5)
    ttaps = _transpose_taps(taps)

    def body(p_ref, w_ref, da_ref, db_ref, dp_ref, dw_ref, pad_ref, gpad_ref):
        _fill_pad(pad_ref, p_ref[...], t)
        for r0, rows in _row_chunks(t):
            pre = _taps_apply(pad_ref, w_ref, taps, r0, rows)
            a = _silu(pre)
            dy = da_ref[r0:r0 + rows, :] + db_ref[r0:r0 + rows, :]
            if norm_scale is not None:
                r = lax.rsqrt(jnp.sum(a * a, axis=-1, keepdims=True) + RMS_EPS)
                da = norm_scale * (dy * r - a * (r * r * r) * jnp.sum(dy * a, axis=-1, keepdims=True))
            else:
                da = dy
            gpad_ref[PAD_ROWS + r0:PAD_ROWS + r0 + rows, :] = da * _dsilu(pre)
        zeros = jnp.zeros((PAD_ROWS, LANE), f32)
        gpad_ref[0:PAD_ROWS, :] = zeros
        gpad_ref[PAD_ROWS + t:2 * PAD_ROWS + t, :] = zeros
        dw = jnp.zeros((nw, LANE), f32)
        for r0, rows in _row_chunks(t):
            dp_ref[r0:r0 + rows, :] = _taps_apply(gpad_ref, w_ref, ttaps, r0, rows).astype(bf16)
            dw = dw + _taps_wgrad(pad_ref, gpad_ref[PAD_ROWS + r0:PAD_ROWS + r0 + rows, :], taps, r0, rows, nw)
        dw_ref[...] = dw

    return pl.pallas_call(
        body, out_shape=(SDS((t, nblk * LANE), bf16), SDS((nw, nblk * LANE), f32)), grid=(nblk,),
        in_specs=[_col_spec(t, col0), _w_spec(nw, col0), _col_spec(t, 0), _col_spec(t, 0)],
        out_specs=(_col_spec(t, 0), _w_spec(nw)),
        scratch_shapes=[pltpu.VMEM((t + 2 * PAD_ROWS, LANE), f32)] * 2, name=name,
        compiler_params=_cparams(("parallel",)),
    )(p, w, d_a, d_b)


def short_conv(p, w, name):
    t = p.shape[0]
    nw = w.shape[0]
    taps = _taps_1d(3)

    def body(gb_ref, gc_ref, h_ref, w_ref, o_ref, pad_ref):
        _fill_pad(pad_ref, gc_ref[...] * h_ref[...], t)
        for r0, rows in _row_chunks(t):
            o_ref[r0:r0 + rows, :] = (gb_ref[r0:r0 + rows, :] * _taps_apply(pad_ref, w_ref, taps, r0, rows)).astype(bf16)

    return pl.pallas_call(
        body, out_shape=SDS((t, 4 * LANE), bf16), grid=(4,),
        in_specs=[_col_spec(t, 0), _col_spec(t, 4), _col_spec(t, 8), _w_spec(nw)], out_specs=_col_spec(t, 0),
        scratch_shapes=[pltpu.VMEM((t + 2 * PAD_ROWS, LANE), f32)], name=name,
        compiler_params=_cparams(("parallel",)),
    )(p, p, p, w)


def short_conv_bwd(p, w, dy, name):
    t = p.shape[0]
    nw = w.shape[0]
    taps = _taps_1d(3)
    ttaps = _transpose_taps(taps)

    def body(gb_ref, gc_ref, h_ref, w_ref, dy_ref, dgb_ref, dgc_ref, dh_ref, dw_ref, pad_ref, gpad_ref):
        _fill_pad(pad_ref, gc_ref[...] * h_ref[...], t)
        _fill_pad(gpad_ref, dy_ref[...] * gb_ref[...], t)
        dw = jnp.zeros((nw, LANE), f32)
        for r0, rows in _row_chunks(t):
            sl = slice(r0, r0 + rows)
            dgb_ref[sl, :] = (dy_ref[sl, :] * _taps_apply(pad_ref, w_ref, taps, r0, rows)).astype(bf16)
            dm = _taps_apply(gpad_ref, w_ref, ttaps, r0, rows)
            dgc_ref[sl, :] = (dm * h_ref[sl, :]).astype(bf16)
            dh_ref[sl, :] = (dm * gc_ref[sl, :]).astype(bf16)
            dw = dw + _taps_wgrad(pad_ref, gpad_ref[PAD_ROWS + r0:PAD_ROWS + r0 + rows, :], taps, r0, rows, nw)
        dw_ref[...] = dw

    blk = SDS((t, 4 * LANE), bf16)
    return pl.pallas_call(
        body, out_shape=(blk, blk, blk, SDS((nw, 4 * LANE), f32)), grid=(4,),
        in_specs=[_col_spec(t, 0), _col_spec(t, 4), _col_spec(t, 8), _w_spec(nw), _col_spec(t, 0)],
        out_specs=(_col_spec(t, 0), _col_spec(t, 0), _col_spec(t, 0), _w_spec(nw)),
        scratch_shapes=[pltpu.VMEM((t + 2 * PAD_ROWS, LANE), f32)] * 2, name=name,
        compiler_params=_cparams(("parallel",)),
    )(p, p, p, w, dy)


def conf_conv(p, w, name):
    t = p.shape[0]
    nw = w.shape[0]
    taps = _taps_1d(31)

    def body(a_ref, b_ref, w_ref, o_ref, pad_ref):
        _fill_pad(pad_ref, a_ref[...] * jax.nn.sigmoid(b_ref[...]), t)
        for r0, rows in _row_chunks(t):
            o_ref[r0:r0 + rows, :] = _taps_apply(pad_ref, w_ref, taps, r0, rows)

    return pl.pallas_call(
        body, out_shape=SDS((t, 4 * LANE), f32), grid=(4,),
        in_specs=[_col_spec(t, 12), _col_spec(t, 16), _w_spec(nw)], out_specs=_col_spec(t, 0),
        scratch_shapes=[pltpu.VMEM((t + 2 * PAD_ROWS, LANE), f32)], name=name,
        compiler_params=_cparams(("parallel",)),
    )(p, p, w)


def conf_conv_bwd(p, w, dz, name):
    t = p.shape[0]
    nw = w.shape[0]
    taps = _taps_1d(31)
    ttaps = _transpose_taps(taps)

    def body(a_ref, b_ref, w_ref, dz_ref, da_ref, db_ref, dw_ref, pad_ref, gpad_ref):
        _fill_pad(pad_ref, a_ref[...] * jax.nn.sigmoid(b_ref[...]), t)
        _fill_pad(gpad_ref, dz_ref[...], t)
        dw = jnp.zeros((nw, LANE), f32)
        for r0, rows in _row_chunks(t):
            sl = slice(r0, r0 + rows)
            dm = _taps_apply(gpad_ref, w_ref, ttaps, r0, rows)
            sg = jax.nn.sigmoid(b_ref[sl, :])
            da_ref[sl, :] = (dm * sg).astype(bf16)
            db_ref[sl, :] = (dm * a_ref[sl, :] * sg * (1.0 - sg)).astype(bf16)
            dw = dw + _taps_wgrad(pad_ref, dz_ref[sl, :], taps, r0, rows, nw)
        dw_ref[...] = dw

    blk = SDS((t, 4 * LANE), bf16)
    return pl.pallas_call(
        body, out_shape=(blk, blk, SDS((nw, 4 * LANE), f32)), grid=(4,),
        in_specs=[_col_spec(t, 12), _col_spec(t, 16), _w_spec(nw), _col_spec(t, 0)],
        out_specs=(_col_spec(t, 0), _col_spec(t, 0), _w_spec(nw)),
        scratch_shapes=[pltpu.VMEM((t + 2 * PAD_ROWS, LANE), f32)] * 2, name=name,
        compiler_params=_cparams(("parallel",)),
    )(p, p, w, dz)


def ffn_conv(h, w, name):
    t = h.shape[0]
    width = 2 * LANE
    nblk = D_FF // width
    nw = w.shape[0]
    taps = _taps_grid3()

    def body(a_ref, g_ref, w_ref, o_ref, *pads):
        for s in range(width // LANE):
            ls = slice(s * LANE, (s + 1) * LANE)
            _grid_pads_clear_edges(pads, t)
            for r0, rows in _row_chunks(t):
                _grid_pads_set(pads, r0, a_ref[r0:r0 + rows, ls])
            for r0, rows in _row_chunks(t):
                conv = _taps_apply(pads, w_ref.at[:, ls], taps, r0, rows)
                o_ref[r0:r0 + rows, ls] = (_silu(conv) * g_ref[r0:r0 + rows, ls]).astype(bf16)

    spec = lambda off: pl.BlockSpec((t, width), lambda c: (0, c + off))
    return pl.pallas_call(
        body, out_shape=SDS((t, D_FF), bf16), grid=(nblk,),
        in_specs=[spec(0), spec(nblk), pl.BlockSpec((nw, width), lambda c: (0, c))], out_specs=spec(0),
        scratch_shapes=[pltpu.VMEM((t + 2 * PAD_ROWS, LANE), f32)] * 3, name=name,
        compiler_params=_cparams(("parallel",)),
    )(h, h, w)


def ffn_conv_bwd(h, w, df, name):
    t = h.shape[0]
    nblk = D_FF // LANE
    nw = w.shape[0]
    taps = _taps_grid3()
    ttaps = _transpose_taps(taps)

    def body(a_ref, g_ref, w_ref, df_ref, dh_ref, dw_ref, *all_pads):
        half = pl.program_id(1)
        pads, gpads = all_pads[:3], all_pads[3:]

        @pl.when(half == 0)
        def _():
            _grid_pads_clear_edges(all_pads, t)
            for r0, rows in _row_chunks(t):
                _grid_pads_set(pads, r0, a_ref[r0:r0 + rows, :])
            for r0, rows in _row_chunks(t):
                sl = slice(r0, r0 + rows)
                pre = _taps_apply(pads, w_ref, taps, r0, rows)
                _grid_pads_set(gpads, r0, df_ref[sl, :] * g_ref[sl, :] * _dsilu(pre))
                dh_ref[sl, :] = (df_ref[sl, :] * _silu(pre)).astype(bf16)

        @pl.when(half == 1)
        def _():
            dw = jnp.zeros((nw, LANE), f32)
            for r0, rows in _row_chunks(t):
                dh_ref[r0:r0 + rows, :] = _taps_apply(gpads, w_ref, ttaps, r0, rows).astype(bf16)
                dw = dw + _taps_wgrad(pads, gpads[1][PAD_ROWS + r0:PAD_ROWS + r0 + rows, :], taps, r0, rows, nw)
            dw_ref[...] = dw

    cspec = lambda off: pl.BlockSpec((t, LANE), lambda c, s: (0, c + off))
    return pl.pallas_call(
        body, out_shape=(SDS((t, 2 * D_FF), bf16), SDS((nw, D_FF), f32)), grid=(nblk, 2),
        in_specs=[cspec(0), cspec(nblk), pl.BlockSpec((nw, LANE), lambda c, s: (0, c)), cspec(0)],
        out_specs=(pl.BlockSpec((t, LANE), lambda c, s: (0, c + nblk * (1 - s))), pl.BlockSpec((nw, LANE), lambda c, s: (0, c))),
        scratch_shapes=[pltpu.VMEM((t + 2 * PAD_ROWS, LANE), f32)] * 6, name=name,
        compiler_params=_cparams(("parallel", "arbitrary")),
    )(h, h, w, df)


def _pool_count(r0, rows, win, t):
    pos = lax.broadcasted_iota(jnp.int32, (rows, 1), 0) + r0
    lo = jnp.clip(pos - win // 2, 0, t)
    hi = jnp.clip(pos - win // 2 + win, 0, t)
    return (hi - lo).astype(f32)


def _window_sum(pad_ref, r0, rows, lo, hi):
    acc = jnp.zeros((rows, LANE), f32)
    for off in range(lo, hi):
        acc = acc + pad_ref[PAD_ROWS + r0 + off:PAD_ROWS + r0 + off + rows, :]
    return acc


def pool_mix(p, pool_w, pool_scale, name):
    t = p.shape[0]

    def body(x_ref, w_ref, s_ref, o_ref, pad_ref):
        for gi, win in enumerate(POOL_WINDOWS):
            cs = slice(gi * LANE, (gi + 1) * LANE)
            _fill_pad(pad_ref, x_ref[:, cs], t)
            wg = w_ref[gi].astype(bf16)
            for r0, rows in _row_chunks(t):
                pooled = _window_sum(pad_ref, r0, rows, -(win // 2), win - win // 2) / _pool_count(r0, rows, win, t) - x_ref[r0:r0 + rows, cs]
                o_ref[r0:r0 + rows, cs] = (_dotb(pooled, wg) * s_ref[:, cs]).astype(bf16)

    return pl.pallas_call(
        body, out_shape=SDS((t, 512), bf16), grid=(1,),
        in_specs=[pl.BlockSpec((t, 512), lambda i: (0, 4)), pl.BlockSpec((4, LANE, LANE), lambda i: (0, 0, 0)),
                  pl.BlockSpec((1, 512), lambda i: (0, 0))],
        out_specs=pl.BlockSpec((t, 512), lambda i: (0, 0)),
        scratch_shapes=[pltpu.VMEM((t + 2 * PAD_ROWS, LANE), f32)], name=name,
        compiler_params=_cparams(("arbitrary",)),
    )(p, pool_w, pool_scale)


def pool_mix_bwd(p, pool_w, pool_scale, dmix, name):
    t = p.shape[0]

    def body(x_ref, w_ref, s_ref, dy_ref, dp_ref, dw_ref, ds_ref, pad_ref, gpad_ref, dpool_ref):
        for gi, win in enumerate(POOL_WINDOWS):
            cs = slice(gi * LANE, (gi + 1) * LANE)
            h = win // 2
            _fill_pad(pad_ref, x_ref[:, cs], t)
            wg = w_ref[gi].astype(bf16)
            dw = jnp.zeros((LANE, LANE), f32)
            ds = jnp.zeros((1, LANE), f32)
            zeros = jnp.zeros((PAD_ROWS, LANE), f32)
            gpad_ref[0:PAD_ROWS, :] = zeros
            gpad_ref[PAD_ROWS + t:2 * PAD_ROWS + t, :] = zeros
            for r0, rows in _row_chunks(t):
                cnt = _pool_count(r0, rows, win, t)
                pooled = _window_sum(pad_ref, r0, rows, -h, win - h) / cnt - x_ref[r0:r0 + rows, cs]
                dy = dy_ref[r0:r0 + rows, cs]
                ds = ds + jnp.sum(dy * _dotb(pooled, wg), axis=0, keepdims=True)
                dypre = dy * s_ref[:, cs]
                dw = dw + _dotb_tn(pooled, dypre)
                dpooled = _dotb_nt(dypre, wg)
                gpad_ref[PAD_ROWS + r0:PAD_ROWS + r0 + rows, :] = dpooled / cnt
                dpool_ref[r0:r0 + rows, :] = dpooled
            dw_ref[gi] = dw
            ds_ref[:, cs] = ds
            for r0, rows in _row_chunks(t):
                dx = _window_sum(gpad_ref, r0, rows, -h + 1, h + 1) - dpool_ref[r0:r0 + rows, :]
                dp_ref[r0:r0 + rows, cs] = dx.astype(bf16)

    return pl.pallas_call(
        body, out_shape=(SDS((t, 512), bf16), SDS((4, LANE, LANE), f32), SDS((1, 512), f32)), grid=(1,),
        in_specs=[pl.BlockSpec((t, 512), lambda i: (0, 4)), pl.BlockSpec((4, LANE, LANE), lambda i: (0, 0, 0)),
                  pl.BlockSpec((1, 512), lambda i: (0, 0)), pl.BlockSpec((t, 512), lambda i: (0, 1))],
        out_specs=(pl.BlockSpec((t, 512), lambda i: (0, 0)), pl.BlockSpec((4, LANE, LANE), lambda i: (0, 0, 0)),
                   pl.BlockSpec((1, 512), lambda i: (0, 0))),
        scratch_shapes=[pltpu.VMEM((t + 2 * PAD_ROWS, LANE), f32)] * 2 + [pltpu.VMEM((t, LANE), f32)], name=name,
        compiler_params=_cparams(("arbitrary",)),
    )(p, pool_w, pool_scale, dmix)


def gated_rmsnorm(o_a, o_b, p, norm_w, name):
    t = o_a.shape[0]
    tt = _row_tile(t)

    def body(oa_ref, ob_ref, g_ref, nw_ref, y_ref):
        for h in range(GDN_HEADS):
            cs = slice(h * LANE, (h + 1) * LANE)
            o = oa_ref[:, cs] + ob_ref[:, cs]
            r = lax.rsqrt(jnp.mean(o * o, axis=-1, keepdims=True) + RMS_EPS)
            y_ref[:, cs] = (o * r * nw_ref[...] * _silu(g_ref[:, cs])).astype(bf16)

    return pl.pallas_call(
        body, out_shape=SDS((t, 512), bf16), grid=(t // tt,),
        in_specs=[_row_spec(tt, 512), _row_spec(tt, 512), pl.BlockSpec((tt, 512), lambda i: (i, 3)), _vec_spec(LANE)],
        out_specs=_row_spec(tt, 512), name=name, compiler_params=_cparams(("parallel",)),
    )(o_a, o_b, p, norm_w)


def gated_rmsnorm_bwd(o_a, o_b, p, norm_w, dmix, name):
    t = o_a.shape[0]
    tt = _row_tile(t)

    def body(oa_ref, ob_ref, g_ref, nw_ref, dy_ref, do_ref, dg_ref, dnw_ref):
        dnw = jnp.zeros((1, LANE), f32)
        for h in range(GDN_HEADS):
            cs = slice(h * LANE, (h + 1) * LANE)
            o = oa_ref[:, cs] + ob_ref[:, cs]
            r = lax.rsqrt(jnp.mean(o * o, axis=-1, keepdims=True) + RMS_EPS)
            gate = g_ref[:, cs]
            dy = dy_ref[:, cs]
            dy1 = dy * _silu(gate)
            dg_ref[:, cs] = (dy * (o * r * nw_ref[...]) * _dsilu(gate)).astype(bf16)
            dnw = dnw + jnp.sum(dy1 * o * r, axis=0, keepdims=True)
            dn = dy1 * nw_ref[...]
            do_ref[:, cs] = r * dn - o * (r * r * r) * jnp.mean(dn * o, axis=-1, keepdims=True)
        _acc_rows(dnw_ref, dnw)

    return pl.pallas_call(
        body, out_shape=(SDS((t, 512), f32), SDS((t, 512), bf16), SDS((1, LANE), f32)), grid=(t // tt,),
        in_specs=[_row_spec(tt, 512), _row_spec(tt, 512), pl.BlockSpec((tt, 512), lambda i: (i, 3)), _vec_spec(LANE),
                  _row_spec(tt, 512)],
        out_specs=(_row_spec(tt, 512), _row_spec(tt, 512), _vec_spec(LANE)),
        name=name, compiler_params=_cparams(("arbitrary",)),
    )(o_a, o_b, p, norm_w, dmix)


def ln_silu(z, g, b, name):
    t, d = z.shape
    tt = _row_tile(t)

    def body(z_ref, g_ref, b_ref, o_ref):
        xhat, _ = _ln_stats(z_ref[...])
        o_ref[...] = _silu(xhat * g_ref[...] + b_ref[...]).astype(bf16)

    return pl.pallas_call(
        body, out_shape=SDS((t, d), bf16), grid=(t // tt,),
        in_specs=[_row_spec(tt, d), _vec_spec(d), _vec_spec(d)], out_specs=_row_spec(tt, d),
        name=name, compiler_params=_cparams(("parallel",)),
    )(z, g, b)


def ln_silu_bwd(z, g, b, dmix, name):
    t, d = z.shape
    tt = _row_tile(t)

    def body(z_ref, g_ref, b_ref, dy_ref, dz_ref, dg_ref, db_ref):
        xhat, rstd = _ln_stats(z_ref[...])
        dn = dy_ref[...] * _dsilu(xhat * g_ref[...] + b_ref[...])
        dz_ref[...] = _ln_bwd(dn * g_ref[...], xhat, rstd)
        _acc_rows(dg_ref, jnp.sum(dn * xhat, axis=0, keepdims=True))
        _acc_rows(db_ref, jnp.sum(dn, axis=0, keepdims=True))

    return pl.pallas_call(
        body, out_shape=(SDS((t, d), f32), SDS((1, d), f32), SDS((1, d), f32)), grid=(t // tt,),
        in_specs=[_row_spec(tt, d), _vec_spec(d), _vec_spec(d), pl.BlockSpec((tt, d), lambda i: (i, 1))],
        out_specs=(_row_spec(tt, d), _vec_spec(d), _vec_spec(d)),
        name=name, compiler_params=_cparams(("arbitrary",)),
    )(z, g, b, dmix)


def gdn_gates(p, neg_a, dt_bias, name):
    t = p.shape[0]
    tt = _row_tile(t)

    def body(s_ref, na_ref, dt_ref, o_ref):
        s = s_ref[...]
        col = lax.broadcasted_iota(jnp.int32, s.shape, 1)
        o_ref[...] = jnp.where(col < 8, jax.nn.sigmoid(s), na_ref[...] * jax.nn.softplus(s + dt_ref[...]))

    return pl.pallas_call(
        body, out_shape=SDS((t, LANE), f32), grid=(t // tt,),
        in_specs=[pl.BlockSpec((tt, LANE), lambda i: (i, 20)), _vec_spec(LANE), _vec_spec(LANE)],
        out_specs=_row_spec(tt, LANE), name=name, compiler_params=_cparams(("parallel",)),
    )(p, neg_a, dt_bias)


def gdn_gates_bwd(p, neg_a, dt_bias, dbg_a, dbg_b, name):
    t = p.shape[0]
    tt = _row_tile(t)

    def body(s_ref, na_ref, dt_ref, d_ref, d2_ref, ds_ref, da_ref, ddt_ref):
        s = s_ref[...]
        d = d_ref[...] + d2_ref[...]
        col = lax.broadcasted_iota(jnp.int32, s.shape, 1)
        sg = jax.nn.sigmoid(s)
        z = s + dt_ref[...]
        dz = jnp.where((col >= 8) & (col < 16), d * na_ref[...] * jax.nn.sigmoid(z), 0.0)
        ds_ref[...] = jnp.where(col < 8, d * sg * (1.0 - sg), dz).astype(bf16)
        dalog = jnp.where((col >= 8) & (col < 16), d * na_ref[...] * jax.nn.softplus(z), 0.0)
        _acc_rows(da_ref, jnp.sum(dalog, axis=0, keepdims=True))
        _acc_rows(ddt_ref, jnp.sum(dz, axis=0, keepdims=True))

    return pl.pallas_call(
        body, out_shape=(SDS((t, LANE), bf16), SDS((1, LANE), f32), SDS((1, LANE), f32)), grid=(t // tt,),
        in_specs=[pl.BlockSpec((tt, LANE), lambda i: (i, 20)), _vec_spec(LANE), _vec_spec(LANE), _row_spec(tt, LANE),
                  _row_spec(tt, LANE)],
        out_specs=(_row_spec(tt, LANE), _vec_spec(LANE), _vec_spec(LANE)),
        name=name, compiler_params=_cparams(("arbitrary",)),
    )(p, neg_a, dt_bias, dbg_a, dbg_b)


N_SCAN = 2 * GDN_HEADS


def _bdot(a, b, ca, cb, precision=None):
    if precision is None:
        a, b = a.astype(bf16), b.astype(bf16)
    return lax.dot_general(a, b, (((ca,), (cb,)), ((0,), (0,))), preferred_element_type=f32, precision=precision)


def _bdot_nn(a, b, precision=None):
    return _bdot(a, b, 2, 1, precision)


def _bdot_nt(a, b):
    return _bdot(a, b, 2, 2)


def _bdot_tn(a, b, precision=None):
    return _bdot(a, b, 1, 1, precision)


def _order_masks():
    shape = (N_SCAN, CHUNK, CHUNK)
    sign = jnp.where(lax.broadcasted_iota(jnp.int32, shape, 0) >= GDN_HEADS, -1, 1)
    ahead = (lax.broadcasted_iota(jnp.int32, shape, 1) - lax.broadcasted_iota(jnp.int32, shape, 2)) * sign
    lower, strict, lower_t = ahead >= 0, ahead > 0, ahead <= 0
    col_shape = (N_SCAN, CHUNK, 1)
    back1 = lax.broadcasted_iota(jnp.int32, col_shape, 0) >= GDN_HEADS
    row1 = lax.broadcasted_iota(jnp.int32, col_shape, 1)
    at_last = (row1 == jnp.where(back1, 0, CHUNK - 1)).astype(f32)
    return lower, strict, lower_t, at_last


def _stack_heads(f_ref, b_ref):
    return jnp.stack([ref[:, h * LANE:(h + 1) * LANE] for ref in (f_ref, b_ref) for h in range(GDN_HEADS)])


def _stack_gates(bgf, bgb, bgtf, bgtb):
    beta = jnp.stack([bg[:, 4 * d + h:4 * d + h + 1] for d, bg in enumerate((bgf, bgb)) for h in range(GDN_HEADS)])
    g_col = jnp.stack([bg[:, 8 + 4 * d + h:9 + 4 * d + h] for d, bg in enumerate((bgf, bgb)) for h in range(GDN_HEADS)])
    g_row = jnp.stack([bgt[8 + 4 * d + h:9 + 4 * d + h, :] for d, bgt in enumerate((bgtf, bgtb)) for h in range(GDN_HEADS)])
    return beta, g_col, g_row


def _chunk_terms(k, v, beta, g_col, g_row, masks, tinv=None):
    lower, strict, lower_t, at_last = masks
    gc = jnp.sum(lower.astype(f32) * g_row, axis=2, keepdims=True)
    gr = jnp.sum(lower_t.astype(f32) * g_col, axis=1, keepdims=True)
    g_last = jnp.sum(at_last * gc, axis=1, keepdims=True)
    e = jnp.exp(gc)
    f = jnp.exp(g_last - gc)
    dm = jnp.exp(jnp.where(lower, gc - gr, -1e30))
    kb = k * beta
    kk = _bdot_nt(kb, k)
    if tinv is None:
        shape = (N_SCAN, CHUNK, CHUNK)
        eye = (lax.broadcasted_iota(jnp.int32, shape, 1) == lax.broadcasted_iota(jnp.int32, shape, 2)).astype(f32)
        pw = -jnp.where(strict, kk * dm, 0.0)
        tinv = eye + pw
        for _ in range(5):
            pw = _bdot_nn(pw, pw, lax.Precision.HIGH)
            tinv = tinv + _bdot_nn(tinv, pw, lax.Precision.HIGH)
    u = _bdot_nn(tinv, v * beta)
    w = _bdot_nn(tinv, kb * e)
    return dict(e=e, f=f, gl=jnp.exp(g_last), dm=dm, kb=kb, kk=kk, tinv=tinv, u=u, w=w, kd=k * f)


def _gdn_specs(nc, width, step_chunk):
    return [pl.BlockSpec((CHUNK, width), functools.partial(lambda i, d: (step_chunk(i, d), 0), d=d)) for d in (0, 1)]


def gdn_forward(q, k, v, bg, bgt, s0, with_out, name):
    t = k.shape[0]
    nc = t // CHUNK

    def body(qf_ref, qb_ref, kf_ref, kb_ref, vf_ref, vb_ref, bgf_ref, bgb_ref, bgtf_ref, bgtb_ref, s0_ref,
             of_ref, ob_ref, sallf_ref, sallb_ref, tinvf_ref, tinvb_ref, sfin_ref, s_ref):
        i = pl.program_id(0)

        @pl.when(i == 0)
        def _():
            s_ref[...] = s0_ref[...]

        masks = _order_masks()
        k8, v8 = _stack_heads(kf_ref, kb_ref), _stack_heads(vf_ref, vb_ref)
        beta, g_col, g_row = _stack_gates(bgf_ref[...], bgb_ref[...], bgtf_ref[0], bgtb_ref[0])
        c = _chunk_terms(k8, v8, beta, g_col, g_row, masks)
        s = s_ref[...]
        sallf_ref[0] = s[:GDN_HEADS]
        sallb_ref[0] = s[GDN_HEADS:]
        tinvf_ref[0] = c["tinv"][:GDN_HEADS]
        tinvb_ref[0] = c["tinv"][GDN_HEADS:]
        vn = c["u"] - _bdot_nn(c["w"], s)
        if with_out:
            q8 = _stack_heads(qf_ref, qb_ref)
            pm = jnp.where(masks[0], _bdot_nt(q8, k8) * c["dm"], 0.0)
            o = _bdot_nn(q8 * c["e"], s) + _bdot_nn(pm, vn)
        for d, o_ref in enumerate((of_ref, ob_ref)):
            for h in range(GDN_HEADS):
                o_ref[:, h * LANE:(h + 1) * LANE] = o[GDN_HEADS * d + h] if with_out else jnp.zeros((CHUNK, LANE), f32)
        s_ref[...] = c["gl"] * s + _bdot_tn(c["kd"], vn)

        @pl.when(i == nc - 1)
        def _():
            sfin_ref[...] = s_ref[...]

    chunk_of = lambda i, d: i if d == 0 else nc - 1 - i
    seq = _gdn_specs(nc, 512, chunk_of)
    gate = _gdn_specs(nc, LANE, chunk_of)
    gate_t = [pl.BlockSpec((1, 16, CHUNK), functools.partial(lambda i, d: (chunk_of(i, d), 0, 0), d=d)) for d in (0, 1)]
    sall = [pl.BlockSpec((1, GDN_HEADS, LANE, LANE), functools.partial(lambda i, d: (chunk_of(i, d), 0, 0, 0), d=d)) for d in (0, 1)]
    tinv = [pl.BlockSpec((1, GDN_HEADS, CHUNK, CHUNK), functools.partial(lambda i, d: (chunk_of(i, d), 0, 0, 0), d=d)) for d in (0, 1)]
    st = pl.BlockSpec((N_SCAN, LANE, LANE), lambda i: (0, 0, 0))
    o_shape, s_shape, t_shape = SDS((t, 512), f32), SDS((nc, GDN_HEADS, LANE, LANE), f32), SDS((nc, GDN_HEADS, CHUNK, CHUNK), f32)
    o_f, o_b, sall_f, sall_b, tinv_f, tinv_b, s_fin = pl.pallas_call(
        body, out_shape=(o_shape, o_shape, s_shape, s_shape, t_shape, t_shape, SDS((N_SCAN, LANE, LANE), f32)), grid=(nc,),
        in_specs=seq + seq + seq + gate + gate_t + [st], out_specs=tuple(seq + sall + tinv + [st]),
        scratch_shapes=[pltpu.VMEM((N_SCAN, LANE, LANE), f32)], name=name,
        compiler_params=_cparams(("arbitrary",)),
    )(q, q, k, k, v, v, bg, bg, bgt, bgt, s0.reshape(N_SCAN, LANE, LANE))
    return o_f, o_b, (sall_f, sall_b, tinv_f, tinv_b), s_fin.reshape(2, GDN_HEADS, LANE, LANE)


def _gdn_chunk_bwd(q, k, v, d_o, beta, g_col, g_row, s, tinv, dsn, masks):
    lower, strict, _, at_last = masks
    c = _chunk_terms(k, v, beta, g_col, g_row, masks, tinv)
    e, f, gl, dm, kb, kk, tinv, u, w, kd = (c[n] for n in ("e", "f", "gl", "dm", "kb", "kk", "tinv", "u", "w", "kd"))
    vn = u - _bdot_nn(w, s)
    ds = gl * dsn
    dgl = jnp.sum(jnp.sum(s * dsn, axis=2, keepdims=True), axis=1, keepdims=True)
    dkd = _bdot_nt(vn, dsn)
    dvn = _bdot_nn(kd, dsn)
    dm_grad = jnp.zeros((N_SCAN, CHUNK, CHUNK), f32)
    de = jnp.zeros((N_SCAN, CHUNK, 1), f32)
    dq = None
    dk = jnp.zeros((N_SCAN, CHUNK, LANE), f32)
    if q is not None:
        qk = _bdot_nt(q, k)
        pm = jnp.where(lower, qk * dm, 0.0)
        dqd = _bdot_nt(d_o, s)
        ds = ds + _bdot_tn(q * e, d_o)
        dpm = jnp.where(lower, _bdot_nt(d_o, vn), 0.0)
        dvn = dvn + _bdot_tn(pm, d_o)
        dqk = dpm * dm
        dm_grad = dm_grad + dpm * qk
        dq = _bdot_nn(dqk, k) + dqd * e
        dk = _bdot_tn(dqk, q)
        de = de + jnp.sum(dqd * q, axis=2, keepdims=True)
    dw = -_bdot_nt(dvn, s)
    ds = ds - _bdot_tn(w, dvn)
    drv = _bdot_tn(tinv, dvn)
    drk = _bdot_tn(tinv, dw)
    da = -jnp.where(strict, _bdot_nt(drv, u) + _bdot_nt(drk, w), 0.0)
    dbeta = jnp.sum(drv * v, axis=2, keepdims=True)
    dv = drv * beta
    dkb = drk * e
    de = de + jnp.sum(drk * kb, axis=2, keepdims=True)
    dkk = da * dm
    dm_grad = dm_grad + da * kk
    dkb = dkb + _bdot_nn(dkk, k)
    dk = dk + _bdot_tn(dkk, kb) + dkd * f
    df = jnp.sum(dkd * k, axis=2, keepdims=True)
    dbeta = dbeta + jnp.sum(dkb * k, axis=2, keepdims=True)
    dk = dk + dkb * beta
    m = dm_grad * dm
    shape = (N_SCAN, CHUNK, CHUNK)
    eye = (lax.broadcasted_iota(jnp.int32, shape, 1) == lax.broadcasted_iota(jnp.int32, shape, 2)).astype(f32)

    def as_col(row):
        return jnp.sum(eye * row, axis=2, keepdims=True)

    rsum = jnp.sum(m, axis=2, keepdims=True)
    csum = as_col(jnp.sum(m, axis=1, keepdims=True))
    dgl_tot = jnp.sum(df * f, axis=1, keepdims=True) + dgl * gl
    dgc = de * e - df * f + rsum - csum + at_last * dgl_tot
    dg = as_col(jnp.sum(lower.astype(f32) * dgc, axis=1, keepdims=True))
    return dq, dk, dv, dbeta, dg, ds


def gdn_backward(q, k, v, bg, bgt, saved, d_o, ds_fin, with_out, name):
    t = k.shape[0]
    nc = t // CHUNK

    def body(qf_ref, qb_ref, kf_ref, kb_ref, vf_ref, vb_ref, bgf_ref, bgb_ref, bgtf_ref, bgtb_ref,
             sallf_ref, sallb_ref, tinvf_ref, tinvb_ref, dof_ref, dob_ref, dsf_ref,
             dqf_ref, dqb_ref, dkf_ref, dkb_ref, dvf_ref, dvb_ref, dbgf_ref, dbgb_ref, ds0_ref, ds_ref):
        i = pl.program_id(0)

        @pl.when(i == 0)
        def _():
            ds_ref[...] = dsf_ref[...]

        lane = lax.broadcasted_iota(jnp.int32, (1, LANE), 1)
        masks = _order_masks()
        beta, g_col, g_row = _stack_gates(bgf_ref[...], bgb_ref[...], bgtf_ref[0], bgtb_ref[0])
        s = jnp.concatenate([sallf_ref[0], sallb_ref[0]], 0)
        tinv = jnp.concatenate([tinvf_ref[0], tinvb_ref[0]], 0)
        dq, dk, dv, dbeta, dg, ds = _gdn_chunk_bwd(
            _stack_heads(qf_ref, qb_ref) if with_out else None, _stack_heads(kf_ref, kb_ref), _stack_heads(vf_ref, vb_ref),
            _stack_heads(dof_ref, dob_ref), beta, g_col, g_row, s, tinv, ds_ref[...], masks)
        ds_ref[...] = ds
        for d, (dq_ref, dk_ref, dv_ref, dbg_ref) in enumerate(((dqf_ref, dkf_ref, dvf_ref, dbgf_ref), (dqb_ref, dkb_ref, dvb_ref, dbgb_ref))):
            dbg = jnp.zeros((CHUNK, LANE), f32)
            for h in range(GDN_HEADS):
                b = GDN_HEADS * d + h
                cs = slice(h * LANE, (h + 1) * LANE)
                dq_ref[:, cs] = dq[b] if with_out else jnp.zeros((CHUNK, LANE), f32)
                dk_ref[:, cs] = dk[b]
                dv_ref[:, cs] = dv[b]
                dbg = dbg + dbeta[b] * (lane == b).astype(f32) + dg[b] * (lane == 8 + b).astype(f32)
            dbg_ref[...] = dbg

        @pl.when(i == nc - 1)
        def _():
            ds0_ref[...] = ds_ref[...]

    chunk_of = lambda i, d: nc - 1 - i if d == 0 else i
    seq = _gdn_specs(nc, 512, chunk_of)
    gate = _gdn_specs(nc, LANE, chunk_of)
    gate_t = [pl.BlockSpec((1, 16, CHUNK), functools.partial(lambda i, d: (chunk_of(i, d), 0, 0), d=d)) for d in (0, 1)]
    sall = [pl.BlockSpec((1, GDN_HEADS, LANE, LANE), functools.partial(lambda i, d: (chunk_of(i, d), 0, 0, 0), d=d)) for d in (0, 1)]
    tinv = [pl.BlockSpec((1, GDN_HEADS, CHUNK, CHUNK), functools.partial(lambda i, d: (chunk_of(i, d), 0, 0, 0), d=d)) for d in (0, 1)]
    st = pl.BlockSpec((N_SCAN, LANE, LANE), lambda i: (0, 0, 0))
    o_shape, g_shape = SDS((t, 512), f32), SDS((t, LANE), f32)
    res = pl.pallas_call(
        body, out_shape=(o_shape,) * 6 + (g_shape, g_shape, SDS((N_SCAN, LANE, LANE), f32)), grid=(nc,),
        in_specs=seq + seq + seq + gate + gate_t + sall + tinv + seq + [st], out_specs=tuple(seq + seq + seq + gate + [st]),
        scratch_shapes=[pltpu.VMEM((N_SCAN, LANE, LANE), f32)], name=name,
        compiler_params=_cparams(("arbitrary",)),
    )(q, q, k, k, v, v, bg, bg, bgt, bgt, *saved, d_o, d_o, ds_fin.reshape(N_SCAN, LANE, LANE))
    return tuple(res[:8]) + (res[8].reshape(2, GDN_HEADS, LANE, LANE),)


def _my_position():
    x, y, c = lax.axis_index("x"), lax.axis_index("y"), lax.axis_index("c")
    return x, y, c, 4 * x + 2 * y + c


def exchange(arrays, scatter, name):
    n = len(arrays)
    shapes = [a.shape[1:] if scatter else a.shape for a in arrays]

    def body(*refs):
        ins, outs, token = refs[:n], refs[n:2 * n], refs[2 * n]
        send_sems, recv_sems, local_sems = refs[2 * n + 1:]
        x, y, c, me = _my_position()
        token[...] = jnp.zeros_like(token)
        started = []
        for a in range(n):
            mine = pltpu.make_async_copy(ins[a].at[me] if scatter else ins[a], outs[a].at[me], local_sems.at[a])
            mine.start()
            started.append(mine)
        waits = []
        for r in range(1, N_DEV):
            px = 1 - x if r & 4 else x
            py = 1 - y if r & 2 else y
            pc = 1 - c if r & 1 else c
            pid = 4 * px + 2 * py + pc
            for a in range(n):
                cp = pltpu.make_async_remote_copy(
                    src_ref=ins[a].at[pid] if scatter else ins[a], dst_ref=outs[a].at[me],
                    send_sem=send_sems.at[a, r - 1], recv_sem=recv_sems.at[a, r - 1],
                    device_id=(px, py, pc), device_id_type=pl.DeviceIdType.MESH)
                cp.start()
                arrive = pltpu.make_async_remote_copy(
                    src_ref=ins[a].at[pid] if scatter else ins[a], dst_ref=outs[a].at[pid],
                    send_sem=send_sems.at[a, r - 1], recv_sem=recv_sems.at[a, r - 1],
                    device_id=(px, py, pc), device_id_type=pl.DeviceIdType.MESH)
                waits.append((cp, arrive))
        for cp, arrive in waits:
            cp.wait_send()
            arrive.wait_recv()
        for mine in started:
            mine.wait()

    any_spec = pl.BlockSpec(memory_space=pl.ANY)
    return pl.pallas_call(
        body, out_shape=tuple(SDS((N_DEV,) + tuple(s), a.dtype) for s, a in zip(shapes, arrays)) + (SDS((8, LANE), f32),),
        in_specs=[any_spec] * n, out_specs=tuple([any_spec] * n) + (pl.BlockSpec(memory_space=pltpu.VMEM),),
        scratch_shapes=[pltpu.SemaphoreType.DMA((n, N_DEV - 1)), pltpu.SemaphoreType.DMA((n, N_DEV - 1)),
                        pltpu.SemaphoreType.DMA((n,))],
        name=name,
    )(*arrays)


_HBM_SPEC = pl.BlockSpec(memory_space=pltpu.HBM)
_SEM_SPEC = pl.BlockSpec(memory_space=pltpu.SEMAPHORE)
_DATAFLOW = pltpu.SideEffectType.DATAFLOW_SIDE_EFFECTING


def _peers(x, y, c):
    out = []
    for r in range(1, N_DEV):
        px = 1 - x if r & 4 else x
        py = 1 - y if r & 2 else y
        pc = 1 - c if r & 1 else c
        out.append((r, (px, py, pc), 4 * px + 2 * py + pc))
    return out


def _exchange_copies(ins, lands, send_sems, recv_sems, scatter, arrivals):
    x, y, c, me = _my_position()
    pairs = []
    for r, peer, pid in _peers(x, y, c):
        for a in range(len(ins)):
            k = a * (N_DEV - 1) + r - 1
            kw = dict(send_sem=send_sems.at[k], recv_sem=recv_sems.at[k], device_id=peer, device_id_type=pl.DeviceIdType.MESH)
            src = ins[a].at[pid] if scatter else ins[a]
            send = pltpu.make_async_remote_copy(src_ref=src, dst_ref=lands[a].at[me], **kw)
            arrive = pltpu.make_async_remote_copy(src_ref=src, dst_ref=lands[a].at[pid], **kw) if arrivals else None
            pairs.append((send, arrive))
    return pairs


def exchange_start(arrays, scatter, name):
    n = len(arrays)
    shapes = [a.shape[1:] if scatter else a.shape for a in arrays]

    def body(*refs):
        ins, lands = refs[:n], refs[n:2 * n]
        send_sems, recv_sems = refs[2 * n], refs[2 * n + 1]
        token = refs[-1]
        for send, _ in _exchange_copies(ins, lands, send_sems, recv_sems, scatter, False):
            send.start()
        token[...] = jnp.zeros_like(token)

    sem = pltpu.SemaphoreType.DMA((n * (N_DEV - 1),))
    land_shapes = [(N_DEV,) + tuple(s) for s in shapes]
    res = pl.pallas_call(
        body, name=name,
        out_shape=(sem, sem, *[pltpu.HBM(a.shape, a.dtype) for a in arrays],
                   *[pltpu.HBM(s, a.dtype) for s, a in zip(land_shapes, arrays)], SDS((8, LANE), f32)),
        in_specs=[_HBM_SPEC] * (2 * n),
        out_specs=(_SEM_SPEC, _SEM_SPEC, *[_HBM_SPEC] * (2 * n), pl.BlockSpec(memory_space=pltpu.VMEM)),
        input_output_aliases={i: 2 + i for i in range(2 * n)},
        compiler_params=pltpu.CompilerParams(has_side_effects=_DATAFLOW),
    )(*[pltpu.with_memory_space_constraint(a, pltpu.HBM) for a in arrays],
      *[pltpu.with_memory_space_constraint(lax.empty(s, a.dtype), pltpu.HBM) for s, a in zip(land_shapes, arrays)])
    return (res[0], res[1], list(res[2:2 + n]), list(res[2 + n:2 + 2 * n]), scatter), res[-1]


def exchange_wait(handle, after, name):
    send_sems, recv_sems, ins, lands, scatter = handle
    n = len(ins)

    def body(*refs):
        in_refs, land_refs = refs[:n], refs[n:2 * n]
        for send, arrive in _exchange_copies(in_refs, land_refs, refs[2 * n], refs[2 * n + 1], scatter, True):
            send.wait_send()
            arrive.wait_recv()
        refs[-1][...] = jnp.zeros_like(refs[-1])

    res = pl.pallas_call(
        body, name=name,
        out_shape=tuple(pltpu.HBM(a.shape, a.dtype) for a in ins + lands) + (SDS((8, LANE), f32),),
        in_specs=[_HBM_SPEC] * (2 * n) + [_SEM_SPEC, _SEM_SPEC, pl.BlockSpec(memory_space=pl.ANY)],
        out_specs=tuple([_HBM_SPEC] * (2 * n)) + (pl.BlockSpec(memory_space=pltpu.VMEM),),
        input_output_aliases={i: i for i in range(2 * n)},
        compiler_params=pltpu.CompilerParams(has_side_effects=_DATAFLOW),
    )(*ins, *lands, send_sems, recv_sems, after)
    return list(res[:n]), list(res[n:2 * n]), res[-1]


def place_own(lands, arrays, scatter, me):
    own = [lax.dynamic_index_in_dim(a, me, 0, keepdims=False) if scatter else a for a in arrays]
    return [lax.dynamic_update_index_in_dim(l, o, me, 0) for l, o in zip(lands, own)]


def ada_forward(a_raw, ada_w, ada_b_loc, name):
    def body(a_ref, w_ref, b_ref, o_ref):
        a = _silu(a_ref[...])
        for l in range(DEPTH):
            o_ref[l] = _dotf(a, w_ref[l]) + b_ref[l]

    return pl.pallas_call(body, out_shape=SDS((DEPTH, 16, ada_w.shape[2]), f32), name=name,
                          compiler_params=_cparams())(a_raw, ada_w, ada_b_loc)


def ada_backward(a_raw, ada_w, dm, name):
    def body(a_ref, w_ref, dm_ref, gw_ref, dcc_ref):
        a = _silu(a_ref[...])
        for l in range(DEPTH):
            gw_ref[l] = _dotf(a, dm_ref[l], (((0,), (0,)), ((), ())))
        dcc_ref[...] = _dotf(dm_ref[0, 8:16, :], w_ref[0], (((1,), (1,)), ((), ())))

    return pl.pallas_call(body, out_shape=(SDS(ada_w.shape, f32), SDS((8, ada_w.shape[1]), f32)), name=name,
                          compiler_params=_cparams())(a_raw, ada_w, dm)


def sum_parts(parts, name):
    _, r, c = parts.shape

    def body(p_ref, o_ref):
        acc = p_ref[0]
        for i in range(1, N_DEV):
            acc = acc + p_ref[i]
        o_ref[...] = acc

    return pl.pallas_call(body, out_shape=SDS((r, c), f32), name=name, compiler_params=_cparams())(parts)


def cctx_grad(parts, c_ctx, name):
    def body(p_ref, c_ref, o_ref):
        acc = p_ref[0, 0:1, :]
        for i in range(1, N_DEV):
            acc = acc + p_ref[i, 0:1, :]
        o_ref[...] = acc * _dsilu(c_ref[...])

    return pl.pallas_call(body, out_shape=SDS((1, c_ctx.shape[1]), f32), name=name, compiler_params=_cparams())(parts, c_ctx)


def _adamw_math(g, w, m, v):
    m = ADAM_B1 * m + (1.0 - ADAM_B1) * g
    v = ADAM_B2 * v + (1.0 - ADAM_B2) * (g * g)
    m_hat = m / (1.0 - ADAM_B1 ** ADAM_STEP)
    v_hat = v / (1.0 - ADAM_B2 ** ADAM_STEP)
    delta = -ADAM_LR * (m_hat / (jnp.sqrt(v_hat) + ADAM_EPS) + ADAM_WD * w)
    return delta, m, v


def adamw(parts, w, m, v, name):
    n, r, c = parts.shape
    tr = _pick(r, (256, 128, 64, 32, 16, 8))

    def body(p_ref, w_ref, m_ref, v_ref, g_ref, d_ref, nm_ref, nv_ref):
        g = p_ref[0].astype(f32)
        for i in range(1, n):
            g = g + p_ref[i].astype(f32)
        g_ref[...] = g
        d_ref[...], nm_ref[...], nv_ref[...] = _adamw_math(g, w_ref[...], m_ref[...], v_ref[...])

    blk = pl.BlockSpec((tr, c), lambda i: (i, 0))
    out = SDS((r, c), f32)
    return pl.pallas_call(
        body, out_shape=(out, out, out, out), grid=(r // tr,),
        in_specs=[pl.BlockSpec((n, tr, c), lambda i: (0, i, 0)), blk, blk, blk], out_specs=(blk, blk, blk, blk),
        name=name, compiler_params=_cparams(("parallel",)),
    )(parts, w, m, v)


def adamw_small(items, name):
    n = len(items)

    def body(*refs):
        ins, outs = refs[:4 * n], refs[4 * n:]
        for i in range(n):
            g, w, m, v = (ins[4 * i + j][...] for j in range(4))
            outs[3 * i][...], outs[3 * i + 1][...], outs[3 * i + 2][...] = _adamw_math(g, w, m, v)

    flat = [a for it in items for a in it]
    out_shape = tuple(SDS(it[1].shape, f32) for it in items for _ in range(3))
    res = pl.pallas_call(body, out_shape=out_shape, name=name, compiler_params=_cparams())(*flat)
    return [tuple(res[3 * i:3 * i + 3]) for i in range(n)]


def _unshard(g, axis):
    loc = g.shape[1:]
    return jnp.moveaxis(g, 0, axis).reshape(loc[:axis] + (N_DEV * loc[axis],) + loc[axis + 1:])


def _shard_major(full, axis):
    s = full.shape
    return jnp.moveaxis(full.reshape(s[:axis] + (N_DEV, s[axis] // N_DEV) + s[axis + 1:]), axis, 0)


def _my_block(full, axis, me):
    n = full.shape[axis] // N_DEV
    return lax.dynamic_slice_in_dim(full, me * n, n, axis)


def _pack(arrays):
    flat = [a.reshape(-1) for a in arrays]
    sizes = [f.shape[0] for f in flat]
    total = sum(sizes)
    padded = -(-total // (8 * LANE)) * (8 * LANE)
    flat.append(jnp.zeros((padded - total,), f32))
    offs = [sum(sizes[:i]) for i in range(len(sizes))]
    return jnp.concatenate(flat).reshape(padded // LANE, LANE), offs


def _pad_rows(w, n):
    return jnp.concatenate([w, jnp.zeros((n - w.shape[0],) + w.shape[1:], w.dtype)], 0)


def _gate_rows(bg):
    return bg[:, :16].reshape(bg.shape[0] // CHUNK, CHUNK, 16).transpose(0, 2, 1)


def _rows(vec, n):
    m = vec.reshape(n, 1, -1)
    return [m[i] for i in range(n)]


def kernel(x, c, ctx, c_ctx, ada_w, ada_b, ln_g, ln_b, even_w_in, even_w_out, gdn_conv_w, gdn_a_log, gdn_dt_bias, gdn_norm_w, pool_w, pool_scale, odd_w_in, odd_w_out, sconv_w, conf_conv_w, conf_ln_g, conf_ln_b, ffn_w_up, ffn_conv_w, ffn_w_down, loss_target, m_c_ctx, m_ada_w, m_ada_b, m_ln_g, m_ln_b, m_even_w_in, m_even_w_out, m_gdn_conv_w, m_gdn_a_log, m_gdn_dt_bias, m_gdn_norm_w, m_pool_w, m_pool_scale, m_odd_w_in, m_odd_w_out, m_sconv_w, m_conf_conv_w, m_conf_ln_g, m_conf_ln_b, m_ffn_w_up, m_ffn_conv_w, m_ffn_w_down, v_c_ctx, v_ada_w, v_ada_b, v_ln_g, v_ln_b, v_even_w_in, v_even_w_out, v_gdn_conv_w, v_gdn_a_log, v_gdn_dt_bias, v_gdn_norm_w, v_pool_w, v_pool_scale, v_odd_w_in, v_odd_w_out, v_sconv_w, v_conf_conv_w, v_conf_ln_g, v_conf_ln_b, v_ffn_w_up, v_ffn_conv_w, v_ffn_w_down):
    weights = dict(c_ctx=c_ctx, ada_w=ada_w, ada_b=ada_b, ln_g=ln_g, ln_b=ln_b, even_w_in=even_w_in, even_w_out=even_w_out, gdn_conv_w=gdn_conv_w, gdn_a_log=gdn_a_log, gdn_dt_bias=gdn_dt_bias, gdn_norm_w=gdn_norm_w, pool_w=pool_w, pool_scale=pool_scale, odd_w_in=odd_w_in, odd_w_out=odd_w_out, sconv_w=sconv_w, conf_conv_w=conf_conv_w, conf_ln_g=conf_ln_g, conf_ln_b=conf_ln_b, ffn_w_up=ffn_w_up, ffn_conv_w=ffn_conv_w, ffn_w_down=ffn_w_down)
    mom1 = dict(c_ctx=m_c_ctx, ada_w=m_ada_w, ada_b=m_ada_b, ln_g=m_ln_g, ln_b=m_ln_b, even_w_in=m_even_w_in, even_w_out=m_even_w_out, gdn_conv_w=m_gdn_conv_w, gdn_a_log=m_gdn_a_log, gdn_dt_bias=m_gdn_dt_bias, gdn_norm_w=m_gdn_norm_w, pool_w=m_pool_w, pool_scale=m_pool_scale, odd_w_in=m_odd_w_in, odd_w_out=m_odd_w_out, sconv_w=m_sconv_w, conf_conv_w=m_conf_conv_w, conf_ln_g=m_conf_ln_g, conf_ln_b=m_conf_ln_b, ffn_w_up=m_ffn_w_up, ffn_conv_w=m_ffn_conv_w, ffn_w_down=m_ffn_w_down)
    mom2 = dict(c_ctx=v_c_ctx, ada_w=v_ada_w, ada_b=v_ada_b, ln_g=v_ln_g, ln_b=v_ln_b, even_w_in=v_even_w_in, even_w_out=v_even_w_out, gdn_conv_w=v_gdn_conv_w, gdn_a_log=v_gdn_a_log, gdn_dt_bias=v_gdn_dt_bias, gdn_norm_w=v_gdn_norm_w, pool_w=v_pool_w, pool_scale=v_pool_scale, odd_w_in=v_odd_w_in, odd_w_out=v_odd_w_out, sconv_w=v_sconv_w, conf_conv_w=v_conf_conv_w, conf_ln_g=v_conf_ln_g, conf_ln_b=v_conf_ln_b, ffn_w_up=v_ffn_w_up, ffn_conv_w=v_ffn_conv_w, ffn_w_down=v_ffn_w_down)
    order = list(weights)
    me = 4 * lax.axis_index("x") + 2 * lax.axis_index("y") + lax.axis_index("c")
    x, ctx, target = x[0], ctx[0], loss_target[0]
    t, d = x.shape
    tc = ctx.shape[0]

    small_in = [ln_g, ln_b, gdn_conv_w, sconv_w, conf_conv_w, ffn_conv_w, c]
    small_axes = [2, 2, 1, 1, 1, 3, 0]
    small_pack, small_offs = _pack(small_in)
    sm = exchange([small_pack], False, "gather_first")[0].reshape(N_DEV, -1)
    e_in = even_w_in.shape[1] * N_DEV
    e_pad = -(-e_in // LANE) * LANE
    lng_f, lnb_f, gconv_f, sconv_f, cconv_f, fconv_f, c_all = [
        _unshard(sm[:, o:o + a.size].reshape((N_DEV,) + a.shape), ax) for a, o, ax in zip(small_in, small_offs, small_axes)]
    gw8 = _pad_rows(gconv_f, 8)
    sw8 = _pad_rows(sconv_f, 8)
    cw32 = _pad_rows(cconv_f, 32)
    fw16 = [_pad_rows(fconv_f[l].reshape(9, D_FF), 16) for l in range(DEPTH)]

    a_raw = jnp.concatenate([c_all, c_ctx[None], jnp.zeros((7, d), f32)], 0)
    ncol = ada_w.shape[2]
    ada_b_loc = lax.dynamic_slice_in_dim(ada_b, me * ncol, ncol, 1)[:, None, :]
    modpart = ada_forward(a_raw, ada_w, ada_b_loc, "ada_forward")
    mod_send = jnp.stack([jnp.transpose(modpart[:, :N_DEV], (1, 0, 2)),
                          jnp.broadcast_to(modpart[:, N_DEV][None], (N_DEV, DEPTH, ncol))], axis=2)
    mod_recv, token = exchange([mod_send], True, "scatter_mod")
    gather_e, token = exchange_start([even_w_in.astype(bf16) + token[0, 0].astype(bf16)], False, "gather_even_in_start")
    wire_l0 = [even_w_out.astype(bf16) + token[0, 0].astype(bf16), ffn_w_up[0].astype(bf16), ffn_w_down[0].astype(bf16)]
    gather_l0, token = exchange_start(wire_l0, False, "gather_l0_start")
    wire_l1 = [odd_w_in.astype(bf16) + token[0, 0].astype(bf16), odd_w_out.astype(bf16), ffn_w_up[1].astype(bf16),
               ffn_w_down[1].astype(bf16)]
    gather_l1, token = exchange_start(wire_l1, False, "gather_l1_start")
    mod_recv = mod_recv + token[0, 0]
    mod = jnp.transpose(mod_recv[:, :, 0, :], (1, 0, 2)).reshape(DEPTH, 6 * d)
    modc = mod_recv[:, 0, 1, :].reshape(6 * d)
    sh_c, sc_c = modc[None, :d], modc[None, d:2 * d]
    mods = [_rows(mod[l], 6) for l in range(DEPTH)]
    lng = [[lng_f[l, j][None] for j in range(2)] for l in range(DEPTH)]
    lnb = [[lnb_f[l, j][None] for j in range(2)] for l in range(DEPTH)]

    neg_a = jnp.zeros((1, LANE), f32).at[0, 8:16].set(-jnp.exp(gdn_a_log).reshape(8))
    dt_row = jnp.zeros((1, LANE), f32).at[0, 8:16].set(gdn_dt_bias.reshape(8))
    nw_row, ps_row = gdn_norm_w[None], pool_scale[None]
    cg_row, cb_row = conf_ln_g[None], conf_ln_b[None]
    q_scale = GDN_DK ** -0.5

    sh_m, sc_m, gt_m, sh_f, sc_f, gt_f = mods[0]
    u0 = modulate(x, sc_m, sh_m, "mod_l0_mix")
    cu = modulate(ctx, sc_c, sh_c, "mod_ctx")
    sent, landed, _ = exchange_wait(gather_e, u0, "gather_even_in_wait")
    win_e = jnp.pad(_unshard(place_own(landed, sent, False, me)[0], 1), ((0, 0), (0, e_pad - e_in)))
    p0 = matmul(u0, win_e, "nn", f32, "even_in")
    pc = matmul(cu, win_e, "nn", f32, "even_in_ctx")
    qn = gdn_conv(p0, gw8, 0, 4, q_scale, "gdn_conv_q")
    kn = gdn_conv(p0, gw8, 4, 4, 1.0, "gdn_conv_k")
    vv = gdn_conv(p0, gw8, 8, 4, None, "gdn_conv_v")
    kc = gdn_conv(pc, gw8, 4, 4, 1.0, "gdn_conv_k_ctx")
    vc = gdn_conv(pc, gw8, 8, 4, None, "gdn_conv_v_ctx")
    bg = gdn_gates(p0, neg_a, dt_row, "gdn_gates")
    bgc = gdn_gates(pc, neg_a, dt_row, "gdn_gates_ctx")
    bgt, bgtc = _gate_rows(bg), _gate_rows(bgc)
    zero_state = jnp.zeros((2, GDN_HEADS, LANE, LANE), f32)
    _, _, saved_c, sfin_c = gdn_forward(kc, kc, vc, bgc, bgtc, zero_state, False, "gdn_fwd_ctx")
    o_f, o_b, saved, _ = gdn_forward(qn, kn, vv, bg, bgt, sfin_c, True, "gdn_fwd")
    mix0 = jnp.concatenate([gated_rmsnorm(o_f, o_b, p0, nw_row, "gated_rmsnorm"),
                            pool_mix(p0, pool_w, ps_row, "pool_mix")], 1)
    sent, landed, _ = exchange_wait(gather_l0, mix0, "gather_l0_wait")
    full = place_own(landed, sent, False, me)
    wout_e, wup, wdown = _unshard(full[0], 0), [_unshard(full[1], 1)], [_unshard(full[2], 0)]
    y0 = matmul(mix0, wout_e, "nn", f32, "even_out")
    x1, u1 = res_layernorm(x, y0, gt_m, lng[0][0], lnb[0][0], "resln_l0_mix", sc_f, sh_f)
    h0 = matmul(u1, wup[0], "nn", f32, "ffn_up_l0")
    f0 = ffn_conv(h0, fw16[0], "ffn_conv_l0")
    y0f = matmul(f0, wdown[0], "nn", f32, "ffn_down_l0")
    sh_m1, sc_m1, gt_m1, sh_f1, sc_f1, gt_f1 = mods[1]
    x2, u2 = res_layernorm(x1, y0f, gt_f, lng[0][1], lnb[0][1], "resln_l0_ffn", sc_m1, sh_m1)

    sent, landed, _ = exchange_wait(gather_l1, x2, "gather_l1_wait")
    full = place_own(landed, sent, False, me)
    win_o, wout_o = _unshard(full[0], 1), _unshard(full[1], 0)
    wup.append(_unshard(full[2], 1))
    wdown.append(_unshard(full[3], 0))
    p1 = matmul(u2, win_o, "nn", f32, "odd_in")
    zc = conf_conv(p1, cw32, "conf_conv")
    mix1 = jnp.concatenate([short_conv(p1, sw8, "short_conv"), ln_silu(zc, cg_row, cb_row, "conf_ln_silu")], 1)
    y1 = matmul(mix1, wout_o, "nn", f32, "odd_out")
    x3, u3 = res_layernorm(x2, y1, gt_m1, lng[1][0], lnb[1][0], "resln_l1_mix", sc_f1, sh_f1)
    h1 = matmul(u3, wup[1], "nn", f32, "ffn_up_l1")
    f1 = ffn_conv(h1, fw16[1], "ffn_conv_l1")
    y1f = matmul(f1, wdown[1], "nn", f32, "ffn_down_l1")
    loss_row, dxr, dy, dgt_f1, dlg, dlb = res_layernorm_loss(x3, y1f, gt_f1, lng[1][1], lnb[1][1], target, "resln_l1_ffn_loss")
    loss = lax.psum(loss_row[0, 0], ("x", "y", "c"))

    def ffn_backward(dy, u, h, f, l):
        df = matmul(dy, wdown[l], "nt", f32, f"ffn_down_dgrad_l{l}")
        g_down = matmul(f, dy, "tn", bf16, f"ffn_down_wgrad_l{l}")
        dh, dcw = ffn_conv_bwd(h, fw16[l], df, f"ffn_conv_bwd_l{l}")
        du = matmul(dh, wup[l], "nt", f32, f"ffn_up_dgrad_l{l}")
        g_up = matmul(u, dh, "tn", bf16, f"ffn_up_wgrad_l{l}")
        return du, dcw, g_up, g_down

    dln_f1 = (dlg, dlb)
    du, dfcw1, g_up1, g_down1 = ffn_backward(dy, u3, h1, f1, 1)

    scatter_a, token = exchange_start([_shard_major(g_up1, 1), _shard_major(g_down1, 0)], True, "scatter_l1_ffn_start")
    gt_m1 = gt_m1 + token[0:1, 0:1]

    dxr, dy, dsc, dsh, dgt, dlg, dlb = modulate_res_layernorm_bwd(
        du, sc_f1, dxr, x2, y1, gt_m1, lng[1][0], lnb[1][0], "mod_resln_bwd_l1_mix")
    dmod_f1 = (dsh, dsc, dgt_f1)
    dln_m1 = (dlg, dlb)
    dmix = matmul(dy, wout_o, "nt", f32, "odd_out_dgrad")
    g_wout_o = matmul(mix1, dy, "tn", bf16, "odd_out_wgrad")
    dgb, dgc, dhh, d_sconv = short_conv_bwd(p1, sw8, dmix, "short_conv_bwd")
    dzc, d_cg, d_cb = ln_silu_bwd(zc, cg_row, cb_row, dmix, "conf_ln_silu_bwd")
    dga, dgbb, d_cconv = conf_conv_bwd(p1, cw32, dzc, "conf_conv_bwd")
    dp1 = jnp.concatenate([dgb, dgc, dhh, dga, dgbb], 1)
    du = matmul(dp1, win_o, "nt", f32, "odd_in_dgrad")
    g_win_o = matmul(u2, dp1, "tn", bf16, "odd_in_wgrad")
    dgt_m1 = dgt
    dxr, dy, dsc, dsh, dgt_f0, dlg, dlb = modulate_res_layernorm_bwd(
        du, sc_m1, dxr, x1, y0f, gt_f, lng[0][1], lnb[0][1], "mod_resln_bwd_l0_ffn")
    dmod_m1 = (dsh, dsc, dgt_m1)
    dln_f0 = (dlg, dlb)
    du, dfcw0, g_up0, g_down0 = ffn_backward(dy, u1, h0, f0, 0)

    dxr, dy, dsc, dsh, dgt, dlg, dlb = modulate_res_layernorm_bwd(
        du, sc_f, dxr, x, y0, gt_m, lng[0][0], lnb[0][0], "mod_resln_bwd_l0_mix")
    dmod_f0 = (dsh, dsc, dgt_f0)
    dln_m0 = (dlg, dlb)
    dmix = matmul(dy, wout_e, "nt", f32, "even_out_dgrad")
    g_wout_e = matmul(mix0, dy, "tn", bf16, "even_out_wgrad")
    scatter_b, token = exchange_start(
        [_shard_major(g_win_o, 1), _shard_major(g_wout_o, 0), _shard_major(g_up0, 1), _shard_major(g_down0, 0),
         _shard_major(g_wout_e, 0)], True, "scatter_mid_start")
    d_o, dgate, d_nw = gated_rmsnorm_bwd(o_f, o_b, p0, nw_row + token[0:1, 0:1], dmix, "gated_rmsnorm_bwd")
    dpool, d_pw, d_ps = pool_mix_bwd(p0, pool_w, ps_row, dmix, "pool_mix_bwd")
    small_early = [d_nw, d_pw, d_ps, d_sconv[:3], d_cconv[:31], d_cg, d_cb, jnp.stack([dfcw0[:9], dfcw1[:9]])]
    epack, eoffs = _pack(small_early)
    gather_early, token = exchange_start([epack], False, "gather_small_early_start")
    dq_f, dq_b, dk_f, dk_b, dv_f, dv_b, dbg_f, dbg_b, ds0 = gdn_backward(
        qn, kn, vv, bg, bgt, saved, d_o, zero_state + token[0, 0], True, "gdn_bwd")
    _, _, dkc_f, dkc_b, dvc_f, dvc_b, dbgc_f, dbgc_b, _ = gdn_backward(
        kc, kc, vc, bgc, bgtc, saved_c, jnp.zeros((tc, 512), f32), ds0, False, "gdn_bwd_ctx")
    dqp, dwq = gdn_conv_bwd(p0, gw8, dq_f, dq_b, 0, 4, q_scale, "gdn_conv_q_bwd")
    dkp, dwk = gdn_conv_bwd(p0, gw8, dk_f, dk_b, 4, 4, 1.0, "gdn_conv_k_bwd")
    dvp, dwv = gdn_conv_bwd(p0, gw8, dv_f, dv_b, 8, 4, None, "gdn_conv_v_bwd")
    dkcp, dwkc = gdn_conv_bwd(pc, gw8, dkc_f, dkc_b, 4, 4, 1.0, "gdn_conv_k_ctx_bwd")
    dvcp, dwvc = gdn_conv_bwd(pc, gw8, dvc_f, dvc_b, 8, 4, None, "gdn_conv_v_ctx_bwd")
    ds_l, da_l, ddt_l = gdn_gates_bwd(p0, neg_a, dt_row, dbg_f, dbg_b, "gdn_gates_bwd")
    ds_c, da_c, ddt_c = gdn_gates_bwd(pc, neg_a, dt_row, dbgc_f, dbgc_b, "gdn_gates_ctx_bwd")
    zc512 = jnp.zeros((tc, 512), bf16)
    dp0 = jnp.concatenate([dqp, dkp, dvp, dgate, dpool, ds_l], 1)
    dpc = jnp.concatenate([zc512, dkcp, dvcp, zc512, zc512, ds_c], 1)
    du0 = matmul(dp0, win_e, "nt", f32, "even_in_dgrad")
    duc = matmul(dpc, win_e, "nt", f32, "even_in_ctx_dgrad")
    g_win_e = matmul(u0, dp0, "tn", bf16, "even_in_wgrad", init=matmul(cu, dpc, "tn", f32, "even_in_ctx_wgrad"))[:, :e_in]
    scatter_c, token = exchange_start([_shard_major(g_win_e, 1)], True, "scatter_last_start")
    grad_x, dsc, dsh = modulate_bwd(du0, x, sc_m + token[0:1, 0:1], dxr, "mod_bwd_l0_mix")
    dmod_m0 = (dsh, dsc, dgt)
    _, dsc_c, dsh_c = modulate_bwd(duc, ctx, sc_c, jnp.zeros((tc, d), f32), "mod_bwd_ctx")

    grads, delta, new_m, new_v = {}, {}, {}, {}

    def update(n, parts, w, m, v):
        cols = w.shape[-1]
        out = adamw(parts.reshape(parts.shape[0], -1, cols), w.reshape(-1, cols), m.reshape(-1, cols), v.reshape(-1, cols), f"adamw_{n}")
        return [a.reshape(w.shape) for a in out]

    sent, landed, _ = exchange_wait(scatter_a, grad_x, "scatter_l1_ffn_wait")
    recv_a = place_own(landed, sent, True, me)
    sent, landed, _ = exchange_wait(scatter_b, grad_x, "scatter_mid_wait")
    recv_b = place_own(landed, sent, True, me)
    for n, parts in (("odd_w_in", recv_b[0]), ("odd_w_out", recv_b[1])):
        grads[n], delta[n], new_m[n], new_v[n] = update(n, parts, weights[n], mom1[n], mom2[n])
    for n, per_layer in (("ffn_w_up", (recv_b[2], recv_a[0])), ("ffn_w_down", (recv_b[3], recv_a[1]))):
        outs = [update(f"{n}_l{l}", per_layer[l], weights[n][l], mom1[n][l], mom2[n][l]) for l in range(DEPTH)]
        grads[n], delta[n], new_m[n], new_v[n] = (jnp.stack([outs[l][j] for l in range(DEPTH)]) for j in range(4))
    sent, landed, token = exchange_wait(scatter_c, new_v["ffn_w_down"], "scatter_last_wait")
    recv_c = place_own(landed, sent, True, me)
    for n, parts in (("even_w_in", recv_c[0]), ("even_w_out", recv_b[4])):
        grads[n], delta[n], new_m[n], new_v[n] = update(n, parts, weights[n], mom1[n], mom2[n])

    dmod0 = jnp.concatenate(dmod_m0 + dmod_f0, 1)
    dmod1 = jnp.concatenate(dmod_m1 + dmod_f1, 1)
    dmodc = jnp.concatenate([dsh_c, dsc_c], 1)
    d_gconv = jnp.concatenate([dwq, dwk + dwkc, dwv + dwvc], 1)[:5]
    small_late = [dmod0, dmod1, dmodc,
                  jnp.concatenate([dln_m0[0], dln_f0[0], dln_m1[0], dln_f1[0]], 0),
                  jnp.concatenate([dln_m0[1], dln_f0[1], dln_m1[1], dln_f1[1]], 0),
                  d_gconv, (da_l + da_c)[0, 8:16], (ddt_l + ddt_c)[0, 8:16]]
    gpack, goffs = _pack(small_late)
    gparts = exchange([gpack + token[0:1]], False, "gather_small_grads")[0]
    sent, landed, _ = exchange_wait(gather_early, gparts, "gather_small_early_wait")
    eparts = place_own(landed, sent, False, me)[0]
    gsum = sum_parts(gparts, "sum_small_grads").reshape(-1)
    esum = sum_parts(eparts, "sum_small_early").reshape(-1)
    gs = ([gsum[o:o + a.size].reshape(a.shape) for a, o in zip(small_late, goffs)]
          + [esum[o:o + a.size].reshape(a.shape) for a, o in zip(small_early, eoffs)])
    gflat = gparts.reshape(N_DEV, -1)
    dmodc_cols = _my_block(jnp.pad(gs[2], ((0, 0), (0, 4 * d))), 1, me)
    dm = jnp.stack([
        jnp.concatenate([_my_block(gflat[:, goffs[0]:goffs[0] + 6 * d], 1, me), dmodc_cols, jnp.zeros((7, ncol), f32)], 0),
        jnp.concatenate([_my_block(gflat[:, goffs[1]:goffs[1] + 6 * d], 1, me), jnp.zeros((8, ncol), f32)], 0)])
    g_ada_w, dcc = ada_backward(a_raw, ada_w, dm, "ada_backward")
    g_cctx = cctx_grad(exchange([dcc], False, "gather_cctx")[0], c_ctx[None], "cctx_grad")

    grads["c_ctx"] = g_cctx.reshape(c_ctx.shape)
    grads["ada_b"] = jnp.concatenate([gs[0] + jnp.pad(gs[2], ((0, 0), (0, 4 * d))), gs[1]], 0)
    grads["ln_g"] = _my_block(gs[3].reshape(DEPTH, 2, d), 2, me)
    grads["ln_b"] = _my_block(gs[4].reshape(DEPTH, 2, d), 2, me)
    grads["gdn_conv_w"] = _my_block(gs[5], 1, me)
    grads["gdn_a_log"] = gs[6].reshape(2, GDN_HEADS)
    grads["gdn_dt_bias"] = gs[7].reshape(2, GDN_HEADS)
    grads["gdn_norm_w"] = gs[8].reshape(LANE)
    grads["pool_w"] = gs[9]
    grads["pool_scale"] = gs[10].reshape(-1)
    grads["sconv_w"] = _my_block(gs[11], 1, me)
    grads["conf_conv_w"] = _my_block(gs[12], 1, me)
    grads["conf_ln_g"] = gs[13].reshape(-1)
    grads["conf_ln_b"] = gs[14].reshape(-1)
    grads["ffn_conv_w"] = _my_block(gs[15].reshape(DEPTH, 3, 3, D_FF), 3, me)

    def as2d(a):
        return a.reshape(-1, a.shape[-1]) if a.ndim > 1 else a.reshape(1, -1)

    small_names = [n for n in order if n in grads and n not in delta]
    res = adamw_small([(as2d(grads[n]), as2d(weights[n]), as2d(mom1[n]), as2d(mom2[n])) for n in small_names], "adamw_small")
    for n, (dl, nm, nv) in zip(small_names, res):
        delta[n], new_m[n], new_v[n] = (a.reshape(weights[n].shape) for a in (dl, nm, nv))
    grads["ada_w"], delta["ada_w"], new_m["ada_w"], new_v["ada_w"] = update("ada_w", g_ada_w[None], ada_w, m_ada_w, v_ada_w)

    return (loss, grad_x[None], *[grads[n] for n in order], *[delta[n] for n in order],
            *[new_m[n] for n in order], *[new_v[n] for n in order])
```

```python
import functools
import math

import jax
import jax.numpy as jnp
from jax import lax
from jax.experimental import pallas as pl
from jax.experimental.pallas import tpu as pltpu

f32 = jnp.float32
bf16 = jnp.bfloat16
SDS = jax.ShapeDtypeStruct

N_DEV = 8
D_MODEL = 1024
DEPTH = 2
GRID_W = 64
GDN_HEADS = 4
GDN_DK = 128
CHUNK = 64
POOL_WINDOWS = (2, 4, 8, 16)
D_FF = 2816
ALPHA = (2 * DEPTH) ** 0.25
LN_EPS = 1e-5
RMS_EPS = 1e-6
LANE = 128
PAD_ROWS = 72
CONV_ROWS = 256
VMEM_LIMIT = 56 * 2**20

ADAM_LR, ADAM_B1, ADAM_B2, ADAM_EPS, ADAM_WD, ADAM_STEP = 0.001, 0.9, 0.999, 1e-08, 0.01, 10

HI = lax.Precision.HIGHEST


def _cparams(sem=None):
    return pltpu.CompilerParams(dimension_semantics=sem, vmem_limit_bytes=VMEM_LIMIT)


def _silu(x):
    return x * jax.nn.sigmoid(x)


def _dsilu(x):
    s = jax.nn.sigmoid(x)
    return s * (1.0 + x * (1.0 - s))


def _dotb(a, b, dims=(((1,), (0,)), ((), ()))):
    return lax.dot_general(a.astype(bf16), b.astype(bf16), dims, preferred_element_type=f32)


def _dotb_nt(a, b):
    return _dotb(a, b, (((1,), (1,)), ((), ())))


def _dotb_tn(a, b):
    return _dotb(a, b, (((0,), (0,)), ((), ())))


def _dotf(a, b, dims=(((1,), (0,)), ((), ()))):
    return lax.dot_general(a, b, dims, preferred_element_type=f32, precision=HI)


def _pick(n, cands):
    for c in cands:
        if n % c == 0:
            return c
    return n


def matmul(a, b, mode, out_dtype, name, init=None):
    if mode == "nn":
        (M, K), N = a.shape, b.shape[1]
    elif mode == "nt":
        (M, K), N = a.shape, b.shape[0]
    else:
        (K, M), N = a.shape, b.shape[1]
    tm = _pick(M, (1024, 768, 512, 256, 128)) if mode != "tn" else _pick(M, (1024, 1408, 512, 256, 128))
    tn = _pick(N, (1024, 1408, 896, 768, 640, 512, 384, 256, 128))
    tk = _pick(K, (1024, 1408, 896, 768, 640, 512, 384, 256, 128)) if mode != "tn" else _pick(K, (1024, 512, 256))
    nk = K // tk
    dims = {"nn": (((1,), (0,)), ((), ())), "nt": (((1,), (1,)), ((), ())), "tn": (((0,), (0,)), ((), ()))}[mode]

    def body(a_ref, b_ref, *rest):
        o_ref, acc_ref = rest[-2:]
        k = pl.program_id(2)
        part = lax.dot_general(a_ref[...].astype(bf16), b_ref[...].astype(bf16), dims, preferred_element_type=f32)

        @pl.when(k == 0)
        def _():
            acc_ref[...] = part if init is None else part + rest[0][...]

        @pl.when(k > 0)
        def _():
            acc_ref[...] += part

        @pl.when(k == nk - 1)
        def _():
            o_ref[...] = acc_ref[...].astype(out_dtype)

    a_spec = {"nn": pl.BlockSpec((tm, tk), lambda i, j, k: (i, k)),
              "nt": pl.BlockSpec((tm, tk), lambda i, j, k: (i, k)),
              "tn": pl.BlockSpec((tk, tm), lambda i, j, k: (k, i))}[mode]
    b_spec = {"nn": pl.BlockSpec((tk, tn), lambda i, j, k: (k, j)),
              "nt": pl.BlockSpec((tn, tk), lambda i, j, k: (j, k)),
              "tn": pl.BlockSpec((tk, tn), lambda i, j, k: (k, j))}[mode]
    o_spec = pl.BlockSpec((tm, tn), lambda i, j, k: (i, j))
    return pl.pallas_call(
        body, out_shape=SDS((M, N), out_dtype), grid=(M // tm, N // tn, nk),
        in_specs=[a_spec, b_spec] + ([] if init is None else [o_spec]), out_specs=o_spec,
        scratch_shapes=[pltpu.VMEM((tm, tn), f32)], name=name,
        compiler_params=_cparams(("parallel", "parallel", "arbitrary")),
    )(*((a, b) if init is None else (a, b, init)))


def _row_tile(t):
    return _pick(t, (512, 256, 128, 64, 32, 16, 8))


def _row_spec(tt, d):
    return pl.BlockSpec((tt, d), lambda i: (i, 0))


def _vec_spec(d):
    return pl.BlockSpec((1, d), lambda i: (0, 0))


def _acc_rows(ref, val):
    @pl.when(pl.program_id(0) == 0)
    def _():
        ref[...] = val

    @pl.when(pl.program_id(0) > 0)
    def _():
        ref[...] += val


def modulate(x, scale, shift, name):
    t, d = x.shape
    tt = _row_tile(t)

    def body(x_ref, sc_ref, sh_ref, o_ref):
        o_ref[...] = (x_ref[...] * (1.0 + sc_ref[...]) + sh_ref[...]).astype(bf16)

    return pl.pallas_call(
        body, out_shape=SDS((t, d), bf16), grid=(t // tt,),
        in_specs=[_row_spec(tt, d), _vec_spec(d), _vec_spec(d)], out_specs=_row_spec(tt, d),
        name=name, compiler_params=_cparams(("parallel",)),
    )(x, scale, shift)


def modulate_bwd(du, x, scale, dres, name, du_row0=0):
    t, d = x.shape
    tt = _row_tile(t)
    blk0 = du_row0 // tt

    def body(du_ref, x_ref, sc_ref, dres_ref, dx_ref, dsc_ref, dsh_ref):
        du_v = du_ref[...]
        dx_ref[...] = du_v * (1.0 + sc_ref[...]) + dres_ref[...]
        _acc_rows(dsc_ref, jnp.sum(du_v * x_ref[...], axis=0, keepdims=True))
        _acc_rows(dsh_ref, jnp.sum(du_v, axis=0, keepdims=True))

    return pl.pallas_call(
        body, out_shape=(SDS((t, d), f32), SDS((1, d), f32), SDS((1, d), f32)), grid=(t // tt,),
        in_specs=[pl.BlockSpec((tt, d), lambda i: (i + blk0, 0)), _row_spec(tt, d), _vec_spec(d), _row_spec(tt, d)],
        out_specs=(_row_spec(tt, d), _vec_spec(d), _vec_spec(d)),
        name=name, compiler_params=_cparams(("arbitrary",)),
    )(du, x, scale, dres)


def _ln_stats(z):
    mu = jnp.mean(z, axis=-1, keepdims=True)
    zc = z - mu
    var = jnp.mean(zc * zc, axis=-1, keepdims=True)
    rstd = lax.rsqrt(var + LN_EPS)
    return zc * rstd, rstd


def _ln_bwd(dxhat, xhat, rstd):
    m1 = jnp.mean(dxhat, axis=-1, keepdims=True)
    m2 = jnp.mean(dxhat * xhat, axis=-1, keepdims=True)
    return rstd * (dxhat - m1 - xhat * m2)


def res_layernorm(x, y, gate, g, b, name, scale=None, shift=None):
    t, d = x.shape
    tt = _row_tile(t)
    with_mod = scale is not None

    def body(x_ref, y_ref, gt_ref, g_ref, b_ref, *rest):
        xhat, _ = _ln_stats(ALPHA * x_ref[...] + gt_ref[...] * y_ref[...])
        out = xhat * g_ref[...] + b_ref[...]
        if with_mod:
            sc_ref, sh_ref, o_ref, u_ref = rest
            u_ref[...] = (out * (1.0 + sc_ref[...]) + sh_ref[...]).astype(bf16)
        else:
            o_ref, = rest
        o_ref[...] = out

    rows, vec = _row_spec(tt, d), _vec_spec(d)
    return pl.pallas_call(
        body, out_shape=(SDS((t, d), f32), SDS((t, d), bf16)) if with_mod else SDS((t, d), f32), grid=(t // tt,),
        in_specs=[rows, rows, vec, vec, vec] + ([vec, vec] if with_mod else []),
        out_specs=(rows, rows) if with_mod else rows, name=name, compiler_params=_cparams(("parallel",)),
    )(*((x, y, gate, g, b) + ((scale, shift) if with_mod else ())))


def modulate_res_layernorm_bwd(du, scale, dres, x, y, gate, g, b, name):
    t, d = x.shape
    tt = _row_tile(t)

    def body(du_ref, sc_ref, dres_ref, x_ref, y_ref, gt_ref, g_ref, b_ref,
             dxr_ref, dy_ref, dsc_ref, dsh_ref, dgt_ref, dg_ref, db_ref):
        y_v, du_v = y_ref[...], du_ref[...]
        xhat, rstd = _ln_stats(ALPHA * x_ref[...] + gt_ref[...] * y_v)
        do_v = du_v * (1.0 + sc_ref[...]) + dres_ref[...]
        dz = _ln_bwd(do_v * g_ref[...], xhat, rstd)
        dxr_ref[...] = ALPHA * dz
        dy_ref[...] = (gt_ref[...] * dz).astype(bf16)
        _acc_rows(dsc_ref, jnp.sum(du_v * (xhat * g_ref[...] + b_ref[...]), axis=0, keepdims=True))
        _acc_rows(dsh_ref, jnp.sum(du_v, axis=0, keepdims=True))
        _acc_rows(dgt_ref, jnp.sum(dz * y_v, axis=0, keepdims=True))
        _acc_rows(dg_ref, jnp.sum(do_v * xhat, axis=0, keepdims=True))
        _acc_rows(db_ref, jnp.sum(do_v, axis=0, keepdims=True))

    rows, vec, vshape = _row_spec(tt, d), _vec_spec(d), SDS((1, d), f32)
    return pl.pallas_call(
        body, out_shape=(SDS((t, d), f32), SDS((t, d), bf16)) + (vshape,) * 5, grid=(t // tt,),
        in_specs=[rows, vec, rows, rows, rows, vec, vec, vec], out_specs=(rows, rows) + (vec,) * 5,
        name=name, compiler_params=_cparams(("arbitrary",)),
    )(du, scale, dres, x, y, gate, g, b)


def res_layernorm_loss(x, y, gate, g, b, target, name):
    t, d = x.shape
    tt = _row_tile(t)

    def body(x_ref, y_ref, gt_ref, g_ref, b_ref, t_ref, l_ref, dxr_ref, dy_ref, dgt_ref, dg_ref, db_ref):
        y_v = y_ref[...]
        xhat, rstd = _ln_stats(ALPHA * x_ref[...] + gt_ref[...] * y_v)
        e = xhat * g_ref[...] + b_ref[...] - t_ref[...]
        part = jnp.sum(jnp.sum(e * e, axis=1, keepdims=True), axis=0, keepdims=True) * (0.5 / d)
        _acc_rows(l_ref, jnp.broadcast_to(part, (1, LANE)))
        do_v = e * (1.0 / d)
        dz = _ln_bwd(do_v * g_ref[...], xhat, rstd)
        dxr_ref[...] = ALPHA * dz
        dy_ref[...] = (gt_ref[...] * dz).astype(bf16)
        _acc_rows(dgt_ref, jnp.sum(dz * y_v, axis=0, keepdims=True))
        _acc_rows(dg_ref, jnp.sum(do_v * xhat, axis=0, keepdims=True))
        _acc_rows(db_ref, jnp.sum(do_v, axis=0, keepdims=True))

    rows, vec, vshape = _row_spec(tt, d), _vec_spec(d), SDS((1, d), f32)
    return pl.pallas_call(
        body, out_shape=(SDS((1, LANE), f32), SDS((t, d), f32), SDS((t, d), bf16)) + (vshape,) * 3, grid=(t // tt,),
        in_specs=[rows, rows, vec, vec, vec, rows],
        out_specs=(pl.BlockSpec((1, LANE), lambda i: (0, 0)), rows, rows) + (vec,) * 3,
        name=name, compiler_params=_cparams(("arbitrary",)),
    )(x, y, gate, g, b, target)


def _fill_pad(pad_ref, val, t):
    zeros = jnp.zeros((PAD_ROWS, LANE), f32)
    pad_ref[0:PAD_ROWS, :] = zeros
    pad_ref[PAD_ROWS + t:2 * PAD_ROWS + t, :] = zeros
    pad_ref[PAD_ROWS:PAD_ROWS + t, :] = val


def _grid_pads_set(pads, r0, val):
    rows = val.shape[0]
    col = (lax.broadcasted_iota(jnp.int32, (rows, 1), 0) + r0) % GRID_W
    base = PAD_ROWS + r0
    pads[0][base + 1:base + 1 + rows, :] = val * (col <= GRID_W - 2).astype(f32)
    pads[1][base:base + rows, :] = val
    pads[2][base - 1:base - 1 + rows, :] = val * (col >= 1).astype(f32)


def _grid_pads_clear_edges(pads, t):
    zeros = jnp.zeros((PAD_ROWS + 8, LANE), f32)
    for p in pads:
        p[0:PAD_ROWS + 8, :] = zeros
        p[PAD_ROWS + t - 8:2 * PAD_ROWS + t, :] = zeros


def _tap_source(pads, dc):
    return pads if dc is None else pads[dc + 1]


def _taps_apply(pads, w_ref, taps, r0, rows):
    acc = jnp.zeros((rows, LANE), f32)
    for off, dc, wi in taps:
        xs = _tap_source(pads, dc)[PAD_ROWS + r0 + off:PAD_ROWS + r0 + off + rows, :]
        acc = acc + w_ref[wi:wi + 1, :] * xs
    return acc


def _taps_wgrad(pads, dy, taps, r0, rows, nw):
    out = jnp.zeros((nw, LANE), f32)
    rid = lax.broadcasted_iota(jnp.int32, (nw, 1), 0)
    for off, dc, wi in taps:
        xs = _tap_source(pads, dc)[PAD_ROWS + r0 + off:PAD_ROWS + r0 + off + rows, :]
        s = jnp.sum(dy * xs, axis=0, keepdims=True)
        out = out + jnp.where(rid == wi, s, 0.0)
    return out


def _transpose_taps(taps):
    return [(-off, None if dc is None else -dc, wi) for off, dc, wi in taps]


def _taps_1d(width):
    return [(j - width // 2, None, j) for j in range(width)]


def _taps_grid3():
    return [(GRID_W * dr, dc, 3 * (dr + 1) + (dc + 1)) for dr in (-1, 0, 1) for dc in (-1, 0, 1)]


def _row_chunks(t):
    r = min(CONV_ROWS, t)
    return [(i * r, r) for i in range(t // r)]


def _col_spec(t, off):
    return pl.BlockSpec((t, LANE), lambda c: (0, c + off))


def _w_spec(nw, off=0):
    return pl.BlockSpec((nw, LANE), lambda c: (0, c + off))


def gdn_conv(p, w, col0, nblk, norm_scale, name):
    t = p.shape[0]
    nw = w.shape[0]
    taps = _taps_1d(5)

    def body(p_ref, w_ref, o_ref, pad_ref):
        _fill_pad(pad_ref, p_ref[...], t)
        for r0, rows in _row_chunks(t):
            a = _silu(_taps_apply(pad_ref, w_ref, taps, r0, rows))
            if norm_scale is not None:
                a = a * (lax.rsqrt(jnp.sum(a * a, axis=-1, keepdims=True) + RMS_EPS) * norm_scale)
            o_ref[r0:r0 + rows, :] = a

    return pl.pallas_call(
        body, out_shape=SDS((t, nblk * LANE), f32), grid=(nblk,),
        in_specs=[_col_spec(t, col0), _w_spec(nw, col0)], out_specs=_col_spec(t, 0),
        scratch_shapes=[pltpu.VMEM((t + 2 * PAD_ROWS, LANE), f32)], name=name,
        compiler_params=_cparams(("parallel",)),
    )(p, w)


def gdn_conv_bwd(p, w, d_a, d_b, col0, nblk, norm_scale, name):
    t = p.shape[0]
    nw = w.shape[0]
    taps = _taps_1d(5)
    ttaps = _transpose_taps(taps)

    def body(p_ref, w_ref, da_ref, db_ref, dp_ref, dw_ref, pad_ref, gpad_ref):
        _fill_pad(pad_ref, p_ref[...], t)
        for r0, rows in _row_chunks(t):
            pre = _taps_apply(pad_ref, w_ref, taps, r0, rows)
            a = _silu(pre)
            dy = da_ref[r0:r0 + rows, :] + db_ref[r0:r0 + rows, :]
            if norm_scale is not None:
                r = lax.rsqrt(jnp.sum(a * a, axis=-1, keepdims=True) + RMS_EPS)
                da = norm_scale * (dy * r - a * (r * r * r) * jnp.sum(dy * a, axis=-1, keepdims=True))
            else:
                da = dy
            gpad_ref[PAD_ROWS + r0:PAD_ROWS + r0 + rows, :] = da * _dsilu(pre)
        zeros = jnp.zeros((PAD_ROWS, LANE), f32)
        gpad_ref[0:PAD_ROWS, :] = zeros
        gpad_ref[PAD_ROWS + t:2 * PAD_ROWS + t, :] = zeros
        dw = jnp.zeros((nw, LANE), f32)
        for r0, rows in _row_chunks(t):
            dp_ref[r0:r0 + rows, :] = _taps_apply(gpad_ref, w_ref, ttaps, r0, rows).astype(bf16)
            dw = dw + _taps_wgrad(pad_ref, gpad_ref[PAD_ROWS + r0:PAD_ROWS + r0 + rows, :], taps, r0, rows, nw)
        dw_ref[...] = dw

    return pl.pallas_call(
        body, out_shape=(SDS((t, nblk * LANE), bf16), SDS((nw, nblk * LANE), f32)), grid=(nblk,),
        in_specs=[_col_spec(t, col0), _w_spec(nw, col0), _col_spec(t, 0), _col_spec(t, 0)],
        out_specs=(_col_spec(t, 0), _w_spec(nw)),
        scratch_shapes=[pltpu.VMEM((t + 2 * PAD_ROWS, LANE), f32)] * 2, name=name,
        compiler_params=_cparams(("parallel",)),
    )(p, w, d_a, d_b)


def short_conv(p, w, name):
    t = p.shape[0]
    nw = w.shape[0]
    taps = _taps_1d(3)

    def body(gb_ref, gc_ref, h_ref, w_ref, o_ref, pad_ref):
        _fill_pad(pad_ref, gc_ref[...] * h_ref[...], t)
        for r0, rows in _row_chunks(t):
            o_ref[r0:r0 + rows, :] = (gb_ref[r0:r0 + rows, :] * _taps_apply(pad_ref, w_ref, taps, r0, rows)).astype(bf16)

    return pl.pallas_call(
        body, out_shape=SDS((t, 4 * LANE), bf16), grid=(4,),
        in_specs=[_col_spec(t, 0), _col_spec(t, 4), _col_spec(t, 8), _w_spec(nw)], out_specs=_col_spec(t, 0),
        scratch_shapes=[pltpu.VMEM((t + 2 * PAD_ROWS, LANE), f32)], name=name,
        compiler_params=_cparams(("parallel",)),
    )(p, p, p, w)


def short_conv_bwd(p, w, dy, name):
    t = p.shape[0]
    nw = w.shape[0]
    taps = _taps_1d(3)
    ttaps = _transpose_taps(taps)

    def body(gb_ref, gc_ref, h_ref, w_ref, dy_ref, dgb_ref, dgc_ref, dh_ref, dw_ref, pad_ref, gpad_ref):
        _fill_pad(pad_ref, gc_ref[...] * h_ref[...], t)
        _fill_pad(gpad_ref, dy_ref[...] * gb_ref[...], t)
        dw = jnp.zeros((nw, LANE), f32)
        for r0, rows in _row_chunks(t):
            sl = slice(r0, r0 + rows)
            dgb_ref[sl, :] = (dy_ref[sl, :] * _taps_apply(pad_ref, w_ref, taps, r0, rows)).astype(bf16)
            dm = _taps_apply(gpad_ref, w_ref, ttaps, r0, rows)
            dgc_ref[sl, :] = (dm * h_ref[sl, :]).astype(bf16)
            dh_ref[sl, :] = (dm * gc_ref[sl, :]).astype(bf16)
            dw = dw + _taps_wgrad(pad_ref, gpad_ref[PAD_ROWS + r0:PAD_ROWS + r0 + rows, :], taps, r0, rows, nw)
        dw_ref[...] = dw

    blk = SDS((t, 4 * LANE), bf16)
    return pl.pallas_call(
        body, out_shape=(blk, blk, blk, SDS((nw, 4 * LANE), f32)), grid=(4,),
        in_specs=[_col_spec(t, 0), _col_spec(t, 4), _col_spec(t, 8), _w_spec(nw), _col_spec(t, 0)],
        out_specs=(_col_spec(t, 0), _col_spec(t, 0), _col_spec(t, 0), _w_spec(nw)),
        scratch_shapes=[pltpu.VMEM((t + 2 * PAD_ROWS, LANE), f32)] * 2, name=name,
        compiler_params=_cparams(("parallel",)),
    )(p, p, p, w, dy)


def conf_conv(p, w, name):
    t = p.shape[0]
    nw = w.shape[0]
    taps = _taps_1d(31)

    def body(a_ref, b_ref, w_ref, o_ref, pad_ref):
        _fill_pad(pad_ref, a_ref[...] * jax.nn.sigmoid(b_ref[...]), t)
        for r0, rows in _row_chunks(t):
            o_ref[r0:r0 + rows, :] = _taps_apply(pad_ref, w_ref, taps, r0, rows)

    return pl.pallas_call(
        body, out_shape=SDS((t, 4 * LANE), f32), grid=(4,),
        in_specs=[_col_spec(t, 12), _col_spec(t, 16), _w_spec(nw)], out_specs=_col_spec(t, 0),
        scratch_shapes=[pltpu.VMEM((t + 2 * PAD_ROWS, LANE), f32)], name=name,
        compiler_params=_cparams(("parallel",)),
    )(p, p, w)


def conf_conv_bwd(p, w, dz, name):
    t = p.shape[0]
    nw = w.shape[0]
    taps = _taps_1d(31)
    ttaps = _transpose_taps(taps)

    def body(a_ref, b_ref, w_ref, dz_ref, da_ref, db_ref, dw_ref, pad_ref, gpad_ref):
        _fill_pad(pad_ref, a_ref[...] * jax.nn.sigmoid(b_ref[...]), t)
        _fill_pad(gpad_ref, dz_ref[...], t)
        dw = jnp.zeros((nw, LANE), f32)
        for r0, rows in _row_chunks(t):
            sl = slice(r0, r0 + rows)
            dm = _taps_apply(gpad_ref, w_ref, ttaps, r0, rows)
            sg = jax.nn.sigmoid(b_ref[sl, :])
            da_ref[sl, :] = (dm * sg).astype(bf16)
            db_ref[sl, :] = (dm * a_ref[sl, :] * sg * (1.0 - sg)).astype(bf16)
            dw = dw + _taps_wgrad(pad_ref, dz_ref[sl, :], taps, r0, rows, nw)
        dw_ref[...] = dw

    blk = SDS((t, 4 * LANE), bf16)
    return pl.pallas_call(
        body, out_shape=(blk, blk, SDS((nw, 4 * LANE), f32)), grid=(4,),
        in_specs=[_col_spec(t, 12), _col_spec(t, 16), _w_spec(nw), _col_spec(t, 0)],
        out_specs=(_col_spec(t, 0), _col_spec(t, 0), _w_spec(nw)),
        scratch_shapes=[pltpu.VMEM((t + 2 * PAD_ROWS, LANE), f32)] * 2, name=name,
        compiler_params=_cparams(("parallel",)),
    )(p, p, w, dz)


def ffn_conv(h, w, name):
    t = h.shape[0]
    width = 2 * LANE
    nblk = D_FF // width
    nw = w.shape[0]
    taps = _taps_grid3()

    def body(a_ref, g_ref, w_ref, o_ref, *pads):
        for s in range(width // LANE):
            ls = slice(s * LANE, (s + 1) * LANE)
            _grid_pads_clear_edges(pads, t)
            for r0, rows in _row_chunks(t):
                _grid_pads_set(pads, r0, a_ref[r0:r0 + rows, ls])
            for r0, rows in _row_chunks(t):
                conv = _taps_apply(pads, w_ref.at[:, ls], taps, r0, rows)
                o_ref[r0:r0 + rows, ls] = (_silu(conv) * g_ref[r0:r0 + rows, ls]).astype(bf16)

    spec = lambda off: pl.BlockSpec((t, width), lambda c: (0, c + off))
    return pl.pallas_call(
        body, out_shape=SDS((t, D_FF), bf16), grid=(nblk,),
        in_specs=[spec(0), spec(nblk), pl.BlockSpec((nw, width), lambda c: (0, c))], out_specs=spec(0),
        scratch_shapes=[pltpu.VMEM((t + 2 * PAD_ROWS, LANE), f32)] * 3, name=name,
        compiler_params=_cparams(("parallel",)),
    )(h, h, w)


def ffn_conv_bwd(h, w, df, name):
    t = h.shape[0]
    nblk = D_FF // LANE
    nw = w.shape[0]
    taps = _taps_grid3()
    ttaps = _transpose_taps(taps)

    def body(a_ref, g_ref, w_ref, df_ref, dh_ref, dw_ref, *all_pads):
        half = pl.program_id(1)
        pads, gpads = all_pads[:3], all_pads[3:]

        @pl.when(half == 0)
        def _():
            _grid_pads_clear_edges(all_pads, t)
            for r0, rows in _row_chunks(t):
                _grid_pads_set(pads, r0, a_ref[r0:r0 + rows, :])
            for r0, rows in _row_chunks(t):
                sl = slice(r0, r0 + rows)
                pre = _taps_apply(pads, w_ref, taps, r0, rows)
                _grid_pads_set(gpads, r0, df_ref[sl, :] * g_ref[sl, :] * _dsilu(pre))
                dh_ref[sl, :] = (df_ref[sl, :] * _silu(pre)).astype(bf16)

        @pl.when(half == 1)
        def _():
            dw = jnp.zeros((nw, LANE), f32)
            for r0, rows in _row_chunks(t):
                dh_ref[r0:r0 + rows, :] = _taps_apply(gpads, w_ref, ttaps, r0, rows).astype(bf16)
                dw = dw + _taps_wgrad(pads, gpads[1][PAD_ROWS + r0:PAD_ROWS + r0 + rows, :], taps, r0, rows, nw)
            dw_ref[...] = dw

    cspec = lambda off: pl.BlockSpec((t, LANE), lambda c, s: (0, c + off))
    return pl.pallas_call(
        body, out_shape=(SDS((t, 2 * D_FF), bf16), SDS((nw, D_FF), f32)), grid=(nblk, 2),
        in_specs=[cspec(0), cspec(nblk), pl.BlockSpec((nw, LANE), lambda c, s: (0, c)), cspec(0)],
        out_specs=(pl.BlockSpec((t, LANE), lambda c, s: (0, c + nblk * (1 - s))), pl.BlockSpec((nw, LANE), lambda c, s: (0, c))),
        scratch_shapes=[pltpu.VMEM((t + 2 * PAD_ROWS, LANE), f32)] * 6, name=name,
        compiler_params=_cparams(("parallel", "arbitrary")),
    )(h, h, w, df)


def _pool_count(r0, rows, win, t):
    pos = lax.broadcasted_iota(jnp.int32, (rows, 1), 0) + r0
    lo = jnp.clip(pos - win // 2, 0, t)
    hi = jnp.clip(pos - win // 2 + win, 0, t)
    return (hi - lo).astype(f32)


def _window_sum(pad_ref, r0, rows, lo, hi):
    acc = jnp.zeros((rows, LANE), f32)
    for off in range(lo, hi):
        acc = acc + pad_ref[PAD_ROWS + r0 + off:PAD_ROWS + r0 + off + rows, :]
    return acc


def pool_mix(p, pool_w, pool_scale, name):
    t = p.shape[0]

    def body(x_ref, w_ref, s_ref, o_ref, pad_ref):
        for gi, win in enumerate(POOL_WINDOWS):
            cs = slice(gi * LANE, (gi + 1) * LANE)
            _fill_pad(pad_ref, x_ref[:, cs], t)
            wg = w_ref[gi].astype(bf16)
            for r0, rows in _row_chunks(t):
                pooled = _window_sum(pad_ref, r0, rows, -(win // 2), win - win // 2) / _pool_count(r0, rows, win, t) - x_ref[r0:r0 + rows, cs]
                o_ref[r0:r0 + rows, cs] = (_dotb(pooled, wg) * s_ref[:, cs]).astype(bf16)

    return pl.pallas_call(
        body, out_shape=SDS((t, 512), bf16), grid=(1,),
        in_specs=[pl.BlockSpec((t, 512), lambda i: (0, 4)), pl.BlockSpec((4, LANE, LANE), lambda i: (0, 0, 0)),
                  pl.BlockSpec((1, 512), lambda i: (0, 0))],
        out_specs=pl.BlockSpec((t, 512), lambda i: (0, 0)),
        scratch_shapes=[pltpu.VMEM((t + 2 * PAD_ROWS, LANE), f32)], name=name,
        compiler_params=_cparams(("arbitrary",)),
    )(p, pool_w, pool_scale)


def pool_mix_bwd(p, pool_w, pool_scale, dmix, name):
    t = p.shape[0]

    def body(x_ref, w_ref, s_ref, dy_ref, dp_ref, dw_ref, ds_ref, pad_ref, gpad_ref, dpool_ref):
        for gi, win in enumerate(POOL_WINDOWS):
            cs = slice(gi * LANE, (gi + 1) * LANE)
            h = win // 2
            _fill_pad(pad_ref, x_ref[:, cs], t)
            wg = w_ref[gi].astype(bf16)
            dw = jnp.zeros((LANE, LANE), f32)
            ds = jnp.zeros((1, LANE), f32)
            zeros = jnp.zeros((PAD_ROWS, LANE), f32)
            gpad_ref[0:PAD_ROWS, :] = zeros
            gpad_ref[PAD_ROWS + t:2 * PAD_ROWS + t, :] = zeros
            for r0, rows in _row_chunks(t):
                cnt = _pool_count(r0, rows, win, t)
                pooled = _window_sum(pad_ref, r0, rows, -h, win - h) / cnt - x_ref[r0:r0 + rows, cs]
                dy = dy_ref[r0:r0 + rows, cs]
                ds = ds + jnp.sum(dy * _dotb(pooled, wg), axis=0, keepdims=True)
                dypre = dy * s_ref[:, cs]
                dw = dw + _dotb_tn(pooled, dypre)
                dpooled = _dotb_nt(dypre, wg)
                gpad_ref[PAD_ROWS + r0:PAD_ROWS + r0 + rows, :] = dpooled / cnt
                dpool_ref[r0:r0 + rows, :] = dpooled
            dw_ref[gi] = dw
            ds_ref[:, cs] = ds
            for r0, rows in _row_chunks(t):
                dx = _window_sum(gpad_ref, r0, rows, -h + 1, h + 1) - dpool_ref[r0:r0 + rows, :]
                dp_ref[r0:r0 + rows, cs] = dx.astype(bf16)

    return pl.pallas_call(
        body, out_shape=(SDS((t, 512), bf16), SDS((4, LANE, LANE), f32), SDS((1, 512), f32)), grid=(1,),
        in_specs=[pl.BlockSpec((t, 512), lambda i: (0, 4)), pl.BlockSpec((4, LANE, LANE), lambda i: (0, 0, 0)),
                  pl.BlockSpec((1, 512), lambda i: (0, 0)), pl.BlockSpec((t, 512), lambda i: (0, 1))],
        out_specs=(pl.BlockSpec((t, 512), lambda i: (0, 0)), pl.BlockSpec((4, LANE, LANE), lambda i: (0, 0, 0)),
                   pl.BlockSpec((1, 512), lambda i: (0, 0))),
        scratch_shapes=[pltpu.VMEM((t + 2 * PAD_ROWS, LANE), f32)] * 2 + [pltpu.VMEM((t, LANE), f32)], name=name,
        compiler_params=_cparams(("arbitrary",)),
    )(p, pool_w, pool_scale, dmix)


def gated_rmsnorm(o_a, o_b, p, norm_w, name):
    t = o_a.shape[0]
    tt = _row_tile(t)

    def body(oa_ref, ob_ref, g_ref, nw_ref, y_ref):
        for h in range(GDN_HEADS):
            cs = slice(h * LANE, (h + 1) * LANE)
            o = oa_ref[:, cs] + ob_ref[:, cs]
            r = lax.rsqrt(jnp.mean(o * o, axis=-1, keepdims=True) + RMS_EPS)
            y_ref[:, cs] = (o * r * nw_ref[...] * _silu(g_ref[:, cs])).astype(bf16)

    return pl.pallas_call(
        body, out_shape=SDS((t, 512), bf16), grid=(t // tt,),
        in_specs=[_row_spec(tt, 512), _row_spec(tt, 512), pl.BlockSpec((tt, 512), lambda i: (i, 3)), _vec_spec(LANE)],
        out_specs=_row_spec(tt, 512), name=name, compiler_params=_cparams(("parallel",)),
    )(o_a, o_b, p, norm_w)


def gated_rmsnorm_bwd(o_a, o_b, p, norm_w, dmix, name):
    t = o_a.shape[0]
    tt = _row_tile(t)

    def body(oa_ref, ob_ref, g_ref, nw_ref, dy_ref, do_ref, dg_ref, dnw_ref):
        dnw = jnp.zeros((1, LANE), f32)
        for h in range(GDN_HEADS):
            cs = slice(h * LANE, (h + 1) * LANE)
            o = oa_ref[:, cs] + ob_ref[:, cs]
            r = lax.rsqrt(jnp.mean(o * o, axis=-1, keepdims=True) + RMS_EPS)
            gate = g_ref[:, cs]
            dy = dy_ref[:, cs]
            dy1 = dy * _silu(gate)
            dg_ref[:, cs] = (dy * (o * r * nw_ref[...]) * _dsilu(gate)).astype(bf16)
            dnw = dnw + jnp.sum(dy1 * o * r, axis=0, keepdims=True)
            dn = dy1 * nw_ref[...]
            do_ref[:, cs] = r * dn - o * (r * r * r) * jnp.mean(dn * o, axis=-1, keepdims=True)
        _acc_rows(dnw_ref, dnw)

    return pl.pallas_call(
        body, out_shape=(SDS((t, 512), f32), SDS((t, 512), bf16), SDS((1, LANE), f32)), grid=(t // tt,),
        in_specs=[_row_spec(tt, 512), _row_spec(tt, 512), pl.BlockSpec((tt, 512), lambda i: (i, 3)), _vec_spec(LANE),
                  _row_spec(tt, 512)],
        out_specs=(_row_spec(tt, 512), _row_spec(tt, 512), _vec_spec(LANE)),
        name=name, compiler_params=_cparams(("arbitrary",)),
    )(o_a, o_b, p, norm_w, dmix)


def ln_silu(z, g, b, name):
    t, d = z.shape
    tt = _row_tile(t)

    def body(z_ref, g_ref, b_ref, o_ref):
        xhat, _ = _ln_stats(z_ref[...])
        o_ref[...] = _silu(xhat * g_ref[...] + b_ref[...]).astype(bf16)

    return pl.pallas_call(
        body, out_shape=SDS((t, d), bf16), grid=(t // tt,),
        in_specs=[_row_spec(tt, d), _vec_spec(d), _vec_spec(d)], out_specs=_row_spec(tt, d),
        name=name, compiler_params=_cparams(("parallel",)),
    )(z, g, b)


def ln_silu_bwd(z, g, b, dmix, name):
    t, d = z.shape
    tt = _row_tile(t)

    def body(z_ref, g_ref, b_ref, dy_ref, dz_ref, dg_ref, db_ref):
        xhat, rstd = _ln_stats(z_ref[...])
        dn = dy_ref[...] * _dsilu(xhat * g_ref[...] + b_ref[...])
        dz_ref[...] = _ln_bwd(dn * g_ref[...], xhat, rstd)
        _acc_rows(dg_ref, jnp.sum(dn * xhat, axis=0, keepdims=True))
        _acc_rows(db_ref, jnp.sum(dn, axis=0, keepdims=True))

    return pl.pallas_call(
        body, out_shape=(SDS((t, d), f32), SDS((1, d), f32), SDS((1, d), f32)), grid=(t // tt,),
        in_specs=[_row_spec(tt, d), _vec_spec(d), _vec_spec(d), pl.BlockSpec((tt, d), lambda i: (i, 1))],
        out_specs=(_row_spec(tt, d), _vec_spec(d), _vec_spec(d)),
        name=name, compiler_params=_cparams(("arbitrary",)),
    )(z, g, b, dmix)


def gdn_gates(p, neg_a, dt_bias, name):
    t = p.shape[0]
    tt = _row_tile(t)

    def body(s_ref, na_ref, dt_ref, o_ref):
        s = s_ref[...]
        col = lax.broadcasted_iota(jnp.int32, s.shape, 1)
        o_ref[...] = jnp.where(col < 8, jax.nn.sigmoid(s), na_ref[...] * jax.nn.softplus(s + dt_ref[...]))

    return pl.pallas_call(
        body, out_shape=SDS((t, LANE), f32), grid=(t // tt,),
        in_specs=[pl.BlockSpec((tt, LANE), lambda i: (i, 20)), _vec_spec(LANE), _vec_spec(LANE)],
        out_specs=_row_spec(tt, LANE), name=name, compiler_params=_cparams(("parallel",)),
    )(p, neg_a, dt_bias)


def gdn_gates_bwd(p, neg_a, dt_bias, dbg_a, dbg_b, name):
    t = p.shape[0]
    tt = _row_tile(t)

    def body(s_ref, na_ref, dt_ref, d_ref, d2_ref, ds_ref, da_ref, ddt_ref):
        s = s_ref[...]
        d = d_ref[...] + d2_ref[...]
        col = lax.broadcasted_iota(jnp.int32, s.shape, 1)
        sg = jax.nn.sigmoid(s)
        z = s + dt_ref[...]
        dz = jnp.where((col >= 8) & (col < 16), d * na_ref[...] * jax.nn.sigmoid(z), 0.0)
        ds_ref[...] = jnp.where(col < 8, d * sg * (1.0 - sg), dz).astype(bf16)
        dalog = jnp.where((col >= 8) & (col < 16), d * na_ref[...] * jax.nn.softplus(z), 0.0)
        _acc_rows(da_ref, jnp.sum(dalog, axis=0, keepdims=True))
        _acc_rows(ddt_ref, jnp.sum(dz, axis=0, keepdims=True))

    return pl.pallas_call(
        body, out_shape=(SDS((t, LANE), bf16), SDS((1, LANE), f32), SDS((1, LANE), f32)), grid=(t // tt,),
        in_specs=[pl.BlockSpec((tt, LANE), lambda i: (i, 20)), _vec_spec(LANE), _vec_spec(LANE), _row_spec(tt, LANE),
                  _row_spec(tt, LANE)],
        out_specs=(_row_spec(tt, LANE), _vec_spec(LANE), _vec_spec(LANE)),
        name=name, compiler_params=_cparams(("arbitrary",)),
    )(p, neg_a, dt_bias, dbg_a, dbg_b)


N_SCAN = 2 * GDN_HEADS


def _bdot(a, b, ca, cb, precision=None):
    if precision is None:
        a, b = a.astype(bf16), b.astype(bf16)
    return lax.dot_general(a, b, (((ca,), (cb,)), ((0,), (0,))), preferred_element_type=f32, precision=precision)


def _bdot_nn(a, b, precision=None):
    return _bdot(a, b, 2, 1, precision)


def _bdot_nt(a, b):
    return _bdot(a, b, 2, 2)


def _bdot_tn(a, b, precision=None):
    return _bdot(a, b, 1, 1, precision)


def _order_masks():
    shape = (N_SCAN, CHUNK, CHUNK)
    sign = jnp.where(lax.broadcasted_iota(jnp.int32, shape, 0) >= GDN_HEADS, -1, 1)
    ahead = (lax.broadcasted_iota(jnp.int32, shape, 1) - lax.broadcasted_iota(jnp.int32, shape, 2)) * sign
    lower, strict, lower_t = ahead >= 0, ahead > 0, ahead <= 0
    col_shape = (N_SCAN, CHUNK, 1)
    back1 = lax.broadcasted_iota(jnp.int32, col_shape, 0) >= GDN_HEADS
    row1 = lax.broadcasted_iota(jnp.int32, col_shape, 1)
    at_last = (row1 == jnp.where(back1, 0, CHUNK - 1)).astype(f32)
    return lower, strict, lower_t, at_last


def _stack_heads(f_ref, b_ref):
    return jnp.stack([ref[:, h * LANE:(h + 1) * LANE] for ref in (f_ref, b_ref) for h in range(GDN_HEADS)])


def _stack_gates(bgf, bgb, bgtf, bgtb):
    beta = jnp.stack([bg[:, 4 * d + h:4 * d + h + 1] for d, bg in enumerate((bgf, bgb)) for h in range(GDN_HEADS)])
    g_col = jnp.stack([bg[:, 8 + 4 * d + h:9 + 4 * d + h] for d, bg in enumerate((bgf, bgb)) for h in range(GDN_HEADS)])
    g_row = jnp.stack([bgt[8 + 4 * d + h:9 + 4 * d + h, :] for d, bgt in enumerate((bgtf, bgtb)) for h in range(GDN_HEADS)])
    return beta, g_col, g_row


def _chunk_terms(k, v, beta, g_col, g_row, masks, tinv=None):
    lower, strict, lower_t, at_last = masks
    gc = jnp.sum(lower.astype(f32) * g_row, axis=2, keepdims=True)
    gr = jnp.sum(lower_t.astype(f32) * g_col, axis=1, keepdims=True)
    g_last = jnp.sum(at_last * gc, axis=1, keepdims=True)
    e = jnp.exp(gc)
    f = jnp.exp(g_last - gc)
    dm = jnp.exp(jnp.where(lower, gc - gr, -1e30))
    kb = k * beta
    kk = _bdot_nt(kb, k)
    if tinv is None:
        shape = (N_SCAN, CHUNK, CHUNK)
        eye = (lax.broadcasted_iota(jnp.int32, shape, 1) == lax.broadcasted_iota(jnp.int32, shape, 2)).astype(f32)
        pw = -jnp.where(strict, kk * dm, 0.0)
        tinv = eye + pw
        for _ in range(5):
            pw = _bdot_nn(pw, pw, lax.Precision.HIGH)
            tinv = tinv + _bdot_nn(tinv, pw, lax.Precision.HIGH)
    u = _bdot_nn(tinv, v * beta)
    w = _bdot_nn(tinv, kb * e)
    return dict(e=e, f=f, gl=jnp.exp(g_last), dm=dm, kb=kb, kk=kk, tinv=tinv, u=u, w=w, kd=k * f)


def _gdn_specs(nc, width, step_chunk):
    return [pl.BlockSpec((CHUNK, width), functools.partial(lambda i, d: (step_chunk(i, d), 0), d=d)) for d in (0, 1)]


def gdn_forward(q, k, v, bg, bgt, s0, with_out, name):
    t = k.shape[0]
    nc = t // CHUNK

    def body(qf_ref, qb_ref, kf_ref, kb_ref, vf_ref, vb_ref, bgf_ref, bgb_ref, bgtf_ref, bgtb_ref, s0_ref,
             of_ref, ob_ref, sallf_ref, sallb_ref, tinvf_ref, tinvb_ref, sfin_ref, s_ref):
        i = pl.program_id(0)

        @pl.when(i == 0)
        def _():
            s_ref[...] = s0_ref[...]

        masks = _order_masks()
        k8, v8 = _stack_heads(kf_ref, kb_ref), _stack_heads(vf_ref, vb_ref)
        beta, g_col, g_row = _stack_gates(bgf_ref[...], bgb_ref[...], bgtf_ref[0], bgtb_ref[0])
        c = _chunk_terms(k8, v8, beta, g_col, g_row, masks)
        s = s_ref[...]
        sallf_ref[0] = s[:GDN_HEADS]
        sallb_ref[0] = s[GDN_HEADS:]
        tinvf_ref[0] = c["tinv"][:GDN_HEADS]
        tinvb_ref[0] = c["tinv"][GDN_HEADS:]
        vn = c["u"] - _bdot_nn(c["w"], s)
        if with_out:
            q8 = _stack_heads(qf_ref, qb_ref)
            pm = jnp.where(masks[0], _bdot_nt(q8, k8) * c["dm"], 0.0)
            o = _bdot_nn(q8 * c["e"], s) + _bdot_nn(pm, vn)
        for d, o_ref in enumerate((of_ref, ob_ref)):
            for h in range(GDN_HEADS):
                o_ref[:, h * LANE:(h + 1) * LANE] = o[GDN_HEADS * d + h] if with_out else jnp.zeros((CHUNK, LANE), f32)
        s_ref[...] = c["gl"] * s + _bdot_tn(c["kd"], vn)

        @pl.when(i == nc - 1)
        def _():
            sfin_ref[...] = s_ref[...]

    chunk_of = lambda i, d: i if d == 0 else nc - 1 - i
    seq = _gdn_specs(nc, 512, chunk_of)
    gate = _gdn_specs(nc, LANE, chunk_of)
    gate_t = [pl.BlockSpec((1, 16, CHUNK), functools.partial(lambda i, d: (chunk_of(i, d), 0, 0), d=d)) for d in (0, 1)]
    sall = [pl.BlockSpec((1, GDN_HEADS, LANE, LANE), functools.partial(lambda i, d: (chunk_of(i, d), 0, 0, 0), d=d)) for d in (0, 1)]
    tinv = [pl.BlockSpec((1, GDN_HEADS, CHUNK, CHUNK), functools.partial(lambda i, d: (chunk_of(i, d), 0, 0, 0), d=d)) for d in (0, 1)]
    st = pl.BlockSpec((N_SCAN, LANE, LANE), lambda i: (0, 0, 0))
    o_shape, s_shape, t_shape = SDS((t, 512), f32), SDS((nc, GDN_HEADS, LANE, LANE), f32), SDS((nc, GDN_HEADS, CHUNK, CHUNK), f32)
    o_f, o_b, sall_f, sall_b, tinv_f, tinv_b, s_fin = pl.pallas_call(
        body, out_shape=(o_shape, o_shape, s_shape, s_shape, t_shape, t_shape, SDS((N_SCAN, LANE, LANE), f32)), grid=(nc,),
        in_specs=seq + seq + seq + gate + gate_t + [st], out_specs=tuple(seq + sall + tinv + [st]),
        scratch_shapes=[pltpu.VMEM((N_SCAN, LANE, LANE), f32)], name=name,
        compiler_params=_cparams(("arbitrary",)),
    )(q, q, k, k, v, v, bg, bg, bgt, bgt, s0.reshape(N_SCAN, LANE, LANE))
    return o_f, o_b, (sall_f, sall_b, tinv_f, tinv_b), s_fin.reshape(2, GDN_HEADS, LANE, LANE)


def _gdn_chunk_bwd(q, k, v, d_o, beta, g_col, g_row, s, tinv, dsn, masks):
    lower, strict, _, at_last = masks
    c = _chunk_terms(k, v, beta, g_col, g_row, masks, tinv)
    e, f, gl, dm, kb, kk, tinv, u, w, kd = (c[n] for n in ("e", "f", "gl", "dm", "kb", "kk", "tinv", "u", "w", "kd"))
    vn = u - _bdot_nn(w, s)
    ds = gl * dsn
    dgl = jnp.sum(jnp.sum(s * dsn, axis=2, keepdims=True), axis=1, keepdims=True)
    dkd = _bdot_nt(vn, dsn)
    dvn = _bdot_nn(kd, dsn)
    dm_grad = jnp.zeros((N_SCAN, CHUNK, CHUNK), f32)
    de = jnp.zeros((N_SCAN, CHUNK, 1), f32)
    dq = None
    dk = jnp.zeros((N_SCAN, CHUNK, LANE), f32)
    if q is not None:
        qk = _bdot_nt(q, k)
        pm = jnp.where(lower, qk * dm, 0.0)
        dqd = _bdot_nt(d_o, s)
        ds = ds + _bdot_tn(q * e, d_o)
        dpm = jnp.where(lower, _bdot_nt(d_o, vn), 0.0)
        dvn = dvn + _bdot_tn(pm, d_o)
        dqk = dpm * dm
        dm_grad = dm_grad + dpm * qk
        dq = _bdot_nn(dqk, k) + dqd * e
        dk = _bdot_tn(dqk, q)
        de = de + jnp.sum(dqd * q, axis=2, keepdims=True)
    dw = -_bdot_nt(dvn, s)
    ds = ds - _bdot_tn(w, dvn)
    drv = _bdot_tn(tinv, dvn)
    drk = _bdot_tn(tinv, dw)
    da = -jnp.where(strict, _bdot_nt(drv, u) + _bdot_nt(drk, w), 0.0)
    dbeta = jnp.sum(drv * v, axis=2, keepdims=True)
    dv = drv * beta
    dkb = drk * e
    de = de + jnp.sum(drk * kb, axis=2, keepdims=True)
    dkk = da * dm
    dm_grad = dm_grad + da * kk
    dkb = dkb + _bdot_nn(dkk, k)
    dk = dk + _bdot_tn(dkk, kb) + dkd * f
    df = jnp.sum(dkd * k, axis=2, keepdims=True)
    dbeta = dbeta + jnp.sum(dkb * k, axis=2, keepdims=True)
    dk = dk + dkb * beta
    m = dm_grad * dm
    shape = (N_SCAN, CHUNK, CHUNK)
    eye = (lax.broadcasted_iota(jnp.int32, shape, 1) == lax.broadcasted_iota(jnp.int32, shape, 2)).astype(f32)

    def as_col(row):
        return jnp.sum(eye * row, axis=2, keepdims=True)

    rsum = jnp.sum(m, axis=2, keepdims=True)
    csum = as_col(jnp.sum(m, axis=1, keepdims=True))
    dgl_tot = jnp.sum(df * f, axis=1, keepdims=True) + dgl * gl
    dgc = de * e - df * f + rsum - csum + at_last * dgl_tot
    dg = as_col(jnp.sum(lower.astype(f32) * dgc, axis=1, keepdims=True))
    return dq, dk, dv, dbeta, dg, ds


def gdn_backward(q, k, v, bg, bgt, saved, d_o, ds_fin, with_out, name):
    t = k.shape[0]
    nc = t // CHUNK

    def body(qf_ref, qb_ref, kf_ref, kb_ref, vf_ref, vb_ref, bgf_ref, bgb_ref, bgtf_ref, bgtb_ref,
             sallf_ref, sallb_ref, tinvf_ref, tinvb_ref, dof_ref, dob_ref, dsf_ref,
             dqf_ref, dqb_ref, dkf_ref, dkb_ref, dvf_ref, dvb_ref, dbgf_ref, dbgb_ref, ds0_ref, ds_ref):
        i = pl.program_id(0)

        @pl.when(i == 0)
        def _():
            ds_ref[...] = dsf_ref[...]

        lane = lax.broadcasted_iota(jnp.int32, (1, LANE), 1)
        masks = _order_masks()
        beta, g_col, g_row = _stack_gates(bgf_ref[...], bgb_ref[...], bgtf_ref[0], bgtb_ref[0])
        s = jnp.concatenate([sallf_ref[0], sallb_ref[0]], 0)
        tinv = jnp.concatenate([tinvf_ref[0], tinvb_ref[0]], 0)
        dq, dk, dv, dbeta, dg, ds = _gdn_chunk_bwd(
            _stack_heads(qf_ref, qb_ref) if with_out else None, _stack_heads(kf_ref, kb_ref), _stack_heads(vf_ref, vb_ref),
            _stack_heads(dof_ref, dob_ref), beta, g_col, g_row, s, tinv, ds_ref[...], masks)
        ds_ref[...] = ds
        for d, (dq_ref, dk_ref, dv_ref, dbg_ref) in enumerate(((dqf_ref, dkf_ref, dvf_ref, dbgf_ref), (dqb_ref, dkb_ref, dvb_ref, dbgb_ref))):
            dbg = jnp.zeros((CHUNK, LANE), f32)
            for h in range(GDN_HEADS):
                b = GDN_HEADS * d + h
                cs = slice(h * LANE, (h + 1) * LANE)
                dq_ref[:, cs] = dq[b] if with_out else jnp.zeros((CHUNK, LANE), f32)
                dk_ref[:, cs] = dk[b]
                dv_ref[:, cs] = dv[b]
                dbg = dbg + dbeta[b] * (lane == b).astype(f32) + dg[b] * (lane == 8 + b).astype(f32)
            dbg_ref[...] = dbg

        @pl.when(i == nc - 1)
        def _():
            ds0_ref[...] = ds_ref[...]

    chunk_of = lambda i, d: nc - 1 - i if d == 0 else i
    seq = _gdn_specs(nc, 512, chunk_of)
    gate = _gdn_specs(nc, LANE, chunk_of)
    gate_t = [pl.BlockSpec((1, 16, CHUNK), functools.partial(lambda i, d: (chunk_of(i, d), 0, 0), d=d)) for d in (0, 1)]
    sall = [pl.BlockSpec((1, GDN_HEADS, LANE, LANE), functools.partial(lambda i, d: (chunk_of(i, d), 0, 0, 0), d=d)) for d in (0, 1)]
    tinv = [pl.BlockSpec((1, GDN_HEADS, CHUNK, CHUNK), functools.partial(lambda i, d: (chunk_of(i, d), 0, 0, 0), d=d)) for d in (0, 1)]
    st = pl.BlockSpec((N_SCAN, LANE, LANE), lambda i: (0, 0, 0))
    o_shape, g_shape = SDS((t, 512), f32), SDS((t, LANE), f32)
    res = pl.pallas_call(
        body, out_shape=(o_shape,) * 6 + (g_shape, g_shape, SDS((N_SCAN, LANE, LANE), f32)), grid=(nc,),
        in_specs=seq + seq + seq + gate + gate_t + sall + tinv + seq + [st], out_specs=tuple(seq + seq + seq + gate + [st]),
        scratch_shapes=[pltpu.VMEM((N_SCAN, LANE, LANE), f32)], name=name,
        compiler_params=_cparams(("arbitrary",)),
    )(q, q, k, k, v, v, bg, bg, bgt, bgt, *saved, d_o, d_o, ds_fin.reshape(N_SCAN, LANE, LANE))
    return tuple(res[:8]) + (res[8].reshape(2, GDN_HEADS, LANE, LANE),)


def _my_position():
    x, y, c = lax.axis_index("x"), lax.axis_index("y"), lax.axis_index("c")
    return x, y, c, 4 * x + 2 * y + c


def exchange(arrays, scatter, name):
    n = len(arrays)
    shapes = [a.shape[1:] if scatter else a.shape for a in arrays]

    def body(*refs):
        ins, outs, token = refs[:n], refs[n:2 * n], refs[2 * n]
        send_sems, recv_sems, local_sems = refs[2 * n + 1:]
        x, y, c, me = _my_position()
        token[...] = jnp.zeros_like(token)
        started = []
        for a in range(n):
            mine = pltpu.make_async_copy(ins[a].at[me] if scatter else ins[a], outs[a].at[me], local_sems.at[a])
            mine.start()
            started.append(mine)
        waits = []
        for r in range(1, N_DEV):
            px = 1 - x if r & 4 else x
            py = 1 - y if r & 2 else y
            pc = 1 - c if r & 1 else c
            pid = 4 * px + 2 * py + pc
            for a in range(n):
                cp = pltpu.make_async_remote_copy(
                    src_ref=ins[a].at[pid] if scatter else ins[a], dst_ref=outs[a].at[me],
                    send_sem=send_sems.at[a, r - 1], recv_sem=recv_sems.at[a, r - 1],
                    device_id=(px, py, pc), device_id_type=pl.DeviceIdType.MESH)
                cp.start()
                arrive = pltpu.make_async_remote_copy(
                    src_ref=ins[a].at[pid] if scatter else ins[a], dst_ref=outs[a].at[pid],
                    send_sem=send_sems.at[a, r - 1], recv_sem=recv_sems.at[a, r - 1],
                    device_id=(px, py, pc), device_id_type=pl.DeviceIdType.MESH)
                waits.append((cp, arrive))
        for cp, arrive in waits:
            cp.wait_send()
            arrive.wait_recv()
        for mine in started:
            mine.wait()

    any_spec = pl.BlockSpec(memory_space=pl.ANY)
    return pl.pallas_call(
        body, out_shape=tuple(SDS((N_DEV,) + tuple(s), a.dtype) for s, a in zip(shapes, arrays)) + (SDS((8, LANE), f32),),
        in_specs=[any_spec] * n, out_specs=tuple([any_spec] * n) + (pl.BlockSpec(memory_space=pltpu.VMEM),),
        scratch_shapes=[pltpu.SemaphoreType.DMA((n, N_DEV - 1)), pltpu.SemaphoreType.DMA((n, N_DEV - 1)),
                        pltpu.SemaphoreType.DMA((n,))],
        name=name,
    )(*arrays)


_HBM_SPEC = pl.BlockSpec(memory_space=pltpu.HBM)
_SEM_SPEC = pl.BlockSpec(memory_space=pltpu.SEMAPHORE)
_DATAFLOW = pltpu.SideEffectType.DATAFLOW_SIDE_EFFECTING


def _peers(x, y, c):
    out = []
    for r in range(1, N_DEV):
        px = 1 - x if r & 4 else x
        py = 1 - y if r & 2 else y
        pc = 1 - c if r & 1 else c
        out.append((r, (px, py, pc), 4 * px + 2 * py + pc))
    return out


def _exchange_copies(ins, lands, send_sems, recv_sems, scatter, arrivals):
    x, y, c, me = _my_position()
    pairs = []
    for r, peer, pid in _peers(x, y, c):
        for a in range(len(ins)):
            k = a * (N_DEV - 1) + r - 1
            kw = dict(send_sem=send_sems.at[k], recv_sem=recv_sems.at[k], device_id=peer, device_id_type=pl.DeviceIdType.MESH)
            src = ins[a].at[pid] if scatter else ins[a]
            send = pltpu.make_async_remote_copy(src_ref=src, dst_ref=lands[a].at[me], **kw)
            arrive = pltpu.make_async_remote_copy(src_ref=src, dst_ref=lands[a].at[pid], **kw) if arrivals else None
            pairs.append((send, arrive))
    return pairs


def exchange_start(arrays, scatter, name):
    n = len(arrays)
    shapes = [a.shape[1:] if scatter else a.shape for a in arrays]

    def body(*refs):
        ins, lands = refs[:n], refs[n:2 * n]
        send_sems, recv_sems = refs[2 * n], refs[2 * n + 1]
        token = refs[-1]
        for send, _ in _exchange_copies(ins, lands, send_sems, recv_sems, scatter, False):
            send.start()
        token[...] = jnp.zeros_like(token)

    sem = pltpu.SemaphoreType.DMA((n * (N_DEV - 1),))
    land_shapes = [(N_DEV,) + tuple(s) for s in shapes]
    res = pl.pallas_call(
        body, name=name,
        out_shape=(sem, sem, *[pltpu.HBM(a.shape, a.dtype) for a in arrays],
                   *[pltpu.HBM(s, a.dtype) for s, a in zip(land_shapes, arrays)], SDS((8, LANE), f32)),
        in_specs=[_HBM_SPEC] * (2 * n),
        out_specs=(_SEM_SPEC, _SEM_SPEC, *[_HBM_SPEC] * (2 * n), pl.BlockSpec(memory_space=pltpu.VMEM)),
        input_output_aliases={i: 2 + i for i in range(2 * n)},
        compiler_params=pltpu.CompilerParams(has_side_effects=_DATAFLOW),
    )(*[pltpu.with_memory_space_constraint(a, pltpu.HBM) for a in arrays],
      *[pltpu.with_memory_space_constraint(lax.empty(s, a.dtype), pltpu.HBM) for s, a in zip(land_shapes, arrays)])
    return (res[0], res[1], list(res[2:2 + n]), list(res[2 + n:2 + 2 * n]), scatter), res[-1]


def exchange_wait(handle, after, name):
    send_sems, recv_sems, ins, lands, scatter = handle
    n = len(ins)

    def body(*refs):
        in_refs, land_refs = refs[:n], refs[n:2 * n]
        for send, arrive in _exchange_copies(in_refs, land_refs, refs[2 * n], refs[2 * n + 1], scatter, True):
            send.wait_send()
            arrive.wait_recv()
        refs[-1][...] = jnp.zeros_like(refs[-1])

    res = pl.pallas_call(
        body, name=name,
        out_shape=tuple(pltpu.HBM(a.shape, a.dtype) for a in ins + lands) + (SDS((8, LANE), f32),),
        in_specs=[_HBM_SPEC] * (2 * n) + [_SEM_SPEC, _SEM_SPEC, pl.BlockSpec(memory_space=pl.ANY)],
        out_specs=tuple([_HBM_SPEC] * (2 * n)) + (pl.BlockSpec(memory_space=pltpu.VMEM),),
        input_output_aliases={i: i for i in range(2 * n)},
        compiler_params=pltpu.CompilerParams(has_side_effects=_DATAFLOW),
    )(*ins, *lands, send_sems, recv_sems, after)
    return list(res[:n]), list(res[n:2 * n]), res[-1]


def place_own(lands, arrays, scatter, me):
    own = [lax.dynamic_index_in_dim(a, me, 0, keepdims=False) if scatter else a for a in arrays]
    return [lax.dynamic_update_index_in_dim(l, o, me, 0) for l, o in zip(lands, own)]


def ada_forward(a_raw, ada_w, ada_b_loc, name):
    def body(a_ref, w_ref, b_ref, o_ref):
        a = _silu(a_ref[...])
        for l in range(DEPTH):
            o_ref[l] = _dotf(a, w_ref[l]) + b_ref[l]

    return pl.pallas_call(body, out_shape=SDS((DEPTH, 16, ada_w.shape[2]), f32), name=name,
                          compiler_params=_cparams())(a_raw, ada_w, ada_b_loc)


def ada_backward(a_raw, ada_w, dm, name):
    def body(a_ref, w_ref, dm_ref, gw_ref, dcc_ref):
        a = _silu(a_ref[...])
        for l in range(DEPTH):
            gw_ref[l] = _dotf(a, dm_ref[l], (((0,), (0,)), ((), ())))
        dcc_ref[...] = _dotf(dm_ref[0, 8:16, :], w_ref[0], (((1,), (1,)), ((), ())))

    return pl.pallas_call(body, out_shape=(SDS(ada_w.shape, f32), SDS((8, ada_w.shape[1]), f32)), name=name,
                          compiler_params=_cparams())(a_raw, ada_w, dm)


def sum_parts(parts, name):
    _, r, c = parts.shape

    def body(p_ref, o_ref):
        acc = p_ref[0]
        for i in range(1, N_DEV):
            acc = acc + p_ref[i]
        o_ref[...] = acc

    return pl.pallas_call(body, out_shape=SDS((r, c), f32), name=name, compiler_params=_cparams())(parts)


def cctx_grad(parts, c_ctx, name):
    def body(p_ref, c_ref, o_ref):
        acc = p_ref[0, 0:1, :]
        for i in range(1, N_DEV):
            acc = acc + p_ref[i, 0:1, :]
        o_ref[...] = acc * _dsilu(c_ref[...])

    return pl.pallas_call(body, out_shape=SDS((1, c_ctx.shape[1]), f32), name=name, compiler_params=_cparams())(parts, c_ctx)


def _adamw_math(g, w, m, v):
    m = ADAM_B1 * m + (1.0 - ADAM_B1) * g
    v = ADAM_B2 * v + (1.0 - ADAM_B2) * (g * g)
    m_hat = m / (1.0 - ADAM_B1 ** ADAM_STEP)
    v_hat = v / (1.0 - ADAM_B2 ** ADAM_STEP)
    delta = -ADAM_LR * (m_hat / (jnp.sqrt(v_hat) + ADAM_EPS) + ADAM_WD * w)
    return delta, m, v


def adamw(parts, w, m, v, name):
    n, r, c = parts.shape
    tr = _pick(r, (256, 128, 64, 32, 16, 8))

    def body(p_ref, w_ref, m_ref, v_ref, g_ref, d_ref, nm_ref, nv_ref):
        g = p_ref[0].astype(f32)
        for i in range(1, n):
            g = g + p_ref[i].astype(f32)
        g_ref[...] = g
        d_ref[...], nm_ref[...], nv_ref[...] = _adamw_math(g, w_ref[...], m_ref[...], v_ref[...])

    blk = pl.BlockSpec((tr, c), lambda i: (i, 0))
    out = SDS((r, c), f32)
    return pl.pallas_call(
        body, out_shape=(out, out, out, out), grid=(r // tr,),
        in_specs=[pl.BlockSpec((n, tr, c), lambda i: (0, i, 0)), blk, blk, blk], out_specs=(blk, blk, blk, blk),
        name=name, compiler_params=_cparams(("parallel",)),
    )(parts, w, m, v)


def adamw_layers(parts, w, m, v, name):
    nl = len(parts)
    n, r, c = parts[0].shape
    tr = _pick(r, (256, 128, 64, 32, 16, 8))

    def body(*refs):
        p_refs = refs[:nl]
        w_ref, m_ref, v_ref, g_ref, d_ref, nm_ref, nv_ref = refs[nl:]
        for k in range(nl):
            @pl.when(pl.program_id(0) == k)
            def _(p_ref=p_refs[k]):
                g = p_ref[0].astype(f32)
                for i in range(1, n):
                    g = g + p_ref[i].astype(f32)
                g_ref[0] = g
                d_ref[0], nm_ref[0], nv_ref[0] = _adamw_math(g, w_ref[0], m_ref[0], v_ref[0])

    p_specs = [pl.BlockSpec((n, tr, c), functools.partial(lambda l, i, k: (0, jnp.where(l == k, i, 0), 0), k=k)) for k in range(nl)]
    blk = pl.BlockSpec((1, tr, c), lambda l, i: (l, i, 0))
    out = SDS((nl, r, c), f32)
    return pl.pallas_call(
        body, out_shape=(out, out, out, out), grid=(nl, r // tr),
        in_specs=p_specs + [blk, blk, blk], out_specs=(blk, blk, blk, blk),
        name=name, compiler_params=_cparams(("arbitrary", "arbitrary")),
    )(*parts, w, m, v)


def adamw_small(items, name):
    n = len(items)

    def body(*refs):
        ins, outs = refs[:4 * n], refs[4 * n:]
        for i in range(n):
            g, w, m, v = (ins[4 * i + j][...] for j in range(4))
            outs[3 * i][...], outs[3 * i + 1][...], outs[3 * i + 2][...] = _adamw_math(g, w, m, v)

    flat = [a for it in items for a in it]
    out_shape = tuple(SDS(it[1].shape, f32) for it in items for _ in range(3))
    res = pl.pallas_call(body, out_shape=out_shape, name=name, compiler_params=_cparams())(*flat)
    return [tuple(res[3 * i:3 * i + 3]) for i in range(n)]


def _unshard(g, axis):
    loc = g.shape[1:]
    return jnp.moveaxis(g, 0, axis).reshape(loc[:axis] + (N_DEV * loc[axis],) + loc[axis + 1:])


def _shard_major(full, axis):
    s = full.shape
    return jnp.moveaxis(full.reshape(s[:axis] + (N_DEV, s[axis] // N_DEV) + s[axis + 1:]), axis, 0)


def _my_block(full, axis, me):
    n = full.shape[axis] // N_DEV
    return lax.dynamic_slice_in_dim(full, me * n, n, axis)


def _pack(arrays):
    flat = [a.reshape(-1) for a in arrays]
    sizes = [f.shape[0] for f in flat]
    total = sum(sizes)
    padded = -(-total // (8 * LANE)) * (8 * LANE)
    flat.append(jnp.zeros((padded - total,), f32))
    offs = [sum(sizes[:i]) for i in range(len(sizes))]
    return jnp.concatenate(flat).reshape(padded // LANE, LANE), offs


def _pad_rows(w, n):
    return jnp.concatenate([w, jnp.zeros((n - w.shape[0],) + w.shape[1:], w.dtype)], 0)


def _gate_rows(bg):
    return bg[:, :16].reshape(bg.shape[0] // CHUNK, CHUNK, 16).transpose(0, 2, 1)


def _rows(vec, n):
    m = vec.reshape(n, 1, -1)
    return [m[i] for i in range(n)]


def kernel(x, c, ctx, c_ctx, ada_w, ada_b, ln_g, ln_b, even_w_in, even_w_out, gdn_conv_w, gdn_a_log, gdn_dt_bias, gdn_norm_w, pool_w, pool_scale, odd_w_in, odd_w_out, sconv_w, conf_conv_w, conf_ln_g, conf_ln_b, ffn_w_up, ffn_conv_w, ffn_w_down, loss_target, m_c_ctx, m_ada_w, m_ada_b, m_ln_g, m_ln_b, m_even_w_in, m_even_w_out, m_gdn_conv_w, m_gdn_a_log, m_gdn_dt_bias, m_gdn_norm_w, m_pool_w, m_pool_scale, m_odd_w_in, m_odd_w_out, m_sconv_w, m_conf_conv_w, m_conf_ln_g, m_conf_ln_b, m_ffn_w_up, m_ffn_conv_w, m_ffn_w_down, v_c_ctx, v_ada_w, v_ada_b, v_ln_g, v_ln_b, v_even_w_in, v_even_w_out, v_gdn_conv_w, v_gdn_a_log, v_gdn_dt_bias, v_gdn_norm_w, v_pool_w, v_pool_scale, v_odd_w_in, v_odd_w_out, v_sconv_w, v_conf_conv_w, v_conf_ln_g, v_conf_ln_b, v_ffn_w_up, v_ffn_conv_w, v_ffn_w_down):
    weights = dict(c_ctx=c_ctx, ada_w=ada_w, ada_b=ada_b, ln_g=ln_g, ln_b=ln_b, even_w_in=even_w_in, even_w_out=even_w_out, gdn_conv_w=gdn_conv_w, gdn_a_log=gdn_a_log, gdn_dt_bias=gdn_dt_bias, gdn_norm_w=gdn_norm_w, pool_w=pool_w, pool_scale=pool_scale, odd_w_in=odd_w_in, odd_w_out=odd_w_out, sconv_w=sconv_w, conf_conv_w=conf_conv_w, conf_ln_g=conf_ln_g, conf_ln_b=conf_ln_b, ffn_w_up=ffn_w_up, ffn_conv_w=ffn_conv_w, ffn_w_down=ffn_w_down)
    mom1 = dict(c_ctx=m_c_ctx, ada_w=m_ada_w, ada_b=m_ada_b, ln_g=m_ln_g, ln_b=m_ln_b, even_w_in=m_even_w_in, even_w_out=m_even_w_out, gdn_conv_w=m_gdn_conv_w, gdn_a_log=m_gdn_a_log, gdn_dt_bias=m_gdn_dt_bias, gdn_norm_w=m_gdn_norm_w, pool_w=m_pool_w, pool_scale=m_pool_scale, odd_w_in=m_odd_w_in, odd_w_out=m_odd_w_out, sconv_w=m_sconv_w, conf_conv_w=m_conf_conv_w, conf_ln_g=m_conf_ln_g, conf_ln_b=m_conf_ln_b, ffn_w_up=m_ffn_w_up, ffn_conv_w=m_ffn_conv_w, ffn_w_down=m_ffn_w_down)
    mom2 = dict(c_ctx=v_c_ctx, ada_w=v_ada_w, ada_b=v_ada_b, ln_g=v_ln_g, ln_b=v_ln_b, even_w_in=v_even_w_in, even_w_out=v_even_w_out, gdn_conv_w=v_gdn_conv_w, gdn_a_log=v_gdn_a_log, gdn_dt_bias=v_gdn_dt_bias, gdn_norm_w=v_gdn_norm_w, pool_w=v_pool_w, pool_scale=v_pool_scale, odd_w_in=v_odd_w_in, odd_w_out=v_odd_w_out, sconv_w=v_sconv_w, conf_conv_w=v_conf_conv_w, conf_ln_g=v_conf_ln_g, conf_ln_b=v_conf_ln_b, ffn_w_up=v_ffn_w_up, ffn_conv_w=v_ffn_conv_w, ffn_w_down=v_ffn_w_down)
    order = list(weights)
    me = 4 * lax.axis_index("x") + 2 * lax.axis_index("y") + lax.axis_index("c")
    x, ctx, target = x[0], ctx[0], loss_target[0]
    t, d = x.shape
    tc = ctx.shape[0]

    small_in = [ln_g, ln_b, gdn_conv_w, sconv_w, conf_conv_w, ffn_conv_w, c]
    small_axes = [2, 2, 1, 1, 1, 3, 0]
    small_pack, small_offs = _pack(small_in)
    sm = exchange([small_pack], False, "gather_first")[0].reshape(N_DEV, -1)
    e_in = even_w_in.shape[1] * N_DEV
    e_pad = -(-e_in // LANE) * LANE
    lng_f, lnb_f, gconv_f, sconv_f, cconv_f, fconv_f, c_all = [
        _unshard(sm[:, o:o + a.size].reshape((N_DEV,) + a.shape), ax) for a, o, ax in zip(small_in, small_offs, small_axes)]
    gw8 = _pad_rows(gconv_f, 8)
    sw8 = _pad_rows(sconv_f, 8)
    cw32 = _pad_rows(cconv_f, 32)
    fw16 = [_pad_rows(fconv_f[l].reshape(9, D_FF), 16) for l in range(DEPTH)]

    a_raw = jnp.concatenate([c_all, c_ctx[None], jnp.zeros((7, d), f32)], 0)
    ncol = ada_w.shape[2]
    ada_b_loc = lax.dynamic_slice_in_dim(ada_b, me * ncol, ncol, 1)[:, None, :]
    modpart = ada_forward(a_raw, ada_w, ada_b_loc, "ada_forward")
    mod_send = jnp.stack([jnp.transpose(modpart[:, :N_DEV], (1, 0, 2)),
                          jnp.broadcast_to(modpart[:, N_DEV][None], (N_DEV, DEPTH, ncol))], axis=2)
    mod_recv, token = exchange([mod_send], True, "scatter_mod")
    gather_e, token = exchange_start([even_w_in.astype(bf16) + token[0, 0].astype(bf16)], False, "gather_even_in_start")
    wire_l0 = [even_w_out.astype(bf16) + token[0, 0].astype(bf16), ffn_w_up[0].astype(bf16), ffn_w_down[0].astype(bf16)]
    gather_l0, token = exchange_start(wire_l0, False, "gather_l0_start")
    wire_l1 = [odd_w_in.astype(bf16) + token[0, 0].astype(bf16), odd_w_out.astype(bf16), ffn_w_up[1].astype(bf16),
               ffn_w_down[1].astype(bf16)]
    gather_l1, token = exchange_start(wire_l1, False, "gather_l1_start")
    mod_recv = mod_recv + token[0, 0]
    mod = jnp.transpose(mod_recv[:, :, 0, :], (1, 0, 2)).reshape(DEPTH, 6 * d)
    modc = mod_recv[:, 0, 1, :].reshape(6 * d)
    sh_c, sc_c = modc[None, :d], modc[None, d:2 * d]
    mods = [_rows(mod[l], 6) for l in range(DEPTH)]
    lng = [[lng_f[l, j][None] for j in range(2)] for l in range(DEPTH)]
    lnb = [[lnb_f[l, j][None] for j in range(2)] for l in range(DEPTH)]

    neg_a = jnp.zeros((1, LANE), f32).at[0, 8:16].set(-jnp.exp(gdn_a_log).reshape(8))
    dt_row = jnp.zeros((1, LANE), f32).at[0, 8:16].set(gdn_dt_bias.reshape(8))
    nw_row, ps_row = gdn_norm_w[None], pool_scale[None]
    cg_row, cb_row = conf_ln_g[None], conf_ln_b[None]
    q_scale = GDN_DK ** -0.5

    sh_m, sc_m, gt_m, sh_f, sc_f, gt_f = mods[0]
    u0 = modulate(x, sc_m, sh_m, "mod_l0_mix")
    cu = modulate(ctx, sc_c, sh_c, "mod_ctx")
    sent, landed, _ = exchange_wait(gather_e, u0, "gather_even_in_wait")
    win_e = jnp.pad(_unshard(place_own(landed, sent, False, me)[0], 1), ((0, 0), (0, e_pad - e_in)))
    p0 = matmul(u0, win_e, "nn", f32, "even_in")
    pc = matmul(cu, win_e, "nn", f32, "even_in_ctx")
    qn = gdn_conv(p0, gw8, 0, 4, q_scale, "gdn_conv_q")
    kn = gdn_conv(p0, gw8, 4, 4, 1.0, "gdn_conv_k")
    vv = gdn_conv(p0, gw8, 8, 4, None, "gdn_conv_v")
    kc = gdn_conv(pc, gw8, 4, 4, 1.0, "gdn_conv_k_ctx")
    vc = gdn_conv(pc, gw8, 8, 4, None, "gdn_conv_v_ctx")
    bg = gdn_gates(p0, neg_a, dt_row, "gdn_gates")
    bgc = gdn_gates(pc, neg_a, dt_row, "gdn_gates_ctx")
    bgt, bgtc = _gate_rows(bg), _gate_rows(bgc)
    zero_state = jnp.zeros((2, GDN_HEADS, LANE, LANE), f32)
    _, _, saved_c, sfin_c = gdn_forward(kc, kc, vc, bgc, bgtc, zero_state, False, "gdn_fwd_ctx")
    o_f, o_b, saved, _ = gdn_forward(qn, kn, vv, bg, bgt, sfin_c, True, "gdn_fwd")
    mix0 = jnp.concatenate([gated_rmsnorm(o_f, o_b, p0, nw_row, "gated_rmsnorm"),
                            pool_mix(p0, pool_w, ps_row, "pool_mix")], 1)
    sent, landed, _ = exchange_wait(gather_l0, mix0, "gather_l0_wait")
    full = place_own(landed, sent, False, me)
    wout_e, wup, wdown = _unshard(full[0], 0), [_unshard(full[1], 1)], [_unshard(full[2], 0)]
    y0 = matmul(mix0, wout_e, "nn", f32, "even_out")
    x1, u1 = res_layernorm(x, y0, gt_m, lng[0][0], lnb[0][0], "resln_l0_mix", sc_f, sh_f)
    h0 = matmul(u1, wup[0], "nn", f32, "ffn_up_l0")
    f0 = ffn_conv(h0, fw16[0], "ffn_conv_l0")
    y0f = matmul(f0, wdown[0], "nn", f32, "ffn_down_l0")
    sh_m1, sc_m1, gt_m1, sh_f1, sc_f1, gt_f1 = mods[1]
    x2, u2 = res_layernorm(x1, y0f, gt_f, lng[0][1], lnb[0][1], "resln_l0_ffn", sc_m1, sh_m1)

    sent, landed, _ = exchange_wait(gather_l1, x2, "gather_l1_wait")
    full = place_own(landed, sent, False, me)
    win_o, wout_o = _unshard(full[0], 1), _unshard(full[1], 0)
    wup.append(_unshard(full[2], 1))
    wdown.append(_unshard(full[3], 0))
    p1 = matmul(u2, win_o, "nn", f32, "odd_in")
    zc = conf_conv(p1, cw32, "conf_conv")
    mix1 = jnp.concatenate([short_conv(p1, sw8, "short_conv"), ln_silu(zc, cg_row, cb_row, "conf_ln_silu")], 1)
    y1 = matmul(mix1, wout_o, "nn", f32, "odd_out")
    x3, u3 = res_layernorm(x2, y1, gt_m1, lng[1][0], lnb[1][0], "resln_l1_mix", sc_f1, sh_f1)
    h1 = matmul(u3, wup[1], "nn", f32, "ffn_up_l1")
    f1 = ffn_conv(h1, fw16[1], "ffn_conv_l1")
    y1f = matmul(f1, wdown[1], "nn", f32, "ffn_down_l1")
    loss_row, dxr, dy, dgt_f1, dlg, dlb = res_layernorm_loss(x3, y1f, gt_f1, lng[1][1], lnb[1][1], target, "resln_l1_ffn_loss")
    loss = lax.psum(loss_row[0, 0], ("x", "y", "c"))

    def ffn_backward(dy, u, h, f, l):
        df = matmul(dy, wdown[l], "nt", f32, f"ffn_down_dgrad_l{l}")
        g_down = matmul(f, dy, "tn", bf16, f"ffn_down_wgrad_l{l}")
        dh, dcw = ffn_conv_bwd(h, fw16[l], df, f"ffn_conv_bwd_l{l}")
        du = matmul(dh, wup[l], "nt", f32, f"ffn_up_dgrad_l{l}")
        g_up = matmul(u, dh, "tn", bf16, f"ffn_up_wgrad_l{l}")
        return du, dcw, g_up, g_down

    dln_f1 = (dlg, dlb)
    du, dfcw1, g_up1, g_down1 = ffn_backward(dy, u3, h1, f1, 1)

    scatter_a, token = exchange_start([_shard_major(g_up1, 1), _shard_major(g_down1, 0)], True, "scatter_l1_ffn_start")
    gt_m1 = gt_m1 + token[0:1, 0:1]

    dxr, dy, dsc, dsh, dgt, dlg, dlb = modulate_res_layernorm_bwd(
        du, sc_f1, dxr, x2, y1, gt_m1, lng[1][0], lnb[1][0], "mod_resln_bwd_l1_mix")
    dmod_f1 = (dsh, dsc, dgt_f1)
    dln_m1 = (dlg, dlb)
    dmix = matmul(dy, wout_o, "nt", f32, "odd_out_dgrad")
    g_wout_o = matmul(mix1, dy, "tn", bf16, "odd_out_wgrad")
    dgb, dgc, dhh, d_sconv = short_conv_bwd(p1, sw8, dmix, "short_conv_bwd")
    dzc, d_cg, d_cb = ln_silu_bwd(zc, cg_row, cb_row, dmix, "conf_ln_silu_bwd")
    dga, dgbb, d_cconv = conf_conv_bwd(p1, cw32, dzc, "conf_conv_bwd")
    dp1 = jnp.concatenate([dgb, dgc, dhh, dga, dgbb], 1)
    du = matmul(dp1, win_o, "nt", f32, "odd_in_dgrad")
    g_win_o = matmul(u2, dp1, "tn", bf16, "odd_in_wgrad")
    dgt_m1 = dgt
    dxr, dy, dsc, dsh, dgt_f0, dlg, dlb = modulate_res_layernorm_bwd(
        du, sc_m1, dxr, x1, y0f, gt_f, lng[0][1], lnb[0][1], "mod_resln_bwd_l0_ffn")
    dmod_m1 = (dsh, dsc, dgt_m1)
    dln_f0 = (dlg, dlb)
    du, dfcw0, g_up0, g_down0 = ffn_backward(dy, u1, h0, f0, 0)

    dxr, dy, dsc, dsh, dgt, dlg, dlb = modulate_res_layernorm_bwd(
        du, sc_f, dxr, x, y0, gt_m, lng[0][0], lnb[0][0], "mod_resln_bwd_l0_mix")
    dmod_f0 = (dsh, dsc, dgt_f0)
    dln_m0 = (dlg, dlb)
    dmix = matmul(dy, wout_e, "nt", f32, "even_out_dgrad")
    g_wout_e = matmul(mix0, dy, "tn", bf16, "even_out_wgrad")
    scatter_b, token = exchange_start(
        [_shard_major(g_win_o, 1), _shard_major(g_wout_o, 0), _shard_major(g_up0, 1), _shard_major(g_down0, 0),
         _shard_major(g_wout_e, 0)], True, "scatter_mid_start")
    d_o, dgate, d_nw = gated_rmsnorm_bwd(o_f, o_b, p0, nw_row + token[0:1, 0:1], dmix, "gated_rmsnorm_bwd")
    dpool, d_pw, d_ps = pool_mix_bwd(p0, pool_w, ps_row, dmix, "pool_mix_bwd")
    small_early = [d_nw, d_pw, d_ps, d_sconv[:3], d_cconv[:31], d_cg, d_cb, jnp.stack([dfcw0[:9], dfcw1[:9]])]
    epack, eoffs = _pack(small_early)
    gather_early, token = exchange_start([epack], False, "gather_small_early_start")
    dq_f, dq_b, dk_f, dk_b, dv_f, dv_b, dbg_f, dbg_b, ds0 = gdn_backward(
        qn, kn, vv, bg, bgt, saved, d_o, zero_state + token[0, 0], True, "gdn_bwd")
    _, _, dkc_f, dkc_b, dvc_f, dvc_b, dbgc_f, dbgc_b, _ = gdn_backward(
        kc, kc, vc, bgc, bgtc, saved_c, jnp.zeros((tc, 512), f32), ds0, False, "gdn_bwd_ctx")
    dqp, dwq = gdn_conv_bwd(p0, gw8, dq_f, dq_b, 0, 4, q_scale, "gdn_conv_q_bwd")
    dkp, dwk = gdn_conv_bwd(p0, gw8, dk_f, dk_b, 4, 4, 1.0, "gdn_conv_k_bwd")
    dvp, dwv = gdn_conv_bwd(p0, gw8, dv_f, dv_b, 8, 4, None, "gdn_conv_v_bwd")
    dkcp, dwkc = gdn_conv_bwd(pc, gw8, dkc_f, dkc_b, 4, 4, 1.0, "gdn_conv_k_ctx_bwd")
    dvcp, dwvc = gdn_conv_bwd(pc, gw8, dvc_f, dvc_b, 8, 4, None, "gdn_conv_v_ctx_bwd")
    ds_l, da_l, ddt_l = gdn_gates_bwd(p0, neg_a, dt_row, dbg_f, dbg_b, "gdn_gates_bwd")
    ds_c, da_c, ddt_c = gdn_gates_bwd(pc, neg_a, dt_row, dbgc_f, dbgc_b, "gdn_gates_ctx_bwd")
    zc512 = jnp.zeros((tc, 512), bf16)
    dp0 = jnp.concatenate([dqp, dkp, dvp, dgate, dpool, ds_l], 1)
    dpc = jnp.concatenate([zc512, dkcp, dvcp, zc512, zc512, ds_c], 1)
    du0 = matmul(dp0, win_e, "nt", f32, "even_in_dgrad")
    duc = matmul(dpc, win_e, "nt", f32, "even_in_ctx_dgrad")
    g_win_e = matmul(u0, dp0, "tn", bf16, "even_in_wgrad", init=matmul(cu, dpc, "tn", f32, "even_in_ctx_wgrad"))[:, :e_in]
    scatter_c, token = exchange_start([_shard_major(g_win_e, 1)], True, "scatter_last_start")
    grad_x, dsc, dsh = modulate_bwd(du0, x, sc_m + token[0:1, 0:1], dxr, "mod_bwd_l0_mix")
    dmod_m0 = (dsh, dsc, dgt)
    _, dsc_c, dsh_c = modulate_bwd(duc, ctx, sc_c, jnp.zeros((tc, d), f32), "mod_bwd_ctx")

    grads, delta, new_m, new_v = {}, {}, {}, {}

    def update(n, parts, w, m, v):
        cols = w.shape[-1]
        out = adamw(parts.reshape(parts.shape[0], -1, cols), w.reshape(-1, cols), m.reshape(-1, cols), v.reshape(-1, cols), f"adamw_{n}")
        return [a.reshape(w.shape) for a in out]

    sent, landed, _ = exchange_wait(scatter_a, grad_x, "scatter_l1_ffn_wait")
    recv_a = place_own(landed, sent, True, me)
    sent, landed, _ = exchange_wait(scatter_b, grad_x, "scatter_mid_wait")
    recv_b = place_own(landed, sent, True, me)
    for n, parts in (("odd_w_in", recv_b[0]), ("odd_w_out", recv_b[1])):
        grads[n], delta[n], new_m[n], new_v[n] = update(n, parts, weights[n], mom1[n], mom2[n])
    for n, per_layer in (("ffn_w_up", (recv_b[2], recv_a[0])), ("ffn_w_down", (recv_b[3], recv_a[1]))):
        grads[n], delta[n], new_m[n], new_v[n] = adamw_layers(list(per_layer), weights[n], mom1[n], mom2[n], f"adamw_{n}")
    sent, landed, token = exchange_wait(scatter_c, new_v["ffn_w_down"], "scatter_last_wait")
    recv_c = place_own(landed, sent, True, me)
    for n, parts in (("even_w_in", recv_c[0]), ("even_w_out", recv_b[4])):
        grads[n], delta[n], new_m[n], new_v[n] = update(n, parts, weights[n], mom1[n], mom2[n])

    dmod0 = jnp.concatenate(dmod_m0 + dmod_f0, 1)
    dmod1 = jnp.concatenate(dmod_m1 + dmod_f1, 1)
    dmodc = jnp.concatenate([dsh_c, dsc_c], 1)
    d_gconv = jnp.concatenate([dwq, dwk + dwkc, dwv + dwvc], 1)[:5]
    small_late = [dmod0, dmod1, dmodc,
                  jnp.concatenate([dln_m0[0], dln_f0[0], dln_m1[0], dln_f1[0]], 0),
                  jnp.concatenate([dln_m0[1], dln_f0[1], dln_m1[1], dln_f1[1]], 0),
                  d_gconv, (da_l + da_c)[0, 8:16], (ddt_l + ddt_c)[0, 8:16]]
    gpack, goffs = _pack(small_late)
    gparts = exchange([gpack + token[0:1]], False, "gather_small_grads")[0]
    sent, landed, _ = exchange_wait(gather_early, gparts, "gather_small_early_wait")
    eparts = place_own(landed, sent, False, me)[0]
    gsum = sum_parts(gparts, "sum_small_grads").reshape(-1)
    esum = sum_parts(eparts, "sum_small_early").reshape(-1)
    gs = ([gsum[o:o + a.size].reshape(a.shape) for a, o in zip(small_late, goffs)]
          + [esum[o:o + a.size].reshape(a.shape) for a, o in zip(small_early, eoffs)])
    gflat = gparts.reshape(N_DEV, -1)
    dmodc_cols = _my_block(jnp.pad(gs[2], ((0, 0), (0, 4 * d))), 1, me)
    dm = jnp.stack([
        jnp.concatenate([_my_block(gflat[:, goffs[0]:goffs[0] + 6 * d], 1, me), dmodc_cols, jnp.zeros((7, ncol), f32)], 0),
        jnp.concatenate([_my_block(gflat[:, goffs[1]:goffs[1] + 6 * d], 1, me), jnp.zeros((8, ncol), f32)], 0)])
    g_ada_w, dcc = ada_backward(a_raw, ada_w, dm, "ada_backward")
    g_cctx = cctx_grad(exchange([dcc], False, "gather_cctx")[0], c_ctx[None], "cctx_grad")

    grads["c_ctx"] = g_cctx.reshape(c_ctx.shape)
    grads["ada_b"] = jnp.concatenate([gs[0] + jnp.pad(gs[2], ((0, 0), (0, 4 * d))), gs[1]], 0)
    grads["ln_g"] = _my_block(gs[3].reshape(DEPTH, 2, d), 2, me)
    grads["ln_b"] = _my_block(gs[4].reshape(DEPTH, 2, d), 2, me)
    grads["gdn_conv_w"] = _my_block(gs[5], 1, me)
    grads["gdn_a_log"] = gs[6].reshape(2, GDN_HEADS)
    grads["gdn_dt_bias"] = gs[7].reshape(2, GDN_HEADS)
    grads["gdn_norm_w"] = gs[8].reshape(LANE)
    grads["pool_w"] = gs[9]
    grads["pool_scale"] = gs[10].reshape(-1)
    grads["sconv_w"] = _my_block(gs[11], 1, me)
    grads["conf_conv_w"] = _my_block(gs[12], 1, me)
    grads["conf_ln_g"] = gs[13].reshape(-1)
    grads["conf_ln_b"] = gs[14].reshape(-1)
    grads["ffn_conv_w"] = _my_block(gs[15].reshape(DEPTH, 3, 3, D_FF), 3, me)

    def as2d(a):
        return a.reshape(-1, a.shape[-1]) if a.ndim > 1 else a.reshape(1, -1)

    small_names = [n for n in order if n in grads and n not in delta]
    res = adamw_small([(as2d(grads[n]), as2d(weights[n]), as2d(mom1[n]), as2d(mom2[n])) for n in small_names], "adamw_small")
    for n, (dl, nm, nv) in zip(small_names, res):
        delta[n], new_m[n], new_v[n] = (a.reshape(weights[n].shape) for a in (dl, nm, nv))
    grads["ada_w"], delta["ada_w"], new_m["ada_w"], new_v["ada_w"] = update("ada_w", g_ada_w[None], ada_w, m_ada_w, v_ada_w)

    return (loss, grad_x[None], *[grads[n] for n in order], *[delta[n] for n in order],
            *[new_m[n] for n in order], *[new_v[n] for n in order])
```

```python
import functools
import math

import jax
import jax.numpy as jnp
from jax import lax
from jax.experimental import pallas as pl
from jax.experimental.pallas import tpu as pltpu

f32 = jnp.float32
bf16 = jnp.bfloat16
SDS = jax.ShapeDtypeStruct

N_DEV = 8
D_MODEL = 1024
DEPTH = 2
GRID_W = 64
GDN_HEADS = 4
GDN_DK = 128
CHUNK = 64
POOL_WINDOWS = (2, 4, 8, 16)
D_FF = 2816
ALPHA = (2 * DEPTH) ** 0.25
LN_EPS = 1e-5
RMS_EPS = 1e-6
LANE = 128
PAD_ROWS = 72
CONV_ROWS = 256
VMEM_LIMIT = 56 * 2**20

ADAM_LR, ADAM_B1, ADAM_B2, ADAM_EPS, ADAM_WD, ADAM_STEP = 0.001, 0.9, 0.999, 1e-08, 0.01, 10

HI = lax.Precision.HIGHEST


def _cparams(sem=None):
    return pltpu.CompilerParams(dimension_semantics=sem, vmem_limit_bytes=VMEM_LIMIT)


def _silu(x):
    return x * jax.nn.sigmoid(x)


def _dsilu(x):
    s = jax.nn.sigmoid(x)
    return s * (1.0 + x * (1.0 - s))


def _dotb(a, b, dims=(((1,), (0,)), ((), ()))):
    return lax.dot_general(a.astype(bf16), b.astype(bf16), dims, preferred_element_type=f32)


def _dotb_nt(a, b):
    return _dotb(a, b, (((1,), (1,)), ((), ())))


def _dotb_tn(a, b):
    return _dotb(a, b, (((0,), (0,)), ((), ())))


def _dotf(a, b, dims=(((1,), (0,)), ((), ()))):
    return lax.dot_general(a, b, dims, preferred_element_type=f32, precision=HI)


def _pick(n, cands):
    for c in cands:
        if n % c == 0:
            return c
    return n


def matmul(a, b, mode, out_dtype, name, init=None):
    if mode == "nn":
        (M, K), N = a.shape, b.shape[1]
    elif mode == "nt":
        (M, K), N = a.shape, b.shape[0]
    else:
        (K, M), N = a.shape, b.shape[1]
    tm = _pick(M, (1024, 768, 512, 256, 128)) if mode != "tn" else _pick(M, (1024, 1408, 512, 256, 128))
    tn = _pick(N, (1024, 1408, 896, 768, 640, 512, 384, 256, 128))
    tk = _pick(K, (1024, 1408, 896, 768, 640, 512, 384, 256, 128)) if mode != "tn" else _pick(K, (1024, 512, 256))
    nk = K // tk
    dims = {"nn": (((1,), (0,)), ((), ())), "nt": (((1,), (1,)), ((), ())), "tn": (((0,), (0,)), ((), ()))}[mode]

    def body(a_ref, b_ref, *rest):
        o_ref, acc_ref = rest[-2:]
        k = pl.program_id(2)
        part = lax.dot_general(a_ref[...].astype(bf16), b_ref[...].astype(bf16), dims, preferred_element_type=f32)

        @pl.when(k == 0)
        def _():
            acc_ref[...] = part if init is None else part + rest[0][...]

        @pl.when(k > 0)
        def _():
            acc_ref[...] += part

        @pl.when(k == nk - 1)
        def _():
            o_ref[...] = acc_ref[...].astype(out_dtype)

    a_spec = {"nn": pl.BlockSpec((tm, tk), lambda i, j, k: (i, k)),
              "nt": pl.BlockSpec((tm, tk), lambda i, j, k: (i, k)),
              "tn": pl.BlockSpec((tk, tm), lambda i, j, k: (k, i))}[mode]
    b_spec = {"nn": pl.BlockSpec((tk, tn), lambda i, j, k: (k, j)),
              "nt": pl.BlockSpec((tn, tk), lambda i, j, k: (j, k)),
              "tn": pl.BlockSpec((tk, tn), lambda i, j, k: (k, j))}[mode]
    o_spec = pl.BlockSpec((tm, tn), lambda i, j, k: (i, j))
    return pl.pallas_call(
        body, out_shape=SDS((M, N), out_dtype), grid=(M // tm, N // tn, nk),
        in_specs=[a_spec, b_spec] + ([] if init is None else [o_spec]), out_specs=o_spec,
        scratch_shapes=[pltpu.VMEM((tm, tn), f32)], name=name,
        compiler_params=_cparams(("parallel", "parallel", "arbitrary")),
    )(*((a, b) if init is None else (a, b, init)))


def _row_tile(t):
    return _pick(t, (512, 256, 128, 64, 32, 16, 8))


def _row_spec(tt, d):
    return pl.BlockSpec((tt, d), lambda i: (i, 0))


def _vec_spec(d):
    return pl.BlockSpec((1, d), lambda i: (0, 0))


def _acc_rows(ref, val):
    @pl.when(pl.program_id(0) == 0)
    def _():
        ref[...] = val

    @pl.when(pl.program_id(0) > 0)
    def _():
        ref[...] += val


def modulate(x, scale, shift, name):
    t, d = x.shape
    tt = _row_tile(t)

    def body(x_ref, sc_ref, sh_ref, o_ref):
        o_ref[...] = (x_ref[...] * (1.0 + sc_ref[...]) + sh_ref[...]).astype(bf16)

    return pl.pallas_call(
        body, out_shape=SDS((t, d), bf16), grid=(t // tt,),
        in_specs=[_row_spec(tt, d), _vec_spec(d), _vec_spec(d)], out_specs=_row_spec(tt, d),
        name=name, compiler_params=_cparams(("parallel",)),
    )(x, scale, shift)


def modulate_bwd(du, x, scale, dres, name, du_row0=0):
    t, d = x.shape
    tt = _row_tile(t)
    blk0 = du_row0 // tt

    def body(du_ref, x_ref, sc_ref, dres_ref, dx_ref, dsc_ref, dsh_ref):
        du_v = du_ref[...]
        dx_ref[...] = du_v * (1.0 + sc_ref[...]) + dres_ref[...]
        _acc_rows(dsc_ref, jnp.sum(du_v * x_ref[...], axis=0, keepdims=True))
        _acc_rows(dsh_ref, jnp.sum(du_v, axis=0, keepdims=True))

    return pl.pallas_call(
        body, out_shape=(SDS((t, d), f32), SDS((1, d), f32), SDS((1, d), f32)), grid=(t // tt,),
        in_specs=[pl.BlockSpec((tt, d), lambda i: (i + blk0, 0)), _row_spec(tt, d), _vec_spec(d), _row_spec(tt, d)],
        out_specs=(_row_spec(tt, d), _vec_spec(d), _vec_spec(d)),
        name=name, compiler_params=_cparams(("arbitrary",)),
    )(du, x, scale, dres)


def _ln_stats(z):
    mu = jnp.mean(z, axis=-1, keepdims=True)
    zc = z - mu
    var = jnp.mean(zc * zc, axis=-1, keepdims=True)
    rstd = lax.rsqrt(var + LN_EPS)
    return zc * rstd, rstd


def _ln_bwd(dxhat, xhat, rstd):
    m1 = jnp.mean(dxhat, axis=-1, keepdims=True)
    m2 = jnp.mean(dxhat * xhat, axis=-1, keepdims=True)
    return rstd * (dxhat - m1 - xhat * m2)


def res_layernorm(x, y, gate, g, b, name, scale=None, shift=None):
    t, d = x.shape
    tt = _row_tile(t)
    with_mod = scale is not None

    def body(x_ref, y_ref, gt_ref, g_ref, b_ref, *rest):
        xhat, _ = _ln_stats(ALPHA * x_ref[...] + gt_ref[...] * y_ref[...])
        out = xhat * g_ref[...] + b_ref[...]
        if with_mod:
            sc_ref, sh_ref, o_ref, u_ref = rest
            u_ref[...] = (out * (1.0 + sc_ref[...]) + sh_ref[...]).astype(bf16)
        else:
            o_ref, = rest
        o_ref[...] = out

    rows, vec = _row_spec(tt, d), _vec_spec(d)
    return pl.pallas_call(
        body, out_shape=(SDS((t, d), f32), SDS((t, d), bf16)) if with_mod else SDS((t, d), f32), grid=(t // tt,),
        in_specs=[rows, rows, vec, vec, vec] + ([vec, vec] if with_mod else []),
        out_specs=(rows, rows) if with_mod else rows, name=name, compiler_params=_cparams(("parallel",)),
    )(*((x, y, gate, g, b) + ((scale, shift) if with_mod else ())))


def modulate_res_layernorm_bwd(du, scale, dres, x, y, gate, g, b, name):
    t, d = x.shape
    tt = _row_tile(t)

    def body(du_ref, sc_ref, dres_ref, x_ref, y_ref, gt_ref, g_ref, b_ref,
             dxr_ref, dy_ref, dsc_ref, dsh_ref, dgt_ref, dg_ref, db_ref):
        y_v, du_v = y_ref[...], du_ref[...]
        xhat, rstd = _ln_stats(ALPHA * x_ref[...] + gt_ref[...] * y_v)
        do_v = du_v * (1.0 + sc_ref[...]) + dres_ref[...]
        dz = _ln_bwd(do_v * g_ref[...], xhat, rstd)
        dxr_ref[...] = ALPHA * dz
        dy_ref[...] = (gt_ref[...] * dz).astype(bf16)
        _acc_rows(dsc_ref, jnp.sum(du_v * (xhat * g_ref[...] + b_ref[...]), axis=0, keepdims=True))
        _acc_rows(dsh_ref, jnp.sum(du_v, axis=0, keepdims=True))
        _acc_rows(dgt_ref, jnp.sum(dz * y_v, axis=0, keepdims=True))
        _acc_rows(dg_ref, jnp.sum(do_v * xhat, axis=0, keepdims=True))
        _acc_rows(db_ref, jnp.sum(do_v, axis=0, keepdims=True))

    rows, vec, vshape = _row_spec(tt, d), _vec_spec(d), SDS((1, d), f32)
    return pl.pallas_call(
        body, out_shape=(SDS((t, d), f32), SDS((t, d), bf16)) + (vshape,) * 5, grid=(t // tt,),
        in_specs=[rows, vec, rows, rows, rows, vec, vec, vec], out_specs=(rows, rows) + (vec,) * 5,
        name=name, compiler_params=_cparams(("arbitrary",)),
    )(du, scale, dres, x, y, gate, g, b)


def res_layernorm_loss(x, y, gate, g, b, target, name):
    t, d = x.shape
    tt = _row_tile(t)

    def body(x_ref, y_ref, gt_ref, g_ref, b_ref, t_ref, l_ref, dxr_ref, dy_ref, dgt_ref, dg_ref, db_ref):
        y_v = y_ref[...]
        xhat, rstd = _ln_stats(ALPHA * x_ref[...] + gt_ref[...] * y_v)
        e = xhat * g_ref[...] + b_ref[...] - t_ref[...]
        part = jnp.sum(jnp.sum(e * e, axis=1, keepdims=True), axis=0, keepdims=True) * (0.5 / d)
        _acc_rows(l_ref, jnp.broadcast_to(part, (1, LANE)))
        do_v = e * (1.0 / d)
        dz = _ln_bwd(do_v * g_ref[...], xhat, rstd)
        dxr_ref[...] = ALPHA * dz
        dy_ref[...] = (gt_ref[...] * dz).astype(bf16)
        _acc_rows(dgt_ref, jnp.sum(dz * y_v, axis=0, keepdims=True))
        _acc_rows(dg_ref, jnp.sum(do_v * xhat, axis=0, keepdims=True))
        _acc_rows(db_ref, jnp.sum(do_v, axis=0, keepdims=True))

    rows, vec, vshape = _row_spec(tt, d), _vec_spec(d), SDS((1, d), f32)
    return pl.pallas_call(
        body, out_shape=(SDS((1, LANE), f32), SDS((t, d), f32), SDS((t, d), bf16)) + (vshape,) * 3, grid=(t // tt,),
        in_specs=[rows, rows, vec, vec, vec, rows],
        out_specs=(pl.BlockSpec((1, LANE), lambda i: (0, 0)), rows, rows) + (vec,) * 3,
        name=name, compiler_params=_cparams(("arbitrary",)),
    )(x, y, gate, g, b, target)


def _fill_pad(pad_ref, val, t):
    zeros = jnp.zeros((PAD_ROWS, LANE), f32)
    pad_ref[0:PAD_ROWS, :] = zeros
    pad_ref[PAD_ROWS + t:2 * PAD_ROWS + t, :] = zeros
    pad_ref[PAD_ROWS:PAD_ROWS + t, :] = val


def _grid_pads_set(pads, r0, val):
    rows = val.shape[0]
    col = (lax.broadcasted_iota(jnp.int32, (rows, 1), 0) + r0) % GRID_W
    base = PAD_ROWS + r0
    pads[0][base + 1:base + 1 + rows, :] = val * (col <= GRID_W - 2).astype(f32)
    pads[1][base:base + rows, :] = val
    pads[2][base - 1:base - 1 + rows, :] = val * (col >= 1).astype(f32)


def _grid_pads_clear_edges(pads, t):
    zeros = jnp.zeros((PAD_ROWS + 8, LANE), f32)
    for p in pads:
        p[0:PAD_ROWS + 8, :] = zeros
        p[PAD_ROWS + t - 8:2 * PAD_ROWS + t, :] = zeros


def _tap_source(pads, dc):
    return pads if dc is None else pads[dc + 1]


def _taps_apply(pads, w_ref, taps, r0, rows):
    acc = jnp.zeros((rows, LANE), f32)
    for off, dc, wi in taps:
        xs = _tap_source(pads, dc)[PAD_ROWS + r0 + off:PAD_ROWS + r0 + off + rows, :]
        acc = acc + w_ref[wi:wi + 1, :] * xs
    return acc


def _taps_wgrad(pads, dy, taps, r0, rows, nw):
    out = jnp.zeros((nw, LANE), f32)
    rid = lax.broadcasted_iota(jnp.int32, (nw, 1), 0)
    for off, dc, wi in taps:
        xs = _tap_source(pads, dc)[PAD_ROWS + r0 + off:PAD_ROWS + r0 + off + rows, :]
        s = jnp.sum(dy * xs, axis=0, keepdims=True)
        out = out + jnp.where(rid == wi, s, 0.0)
    return out


def _transpose_taps(taps):
    return [(-off, None if dc is None else -dc, wi) for off, dc, wi in taps]


def _taps_1d(width):
    return [(j - width // 2, None, j) for j in range(width)]


def _taps_grid3():
    return [(GRID_W * dr, dc, 3 * (dr + 1) + (dc + 1)) for dr in (-1, 0, 1) for dc in (-1, 0, 1)]


def _row_chunks(t):
    r = min(CONV_ROWS, t)
    return [(i * r, r) for i in range(t // r)]


def _col_spec(t, off):
    return pl.BlockSpec((t, LANE), lambda c: (0, c + off))


def _w_spec(nw, off=0):
    return pl.BlockSpec((nw, LANE), lambda c: (0, c + off))


def gdn_conv(p, w, col0, nblk, norm_scale, name):
    t = p.shape[0]
    nw = w.shape[0]
    taps = _taps_1d(5)

    def body(p_ref, w_ref, o_ref, pad_ref):
        _fill_pad(pad_ref, p_ref[...], t)
        for r0, rows in _row_chunks(t):
            a = _silu(_taps_apply(pad_ref, w_ref, taps, r0, rows))
            if norm_scale is not None:
                a = a * (lax.rsqrt(jnp.sum(a * a, axis=-1, keepdims=True) + RMS_EPS) * norm_scale)
            o_ref[r0:r0 + rows, :] = a

    return pl.pallas_call(
        body, out_shape=SDS((t, nblk * LANE), f32), grid=(nblk,),
        in_specs=[_col_spec(t, col0), _w_spec(nw, col0)], out_specs=_col_spec(t, 0),
        scratch_shapes=[pltpu.VMEM((t + 2 * PAD_ROWS, LANE), f32)], name=name,
        compiler_params=_cparams(("parallel",)),
    )(p, w)


def gdn_conv_bwd(p, w, d_a, d_b, col0, nblk, norm_scale, name):
    t = p.shape[0]
    nw = w.shape[0]
    taps = _taps_1d(5)
    ttaps = _transpose_taps(taps)

    def body(p_ref, w_ref, da_ref, db_ref, dp_ref, dw_ref, pad_ref, gpad_ref):
        _fill_pad(pad_ref, p_ref[...], t)
        for r0, rows in _row_chunks(t):
            pre = _taps_apply(pad_ref, w_ref, taps, r0, rows)
            a = _silu(pre)
            dy = da_ref[r0:r0 + rows, :] + db_ref[r0:r0 + rows, :]
            if norm_scale is not None:
                r = lax.rsqrt(jnp.sum(a * a, axis=-1, keepdims=True) + RMS_EPS)
                da = norm_scale * (dy * r - a * (r * r * r) * jnp.sum(dy * a, axis=-1, keepdims=True))
            else:
                da = dy
            gpad_ref[PAD_ROWS + r0:PAD_ROWS + r0 + rows, :] = da * _dsilu(pre)
        zeros = jnp.zeros((PAD_ROWS, LANE), f32)
        gpad_ref[0:PAD_ROWS, :] = zeros
        gpad_ref[PAD_ROWS + t:2 * PAD_ROWS + t, :] = zeros
        dw = jnp.zeros((nw, LANE), f32)
        for r0, rows in _row_chunks(t):
            dp_ref[r0:r0 + rows, :] = _taps_apply(gpad_ref, w_ref, ttaps, r0, rows).astype(bf16)
            dw = dw + _taps_wgrad(pad_ref, gpad_ref[PAD_ROWS + r0:PAD_ROWS + r0 + rows, :], taps, r0, rows, nw)
        dw_ref[...] = dw

    return pl.pallas_call(
        body, out_shape=(SDS((t, nblk * LANE), bf16), SDS((nw, nblk * LANE), f32)), grid=(nblk,),
        in_specs=[_col_spec(t, col0), _w_spec(nw, col0), _col_spec(t, 0), _col_spec(t, 0)],
        out_specs=(_col_spec(t, 0), _w_spec(nw)),
        scratch_shapes=[pltpu.VMEM((t + 2 * PAD_ROWS, LANE), f32)] * 2, name=name,
        compiler_params=_cparams(("parallel",)),
    )(p, w, d_a, d_b)


def short_conv(p, w, name):
    t = p.shape[0]
    nw = w.shape[0]
    taps = _taps_1d(3)

    def body(gb_ref, gc_ref, h_ref, w_ref, o_ref, pad_ref):
        _fill_pad(pad_ref, gc_ref[...] * h_ref[...], t)
        for r0, rows in _row_chunks(t):
            o_ref[r0:r0 + rows, :] = (gb_ref[r0:r0 + rows, :] * _taps_apply(pad_ref, w_ref, taps, r0, rows)).astype(bf16)

    return pl.pallas_call(
        body, out_shape=SDS((t, 4 * LANE), bf16), grid=(4,),
        in_specs=[_col_spec(t, 0), _col_spec(t, 4), _col_spec(t, 8), _w_spec(nw)], out_specs=_col_spec(t, 0),
        scratch_shapes=[pltpu.VMEM((t + 2 * PAD_ROWS, LANE), f32)], name=name,
        compiler_params=_cparams(("parallel",)),
    )(p, p, p, w)


def short_conv_bwd(p, w, dy, name):
    t = p.shape[0]
    nw = w.shape[0]
    taps = _taps_1d(3)
    ttaps = _transpose_taps(taps)

    def body(gb_ref, gc_ref, h_ref, w_ref, dy_ref, dgb_ref, dgc_ref, dh_ref, dw_ref, pad_ref, gpad_ref):
        _fill_pad(pad_ref, gc_ref[...] * h_ref[...], t)
        _fill_pad(gpad_ref, dy_ref[...] * gb_ref[...], t)
        dw = jnp.zeros((nw, LANE), f32)
        for r0, rows in _row_chunks(t):
            sl = slice(r0, r0 + rows)
            dgb_ref[sl, :] = (dy_ref[sl, :] * _taps_apply(pad_ref, w_ref, taps, r0, rows)).astype(bf16)
            dm = _taps_apply(gpad_ref, w_ref, ttaps, r0, rows)
            dgc_ref[sl, :] = (dm * h_ref[sl, :]).astype(bf16)
            dh_ref[sl, :] = (dm * gc_ref[sl, :]).astype(bf16)
            dw = dw + _taps_wgrad(pad_ref, gpad_ref[PAD_ROWS + r0:PAD_ROWS + r0 + rows, :], taps, r0, rows, nw)
        dw_ref[...] = dw

    blk = SDS((t, 4 * LANE), bf16)
    return pl.pallas_call(
        body, out_shape=(blk, blk, blk, SDS((nw, 4 * LANE), f32)), grid=(4,),
        in_specs=[_col_spec(t, 0), _col_spec(t, 4), _col_spec(t, 8), _w_spec(nw), _col_spec(t, 0)],
        out_specs=(_col_spec(t, 0), _col_spec(t, 0), _col_spec(t, 0), _w_spec(nw)),
        scratch_shapes=[pltpu.VMEM((t + 2 * PAD_ROWS, LANE), f32)] * 2, name=name,
        compiler_params=_cparams(("parallel",)),
    )(p, p, p, w, dy)


def conf_conv(p, w, name):
    t = p.shape[0]
    nw = w.shape[0]
    taps = _taps_1d(31)

    def body(a_ref, b_ref, w_ref, o_ref, pad_ref):
        _fill_pad(pad_ref, a_ref[...] * jax.nn.sigmoid(b_ref[...]), t)
        for r0, rows in _row_chunks(t):
            o_ref[r0:r0 + rows, :] = _taps_apply(pad_ref, w_ref, taps, r0, rows)

    return pl.pallas_call(
        body, out_shape=SDS((t, 4 * LANE), f32), grid=(4,),
        in_specs=[_col_spec(t, 12), _col_spec(t, 16), _w_spec(nw)], out_specs=_col_spec(t, 0),
        scratch_shapes=[pltpu.VMEM((t + 2 * PAD_ROWS, LANE), f32)], name=name,
        compiler_params=_cparams(("parallel",)),
    )(p, p, w)


def conf_conv_bwd(p, w, dz, name):
    t = p.shape[0]
    nw = w.shape[0]
    taps = _taps_1d(31)
    ttaps = _transpose_taps(taps)

    def body(a_ref, b_ref, w_ref, dz_ref, da_ref, db_ref, dw_ref, pad_ref, gpad_ref):
        _fill_pad(pad_ref, a_ref[...] * jax.nn.sigmoid(b_ref[...]), t)
        _fill_pad(gpad_ref, dz_ref[...], t)
        dw = jnp.zeros((nw, LANE), f32)
        for r0, rows in _row_chunks(t):
            sl = slice(r0, r0 + rows)
            dm = _taps_apply(gpad_ref, w_ref, ttaps, r0, rows)
            sg = jax.nn.sigmoid(b_ref[sl, :])
            da_ref[sl, :] = (dm * sg).astype(bf16)
            db_ref[sl, :] = (dm * a_ref[sl, :] * sg * (1.0 - sg)).astype(bf16)
            dw = dw + _taps_wgrad(pad_ref, dz_ref[sl, :], taps, r0, rows, nw)
        dw_ref[...] = dw

    blk = SDS((t, 4 * LANE), bf16)
    return pl.pallas_call(
        body, out_shape=(blk, blk, SDS((nw, 4 * LANE), f32)), grid=(4,),
        in_specs=[_col_spec(t, 12), _col_spec(t, 16), _w_spec(nw), _col_spec(t, 0)],
        out_specs=(_col_spec(t, 0), _col_spec(t, 0), _w_spec(nw)),
        scratch_shapes=[pltpu.VMEM((t + 2 * PAD_ROWS, LANE), f32)] * 2, name=name,
        compiler_params=_cparams(("parallel",)),
    )(p, p, w, dz)


def ffn_conv(h, w, name):
    t = h.shape[0]
    width = 2 * LANE
    nblk = D_FF // width
    nw = w.shape[0]
    taps = _taps_grid3()

    def body(a_ref, g_ref, w_ref, o_ref, *pads):
        for s in range(width // LANE):
            ls = slice(s * LANE, (s + 1) * LANE)
            _grid_pads_clear_edges(pads, t)
            for r0, rows in _row_chunks(t):
                _grid_pads_set(pads, r0, a_ref[r0:r0 + rows, ls])
            for r0, rows in _row_chunks(t):
                conv = _taps_apply(pads, w_ref.at[:, ls], taps, r0, rows)
                o_ref[r0:r0 + rows, ls] = (_silu(conv) * g_ref[r0:r0 + rows, ls]).astype(bf16)

    spec = lambda off: pl.BlockSpec((t, width), lambda c: (0, c + off))
    return pl.pallas_call(
        body, out_shape=SDS((t, D_FF), bf16), grid=(nblk,),
        in_specs=[spec(0), spec(nblk), pl.BlockSpec((nw, width), lambda c: (0, c))], out_specs=spec(0),
        scratch_shapes=[pltpu.VMEM((t + 2 * PAD_ROWS, LANE), f32)] * 3, name=name,
        compiler_params=_cparams(("parallel",)),
    )(h, h, w)


def ffn_conv_bwd(h, w, df, name):
    t = h.shape[0]
    nblk = D_FF // LANE
    nw = w.shape[0]
    taps = _taps_grid3()
    ttaps = _transpose_taps(taps)

    def body(a_ref, g_ref, w_ref, df_ref, dh_ref, dw_ref, *all_pads):
        half = pl.program_id(1)
        pads, gpads = all_pads[:3], all_pads[3:]

        @pl.when(half == 0)
        def _():
            _grid_pads_clear_edges(all_pads, t)
            for r0, rows in _row_chunks(t):
                _grid_pads_set(pads, r0, a_ref[r0:r0 + rows, :])
            for r0, rows in _row_chunks(t):
                sl = slice(r0, r0 + rows)
                pre = _taps_apply(pads, w_ref, taps, r0, rows)
                _grid_pads_set(gpads, r0, df_ref[sl, :] * g_ref[sl, :] * _dsilu(pre))
                dh_ref[sl, :] = (df_ref[sl, :] * _silu(pre)).astype(bf16)

        @pl.when(half == 1)
        def _():
            dw = jnp.zeros((nw, LANE), f32)
            for r0, rows in _row_chunks(t):
                dh_ref[r0:r0 + rows, :] = _taps_apply(gpads, w_ref, ttaps, r0, rows).astype(bf16)
                dw = dw + _taps_wgrad(pads, gpads[1][PAD_ROWS + r0:PAD_ROWS + r0 + rows, :], taps, r0, rows, nw)
            dw_ref[...] = dw

    cspec = lambda off: pl.BlockSpec((t, LANE), lambda c, s: (0, c + off))
    return pl.pallas_call(
        body, out_shape=(SDS((t, 2 * D_FF), bf16), SDS((nw, D_FF), f32)), grid=(nblk, 2),
        in_specs=[cspec(0), cspec(nblk), pl.BlockSpec((nw, LANE), lambda c, s: (0, c)), cspec(0)],
        out_specs=(pl.BlockSpec((t, LANE), lambda c, s: (0, c + nblk * (1 - s))), pl.BlockSpec((nw, LANE), lambda c, s: (0, c))),
        scratch_shapes=[pltpu.VMEM((t + 2 * PAD_ROWS, LANE), f32)] * 6, name=name,
        compiler_params=_cparams(("parallel", "arbitrary")),
    )(h, h, w, df)


def _pool_count(r0, rows, win, t):
    pos = lax.broadcasted_iota(jnp.int32, (rows, 1), 0) + r0
    lo = jnp.clip(pos - win // 2, 0, t)
    hi = jnp.clip(pos - win // 2 + win, 0, t)
    return (hi - lo).astype(f32)


def _window_sum(pad_ref, r0, rows, lo, hi):
    acc = jnp.zeros((rows, LANE), f32)
    for off in range(lo, hi):
        acc = acc + pad_ref[PAD_ROWS + r0 + off:PAD_ROWS + r0 + off + rows, :]
    return acc


def pool_mix(p, pool_w, pool_scale, name):
    t = p.shape[0]

    def body(x_ref, w_ref, s_ref, o_ref, pad_ref):
        for gi, win in enumerate(POOL_WINDOWS):
            cs = slice(gi * LANE, (gi + 1) * LANE)
            _fill_pad(pad_ref, x_ref[:, cs], t)
            wg = w_ref[gi].astype(bf16)
            for r0, rows in _row_chunks(t):
                pooled = _window_sum(pad_ref, r0, rows, -(win // 2), win - win // 2) / _pool_count(r0, rows, win, t) - x_ref[r0:r0 + rows, cs]
                o_ref[r0:r0 + rows, cs] = (_dotb(pooled, wg) * s_ref[:, cs]).astype(bf16)

    return pl.pallas_call(
        body, out_shape=SDS((t, 512), bf16), grid=(1,),
        in_specs=[pl.BlockSpec((t, 512), lambda i: (0, 4)), pl.BlockSpec((4, LANE, LANE), lambda i: (0, 0, 0)),
                  pl.BlockSpec((1, 512), lambda i: (0, 0))],
        out_specs=pl.BlockSpec((t, 512), lambda i: (0, 0)),
        scratch_shapes=[pltpu.VMEM((t + 2 * PAD_ROWS, LANE), f32)], name=name,
        compiler_params=_cparams(("arbitrary",)),
    )(p, pool_w, pool_scale)


def pool_mix_bwd(p, pool_w, pool_scale, dmix, name):
    t = p.shape[0]

    def body(x_ref, w_ref, s_ref, dy_ref, dp_ref, dw_ref, ds_ref, pad_ref, gpad_ref, dpool_ref):
        for gi, win in enumerate(POOL_WINDOWS):
            cs = slice(gi * LANE, (gi + 1) * LANE)
            h = win // 2
            _fill_pad(pad_ref, x_ref[:, cs], t)
            wg = w_ref[gi].astype(bf16)
            dw = jnp.zeros((LANE, LANE), f32)
            ds = jnp.zeros((1, LANE), f32)
            zeros = jnp.zeros((PAD_ROWS, LANE), f32)
            gpad_ref[0:PAD_ROWS, :] = zeros
            gpad_ref[PAD_ROWS + t:2 * PAD_ROWS + t, :] = zeros
            for r0, rows in _row_chunks(t):
                cnt = _pool_count(r0, rows, win, t)
                pooled = _window_sum(pad_ref, r0, rows, -h, win - h) / cnt - x_ref[r0:r0 + rows, cs]
                dy = dy_ref[r0:r0 + rows, cs]
                ds = ds + jnp.sum(dy * _dotb(pooled, wg), axis=0, keepdims=True)
                dypre = dy * s_ref[:, cs]
                dw = dw + _dotb_tn(pooled, dypre)
                dpooled = _dotb_nt(dypre, wg)
                gpad_ref[PAD_ROWS + r0:PAD_ROWS + r0 + rows, :] = dpooled / cnt
                dpool_ref[r0:r0 + rows, :] = dpooled
            dw_ref[gi] = dw
            ds_ref[:, cs] = ds
            for r0, rows in _row_chunks(t):
                dx = _window_sum(gpad_ref, r0, rows, -h + 1, h + 1) - dpool_ref[r0:r0 + rows, :]
                dp_ref[r0:r0 + rows, cs] = dx.astype(bf16)

    return pl.pallas_call(
        body, out_shape=(SDS((t, 512), bf16), SDS((4, LANE, LANE), f32), SDS((1, 512), f32)), grid=(1,),
        in_specs=[pl.BlockSpec((t, 512), lambda i: (0, 4)), pl.BlockSpec((4, LANE, LANE), lambda i: (0, 0, 0)),
                  pl.BlockSpec((1, 512), lambda i: (0, 0)), pl.BlockSpec((t, 512), lambda i: (0, 1))],
        out_specs=(pl.BlockSpec((t, 512), lambda i: (0, 0)), pl.BlockSpec((4, LANE, LANE), lambda i: (0, 0, 0)),
                   pl.BlockSpec((1, 512), lambda i: (0, 0))),
        scratch_shapes=[pltpu.VMEM((t + 2 * PAD_ROWS, LANE), f32)] * 2 + [pltpu.VMEM((t, LANE), f32)], name=name,
        compiler_params=_cparams(("arbitrary",)),
    )(p, pool_w, pool_scale, dmix)


def gated_rmsnorm(o_a, o_b, p, norm_w, name):
    t = o_a.shape[0]
    tt = _row_tile(t)

    def body(oa_ref, ob_ref, g_ref, nw_ref, y_ref):
        for h in range(GDN_HEADS):
            cs = slice(h * LANE, (h + 1) * LANE)
            o = oa_ref[:, cs] + ob_ref[:, cs]
            r = lax.rsqrt(jnp.mean(o * o, axis=-1, keepdims=True) + RMS_EPS)
            y_ref[:, cs] = (o * r * nw_ref[...] * _silu(g_ref[:, cs])).astype(bf16)

    return pl.pallas_call(
        body, out_shape=SDS((t, 512), bf16), grid=(t // tt,),
        in_specs=[_row_spec(tt, 512), _row_spec(tt, 512), pl.BlockSpec((tt, 512), lambda i: (i, 3)), _vec_spec(LANE)],
        out_specs=_row_spec(tt, 512), name=name, compiler_params=_cparams(("parallel",)),
    )(o_a, o_b, p, norm_w)


def gated_rmsnorm_bwd(o_a, o_b, p, norm_w, dmix, name):
    t = o_a.shape[0]
    tt = _row_tile(t)

    def body(oa_ref, ob_ref, g_ref, nw_ref, dy_ref, do_ref, dg_ref, dnw_ref):
        dnw = jnp.zeros((1, LANE), f32)
        for h in range(GDN_HEADS):
            cs = slice(h * LANE, (h + 1) * LANE)
            o = oa_ref[:, cs] + ob_ref[:, cs]
            r = lax.rsqrt(jnp.mean(o * o, axis=-1, keepdims=True) + RMS_EPS)
            gate = g_ref[:, cs]
            dy = dy_ref[:, cs]
            dy1 = dy * _silu(gate)
            dg_ref[:, cs] = (dy * (o * r * nw_ref[...]) * _dsilu(gate)).astype(bf16)
            dnw = dnw + jnp.sum(dy1 * o * r, axis=0, keepdims=True)
            dn = dy1 * nw_ref[...]
            do_ref[:, cs] = r * dn - o * (r * r * r) * jnp.mean(dn * o, axis=-1, keepdims=True)
        _acc_rows(dnw_ref, dnw)

    return pl.pallas_call(
        body, out_shape=(SDS((t, 512), f32), SDS((t, 512), bf16), SDS((1, LANE), f32)), grid=(t // tt,),
        in_specs=[_row_spec(tt, 512), _row_spec(tt, 512), pl.BlockSpec((tt, 512), lambda i: (i, 3)), _vec_spec(LANE),
                  _row_spec(tt, 512)],
        out_specs=(_row_spec(tt, 512), _row_spec(tt, 512), _vec_spec(LANE)),
        name=name, compiler_params=_cparams(("arbitrary",)),
    )(o_a, o_b, p, norm_w, dmix)


def ln_silu(z, g, b, name):
    t, d = z.shape
    tt = _row_tile(t)

    def body(z_ref, g_ref, b_ref, o_ref):
        xhat, _ = _ln_stats(z_ref[...])
        o_ref[...] = _silu(xhat * g_ref[...] + b_ref[...]).astype(bf16)

    return pl.pallas_call(
        body, out_shape=SDS((t, d), bf16), grid=(t // tt,),
        in_specs=[_row_spec(tt, d), _vec_spec(d), _vec_spec(d)], out_specs=_row_spec(tt, d),
        name=name, compiler_params=_cparams(("parallel",)),
    )(z, g, b)


def ln_silu_bwd(z, g, b, dmix, name):
    t, d = z.shape
    tt = _row_tile(t)

    def body(z_ref, g_ref, b_ref, dy_ref, dz_ref, dg_ref, db_ref):
        xhat, rstd = _ln_stats(z_ref[...])
        dn = dy_ref[...] * _dsilu(xhat * g_ref[...] + b_ref[...])
        dz_ref[...] = _ln_bwd(dn * g_ref[...], xhat, rstd)
        _acc_rows(dg_ref, jnp.sum(dn * xhat, axis=0, keepdims=True))
        _acc_rows(db_ref, jnp.sum(dn, axis=0, keepdims=True))

    return pl.pallas_call(
        body, out_shape=(SDS((t, d), f32), SDS((1, d), f32), SDS((1, d), f32)), grid=(t // tt,),
        in_specs=[_row_spec(tt, d), _vec_spec(d), _vec_spec(d), pl.BlockSpec((tt, d), lambda i: (i, 1))],
        out_specs=(_row_spec(tt, d), _vec_spec(d), _vec_spec(d)),
        name=name, compiler_params=_cparams(("arbitrary",)),
    )(z, g, b, dmix)


def gdn_gates(p, neg_a, dt_bias, name):
    t = p.shape[0]
    tt = _row_tile(t)

    def body(s_ref, na_ref, dt_ref, o_ref):
        s = s_ref[...]
        col = lax.broadcasted_iota(jnp.int32, s.shape, 1)
        o_ref[...] = jnp.where(col < 8, jax.nn.sigmoid(s), na_ref[...] * jax.nn.softplus(s + dt_ref[...]))

    return pl.pallas_call(
        body, out_shape=SDS((t, LANE), f32), grid=(t // tt,),
        in_specs=[pl.BlockSpec((tt, LANE), lambda i: (i, 20)), _vec_spec(LANE), _vec_spec(LANE)],
        out_specs=_row_spec(tt, LANE), name=name, compiler_params=_cparams(("parallel",)),
    )(p, neg_a, dt_bias)


def gdn_gates_bwd(p, neg_a, dt_bias, dbg_a, dbg_b, name):
    t = p.shape[0]
    tt = _row_tile(t)

    def body(s_ref, na_ref, dt_ref, d_ref, d2_ref, ds_ref, da_ref, ddt_ref):
        s = s_ref[...]
        d = d_ref[...] + d2_ref[...]
        col = lax.broadcasted_iota(jnp.int32, s.shape, 1)
        sg = jax.nn.sigmoid(s)
        z = s + dt_ref[...]
        dz = jnp.where((col >= 8) & (col < 16), d * na_ref[...] * jax.nn.sigmoid(z), 0.0)
        ds_ref[...] = jnp.where(col < 8, d * sg * (1.0 - sg), dz).astype(bf16)
        dalog = jnp.where((col >= 8) & (col < 16), d * na_ref[...] * jax.nn.softplus(z), 0.0)
        _acc_rows(da_ref, jnp.sum(dalog, axis=0, keepdims=True))
        _acc_rows(ddt_ref, jnp.sum(dz, axis=0, keepdims=True))

    return pl.pallas_call(
        body, out_shape=(SDS((t, LANE), bf16), SDS((1, LANE), f32), SDS((1, LANE), f32)), grid=(t // tt,),
        in_specs=[pl.BlockSpec((tt, LANE), lambda i: (i, 20)), _vec_spec(LANE), _vec_spec(LANE), _row_spec(tt, LANE),
                  _row_spec(tt, LANE)],
        out_specs=(_row_spec(tt, LANE), _vec_spec(LANE), _vec_spec(LANE)),
        name=name, compiler_params=_cparams(("arbitrary",)),
    )(p, neg_a, dt_bias, dbg_a, dbg_b)


N_SCAN = 2 * GDN_HEADS


def _bdot(a, b, ca, cb, precision=None):
    if precision is None:
        a, b = a.astype(bf16), b.astype(bf16)
    return lax.dot_general(a, b, (((ca,), (cb,)), ((0,), (0,))), preferred_element_type=f32, precision=precision)


def _bdot_nn(a, b, precision=None):
    return _bdot(a, b, 2, 1, precision)


def _bdot_nt(a, b):
    return _bdot(a, b, 2, 2)


def _bdot_tn(a, b, precision=None):
    return _bdot(a, b, 1, 1, precision)


def _order_masks():
    shape = (N_SCAN, CHUNK, CHUNK)
    sign = jnp.where(lax.broadcasted_iota(jnp.int32, shape, 0) >= GDN_HEADS, -1, 1)
    ahead = (lax.broadcasted_iota(jnp.int32, shape, 1) - lax.broadcasted_iota(jnp.int32, shape, 2)) * sign
    lower, strict, lower_t = ahead >= 0, ahead > 0, ahead <= 0
    col_shape = (N_SCAN, CHUNK, 1)
    back1 = lax.broadcasted_iota(jnp.int32, col_shape, 0) >= GDN_HEADS
    row1 = lax.broadcasted_iota(jnp.int32, col_shape, 1)
    at_last = (row1 == jnp.where(back1, 0, CHUNK - 1)).astype(f32)
    return lower, strict, lower_t, at_last


def _stack_heads(f_ref, b_ref):
    return jnp.stack([ref[:, h * LANE:(h + 1) * LANE] for ref in (f_ref, b_ref) for h in range(GDN_HEADS)])


def _stack_gates(bgf, bgb, bgtf, bgtb):
    beta = jnp.stack([bg[:, 4 * d + h:4 * d + h + 1] for d, bg in enumerate((bgf, bgb)) for h in range(GDN_HEADS)])
    g_col = jnp.stack([bg[:, 8 + 4 * d + h:9 + 4 * d + h] for d, bg in enumerate((bgf, bgb)) for h in range(GDN_HEADS)])
    g_row = jnp.stack([bgt[8 + 4 * d + h:9 + 4 * d + h, :] for d, bgt in enumerate((bgtf, bgtb)) for h in range(GDN_HEADS)])
    return beta, g_col, g_row


def _chunk_terms(k, v, beta, g_col, g_row, masks, tinv=None):
    lower, strict, lower_t, at_last = masks
    gc = jnp.sum(lower.astype(f32) * g_row, axis=2, keepdims=True)
    gr = jnp.sum(lower_t.astype(f32) * g_col, axis=1, keepdims=True)
    g_last = jnp.sum(at_last * gc, axis=1, keepdims=True)
    e = jnp.exp(gc)
    f = jnp.exp(g_last - gc)
    dm = jnp.exp(jnp.where(lower, gc - gr, -1e30))
    kb = k * beta
    kk = _bdot_nt(kb, k)
    if tinv is None:
        shape = (N_SCAN, CHUNK, CHUNK)
        eye = (lax.broadcasted_iota(jnp.int32, shape, 1) == lax.broadcasted_iota(jnp.int32, shape, 2)).astype(f32)
        pw = -jnp.where(strict, kk * dm, 0.0)
        tinv = eye + pw
        for _ in range(5):
            pw = _bdot_nn(pw, pw, lax.Precision.HIGH)
            tinv = tinv + _bdot_nn(tinv, pw, lax.Precision.HIGH)
    u = _bdot_nn(tinv, v * beta)
    w = _bdot_nn(tinv, kb * e)
    return dict(e=e, f=f, gl=jnp.exp(g_last), dm=dm, kb=kb, kk=kk, tinv=tinv, u=u, w=w, kd=k * f)


def _gdn_specs(nc, width, step_chunk):
    return [pl.BlockSpec((CHUNK, width), functools.partial(lambda i, d: (step_chunk(i, d), 0), d=d)) for d in (0, 1)]


def gdn_forward(q, k, v, bg, bgt, s0, with_out, name):
    t = k.shape[0]
    nc = t // CHUNK

    def body(qf_ref, qb_ref, kf_ref, kb_ref, vf_ref, vb_ref, bgf_ref, bgb_ref, bgtf_ref, bgtb_ref, s0_ref,
             of_ref, ob_ref, sallf_ref, sallb_ref, tinvf_ref, tinvb_ref, sfin_ref, s_ref):
        i = pl.program_id(0)

        @pl.when(i == 0)
        def _():
            s_ref[...] = s0_ref[...]

        masks = _order_masks()
        k8, v8 = _stack_heads(kf_ref, kb_ref), _stack_heads(vf_ref, vb_ref)
        beta, g_col, g_row = _stack_gates(bgf_ref[...], bgb_ref[...], bgtf_ref[0], bgtb_ref[0])
        c = _chunk_terms(k8, v8, beta, g_col, g_row, masks)
        s = s_ref[...]
        sallf_ref[0] = s[:GDN_HEADS]
        sallb_ref[0] = s[GDN_HEADS:]
        tinvf_ref[0] = c["tinv"][:GDN_HEADS]
        tinvb_ref[0] = c["tinv"][GDN_HEADS:]
        vn = c["u"] - _bdot_nn(c["w"], s)
        if with_out:
            q8 = _stack_heads(qf_ref, qb_ref)
            pm = jnp.where(masks[0], _bdot_nt(q8, k8) * c["dm"], 0.0)
            o = _bdot_nn(q8 * c["e"], s) + _bdot_nn(pm, vn)
        for d, o_ref in enumerate((of_ref, ob_ref)):
            for h in range(GDN_HEADS):
                o_ref[:, h * LANE:(h + 1) * LANE] = o[GDN_HEADS * d + h] if with_out else jnp.zeros((CHUNK, LANE), f32)
        s_ref[...] = c["gl"] * s + _bdot_tn(c["kd"], vn)

        @pl.when(i == nc - 1)
        def _():
            sfin_ref[...] = s_ref[...]

    chunk_of = lambda i, d: i if d == 0 else nc - 1 - i
    seq = _gdn_specs(nc, 512, chunk_of)
    gate = _gdn_specs(nc, LANE, chunk_of)
    gate_t = [pl.BlockSpec((1, 16, CHUNK), functools.partial(lambda i, d: (chunk_of(i, d), 0, 0), d=d)) for d in (0, 1)]
    sall = [pl.BlockSpec((1, GDN_HEADS, LANE, LANE), functools.partial(lambda i, d: (chunk_of(i, d), 0, 0, 0), d=d)) for d in (0, 1)]
    tinv = [pl.BlockSpec((1, GDN_HEADS, CHUNK, CHUNK), functools.partial(lambda i, d: (chunk_of(i, d), 0, 0, 0), d=d)) for d in (0, 1)]
    st = pl.BlockSpec((N_SCAN, LANE, LANE), lambda i: (0, 0, 0))
    o_shape, s_shape, t_shape = SDS((t, 512), f32), SDS((nc, GDN_HEADS, LANE, LANE), f32), SDS((nc, GDN_HEADS, CHUNK, CHUNK), f32)
    o_f, o_b, sall_f, sall_b, tinv_f, tinv_b, s_fin = pl.pallas_call(
        body, out_shape=(o_shape, o_shape, s_shape, s_shape, t_shape, t_shape, SDS((N_SCAN, LANE, LANE), f32)), grid=(nc,),
        in_specs=seq + seq + seq + gate + gate_t + [st], out_specs=tuple(seq + sall + tinv + [st]),
        scratch_shapes=[pltpu.VMEM((N_SCAN, LANE, LANE), f32)], name=name,
        compiler_params=_cparams(("arbitrary",)),
    )(q, q, k, k, v, v, bg, bg, bgt, bgt, s0.reshape(N_SCAN, LANE, LANE))
    return o_f, o_b, (sall_f, sall_b, tinv_f, tinv_b), s_fin.reshape(2, GDN_HEADS, LANE, LANE)


def _gdn_chunk_bwd(q, k, v, d_o, beta, g_col, g_row, s, tinv, dsn, masks):
    lower, strict, _, at_last = masks
    c = _chunk_terms(k, v, beta, g_col, g_row, masks, tinv)
    e, f, gl, dm, kb, kk, tinv, u, w, kd = (c[n] for n in ("e", "f", "gl", "dm", "kb", "kk", "tinv", "u", "w", "kd"))
    vn = u - _bdot_nn(w, s)
    ds = gl * dsn
    dgl = jnp.sum(jnp.sum(s * dsn, axis=2, keepdims=True), axis=1, keepdims=True)
    dkd = _bdot_nt(vn, dsn)
    dvn = _bdot_nn(kd, dsn)
    dm_grad = jnp.zeros((N_SCAN, CHUNK, CHUNK), f32)
    de = jnp.zeros((N_SCAN, CHUNK, 1), f32)
    dq = None
    dk = jnp.zeros((N_SCAN, CHUNK, LANE), f32)
    if q is not None:
        qk = _bdot_nt(q, k)
        pm = jnp.where(lower, qk * dm, 0.0)
        dqd = _bdot_nt(d_o, s)
        ds = ds + _bdot_tn(q * e, d_o)
        dpm = jnp.where(lower, _bdot_nt(d_o, vn), 0.0)
        dvn = dvn + _bdot_tn(pm, d_o)
        dqk = dpm * dm
        dm_grad = dm_grad + dpm * qk
        dq = _bdot_nn(dqk, k) + dqd * e
        dk = _bdot_tn(dqk, q)
        de = de + jnp.sum(dqd * q, axis=2, keepdims=True)
    dw = -_bdot_nt(dvn, s)
    ds = ds - _bdot_tn(w, dvn)
    drv = _bdot_tn(tinv, dvn)
    drk = _bdot_tn(tinv, dw)
    da = -jnp.where(strict, _bdot_nt(drv, u) + _bdot_nt(drk, w), 0.0)
    dbeta = jnp.sum(drv * v, axis=2, keepdims=True)
    dv = drv * beta
    dkb = drk * e
    de = de + jnp.sum(drk * kb, axis=2, keepdims=True)
    dkk = da * dm
    dm_grad = dm_grad + da * kk
    dkb = dkb + _bdot_nn(dkk, k)
    dk = dk + _bdot_tn(dkk, kb) + dkd * f
    df = jnp.sum(dkd * k, axis=2, keepdims=True)
    dbeta = dbeta + jnp.sum(dkb * k, axis=2, keepdims=True)
    dk = dk + dkb * beta
    m = dm_grad * dm
    shape = (N_SCAN, CHUNK, CHUNK)
    eye = (lax.broadcasted_iota(jnp.int32, shape, 1) == lax.broadcasted_iota(jnp.int32, shape, 2)).astype(f32)

    def as_col(row):
        return jnp.sum(eye * row, axis=2, keepdims=True)

    rsum = jnp.sum(m, axis=2, keepdims=True)
    csum = as_col(jnp.sum(m, axis=1, keepdims=True))
    dgl_tot = jnp.sum(df * f, axis=1, keepdims=True) + dgl * gl
    dgc = de * e - df * f + rsum - csum + at_last * dgl_tot
    dg = as_col(jnp.sum(lower.astype(f32) * dgc, axis=1, keepdims=True))
    return dq, dk, dv, dbeta, dg, ds


def gdn_backward(q, k, v, bg, bgt, saved, d_o, ds_fin, with_out, name):
    t = k.shape[0]
    nc = t // CHUNK

    def body(qf_ref, qb_ref, kf_ref, kb_ref, vf_ref, vb_ref, bgf_ref, bgb_ref, bgtf_ref, bgtb_ref,
             sallf_ref, sallb_ref, tinvf_ref, tinvb_ref, dof_ref, dob_ref, dsf_ref,
             dqf_ref, dqb_ref, dkf_ref, dkb_ref, dvf_ref, dvb_ref, dbgf_ref, dbgb_ref, ds0_ref, ds_ref):
        i = pl.program_id(0)

        @pl.when(i == 0)
        def _():
            ds_ref[...] = dsf_ref[...]

        lane = lax.broadcasted_iota(jnp.int32, (1, LANE), 1)
        masks = _order_masks()
        beta, g_col, g_row = _stack_gates(bgf_ref[...], bgb_ref[...], bgtf_ref[0], bgtb_ref[0])
        s = jnp.concatenate([sallf_ref[0], sallb_ref[0]], 0)
        tinv = jnp.concatenate([tinvf_ref[0], tinvb_ref[0]], 0)
        dq, dk, dv, dbeta, dg, ds = _gdn_chunk_bwd(
            _stack_heads(qf_ref, qb_ref) if with_out else None, _stack_heads(kf_ref, kb_ref), _stack_heads(vf_ref, vb_ref),
            _stack_heads(dof_ref, dob_ref), beta, g_col, g_row, s, tinv, ds_ref[...], masks)
        ds_ref[...] = ds
        for d, (dq_ref, dk_ref, dv_ref, dbg_ref) in enumerate(((dqf_ref, dkf_ref, dvf_ref, dbgf_ref), (dqb_ref, dkb_ref, dvb_ref, dbgb_ref))):
            dbg = jnp.zeros((CHUNK, LANE), f32)
            for h in range(GDN_HEADS):
                b = GDN_HEADS * d + h
                cs = slice(h * LANE, (h + 1) * LANE)
                dq_ref[:, cs] = dq[b] if with_out else jnp.zeros((CHUNK, LANE), f32)
                dk_ref[:, cs] = dk[b]
                dv_ref[:, cs] = dv[b]
                dbg = dbg + dbeta[b] * (lane == b).astype(f32) + dg[b] * (lane == 8 + b).astype(f32)
            dbg_ref[...] = dbg

        @pl.when(i == nc - 1)
        def _():
            ds0_ref[...] = ds_ref[...]

    chunk_of = lambda i, d: nc - 1 - i if d == 0 else i
    seq = _gdn_specs(nc, 512, chunk_of)
    gate = _gdn_specs(nc, LANE, chunk_of)
    gate_t = [pl.BlockSpec((1, 16, CHUNK), functools.partial(lambda i, d: (chunk_of(i, d), 0, 0), d=d)) for d in (0, 1)]
    sall = [pl.BlockSpec((1, GDN_HEADS, LANE, LANE), functools.partial(lambda i, d: (chunk_of(i, d), 0, 0, 0), d=d)) for d in (0, 1)]
    tinv = [pl.BlockSpec((1, GDN_HEADS, CHUNK, CHUNK), functools.partial(lambda i, d: (chunk_of(i, d), 0, 0, 0), d=d)) for d in (0, 1)]
    st = pl.BlockSpec((N_SCAN, LANE, LANE), lambda i: (0, 0, 0))
    o_shape, g_shape = SDS((t, 512), f32), SDS((t, LANE), f32)
    res = pl.pallas_call(
        body, out_shape=(o_shape,) * 6 + (g_shape, g_shape, SDS((N_SCAN, LANE, LANE), f32)), grid=(nc,),
        in_specs=seq + seq + seq + gate + gate_t + sall + tinv + seq + [st], out_specs=tuple(seq + seq + seq + gate + [st]),
        scratch_shapes=[pltpu.VMEM((N_SCAN, LANE, LANE), f32)], name=name,
        compiler_params=_cparams(("arbitrary",)),
    )(q, q, k, k, v, v, bg, bg, bgt, bgt, *saved, d_o, d_o, ds_fin.reshape(N_SCAN, LANE, LANE))
    return tuple(res[:8]) + (res[8].reshape(2, GDN_HEADS, LANE, LANE),)


def _my_position():
    x, y, c = lax.axis_index("x"), lax.axis_index("y"), lax.axis_index("c")
    return x, y, c, 4 * x + 2 * y + c


def exchange(arrays, scatter, name):
    n = len(arrays)
    shapes = [a.shape[1:] if scatter else a.shape for a in arrays]

    def body(*refs):
        ins, outs, token = refs[:n], refs[n:2 * n], refs[2 * n]
        send_sems, recv_sems, local_sems = refs[2 * n + 1:]
        x, y, c, me = _my_position()
        token[...] = jnp.zeros_like(token)
        started = []
        for a in range(n):
            mine = pltpu.make_async_copy(ins[a].at[me] if scatter else ins[a], outs[a].at[me], local_sems.at[a])
            mine.start()
            started.append(mine)
        waits = []
        for r in range(1, N_DEV):
            px = 1 - x if r & 4 else x
            py = 1 - y if r & 2 else y
            pc = 1 - c if r & 1 else c
            pid = 4 * px + 2 * py + pc
            for a in range(n):
                cp = pltpu.make_async_remote_copy(
                    src_ref=ins[a].at[pid] if scatter else ins[a], dst_ref=outs[a].at[me],
                    send_sem=send_sems.at[a, r - 1], recv_sem=recv_sems.at[a, r - 1],
                    device_id=(px, py, pc), device_id_type=pl.DeviceIdType.MESH)
                cp.start()
                arrive = pltpu.make_async_remote_copy(
                    src_ref=ins[a].at[pid] if scatter else ins[a], dst_ref=outs[a].at[pid],
                    send_sem=send_sems.at[a, r - 1], recv_sem=recv_sems.at[a, r - 1],
                    device_id=(px, py, pc), device_id_type=pl.DeviceIdType.MESH)
                waits.append((cp, arrive))
        for cp, arrive in waits:
            cp.wait_send()
            arrive.wait_recv()
        for mine in started:
            mine.wait()

    any_spec = pl.BlockSpec(memory_space=pl.ANY)
    return pl.pallas_call(
        body, out_shape=tuple(SDS((N_DEV,) + tuple(s), a.dtype) for s, a in zip(shapes, arrays)) + (SDS((8, LANE), f32),),
        in_specs=[any_spec] * n, out_specs=tuple([any_spec] * n) + (pl.BlockSpec(memory_space=pltpu.VMEM),),
        scratch_shapes=[pltpu.SemaphoreType.DMA((n, N_DEV - 1)), pltpu.SemaphoreType.DMA((n, N_DEV - 1)),
                        pltpu.SemaphoreType.DMA((n,))],
        name=name,
    )(*arrays)


_HBM_SPEC = pl.BlockSpec(memory_space=pltpu.HBM)
_SEM_SPEC = pl.BlockSpec(memory_space=pltpu.SEMAPHORE)
_DATAFLOW = pltpu.SideEffectType.DATAFLOW_SIDE_EFFECTING


def _peers(x, y, c):
    out = []
    for r in range(1, N_DEV):
        px = 1 - x if r & 4 else x
        py = 1 - y if r & 2 else y
        pc = 1 - c if r & 1 else c
        out.append((r, (px, py, pc), 4 * px + 2 * py + pc))
    return out


def _exchange_copies(ins, lands, send_sems, recv_sems, scatter, arrivals):
    x, y, c, me = _my_position()
    pairs = []
    for r, peer, pid in _peers(x, y, c):
        for a in range(len(ins)):
            k = a * (N_DEV - 1) + r - 1
            kw = dict(send_sem=send_sems.at[k], recv_sem=recv_sems.at[k], device_id=peer, device_id_type=pl.DeviceIdType.MESH)
            src = ins[a].at[pid] if scatter else ins[a]
            send = pltpu.make_async_remote_copy(src_ref=src, dst_ref=lands[a].at[me], **kw)
            arrive = pltpu.make_async_remote_copy(src_ref=src, dst_ref=lands[a].at[pid], **kw) if arrivals else None
            pairs.append((send, arrive))
    return pairs


def exchange_start(arrays, scatter, name):
    n = len(arrays)
    shapes = [a.shape[1:] if scatter else a.shape for a in arrays]

    def body(*refs):
        ins, lands = refs[:n], refs[n:2 * n]
        send_sems, recv_sems = refs[2 * n], refs[2 * n + 1]
        token = refs[-1]
        for send, _ in _exchange_copies(ins, lands, send_sems, recv_sems, scatter, False):
            send.start()
        token[...] = jnp.zeros_like(token)

    sem = pltpu.SemaphoreType.DMA((n * (N_DEV - 1),))
    land_shapes = [(N_DEV,) + tuple(s) for s in shapes]
    res = pl.pallas_call(
        body, name=name,
        out_shape=(sem, sem, *[pltpu.HBM(a.shape, a.dtype) for a in arrays],
                   *[pltpu.HBM(s, a.dtype) for s, a in zip(land_shapes, arrays)], SDS((8, LANE), f32)),
        in_specs=[_HBM_SPEC] * (2 * n),
        out_specs=(_SEM_SPEC, _SEM_SPEC, *[_HBM_SPEC] * (2 * n), pl.BlockSpec(memory_space=pltpu.VMEM)),
        input_output_aliases={i: 2 + i for i in range(2 * n)},
        compiler_params=pltpu.CompilerParams(has_side_effects=_DATAFLOW),
    )(*[pltpu.with_memory_space_constraint(a, pltpu.HBM) for a in arrays],
      *[pltpu.with_memory_space_constraint(lax.empty(s, a.dtype), pltpu.HBM) for s, a in zip(land_shapes, arrays)])
    return (res[0], res[1], list(res[2:2 + n]), list(res[2 + n:2 + 2 * n]), scatter), res[-1]


def exchange_wait(handle, after, name):
    send_sems, recv_sems, ins, lands, scatter = handle
    n = len(ins)

    def body(*refs):
        in_refs, land_refs = refs[:n], refs[n:2 * n]
        for send, arrive in _exchange_copies(in_refs, land_refs, refs[2 * n], refs[2 * n + 1], scatter, True):
            send.wait_send()
            arrive.wait_recv()
        refs[-1][...] = jnp.zeros_like(refs[-1])

    res = pl.pallas_call(
        body, name=name,
        out_shape=tuple(pltpu.HBM(a.shape, a.dtype) for a in ins + lands) + (SDS((8, LANE), f32),),
        in_specs=[_HBM_SPEC] * (2 * n) + [_SEM_SPEC, _SEM_SPEC, pl.BlockSpec(memory_space=pl.ANY)],
        out_specs=tuple([_HBM_SPEC] * (2 * n)) + (pl.BlockSpec(memory_space=pltpu.VMEM),),
        input_output_aliases={i: i for i in range(2 * n)},
        compiler_params=pltpu.CompilerParams(has_side_effects=_DATAFLOW),
    )(*ins, *lands, send_sems, recv_sems, after)
    return list(res[:n]), list(res[n:2 * n]), res[-1]


def place_own(lands, arrays, scatter, me):
    own = [lax.dynamic_index_in_dim(a, me, 0, keepdims=False) if scatter else a for a in arrays]
    return [lax.dynamic_update_index_in_dim(l, o, me, 0) for l, o in zip(lands, own)]


def ada_forward(a_raw, ada_w, ada_b_loc, name):
    def body(a_ref, w_ref, b_ref, o_ref):
        a = _silu(a_ref[...])
        for l in range(DEPTH):
            o_ref[l] = _dotf(a, w_ref[l]) + b_ref[l]

    return pl.pallas_call(body, out_shape=SDS((DEPTH, 16, ada_w.shape[2]), f32), name=name,
                          compiler_params=_cparams())(a_raw, ada_w, ada_b_loc)


def ada_backward(a_raw, ada_w, dm, name):
    def body(a_ref, w_ref, dm_ref, gw_ref, dcc_ref):
        a = _silu(a_ref[...])
        for l in range(DEPTH):
            gw_ref[l] = _dotf(a, dm_ref[l], (((0,), (0,)), ((), ())))
        dcc_ref[...] = _dotf(dm_ref[0, 8:16, :], w_ref[0], (((1,), (1,)), ((), ())))

    return pl.pallas_call(body, out_shape=(SDS(ada_w.shape, f32), SDS((8, ada_w.shape[1]), f32)), name=name,
                          compiler_params=_cparams())(a_raw, ada_w, dm)


def sum_parts(parts, name):
    _, r, c = parts.shape

    def body(p_ref, o_ref):
        acc = p_ref[0]
        for i in range(1, N_DEV):
            acc = acc + p_ref[i]
        o_ref[...] = acc

    return pl.pallas_call(body, out_shape=SDS((r, c), f32), name=name, compiler_params=_cparams())(parts)


def cctx_grad(parts, c_ctx, name):
    def body(p_ref, c_ref, o_ref):
        acc = p_ref[0, 0:1, :]
        for i in range(1, N_DEV):
            acc = acc + p_ref[i, 0:1, :]
        o_ref[...] = acc * _dsilu(c_ref[...])

    return pl.pallas_call(body, out_shape=SDS((1, c_ctx.shape[1]), f32), name=name, compiler_params=_cparams())(parts, c_ctx)


def _adamw_math(g, w, m, v):
    m = ADAM_B1 * m + (1.0 - ADAM_B1) * g
    v = ADAM_B2 * v + (1.0 - ADAM_B2) * (g * g)
    m_hat = m / (1.0 - ADAM_B1 ** ADAM_STEP)
    v_hat = v / (1.0 - ADAM_B2 ** ADAM_STEP)
    delta = -ADAM_LR * (m_hat / (jnp.sqrt(v_hat) + ADAM_EPS) + ADAM_WD * w)
    return delta, m, v


def adamw(parts, w, m, v, name):
    n, r, c = parts.shape
    tr = _pick(r, (256, 128, 64, 32, 16, 8))

    def body(p_ref, w_ref, m_ref, v_ref, g_ref, d_ref, nm_ref, nv_ref):
        g = p_ref[0].astype(f32)
        for i in range(1, n):
            g = g + p_ref[i].astype(f32)
        g_ref[...] = g
        d_ref[...], nm_ref[...], nv_ref[...] = _adamw_math(g, w_ref[...], m_ref[...], v_ref[...])

    blk = pl.BlockSpec((tr, c), lambda i: (i, 0))
    out = SDS((r, c), f32)
    return pl.pallas_call(
        body, out_shape=(out, out, out, out), grid=(r // tr,),
        in_specs=[pl.BlockSpec((n, tr, c), lambda i: (0, i, 0)), blk, blk, blk], out_specs=(blk, blk, blk, blk),
        name=name, compiler_params=_cparams(("parallel",)),
    )(parts, w, m, v)


def adamw_layers(parts, w, m, v, name):
    nl = len(parts)
    n, r, c = parts[0].shape
    tr = _pick(r, (256, 128, 64, 32, 16, 8))

    def body(*refs):
        p_refs = refs[:nl]
        w_ref, m_ref, v_ref, g_ref, d_ref, nm_ref, nv_ref = refs[nl:]
        for k in range(nl):
            @pl.when(pl.program_id(0) == k)
            def _(p_ref=p_refs[k]):
                g = p_ref[0].astype(f32)
                for i in range(1, n):
                    g = g + p_ref[i].astype(f32)
                g_ref[0] = g
                d_ref[0], nm_ref[0], nv_ref[0] = _adamw_math(g, w_ref[0], m_ref[0], v_ref[0])

    p_specs = [pl.BlockSpec((n, tr, c), functools.partial(lambda l, i, k: (0, jnp.where(l == k, i, 0), 0), k=k)) for k in range(nl)]
    blk = pl.BlockSpec((1, tr, c), lambda l, i: (l, i, 0))
    out = SDS((nl, r, c), f32)
    return pl.pallas_call(
        body, out_shape=(out, out, out, out), grid=(nl, r // tr),
        in_specs=p_specs + [blk, blk, blk], out_specs=(blk, blk, blk, blk),
        name=name, compiler_params=_cparams(("arbitrary", "arbitrary")),
    )(*parts, w, m, v)


def adamw_small(items, name):
    n = len(items)

    def body(*refs):
        ins, outs = refs[:4 * n], refs[4 * n:]
        for i in range(n):
            g, w, m, v = (ins[4 * i + j][...] for j in range(4))
            outs[3 * i][...], outs[3 * i + 1][...], outs[3 * i + 2][...] = _adamw_math(g, w, m, v)

    flat = [a for it in items for a in it]
    out_shape = tuple(SDS(it[1].shape, f32) for it in items for _ in range(3))
    res = pl.pallas_call(body, out_shape=out_shape, name=name, compiler_params=_cparams())(*flat)
    return [tuple(res[3 * i:3 * i + 3]) for i in range(n)]


def _unshard(g, axis):
    loc = g.shape[1:]
    return jnp.moveaxis(g, 0, axis).reshape(loc[:axis] + (N_DEV * loc[axis],) + loc[axis + 1:])


def _shard_major(full, axis):
    s = full.shape
    return jnp.moveaxis(full.reshape(s[:axis] + (N_DEV, s[axis] // N_DEV) + s[axis + 1:]), axis, 0)


def _my_block(full, axis, me):
    n = full.shape[axis] // N_DEV
    return lax.dynamic_slice_in_dim(full, me * n, n, axis)


def _pack(arrays):
    flat = [a.reshape(-1) for a in arrays]
    sizes = [f.shape[0] for f in flat]
    total = sum(sizes)
    padded = -(-total // (8 * LANE)) * (8 * LANE)
    flat.append(jnp.zeros((padded - total,), f32))
    offs = [sum(sizes[:i]) for i in range(len(sizes))]
    return jnp.concatenate(flat).reshape(padded // LANE, LANE), offs


def _pad_rows(w, n):
    return jnp.concatenate([w, jnp.zeros((n - w.shape[0],) + w.shape[1:], w.dtype)], 0)


def _gate_rows(bg):
    return bg[:, :16].reshape(bg.shape[0] // CHUNK, CHUNK, 16).transpose(0, 2, 1)


def _rows(vec, n):
    m = vec.reshape(n, 1, -1)
    return [m[i] for i in range(n)]


def kernel(x, c, ctx, c_ctx, ada_w, ada_b, ln_g, ln_b, even_w_in, even_w_out, gdn_conv_w, gdn_a_log, gdn_dt_bias, gdn_norm_w, pool_w, pool_scale, odd_w_in, odd_w_out, sconv_w, conf_conv_w, conf_ln_g, conf_ln_b, ffn_w_up, ffn_conv_w, ffn_w_down, loss_target, m_c_ctx, m_ada_w, m_ada_b, m_ln_g, m_ln_b, m_even_w_in, m_even_w_out, m_gdn_conv_w, m_gdn_a_log, m_gdn_dt_bias, m_gdn_norm_w, m_pool_w, m_pool_scale, m_odd_w_in, m_odd_w_out, m_sconv_w, m_conf_conv_w, m_conf_ln_g, m_conf_ln_b, m_ffn_w_up, m_ffn_conv_w, m_ffn_w_down, v_c_ctx, v_ada_w, v_ada_b, v_ln_g, v_ln_b, v_even_w_in, v_even_w_out, v_gdn_conv_w, v_gdn_a_log, v_gdn_dt_bias, v_gdn_norm_w, v_pool_w, v_pool_scale, v_odd_w_in, v_odd_w_out, v_sconv_w, v_conf_conv_w, v_conf_ln_g, v_conf_ln_b, v_ffn_w_up, v_ffn_conv_w, v_ffn_w_down):
    weights = dict(c_ctx=c_ctx, ada_w=ada_w, ada_b=ada_b, ln_g=ln_g, ln_b=ln_b, even_w_in=even_w_in, even_w_out=even_w_out, gdn_conv_w=gdn_conv_w, gdn_a_log=gdn_a_log, gdn_dt_bias=gdn_dt_bias, gdn_norm_w=gdn_norm_w, pool_w=pool_w, pool_scale=pool_scale, odd_w_in=odd_w_in, odd_w_out=odd_w_out, sconv_w=sconv_w, conf_conv_w=conf_conv_w, conf_ln_g=conf_ln_g, conf_ln_b=conf_ln_b, ffn_w_up=ffn_w_up, ffn_conv_w=ffn_conv_w, ffn_w_down=ffn_w_down)
    mom1 = dict(c_ctx=m_c_ctx, ada_w=m_ada_w, ada_b=m_ada_b, ln_g=m_ln_g, ln_b=m_ln_b, even_w_in=m_even_w_in, even_w_out=m_even_w_out, gdn_conv_w=m_gdn_conv_w, gdn_a_log=m_gdn_a_log, gdn_dt_bias=m_gdn_dt_bias, gdn_norm_w=m_gdn_norm_w, pool_w=m_pool_w, pool_scale=m_pool_scale, odd_w_in=m_odd_w_in, odd_w_out=m_odd_w_out, sconv_w=m_sconv_w, conf_conv_w=m_conf_conv_w, conf_ln_g=m_conf_ln_g, conf_ln_b=m_conf_ln_b, ffn_w_up=m_ffn_w_up, ffn_conv_w=m_ffn_conv_w, ffn_w_down=m_ffn_w_down)
    mom2 = dict(c_ctx=v_c_ctx, ada_w=v_ada_w, ada_b=v_ada_b, ln_g=v_ln_g, ln_b=v_ln_b, even_w_in=v_even_w_in, even_w_out=v_even_w_out, gdn_conv_w=v_gdn_conv_w, gdn_a_log=v_gdn_a_log, gdn_dt_bias=v_gdn_dt_bias, gdn_norm_w=v_gdn_norm_w, pool_w=v_pool_w, pool_scale=v_pool_scale, odd_w_in=v_odd_w_in, odd_w_out=v_odd_w_out, sconv_w=v_sconv_w, conf_conv_w=v_conf_conv_w, conf_ln_g=v_conf_ln_g, conf_ln_b=v_conf_ln_b, ffn_w_up=v_ffn_w_up, ffn_conv_w=v_ffn_conv_w, ffn_w_down=v_ffn_w_down)
    order = list(weights)
    me = 4 * lax.axis_index("x") + 2 * lax.axis_index("y") + lax.axis_index("c")
    x, ctx, target = x[0], ctx[0], loss_target[0]
    t, d = x.shape
    tc = ctx.shape[0]

    small_in = [ln_g, ln_b, gdn_conv_w, sconv_w, conf_conv_w, ffn_conv_w, c]
    small_axes = [2, 2, 1, 1, 1, 3, 0]
    small_pack, small_offs = _pack(small_in)
    sm = exchange([small_pack], False, "gather_first")[0].reshape(N_DEV, -1)
    e_in = even_w_in.shape[1] * N_DEV
    e_pad = -(-e_in // LANE) * LANE
    lng_f, lnb_f, gconv_f, sconv_f, cconv_f, fconv_f, c_all = [
        _unshard(sm[:, o:o + a.size].reshape((N_DEV,) + a.shape), ax) for a, o, ax in zip(small_in, small_offs, small_axes)]
    gw8 = _pad_rows(gconv_f, 8)
    sw8 = _pad_rows(sconv_f, 8)
    cw32 = _pad_rows(cconv_f, 32)
    fw16 = [_pad_rows(fconv_f[l].reshape(9, D_FF), 16) for l in range(DEPTH)]

    a_raw = jnp.concatenate([c_all, c_ctx[None], jnp.zeros((7, d), f32)], 0)
    ncol = ada_w.shape[2]
    ada_b_loc = lax.dynamic_slice_in_dim(ada_b, me * ncol, ncol, 1)[:, None, :]
    modpart = ada_forward(a_raw, ada_w, ada_b_loc, "ada_forward")
    mod_send = jnp.stack([jnp.transpose(modpart[:, :N_DEV], (1, 0, 2)),
                          jnp.broadcast_to(modpart[:, N_DEV][None], (N_DEV, DEPTH, ncol))], axis=2)
    mod_recv, token = exchange([mod_send], True, "scatter_mod")
    gather_e, token = exchange_start([even_w_in.astype(bf16) + token[0, 0].astype(bf16)], False, "gather_even_in_start")
    wire_l0 = [even_w_out.astype(bf16) + token[0, 0].astype(bf16), ffn_w_up[0].astype(bf16), ffn_w_down[0].astype(bf16)]
    gather_l0, token = exchange_start(wire_l0, False, "gather_l0_start")
    wire_l1 = [odd_w_in.astype(bf16) + token[0, 0].astype(bf16), odd_w_out.astype(bf16), ffn_w_up[1].astype(bf16),
               ffn_w_down[1].astype(bf16)]
    gather_l1, token = exchange_start(wire_l1, False, "gather_l1_start")
    mod_recv = mod_recv + token[0, 0]
    mod = jnp.transpose(mod_recv[:, :, 0, :], (1, 0, 2)).reshape(DEPTH, 6 * d)
    modc = mod_recv[:, 0, 1, :].reshape(6 * d)
    sh_c, sc_c = modc[None, :d], modc[None, d:2 * d]
    mods = [_rows(mod[l], 6) for l in range(DEPTH)]
    lng = [[lng_f[l, j][None] for j in range(2)] for l in range(DEPTH)]
    lnb = [[lnb_f[l, j][None] for j in range(2)] for l in range(DEPTH)]

    neg_a = jnp.zeros((1, LANE), f32).at[0, 8:16].set(-jnp.exp(gdn_a_log).reshape(8))
    dt_row = jnp.zeros((1, LANE), f32).at[0, 8:16].set(gdn_dt_bias.reshape(8))
    nw_row, ps_row = gdn_norm_w[None], pool_scale[None]
    cg_row, cb_row = conf_ln_g[None], conf_ln_b[None]
    q_scale = GDN_DK ** -0.5

    sh_m, sc_m, gt_m, sh_f, sc_f, gt_f = mods[0]
    u0 = modulate(x, sc_m, sh_m, "mod_l0_mix")
    cu = modulate(ctx, sc_c, sh_c, "mod_ctx")
    sent, landed, _ = exchange_wait(gather_e, u0, "gather_even_in_wait")
    win_e = jnp.pad(_unshard(place_own(landed, sent, False, me)[0], 1), ((0, 0), (0, e_pad - e_in)))
    p0 = matmul(u0, win_e, "nn", f32, "even_in")
    pc = matmul(cu, win_e, "nn", f32, "even_in_ctx")
    qn = gdn_conv(p0, gw8, 0, 4, q_scale, "gdn_conv_q")
    kn = gdn_conv(p0, gw8, 4, 4, 1.0, "gdn_conv_k")
    vv = gdn_conv(p0, gw8, 8, 4, None, "gdn_conv_v")
    kc = gdn_conv(pc, gw8, 4, 4, 1.0, "gdn_conv_k_ctx")
    vc = gdn_conv(pc, gw8, 8, 4, None, "gdn_conv_v_ctx")
    bg = gdn_gates(p0, neg_a, dt_row, "gdn_gates")
    bgc = gdn_gates(pc, neg_a, dt_row, "gdn_gates_ctx")
    bgt, bgtc = _gate_rows(bg), _gate_rows(bgc)
    zero_state = jnp.zeros((2, GDN_HEADS, LANE, LANE), f32)
    _, _, saved_c, sfin_c = gdn_forward(kc, kc, vc, bgc, bgtc, zero_state, False, "gdn_fwd_ctx")
    o_f, o_b, saved, _ = gdn_forward(qn, kn, vv, bg, bgt, sfin_c, True, "gdn_fwd")
    mix0 = jnp.concatenate([gated_rmsnorm(o_f, o_b, p0, nw_row, "gated_rmsnorm"),
                            pool_mix(p0, pool_w, ps_row, "pool_mix")], 1)
    sent, landed, _ = exchange_wait(gather_l0, mix0, "gather_l0_wait")
    full = place_own(landed, sent, False, me)
    wout_e, wup, wdown = _unshard(full[0], 0), [_unshard(full[1], 1)], [_unshard(full[2], 0)]
    y0 = matmul(mix0, wout_e, "nn", f32, "even_out")
    x1, u1 = res_layernorm(x, y0, gt_m, lng[0][0], lnb[0][0], "resln_l0_mix", sc_f, sh_f)
    h0 = matmul(u1, wup[0], "nn", f32, "ffn_up_l0")
    f0 = ffn_conv(h0, fw16[0], "ffn_conv_l0")
    y0f = matmul(f0, wdown[0], "nn", f32, "ffn_down_l0")
    sh_m1, sc_m1, gt_m1, sh_f1, sc_f1, gt_f1 = mods[1]
    x2, u2 = res_layernorm(x1, y0f, gt_f, lng[0][1], lnb[0][1], "resln_l0_ffn", sc_m1, sh_m1)

    sent, landed, _ = exchange_wait(gather_l1, x2, "gather_l1_wait")
    full = place_own(landed, sent, False, me)
    win_o, wout_o = _unshard(full[0], 1), _unshard(full[1], 0)
    wup.append(_unshard(full[2], 1))
    wdown.append(_unshard(full[3], 0))
    p1 = matmul(u2, win_o, "nn", f32, "odd_in")
    zc = conf_conv(p1, cw32, "conf_conv")
    mix1 = jnp.concatenate([short_conv(p1, sw8, "short_conv"), ln_silu(zc, cg_row, cb_row, "conf_ln_silu")], 1)
    y1 = matmul(mix1, wout_o, "nn", f32, "odd_out")
    x3, u3 = res_layernorm(x2, y1, gt_m1, lng[1][0], lnb[1][0], "resln_l1_mix", sc_f1, sh_f1)
    h1 = matmul(u3, wup[1], "nn", f32, "ffn_up_l1")
    f1 = ffn_conv(h1, fw16[1], "ffn_conv_l1")
    y1f = matmul(f1, wdown[1], "nn", f32, "ffn_down_l1")
    loss_row, dxr, dy, dgt_f1, dlg, dlb = res_layernorm_loss(x3, y1f, gt_f1, lng[1][1], lnb[1][1], target, "resln_l1_ffn_loss")

    def ffn_backward(dy, u, h, f, l):
        df = matmul(dy, wdown[l], "nt", f32, f"ffn_down_dgrad_l{l}")
        g_down = matmul(f, dy, "tn", bf16, f"ffn_down_wgrad_l{l}")
        dh, dcw = ffn_conv_bwd(h, fw16[l], df, f"ffn_conv_bwd_l{l}")
        du = matmul(dh, wup[l], "nt", f32, f"ffn_up_dgrad_l{l}")
        g_up = matmul(u, dh, "tn", bf16, f"ffn_up_wgrad_l{l}")
        return du, dcw, g_up, g_down

    dln_f1 = (dlg, dlb)
    du, dfcw1, g_up1, g_down1 = ffn_backward(dy, u3, h1, f1, 1)

    scatter_a, token = exchange_start([_shard_major(g_up1, 1), _shard_major(g_down1, 0)], True, "scatter_l1_ffn_start")
    gt_m1 = gt_m1 + token[0:1, 0:1]

    dxr, dy, dsc, dsh, dgt, dlg, dlb = modulate_res_layernorm_bwd(
        du, sc_f1, dxr, x2, y1, gt_m1, lng[1][0], lnb[1][0], "mod_resln_bwd_l1_mix")
    dmod_f1 = (dsh, dsc, dgt_f1)
    dln_m1 = (dlg, dlb)
    dmix = matmul(dy, wout_o, "nt", f32, "odd_out_dgrad")
    g_wout_o = matmul(mix1, dy, "tn", bf16, "odd_out_wgrad")
    dgb, dgc, dhh, d_sconv = short_conv_bwd(p1, sw8, dmix, "short_conv_bwd")
    dzc, d_cg, d_cb = ln_silu_bwd(zc, cg_row, cb_row, dmix, "conf_ln_silu_bwd")
    dga, dgbb, d_cconv = conf_conv_bwd(p1, cw32, dzc, "conf_conv_bwd")
    dp1 = jnp.concatenate([dgb, dgc, dhh, dga, dgbb], 1)
    du = matmul(dp1, win_o, "nt", f32, "odd_in_dgrad")
    g_win_o = matmul(u2, dp1, "tn", bf16, "odd_in_wgrad")
    dgt_m1 = dgt
    dxr, dy, dsc, dsh, dgt_f0, dlg, dlb = modulate_res_layernorm_bwd(
        du, sc_m1, dxr, x1, y0f, gt_f, lng[0][1], lnb[0][1], "mod_resln_bwd_l0_ffn")
    dmod_m1 = (dsh, dsc, dgt_m1)
    dln_f0 = (dlg, dlb)
    du, dfcw0, g_up0, g_down0 = ffn_backward(dy, u1, h0, f0, 0)

    dxr, dy, dsc, dsh, dgt, dlg, dlb = modulate_res_layernorm_bwd(
        du, sc_f, dxr, x, y0, gt_m, lng[0][0], lnb[0][0], "mod_resln_bwd_l0_mix")
    dmod_f0 = (dsh, dsc, dgt_f0)
    dln_m0 = (dlg, dlb)
    dmix = matmul(dy, wout_e, "nt", f32, "even_out_dgrad")
    g_wout_e = matmul(mix0, dy, "tn", bf16, "even_out_wgrad")
    scatter_b, token = exchange_start(
        [_shard_major(g_win_o, 1), _shard_major(g_wout_o, 0), _shard_major(g_up0, 1), _shard_major(g_down0, 0),
         _shard_major(g_wout_e, 0)], True, "scatter_mid_start")
    d_o, dgate, d_nw = gated_rmsnorm_bwd(o_f, o_b, p0, nw_row + token[0:1, 0:1], dmix, "gated_rmsnorm_bwd")
    dpool, d_pw, d_ps = pool_mix_bwd(p0, pool_w, ps_row, dmix, "pool_mix_bwd")
    small_early = [d_nw, d_pw, d_ps, d_sconv[:3], d_cconv[:31], d_cg, d_cb, jnp.stack([dfcw0[:9], dfcw1[:9]]),
                   loss_row[:, :1]]
    epack, eoffs = _pack(small_early)
    gather_early, token = exchange_start([epack], False, "gather_small_early_start")
    dq_f, dq_b, dk_f, dk_b, dv_f, dv_b, dbg_f, dbg_b, ds0 = gdn_backward(
        qn, kn, vv, bg, bgt, saved, d_o, zero_state + token[0, 0], True, "gdn_bwd")
    _, _, dkc_f, dkc_b, dvc_f, dvc_b, dbgc_f, dbgc_b, _ = gdn_backward(
        kc, kc, vc, bgc, bgtc, saved_c, jnp.zeros((tc, 512), f32), ds0, False, "gdn_bwd_ctx")
    dqp, dwq = gdn_conv_bwd(p0, gw8, dq_f, dq_b, 0, 4, q_scale, "gdn_conv_q_bwd")
    dkp, dwk = gdn_conv_bwd(p0, gw8, dk_f, dk_b, 4, 4, 1.0, "gdn_conv_k_bwd")
    dvp, dwv = gdn_conv_bwd(p0, gw8, dv_f, dv_b, 8, 4, None, "gdn_conv_v_bwd")
    dkcp, dwkc = gdn_conv_bwd(pc, gw8, dkc_f, dkc_b, 4, 4, 1.0, "gdn_conv_k_ctx_bwd")
    dvcp, dwvc = gdn_conv_bwd(pc, gw8, dvc_f, dvc_b, 8, 4, None, "gdn_conv_v_ctx_bwd")
    ds_l, da_l, ddt_l = gdn_gates_bwd(p0, neg_a, dt_row, dbg_f, dbg_b, "gdn_gates_bwd")
    ds_c, da_c, ddt_c = gdn_gates_bwd(pc, neg_a, dt_row, dbgc_f, dbgc_b, "gdn_gates_ctx_bwd")
    zc512 = jnp.zeros((tc, 512), bf16)
    dp0 = jnp.concatenate([dqp, dkp, dvp, dgate, dpool, ds_l], 1)
    dpc = jnp.concatenate([zc512, dkcp, dvcp, zc512, zc512, ds_c], 1)
    du0 = matmul(dp0, win_e, "nt", f32, "even_in_dgrad")
    duc = matmul(dpc, win_e, "nt", f32, "even_in_ctx_dgrad")
    g_win_e = matmul(u0, dp0, "tn", bf16, "even_in_wgrad", init=matmul(cu, dpc, "tn", f32, "even_in_ctx_wgrad"))[:, :e_in]
    scatter_c, token = exchange_start([_shard_major(g_win_e, 1)], True, "scatter_last_start")
    grad_x, dsc, dsh = modulate_bwd(du0, x, sc_m + token[0:1, 0:1], dxr, "mod_bwd_l0_mix")
    dmod_m0 = (dsh, dsc, dgt)
    _, dsc_c, dsh_c = modulate_bwd(duc, ctx, sc_c, jnp.zeros((tc, d), f32), "mod_bwd_ctx")

    grads, delta, new_m, new_v = {}, {}, {}, {}

    def update(n, parts, w, m, v):
        cols = w.shape[-1]
        out = adamw(parts.reshape(parts.shape[0], -1, cols), w.reshape(-1, cols), m.reshape(-1, cols), v.reshape(-1, cols), f"adamw_{n}")
        return [a.reshape(w.shape) for a in out]

    sent, landed, _ = exchange_wait(scatter_a, grad_x, "scatter_l1_ffn_wait")
    recv_a = place_own(landed, sent, True, me)
    sent, landed, _ = exchange_wait(scatter_b, grad_x, "scatter_mid_wait")
    recv_b = place_own(landed, sent, True, me)
    for n, parts in (("odd_w_in", recv_b[0]), ("odd_w_out", recv_b[1])):
        grads[n], delta[n], new_m[n], new_v[n] = update(n, parts, weights[n], mom1[n], mom2[n])
    for n, per_layer in (("ffn_w_up", (recv_b[2], recv_a[0])), ("ffn_w_down", (recv_b[3], recv_a[1]))):
        grads[n], delta[n], new_m[n], new_v[n] = adamw_layers(list(per_layer), weights[n], mom1[n], mom2[n], f"adamw_{n}")
    sent, landed, token = exchange_wait(scatter_c, new_v["ffn_w_down"], "scatter_last_wait")
    recv_c = place_own(landed, sent, True, me)
    for n, parts in (("even_w_in", recv_c[0]), ("even_w_out", recv_b[4])):
        grads[n], delta[n], new_m[n], new_v[n] = update(n, parts, weights[n], mom1[n], mom2[n])

    dmod0 = jnp.concatenate(dmod_m0 + dmod_f0, 1)
    dmod1 = jnp.concatenate(dmod_m1 + dmod_f1, 1)
    dmodc = jnp.concatenate([dsh_c, dsc_c], 1)
    d_gconv = jnp.concatenate([dwq, dwk + dwkc, dwv + dwvc], 1)[:5]
    small_late = [dmod0, dmod1, dmodc,
                  jnp.concatenate([dln_m0[0], dln_f0[0], dln_m1[0], dln_f1[0]], 0),
                  jnp.concatenate([dln_m0[1], dln_f0[1], dln_m1[1], dln_f1[1]], 0),
                  d_gconv, (da_l + da_c)[0, 8:16], (ddt_l + ddt_c)[0, 8:16]]
    gpack, goffs = _pack(small_late)
    gparts = exchange([gpack + token[0:1]], False, "gather_small_grads")[0]
    sent, landed, _ = exchange_wait(gather_early, gparts, "gather_small_early_wait")
    eparts = place_own(landed, sent, False, me)[0]
    gsum = sum_parts(gparts, "sum_small_grads").reshape(-1)
    esum = sum_parts(eparts, "sum_small_early").reshape(-1)
    gs = ([gsum[o:o + a.size].reshape(a.shape) for a, o in zip(small_late, goffs)]
          + [esum[o:o + a.size].reshape(a.shape) for a, o in zip(small_early, eoffs)])
    loss = gs[16].reshape(())
    gflat = gparts.reshape(N_DEV, -1)
    dmodc_cols = _my_block(jnp.pad(gs[2], ((0, 0), (0, 4 * d))), 1, me)
    dm = jnp.stack([
        jnp.concatenate([_my_block(gflat[:, goffs[0]:goffs[0] + 6 * d], 1, me), dmodc_cols, jnp.zeros((7, ncol), f32)], 0),
        jnp.concatenate([_my_block(gflat[:, goffs[1]:goffs[1] + 6 * d], 1, me), jnp.zeros((8, ncol), f32)], 0)])
    g_ada_w, dcc = ada_backward(a_raw, ada_w, dm, "ada_backward")
    g_cctx = cctx_grad(exchange([dcc], False, "gather_cctx")[0], c_ctx[None], "cctx_grad")

    grads["c_ctx"] = g_cctx.reshape(c_ctx.shape)
    grads["ada_b"] = jnp.concatenate([gs[0] + jnp.pad(gs[2], ((0, 0), (0, 4 * d))), gs[1]], 0)
    grads["ln_g"] = _my_block(gs[3].reshape(DEPTH, 2, d), 2, me)
    grads["ln_b"] = _my_block(gs[4].reshape(DEPTH, 2, d), 2, me)
    grads["gdn_conv_w"] = _my_block(gs[5], 1, me)
    grads["gdn_a_log"] = gs[6].reshape(2, GDN_HEADS)
    grads["gdn_dt_bias"] = gs[7].reshape(2, GDN_HEADS)
    grads["gdn_norm_w"] = gs[8].reshape(LANE)
    grads["pool_w"] = gs[9]
    grads["pool_scale"] = gs[10].reshape(-1)
    grads["sconv_w"] = _my_block(gs[11], 1, me)
    grads["conf_conv_w"] = _my_block(gs[12], 1, me)
    grads["conf_ln_g"] = gs[13].reshape(-1)
    grads["conf_ln_b"] = gs[14].reshape(-1)
    grads["ffn_conv_w"] = _my_block(gs[15].reshape(DEPTH, 3, 3, D_FF), 3, me)

    def as2d(a):
        return a.reshape(-1, a.shape[-1]) if a.ndim > 1 else a.reshape(1, -1)

    small_names = [n for n in order if n in grads and n not in delta]
    res = adamw_small([(as2d(grads[n]), as2d(weights[n]), as2d(mom1[n]), as2d(mom2[n])) for n in small_names], "adamw_small")
    for n, (dl, nm, nv) in zip(small_names, res):
        delta[n], new_m[n], new_v[n] = (a.reshape(weights[n].shape) for a in (dl, nm, nv))
    grads["ada_w"], delta["ada_w"], new_m["ada_w"], new_v["ada_w"] = update("ada_w", g_ada_w[None], ada_w, m_ada_w, v_ada_w)

    return (loss, grad_x[None], *[grads[n] for n in order], *[delta[n] for n in order],
            *[new_m[n] for n in order], *[new_v[n] for n in order])
```
